```python
import math
import jax
import jax.numpy as jnp
from jax import lax
import numpy as np

D_MODEL = 1024
BATCH = 4
SEQ = 4096
DEPTH = 4

GRID_W = 64
CTX_LEN = 256
HEAD_DIM = 64
N_MIXERS = 4
GROUP_W = D_MODEL // N_MIXERS
ROPE_THETA = 10000.0
EPS = 1e-6
NEG_INF = -1e30

SWA_HEADS = GROUP_W // HEAD_DIM
SWA_KV_HEADS = SWA_HEADS // 2
SWA_WINDOW = 128
SWA_BLOCK = 128
GDN_HEADS = GROUP_W // HEAD_DIM
GDN_DK = HEAD_DIM
GDN_DV = HEAD_DIM
GDN_CONV = 5
GDN_CHUNK = 64
MLA_HEADS = GROUP_W // HEAD_DIM
MLA_Q_RANK = D_MODEL // 4
MLA_KV_RANK = D_MODEL // 8
MLA_NOPE = HEAD_DIM
MLA_ROPE = HEAD_DIM // 2
MLA_V = HEAD_DIM
MLA_BLOCK = 128
RET_HEADS = GROUP_W // HEAD_DIM
RET_DK = HEAD_DIM
RET_DV = HEAD_DIM
RET_CHUNK = 64
D_FF = 7 * D_MODEL // 2
N_EXPERTS = 8
TOP_K = 2

PROJ_SIZES = (
    SWA_HEADS * HEAD_DIM,
    SWA_KV_HEADS * HEAD_DIM,
    SWA_KV_HEADS * HEAD_DIM,
    GDN_HEADS * (2 * GDN_DK + GDN_DV),
    GDN_HEADS * GDN_DV,
    4 * GDN_HEADS,
    MLA_Q_RANK,
    MLA_KV_RANK,
    MLA_ROPE,
    RET_HEADS * RET_DK,
    RET_HEADS * RET_DK,
    RET_HEADS * RET_DV,
    RET_HEADS * RET_DV,
)
D_IN = sum(PROJ_SIZES)

kernel_name = 'hybrid_parallel_head_diffusion_trunk'


def rms_norm(x, g):
    xf = x.astype(jnp.float32)
    y = xf * lax.rsqrt(jnp.mean(xf * xf, axis=-1, keepdims=True) + EPS)
    return (y * g.astype(jnp.float32)).astype(x.dtype)


def modulate(x, g, shift, scale):
    return rms_norm(x, g) * (1.0 + scale) + shift


def l2_normalize(t):
    return t * lax.rsqrt(jnp.sum(t * t, axis=-1, keepdims=True) + EPS)


def head_group_norm(o, g):
    mu = jnp.mean(o, axis=-1, keepdims=True)
    var = jnp.mean(jnp.square(o - mu), axis=-1, keepdims=True)
    y = (o - mu) * lax.rsqrt(var + EPS)
    b, l, h, d = o.shape
    return y.reshape(b, l, h * d) * g.astype(jnp.float32)


def heads(t, n):
    return t.reshape(t.shape[0], t.shape[1], n, -1)


def split_proj(p):
    offsets = [int(o) for o in np.cumsum(PROJ_SIZES)[:-1]]
    return jnp.split(p, offsets, axis=-1)


def axial_rope_tables(rows, rot_dim):
    n_freq = rot_dim // 4
    inv_freq = ROPE_THETA ** (-jnp.arange(n_freq, dtype=jnp.float32) / n_freq)
    row = jnp.repeat(jnp.arange(rows, dtype=jnp.float32), GRID_W)
    col = jnp.tile(jnp.arange(GRID_W, dtype=jnp.float32), rows)
    ang_r = row[:, None] * inv_freq
    ang_c = col[:, None] * inv_freq
    ang = jnp.concatenate([ang_r, ang_r, ang_c, ang_c], axis=-1)
    return jnp.cos(ang), jnp.sin(ang)


def apply_rope(x, cos, sin):
    x1, x2, x3, x4 = jnp.split(x, 4, axis=-1)
    rot = jnp.concatenate([-x2, x1, -x4, x3], axis=-1)
    return (x * cos[:, None, :] + rot * sin[:, None, :]).astype(x.dtype)


def joint_softmax(scores, sink=None):
    m = scores[0].max(axis=-1, keepdims=True)
    for s in scores[1:]:
        m = jnp.maximum(m, s.max(axis=-1, keepdims=True))
    if sink is not None:
        m = jnp.maximum(m, sink)
    ps = [jnp.exp(s - m) for s in scores]
    denom = sum(p.sum(axis=-1, keepdims=True) for p in ps)
    if sink is not None:
        denom = denom + jnp.exp(sink - m)
    return [p / denom for p in ps]


def swa_latent(q, k, v, k_c, v_c, sink):
    b, n, hq, d = q.shape
    hkv = k.shape[2]
    rep = hq // hkv
    blk = SWA_BLOCK
    nb = n // blk
    scale = d ** -0.5
    qb = q.reshape(b, nb, blk, hkv, rep, d)

    def band(t):
        tb = t.reshape(b, nb, blk, hkv, d)
        pad = jnp.zeros_like(tb[:, :1])
        prev = jnp.concatenate([pad, tb[:, :-1]], axis=1)
        nxt = jnp.concatenate([tb[:, 1:], pad], axis=1)
        return jnp.concatenate([prev, tb, nxt], axis=2)

    kb, vb = band(k), band(v)
    q_off = jnp.arange(blk)[:, None] + blk
    k_off = jnp.arange(3 * blk)[None, :]
    key_pos = (jnp.arange(nb)[:, None] - 1) * blk + jnp.arange(3 * blk)[None, :]
    valid = (jnp.abs(q_off - k_off) <= SWA_WINDOW)[None] & ((key_pos >= 0) & (key_pos < n))[:, None, :]
    s_loc = jnp.einsum('bnqgrd,bnkgd->bngrqk', qb, kb).astype(jnp.float32) * scale
    s_loc = jnp.where(valid[None, :, None, None], s_loc, NEG_INF)
    s_ctx = jnp.einsum('bnqgrd,bmgd->bngrqm', qb, k_c).astype(jnp.float32) * scale
    sink_b = sink.astype(jnp.float32).reshape(1, 1, hkv, rep, 1, 1)
    p_loc, p_ctx = joint_softmax([s_loc, s_ctx], sink_b)
    o = (jnp.einsum('bngrqk,bnkgd->bnqgrd', p_loc.astype(v.dtype), vb)
         + jnp.einsum('bngrqm,bmgd->bnqgrd', p_ctx.astype(v.dtype), v_c))
    return o.reshape(b, n, hq * d)


def swa_context(q, k, v, sink):
    b, m, hq, d = q.shape
    hkv = k.shape[2]
    rep = hq // hkv
    s = jnp.einsum('bqgrd,bkgd->bgrqk', q.reshape(b, m, hkv, rep, d), k).astype(jnp.float32) * d ** -0.5
    (p,) = joint_softmax([s], sink.astype(jnp.float32).reshape(1, hkv, rep, 1, 1))
    return jnp.einsum('bgrqk,bkgd->bqgrd', p.astype(v.dtype), v).reshape(b, m, hq * d)


def directional_scan(chunked_fn, seqs, consts, s0, reverse):
    if reverse:
        seqs = tuple(jnp.flip(t, axis=1) for t in seqs)
    o, s = chunked_fn(*seqs, *consts, s0)
    if reverse:
        o = jnp.flip(o, axis=1)
    return o, s


def prefix_bidirectional_scan(chunked_fn, seqs_c, seqs_x, consts, state_shape):
    out_c, out_x = 0.0, 0.0
    for d in range(2):
        rev = d == 1
        s0 = jnp.zeros(state_shape, jnp.float32)
        oc, s_ctx = directional_scan(chunked_fn, seqs_c[d], consts[d], s0, rev)
        ox, _ = directional_scan(chunked_fn, seqs_x[d], consts[d], s_ctx, rev)
        out_c = out_c + oc
        out_x = out_x + ox
    return out_c, out_x


def short_conv(x, w):
    k = w.shape[0]
    y = lax.conv_general_dilated(
        x, w[:, None, :].astype(x.dtype), window_strides=(1,), padding=[(k // 2, k // 2)],
        dimension_numbers=('NWC', 'WIO', 'NWC'), feature_group_count=x.shape[-1])
    return jax.nn.silu(y)


def gated_delta_chunked(q, k, v, g, beta, s0):
    b, l, h, _ = q.shape
    dv = v.shape[-1]
    cs = GDN_CHUNK
    nc = l // cs

    def chunks(t):
        return t.reshape(b, nc, cs, h, -1).transpose(1, 0, 3, 2, 4)

    qc, kc, vc = chunks(q), chunks(k), chunks(v)
    gc = jnp.cumsum(chunks(g[..., None])[..., 0], axis=-1)
    bc = chunks(beta[..., None])
    idx = jnp.arange(cs)
    lower = idx[:, None] >= idx[None, :]
    strict = idx[:, None] > idx[None, :]
    decay = jnp.exp(jnp.where(lower, gc[..., :, None] - gc[..., None, :], NEG_INF))
    kb = kc * bc
    lmat = jnp.where(strict, jnp.einsum('nbhid,nbhjd->nbhij', kb, kc) * decay, 0.0)
    a_mat = lmat + jnp.eye(cs, dtype=jnp.float32)
    u = lax.linalg.triangular_solve(a_mat, vc * bc, left_side=True, lower=True)
    w = lax.linalg.triangular_solve(a_mat, kb * jnp.exp(gc)[..., None], left_side=True, lower=True)
    attn = jnp.einsum('nbhid,nbhjd->nbhij', qc, kc) * decay

    def step(s, xs):
        q_i, k_i, u_i, w_i, a_i, g_i = xs
        v_new = u_i - jnp.einsum('bhck,bhkv->bhcv', w_i, s)
        o_i = (jnp.einsum('bhck,bhkv->bhcv', q_i * jnp.exp(g_i)[..., None], s)
               + jnp.einsum('bhij,bhjv->bhiv', a_i, v_new))
        g_last = g_i[..., -1:]
        s = (s * jnp.exp(g_last)[..., None]
             + jnp.einsum('bhck,bhcv->bhkv', k_i * jnp.exp(g_last - g_i)[..., None], v_new))
        return s, o_i

    s_fin, o = lax.scan(step, s0, (qc, kc, u, w, attn, gc))
    return o.transpose(1, 0, 3, 2, 4).reshape(b, l, h, dv), s_fin


def gdn_mixer(parts_c, parts_x, need_ctx, conv_w, a_log, dt_bias, norm_g):
    def prep(qkv, ab):
        b, l, _ = qkv.shape
        qkv = short_conv(qkv, conv_w).astype(jnp.float32)
        q, k, v = jnp.split(qkv, [GDN_HEADS * GDN_DK, 2 * GDN_HEADS * GDN_DK], axis=-1)
        q = l2_normalize(q.reshape(b, l, GDN_HEADS, GDN_DK)) * GDN_DK ** -0.5
        k = l2_normalize(k.reshape(b, l, GDN_HEADS, GDN_DK))
        v = v.reshape(b, l, GDN_HEADS, GDN_DV)
        ab = ab.astype(jnp.float32).reshape(b, l, 2, 2, GDN_HEADS)
        g = -jnp.exp(a_log.astype(jnp.float32)) * jax.nn.softplus(ab[:, :, :, 0] + dt_bias.astype(jnp.float32))
        beta = jax.nn.sigmoid(ab[:, :, :, 1])
        return [(q, k, v, g[:, :, d], beta[:, :, d]) for d in range(2)]

    qkv_c, gate_c, ab_c = parts_c
    qkv_x, gate_x, ab_x = parts_x
    s_shape = (qkv_x.shape[0], GDN_HEADS, GDN_DK, GDN_DV)
    o_c, o_x = prefix_bidirectional_scan(gated_delta_chunked, prep(qkv_c, ab_c), prep(qkv_x, ab_x),
                                         [(), ()], s_shape)

    def gated_out(o, gate):
        b, l = gate.shape[:2]
        y = rms_norm(o, norm_g) * jax.nn.silu(gate.astype(jnp.float32)).reshape(o.shape)
        return y.reshape(b, l, GDN_HEADS * GDN_DV).astype(gate.dtype)

    return (gated_out(o_c, gate_c) if need_ctx else None), gated_out(o_x, gate_x)


def mla_queries(cq, q_norm, w_q_up):
    q = heads(rms_norm(cq, q_norm) @ w_q_up, MLA_HEADS)
    return q[..., :MLA_NOPE], q[..., MLA_NOPE:]


def mla_keys_values(ckv, kv_norm, w_kv_up):
    kv = heads(rms_norm(ckv, kv_norm) @ w_kv_up, MLA_HEADS)
    return kv[..., :MLA_NOPE], kv[..., MLA_NOPE:]


def mla_latent(qn, qr, kn, kr, v, kn_c, kr_c, v_c):
    b, n, h, _ = qn.shape
    nb = n // MLA_BLOCK
    scale = (MLA_NOPE + MLA_ROPE) ** -0.5

    def blocks(t):
        return jnp.moveaxis(t.reshape((b, nb, MLA_BLOCK) + t.shape[2:]), 1, 0)

    def one_block(args):
        qn_i, qr_i = args
        s_lat = (jnp.einsum('bqhd,bkhd->bhqk', qn_i, kn)
                 + jnp.einsum('bqhr,bkr->bhqk', qr_i, kr)).astype(jnp.float32) * scale
        s_ctx = (jnp.einsum('bqhd,bkhd->bhqk', qn_i, kn_c)
                 + jnp.einsum('bqhr,bkr->bhqk', qr_i, kr_c)).astype(jnp.float32) * scale
        p_lat, p_ctx = joint_softmax([s_lat, s_ctx])
        return (jnp.einsum('bhqk,bkhd->bqhd', p_lat.astype(v.dtype), v)
                + jnp.einsum('bhqk,bkhd->bqhd', p_ctx.astype(v.dtype), v_c))

    o = lax.map(one_block, (blocks(qn), blocks(qr)))
    return jnp.moveaxis(o, 0, 1).reshape(b, n, h * MLA_V)


def mla_context(qn, qr, kn, kr, v):
    b, m, h, _ = qn.shape
    scale = (MLA_NOPE + MLA_ROPE) ** -0.5
    s = (jnp.einsum('bqhd,bkhd->bhqk', qn, kn)
         + jnp.einsum('bqhr,bkr->bhqk', qr, kr)).astype(jnp.float32) * scale
    (p,) = joint_softmax([s])
    return jnp.einsum('bhqk,bkhd->bqhd', p.astype(v.dtype), v).reshape(b, m, h * MLA_V)


def retention_chunked(q, k, v, log_gamma, s0):
    b, l, h, _ = q.shape
    dv = v.shape[-1]
    cs = RET_CHUNK
    nc = l // cs

    def chunks(t):
        return t.reshape(b, nc, cs, h, -1).transpose(1, 0, 3, 2, 4)

    qc, kc, vc = chunks(q), chunks(k), chunks(v)
    pos = jnp.arange(cs, dtype=jnp.float32)
    lg = log_gamma[:, None]
    rel = pos[:, None] - pos[None, :]
    decay = jnp.where(rel >= 0, jnp.exp(lg[..., None] * jnp.maximum(rel, 0.0)), 0.0)
    q_decay = jnp.exp(lg * (pos + 1.0))[None, :, :, None]
    k_decay = jnp.exp(lg * (cs - 1.0 - pos))[None, :, :, None]
    chunk_decay = jnp.exp(log_gamma * cs)[None, :, None, None]
    intra = jnp.einsum('nbhij,nbhjv->nbhiv', jnp.einsum('nbhid,nbhjd->nbhij', qc, kc) * decay, vc)

    def step(s, xs):
        q_i, k_i, v_i = xs
        o_i = jnp.einsum('bhck,bhkv->bhcv', q_i * q_decay, s)
        s = s * chunk_decay + jnp.einsum('bhck,bhcv->bhkv', k_i * k_decay, v_i)
        return s, o_i

    s_fin, inter = lax.scan(step, s0, (qc, kc, vc))
    o = intra + inter
    return o.transpose(1, 0, 3, 2, 4).reshape(b, l, h, dv), s_fin


def retention_mixer(parts_c, parts_x, need_ctx, log_decay, norm_g, cos, sin):
    def prep(parts, rotate):
        q, k, v, _ = parts
        q = heads(q, RET_HEADS)
        k = heads(k, RET_HEADS)
        if rotate:
            q = apply_rope(q, cos, sin)
            k = apply_rope(k, cos, sin)
        seq = (q.astype(jnp.float32), k.astype(jnp.float32) * RET_DK ** -0.5,
               heads(v, RET_HEADS).astype(jnp.float32))
        return [seq, seq]

    log_gamma = -jnp.exp(log_decay.astype(jnp.float32))
    consts = [(log_gamma[0],), (log_gamma[1],)]
    s_shape = (parts_x[0].shape[0], RET_HEADS, RET_DK, RET_DV)
    o_c, o_x = prefix_bidirectional_scan(retention_chunked, prep(parts_c, False), prep(parts_x, True),
                                         consts, s_shape)

    def gated_out(o, gate):
        return (head_group_norm(o, norm_g) * jax.nn.silu(gate.astype(jnp.float32))).astype(gate.dtype)

    return (gated_out(o_c, parts_c[3]) if need_ctx else None), gated_out(o_x, parts_x[3])


def token_mixers(u_c, u_x, need_ctx, w_in, w_out, swa_sink, gdn_conv, gdn_a_log, gdn_dt_bias, gdn_norm,
                 mla_q_norm, mla_kv_norm, mla_w_q_up, mla_w_kv_up, ret_log_decay, ret_norm,
                 rope_hd, rope_mla):
    cos_h, sin_h = rope_hd
    cos_r, sin_r = rope_mla
    pc = split_proj(u_c @ w_in)
    px = split_proj(u_x @ w_in)
    ak_c, av_c = heads(pc[1], SWA_KV_HEADS), heads(pc[2], SWA_KV_HEADS)
    aq_x = apply_rope(heads(px[0], SWA_HEADS), cos_h, sin_h)
    ak_x = apply_rope(heads(px[1], SWA_KV_HEADS), cos_h, sin_h)
    out_a_x = swa_latent(aq_x, ak_x, heads(px[2], SWA_KV_HEADS), ak_c, av_c, swa_sink)
    out_b_c, out_b_x = gdn_mixer(pc[3:6], px[3:6], need_ctx, gdn_conv, gdn_a_log, gdn_dt_bias, gdn_norm)
    kn_c, v_c = mla_keys_values(pc[7], mla_kv_norm, mla_w_kv_up)
    kn_x, v_x = mla_keys_values(px[7], mla_kv_norm, mla_w_kv_up)
    qn_x, qr_x = mla_queries(px[6], mla_q_norm, mla_w_q_up)
    qr_x = apply_rope(qr_x, cos_r, sin_r)
    kr_x = apply_rope(px[8][:, :, None, :], cos_r, sin_r)[:, :, 0, :]
    out_c_x = mla_latent(qn_x, qr_x, kn_x, kr_x, v_x, kn_c, pc[8], v_c)
    out_d_c, out_d_x = retention_mixer(pc[9:13], px[9:13], need_ctx, ret_log_decay, ret_norm, cos_h, sin_h)
    mix_x = jnp.concatenate([out_a_x, out_b_x, out_c_x, out_d_x], axis=-1) @ w_out
    if not need_ctx:
        return None, mix_x
    out_a_c = swa_context(heads(pc[0], SWA_HEADS), ak_c, av_c, swa_sink)
    qn_c, qr_c = mla_queries(pc[6], mla_q_norm, mla_w_q_up)
    out_c_c = mla_context(qn_c, qr_c, kn_c, pc[8], v_c)
    mix_c = jnp.concatenate([out_a_c, out_b_c, out_c_c, out_d_c], axis=-1) @ w_out
    return mix_c, mix_x


def swiglu(x, w_gate, w_up, w_down):
    return (jax.nn.silu(x @ w_gate) * (x @ w_up)) @ w_down


def moe_swiglu(x, w_router, w_gate, w_up, w_down):
    logits = (x @ w_router).astype(jnp.float32)
    top_val, top_idx = lax.top_k(logits, TOP_K)
    gates = jax.nn.softmax(top_val, axis=-1)
    combine = jnp.sum(jax.nn.one_hot(top_idx, N_EXPERTS, dtype=jnp.float32) * gates[..., None], axis=1)
    out = jnp.zeros_like(x)
    for e in range(N_EXPERTS):
        out = out + combine[:, e:e + 1].astype(x.dtype) * swiglu(x, w_gate[e], w_up[e], w_down[e])
    return out


def setup_inputs(seed: int = 0) -> dict:
    key = jax.random.key(seed)
    keys = iter(jax.random.split(key, 40))
    f32 = jnp.float32
    d = D_MODEL
    n_dense = (DEPTH + 1) // 2
    n_moe = DEPTH // 2

    def normal(shape, scale):
        return jax.random.normal(next(keys), shape, f32) * scale

    def gain(shape):
        return 1.0 + normal(shape, 0.02)

    ret_base = jnp.log(-jnp.log1p(-(2.0 ** (-5.0 - jnp.arange(RET_HEADS, dtype=f32)))))
    dt = jnp.exp(jax.random.uniform(next(keys), (DEPTH, 2, GDN_HEADS), f32, math.log(1e-3), math.log(1e-1)))
    mix_w = N_MIXERS * GROUP_W
    return {
        'x': normal((BATCH, SEQ, d), 1.0),
        'c': normal((BATCH, d), 1.0),
        'ctx': normal((BATCH, CTX_LEN, d), 1.0),
        'c_ctx': normal((d,), 1.0),
        'w_mod': normal((DEPTH, d, 6 * d), 0.5 * d ** -0.5),
        'b_mod': normal((DEPTH, 6 * d), 0.02),
        'norm1': gain((DEPTH, d)),
        'norm2': gain((DEPTH, d)),
        'w_in': normal((DEPTH, d, D_IN), d ** -0.5),
        'w_out': normal((DEPTH, mix_w, d), mix_w ** -0.5),
        'swa_sink': normal((DEPTH, SWA_HEADS), 0.5),
        'gdn_conv': normal((DEPTH, GDN_CONV, GDN_HEADS * (2 * GDN_DK + GDN_DV)), GDN_CONV ** -0.5),
        'gdn_a_log': jnp.log(jax.random.uniform(next(keys), (DEPTH, 2, GDN_HEADS), f32, 1.0, 16.0)),
        'gdn_dt_bias': dt + jnp.log(-jnp.expm1(-dt)),
        'gdn_norm': gain((DEPTH, GDN_DV)),
        'mla_q_norm': gain((DEPTH, MLA_Q_RANK)),
        'mla_kv_norm': gain((DEPTH, MLA_KV_RANK)),
        'mla_w_q_up': normal((DEPTH, MLA_Q_RANK, MLA_HEADS * (MLA_NOPE + MLA_ROPE)), MLA_Q_RANK ** -0.5),
        'mla_w_kv_up': normal((DEPTH, MLA_KV_RANK, MLA_HEADS * (MLA_NOPE + MLA_V)), MLA_KV_RANK ** -0.5),
        'ret_log_decay': ret_base + normal((DEPTH, 2, RET_HEADS), 0.05),
        'ret_norm': gain((DEPTH, RET_HEADS * RET_DV)),
        'ffn_w_gate': normal((n_dense, d, D_FF), d ** -0.5),
        'ffn_w_up': normal((n_dense, d, D_FF), d ** -0.5),
        'ffn_w_down': normal((n_dense, D_FF, d), D_FF ** -0.5),
        'moe_router': normal((n_moe, d, N_EXPERTS), d ** -0.5),
        'moe_w_gate': normal((n_moe, N_EXPERTS, d, D_FF), d ** -0.5),
        'moe_w_up': normal((n_moe, N_EXPERTS, d, D_FF), d ** -0.5),
        'moe_w_down': normal((n_moe, N_EXPERTS, D_FF, d), D_FF ** -0.5),
        'final_norm': gain((d,)),
    }


def reference(x, c, ctx, c_ctx, w_mod, b_mod, norm1, norm2, w_in, w_out, swa_sink, gdn_conv, gdn_a_log,
              gdn_dt_bias, gdn_norm, mla_q_norm, mla_kv_norm, mla_w_q_up, mla_w_kv_up, ret_log_decay,
              ret_norm, ffn_w_gate, ffn_w_up, ffn_w_down, moe_router, moe_w_gate, moe_w_up, moe_w_down,
              final_norm):
    b, n, _ = x.shape
    m = ctx.shape[1]
    rows = n // GRID_W
    rope_hd = axial_rope_tables(rows, HEAD_DIM)
    rope_mla = axial_rope_tables(rows, MLA_ROPE)
    silu_c = jax.nn.silu(c)
    silu_cc = jax.nn.silu(c_ctx)
    h_x, h_c = x, ctx
    for layer in range(DEPTH):
        need_ctx = layer < DEPTH - 1
        mod_x = (silu_c @ w_mod[layer] + b_mod[layer])[:, None, :]
        mod_c = silu_cc @ w_mod[layer] + b_mod[layer]
        sh1_x, sc1_x, g1_x, sh2_x, sc2_x, g2_x = jnp.split(mod_x, 6, axis=-1)
        sh1_c, sc1_c, g1_c, sh2_c, sc2_c, g2_c = jnp.split(mod_c, 6, axis=-1)
        u_x = modulate(h_x, norm1[layer], sh1_x, sc1_x)
        u_c = modulate(h_c, norm1[layer], sh1_c, sc1_c)
        mix_c, mix_x = token_mixers(
            u_c, u_x, need_ctx, w_in[layer], w_out[layer], swa_sink[layer], gdn_conv[layer],
            gdn_a_log[layer], gdn_dt_bias[layer], gdn_norm[layer], mla_q_norm[layer], mla_kv_norm[layer],
            mla_w_q_up[layer], mla_w_kv_up[layer], ret_log_decay[layer], ret_norm[layer], rope_hd, rope_mla)
        h_x = h_x + g1_x * mix_x
        v_x = modulate(h_x, norm2[layer], sh2_x, sc2_x)
        if need_ctx:
            h_c = h_c + g1_c * mix_c
            tokens = jnp.concatenate([modulate(h_c, norm2[layer], sh2_c, sc2_c), v_x], axis=1)
        else:
            tokens = v_x
        lt = tokens.shape[1]
        flat = tokens.reshape(b * lt, D_MODEL)
        i = layer // 2
        if layer % 2 == 0:
            f = swiglu(flat, ffn_w_gate[i], ffn_w_up[i], ffn_w_down[i])
        else:
            f = moe_swiglu(flat, moe_router[i], moe_w_gate[i], moe_w_up[i], moe_w_down[i])
        f = f.reshape(b, lt, D_MODEL)
        if need_ctx:
            h_c = h_c + g2_c * f[:, :m]
            h_x = h_x + g2_x * f[:, m:]
        else:
            h_x = h_x + g2_x * f
    return rms_norm(h_x, final_norm)
```

```python
import functools
import math

import numpy as np
import jax
import jax.numpy as jnp
from jax import lax
from jax.experimental import pallas as pl
from jax.experimental.pallas import tpu as pltpu

D_MODEL = 1024
GRID_W = 64
CTX_LEN = 256
HEAD_DIM = 64
GROUP_W = 256
ROPE_THETA = 10000.0
EPS = 1e-6
NEG_INF = -1e30

SWA_HEADS = 4
SWA_KV_HEADS = 2
SWA_WINDOW = 128
SWA_BLOCK = 128
GDN_HEADS = 4
GDN_DK = 64
GDN_DV = 64
GDN_CHUNK = 64
MLA_HEADS = 4
MLA_Q_RANK = 256
MLA_KV_RANK = 128
MLA_NOPE = 64
MLA_ROPE = 32
MLA_V = 64
MLA_BLOCK = 128
RET_HEADS = 4
RET_DK = 64
RET_DV = 64
RET_CHUNK = 64
D_FF = 3584
N_EXPERTS = 8
TOP_K = 2

PROJ_SIZES = (256, 128, 128, 768, 256, 16, 256, 128, 32, 256, 256, 256, 256)
D_IN = sum(PROJ_SIZES)

LANES = 128
SUBLANES = 8
VMEM_LIMIT = 56 * 1024 * 1024

ROW_TILE = 256
FF_CHUNK = 512
MOE_TILE = 512

F32 = jnp.float32
BF16 = jnp.bfloat16


def _rms(x):
    return x * lax.rsqrt(jnp.mean(x * x, axis=-1, keepdims=True) + EPS)


def _silu(x):
    return x * (1.0 / (1.0 + jnp.exp(-x)))


def _norm_proj_kernel(h_ref, mod_ref, g_ref, w_ref, o_ref):
    x = h_ref[...]
    u = _rms(x) * g_ref[...] * (1.0 + mod_ref[1:2, :]) + mod_ref[0:1, :]
    o_ref[...] = jnp.dot(u.astype(BF16), w_ref[...], preferred_element_type=F32)


def norm_proj(h, mod, gain, w):
    b, l, d = h.shape
    n_out = w.shape[1]
    tm = ROW_TILE
    return pl.pallas_call(
        _norm_proj_kernel,
        grid=(b, l // tm),
        in_specs=[
            pl.BlockSpec((None, tm, d), lambda i, t: (i, t, 0)),
            pl.BlockSpec((None, None, SUBLANES, d), lambda i, t: (i, jnp.minimum(t, 1), 0, 0)),
            pl.BlockSpec((1, d), lambda i, t: (0, 0)),
            pl.BlockSpec((d, n_out), lambda i, t: (0, 0), pipeline_mode=pl.Buffered(1)),
        ],
        out_specs=pl.BlockSpec((None, tm, n_out), lambda i, t: (i, t, 0)),
        out_shape=jax.ShapeDtypeStruct((b, l, n_out), F32),
        compiler_params=pltpu.CompilerParams(
            dimension_semantics=("parallel", "parallel"), vmem_limit_bytes=VMEM_LIMIT),
        name="norm_proj",
    )(h, mod, gain, w)


def _out_proj_kernel(mix_ref, h_ref, mod_ref, g_ref, w_ref, *rest, with_router):
    if with_router:
        wr_ref, hn_ref, v_ref, lg_ref = rest
    else:
        hn_ref, v_ref = rest
    mix = jnp.dot(mix_ref[...].astype(BF16), w_ref[...], preferred_element_type=F32)
    hn = h_ref[...] + mod_ref[2:3, :] * mix
    hn_ref[...] = hn
    v = _rms(hn) * g_ref[...] * (1.0 + mod_ref[4:5, :]) + mod_ref[3:4, :]
    v_ref[...] = v.astype(BF16)
    if with_router:
        lg_ref[...] = jnp.dot(v, wr_ref[...], preferred_element_type=F32, precision=lax.Precision.HIGHEST)


def out_proj(mix, h, mod, gain, w, w_router=None):
    b, l, d = h.shape
    tm = ROW_TILE
    with_router = w_router is not None
    row = lambda i, t: (i, t, 0)
    in_specs = [
        pl.BlockSpec((None, tm, mix.shape[-1]), row),
        pl.BlockSpec((None, tm, d), row),
        pl.BlockSpec((None, None, SUBLANES, d), lambda i, t: (i, jnp.minimum(t, 1), 0, 0)),
        pl.BlockSpec((1, d), lambda i, t: (0, 0)),
        pl.BlockSpec(w.shape, lambda i, t: (0, 0), pipeline_mode=pl.Buffered(1)),
    ]
    out_specs = [pl.BlockSpec((None, tm, d), row), pl.BlockSpec((None, tm, d), row)]
    out_shape = [jax.ShapeDtypeStruct((b, l, d), F32), jax.ShapeDtypeStruct((b, l, d), BF16)]
    args = [mix, h, mod, gain, w]
    if with_router:
        in_specs.append(pl.BlockSpec(w_router.shape, lambda i, t: (0, 0)))
        out_specs.append(pl.BlockSpec((None, tm, LANES), row))
        out_shape.append(jax.ShapeDtypeStruct((b, l, LANES), F32))
        args.append(w_router)
    return pl.pallas_call(
        functools.partial(_out_proj_kernel, with_router=with_router),
        grid=(b, l // tm),
        in_specs=in_specs,
        out_specs=out_specs,
        out_shape=out_shape,
        compiler_params=pltpu.CompilerParams(
            dimension_semantics=("parallel", "parallel"), vmem_limit_bytes=VMEM_LIMIT),
        name="out_proj",
    )(*args)


def _ffn_kernel(v_ref, h_ref, mod_ref, wg_ref, wu_ref, wd_ref, o_ref):
    v = v_ref[...]
    acc = jnp.zeros(o_ref.shape, F32)
    for j in range(D_FF // FF_CHUNK):
        cols = slice(j * FF_CHUNK, (j + 1) * FF_CHUNK)
        a = jnp.dot(v, wg_ref[:, cols], preferred_element_type=F32)
        u = jnp.dot(v, wu_ref[:, cols], preferred_element_type=F32)
        mid = (_silu(a) * u).astype(BF16)
        acc = acc + jnp.dot(mid, wd_ref[cols, :], preferred_element_type=F32)
    o_ref[...] = h_ref[...] + mod_ref[5:6, :] * acc


def dense_ffn(v, h, mod, wg, wu, wd):
    b, l, d = h.shape
    tm = ROW_TILE
    row = lambda i, t: (i, t, 0)
    const = lambda i, t: (0, 0)
    return pl.pallas_call(
        _ffn_kernel,
        grid=(b, l // tm),
        in_specs=[
            pl.BlockSpec((None, tm, d), row),
            pl.BlockSpec((None, tm, d), row),
            pl.BlockSpec((None, None, SUBLANES, d), lambda i, t: (i, jnp.minimum(t, 1), 0, 0)),
            pl.BlockSpec(wg.shape, const, pipeline_mode=pl.Buffered(1)),
            pl.BlockSpec(wu.shape, const, pipeline_mode=pl.Buffered(1)),
            pl.BlockSpec(wd.shape, const, pipeline_mode=pl.Buffered(1)),
        ],
        out_specs=pl.BlockSpec((None, tm, d), row),
        out_shape=jax.ShapeDtypeStruct((b, l, d), F32),
        compiler_params=pltpu.CompilerParams(
            dimension_semantics=("parallel", "parallel"), vmem_limit_bytes=VMEM_LIMIT),
        name="dense_ffn",
    )(v, h, mod, wg, wu, wd)


def _moe_kernel(te_ref, tv_ref, x_ref, gate_ref, wg_ref, wu_ref, wd_ref, o_ref, acc_ref):
    i = pl.program_id(0)
    j = pl.program_id(1)
    nj = pl.num_programs(1)

    @pl.when(j == 0)
    def _():
        acc_ref[...] = jnp.zeros_like(acc_ref)

    @pl.when(tv_ref[i] > 0)
    def _():
        x = x_ref[...]
        a = jnp.dot(x, wg_ref[...], preferred_element_type=F32)
        u = jnp.dot(x, wu_ref[...], preferred_element_type=F32)
        mid = (_silu(a) * u).astype(BF16)
        acc_ref[...] += jnp.dot(mid, wd_ref[...], preferred_element_type=F32)

    @pl.when(j == nj - 1)
    def _():
        o_ref[...] = gate_ref[...] * acc_ref[...]


def moe_grouped_ffn(xs, gate_sorted, tile_expert, tile_valid, wg, wu, wd):
    p, d = xs.shape
    tm = MOE_TILE
    nt = p // tm
    nj = D_FF // FF_CHUNK
    grid_spec = pltpu.PrefetchScalarGridSpec(
        num_scalar_prefetch=2,
        grid=(nt, nj),
        in_specs=[
            pl.BlockSpec((tm, d), lambda i, j, te, tv: (i, 0)),
            pl.BlockSpec((tm, 1), lambda i, j, te, tv: (i, 0)),
            pl.BlockSpec((None, d, FF_CHUNK), lambda i, j, te, tv: (te[i], 0, j)),
            pl.BlockSpec((None, d, FF_CHUNK), lambda i, j, te, tv: (te[i], 0, j)),
            pl.BlockSpec((None, FF_CHUNK, d), lambda i, j, te, tv: (te[i], j, 0)),
        ],
        out_specs=pl.BlockSpec((tm, d), lambda i, j, te, tv: (i, 0)),
        scratch_shapes=[pltpu.VMEM((tm, d), F32)],
    )
    return pl.pallas_call(
        _moe_kernel,
        grid_spec=grid_spec,
        out_shape=jax.ShapeDtypeStruct((p, d), F32),
        compiler_params=pltpu.CompilerParams(
            dimension_semantics=("parallel", "arbitrary"), vmem_limit_bytes=VMEM_LIMIT),
        name="moe_ffn",
    )(tile_expert, tile_valid, xs, gate_sorted, wg, wu, wd)


def _residual_kernel(h_ref, f_ref, mod_ref, o_ref):
    o_ref[...] = h_ref[...] + mod_ref[5:6, :] * f_ref[...]


def gated_residual(h, f, mod):
    b, l, d = h.shape
    tm = ROW_TILE
    row = lambda i, t: (i, t, 0)
    return pl.pallas_call(
        _residual_kernel,
        grid=(b, l // tm),
        in_specs=[
            pl.BlockSpec((None, tm, d), row),
            pl.BlockSpec((None, tm, d), row),
            pl.BlockSpec((None, None, SUBLANES, d), lambda i, t: (i, jnp.minimum(t, 1), 0, 0)),
        ],
        out_specs=pl.BlockSpec((None, tm, d), row),
        out_shape=jax.ShapeDtypeStruct((b, l, d), F32),
        compiler_params=pltpu.CompilerParams(dimension_semantics=("parallel", "parallel")),
        name="gated_residual",
    )(h, f, mod)


def moe_ffn(v, logits, h, mod, wg, wu, wd):
    b, l, d = h.shape
    t = b * l
    tm = MOE_TILE
    p = TOP_K * t + N_EXPERTS * tm
    nt = p // tm
    lg = logits.reshape(t, LANES)[:, :N_EXPERTS]
    top_val, top_idx = lax.top_k(lg, TOP_K)
    gates = jax.nn.softmax(top_val, axis=-1)
    e_flat = top_idx.reshape(-1).astype(jnp.int32)
    onehot = (e_flat[:, None] == jnp.arange(N_EXPERTS, dtype=jnp.int32)[None, :]).astype(jnp.int32)
    rank = jnp.sum((jnp.cumsum(onehot, axis=0) - 1) * onehot, axis=1)
    counts = jnp.sum(onehot, axis=0)
    padded = ((counts + tm - 1) // tm) * tm
    ends = jnp.cumsum(padded)
    starts = ends - padded
    dest = starts[e_flat] + rank
    token = jnp.arange(TOP_K * t, dtype=jnp.int32) // TOP_K
    src = jnp.zeros((p,), jnp.int32).at[dest].set(token)
    gate_sorted = jnp.zeros((p,), F32).at[dest].set(gates.reshape(-1))
    tile_start = jnp.arange(nt, dtype=jnp.int32) * tm
    tile_expert = jnp.minimum(jnp.searchsorted(ends, tile_start, side="right"), N_EXPERTS - 1).astype(jnp.int32)
    tile_valid = (tile_start < ends[-1]).astype(jnp.int32)
    xs = jnp.take(v.reshape(t, d), src, axis=0)
    ys = moe_grouped_ffn(xs, gate_sorted[:, None], tile_expert, tile_valid, wg, wu, wd)
    dest2 = dest.reshape(t, TOP_K)
    f = jnp.take(ys, dest2[:, 0], axis=0) + jnp.take(ys, dest2[:, 1], axis=0)
    return gated_residual(h, f.reshape(b, l, d), mod)


def _final_norm_kernel(h_ref, g_ref, o_ref):
    o_ref[...] = _rms(h_ref[...]) * g_ref[...]


def final_rms_norm(h, gain, n_ctx_tiles):
    b, l, d = h.shape
    tm = ROW_TILE
    n = l - n_ctx_tiles * tm
    return pl.pallas_call(
        _final_norm_kernel,
        grid=(b, n // tm),
        in_specs=[
            pl.BlockSpec((None, tm, d), lambda i, t: (i, t + n_ctx_tiles, 0)),
            pl.BlockSpec((1, d), lambda i, t: (0, 0)),
        ],
        out_specs=pl.BlockSpec((None, tm, d), lambda i, t: (i, t, 0)),
        out_shape=jax.ShapeDtypeStruct((b, n, d), F32),
        compiler_params=pltpu.CompilerParams(dimension_semantics=("parallel", "parallel")),
        name="final_norm",
    )(h, gain)


def rms_norm(x, g):
    xf = x.astype(F32)
    y = xf * lax.rsqrt(jnp.mean(xf * xf, axis=-1, keepdims=True) + EPS)
    return (y * g.astype(F32)).astype(x.dtype)


def l2_normalize(t):
    return t * lax.rsqrt(jnp.sum(t * t, axis=-1, keepdims=True) + EPS)


def head_group_norm(o, g):
    mu = jnp.mean(o, axis=-1, keepdims=True)
    var = jnp.mean(jnp.square(o - mu), axis=-1, keepdims=True)
    y = (o - mu) * lax.rsqrt(var + EPS)
    b, l, h, d = o.shape
    return y.reshape(b, l, h * d) * g.astype(F32)


def heads(t, n):
    return t.reshape(t.shape[0], t.shape[1], n, -1)


def axial_rope_tables(rows, rot_dim):
    n_freq = rot_dim // 4
    inv_freq = ROPE_THETA ** (-jnp.arange(n_freq, dtype=F32) / n_freq)
    row = jnp.repeat(jnp.arange(rows, dtype=F32), GRID_W)
    col = jnp.tile(jnp.arange(GRID_W, dtype=F32), rows)
    ang_r = row[:, None] * inv_freq
    ang_c = col[:, None] * inv_freq
    ang = jnp.concatenate([ang_r, ang_r, ang_c, ang_c], axis=-1)
    return jnp.cos(ang), jnp.sin(ang)


def apply_rope(x, cos, sin):
    x1, x2, x3, x4 = jnp.split(x, 4, axis=-1)
    rot = jnp.concatenate([-x2, x1, -x4, x3], axis=-1)
    return (x * cos[:, None, :] + rot * sin[:, None, :]).astype(x.dtype)


def joint_softmax(scores, sink=None):
    m = scores[0].max(axis=-1, keepdims=True)
    for s in scores[1:]:
        m = jnp.maximum(m, s.max(axis=-1, keepdims=True))
    if sink is not None:
        m = jnp.maximum(m, sink)
    ps = [jnp.exp(s - m) for s in scores]
    denom = sum(p.sum(axis=-1, keepdims=True) for p in ps)
    if sink is not None:
        denom = denom + jnp.exp(sink - m)
    return [p / denom for p in ps]


def swa_latent(q, k, v, k_c, v_c, sink):
    b, n, hq, d = q.shape
    hkv = k.shape[2]
    rep = hq // hkv
    blk = SWA_BLOCK
    nb = n // blk
    scale = d ** -0.5
    qb = q.reshape(b, nb, blk, hkv, rep, d)

    def band(t):
        tb = t.reshape(b, nb, blk, hkv, d)
        pad = jnp.zeros_like(tb[:, :1])
        prev = jnp.concatenate([pad, tb[:, :-1]], axis=1)
        nxt = jnp.concatenate([tb[:, 1:], pad], axis=1)
        return jnp.concatenate([prev, tb, nxt], axis=2)

    kb, vb = band(k), band(v)
    q_off = jnp.arange(blk)[:, None] + blk
    k_off = jnp.arange(3 * blk)[None, :]
    key_pos = (jnp.arange(nb)[:, None] - 1) * blk + jnp.arange(3 * blk)[None, :]
    valid = (jnp.abs(q_off - k_off) <= SWA_WINDOW)[None] & ((key_pos >= 0) & (key_pos < n))[:, None, :]
    s_loc = jnp.einsum('bnqgrd,bnkgd->bngrqk', qb, kb).astype(F32) * scale
    s_loc = jnp.where(valid[None, :, None, None], s_loc, NEG_INF)
    s_ctx = jnp.einsum('bnqgrd,bmgd->bngrqm', qb, k_c).astype(F32) * scale
    sink_b = sink.astype(F32).reshape(1, 1, hkv, rep, 1, 1)
    p_loc, p_ctx = joint_softmax([s_loc, s_ctx], sink_b)
    o = (jnp.einsum('bngrqk,bnkgd->bnqgrd', p_loc.astype(v.dtype), vb)
         + jnp.einsum('bngrqm,bmgd->bnqgrd', p_ctx.astype(v.dtype), v_c))
    return o.reshape(b, n, hq * d)


def swa_context(q, k, v, sink):
    b, m, hq, d = q.shape
    hkv = k.shape[2]
    rep = hq // hkv
    s = jnp.einsum('bqgrd,bkgd->bgrqk', q.reshape(b, m, hkv, rep, d), k).astype(F32) * d ** -0.5
    (p,) = joint_softmax([s], sink.astype(F32).reshape(1, hkv, rep, 1, 1))
    return jnp.einsum('bgrqk,bkgd->bqgrd', p.astype(v.dtype), v).reshape(b, m, hq * d)


def directional_scan(chunked_fn, seqs, consts, s0, reverse):
    if reverse:
        seqs = tuple(jnp.flip(t, axis=1) for t in seqs)
    o, s = chunked_fn(*seqs, *consts, s0)
    if reverse:
        o = jnp.flip(o, axis=1)
    return o, s


def prefix_bidirectional_scan(chunked_fn, seqs_c, seqs_x, consts, state_shape):
    out_c, out_x = 0.0, 0.0
    for d in range(2):
        rev = d == 1
        s0 = jnp.zeros(state_shape, F32)
        oc, s_ctx = directional_scan(chunked_fn, seqs_c[d], consts[d], s0, rev)
        ox, _ = directional_scan(chunked_fn, seqs_x[d], consts[d], s_ctx, rev)
        out_c = out_c + oc
        out_x = out_x + ox
    return out_c, out_x


def short_conv(x, w):
    k = w.shape[0]
    y = lax.conv_general_dilated(
        x, w[:, None, :].astype(x.dtype), window_strides=(1,), padding=[(k // 2, k // 2)],
        dimension_numbers=('NWC', 'WIO', 'NWC'), feature_group_count=x.shape[-1])
    return jax.nn.silu(y)


def gated_delta_chunked(q, k, v, g, beta, s0):
    b, l, h, _ = q.shape
    dv = v.shape[-1]
    cs = GDN_CHUNK
    nc = l // cs

    def chunks(t):
        return t.reshape(b, nc, cs, h, -1).transpose(1, 0, 3, 2, 4)

    qc, kc, vc = chunks(q), chunks(k), chunks(v)
    gc = jnp.cumsum(chunks(g[..., None])[..., 0], axis=-1)
    bc = chunks(beta[..., None])
    idx = jnp.arange(cs)
    lower = idx[:, None] >= idx[None, :]
    strict = idx[:, None] > idx[None, :]
    decay = jnp.exp(jnp.where(lower, gc[..., :, None] - gc[..., None, :], NEG_INF))
    kb = kc * bc
    lmat = jnp.where(strict, jnp.einsum('nbhid,nbhjd->nbhij', kb, kc) * decay, 0.0)
    a_mat = lmat + jnp.eye(cs, dtype=F32)
    u = lax.linalg.triangular_solve(a_mat, vc * bc, left_side=True, lower=True)
    w = lax.linalg.triangular_solve(a_mat, kb * jnp.exp(gc)[..., None], left_side=True, lower=True)
    attn = jnp.einsum('nbhid,nbhjd->nbhij', qc, kc) * decay

    def step(s, xs):
        q_i, k_i, u_i, w_i, a_i, g_i = xs
        v_new = u_i - jnp.einsum('bhck,bhkv->bhcv', w_i, s)
        o_i = (jnp.einsum('bhck,bhkv->bhcv', q_i * jnp.exp(g_i)[..., None], s)
               + jnp.einsum('bhij,bhjv->bhiv', a_i, v_new))
        g_last = g_i[..., -1:]
        s = (s * jnp.exp(g_last)[..., None]
             + jnp.einsum('bhck,bhcv->bhkv', k_i * jnp.exp(g_last - g_i)[..., None], v_new))
        return s, o_i

    s_fin, o = lax.scan(step, s0, (qc, kc, u, w, attn, gc))
    return o.transpose(1, 0, 3, 2, 4).reshape(b, l, h, dv), s_fin


def gdn_mixer(parts_c, parts_x, conv_w, a_log, dt_bias, norm_g):
    def prep(qkv, ab):
        b, l, _ = qkv.shape
        qkv = short_conv(qkv, conv_w).astype(F32)
        q, k, v = jnp.split(qkv, [GDN_HEADS * GDN_DK, 2 * GDN_HEADS * GDN_DK], axis=-1)
        q = l2_normalize(q.reshape(b, l, GDN_HEADS, GDN_DK)) * GDN_DK ** -0.5
        k = l2_normalize(k.reshape(b, l, GDN_HEADS, GDN_DK))
        v = v.reshape(b, l, GDN_HEADS, GDN_DV)
        ab = ab.astype(F32).reshape(b, l, 2, 2, GDN_HEADS)
        g = -jnp.exp(a_log.astype(F32)) * jax.nn.softplus(ab[:, :, :, 0] + dt_bias.astype(F32))
        beta = jax.nn.sigmoid(ab[:, :, :, 1])
        return [(q, k, v, g[:, :, d], beta[:, :, d]) for d in range(2)]

    qkv_c, gate_c, ab_c = parts_c
    qkv_x, gate_x, ab_x = parts_x
    s_shape = (qkv_x.shape[0], GDN_HEADS, GDN_DK, GDN_DV)
    o_c, o_x = prefix_bidirectional_scan(gated_delta_chunked, prep(qkv_c, ab_c), prep(qkv_x, ab_x),
                                         [(), ()], s_shape)

    def gated_out(o, gate):
        b, l = gate.shape[:2]
        y = rms_norm(o, norm_g) * jax.nn.silu(gate.astype(F32)).reshape(o.shape)
        return y.reshape(b, l, GDN_HEADS * GDN_DV).astype(gate.dtype)

    return gated_out(o_c, gate_c), gated_out(o_x, gate_x)


def mla_queries(cq, q_norm, w_q_up):
    q = heads(rms_norm(cq, q_norm) @ w_q_up, MLA_HEADS)
    return q[..., :MLA_NOPE], q[..., MLA_NOPE:]


def mla_keys_values(ckv, kv_norm, w_kv_up):
    kv = heads(rms_norm(ckv, kv_norm) @ w_kv_up, MLA_HEADS)
    return kv[..., :MLA_NOPE], kv[..., MLA_NOPE:]


def mla_latent(qn, qr, kn, kr, v, kn_c, kr_c, v_c):
    b, n, h, _ = qn.shape
    nb = n // MLA_BLOCK
    scale = (MLA_NOPE + MLA_ROPE) ** -0.5

    def blocks(t):
        return jnp.moveaxis(t.reshape((b, nb, MLA_BLOCK) + t.shape[2:]), 1, 0)

    def one_block(args):
        qn_i, qr_i = args
        s_lat = (jnp.einsum('bqhd,bkhd->bhqk', qn_i, kn)
                 + jnp.einsum('bqhr,bkr->bhqk', qr_i, kr)).astype(F32) * scale
        s_ctx = (jnp.einsum('bqhd,bkhd->bhqk', qn_i, kn_c)
                 + jnp.einsum('bqhr,bkr->bhqk', qr_i, kr_c)).astype(F32) * scale
        p_lat, p_ctx = joint_softmax([s_lat, s_ctx])
        return (jnp.einsum('bhqk,bkhd->bqhd', p_lat.astype(v.dtype), v)
                + jnp.einsum('bhqk,bkhd->bqhd', p_ctx.astype(v.dtype), v_c))

    o = lax.map(one_block, (blocks(qn), blocks(qr)))
    return jnp.moveaxis(o, 0, 1).reshape(b, n, h * MLA_V)


def mla_context(qn, qr, kn, kr, v):
    b, m, h, _ = qn.shape
    scale = (MLA_NOPE + MLA_ROPE) ** -0.5
    s = (jnp.einsum('bqhd,bkhd->bhqk', qn, kn)
         + jnp.einsum('bqhr,bkr->bhqk', qr, kr)).astype(F32) * scale
    (p,) = joint_softmax([s])
    return jnp.einsum('bhqk,bkhd->bqhd', p.astype(v.dtype), v).reshape(b, m, h * MLA_V)


def retention_chunked(q, k, v, log_gamma, s0):
    b, l, h, _ = q.shape
    dv = v.shape[-1]
    cs = RET_CHUNK
    nc = l // cs

    def chunks(t):
        return t.reshape(b, nc, cs, h, -1).transpose(1, 0, 3, 2, 4)

    qc, kc, vc = chunks(q), chunks(k), chunks(v)
    pos = jnp.arange(cs, dtype=F32)
    lg = log_gamma[:, None]
    rel = pos[:, None] - pos[None, :]
    decay = jnp.where(rel >= 0, jnp.exp(lg[..., None] * jnp.maximum(rel, 0.0)), 0.0)
    q_decay = jnp.exp(lg * (pos + 1.0))[None, :, :, None]
    k_decay = jnp.exp(lg * (cs - 1.0 - pos))[None, :, :, None]
    chunk_decay = jnp.exp(log_gamma * cs)[None, :, None, None]
    intra = jnp.einsum('nbhij,nbhjv->nbhiv', jnp.einsum('nbhid,nbhjd->nbhij', qc, kc) * decay, vc)

    def step(s, xs):
        q_i, k_i, v_i = xs
        o_i = jnp.einsum('bhck,bhkv->bhcv', q_i * q_decay, s)
        s = s * chunk_decay + jnp.einsum('bhck,bhcv->bhkv', k_i * k_decay, v_i)
        return s, o_i

    s_fin, inter = lax.scan(step, s0, (qc, kc, vc))
    o = intra + inter
    return o.transpose(1, 0, 3, 2, 4).reshape(b, l, h, dv), s_fin


def retention_mixer(parts_c, parts_x, log_decay, norm_g, cos, sin):
    def prep(parts, rotate):
        q, k, v, _ = parts
        q = heads(q, RET_HEADS)
        k = heads(k, RET_HEADS)
        if rotate:
            q = apply_rope(q, cos, sin)
            k = apply_rope(k, cos, sin)
        seq = (q.astype(F32), k.astype(F32) * RET_DK ** -0.5, heads(v, RET_HEADS).astype(F32))
        return [seq, seq]

    log_gamma = -jnp.exp(log_decay.astype(F32))
    consts = [(log_gamma[0],), (log_gamma[1],)]
    s_shape = (parts_x[0].shape[0], RET_HEADS, RET_DK, RET_DV)
    o_c, o_x = prefix_bidirectional_scan(retention_chunked, prep(parts_c, False), prep(parts_x, True),
                                         consts, s_shape)

    def gated_out(o, gate):
        return (head_group_norm(o, norm_g) * jax.nn.silu(gate.astype(F32))).astype(gate.dtype)

    return gated_out(o_c, parts_c[3]), gated_out(o_x, parts_x[3])


def token_mixers(p, swa_sink, gdn_conv, gdn_a_log, gdn_dt_bias, gdn_norm, mla_q_norm, mla_kv_norm,
                 mla_w_q_up, mla_w_kv_up, ret_log_decay, ret_norm, rope_hd, rope_mla):
    cos_h, sin_h = rope_hd
    cos_r, sin_r = rope_mla
    offsets = [int(o) for o in np.cumsum(PROJ_SIZES)[:-1]]
    pc = jnp.split(p[:, :CTX_LEN], offsets, axis=-1)
    px = jnp.split(p[:, CTX_LEN:], offsets, axis=-1)
    ak_c, av_c = heads(pc[1], SWA_KV_HEADS), heads(pc[2], SWA_KV_HEADS)
    aq_x = apply_rope(heads(px[0], SWA_HEADS), cos_h, sin_h)
    ak_x = apply_rope(heads(px[1], SWA_KV_HEADS), cos_h, sin_h)
    out_a_x = swa_latent(aq_x, ak_x, heads(px[2], SWA_KV_HEADS), ak_c, av_c, swa_sink)
    out_b_c, out_b_x = gdn_mixer(pc[3:6], px[3:6], gdn_conv, gdn_a_log, gdn_dt_bias, gdn_norm)
    kn_c, v_c = mla_keys_values(pc[7], mla_kv_norm, mla_w_kv_up)
    kn_x, v_x = mla_keys_values(px[7], mla_kv_norm, mla_w_kv_up)
    qn_x, qr_x = mla_queries(px[6], mla_q_norm, mla_w_q_up)
    qr_x = apply_rope(qr_x, cos_r, sin_r)
    kr_x = apply_rope(px[8][:, :, None, :], cos_r, sin_r)[:, :, 0, :]
    out_c_x = mla_latent(qn_x, qr_x, kn_x, kr_x, v_x, kn_c, pc[8], v_c)
    out_d_c, out_d_x = retention_mixer(pc[9:13], px[9:13], ret_log_decay, ret_norm, cos_h, sin_h)
    out_a_c = swa_context(heads(pc[0], SWA_HEADS), ak_c, av_c, swa_sink)
    qn_c, qr_c = mla_queries(pc[6], mla_q_norm, mla_w_q_up)
    out_c_c = mla_context(qn_c, qr_c, kn_c, pc[8], v_c)
    mix_c = jnp.concatenate([out_a_c, out_b_c, out_c_c, out_d_c], axis=-1)
    mix_x = jnp.concatenate([out_a_x, out_b_x, out_c_x, out_d_x], axis=-1)
    return jnp.concatenate([mix_c, mix_x], axis=1)


def kernel(x, c, ctx, c_ctx, w_mod, b_mod, norm1, norm2, w_in, w_out, swa_sink, gdn_conv, gdn_a_log, gdn_dt_bias, gdn_norm, mla_q_norm, mla_kv_norm, mla_w_q_up, mla_w_kv_up, ret_log_decay, ret_norm, ffn_w_gate, ffn_w_up, ffn_w_down, moe_router, moe_w_gate, moe_w_up, moe_w_down, final_norm):
    b, n, d = x.shape
    depth = w_in.shape[0]
    rows = n // GRID_W
    rope_hd = axial_rope_tables(rows, HEAD_DIM)
    rope_mla = axial_rope_tables(rows, MLA_ROPE)
    silu_c = jax.nn.silu(c)
    silu_cc = jax.nn.silu(c_ctx)
    h = jnp.concatenate([ctx, x], axis=1)
    for layer in range(depth):
        mod_x = jnp.dot(silu_c, w_mod[layer], precision=lax.Precision.HIGHEST) + b_mod[layer]
        mod_c = jnp.dot(silu_cc, w_mod[layer], precision=lax.Precision.HIGHEST) + b_mod[layer]
        mod = jnp.stack([jnp.broadcast_to(mod_c, mod_x.shape), mod_x], axis=1).reshape(b, 2, 6, d)
        mod = jnp.pad(mod, ((0, 0), (0, 0), (0, SUBLANES - 6), (0, 0)))
        p = norm_proj(h, mod, norm1[layer][None, :], w_in[layer].astype(BF16))
        mix = token_mixers(
            p, swa_sink[layer], gdn_conv[layer], gdn_a_log[layer], gdn_dt_bias[layer], gdn_norm[layer],
            mla_q_norm[layer], mla_kv_norm[layer], mla_w_q_up[layer], mla_w_kv_up[layer],
            ret_log_decay[layer], ret_norm[layer], rope_hd, rope_mla)
        i = layer // 2
        if layer % 2 == 0:
            h, v = out_proj(mix, h, mod, norm2[layer][None, :], w_out[layer].astype(BF16))
            h = dense_ffn(v, h, mod, ffn_w_gate[i].astype(BF16), ffn_w_up[i].astype(BF16),
                          ffn_w_down[i].astype(BF16))
        else:
            w_r = jnp.pad(moe_router[i], ((0, 0), (0, LANES - N_EXPERTS)))
            h, v, logits = out_proj(mix, h, mod, norm2[layer][None, :], w_out[layer].astype(BF16), w_r)
            h = moe_ffn(v, logits, h, mod, moe_w_gate[i].astype(BF16), moe_w_up[i].astype(BF16),
                        moe_w_down[i].astype(BF16))
    return final_rms_norm(h, final_norm[None, :], CTX_LEN // ROW_TILE)
```

```python
import functools

import numpy as np
import jax
import jax.numpy as jnp
from jax import lax
from jax.experimental import pallas as pl
from jax.experimental.pallas import tpu as pltpu

D_MODEL = 1024
GRID_W = 64
CTX_LEN = 256
HEAD_DIM = 64
ROPE_THETA = 10000.0
EPS = 1e-6
NEG_INF = -1e30

SWA_WINDOW = 128
GDN_HEADS = 4
GDN_DK = 64
GDN_DV = 64
GDN_CHUNK = 64
MLA_HEADS = 4
MLA_NOPE = 64
MLA_ROPE = 32
MLA_V = 64
RET_HEADS = 4
RET_DK = 64
D_FF = 3584
N_EXPERTS = 8
TOP_K = 2

LANES = 128
SUBLANES = 8
VMEM_LIMIT = 56 * 1024 * 1024

ROW_TILE = 256
FF_CHUNK = 512
MOE_TILE = 512

A_W, B_W, C_W, D_W = 768, 1152, 512, 1024
A_ROT_W, C_ROT_W, D_ROT_W = 640, 128, 512
OFF_A = 0
OFF_B = OFF_A + A_W
OFF_C = OFF_B + B_W
OFF_D = OFF_C + C_W
OFF_AR = OFF_D + D_W
OFF_CR = OFF_AR + A_ROT_W
OFF_DR = OFF_CR + C_ROT_W
W_ALL = OFF_DR + D_ROT_W
ROPE_W = A_ROT_W + C_ROT_W + D_ROT_W

F32 = jnp.float32
BF16 = jnp.bfloat16
NT_DIMS = (((1,), (1,)), ((), ()))
TN_DIMS = (((0,), (0,)), ((), ()))


def _rms(x):
    return x * lax.rsqrt(jnp.mean(x * x, axis=-1, keepdims=True) + EPS)


def _silu(x):
    return x * (1.0 / (1.0 + jnp.exp(-x)))


def _dot(a, b):
    return jnp.dot(a, b, preferred_element_type=F32)


def _dot_nt(a, b):
    return lax.dot_general(a, b, NT_DIMS, preferred_element_type=F32)


def _dot_tn(a, b):
    return lax.dot_general(a, b, TN_DIMS, preferred_element_type=F32)


def _mod_spec(d):
    return pl.BlockSpec((None, None, SUBLANES, d), lambda i, t: (i, jnp.minimum(t, 1), 0, 0))


def _norm_proj_kernel(h_ref, mod_ref, g_ref, w_ref, cos_ref, sin_ref, a_ref, b_ref, c_ref, d_ref):
    x = h_ref[...]
    u = (_rms(x) * g_ref[...] * (1.0 + mod_ref[1:2, :]) + mod_ref[0:1, :]).astype(BF16)

    def mm(lo, width):
        return _dot(u, w_ref[:, lo:lo + width])

    a_main = mm(OFF_A, A_W)
    a_rot = mm(OFF_AR, A_ROT_W)
    a_ref[:, :A_ROT_W] = (a_main[:, :A_ROT_W] * cos_ref[:, :A_ROT_W] + a_rot * sin_ref[:, :A_ROT_W]).astype(BF16)
    a_ref[:, A_ROT_W:] = a_main[:, A_ROT_W:].astype(BF16)
    b_ref[...] = mm(OFF_B, B_W)
    c_main = mm(OFF_C, C_W)
    c_rot = mm(OFF_CR, C_ROT_W)
    lo, hi = A_ROT_W, A_ROT_W + C_ROT_W
    c_ref[:, :C_W - C_ROT_W] = c_main[:, :C_W - C_ROT_W]
    c_ref[:, C_W - C_ROT_W:] = c_main[:, C_W - C_ROT_W:] * cos_ref[:, lo:hi] + c_rot * sin_ref[:, lo:hi]
    d_main = mm(OFF_D, D_W)
    d_rot = mm(OFF_DR, D_ROT_W)
    d_ref[:, :D_ROT_W] = d_main[:, :D_ROT_W] * cos_ref[:, hi:] + d_rot * sin_ref[:, hi:]
    d_ref[:, D_ROT_W:] = d_main[:, D_ROT_W:]


def norm_proj(h, mod, gain, w, cos_t, sin_t):
    b, l, d = h.shape
    tm = ROW_TILE
    row = lambda i, t: (i, t, 0)
    return pl.pallas_call(
        _norm_proj_kernel,
        grid=(b, l // tm),
        in_specs=[
            pl.BlockSpec((None, tm, d), row),
            _mod_spec(d),
            pl.BlockSpec((1, d), lambda i, t: (0, 0)),
            pl.BlockSpec((d, W_ALL), lambda i, t: (0, 0), pipeline_mode=pl.Buffered(1)),
            pl.BlockSpec((tm, ROPE_W), lambda i, t: (t, 0)),
            pl.BlockSpec((tm, ROPE_W), lambda i, t: (t, 0)),
        ],
        out_specs=[pl.BlockSpec((None, tm, A_W), row), pl.BlockSpec((None, tm, B_W), row),
                   pl.BlockSpec((None, tm, C_W), row), pl.BlockSpec((None, tm, D_W), row)],
        out_shape=[jax.ShapeDtypeStruct((b, l, A_W), BF16), jax.ShapeDtypeStruct((b, l, B_W), F32),
                   jax.ShapeDtypeStruct((b, l, C_W), F32), jax.ShapeDtypeStruct((b, l, D_W), F32)],
        compiler_params=pltpu.CompilerParams(
            dimension_semantics=("parallel", "parallel"), vmem_limit_bytes=VMEM_LIMIT),
        name="norm_proj",
    )(h, mod, gain, w, cos_t, sin_t)


def _rot_cols(w, hd):
    x = w.reshape(w.shape[:-1] + (w.shape[-1] // hd, 4, hd // 4))
    x1, x2, x3, x4 = x[..., 0, :], x[..., 1, :], x[..., 2, :], x[..., 3, :]
    return jnp.stack([-x2, x1, -x4, x3], axis=-2).reshape(w.shape)


def _place_swa_q(q):
    z = jnp.zeros((q.shape[0], HEAD_DIM), q.dtype)
    blocks = []
    for h in range(4):
        qh = q[:, HEAD_DIM * h:HEAD_DIM * (h + 1)]
        blocks += [qh, z] if h // 2 == 0 else [z, qh]
    return jnp.concatenate(blocks, axis=1)


def build_in_weight(w):
    d = w.shape[0]
    o = [int(v) for v in np.cumsum((256, 128, 128, 768, 256, 16, 256, 128, 32, 256, 256, 256, 256))]
    aq, ak, av = w[:, :o[0]] * HEAD_DIM ** -0.5, w[:, o[0]:o[1]], w[:, o[1]:o[2]]
    b_main, b_ab = w[:, o[2]:o[4]], w[:, o[4]:o[5]]
    c_q, c_kv, c_kr = w[:, o[5]:o[6]], w[:, o[6]:o[7]], w[:, o[7]:o[8]]
    dq, dk, dvg = w[:, o[8]:o[9]], w[:, o[9]:o[10]] * RET_DK ** -0.5, w[:, o[10]:]
    z = lambda n: jnp.zeros((d, n), w.dtype)
    parts = [
        _place_swa_q(aq), ak, av,
        b_main, b_ab, z(LANES - b_ab.shape[1]),
        c_q, c_kv, z(64), c_kr, z(32),
        dq, dk, dvg,
        _place_swa_q(_rot_cols(aq, HEAD_DIM)), _rot_cols(ak, HEAD_DIM),
        z(64), _rot_cols(c_kr, MLA_ROPE), z(32),
        _rot_cols(dq, HEAD_DIM), _rot_cols(dk, HEAD_DIM),
    ]
    out = jnp.concatenate(parts, axis=1)
    assert out.shape[1] == W_ALL
    return out.astype(BF16)


def rope_tables(n):
    def axial(rot_dim):
        n_freq = rot_dim // 4
        inv_freq = ROPE_THETA ** (-jnp.arange(n_freq, dtype=F32) / n_freq)
        row = jnp.repeat(jnp.arange(n // GRID_W, dtype=F32), GRID_W)
        col = jnp.tile(jnp.arange(GRID_W, dtype=F32), n // GRID_W)
        ang_r = row[:, None] * inv_freq
        ang_c = col[:, None] * inv_freq
        ang = jnp.concatenate([ang_r, ang_r, ang_c, ang_c], axis=-1)
        return jnp.cos(ang), jnp.sin(ang)

    cos_h, sin_h = axial(HEAD_DIM)
    cos_r, sin_r = axial(MLA_ROPE)
    one, zero = jnp.ones((n, 1), F32), jnp.zeros((n, 1), F32)
    cos_c = jnp.concatenate([jnp.tile(one, (1, 64)), cos_r, jnp.tile(one, (1, 32))], axis=1)
    sin_c = jnp.concatenate([jnp.tile(zero, (1, 64)), sin_r, jnp.tile(zero, (1, 32))], axis=1)
    cos_t = jnp.concatenate([jnp.tile(cos_h, (1, A_ROT_W // HEAD_DIM)), cos_c,
                             jnp.tile(cos_h, (1, D_ROT_W // HEAD_DIM))], axis=1)
    sin_t = jnp.concatenate([jnp.tile(sin_h, (1, A_ROT_W // HEAD_DIM)), sin_c,
                             jnp.tile(sin_h, (1, D_ROT_W // HEAD_DIM))], axis=1)
    cos_t = jnp.concatenate([jnp.ones((CTX_LEN, ROPE_W), F32), cos_t], axis=0)
    sin_t = jnp.concatenate([jnp.zeros((CTX_LEN, ROPE_W), F32), sin_t], axis=0)
    return cos_t, sin_t


def _swa_kernel(sink_ref, q_ref, kp_ref, ko_ref, kn_ref, kc_ref, vp_ref, vo_ref, vn_ref, vc_ref, o_ref):
    t = pl.program_id(1)
    last = pl.num_programs(1) - 1
    tq = q_ref.shape[0]
    half = tq // 2

    def head_out(h, pieces):
        q = q_ref[:, LANES * h:LANES * (h + 1)]
        ss = []
        for k_ref, _, mask in pieces:
            s = _dot_nt(q, k_ref[...])
            ss.append(s if mask is None else jnp.where(mask, s, NEG_INF))
        sink = sink_ref[h]
        m = jnp.maximum(functools.reduce(jnp.maximum, [s.max(axis=-1, keepdims=True) for s in ss]), sink)
        ps = [jnp.exp(s - m) for s in ss]
        denom = functools.reduce(jnp.add, [p.sum(axis=-1, keepdims=True) for p in ps]) + jnp.exp(sink - m)
        o = functools.reduce(jnp.add, [_dot(p.astype(BF16), piece[1][...]) for p, piece in zip(ps, pieces)])
        return o / denom

    def write(pieces):
        outs = [head_out(h, pieces) for h in range(4)]
        lane = lax.broadcasted_iota(jnp.int32, (tq, LANES), 1)
        for r in range(2):
            o_ref[:, LANES * r:LANES * (r + 1)] = jnp.where(lane < HEAD_DIM, outs[r], outs[2 + r]).astype(o_ref.dtype)

    @pl.when(t == 0)
    def _():
        write([(kc_ref, vc_ref, None)])

    @pl.when(t > 0)
    def _():
        qi = lax.broadcasted_iota(jnp.int32, (tq, half), 0)
        kj = lax.broadcasted_iota(jnp.int32, (tq, half), 1)
        mask_prev = (kj >= qi) & (t > 1)
        mask_next = (kj <= qi - half) & (t < last)
        qo = lax.broadcasted_iota(jnp.int32, (tq, tq), 0)
        ko = lax.broadcasted_iota(jnp.int32, (tq, tq), 1)
        mask_own = jnp.abs(qo - ko) <= SWA_WINDOW
        write([(kp_ref, vp_ref, mask_prev), (ko_ref, vo_ref, mask_own), (kn_ref, vn_ref, mask_next),
               (kc_ref, vc_ref, None)])


def swa_mixer(pa, sink):
    b, l, _ = pa.shape
    tq = ROW_TILE
    nblk = l // SWA_WINDOW
    kcol, vcol = 4, 5
    prev = lambda c: (lambda i, t: (i, jnp.maximum(2 * t - 1, 2), c))
    nxt = lambda c: (lambda i, t: (i, jnp.minimum(2 * t + 2, nblk - 1), c))
    own = lambda c: (lambda i, t: (i, t, c))
    ctx = lambda c: (lambda i, t: (i, 0, c))
    kv_specs = lambda c: [pl.BlockSpec((None, SWA_WINDOW, LANES), prev(c)), pl.BlockSpec((None, tq, LANES), own(c)),
                          pl.BlockSpec((None, SWA_WINDOW, LANES), nxt(c)), pl.BlockSpec((None, tq, LANES), ctx(c))]
    return pl.pallas_call(
        _swa_kernel,
        grid=(b, l // tq),
        in_specs=[pl.BlockSpec(memory_space=pltpu.SMEM),
                  pl.BlockSpec((None, tq, 4 * LANES), lambda i, t: (i, t, 0))] + kv_specs(kcol) + kv_specs(vcol),
        out_specs=pl.BlockSpec((None, tq, 2 * LANES), lambda i, t: (i, t, 0)),
        out_shape=jax.ShapeDtypeStruct((b, l, 2 * LANES), BF16),
        compiler_params=pltpu.CompilerParams(
            dimension_semantics=("parallel", "parallel"), vmem_limit_bytes=VMEM_LIMIT),
        name="swa",
    )(sink, pa, pa, pa, pa, pa, pa, pa, pa, pa)


def _mla_prep_kernel(c_ref, qn_ref, kvn_ref, wq_ref, wqr_ref, wk_ref, wv_ref, cos_ref, sin_ref,
                     q_ref, k_ref, v_ref):
    cq = c_ref[:, 0:256]
    ckv = c_ref[:, 256:384]
    kr = c_ref[:, 384:512]
    nq = (_rms(cq) * qn_ref[...]).astype(BF16)
    nkv = (_rms(ckv) * kvn_ref[...]).astype(BF16)
    cos = jnp.concatenate([cos_ref[...]] * MLA_HEADS, axis=1)
    sin = jnp.concatenate([sin_ref[...]] * MLA_HEADS, axis=1)
    q_ref[...] = (_dot(nq, wq_ref[...]) * cos + _dot(nq, wqr_ref[...]) * sin).astype(BF16)
    k_ref[...] = (_dot(nkv, wk_ref[...]) + jnp.concatenate([kr] * MLA_HEADS, axis=1)).astype(BF16)
    v_ref[...] = _dot(nkv, wv_ref[...]).astype(BF16)


def mla_prep(pc, q_norm, kv_norm, w_q_up, w_kv_up, cos_c, sin_c):
    b, l, _ = pc.shape
    tm = ROW_TILE
    scale = (MLA_NOPE + MLA_ROPE) ** -0.5
    wq = (w_q_up * scale).reshape(-1, MLA_HEADS, MLA_NOPE + MLA_ROPE)
    zq = jnp.zeros(wq.shape[:2] + (LANES - MLA_NOPE - MLA_ROPE,), F32)
    wq_main = jnp.concatenate([wq, zq], axis=-1).reshape(-1, MLA_HEADS * LANES)
    wq_rot = jnp.concatenate([jnp.zeros_like(wq[..., :MLA_NOPE]), _rot_cols(wq[..., MLA_NOPE:], MLA_ROPE), zq],
                             axis=-1).reshape(-1, MLA_HEADS * LANES)
    wkv = w_kv_up.reshape(-1, MLA_HEADS, MLA_NOPE + MLA_V)
    wk = jnp.concatenate([wkv[..., :MLA_NOPE], jnp.zeros_like(wkv[..., :LANES - MLA_NOPE])],
                         axis=-1).reshape(-1, MLA_HEADS * LANES)
    wv = wkv[..., MLA_NOPE:].reshape(-1, MLA_HEADS * MLA_V)
    row = lambda i, t: (i, t, 0)
    const = lambda i, t: (0, 0)
    full = lambda a: pl.BlockSpec(a.shape, const)
    args = [q_norm[None, :], kv_norm[None, :], wq_main.astype(BF16), wq_rot.astype(BF16), wk.astype(BF16),
            wv.astype(BF16)]
    return pl.pallas_call(
        _mla_prep_kernel,
        grid=(b, l // tm),
        in_specs=[pl.BlockSpec((None, tm, C_W), row)] + [full(a) for a in args]
        + [pl.BlockSpec((tm, LANES), lambda i, t: (t, 0)), pl.BlockSpec((tm, LANES), lambda i, t: (t, 0))],
        out_specs=[pl.BlockSpec((None, tm, 4 * LANES), row), pl.BlockSpec((None, tm, 4 * LANES), row),
                   pl.BlockSpec((None, tm, 2 * LANES), row)],
        out_shape=[jax.ShapeDtypeStruct((b, l, 4 * LANES), BF16), jax.ShapeDtypeStruct((b, l, 4 * LANES), BF16),
                   jax.ShapeDtypeStruct((b, l, 2 * LANES), BF16)],
        compiler_params=pltpu.CompilerParams(dimension_semantics=("parallel", "parallel")),
        name="mla_prep",
    )(pc, *args, cos_c, sin_c)


def _mla_attn_kernel(q_ref, k_ref, v_ref, o_ref):
    t = pl.program_id(2)
    tq = q_ref.shape[0]

    def attend(nk):
        outs = []
        for j in range(2):
            q = q_ref[:, LANES * j:LANES * (j + 1)]
            k = k_ref[0:nk, LANES * j:LANES * (j + 1)]
            s = _dot_nt(q, k)
            p = jnp.exp(s - s.max(axis=-1, keepdims=True))
            denom = p.sum(axis=-1, keepdims=True)
            outs.append(_dot(p.astype(BF16), v_ref[0:nk, :]) / denom)
        lane = lax.broadcasted_iota(jnp.int32, (tq, LANES), 1)
        o_ref[...] = jnp.where(lane < MLA_V, outs[0], outs[1]).astype(o_ref.dtype)

    @pl.when(t == 0)
    def _():
        attend(CTX_LEN)

    @pl.when(t > 0)
    def _():
        attend(k_ref.shape[0])


def mla_attention(q, k, v):
    b, l, _ = q.shape
    tq = ROW_TILE
    return pl.pallas_call(
        _mla_attn_kernel,
        grid=(b, 2, l // tq),
        in_specs=[pl.BlockSpec((None, tq, 2 * LANES), lambda i, p, t: (i, t, p)),
                  pl.BlockSpec((None, l, 2 * LANES), lambda i, p, t: (i, 0, p)),
                  pl.BlockSpec((None, l, LANES), lambda i, p, t: (i, 0, p))],
        out_specs=pl.BlockSpec((None, tq, LANES), lambda i, p, t: (i, t, p)),
        out_shape=jax.ShapeDtypeStruct((b, l, 2 * LANES), BF16),
        compiler_params=pltpu.CompilerParams(
            dimension_semantics=("parallel", "parallel", "parallel"), vmem_limit_bytes=VMEM_LIMIT),
        name="mla_attn",
    )(q, k, v)


def _head_mean(x, ones_bd):
    hi = x.astype(BF16)
    lo = (x - hi.astype(F32)).astype(BF16)
    return (_dot(hi, ones_bd) + _dot(lo, ones_bd)) * (1.0 / HEAD_DIM)


def _ret_kernel(x_ref, lg_ref, g_ref, o_ref, s_ref, dec_ref):
    dr = pl.program_id(1)
    s = pl.program_id(2)
    ns = pl.num_programs(2)
    c = x_ref.shape[0]
    w = RET_HEADS * HEAD_DIM
    chunk = jnp.where(s == 0, 0, jnp.where(dr == 0, s, ns - s))
    lg = lg_ref[...]
    fwd = dr == 0
    row_h = lax.broadcasted_iota(jnp.int32, (w, w), 0) // HEAD_DIM
    col_h = lax.broadcasted_iota(jnp.int32, (w, w), 1) // HEAD_DIM
    same_head = row_h == col_h

    @pl.when(s == 0)
    def _():
        s_ref[...] = jnp.zeros_like(s_ref)
        i = lax.broadcasted_iota(jnp.int32, (c, c), 0)
        j = lax.broadcasted_iota(jnp.int32, (c, c), 1)
        rel = jnp.where(fwd, i - j, j - i)
        relf = jnp.maximum(rel, 0).astype(F32)
        for h in range(RET_HEADS):
            lg_h = lg_ref[0:1, HEAD_DIM * h:HEAD_DIM * h + 1]
            dec_ref[h] = jnp.where(rel >= 0, jnp.exp(lg_h * relf), 0.0)

    q = x_ref[:, 0:w]
    k = x_ref[:, w:2 * w]
    v = x_ref[:, 2 * w:3 * w].astype(BF16)
    kb = k.astype(BF16)
    pos = lax.broadcasted_iota(jnp.int32, (c, 1), 0).astype(F32)
    q_pow = jnp.where(fwd, pos + 1.0, c - pos)
    k_pow = jnp.where(fwd, c - 1.0 - pos, pos)
    lane_h = lax.broadcasted_iota(jnp.int32, (c, w), 1) // HEAD_DIM
    acc = _dot((q * jnp.exp(lg * q_pow)).astype(BF16), s_ref[...].astype(BF16))
    for h in range(RET_HEADS):
        qh = jnp.where(lane_h == h, q, 0.0).astype(BF16)
        a = _dot_nt(qh, kb) * dec_ref[h]
        acc = acc + jnp.where(lane_h == h, _dot(a.astype(BF16), v), 0.0)
    kv = _dot_tn((k * jnp.exp(lg * k_pow)).astype(BF16), v)
    s_ref[...] = s_ref[...] * jnp.exp(lg * float(c)) + jnp.where(same_head, kv, 0.0)

    rows = pl.ds(pl.multiple_of(chunk * c, c), c)

    @pl.when(dr == 0)
    def _():
        o_ref[rows, :] = acc

    @pl.when(dr == 1)
    def _():
        o = o_ref[rows, :] + acc
        ones_bd = jnp.where(same_head, 1.0, 0.0).astype(BF16)
        mu = _head_mean(o, ones_bd)
        var = _head_mean(jnp.square(o - mu), ones_bd)
        y = (o - mu) * lax.rsqrt(var + EPS) * g_ref[...]
        o_ref[rows, :] = y * _silu(x_ref[:, 3 * w:4 * w])


def retention_mixer(pd, log_decay, norm_g):
    b, l, _ = pd.shape
    c = ROW_TILE
    ns = l // c
    w = RET_HEADS * HEAD_DIM
    lg = jnp.repeat(-jnp.exp(log_decay.astype(F32)), HEAD_DIM, axis=-1)[:, None, :]

    def chunk_of(dr, s):
        return jnp.where(s == 0, 0, jnp.where(dr == 0, s, ns - s))

    return pl.pallas_call(
        _ret_kernel,
        grid=(b, 2, ns),
        in_specs=[pl.BlockSpec((None, c, D_W), lambda i, dr, s: (i, chunk_of(dr, s), 0)),
                  pl.BlockSpec((None, 1, w), lambda i, dr, s: (dr, 0, 0)),
                  pl.BlockSpec((1, w), lambda i, dr, s: (0, 0))],
        out_specs=pl.BlockSpec((None, l, w), lambda i, dr, s: (i, 0, 0)),
        out_shape=jax.ShapeDtypeStruct((b, l, w), F32),
        scratch_shapes=[pltpu.VMEM((w, w), F32), pltpu.VMEM((RET_HEADS, c, c), F32)],
        compiler_params=pltpu.CompilerParams(
            dimension_semantics=("parallel", "arbitrary", "arbitrary"), vmem_limit_bytes=VMEM_LIMIT),
        name="retention",
    )(pd, lg, norm_g[None, :])


def _out_proj_kernel(ma_ref, mb_ref, mc_ref, md_ref, h_ref, mod_ref, g_ref, w_ref, *rest, with_router):
    if with_router:
        wr_ref, hn_ref, v_ref, lg_ref = rest
    else:
        hn_ref, v_ref = rest
    gw = 2 * LANES
    mix = functools.reduce(jnp.add, [
        _dot(m_ref[...].astype(BF16), w_ref[gw * i:gw * (i + 1), :])
        for i, m_ref in enumerate((ma_ref, mb_ref, mc_ref, md_ref))])
    hn = h_ref[...] + mod_ref[2:3, :] * mix
    hn_ref[...] = hn
    v = _rms(hn) * g_ref[...] * (1.0 + mod_ref[4:5, :]) + mod_ref[3:4, :]
    v_ref[...] = v.astype(BF16)
    if with_router:
        lg_ref[...] = jnp.dot(v, wr_ref[...], preferred_element_type=F32, precision=lax.Precision.HIGHEST)


def out_proj(mixes, h, mod, gain, w, w_router=None):
    b, l, d = h.shape
    tm = ROW_TILE
    with_router = w_router is not None
    row = lambda i, t: (i, t, 0)
    in_specs = [pl.BlockSpec((None, tm, 2 * LANES), row) for _ in mixes] + [
        pl.BlockSpec((None, tm, d), row),
        _mod_spec(d),
        pl.BlockSpec((1, d), lambda i, t: (0, 0)),
        pl.BlockSpec(w.shape, lambda i, t: (0, 0), pipeline_mode=pl.Buffered(1)),
    ]
    out_specs = [pl.BlockSpec((None, tm, d), row), pl.BlockSpec((None, tm, d), row)]
    out_shape = [jax.ShapeDtypeStruct((b, l, d), F32), jax.ShapeDtypeStruct((b, l, d), BF16)]
    args = list(mixes) + [h, mod, gain, w]
    if with_router:
        in_specs.append(pl.BlockSpec(w_router.shape, lambda i, t: (0, 0)))
        out_specs.append(pl.BlockSpec((None, tm, LANES), row))
        out_shape.append(jax.ShapeDtypeStruct((b, l, LANES), F32))
        args.append(w_router)
    return pl.pallas_call(
        functools.partial(_out_proj_kernel, with_router=with_router),
        grid=(b, l // tm),
        in_specs=in_specs,
        out_specs=out_specs,
        out_shape=out_shape,
        compiler_params=pltpu.CompilerParams(
            dimension_semantics=("parallel", "parallel"), vmem_limit_bytes=VMEM_LIMIT),
        name="out_proj",
    )(*args)


def build_out_weight(w):
    hd = HEAD_DIM
    perm = jnp.concatenate([w[0:hd], w[2 * hd:3 * hd], w[hd:2 * hd], w[3 * hd:4 * hd]], axis=0)
    return jnp.concatenate([perm, w[4 * hd:]], axis=0).astype(BF16)


def _ffn_kernel(v_ref, h_ref, mod_ref, wg_ref, wu_ref, wd_ref, o_ref):
    v = v_ref[...]
    acc = jnp.zeros(o_ref.shape, F32)
    for j in range(D_FF // FF_CHUNK):
        cols = slice(j * FF_CHUNK, (j + 1) * FF_CHUNK)
        a = _dot(v, wg_ref[:, cols])
        u = _dot(v, wu_ref[:, cols])
        mid = (_silu(a) * u).astype(BF16)
        acc = acc + _dot(mid, wd_ref[cols, :])
    o_ref[...] = h_ref[...] + mod_ref[5:6, :] * acc


def dense_ffn(v, h, mod, wg, wu, wd):
    b, l, d = h.shape
    tm = ROW_TILE
    row = lambda i, t: (i, t, 0)
    const = lambda i, t: (0, 0)
    return pl.pallas_call(
        _ffn_kernel,
        grid=(b, l // tm),
        in_specs=[
            pl.BlockSpec((None, tm, d), row),
            pl.BlockSpec((None, tm, d), row),
            _mod_spec(d),
            pl.BlockSpec(wg.shape, const, pipeline_mode=pl.Buffered(1)),
            pl.BlockSpec(wu.shape, const, pipeline_mode=pl.Buffered(1)),
            pl.BlockSpec(wd.shape, const, pipeline_mode=pl.Buffered(1)),
        ],
        out_specs=pl.BlockSpec((None, tm, d), row),
        out_shape=jax.ShapeDtypeStruct((b, l, d), F32),
        compiler_params=pltpu.CompilerParams(
            dimension_semantics=("parallel", "parallel"), vmem_limit_bytes=VMEM_LIMIT),
        name="dense_ffn",
    )(v, h, mod, wg, wu, wd)


def _moe_kernel(te_ref, tv_ref, x_ref, gate_ref, wg_ref, wu_ref, wd_ref, o_ref, acc_ref):
    i = pl.program_id(0)
    j = pl.program_id(1)
    nj = pl.num_programs(1)

    @pl.when(j == 0)
    def _():
        acc_ref[...] = jnp.zeros_like(acc_ref)

    @pl.when(tv_ref[i] > 0)
    def _():
        x = x_ref[...]
        a = _dot(x, wg_ref[...])
        u = _dot(x, wu_ref[...])
        mid = (_silu(a) * u).astype(BF16)
        acc_ref[...] += _dot(mid, wd_ref[...])

    @pl.when(j == nj - 1)
    def _():
        o_ref[...] = gate_ref[...] * acc_ref[...]


def moe_grouped_ffn(xs, gate_sorted, tile_expert, tile_valid, wg, wu, wd):
    p, d = xs.shape
    tm = MOE_TILE
    nt = p // tm
    nj = D_FF // FF_CHUNK
    grid_spec = pltpu.PrefetchScalarGridSpec(
        num_scalar_prefetch=2,
        grid=(nt, nj),
        in_specs=[
            pl.BlockSpec((tm, d), lambda i, j, te, tv: (i, 0)),
            pl.BlockSpec((tm, 1), lambda i, j, te, tv: (i, 0)),
            pl.BlockSpec((None, d, FF_CHUNK), lambda i, j, te, tv: (te[i], 0, j)),
            pl.BlockSpec((None, d, FF_CHUNK), lambda i, j, te, tv: (te[i], 0, j)),
            pl.BlockSpec((None, FF_CHUNK, d), lambda i, j, te, tv: (te[i], j, 0)),
        ],
        out_specs=pl.BlockSpec((tm, d), lambda i, j, te, tv: (i, 0)),
        scratch_shapes=[pltpu.VMEM((tm, d), F32)],
    )
    return pl.pallas_call(
        _moe_kernel,
        grid_spec=grid_spec,
        out_shape=jax.ShapeDtypeStruct((p, d), F32),
        compiler_params=pltpu.CompilerParams(
            dimension_semantics=("parallel", "arbitrary"), vmem_limit_bytes=VMEM_LIMIT),
        name="moe_ffn",
    )(tile_expert, tile_valid, xs, gate_sorted, wg, wu, wd)


def _residual_kernel(h_ref, f_ref, mod_ref, o_ref):
    o_ref[...] = h_ref[...] + mod_ref[5:6, :] * f_ref[...]


def gated_residual(h, f, mod):
    b, l, d = h.shape
    tm = ROW_TILE
    row = lambda i, t: (i, t, 0)
    return pl.pallas_call(
        _residual_kernel,
        grid=(b, l // tm),
        in_specs=[pl.BlockSpec((None, tm, d), row), pl.BlockSpec((None, tm, d), row), _mod_spec(d)],
        out_specs=pl.BlockSpec((None, tm, d), row),
        out_shape=jax.ShapeDtypeStruct((b, l, d), F32),
        compiler_params=pltpu.CompilerParams(dimension_semantics=("parallel", "parallel")),
        name="gated_residual",
    )(h, f, mod)


def moe_ffn(v, logits, h, mod, wg, wu, wd):
    b, l, d = h.shape
    t = b * l
    tm = MOE_TILE
    p = TOP_K * t + N_EXPERTS * tm
    nt = p // tm
    lg = logits.reshape(t, LANES)[:, :N_EXPERTS]
    top_val, top_idx = lax.top_k(lg, TOP_K)
    gates = jax.nn.softmax(top_val, axis=-1)
    e_flat = top_idx.reshape(-1).astype(jnp.int32)
    onehot = (e_flat[:, None] == jnp.arange(N_EXPERTS, dtype=jnp.int32)[None, :]).astype(jnp.int32)
    rank = jnp.sum((jnp.cumsum(onehot, axis=0) - 1) * onehot, axis=1)
    counts = jnp.sum(onehot, axis=0)
    padded = ((counts + tm - 1) // tm) * tm
    ends = jnp.cumsum(padded)
    starts = ends - padded
    dest = starts[e_flat] + rank
    token = jnp.arange(TOP_K * t, dtype=jnp.int32) // TOP_K
    src = jnp.zeros((p,), jnp.int32).at[dest].set(token)
    gate_sorted = jnp.zeros((p,), F32).at[dest].set(gates.reshape(-1))
    tile_start = jnp.arange(nt, dtype=jnp.int32) * tm
    tile_expert = jnp.minimum(jnp.searchsorted(ends, tile_start, side="right"), N_EXPERTS - 1).astype(jnp.int32)
    tile_valid = (tile_start < ends[-1]).astype(jnp.int32)
    xs = jnp.take(v.reshape(t, d), src, axis=0)
    ys = moe_grouped_ffn(xs, gate_sorted[:, None], tile_expert, tile_valid, wg, wu, wd)
    dest2 = dest.reshape(t, TOP_K)
    f = jnp.take(ys, dest2[:, 0], axis=0) + jnp.take(ys, dest2[:, 1], axis=0)
    return gated_residual(h, f.reshape(b, l, d), mod)


def _final_norm_kernel(h_ref, g_ref, o_ref):
    o_ref[...] = _rms(h_ref[...]) * g_ref[...]


def final_rms_norm(h, gain, n_ctx_tiles):
    b, l, d = h.shape
    tm = ROW_TILE
    n = l - n_ctx_tiles * tm
    return pl.pallas_call(
        _final_norm_kernel,
        grid=(b, n // tm),
        in_specs=[
            pl.BlockSpec((None, tm, d), lambda i, t: (i, t + n_ctx_tiles, 0)),
            pl.BlockSpec((1, d), lambda i, t: (0, 0)),
        ],
        out_specs=pl.BlockSpec((None, tm, d), lambda i, t: (i, t, 0)),
        out_shape=jax.ShapeDtypeStruct((b, n, d), F32),
        compiler_params=pltpu.CompilerParams(dimension_semantics=("parallel", "parallel")),
        name="final_norm",
    )(h, gain)


def rms_norm(x, g):
    xf = x.astype(F32)
    y = xf * lax.rsqrt(jnp.mean(xf * xf, axis=-1, keepdims=True) + EPS)
    return (y * g.astype(F32)).astype(x.dtype)


def l2_normalize(t):
    return t * lax.rsqrt(jnp.sum(t * t, axis=-1, keepdims=True) + EPS)


def directional_scan(chunked_fn, seqs, consts, s0, reverse):
    if reverse:
        seqs = tuple(jnp.flip(t, axis=1) for t in seqs)
    o, s = chunked_fn(*seqs, *consts, s0)
    if reverse:
        o = jnp.flip(o, axis=1)
    return o, s


def prefix_bidirectional_scan(chunked_fn, seqs_c, seqs_x, consts, state_shape):
    out_c, out_x = 0.0, 0.0
    for d in range(2):
        rev = d == 1
        s0 = jnp.zeros(state_shape, F32)
        oc, s_ctx = directional_scan(chunked_fn, seqs_c[d], consts[d], s0, rev)
        ox, _ = directional_scan(chunked_fn, seqs_x[d], consts[d], s_ctx, rev)
        out_c = out_c + oc
        out_x = out_x + ox
    return out_c, out_x


def short_conv(x, w):
    k = w.shape[0]
    y = lax.conv_general_dilated(
        x, w[:, None, :].astype(x.dtype), window_strides=(1,), padding=[(k // 2, k // 2)],
        dimension_numbers=('NWC', 'WIO', 'NWC'), feature_group_count=x.shape[-1])
    return jax.nn.silu(y)


def gated_delta_chunked(q, k, v, g, beta, s0):
    b, l, h, _ = q.shape
    dv = v.shape[-1]
    cs = GDN_CHUNK
    nc = l // cs

    def chunks(t):
        return t.reshape(b, nc, cs, h, -1).transpose(1, 0, 3, 2, 4)

    qc, kc, vc = chunks(q), chunks(k), chunks(v)
    gc = jnp.cumsum(chunks(g[..., None])[..., 0], axis=-1)
    bc = chunks(beta[..., None])
    idx = jnp.arange(cs)
    lower = idx[:, None] >= idx[None, :]
    strict = idx[:, None] > idx[None, :]
    decay = jnp.exp(jnp.where(lower, gc[..., :, None] - gc[..., None, :], NEG_INF))
    kb = kc * bc
    lmat = jnp.where(strict, jnp.einsum('nbhid,nbhjd->nbhij', kb, kc) * decay, 0.0)
    a_mat = lmat + jnp.eye(cs, dtype=F32)
    u = lax.linalg.triangular_solve(a_mat, vc * bc, left_side=True, lower=True)
    w = lax.linalg.triangular_solve(a_mat, kb * jnp.exp(gc)[..., None], left_side=True, lower=True)
    attn = jnp.einsum('nbhid,nbhjd->nbhij', qc, kc) * decay

    def step(s, xs):
        q_i, k_i, u_i, w_i, a_i, g_i = xs
        v_new = u_i - jnp.einsum('bhck,bhkv->bhcv', w_i, s)
        o_i = (jnp.einsum('bhck,bhkv->bhcv', q_i * jnp.exp(g_i)[..., None], s)
               + jnp.einsum('bhij,bhjv->bhiv', a_i, v_new))
        g_last = g_i[..., -1:]
        s = (s * jnp.exp(g_last)[..., None]
             + jnp.einsum('bhck,bhcv->bhkv', k_i * jnp.exp(g_last - g_i)[..., None], v_new))
        return s, o_i

    s_fin, o = lax.scan(step, s0, (qc, kc, u, w, attn, gc))
    return o.transpose(1, 0, 3, 2, 4).reshape(b, l, h, dv), s_fin


def gdn_mixer(pb, conv_w, a_log, dt_bias, norm_g):
    def prep(qkv, ab):
        b, l, _ = qkv.shape
        qkv = short_conv(qkv, conv_w).astype(F32)
        q, k, v = jnp.split(qkv, [GDN_HEADS * GDN_DK, 2 * GDN_HEADS * GDN_DK], axis=-1)
        q = l2_normalize(q.reshape(b, l, GDN_HEADS, GDN_DK)) * GDN_DK ** -0.5
        k = l2_normalize(k.reshape(b, l, GDN_HEADS, GDN_DK))
        v = v.reshape(b, l, GDN_HEADS, GDN_DV)
        ab = ab.astype(F32).reshape(b, l, 2, 2, GDN_HEADS)
        g = -jnp.exp(a_log.astype(F32)) * jax.nn.softplus(ab[:, :, :, 0] + dt_bias.astype(F32))
        beta = jax.nn.sigmoid(ab[:, :, :, 1])
        return [(q, k, v, g[:, :, d], beta[:, :, d]) for d in range(2)]

    def parts(rows):
        return rows[..., :768], rows[..., 768:1024], rows[..., 1024:1040]

    qkv_c, gate_c, ab_c = parts(pb[:, :CTX_LEN])
    qkv_x, gate_x, ab_x = parts(pb[:, CTX_LEN:])
    s_shape = (qkv_x.shape[0], GDN_HEADS, GDN_DK, GDN_DV)
    o_c, o_x = prefix_bidirectional_scan(gated_delta_chunked, prep(qkv_c, ab_c), prep(qkv_x, ab_x),
                                         [(), ()], s_shape)

    def gated_out(o, gate):
        b, l = gate.shape[:2]
        y = rms_norm(o, norm_g) * jax.nn.silu(gate.astype(F32)).reshape(o.shape)
        return y.reshape(b, l, GDN_HEADS * GDN_DV).astype(BF16)

    return jnp.concatenate([gated_out(o_c, gate_c), gated_out(o_x, gate_x)], axis=1)


def kernel(x, c, ctx, c_ctx, w_mod, b_mod, norm1, norm2, w_in, w_out, swa_sink, gdn_conv, gdn_a_log, gdn_dt_bias, gdn_norm, mla_q_norm, mla_kv_norm, mla_w_q_up, mla_w_kv_up, ret_log_decay, ret_norm, ffn_w_gate, ffn_w_up, ffn_w_down, moe_router, moe_w_gate, moe_w_up, moe_w_down, final_norm):
    b, n, d = x.shape
    depth = w_in.shape[0]
    cos_t, sin_t = rope_tables(n)
    cos_c, sin_c = cos_t[:, A_ROT_W:A_ROT_W + C_ROT_W], sin_t[:, A_ROT_W:A_ROT_W + C_ROT_W]
    silu_c = jax.nn.silu(c)
    silu_cc = jax.nn.silu(c_ctx)
    h = jnp.concatenate([ctx, x], axis=1)
    for layer in range(depth):
        mod_x = jnp.dot(silu_c, w_mod[layer], precision=lax.Precision.HIGHEST) + b_mod[layer]
        mod_c = jnp.dot(silu_cc, w_mod[layer], precision=lax.Precision.HIGHEST) + b_mod[layer]
        mod = jnp.stack([jnp.broadcast_to(mod_c, mod_x.shape), mod_x], axis=1).reshape(b, 2, 6, d)
        mod = jnp.pad(mod, ((0, 0), (0, 0), (0, SUBLANES - 6), (0, 0)))
        pa, pb, pc, pd = norm_proj(h, mod, norm1[layer][None, :], build_in_weight(w_in[layer]), cos_t, sin_t)
        mix_a = swa_mixer(pa, swa_sink[layer])
        mix_b = gdn_mixer(pb, gdn_conv[layer], gdn_a_log[layer], gdn_dt_bias[layer], gdn_norm[layer])
        mq, mk, mv = mla_prep(pc, mla_q_norm[layer], mla_kv_norm[layer], mla_w_q_up[layer], mla_w_kv_up[layer],
                              cos_c, sin_c)
        mix_c = mla_attention(mq, mk, mv)
        mix_d = retention_mixer(pd, ret_log_decay[layer], ret_norm[layer])
        mixes = (mix_a, mix_b, mix_c, mix_d)
        w_o = build_out_weight(w_out[layer])
        i = layer // 2
        if layer % 2 == 0:
            h, v = out_proj(mixes, h, mod, norm2[layer][None, :], w_o)
            h = dense_ffn(v, h, mod, ffn_w_gate[i].astype(BF16), ffn_w_up[i].astype(BF16),
                          ffn_w_down[i].astype(BF16))
        else:
            w_r = jnp.pad(moe_router[i], ((0, 0), (0, LANES - N_EXPERTS)))
            h, v, logits = out_proj(mixes, h, mod, norm2[layer][None, :], w_o, w_r)
            h = moe_ffn(v, logits, h, mod, moe_w_gate[i].astype(BF16), moe_w_up[i].astype(BF16),
                        moe_w_down[i].astype(BF16))
    return final_rms_norm(h, final_norm[None, :], CTX_LEN // ROW_TILE)
```

```python
import functools

import numpy as np
import jax
import jax.numpy as jnp
from jax import lax
from jax.experimental import pallas as pl
from jax.experimental.pallas import tpu as pltpu

D_MODEL = 1024
GRID_W = 64
CTX_LEN = 256
HEAD_DIM = 64
ROPE_THETA = 10000.0
EPS = 1e-6
NEG_INF = -1e30

SWA_WINDOW = 128
GDN_HEADS = 4
GDN_DK = 64
GDN_DV = 64
GDN_CHUNK = 64
MLA_HEADS = 4
MLA_NOPE = 64
MLA_ROPE = 32
MLA_V = 64
RET_HEADS = 4
RET_DK = 64
D_FF = 3584
N_EXPERTS = 8
TOP_K = 2

LANES = 128
SUBLANES = 8
VMEM_LIMIT = 56 * 1024 * 1024

ROW_TILE = 256
FF_CHUNK = 512
MOE_TILE = 512

A_W, B_W, C_W, D_W = 768, 1152, 512, 1024
A_ROT_W, C_ROT_W, D_ROT_W = 640, 128, 512
OFF_A = 0
OFF_B = OFF_A + A_W
OFF_C = OFF_B + B_W
OFF_D = OFF_C + C_W
OFF_AR = OFF_D + D_W
OFF_CR = OFF_AR + A_ROT_W
OFF_DR = OFF_CR + C_ROT_W
W_ALL = OFF_DR + D_ROT_W
ROPE_W = A_ROT_W + C_ROT_W + D_ROT_W

F32 = jnp.float32
BF16 = jnp.bfloat16
NT_DIMS = (((1,), (1,)), ((), ()))
TN_DIMS = (((0,), (0,)), ((), ()))


def _rms(x):
    return x * lax.rsqrt(jnp.mean(x * x, axis=-1, keepdims=True) + EPS)


def _silu(x):
    return x * (1.0 / (1.0 + jnp.exp(-x)))


def _dot(a, b):
    return jnp.dot(a, b, preferred_element_type=F32)


def _dot_nt(a, b):
    return lax.dot_general(a, b, NT_DIMS, preferred_element_type=F32)


def _dot_tn(a, b):
    return lax.dot_general(a, b, TN_DIMS, preferred_element_type=F32)


def _mod_spec(d):
    return pl.BlockSpec((None, None, SUBLANES, d), lambda i, t: (i, jnp.minimum(t, 1), 0, 0))


def _norm_proj_kernel(h_ref, mod_ref, g_ref, w_ref, cos_ref, sin_ref, a_ref, b_ref, c_ref, d_ref):
    x = h_ref[...]
    u = (_rms(x) * g_ref[...] * (1.0 + mod_ref[1:2, :]) + mod_ref[0:1, :]).astype(BF16)

    def mm(lo, width):
        return _dot(u, w_ref[:, lo:lo + width])

    a_main = mm(OFF_A, A_W)
    a_rot = mm(OFF_AR, A_ROT_W)
    a_ref[:, :A_ROT_W] = (a_main[:, :A_ROT_W] * cos_ref[:, :A_ROT_W] + a_rot * sin_ref[:, :A_ROT_W]).astype(BF16)
    a_ref[:, A_ROT_W:] = a_main[:, A_ROT_W:].astype(BF16)
    b_ref[...] = mm(OFF_B, B_W)
    c_main = mm(OFF_C, C_W)
    c_rot = mm(OFF_CR, C_ROT_W)
    lo, hi = A_ROT_W, A_ROT_W + C_ROT_W
    c_ref[:, :C_W - C_ROT_W] = c_main[:, :C_W - C_ROT_W]
    c_ref[:, C_W - C_ROT_W:] = c_main[:, C_W - C_ROT_W:] * cos_ref[:, lo:hi] + c_rot * sin_ref[:, lo:hi]
    d_main = mm(OFF_D, D_W)
    d_rot = mm(OFF_DR, D_ROT_W)
    d_ref[:, :D_ROT_W] = d_main[:, :D_ROT_W] * cos_ref[:, hi:] + d_rot * sin_ref[:, hi:]
    d_ref[:, D_ROT_W:] = d_main[:, D_ROT_W:]


def norm_proj(h, mod, gain, w, cos_t, sin_t):
    b, l, d = h.shape
    tm = ROW_TILE
    row = lambda i, t: (i, t, 0)
    return pl.pallas_call(
        _norm_proj_kernel,
        grid=(b, l // tm),
        in_specs=[
            pl.BlockSpec((None, tm, d), row),
            _mod_spec(d),
            pl.BlockSpec((1, d), lambda i, t: (0, 0)),
            pl.BlockSpec((d, W_ALL), lambda i, t: (0, 0), pipeline_mode=pl.Buffered(1)),
            pl.BlockSpec((tm, ROPE_W), lambda i, t: (t, 0)),
            pl.BlockSpec((tm, ROPE_W), lambda i, t: (t, 0)),
        ],
        out_specs=[pl.BlockSpec((None, tm, A_W), row), pl.BlockSpec((None, tm, B_W), row),
                   pl.BlockSpec((None, tm, C_W), row), pl.BlockSpec((None, tm, D_W), row)],
        out_shape=[jax.ShapeDtypeStruct((b, l, A_W), BF16), jax.ShapeDtypeStruct((b, l, B_W), F32),
                   jax.ShapeDtypeStruct((b, l, C_W), F32), jax.ShapeDtypeStruct((b, l, D_W), F32)],
        compiler_params=pltpu.CompilerParams(
            dimension_semantics=("parallel", "parallel"), vmem_limit_bytes=VMEM_LIMIT),
        name="norm_proj",
    )(h, mod, gain, w, cos_t, sin_t)


def _rot_cols(w, hd):
    x = w.reshape(w.shape[:-1] + (w.shape[-1] // hd, 4, hd // 4))
    x1, x2, x3, x4 = x[..., 0, :], x[..., 1, :], x[..., 2, :], x[..., 3, :]
    return jnp.stack([-x2, x1, -x4, x3], axis=-2).reshape(w.shape)


def _place_swa_q(q):
    z = jnp.zeros((q.shape[0], HEAD_DIM), q.dtype)
    blocks = []
    for h in range(4):
        qh = q[:, HEAD_DIM * h:HEAD_DIM * (h + 1)]
        blocks += [qh, z] if h // 2 == 0 else [z, qh]
    return jnp.concatenate(blocks, axis=1)


def build_in_weight(w):
    d = w.shape[0]
    o = [int(v) for v in np.cumsum((256, 128, 128, 768, 256, 16, 256, 128, 32, 256, 256, 256, 256))]
    aq, ak, av = w[:, :o[0]] * HEAD_DIM ** -0.5, w[:, o[0]:o[1]], w[:, o[1]:o[2]]
    b_main, b_ab = w[:, o[2]:o[4]], w[:, o[4]:o[5]]
    c_q, c_kv, c_kr = w[:, o[5]:o[6]], w[:, o[6]:o[7]], w[:, o[7]:o[8]]
    dq, dk, dvg = w[:, o[8]:o[9]], w[:, o[9]:o[10]] * RET_DK ** -0.5, w[:, o[10]:]
    z = lambda n: jnp.zeros((d, n), w.dtype)
    parts = [
        _place_swa_q(aq), ak, av,
        b_main, b_ab, z(LANES - b_ab.shape[1]),
        c_q, c_kv, z(64), c_kr, z(32),
        dq, dk, dvg,
        _place_swa_q(_rot_cols(aq, HEAD_DIM)), _rot_cols(ak, HEAD_DIM),
        z(64), _rot_cols(c_kr, MLA_ROPE), z(32),
        _rot_cols(dq, HEAD_DIM), _rot_cols(dk, HEAD_DIM),
    ]
    out = jnp.concatenate(parts, axis=1)
    assert out.shape[1] == W_ALL
    return out.astype(BF16)


def rope_tables(n):
    def axial(rot_dim):
        n_freq = rot_dim // 4
        inv_freq = ROPE_THETA ** (-jnp.arange(n_freq, dtype=F32) / n_freq)
        row = jnp.repeat(jnp.arange(n // GRID_W, dtype=F32), GRID_W)
        col = jnp.tile(jnp.arange(GRID_W, dtype=F32), n // GRID_W)
        ang_r = row[:, None] * inv_freq
        ang_c = col[:, None] * inv_freq
        ang = jnp.concatenate([ang_r, ang_r, ang_c, ang_c], axis=-1)
        return jnp.cos(ang), jnp.sin(ang)

    cos_h, sin_h = axial(HEAD_DIM)
    cos_r, sin_r = axial(MLA_ROPE)
    one, zero = jnp.ones((n, 1), F32), jnp.zeros((n, 1), F32)
    cos_c = jnp.concatenate([jnp.tile(one, (1, 64)), cos_r, jnp.tile(one, (1, 32))], axis=1)
    sin_c = jnp.concatenate([jnp.tile(zero, (1, 64)), sin_r, jnp.tile(zero, (1, 32))], axis=1)
    cos_t = jnp.concatenate([jnp.tile(cos_h, (1, A_ROT_W // HEAD_DIM)), cos_c,
                             jnp.tile(cos_h, (1, D_ROT_W // HEAD_DIM))], axis=1)
    sin_t = jnp.concatenate([jnp.tile(sin_h, (1, A_ROT_W // HEAD_DIM)), sin_c,
                             jnp.tile(sin_h, (1, D_ROT_W // HEAD_DIM))], axis=1)
    cos_t = jnp.concatenate([jnp.ones((CTX_LEN, ROPE_W), F32), cos_t], axis=0)
    sin_t = jnp.concatenate([jnp.zeros((CTX_LEN, ROPE_W), F32), sin_t], axis=0)
    return cos_t, sin_t


def _swa_kernel(sink_ref, q_ref, kp_ref, ko_ref, kn_ref, kc_ref, vp_ref, vo_ref, vn_ref, vc_ref, o_ref):
    t = pl.program_id(1)
    last = pl.num_programs(1) - 1
    tq = q_ref.shape[0]
    half = tq // 2

    def head_out(h, pieces):
        q = q_ref[:, LANES * h:LANES * (h + 1)]
        ss = []
        for k_ref, _, mask in pieces:
            s = _dot_nt(q, k_ref[...])
            ss.append(s if mask is None else jnp.where(mask, s, NEG_INF))
        sink = sink_ref[h]
        m = jnp.maximum(functools.reduce(jnp.maximum, [s.max(axis=-1, keepdims=True) for s in ss]), sink)
        ps = [jnp.exp(s - m) for s in ss]
        denom = functools.reduce(jnp.add, [p.sum(axis=-1, keepdims=True) for p in ps]) + jnp.exp(sink - m)
        o = functools.reduce(jnp.add, [_dot(p.astype(BF16), piece[1][...]) for p, piece in zip(ps, pieces)])
        return o / denom

    def write(pieces):
        outs = [head_out(h, pieces) for h in range(4)]
        lane = lax.broadcasted_iota(jnp.int32, (tq, LANES), 1)
        for r in range(2):
            o_ref[:, LANES * r:LANES * (r + 1)] = jnp.where(lane < HEAD_DIM, outs[r], outs[2 + r]).astype(o_ref.dtype)

    @pl.when(t == 0)
    def _():
        write([(kc_ref, vc_ref, None)])

    @pl.when(t > 0)
    def _():
        qi = lax.broadcasted_iota(jnp.int32, (tq, half), 0)
        kj = lax.broadcasted_iota(jnp.int32, (tq, half), 1)
        mask_prev = (kj >= qi) & (t > 1)
        mask_next = (kj <= qi - half) & (t < last)
        qo = lax.broadcasted_iota(jnp.int32, (tq, tq), 0)
        ko = lax.broadcasted_iota(jnp.int32, (tq, tq), 1)
        mask_own = jnp.abs(qo - ko) <= SWA_WINDOW
        write([(kp_ref, vp_ref, mask_prev), (ko_ref, vo_ref, mask_own), (kn_ref, vn_ref, mask_next),
               (kc_ref, vc_ref, None)])


def swa_mixer(pa, sink):
    b, l, _ = pa.shape
    tq = ROW_TILE
    nblk = l // SWA_WINDOW
    kcol, vcol = 4, 5
    prev = lambda c: (lambda i, t: (i, jnp.maximum(2 * t - 1, 2), c))
    nxt = lambda c: (lambda i, t: (i, jnp.minimum(2 * t + 2, nblk - 1), c))
    own = lambda c: (lambda i, t: (i, t, c))
    ctx = lambda c: (lambda i, t: (i, 0, c))
    kv_specs = lambda c: [pl.BlockSpec((None, SWA_WINDOW, LANES), prev(c)), pl.BlockSpec((None, tq, LANES), own(c)),
                          pl.BlockSpec((None, SWA_WINDOW, LANES), nxt(c)), pl.BlockSpec((None, tq, LANES), ctx(c))]
    return pl.pallas_call(
        _swa_kernel,
        grid=(b, l // tq),
        in_specs=[pl.BlockSpec(memory_space=pltpu.SMEM),
                  pl.BlockSpec((None, tq, 4 * LANES), lambda i, t: (i, t, 0))] + kv_specs(kcol) + kv_specs(vcol),
        out_specs=pl.BlockSpec((None, tq, 2 * LANES), lambda i, t: (i, t, 0)),
        out_shape=jax.ShapeDtypeStruct((b, l, 2 * LANES), BF16),
        compiler_params=pltpu.CompilerParams(
            dimension_semantics=("parallel", "parallel"), vmem_limit_bytes=VMEM_LIMIT),
        name="swa",
    )(sink, pa, pa, pa, pa, pa, pa, pa, pa, pa)


def _mla_prep_kernel(c_ref, qn_ref, kvn_ref, wq_ref, wqr_ref, wk_ref, wv_ref, cos_ref, sin_ref,
                     q_ref, k_ref, v_ref):
    cq = c_ref[:, 0:256]
    ckv = c_ref[:, 256:384]
    kr = c_ref[:, 384:512]
    nq = (_rms(cq) * qn_ref[...]).astype(BF16)
    nkv = (_rms(ckv) * kvn_ref[...]).astype(BF16)
    cos = jnp.concatenate([cos_ref[...]] * MLA_HEADS, axis=1)
    sin = jnp.concatenate([sin_ref[...]] * MLA_HEADS, axis=1)
    q_ref[...] = (_dot(nq, wq_ref[...]) * cos + _dot(nq, wqr_ref[...]) * sin).astype(BF16)
    k_ref[...] = (_dot(nkv, wk_ref[...]) + jnp.concatenate([kr] * MLA_HEADS, axis=1)).astype(BF16)
    v_ref[...] = _dot(nkv, wv_ref[...]).astype(BF16)


def mla_prep(pc, q_norm, kv_norm, w_q_up, w_kv_up, cos_c, sin_c):
    b, l, _ = pc.shape
    tm = ROW_TILE
    scale = (MLA_NOPE + MLA_ROPE) ** -0.5
    wq = (w_q_up * scale).reshape(-1, MLA_HEADS, MLA_NOPE + MLA_ROPE)
    zq = jnp.zeros(wq.shape[:2] + (LANES - MLA_NOPE - MLA_ROPE,), F32)
    wq_main = jnp.concatenate([wq, zq], axis=-1).reshape(-1, MLA_HEADS * LANES)
    wq_rot = jnp.concatenate([jnp.zeros_like(wq[..., :MLA_NOPE]), _rot_cols(wq[..., MLA_NOPE:], MLA_ROPE), zq],
                             axis=-1).reshape(-1, MLA_HEADS * LANES)
    wkv = w_kv_up.reshape(-1, MLA_HEADS, MLA_NOPE + MLA_V)
    wk = jnp.concatenate([wkv[..., :MLA_NOPE], jnp.zeros_like(wkv[..., :LANES - MLA_NOPE])],
                         axis=-1).reshape(-1, MLA_HEADS * LANES)
    wv = wkv[..., MLA_NOPE:].reshape(-1, MLA_HEADS * MLA_V)
    row = lambda i, t: (i, t, 0)
    const = lambda i, t: (0, 0)
    full = lambda a: pl.BlockSpec(a.shape, const)
    args = [q_norm[None, :], kv_norm[None, :], wq_main.astype(BF16), wq_rot.astype(BF16), wk.astype(BF16),
            wv.astype(BF16)]
    return pl.pallas_call(
        _mla_prep_kernel,
        grid=(b, l // tm),
        in_specs=[pl.BlockSpec((None, tm, C_W), row)] + [full(a) for a in args]
        + [pl.BlockSpec((tm, LANES), lambda i, t: (t, 0)), pl.BlockSpec((tm, LANES), lambda i, t: (t, 0))],
        out_specs=[pl.BlockSpec((None, tm, 4 * LANES), row), pl.BlockSpec((None, tm, 4 * LANES), row),
                   pl.BlockSpec((None, tm, 2 * LANES), row)],
        out_shape=[jax.ShapeDtypeStruct((b, l, 4 * LANES), BF16), jax.ShapeDtypeStruct((b, l, 4 * LANES), BF16),
                   jax.ShapeDtypeStruct((b, l, 2 * LANES), BF16)],
        compiler_params=pltpu.CompilerParams(dimension_semantics=("parallel", "parallel")),
        name="mla_prep",
    )(pc, *args, cos_c, sin_c)


def _mla_attn_kernel(q_ref, k_ref, v_ref, o_ref):
    t = pl.program_id(2)
    tq = q_ref.shape[0]

    def attend(nk):
        outs = []
        for j in range(2):
            q = q_ref[:, LANES * j:LANES * (j + 1)]
            k = k_ref[0:nk, LANES * j:LANES * (j + 1)]
            s = _dot_nt(q, k)
            p = jnp.exp(s - s.max(axis=-1, keepdims=True))
            denom = p.sum(axis=-1, keepdims=True)
            outs.append(_dot(p.astype(BF16), v_ref[0:nk, :]) / denom)
        lane = lax.broadcasted_iota(jnp.int32, (tq, LANES), 1)
        o_ref[...] = jnp.where(lane < MLA_V, outs[0], outs[1]).astype(o_ref.dtype)

    @pl.when(t == 0)
    def _():
        attend(CTX_LEN)

    @pl.when(t > 0)
    def _():
        attend(k_ref.shape[0])


def mla_attention(q, k, v):
    b, l, _ = q.shape
    tq = ROW_TILE
    return pl.pallas_call(
        _mla_attn_kernel,
        grid=(b, 2, l // tq),
        in_specs=[pl.BlockSpec((None, tq, 2 * LANES), lambda i, p, t: (i, t, p)),
                  pl.BlockSpec((None, l, 2 * LANES), lambda i, p, t: (i, 0, p)),
                  pl.BlockSpec((None, l, LANES), lambda i, p, t: (i, 0, p))],
        out_specs=pl.BlockSpec((None, tq, LANES), lambda i, p, t: (i, t, p)),
        out_shape=jax.ShapeDtypeStruct((b, l, 2 * LANES), BF16),
        compiler_params=pltpu.CompilerParams(
            dimension_semantics=("parallel", "parallel", "parallel"), vmem_limit_bytes=VMEM_LIMIT),
        name="mla_attn",
    )(q, k, v)


def _head_mean(x, ones_bd):
    hi = x.astype(BF16)
    lo = (x - hi.astype(F32)).astype(BF16)
    return (_dot(hi, ones_bd) + _dot(lo, ones_bd)) * (1.0 / HEAD_DIM)


def _ret_kernel(x_ref, lg_ref, g_ref, o_ref, s_ref, dec_ref):
    dr = pl.program_id(1)
    s = pl.program_id(2)
    ns = pl.num_programs(2)
    c = x_ref.shape[0]
    w = RET_HEADS * HEAD_DIM
    chunk = jnp.where(s == 0, 0, jnp.where(dr == 0, s, ns - s))
    lg = lg_ref[...]
    fwd = dr == 0
    row_h = lax.broadcasted_iota(jnp.int32, (w, w), 0) // HEAD_DIM
    col_h = lax.broadcasted_iota(jnp.int32, (w, w), 1) // HEAD_DIM
    same_head = row_h == col_h

    @pl.when(s == 0)
    def _():
        s_ref[...] = jnp.zeros_like(s_ref)
        i = lax.broadcasted_iota(jnp.int32, (c, c), 0)
        j = lax.broadcasted_iota(jnp.int32, (c, c), 1)
        rel = jnp.where(fwd, i - j, j - i)
        relf = jnp.maximum(rel, 0).astype(F32)
        for h in range(RET_HEADS):
            lg_h = lg_ref[0:1, HEAD_DIM * h:HEAD_DIM * h + 1]
            dec_ref[h] = jnp.where(rel >= 0, jnp.exp(lg_h * relf), 0.0)

    q = x_ref[:, 0:w]
    k = x_ref[:, w:2 * w]
    v = x_ref[:, 2 * w:3 * w].astype(BF16)
    kb = k.astype(BF16)
    pos = lax.broadcasted_iota(jnp.int32, (c, 1), 0).astype(F32)
    q_pow = jnp.where(fwd, pos + 1.0, c - pos)
    k_pow = jnp.where(fwd, c - 1.0 - pos, pos)
    lane_h = lax.broadcasted_iota(jnp.int32, (c, w), 1) // HEAD_DIM
    acc = _dot((q * jnp.exp(lg * q_pow)).astype(BF16), s_ref[...].astype(BF16))
    for h in range(RET_HEADS):
        qh = jnp.where(lane_h == h, q, 0.0).astype(BF16)
        a = _dot_nt(qh, kb) * dec_ref[h]
        acc = acc + jnp.where(lane_h == h, _dot(a.astype(BF16), v), 0.0)
    kv = _dot_tn((k * jnp.exp(lg * k_pow)).astype(BF16), v)
    s_ref[...] = s_ref[...] * jnp.exp(lg * float(c)) + jnp.where(same_head, kv, 0.0)

    rows = pl.ds(pl.multiple_of(chunk * c, c), c)

    @pl.when(dr == 0)
    def _():
        o_ref[rows, :] = acc

    @pl.when(dr == 1)
    def _():
        o = o_ref[rows, :] + acc
        ones_bd = jnp.where(same_head, 1.0, 0.0).astype(BF16)
        mu = _head_mean(o, ones_bd)
        var = _head_mean(jnp.square(o - mu), ones_bd)
        y = (o - mu) * lax.rsqrt(var + EPS) * g_ref[...]
        o_ref[rows, :] = y * _silu(x_ref[:, 3 * w:4 * w])


def retention_mixer(pd, log_decay, norm_g):
    b, l, _ = pd.shape
    c = ROW_TILE
    ns = l // c
    w = RET_HEADS * HEAD_DIM
    lg = jnp.repeat(-jnp.exp(log_decay.astype(F32)), HEAD_DIM, axis=-1)[:, None, :]

    def chunk_of(dr, s):
        return jnp.where(s == 0, 0, jnp.where(dr == 0, s, ns - s))

    return pl.pallas_call(
        _ret_kernel,
        grid=(b, 2, ns),
        in_specs=[pl.BlockSpec((None, c, D_W), lambda i, dr, s: (i, chunk_of(dr, s), 0)),
                  pl.BlockSpec((None, 1, w), lambda i, dr, s: (dr, 0, 0)),
                  pl.BlockSpec((1, w), lambda i, dr, s: (0, 0))],
        out_specs=pl.BlockSpec((None, l, w), lambda i, dr, s: (i, 0, 0)),
        out_shape=jax.ShapeDtypeStruct((b, l, w), F32),
        scratch_shapes=[pltpu.VMEM((w, w), F32), pltpu.VMEM((RET_HEADS, c, c), F32)],
        compiler_params=pltpu.CompilerParams(
            dimension_semantics=("parallel", "arbitrary", "arbitrary"), vmem_limit_bytes=VMEM_LIMIT),
        name="retention",
    )(pd, lg, norm_g[None, :])


def _out_proj_kernel(ma_ref, mb_ref, mc_ref, md_ref, h_ref, mod_ref, g_ref, w_ref, *rest, with_router):
    if with_router:
        wr_ref, hn_ref, v_ref, lg_ref = rest
    else:
        hn_ref, v_ref = rest
    gw = 2 * LANES
    mix = functools.reduce(jnp.add, [
        _dot(m_ref[...].astype(BF16), w_ref[gw * i:gw * (i + 1), :])
        for i, m_ref in enumerate((ma_ref, mb_ref, mc_ref, md_ref))])
    hn = h_ref[...] + mod_ref[2:3, :] * mix
    hn_ref[...] = hn
    v = _rms(hn) * g_ref[...] * (1.0 + mod_ref[4:5, :]) + mod_ref[3:4, :]
    v_ref[...] = v.astype(BF16)
    if with_router:
        lg_ref[...] = jnp.dot(v, wr_ref[...], preferred_element_type=F32, precision=lax.Precision.HIGHEST)


def out_proj(mixes, h, mod, gain, w, w_router=None):
    b, l, d = h.shape
    tm = ROW_TILE
    with_router = w_router is not None
    row = lambda i, t: (i, t, 0)
    in_specs = [pl.BlockSpec((None, tm, 2 * LANES), row) for _ in mixes] + [
        pl.BlockSpec((None, tm, d), row),
        _mod_spec(d),
        pl.BlockSpec((1, d), lambda i, t: (0, 0)),
        pl.BlockSpec(w.shape, lambda i, t: (0, 0), pipeline_mode=pl.Buffered(1)),
    ]
    out_specs = [pl.BlockSpec((None, tm, d), row), pl.BlockSpec((None, tm, d), row)]
    out_shape = [jax.ShapeDtypeStruct((b, l, d), F32), jax.ShapeDtypeStruct((b, l, d), BF16)]
    args = list(mixes) + [h, mod, gain, w]
    if with_router:
        in_specs.append(pl.BlockSpec(w_router.shape, lambda i, t: (0, 0)))
        out_specs.append(pl.BlockSpec((None, tm, LANES), row))
        out_shape.append(jax.ShapeDtypeStruct((b, l, LANES), F32))
        args.append(w_router)
    return pl.pallas_call(
        functools.partial(_out_proj_kernel, with_router=with_router),
        grid=(b, l // tm),
        in_specs=in_specs,
        out_specs=out_specs,
        out_shape=out_shape,
        compiler_params=pltpu.CompilerParams(
            dimension_semantics=("parallel", "parallel"), vmem_limit_bytes=VMEM_LIMIT),
        name="out_proj",
    )(*args)


def build_out_weight(w):
    hd = HEAD_DIM
    perm = jnp.concatenate([w[0:hd], w[2 * hd:3 * hd], w[hd:2 * hd], w[3 * hd:4 * hd]], axis=0)
    return jnp.concatenate([perm, w[4 * hd:]], axis=0).astype(BF16)


def _ffn_kernel(v_ref, h_ref, mod_ref, wg_ref, wu_ref, wd_ref, o_ref):
    v = v_ref[...]
    acc = jnp.zeros(o_ref.shape, F32)
    for j in range(D_FF // FF_CHUNK):
        cols = slice(j * FF_CHUNK, (j + 1) * FF_CHUNK)
        a = _dot(v, wg_ref[:, cols])
        u = _dot(v, wu_ref[:, cols])
        mid = (_silu(a) * u).astype(BF16)
        acc = acc + _dot(mid, wd_ref[cols, :])
    o_ref[...] = h_ref[...] + mod_ref[5:6, :] * acc


def dense_ffn(v, h, mod, wg, wu, wd):
    b, l, d = h.shape
    tm = ROW_TILE
    row = lambda i, t: (i, t, 0)
    const = lambda i, t: (0, 0)
    return pl.pallas_call(
        _ffn_kernel,
        grid=(b, l // tm),
        in_specs=[
            pl.BlockSpec((None, tm, d), row),
            pl.BlockSpec((None, tm, d), row),
            _mod_spec(d),
            pl.BlockSpec(wg.shape, const, pipeline_mode=pl.Buffered(1)),
            pl.BlockSpec(wu.shape, const, pipeline_mode=pl.Buffered(1)),
            pl.BlockSpec(wd.shape, const, pipeline_mode=pl.Buffered(1)),
        ],
        out_specs=pl.BlockSpec((None, tm, d), row),
        out_shape=jax.ShapeDtypeStruct((b, l, d), F32),
        compiler_params=pltpu.CompilerParams(
            dimension_semantics=("parallel", "parallel"), vmem_limit_bytes=VMEM_LIMIT),
        name="dense_ffn",
    )(v, h, mod, wg, wu, wd)


def _moe_kernel(te_ref, tv_ref, x_ref, gate_ref, wg_ref, wu_ref, wd_ref, o_ref, acc_ref):
    i = pl.program_id(0)
    j = pl.program_id(1)
    nj = pl.num_programs(1)

    @pl.when(j == 0)
    def _():
        acc_ref[...] = jnp.zeros_like(acc_ref)

    @pl.when(tv_ref[i] > 0)
    def _():
        x = x_ref[...]
        a = _dot(x, wg_ref[...])
        u = _dot(x, wu_ref[...])
        mid = (_silu(a) * u).astype(BF16)
        acc_ref[...] += _dot(mid, wd_ref[...])

    @pl.when(j == nj - 1)
    def _():
        o_ref[...] = gate_ref[...] * acc_ref[...]


def moe_grouped_ffn(xs, gate_sorted, tile_expert, tile_valid, wg, wu, wd):
    p, d = xs.shape
    tm = MOE_TILE
    nt = p // tm
    nj = D_FF // FF_CHUNK
    grid_spec = pltpu.PrefetchScalarGridSpec(
        num_scalar_prefetch=2,
        grid=(nt, nj),
        in_specs=[
            pl.BlockSpec((tm, d), lambda i, j, te, tv: (i, 0)),
            pl.BlockSpec((tm, 1), lambda i, j, te, tv: (i, 0)),
            pl.BlockSpec((None, d, FF_CHUNK), lambda i, j, te, tv: (te[i], 0, j)),
            pl.BlockSpec((None, d, FF_CHUNK), lambda i, j, te, tv: (te[i], 0, j)),
            pl.BlockSpec((None, FF_CHUNK, d), lambda i, j, te, tv: (te[i], j, 0)),
        ],
        out_specs=pl.BlockSpec((tm, d), lambda i, j, te, tv: (i, 0)),
        scratch_shapes=[pltpu.VMEM((tm, d), F32)],
    )
    return pl.pallas_call(
        _moe_kernel,
        grid_spec=grid_spec,
        out_shape=jax.ShapeDtypeStruct((p, d), F32),
        compiler_params=pltpu.CompilerParams(
            dimension_semantics=("parallel", "arbitrary"), vmem_limit_bytes=VMEM_LIMIT),
        name="moe_ffn",
    )(tile_expert, tile_valid, xs, gate_sorted, wg, wu, wd)


def _residual_kernel(h_ref, f_ref, mod_ref, o_ref):
    o_ref[...] = h_ref[...] + mod_ref[5:6, :] * f_ref[...]


def gated_residual(h, f, mod):
    b, l, d = h.shape
    tm = ROW_TILE
    row = lambda i, t: (i, t, 0)
    return pl.pallas_call(
        _residual_kernel,
        grid=(b, l // tm),
        in_specs=[pl.BlockSpec((None, tm, d), row), pl.BlockSpec((None, tm, d), row), _mod_spec(d)],
        out_specs=pl.BlockSpec((None, tm, d), row),
        out_shape=jax.ShapeDtypeStruct((b, l, d), F32),
        compiler_params=pltpu.CompilerParams(dimension_semantics=("parallel", "parallel")),
        name="gated_residual",
    )(h, f, mod)


def moe_ffn(v, logits, h, mod, wg, wu, wd):
    b, l, d = h.shape
    t = b * l
    tm = MOE_TILE
    p = TOP_K * t + N_EXPERTS * tm
    nt = p // tm
    lg = logits.reshape(t, LANES)[:, :N_EXPERTS]
    top_val, top_idx = lax.top_k(lg, TOP_K)
    gates = jax.nn.softmax(top_val, axis=-1)
    e_flat = top_idx.reshape(-1).astype(jnp.int32)
    onehot = (e_flat[:, None] == jnp.arange(N_EXPERTS, dtype=jnp.int32)[None, :]).astype(jnp.int32)
    rank = jnp.sum((jnp.cumsum(onehot, axis=0) - 1) * onehot, axis=1)
    counts = jnp.sum(onehot, axis=0)
    padded = ((counts + tm - 1) // tm) * tm
    ends = jnp.cumsum(padded)
    starts = ends - padded
    dest = starts[e_flat] + rank
    token = jnp.arange(TOP_K * t, dtype=jnp.int32) // TOP_K
    src = jnp.zeros((p,), jnp.int32).at[dest].set(token)
    gate_sorted = jnp.zeros((p,), F32).at[dest].set(gates.reshape(-1))
    tile_start = jnp.arange(nt, dtype=jnp.int32) * tm
    tile_expert = jnp.minimum(jnp.searchsorted(ends, tile_start, side="right"), N_EXPERTS - 1).astype(jnp.int32)
    tile_valid = (tile_start < ends[-1]).astype(jnp.int32)
    xs = jnp.take(v.reshape(t, d), src, axis=0)
    ys = moe_grouped_ffn(xs, gate_sorted[:, None], tile_expert, tile_valid, wg, wu, wd)
    dest2 = dest.reshape(t, TOP_K)
    f = jnp.take(ys, dest2[:, 0], axis=0) + jnp.take(ys, dest2[:, 1], axis=0)
    return gated_residual(h, f.reshape(b, l, d), mod)


def _final_norm_kernel(h_ref, g_ref, o_ref):
    o_ref[...] = _rms(h_ref[...]) * g_ref[...]


def final_rms_norm(h, gain, n_ctx_tiles):
    b, l, d = h.shape
    tm = ROW_TILE
    n = l - n_ctx_tiles * tm
    return pl.pallas_call(
        _final_norm_kernel,
        grid=(b, n // tm),
        in_specs=[
            pl.BlockSpec((None, tm, d), lambda i, t: (i, t + n_ctx_tiles, 0)),
            pl.BlockSpec((1, d), lambda i, t: (0, 0)),
        ],
        out_specs=pl.BlockSpec((None, tm, d), lambda i, t: (i, t, 0)),
        out_shape=jax.ShapeDtypeStruct((b, n, d), F32),
        compiler_params=pltpu.CompilerParams(dimension_semantics=("parallel", "parallel")),
        name="final_norm",
    )(h, gain)


GDN_W = GDN_HEADS * GDN_DK
GDN_CONV_K = 5
GDN_HALO = SUBLANES


def _split3(x):
    p0 = x.astype(BF16)
    r1 = x - p0.astype(F32)
    p1 = r1.astype(BF16)
    p2 = (r1 - p1.astype(F32)).astype(BF16)
    return p0, p1, p2


def _gdn_prep_kernel(x_ref, prev_ref, next_ref, cw_ref, par_ref, q_ref, k_ref, v_ref, gb_ref):
    t = pl.program_id(1)
    last = pl.num_programs(1) - 1
    tm = x_ref.shape[0]
    w3 = 3 * GDN_W
    has_prev = t > 1
    has_next = (t > 0) & (t < last)
    prev = jnp.where(has_prev, prev_ref[...], 0.0)
    nxt = jnp.where(has_next, next_ref[...], 0.0)
    xe = jnp.concatenate([prev, x_ref[:, :w3], nxt], axis=0)
    y = jnp.zeros((tm, w3), F32)
    for j in range(GDN_CONV_K):
        lo = GDN_HALO - GDN_CONV_K // 2 + j
        y = y + cw_ref[j:j + 1, :] * xe[lo:lo + tm, :]
    y = _silu(y)
    r = lax.broadcasted_iota(jnp.int32, (GDN_W, GDN_W), 0)
    c = lax.broadcasted_iota(jnp.int32, (GDN_W, GDN_W), 1)
    ones_bd = jnp.where(r // GDN_DK == c // GDN_DK, 1.0, 0.0).astype(BF16)

    def l2n(x):
        sq = x * x
        hi = sq.astype(BF16)
        lo = (sq - hi.astype(F32)).astype(BF16)
        return x * lax.rsqrt(_dot(hi, ones_bd) + _dot(lo, ones_bd) + EPS)

    q_ref[...] = l2n(y[:, :GDN_W]) * GDN_DK ** -0.5
    k_ref[...] = l2n(y[:, GDN_W:2 * GDN_W])
    v_ref[...] = y[:, 2 * GDN_W:]
    ab = x_ref[:, w3 + GDN_W:]
    lane = lax.broadcasted_iota(jnp.int32, ab.shape, 1)
    is_g = (lane % 8) < 4
    z = ab + par_ref[1:2, :]
    softplus = jnp.maximum(z, 0.0) + jnp.log1p(jnp.exp(-jnp.abs(z)))
    g = jnp.where(is_g, par_ref[0:1, :] * softplus, 0.0)
    beta = 1.0 / (1.0 + jnp.exp(-ab))
    i = lax.broadcasted_iota(jnp.int32, (tm, tm), 0)
    j = lax.broadcasted_iota(jnp.int32, (tm, tm), 1)
    same_chunk = i // GDN_CHUNK == j // GDN_CHUNK
    tri_f = jnp.where(same_chunk & (j <= i), 1.0, 0.0).astype(BF16)
    tri_b = jnp.where(same_chunk & (j >= i), 1.0, 0.0).astype(BF16)
    pieces = _split3(g)
    gc_f = functools.reduce(jnp.add, [_dot(tri_f, p) for p in pieces])
    gc_b = functools.reduce(jnp.add, [_dot(tri_b, p) for p in pieces])
    gb_ref[...] = jnp.where(is_g, jnp.where(lane < 8, gc_f, gc_b), beta)


def gdn_prep(pb, conv_w, a_log, dt_bias):
    b, l, _ = pb.shape
    tm = ROW_TILE
    w3 = 3 * GDN_W
    halo_blocks = tm // GDN_HALO
    n_halo = l // GDN_HALO
    cw = jnp.pad(conv_w, ((0, SUBLANES - GDN_CONV_K), (0, 0)))
    neg_a = jnp.pad(-jnp.exp(a_log.astype(F32)), ((0, 0), (0, 4))).reshape(-1)
    dtb = jnp.pad(dt_bias.astype(F32), ((0, 0), (0, 4))).reshape(-1)
    par = jnp.pad(jnp.stack([neg_a, dtb]), ((0, SUBLANES - 2), (0, LANES - 16)))
    row = lambda i, t: (i, t, 0)
    out = lambda w: pl.BlockSpec((None, tm, w), row)
    return pl.pallas_call(
        _gdn_prep_kernel,
        grid=(b, l // tm),
        in_specs=[pl.BlockSpec((None, tm, B_W), row),
                  pl.BlockSpec((None, GDN_HALO, w3), lambda i, t: (i, jnp.maximum(t * halo_blocks - 1, 0), 0)),
                  pl.BlockSpec((None, GDN_HALO, w3),
                               lambda i, t: (i, jnp.minimum((t + 1) * halo_blocks, n_halo - 1), 0)),
                  pl.BlockSpec(cw.shape, lambda i, t: (0, 0)),
                  pl.BlockSpec(par.shape, lambda i, t: (0, 0))],
        out_specs=[out(GDN_W), out(GDN_W), out(GDN_W), out(LANES)],
        out_shape=[jax.ShapeDtypeStruct((b, l, GDN_W), F32)] * 3 + [jax.ShapeDtypeStruct((b, l, LANES), F32)],
        compiler_params=pltpu.CompilerParams(
            dimension_semantics=("parallel", "parallel"), vmem_limit_bytes=VMEM_LIMIT),
        name="gdn_prep",
    )(pb, pb, pb, cw, par)


def _tile_heads(x):
    return jnp.concatenate([x] * GDN_HEADS, axis=0)


def _collapse_heads(x):
    c = GDN_CHUNK
    return x[0:c] + x[c:2 * c] + x[2 * c:3 * c] + x[3 * c:4 * c]


def _gdn_chunk_kernel(q_ref, k_ref, v_ref, gb_ref, o0_ref, qe_ref, a_ref, bm_ref, gam_ref):
    dr = pl.program_id(1)
    fwd = dr == 0
    n = GDN_W
    cs = GDN_CHUNK
    r = lax.broadcasted_iota(jnp.int32, (n, n), 0)
    c = lax.broadcasted_iota(jnp.int32, (n, n), 1)
    ri, ci = r % cs, c % cs
    head = r // cs == c // cs
    ahead = (ri - ci) * jnp.where(fwd, 1, -1)
    tri = head & (ahead >= 0)
    tri_strict = head & (ahead > 0)
    eye = jnp.where(r == c, 1.0, 0.0)
    blk = lambda s: r // s == c // s
    b8, b16, b32 = blk(8), blk(16), blk(32)
    lane = lax.broadcasted_iota(jnp.int32, (n, LANES), 1)
    row_head = lax.broadcasted_iota(jnp.int32, (n, LANES), 0) // cs
    sel_g = lane == dr * 8 + row_head
    sel_b = lane == dr * 8 + 4 + row_head
    pick = lambda sel, x: jnp.sum(jnp.where(sel, x, 0.0), axis=1, keepdims=True)
    mm = lambda a, b: _dot(a.astype(BF16), b.astype(BF16))

    for ch in range(q_ref.shape[0] // cs):
        rows = slice(ch * cs, (ch + 1) * cs)
        kh = jnp.where(head, _tile_heads(k_ref[rows, :]), 0.0)
        qh = jnp.where(head, _tile_heads(q_ref[rows, :]), 0.0)
        vh = jnp.where(head, _tile_heads(v_ref[rows, :]), 0.0)
        gb = gb_ref[rows, :]
        gb4 = _tile_heads(gb)
        gc = pick(sel_g, gb4)
        beta = pick(sel_b, gb4)
        g_end = jnp.where(fwd, gb[cs - 1:cs, :], gb[0:1, :])
        gl = pick(sel_g, jnp.broadcast_to(g_end, (n, LANES)))
        gc_b = jnp.broadcast_to(gc, (n, n))
        decay = jnp.exp(jnp.minimum(gc_b - gc_b.T, 0.0))
        khb = kh.astype(BF16)
        kk = _dot_nt(khb, khb)
        lmat = jnp.where(tri_strict, beta * kk * decay, 0.0)
        nl = jnp.where(b8, -lmat, 0.0)
        n2 = mm(nl, nl)
        n4 = mm(n2, n2)
        tinv = mm(mm(eye + nl, eye + n2), eye + n4)
        for inner, outer in ((b8, b16), (b16, b32), (b32, head)):
            off = jnp.where(outer & ~inner, lmat, 0.0)
            tinv = tinv - mm(mm(tinv, off), tinv)
        eg = jnp.exp(gc)
        u = mm(tinv, beta * vh)
        w = mm(tinv, (beta * eg) * kh)
        attn = jnp.where(tri, _dot_nt(qh.astype(BF16), khb) * decay, 0.0)
        o0 = mm(attn, u)
        qe = qh * eg - mm(attn, w)
        kg = (kh * jnp.exp(gl - gc)).astype(BF16)
        a_mat = _dot_tn(kg, w.astype(BF16))
        b_mat = _dot_tn(kg, u.astype(BF16))
        o0_ref[rows, :] = _collapse_heads(o0)
        qe_ref[rows, :] = _collapse_heads(qe)
        a_ref[rows, :] = _collapse_heads(a_mat)
        bm_ref[rows, :] = _collapse_heads(b_mat)
        gam_ref[rows, :] = _collapse_heads(jnp.where(head, jnp.exp(jnp.broadcast_to(gl, (n, n))), 0.0))


def gdn_chunks(q, k, v, gb):
    b, l, _ = q.shape
    tm = ROW_TILE
    row = lambda i, dr, t: (i, t, 0)
    out = pl.BlockSpec((None, None, tm, GDN_W), lambda i, dr, t: (dr, i, t, 0))
    return pl.pallas_call(
        _gdn_chunk_kernel,
        grid=(b, 2, l // tm),
        in_specs=[pl.BlockSpec((None, tm, GDN_W), row)] * 3 + [pl.BlockSpec((None, tm, LANES), row)],
        out_specs=[out] * 5,
        out_shape=[jax.ShapeDtypeStruct((2, b, l, GDN_W), F32)] * 5,
        compiler_params=pltpu.CompilerParams(
            dimension_semantics=("parallel", "parallel", "parallel"), vmem_limit_bytes=VMEM_LIMIT),
        name="gdn_chunk",
    )(q, k, v, gb)


def _gdn_scan_kernel(o0_ref, qe_ref, a_ref, bm_ref, gam_ref, gate_ref, g_ref, o_ref, s_ref):
    dr = pl.program_id(1)
    s = pl.program_id(2)
    ns = pl.num_programs(2)
    tm = o0_ref.shape[0]
    cs = GDN_CHUNK
    n = GDN_W
    nch = tm // cs
    tile = jnp.where(s == 0, 0, jnp.where(dr == 0, s, ns - s))
    r = lax.broadcasted_iota(jnp.int32, (n, n), 0)
    c = lax.broadcasted_iota(jnp.int32, (n, n), 1)
    head = r // cs == c // cs

    @pl.when(s == 0)
    def _():
        s_ref[...] = jnp.zeros_like(s_ref)

    def run(order):
        state = s_ref[...]
        outs = {}
        for ch in order:
            rows = slice(ch * cs, (ch + 1) * cs)
            sb = state.astype(BF16)
            outs[ch] = o0_ref[rows, :] + _dot(qe_ref[rows, :].astype(BF16), sb)
            a_full = jnp.where(head, _tile_heads(a_ref[rows, :]), 0.0).astype(BF16)
            b_full = jnp.where(head, _tile_heads(bm_ref[rows, :]), 0.0)
            state = _tile_heads(gam_ref[rows, :]) * state - _dot(a_full, sb) + b_full
        s_ref[...] = state
        return jnp.concatenate([outs[ch] for ch in range(nch)], axis=0)

    rows_out = pl.ds(pl.multiple_of(tile * tm, tm), tm)

    @pl.when(dr == 0)
    def _():
        o_ref[rows_out, :] = run(range(nch))

    @pl.when(dr == 1)
    def _():
        o = o_ref[rows_out, :] + run(range(nch - 1, -1, -1))
        ones_bd = jnp.where(head, 1.0, 0.0).astype(BF16)
        ms = _head_mean(o * o, ones_bd)
        o_ref[rows_out, :] = o * lax.rsqrt(ms + EPS) * g_ref[...] * _silu(gate_ref[...])


def gdn_scan(o0, qe, a, bm, gam, pb, norm_g):
    _, b, l, _ = o0.shape
    tm = ROW_TILE
    ns = l // tm

    def tile_of(dr, s):
        return jnp.where(s == 0, 0, jnp.where(dr == 0, s, ns - s))

    per_dir = pl.BlockSpec((None, None, tm, GDN_W), lambda i, dr, s: (dr, i, tile_of(dr, s), 0))
    gate_col = 3 * GDN_W // GDN_W
    return pl.pallas_call(
        _gdn_scan_kernel,
        grid=(b, 2, ns),
        in_specs=[per_dir] * 5 + [
            pl.BlockSpec((None, tm, GDN_W), lambda i, dr, s: (i, tile_of(dr, s), gate_col)),
            pl.BlockSpec((1, GDN_W), lambda i, dr, s: (0, 0))],
        out_specs=pl.BlockSpec((None, l, GDN_W), lambda i, dr, s: (i, 0, 0)),
        out_shape=jax.ShapeDtypeStruct((b, l, GDN_W), F32),
        scratch_shapes=[pltpu.VMEM((GDN_W, GDN_W), F32)],
        compiler_params=pltpu.CompilerParams(
            dimension_semantics=("parallel", "arbitrary", "arbitrary"), vmem_limit_bytes=VMEM_LIMIT),
        name="gdn_scan",
    )(o0, qe, a, bm, gam, pb, jnp.tile(norm_g, GDN_HEADS)[None, :])


def gdn_mixer(pb, conv_w, a_log, dt_bias, norm_g):
    q, k, v, gb = gdn_prep(pb, conv_w, a_log, dt_bias)
    o0, qe, a, bm, gam = gdn_chunks(q, k, v, gb)
    return gdn_scan(o0, qe, a, bm, gam, pb, norm_g)


def rms_norm(x, g):
    xf = x.astype(F32)
    y = xf * lax.rsqrt(jnp.mean(xf * xf, axis=-1, keepdims=True) + EPS)
    return (y * g.astype(F32)).astype(x.dtype)


def l2_normalize(t):
    return t * lax.rsqrt(jnp.sum(t * t, axis=-1, keepdims=True) + EPS)


def directional_scan(chunked_fn, seqs, consts, s0, reverse):
    if reverse:
        seqs = tuple(jnp.flip(t, axis=1) for t in seqs)
    o, s = chunked_fn(*seqs, *consts, s0)
    if reverse:
        o = jnp.flip(o, axis=1)
    return o, s


def prefix_bidirectional_scan(chunked_fn, seqs_c, seqs_x, consts, state_shape):
    out_c, out_x = 0.0, 0.0
    for d in range(2):
        rev = d == 1
        s0 = jnp.zeros(state_shape, F32)
        oc, s_ctx = directional_scan(chunked_fn, seqs_c[d], consts[d], s0, rev)
        ox, _ = directional_scan(chunked_fn, seqs_x[d], consts[d], s_ctx, rev)
        out_c = out_c + oc
        out_x = out_x + ox
    return out_c, out_x


def short_conv(x, w):
    k = w.shape[0]
    y = lax.conv_general_dilated(
        x, w[:, None, :].astype(x.dtype), window_strides=(1,), padding=[(k // 2, k // 2)],
        dimension_numbers=('NWC', 'WIO', 'NWC'), feature_group_count=x.shape[-1])
    return jax.nn.silu(y)


def gated_delta_chunked(q, k, v, g, beta, s0):
    b, l, h, _ = q.shape
    dv = v.shape[-1]
    cs = GDN_CHUNK
    nc = l // cs

    def chunks(t):
        return t.reshape(b, nc, cs, h, -1).transpose(1, 0, 3, 2, 4)

    qc, kc, vc = chunks(q), chunks(k), chunks(v)
    gc = jnp.cumsum(chunks(g[..., None])[..., 0], axis=-1)
    bc = chunks(beta[..., None])
    idx = jnp.arange(cs)
    lower = idx[:, None] >= idx[None, :]
    strict = idx[:, None] > idx[None, :]
    decay = jnp.exp(jnp.where(lower, gc[..., :, None] - gc[..., None, :], NEG_INF))
    kb = kc * bc
    lmat = jnp.where(strict, jnp.einsum('nbhid,nbhjd->nbhij', kb, kc) * decay, 0.0)
    a_mat = lmat + jnp.eye(cs, dtype=F32)
    u = lax.linalg.triangular_solve(a_mat, vc * bc, left_side=True, lower=True)
    w = lax.linalg.triangular_solve(a_mat, kb * jnp.exp(gc)[..., None], left_side=True, lower=True)
    attn = jnp.einsum('nbhid,nbhjd->nbhij', qc, kc) * decay

    def step(s, xs):
        q_i, k_i, u_i, w_i, a_i, g_i = xs
        v_new = u_i - jnp.einsum('bhck,bhkv->bhcv', w_i, s)
        o_i = (jnp.einsum('bhck,bhkv->bhcv', q_i * jnp.exp(g_i)[..., None], s)
               + jnp.einsum('bhij,bhjv->bhiv', a_i, v_new))
        g_last = g_i[..., -1:]
        s = (s * jnp.exp(g_last)[..., None]
             + jnp.einsum('bhck,bhcv->bhkv', k_i * jnp.exp(g_last - g_i)[..., None], v_new))
        return s, o_i

    s_fin, o = lax.scan(step, s0, (qc, kc, u, w, attn, gc))
    return o.transpose(1, 0, 3, 2, 4).reshape(b, l, h, dv), s_fin


def gdn_mixer_jax(pb, conv_w, a_log, dt_bias, norm_g):
    def prep(qkv, ab):
        b, l, _ = qkv.shape
        qkv = short_conv(qkv, conv_w).astype(F32)
        q, k, v = jnp.split(qkv, [GDN_HEADS * GDN_DK, 2 * GDN_HEADS * GDN_DK], axis=-1)
        q = l2_normalize(q.reshape(b, l, GDN_HEADS, GDN_DK)) * GDN_DK ** -0.5
        k = l2_normalize(k.reshape(b, l, GDN_HEADS, GDN_DK))
        v = v.reshape(b, l, GDN_HEADS, GDN_DV)
        ab = ab.astype(F32).reshape(b, l, 2, 2, GDN_HEADS)
        g = -jnp.exp(a_log.astype(F32)) * jax.nn.softplus(ab[:, :, :, 0] + dt_bias.astype(F32))
        beta = jax.nn.sigmoid(ab[:, :, :, 1])
        return [(q, k, v, g[:, :, d], beta[:, :, d]) for d in range(2)]

    def parts(rows):
        return rows[..., :768], rows[..., 768:1024], rows[..., 1024:1040]

    qkv_c, gate_c, ab_c = parts(pb[:, :CTX_LEN])
    qkv_x, gate_x, ab_x = parts(pb[:, CTX_LEN:])
    s_shape = (qkv_x.shape[0], GDN_HEADS, GDN_DK, GDN_DV)
    o_c, o_x = prefix_bidirectional_scan(gated_delta_chunked, prep(qkv_c, ab_c), prep(qkv_x, ab_x),
                                         [(), ()], s_shape)

    def gated_out(o, gate):
        b, l = gate.shape[:2]
        y = rms_norm(o, norm_g) * jax.nn.silu(gate.astype(F32)).reshape(o.shape)
        return y.reshape(b, l, GDN_HEADS * GDN_DV).astype(BF16)

    return jnp.concatenate([gated_out(o_c, gate_c), gated_out(o_x, gate_x)], axis=1)


def kernel(x, c, ctx, c_ctx, w_mod, b_mod, norm1, norm2, w_in, w_out, swa_sink, gdn_conv, gdn_a_log, gdn_dt_bias, gdn_norm, mla_q_norm, mla_kv_norm, mla_w_q_up, mla_w_kv_up, ret_log_decay, ret_norm, ffn_w_gate, ffn_w_up, ffn_w_down, moe_router, moe_w_gate, moe_w_up, moe_w_down, final_norm):
    b, n, d = x.shape
    depth = w_in.shape[0]
    cos_t, sin_t = rope_tables(n)
    cos_c, sin_c = cos_t[:, A_ROT_W:A_ROT_W + C_ROT_W], sin_t[:, A_ROT_W:A_ROT_W + C_ROT_W]
    silu_c = jax.nn.silu(c)
    silu_cc = jax.nn.silu(c_ctx)
    h = jnp.concatenate([ctx, x], axis=1)
    for layer in range(depth):
        mod_x = jnp.dot(silu_c, w_mod[layer], precision=lax.Precision.HIGHEST) + b_mod[layer]
        mod_c = jnp.dot(silu_cc, w_mod[layer], precision=lax.Precision.HIGHEST) + b_mod[layer]
        mod = jnp.stack([jnp.broadcast_to(mod_c, mod_x.shape), mod_x], axis=1).reshape(b, 2, 6, d)
        mod = jnp.pad(mod, ((0, 0), (0, 0), (0, SUBLANES - 6), (0, 0)))
        pa, pb, pc, pd = norm_proj(h, mod, norm1[layer][None, :], build_in_weight(w_in[layer]), cos_t, sin_t)
        mix_a = swa_mixer(pa, swa_sink[layer])
        mix_b = gdn_mixer(pb, gdn_conv[layer], gdn_a_log[layer], gdn_dt_bias[layer], gdn_norm[layer])
        mq, mk, mv = mla_prep(pc, mla_q_norm[layer], mla_kv_norm[layer], mla_w_q_up[layer], mla_w_kv_up[layer],
                              cos_c, sin_c)
        mix_c = mla_attention(mq, mk, mv)
        mix_d = retention_mixer(pd, ret_log_decay[layer], ret_norm[layer])
        mixes = (mix_a, mix_b, mix_c, mix_d)
        w_o = build_out_weight(w_out[layer])
        i = layer // 2
        if layer % 2 == 0:
            h, v = out_proj(mixes, h, mod, norm2[layer][None, :], w_o)
            h = dense_ffn(v, h, mod, ffn_w_gate[i].astype(BF16), ffn_w_up[i].astype(BF16),
                          ffn_w_down[i].astype(BF16))
        else:
            w_r = jnp.pad(moe_router[i], ((0, 0), (0, LANES - N_EXPERTS)))
            h, v, logits = out_proj(mixes, h, mod, norm2[layer][None, :], w_o, w_r)
            h = moe_ffn(v, logits, h, mod, moe_w_gate[i].astype(BF16), moe_w_up[i].astype(BF16),
                        moe_w_down[i].astype(BF16))
    return final_rms_norm(h, final_norm[None, :], CTX_LEN // ROW_TILE)
```

```python
import functools

import numpy as np
import jax
import jax.numpy as jnp
from jax import lax
from jax.experimental import pallas as pl
from jax.experimental.pallas import tpu as pltpu

D_MODEL = 1024
GRID_W = 64
CTX_LEN = 256
HEAD_DIM = 64
ROPE_THETA = 10000.0
EPS = 1e-6
NEG_INF = -1e30

SWA_WINDOW = 128
GDN_HEADS = 4
GDN_DK = 64
GDN_DV = 64
GDN_CHUNK = 64
MLA_HEADS = 4
MLA_NOPE = 64
MLA_ROPE = 32
MLA_V = 64
RET_HEADS = 4
RET_DK = 64
D_FF = 3584
N_EXPERTS = 8
TOP_K = 2

LANES = 128
SUBLANES = 8
VMEM_LIMIT = 56 * 1024 * 1024

ROW_TILE = 256
FF_CHUNK = 512
MOE_TILE = 512

A_W, B_W, C_W, D_W = 768, 1152, 512, 1024
A_ROT_W, C_ROT_W, D_ROT_W = 640, 128, 512
OFF_A = 0
OFF_B = OFF_A + A_W
OFF_C = OFF_B + B_W
OFF_D = OFF_C + C_W
OFF_AR = OFF_D + D_W
OFF_CR = OFF_AR + A_ROT_W
OFF_DR = OFF_CR + C_ROT_W
W_ALL = OFF_DR + D_ROT_W
ROPE_W = A_ROT_W + C_ROT_W + D_ROT_W

F32 = jnp.float32
BF16 = jnp.bfloat16
NT_DIMS = (((1,), (1,)), ((), ()))
TN_DIMS = (((0,), (0,)), ((), ()))


def _rms(x):
    return x * lax.rsqrt(jnp.mean(x * x, axis=-1, keepdims=True) + EPS)


def _silu(x):
    return x * (1.0 / (1.0 + jnp.exp(-x)))


def _dot(a, b):
    return jnp.dot(a, b, preferred_element_type=F32)


def _dot_nt(a, b):
    return lax.dot_general(a, b, NT_DIMS, preferred_element_type=F32)


def _dot_tn(a, b):
    return lax.dot_general(a, b, TN_DIMS, preferred_element_type=F32)


def _mod_spec(d):
    return pl.BlockSpec((None, None, SUBLANES, d), lambda i, t: (i, jnp.minimum(t, 1), 0, 0))


def _norm_proj_kernel(h_ref, mod_ref, g_ref, w_ref, cos_ref, sin_ref, a_ref, b_ref, c_ref, d_ref):
    x = h_ref[...]
    u = (_rms(x) * g_ref[...] * (1.0 + mod_ref[1:2, :]) + mod_ref[0:1, :]).astype(BF16)

    def mm(lo, width):
        return _dot(u, w_ref[:, lo:lo + width])

    a_main = mm(OFF_A, A_W)
    a_rot = mm(OFF_AR, A_ROT_W)
    a_ref[:, :A_ROT_W] = (a_main[:, :A_ROT_W] * cos_ref[:, :A_ROT_W] + a_rot * sin_ref[:, :A_ROT_W]).astype(BF16)
    a_ref[:, A_ROT_W:] = a_main[:, A_ROT_W:].astype(BF16)
    b_ref[...] = mm(OFF_B, B_W)
    c_main = mm(OFF_C, C_W)
    c_rot = mm(OFF_CR, C_ROT_W)
    lo, hi = A_ROT_W, A_ROT_W + C_ROT_W
    c_ref[:, :C_W - C_ROT_W] = c_main[:, :C_W - C_ROT_W]
    c_ref[:, C_W - C_ROT_W:] = c_main[:, C_W - C_ROT_W:] * cos_ref[:, lo:hi] + c_rot * sin_ref[:, lo:hi]
    d_main = mm(OFF_D, D_W)
    d_rot = mm(OFF_DR, D_ROT_W)
    d_ref[:, :D_ROT_W] = d_main[:, :D_ROT_W] * cos_ref[:, hi:] + d_rot * sin_ref[:, hi:]
    d_ref[:, D_ROT_W:] = d_main[:, D_ROT_W:]


def norm_proj(h, mod, gain, w, cos_t, sin_t):
    b, l, d = h.shape
    tm = ROW_TILE
    row = lambda i, t: (i, t, 0)
    return pl.pallas_call(
        _norm_proj_kernel,
        grid=(b, l // tm),
        in_specs=[
            pl.BlockSpec((None, tm, d), row),
            _mod_spec(d),
            pl.BlockSpec((1, d), lambda i, t: (0, 0)),
            pl.BlockSpec((d, W_ALL), lambda i, t: (0, 0), pipeline_mode=pl.Buffered(1)),
            pl.BlockSpec((tm, ROPE_W), lambda i, t: (t, 0)),
            pl.BlockSpec((tm, ROPE_W), lambda i, t: (t, 0)),
        ],
        out_specs=[pl.BlockSpec((None, tm, A_W), row), pl.BlockSpec((None, tm, B_W), row),
                   pl.BlockSpec((None, tm, C_W), row), pl.BlockSpec((None, tm, D_W), row)],
        out_shape=[jax.ShapeDtypeStruct((b, l, A_W), BF16), jax.ShapeDtypeStruct((b, l, B_W), F32),
                   jax.ShapeDtypeStruct((b, l, C_W), F32), jax.ShapeDtypeStruct((b, l, D_W), F32)],
        compiler_params=pltpu.CompilerParams(
            dimension_semantics=("parallel", "parallel"), vmem_limit_bytes=VMEM_LIMIT),
        name="norm_proj",
    )(h, mod, gain, w, cos_t, sin_t)


def _rot_cols(w, hd):
    x = w.reshape(w.shape[:-1] + (w.shape[-1] // hd, 4, hd // 4))
    x1, x2, x3, x4 = x[..., 0, :], x[..., 1, :], x[..., 2, :], x[..., 3, :]
    return jnp.stack([-x2, x1, -x4, x3], axis=-2).reshape(w.shape)


def _place_swa_q(q):
    z = jnp.zeros((q.shape[0], HEAD_DIM), q.dtype)
    blocks = []
    for h in range(4):
        qh = q[:, HEAD_DIM * h:HEAD_DIM * (h + 1)]
        blocks += [qh, z] if h // 2 == 0 else [z, qh]
    return jnp.concatenate(blocks, axis=1)


def build_in_weight(w):
    d = w.shape[0]
    o = [int(v) for v in np.cumsum((256, 128, 128, 768, 256, 16, 256, 128, 32, 256, 256, 256, 256))]
    aq, ak, av = w[:, :o[0]] * HEAD_DIM ** -0.5, w[:, o[0]:o[1]], w[:, o[1]:o[2]]
    b_main, b_ab = w[:, o[2]:o[4]], w[:, o[4]:o[5]]
    c_q, c_kv, c_kr = w[:, o[5]:o[6]], w[:, o[6]:o[7]], w[:, o[7]:o[8]]
    dq, dk, dvg = w[:, o[8]:o[9]], w[:, o[9]:o[10]] * RET_DK ** -0.5, w[:, o[10]:]
    z = lambda n: jnp.zeros((d, n), w.dtype)
    parts = [
        _place_swa_q(aq), ak, av,
        b_main, b_ab, z(LANES - b_ab.shape[1]),
        c_q, c_kv, z(64), c_kr, z(32),
        dq, dk, dvg,
        _place_swa_q(_rot_cols(aq, HEAD_DIM)), _rot_cols(ak, HEAD_DIM),
        z(64), _rot_cols(c_kr, MLA_ROPE), z(32),
        _rot_cols(dq, HEAD_DIM), _rot_cols(dk, HEAD_DIM),
    ]
    out = jnp.concatenate(parts, axis=1)
    assert out.shape[1] == W_ALL
    return out.astype(BF16)


def rope_tables(n):
    def axial(rot_dim):
        n_freq = rot_dim // 4
        inv_freq = ROPE_THETA ** (-jnp.arange(n_freq, dtype=F32) / n_freq)
        row = jnp.repeat(jnp.arange(n // GRID_W, dtype=F32), GRID_W)
        col = jnp.tile(jnp.arange(GRID_W, dtype=F32), n // GRID_W)
        ang_r = row[:, None] * inv_freq
        ang_c = col[:, None] * inv_freq
        ang = jnp.concatenate([ang_r, ang_r, ang_c, ang_c], axis=-1)
        return jnp.cos(ang), jnp.sin(ang)

    cos_h, sin_h = axial(HEAD_DIM)
    cos_r, sin_r = axial(MLA_ROPE)
    one, zero = jnp.ones((n, 1), F32), jnp.zeros((n, 1), F32)
    cos_c = jnp.concatenate([jnp.tile(one, (1, 64)), cos_r, jnp.tile(one, (1, 32))], axis=1)
    sin_c = jnp.concatenate([jnp.tile(zero, (1, 64)), sin_r, jnp.tile(zero, (1, 32))], axis=1)
    cos_t = jnp.concatenate([jnp.tile(cos_h, (1, A_ROT_W // HEAD_DIM)), cos_c,
                             jnp.tile(cos_h, (1, D_ROT_W // HEAD_DIM))], axis=1)
    sin_t = jnp.concatenate([jnp.tile(sin_h, (1, A_ROT_W // HEAD_DIM)), sin_c,
                             jnp.tile(sin_h, (1, D_ROT_W // HEAD_DIM))], axis=1)
    cos_t = jnp.concatenate([jnp.ones((CTX_LEN, ROPE_W), F32), cos_t], axis=0)
    sin_t = jnp.concatenate([jnp.zeros((CTX_LEN, ROPE_W), F32), sin_t], axis=0)
    return cos_t, sin_t


def _swa_kernel(sink_ref, q_ref, kp_ref, ko_ref, kn_ref, kc_ref, vp_ref, vo_ref, vn_ref, vc_ref, o_ref):
    t = pl.program_id(1)
    last = pl.num_programs(1) - 1
    tq = q_ref.shape[0]
    half = tq // 2

    def head_out(h, pieces):
        q = q_ref[:, LANES * h:LANES * (h + 1)]
        ss = []
        for k_ref, _, mask in pieces:
            s = _dot_nt(q, k_ref[...])
            ss.append(s if mask is None else jnp.where(mask, s, NEG_INF))
        sink = sink_ref[h]
        m = jnp.maximum(functools.reduce(jnp.maximum, [s.max(axis=-1, keepdims=True) for s in ss]), sink)
        ps = [jnp.exp(s - m) for s in ss]
        denom = functools.reduce(jnp.add, [p.sum(axis=-1, keepdims=True) for p in ps]) + jnp.exp(sink - m)
        o = functools.reduce(jnp.add, [_dot(p.astype(BF16), piece[1][...]) for p, piece in zip(ps, pieces)])
        return o / denom

    def write(pieces):
        outs = [head_out(h, pieces) for h in range(4)]
        lane = lax.broadcasted_iota(jnp.int32, (tq, LANES), 1)
        for r in range(2):
            o_ref[:, LANES * r:LANES * (r + 1)] = jnp.where(lane < HEAD_DIM, outs[r], outs[2 + r]).astype(o_ref.dtype)

    @pl.when(t == 0)
    def _():
        write([(kc_ref, vc_ref, None)])

    @pl.when(t > 0)
    def _():
        qi = lax.broadcasted_iota(jnp.int32, (tq, half), 0)
        kj = lax.broadcasted_iota(jnp.int32, (tq, half), 1)
        mask_prev = (kj >= qi) & (t > 1)
        mask_next = (kj <= qi - half) & (t < last)
        qo = lax.broadcasted_iota(jnp.int32, (tq, tq), 0)
        ko = lax.broadcasted_iota(jnp.int32, (tq, tq), 1)
        mask_own = jnp.abs(qo - ko) <= SWA_WINDOW
        write([(kp_ref, vp_ref, mask_prev), (ko_ref, vo_ref, mask_own), (kn_ref, vn_ref, mask_next),
               (kc_ref, vc_ref, None)])


def swa_mixer(pa, sink):
    b, l, _ = pa.shape
    tq = ROW_TILE
    nblk = l // SWA_WINDOW
    kcol, vcol = 4, 5
    prev = lambda c: (lambda i, t: (i, jnp.maximum(2 * t - 1, 2), c))
    nxt = lambda c: (lambda i, t: (i, jnp.minimum(2 * t + 2, nblk - 1), c))
    own = lambda c: (lambda i, t: (i, t, c))
    ctx = lambda c: (lambda i, t: (i, 0, c))
    kv_specs = lambda c: [pl.BlockSpec((None, SWA_WINDOW, LANES), prev(c)), pl.BlockSpec((None, tq, LANES), own(c)),
                          pl.BlockSpec((None, SWA_WINDOW, LANES), nxt(c)), pl.BlockSpec((None, tq, LANES), ctx(c))]
    return pl.pallas_call(
        _swa_kernel,
        grid=(b, l // tq),
        in_specs=[pl.BlockSpec(memory_space=pltpu.SMEM),
                  pl.BlockSpec((None, tq, 4 * LANES), lambda i, t: (i, t, 0))] + kv_specs(kcol) + kv_specs(vcol),
        out_specs=pl.BlockSpec((None, tq, 2 * LANES), lambda i, t: (i, t, 0)),
        out_shape=jax.ShapeDtypeStruct((b, l, 2 * LANES), BF16),
        compiler_params=pltpu.CompilerParams(
            dimension_semantics=("parallel", "parallel"), vmem_limit_bytes=VMEM_LIMIT),
        name="swa",
    )(sink, pa, pa, pa, pa, pa, pa, pa, pa, pa)


def _mla_prep_kernel(c_ref, qn_ref, kvn_ref, wq_ref, wqr_ref, wk_ref, wv_ref, cos_ref, sin_ref,
                     q_ref, k_ref, v_ref):
    cq = c_ref[:, 0:256]
    ckv = c_ref[:, 256:384]
    kr = c_ref[:, 384:512]
    nq = (_rms(cq) * qn_ref[...]).astype(BF16)
    nkv = (_rms(ckv) * kvn_ref[...]).astype(BF16)
    cos = jnp.concatenate([cos_ref[...]] * MLA_HEADS, axis=1)
    sin = jnp.concatenate([sin_ref[...]] * MLA_HEADS, axis=1)
    q_ref[...] = (_dot(nq, wq_ref[...]) * cos + _dot(nq, wqr_ref[...]) * sin).astype(BF16)
    k_ref[...] = (_dot(nkv, wk_ref[...]) + jnp.concatenate([kr] * MLA_HEADS, axis=1)).astype(BF16)
    v_ref[...] = _dot(nkv, wv_ref[...]).astype(BF16)


def mla_prep(pc, q_norm, kv_norm, w_q_up, w_kv_up, cos_c, sin_c):
    b, l, _ = pc.shape
    tm = ROW_TILE
    scale = (MLA_NOPE + MLA_ROPE) ** -0.5 * float(np.log2(np.e))
    wq = (w_q_up * scale).reshape(-1, MLA_HEADS, MLA_NOPE + MLA_ROPE)
    zq = jnp.zeros(wq.shape[:2] + (LANES - MLA_NOPE - MLA_ROPE,), F32)
    wq_main = jnp.concatenate([wq, zq], axis=-1).reshape(-1, MLA_HEADS * LANES)
    wq_rot = jnp.concatenate([jnp.zeros_like(wq[..., :MLA_NOPE]), _rot_cols(wq[..., MLA_NOPE:], MLA_ROPE), zq],
                             axis=-1).reshape(-1, MLA_HEADS * LANES)
    wkv = w_kv_up.reshape(-1, MLA_HEADS, MLA_NOPE + MLA_V)
    wk = jnp.concatenate([wkv[..., :MLA_NOPE], jnp.zeros_like(wkv[..., :LANES - MLA_NOPE])],
                         axis=-1).reshape(-1, MLA_HEADS * LANES)
    wv = wkv[..., MLA_NOPE:].reshape(-1, MLA_HEADS * MLA_V)
    row = lambda i, t: (i, t, 0)
    const = lambda i, t: (0, 0)
    full = lambda a: pl.BlockSpec(a.shape, const)
    args = [q_norm[None, :], kv_norm[None, :], wq_main.astype(BF16), wq_rot.astype(BF16), wk.astype(BF16),
            wv.astype(BF16)]
    return pl.pallas_call(
        _mla_prep_kernel,
        grid=(b, l // tm),
        in_specs=[pl.BlockSpec((None, tm, C_W), row)] + [full(a) for a in args]
        + [pl.BlockSpec((tm, LANES), lambda i, t: (t, 0)), pl.BlockSpec((tm, LANES), lambda i, t: (t, 0))],
        out_specs=[pl.BlockSpec((None, tm, 4 * LANES), row), pl.BlockSpec((None, tm, 4 * LANES), row),
                   pl.BlockSpec((None, tm, 2 * LANES), row)],
        out_shape=[jax.ShapeDtypeStruct((b, l, 4 * LANES), BF16), jax.ShapeDtypeStruct((b, l, 4 * LANES), BF16),
                   jax.ShapeDtypeStruct((b, l, 2 * LANES), BF16)],
        compiler_params=pltpu.CompilerParams(dimension_semantics=("parallel", "parallel")),
        name="mla_prep",
    )(pc, *args, cos_c, sin_c)


def _mla_attn_kernel(q_ref, k_ref, v_ref, o_ref):
    t = pl.program_id(2)
    tq = q_ref.shape[0]

    def attend(nk):
        v = v_ref[0:nk, :]
        v_lane = lax.broadcasted_iota(jnp.int32, v.shape, 1)
        outs = []
        for j in range(2):
            q = q_ref[:, LANES * j:LANES * (j + 1)]
            k = k_ref[0:nk, LANES * j:LANES * (j + 1)]
            s = _dot_nt(q, k)
            p = jnp.exp2(s - s.max(axis=-1, keepdims=True)).astype(BF16)
            other = (v_lane >= MLA_V) if j == 0 else (v_lane < MLA_V)
            o = _dot(p, jnp.where(other, jnp.ones_like(v), v))
            den_lane = MLA_V if j == 0 else 0
            outs.append(o / o[:, den_lane:den_lane + 1])
        lane = lax.broadcasted_iota(jnp.int32, (tq, LANES), 1)
        o_ref[...] = jnp.where(lane < MLA_V, outs[0], outs[1]).astype(o_ref.dtype)

    @pl.when(t == 0)
    def _():
        attend(CTX_LEN)

    @pl.when(t > 0)
    def _():
        attend(k_ref.shape[0])


def mla_attention(q, k, v):
    b, l, _ = q.shape
    tq = ROW_TILE
    return pl.pallas_call(
        _mla_attn_kernel,
        grid=(b, 2, l // tq),
        in_specs=[pl.BlockSpec((None, tq, 2 * LANES), lambda i, p, t: (i, t, p)),
                  pl.BlockSpec((None, l, 2 * LANES), lambda i, p, t: (i, 0, p)),
                  pl.BlockSpec((None, l, LANES), lambda i, p, t: (i, 0, p))],
        out_specs=pl.BlockSpec((None, tq, LANES), lambda i, p, t: (i, t, p)),
        out_shape=jax.ShapeDtypeStruct((b, l, 2 * LANES), BF16),
        compiler_params=pltpu.CompilerParams(
            dimension_semantics=("parallel", "parallel", "parallel"), vmem_limit_bytes=VMEM_LIMIT),
        name="mla_attn",
    )(q, k, v)


def _head_mean(x, ones_bd):
    hi = x.astype(BF16)
    lo = (x - hi.astype(F32)).astype(BF16)
    return (_dot(hi, ones_bd) + _dot(lo, ones_bd)) * (1.0 / HEAD_DIM)


def _ret_kernel(x_ref, lg_ref, g_ref, o_ref, s_ref, dec_ref):
    dr = pl.program_id(1)
    s = pl.program_id(2)
    ns = pl.num_programs(2)
    c = x_ref.shape[0]
    w = RET_HEADS * HEAD_DIM
    chunk = jnp.where(s == 0, 0, jnp.where(dr == 0, s, ns - s))
    lg = lg_ref[...]
    fwd = dr == 0
    row_h = lax.broadcasted_iota(jnp.int32, (w, w), 0) // HEAD_DIM
    col_h = lax.broadcasted_iota(jnp.int32, (w, w), 1) // HEAD_DIM
    same_head = row_h == col_h

    @pl.when(s == 0)
    def _():
        s_ref[...] = jnp.zeros_like(s_ref)
        i = lax.broadcasted_iota(jnp.int32, (c, c), 0)
        j = lax.broadcasted_iota(jnp.int32, (c, c), 1)
        rel = jnp.where(fwd, i - j, j - i)
        relf = jnp.maximum(rel, 0).astype(F32)
        for h in range(RET_HEADS):
            lg_h = lg_ref[0:1, HEAD_DIM * h:HEAD_DIM * h + 1]
            dec_ref[h] = jnp.where(rel >= 0, jnp.exp(lg_h * relf), 0.0)

    q = x_ref[:, 0:w]
    k = x_ref[:, w:2 * w]
    v = x_ref[:, 2 * w:3 * w].astype(BF16)
    kb = k.astype(BF16)
    pos = lax.broadcasted_iota(jnp.int32, (c, 1), 0).astype(F32)
    q_pow = jnp.where(fwd, pos + 1.0, c - pos)
    k_pow = jnp.where(fwd, c - 1.0 - pos, pos)
    lane_h = lax.broadcasted_iota(jnp.int32, (c, w), 1) // HEAD_DIM
    acc = _dot((q * jnp.exp(lg * q_pow)).astype(BF16), s_ref[...].astype(BF16))
    for h in range(RET_HEADS):
        qh = jnp.where(lane_h == h, q, 0.0).astype(BF16)
        a = _dot_nt(qh, kb) * dec_ref[h]
        acc = acc + jnp.where(lane_h == h, _dot(a.astype(BF16), v), 0.0)
    kv = _dot_tn((k * jnp.exp(lg * k_pow)).astype(BF16), v)
    s_ref[...] = s_ref[...] * jnp.exp(lg * float(c)) + jnp.where(same_head, kv, 0.0)

    rows = pl.ds(pl.multiple_of(chunk * c, c), c)

    @pl.when(dr == 0)
    def _():
        o_ref[rows, :] = acc

    @pl.when(dr == 1)
    def _():
        o = o_ref[rows, :] + acc
        ones_bd = jnp.where(same_head, 1.0, 0.0).astype(BF16)
        mu = _head_mean(o, ones_bd)
        var = _head_mean(jnp.square(o - mu), ones_bd)
        y = (o - mu) * lax.rsqrt(var + EPS) * g_ref[...]
        o_ref[rows, :] = y * _silu(x_ref[:, 3 * w:4 * w])


def retention_mixer(pd, log_decay, norm_g):
    b, l, _ = pd.shape
    c = ROW_TILE
    ns = l // c
    w = RET_HEADS * HEAD_DIM
    lg = jnp.repeat(-jnp.exp(log_decay.astype(F32)), HEAD_DIM, axis=-1)[:, None, :]

    def chunk_of(dr, s):
        return jnp.where(s == 0, 0, jnp.where(dr == 0, s, ns - s))

    return pl.pallas_call(
        _ret_kernel,
        grid=(b, 2, ns),
        in_specs=[pl.BlockSpec((None, c, D_W), lambda i, dr, s: (i, chunk_of(dr, s), 0)),
                  pl.BlockSpec((None, 1, w), lambda i, dr, s: (dr, 0, 0)),
                  pl.BlockSpec((1, w), lambda i, dr, s: (0, 0))],
        out_specs=pl.BlockSpec((None, l, w), lambda i, dr, s: (i, 0, 0)),
        out_shape=jax.ShapeDtypeStruct((b, l, w), F32),
        scratch_shapes=[pltpu.VMEM((w, w), F32), pltpu.VMEM((RET_HEADS, c, c), F32)],
        compiler_params=pltpu.CompilerParams(
            dimension_semantics=("parallel", "arbitrary", "arbitrary"), vmem_limit_bytes=VMEM_LIMIT),
        name="retention",
    )(pd, lg, norm_g[None, :])


def _out_proj_kernel(ma_ref, mb_ref, mc_ref, md_ref, h_ref, mod_ref, g_ref, w_ref, *rest, with_router):
    if with_router:
        wr_ref, hn_ref, v_ref, lg_ref = rest
    else:
        hn_ref, v_ref = rest
    gw = 2 * LANES
    mix = functools.reduce(jnp.add, [
        _dot(m_ref[...].astype(BF16), w_ref[gw * i:gw * (i + 1), :])
        for i, m_ref in enumerate((ma_ref, mb_ref, mc_ref, md_ref))])
    hn = h_ref[...] + mod_ref[2:3, :] * mix
    hn_ref[...] = hn
    v = _rms(hn) * g_ref[...] * (1.0 + mod_ref[4:5, :]) + mod_ref[3:4, :]
    v_ref[...] = v.astype(BF16)
    if with_router:
        lg_ref[...] = jnp.dot(v, wr_ref[...], preferred_element_type=F32, precision=lax.Precision.HIGHEST)


def out_proj(mixes, h, mod, gain, w, w_router=None):
    b, l, d = h.shape
    tm = ROW_TILE
    with_router = w_router is not None
    row = lambda i, t: (i, t, 0)
    in_specs = [pl.BlockSpec((None, tm, 2 * LANES), row) for _ in mixes] + [
        pl.BlockSpec((None, tm, d), row),
        _mod_spec(d),
        pl.BlockSpec((1, d), lambda i, t: (0, 0)),
        pl.BlockSpec(w.shape, lambda i, t: (0, 0), pipeline_mode=pl.Buffered(1)),
    ]
    out_specs = [pl.BlockSpec((None, tm, d), row), pl.BlockSpec((None, tm, d), row)]
    out_shape = [jax.ShapeDtypeStruct((b, l, d), F32), jax.ShapeDtypeStruct((b, l, d), BF16)]
    args = list(mixes) + [h, mod, gain, w]
    if with_router:
        in_specs.append(pl.BlockSpec(w_router.shape, lambda i, t: (0, 0)))
        out_specs.append(pl.BlockSpec((None, tm, LANES), row))
        out_shape.append(jax.ShapeDtypeStruct((b, l, LANES), F32))
        args.append(w_router)
    return pl.pallas_call(
        functools.partial(_out_proj_kernel, with_router=with_router),
        grid=(b, l // tm),
        in_specs=in_specs,
        out_specs=out_specs,
        out_shape=out_shape,
        compiler_params=pltpu.CompilerParams(
            dimension_semantics=("parallel", "parallel"), vmem_limit_bytes=VMEM_LIMIT),
        name="out_proj",
    )(*args)


def build_out_weight(w):
    hd = HEAD_DIM
    perm = jnp.concatenate([w[0:hd], w[2 * hd:3 * hd], w[hd:2 * hd], w[3 * hd:4 * hd]], axis=0)
    return jnp.concatenate([perm, w[4 * hd:]], axis=0).astype(BF16)


def _ffn_kernel(v_ref, h_ref, mod_ref, wg_ref, wu_ref, wd_ref, o_ref):
    v = v_ref[...]
    acc = jnp.zeros(o_ref.shape, F32)
    for j in range(D_FF // FF_CHUNK):
        cols = slice(j * FF_CHUNK, (j + 1) * FF_CHUNK)
        a = _dot(v, wg_ref[:, cols])
        u = _dot(v, wu_ref[:, cols])
        mid = (_silu(a) * u).astype(BF16)
        acc = acc + _dot(mid, wd_ref[cols, :])
    o_ref[...] = h_ref[...] + mod_ref[5:6, :] * acc


def dense_ffn(v, h, mod, wg, wu, wd):
    b, l, d = h.shape
    tm = ROW_TILE
    row = lambda i, t: (i, t, 0)
    const = lambda i, t: (0, 0)
    return pl.pallas_call(
        _ffn_kernel,
        grid=(b, l // tm),
        in_specs=[
            pl.BlockSpec((None, tm, d), row),
            pl.BlockSpec((None, tm, d), row),
            _mod_spec(d),
            pl.BlockSpec(wg.shape, const, pipeline_mode=pl.Buffered(1)),
            pl.BlockSpec(wu.shape, const, pipeline_mode=pl.Buffered(1)),
            pl.BlockSpec(wd.shape, const, pipeline_mode=pl.Buffered(1)),
        ],
        out_specs=pl.BlockSpec((None, tm, d), row),
        out_shape=jax.ShapeDtypeStruct((b, l, d), F32),
        compiler_params=pltpu.CompilerParams(
            dimension_semantics=("parallel", "parallel"), vmem_limit_bytes=VMEM_LIMIT),
        name="dense_ffn",
    )(v, h, mod, wg, wu, wd)


def _moe_kernel(te_ref, tv_ref, x_ref, gate_ref, wg_ref, wu_ref, wd_ref, o_ref, acc_ref):
    i = pl.program_id(0)
    j = pl.program_id(1)
    nj = pl.num_programs(1)

    @pl.when(j == 0)
    def _():
        acc_ref[...] = jnp.zeros_like(acc_ref)

    @pl.when(tv_ref[i] > 0)
    def _():
        x = x_ref[...]
        a = _dot(x, wg_ref[...])
        u = _dot(x, wu_ref[...])
        mid = (_silu(a) * u).astype(BF16)
        acc_ref[...] += _dot(mid, wd_ref[...])

    @pl.when(j == nj - 1)
    def _():
        o_ref[...] = gate_ref[...] * acc_ref[...]


def moe_grouped_ffn(xs, gate_sorted, tile_expert, tile_valid, wg, wu, wd):
    p, d = xs.shape
    tm = MOE_TILE
    nt = p // tm
    nj = D_FF // FF_CHUNK
    grid_spec = pltpu.PrefetchScalarGridSpec(
        num_scalar_prefetch=2,
        grid=(nt, nj),
        in_specs=[
            pl.BlockSpec((tm, d), lambda i, j, te, tv: (i, 0)),
            pl.BlockSpec((tm, 1), lambda i, j, te, tv: (i, 0)),
            pl.BlockSpec((None, d, FF_CHUNK), lambda i, j, te, tv: (te[i], 0, j)),
            pl.BlockSpec((None, d, FF_CHUNK), lambda i, j, te, tv: (te[i], 0, j)),
            pl.BlockSpec((None, FF_CHUNK, d), lambda i, j, te, tv: (te[i], j, 0)),
        ],
        out_specs=pl.BlockSpec((tm, d), lambda i, j, te, tv: (i, 0)),
        scratch_shapes=[pltpu.VMEM((tm, d), F32)],
    )
    return pl.pallas_call(
        _moe_kernel,
        grid_spec=grid_spec,
        out_shape=jax.ShapeDtypeStruct((p, d), F32),
        compiler_params=pltpu.CompilerParams(
            dimension_semantics=("parallel", "arbitrary"), vmem_limit_bytes=VMEM_LIMIT),
        name="moe_ffn",
    )(tile_expert, tile_valid, xs, gate_sorted, wg, wu, wd)


def _residual_kernel(h_ref, f_ref, mod_ref, o_ref):
    d = h_ref.shape[1]
    o_ref[...] = h_ref[...] + mod_ref[5:6, :] * (f_ref[:, :d] + f_ref[:, d:])


def gated_residual(h, f, mod):
    b, l, d = h.shape
    tm = ROW_TILE
    row = lambda i, t: (i, t, 0)
    return pl.pallas_call(
        _residual_kernel,
        grid=(b, l // tm),
        in_specs=[pl.BlockSpec((None, tm, d), row), pl.BlockSpec((None, tm, TOP_K * d), row), _mod_spec(d)],
        out_specs=pl.BlockSpec((None, tm, d), row),
        out_shape=jax.ShapeDtypeStruct((b, l, d), F32),
        compiler_params=pltpu.CompilerParams(dimension_semantics=("parallel", "parallel")),
        name="gated_residual",
    )(h, f, mod)


def moe_ffn(v, logits, h, mod, wg, wu, wd):
    b, l, d = h.shape
    t = b * l
    tm = MOE_TILE
    p = TOP_K * t + N_EXPERTS * tm
    nt = p // tm
    lg = logits.reshape(t, LANES)[:, :N_EXPERTS]
    top_val, top_idx = lax.top_k(lg, TOP_K)
    gates = jax.nn.softmax(top_val, axis=-1)
    e_flat = top_idx.reshape(-1).astype(jnp.int32)
    onehot = (e_flat[:, None] == jnp.arange(N_EXPERTS, dtype=jnp.int32)[None, :]).astype(jnp.int32)
    rank = jnp.sum((jnp.cumsum(onehot, axis=0) - 1) * onehot, axis=1)
    counts = jnp.sum(onehot, axis=0)
    padded = ((counts + tm - 1) // tm) * tm
    ends = jnp.cumsum(padded)
    starts = ends - padded
    dest = starts[e_flat] + rank
    token = jnp.arange(TOP_K * t, dtype=jnp.int32) // TOP_K
    src = jnp.zeros((p,), jnp.int32).at[dest].set(token)
    gate_sorted = jnp.zeros((p,), F32).at[dest].set(gates.reshape(-1))
    tile_start = jnp.arange(nt, dtype=jnp.int32) * tm
    tile_expert = jnp.minimum(jnp.searchsorted(ends, tile_start, side="right"), N_EXPERTS - 1).astype(jnp.int32)
    tile_valid = (tile_start < ends[-1]).astype(jnp.int32)
    v32 = lax.bitcast_convert_type(v.reshape(t, d // 2, 2), jnp.uint32)
    xs = lax.bitcast_convert_type(jnp.take(v32, src, axis=0), BF16).reshape(p, d)
    ys = moe_grouped_ffn(xs, gate_sorted[:, None], tile_expert, tile_valid, wg, wu, wd)
    f = jnp.take(ys, dest, axis=0)
    return gated_residual(h, f.reshape(b, l, TOP_K * d), mod)


def _final_norm_kernel(h_ref, g_ref, o_ref):
    o_ref[...] = _rms(h_ref[...]) * g_ref[...]


def final_rms_norm(h, gain, n_ctx_tiles):
    b, l, d = h.shape
    tm = ROW_TILE
    n = l - n_ctx_tiles * tm
    return pl.pallas_call(
        _final_norm_kernel,
        grid=(b, n // tm),
        in_specs=[
            pl.BlockSpec((None, tm, d), lambda i, t: (i, t + n_ctx_tiles, 0)),
            pl.BlockSpec((1, d), lambda i, t: (0, 0)),
        ],
        out_specs=pl.BlockSpec((None, tm, d), lambda i, t: (i, t, 0)),
        out_shape=jax.ShapeDtypeStruct((b, n, d), F32),
        compiler_params=pltpu.CompilerParams(dimension_semantics=("parallel", "parallel")),
        name="final_norm",
    )(h, gain)


GDN_W = GDN_HEADS * GDN_DK
GDN_CONV_K = 5
GDN_HALO = SUBLANES


def _split3(x):
    p0 = x.astype(BF16)
    r1 = x - p0.astype(F32)
    p1 = r1.astype(BF16)
    p2 = (r1 - p1.astype(F32)).astype(BF16)
    return p0, p1, p2


def _gdn_prep_kernel(x_ref, prev_ref, next_ref, cw_ref, par_ref, q_ref, k_ref, v_ref, gb_ref):
    t = pl.program_id(1)
    last = pl.num_programs(1) - 1
    tm = x_ref.shape[0]
    w3 = 3 * GDN_W
    has_prev = t > 1
    has_next = (t > 0) & (t < last)
    prev = jnp.where(has_prev, prev_ref[...], 0.0)
    nxt = jnp.where(has_next, next_ref[...], 0.0)
    xe = jnp.concatenate([prev, x_ref[:, :w3], nxt], axis=0)
    y = jnp.zeros((tm, w3), F32)
    for j in range(GDN_CONV_K):
        lo = GDN_HALO - GDN_CONV_K // 2 + j
        y = y + cw_ref[j:j + 1, :] * xe[lo:lo + tm, :]
    y = _silu(y)
    r = lax.broadcasted_iota(jnp.int32, (GDN_W, GDN_W), 0)
    c = lax.broadcasted_iota(jnp.int32, (GDN_W, GDN_W), 1)
    ones_bd = jnp.where(r // GDN_DK == c // GDN_DK, 1.0, 0.0).astype(BF16)

    def l2n(x):
        sq = x * x
        hi = sq.astype(BF16)
        lo = (sq - hi.astype(F32)).astype(BF16)
        return x * lax.rsqrt(_dot(hi, ones_bd) + _dot(lo, ones_bd) + EPS)

    q_ref[...] = l2n(y[:, :GDN_W]) * GDN_DK ** -0.5
    k_ref[...] = l2n(y[:, GDN_W:2 * GDN_W])
    v_ref[...] = y[:, 2 * GDN_W:]
    ab = x_ref[:, w3 + GDN_W:]
    lane = lax.broadcasted_iota(jnp.int32, ab.shape, 1)
    is_g = (lane % 8) < 4
    z = ab + par_ref[1:2, :]
    softplus = jnp.maximum(z, 0.0) + jnp.log1p(jnp.exp(-jnp.abs(z)))
    g = jnp.where(is_g, par_ref[0:1, :] * softplus, 0.0)
    beta = 1.0 / (1.0 + jnp.exp(-ab))
    i = lax.broadcasted_iota(jnp.int32, (tm, tm), 0)
    j = lax.broadcasted_iota(jnp.int32, (tm, tm), 1)
    same_chunk = i // GDN_CHUNK == j // GDN_CHUNK
    tri_f = jnp.where(same_chunk & (j <= i), 1.0, 0.0).astype(BF16)
    tri_b = jnp.where(same_chunk & (j >= i), 1.0, 0.0).astype(BF16)
    pieces = _split3(g)
    gc_f = functools.reduce(jnp.add, [_dot(tri_f, p) for p in pieces])
    gc_b = functools.reduce(jnp.add, [_dot(tri_b, p) for p in pieces])
    gb_ref[...] = jnp.where(is_g, jnp.where(lane < 8, gc_f, gc_b), beta)


def gdn_prep(pb, conv_w, a_log, dt_bias):
    b, l, _ = pb.shape
    tm = ROW_TILE
    w3 = 3 * GDN_W
    halo_blocks = tm // GDN_HALO
    n_halo = l // GDN_HALO
    cw = jnp.pad(conv_w, ((0, SUBLANES - GDN_CONV_K), (0, 0)))
    neg_a = jnp.pad(-jnp.exp(a_log.astype(F32)), ((0, 0), (0, 4))).reshape(-1)
    dtb = jnp.pad(dt_bias.astype(F32), ((0, 0), (0, 4))).reshape(-1)
    par = jnp.pad(jnp.stack([neg_a, dtb]), ((0, SUBLANES - 2), (0, LANES - 16)))
    row = lambda i, t: (i, t, 0)
    out = lambda w: pl.BlockSpec((None, tm, w), row)
    return pl.pallas_call(
        _gdn_prep_kernel,
        grid=(b, l // tm),
        in_specs=[pl.BlockSpec((None, tm, B_W), row),
                  pl.BlockSpec((None, GDN_HALO, w3), lambda i, t: (i, jnp.maximum(t * halo_blocks - 1, 0), 0)),
                  pl.BlockSpec((None, GDN_HALO, w3),
                               lambda i, t: (i, jnp.minimum((t + 1) * halo_blocks, n_halo - 1), 0)),
                  pl.BlockSpec(cw.shape, lambda i, t: (0, 0)),
                  pl.BlockSpec(par.shape, lambda i, t: (0, 0))],
        out_specs=[out(GDN_W), out(GDN_W), out(GDN_W), out(LANES)],
        out_shape=[jax.ShapeDtypeStruct((b, l, GDN_W), F32)] * 3 + [jax.ShapeDtypeStruct((b, l, LANES), F32)],
        compiler_params=pltpu.CompilerParams(
            dimension_semantics=("parallel", "parallel"), vmem_limit_bytes=VMEM_LIMIT),
        name="gdn_prep",
    )(pb, pb, pb, cw, par)


def _tile_heads(x):
    return jnp.concatenate([x] * GDN_HEADS, axis=0)


def _collapse_heads(x):
    c = GDN_CHUNK
    return x[0:c] + x[c:2 * c] + x[2 * c:3 * c] + x[3 * c:4 * c]


def _gdn_chunk_kernel(q_ref, k_ref, v_ref, gb_ref, o0_ref, qe_ref, a_ref, bm_ref, gam_ref):
    dr = pl.program_id(1)
    fwd = dr == 0
    n = GDN_W
    cs = GDN_CHUNK
    r = lax.broadcasted_iota(jnp.int32, (n, n), 0)
    c = lax.broadcasted_iota(jnp.int32, (n, n), 1)
    ri, ci = r % cs, c % cs
    head = r // cs == c // cs
    ahead = (ri - ci) * jnp.where(fwd, 1, -1)
    tri = head & (ahead >= 0)
    tri_strict = head & (ahead > 0)
    eye = jnp.where(r == c, 1.0, 0.0)
    blk = lambda s: r // s == c // s
    b8, b16, b32 = blk(8), blk(16), blk(32)
    lane = lax.broadcasted_iota(jnp.int32, (n, LANES), 1)
    row_head = lax.broadcasted_iota(jnp.int32, (n, LANES), 0) // cs
    sel_g = lane == dr * 8 + row_head
    sel_b = lane == dr * 8 + 4 + row_head
    pick = lambda sel, x: jnp.sum(jnp.where(sel, x, 0.0), axis=1, keepdims=True)
    mm = lambda a, b: _dot(a.astype(BF16), b.astype(BF16))

    rows = [slice(ch * cs, (ch + 1) * cs) for ch in range(q_ref.shape[0] // cs)]
    each = lambda f, *xs: [f(*a) for a in zip(*xs)]
    bf = lambda xs: [x.astype(BF16) for x in xs]
    spread = lambda ref: [jnp.where(head, _tile_heads(ref[rw, :]), 0.0) for rw in rows]
    kh, qh, vh = spread(k_ref), spread(q_ref), spread(v_ref)
    gb = [gb_ref[rw, :] for rw in rows]
    gb4 = [_tile_heads(x) for x in gb]
    gc = [pick(sel_g, x) for x in gb4]
    beta = [pick(sel_b, x) for x in gb4]
    gl = [pick(sel_g, jnp.broadcast_to(jnp.where(fwd, x[cs - 1:cs, :], x[0:1, :]), (n, LANES))) for x in gb]
    gc_b = [jnp.broadcast_to(x, (n, n)) for x in gc]
    decay = [jnp.exp(jnp.minimum(x - x.T, 0.0)) for x in gc_b]
    khb, qhb = bf(kh), bf(qh)
    kk = each(_dot_nt, khb, khb)
    qk = each(_dot_nt, qhb, khb)
    lmat = each(lambda b_, kk_, d_: jnp.where(tri_strict, b_ * kk_ * d_, 0.0), beta, kk, decay)
    attn = bf(each(lambda qk_, d_: jnp.where(tri, qk_ * d_, 0.0), qk, decay))
    nl = bf([jnp.where(b8, -x, 0.0) for x in lmat])
    n2 = bf(each(_dot, nl, nl))
    n4 = each(_dot, n2, n2)
    p1 = bf(each(lambda a, b_: _dot((eye + a).astype(BF16), (eye + b_).astype(BF16)), nl, n2))
    tinv = each(lambda p, x: _dot(p, (eye + x).astype(BF16)), p1, n4)
    for inner, outer in ((b8, b16), (b16, b32), (b32, head)):
        off = bf([jnp.where(outer & ~inner, x, 0.0) for x in lmat])
        tb = bf(tinv)
        to = bf(each(_dot, tb, off))
        tinv = each(lambda t_, to_, tb_: t_ - _dot(to_, tb_), tinv, to, tb)
    tb = bf(tinv)
    eg = [jnp.exp(x) for x in gc]
    u = bf(each(lambda t_, b_, v_: _dot(t_, (b_ * v_).astype(BF16)), tb, beta, vh))
    w = bf(each(lambda t_, b_, e_, k_: _dot(t_, ((b_ * e_) * k_).astype(BF16)), tb, beta, eg, kh))
    o0 = each(_dot, attn, u)
    qe = each(lambda q_, e_, a_, w_: q_ * e_ - _dot(a_, w_), qh, eg, attn, w)
    kg = bf(each(lambda k_, gl_, gc_: k_ * jnp.exp(gl_ - gc_), kh, gl, gc))
    a_mat = each(_dot_tn, kg, w)
    b_mat = each(_dot_tn, kg, u)
    for rw, o0_, qe_, a_, b_, gl_ in zip(rows, o0, qe, a_mat, b_mat, gl):
        o0_ref[rw, :] = _collapse_heads(o0_)
        qe_ref[rw, :] = _collapse_heads(qe_)
        a_ref[rw, :] = _collapse_heads(a_)
        bm_ref[rw, :] = _collapse_heads(b_)
        gam_ref[rw, :] = _collapse_heads(jnp.where(head, jnp.broadcast_to(jnp.exp(gl_), (n, n)), 0.0))


def gdn_chunks(q, k, v, gb):
    b, l, _ = q.shape
    tm = ROW_TILE
    row = lambda i, dr, t: (i, t, 0)
    out = pl.BlockSpec((None, None, tm, GDN_W), lambda i, dr, t: (dr, i, t, 0))
    return pl.pallas_call(
        _gdn_chunk_kernel,
        grid=(b, 2, l // tm),
        in_specs=[pl.BlockSpec((None, tm, GDN_W), row)] * 3 + [pl.BlockSpec((None, tm, LANES), row)],
        out_specs=[out] * 5,
        out_shape=[jax.ShapeDtypeStruct((2, b, l, GDN_W), F32)] * 5,
        compiler_params=pltpu.CompilerParams(
            dimension_semantics=("parallel", "parallel", "parallel"), vmem_limit_bytes=VMEM_LIMIT),
        name="gdn_chunk",
    )(q, k, v, gb)


def _gdn_scan_kernel(o0_ref, qe_ref, a_ref, bm_ref, gam_ref, gate_ref, g_ref, o_ref, s_ref):
    dr = pl.program_id(1)
    s = pl.program_id(2)
    ns = pl.num_programs(2)
    tm = o0_ref.shape[0]
    cs = GDN_CHUNK
    n = GDN_W
    nch = tm // cs
    tile = jnp.where(s == 0, 0, jnp.where(dr == 0, s, ns - s))
    r = lax.broadcasted_iota(jnp.int32, (n, n), 0)
    c = lax.broadcasted_iota(jnp.int32, (n, n), 1)
    head = r // cs == c // cs

    @pl.when(s == 0)
    def _():
        s_ref[...] = jnp.zeros_like(s_ref)

    def run(order):
        state = s_ref[...]
        outs = {}
        for ch in order:
            rows = slice(ch * cs, (ch + 1) * cs)
            sb = state.astype(BF16)
            outs[ch] = o0_ref[rows, :] + _dot(qe_ref[rows, :].astype(BF16), sb)
            a_full = jnp.where(head, _tile_heads(a_ref[rows, :]), 0.0).astype(BF16)
            b_full = jnp.where(head, _tile_heads(bm_ref[rows, :]), 0.0)
            state = _tile_heads(gam_ref[rows, :]) * state - _dot(a_full, sb) + b_full
        s_ref[...] = state
        return jnp.concatenate([outs[ch] for ch in range(nch)], axis=0)

    rows_out = pl.ds(pl.multiple_of(tile * tm, tm), tm)

    @pl.when(dr == 0)
    def _():
        o_ref[rows_out, :] = run(range(nch))

    @pl.when(dr == 1)
    def _():
        o = o_ref[rows_out, :] + run(range(nch - 1, -1, -1))
        ones_bd = jnp.where(head, 1.0, 0.0).astype(BF16)
        ms = _head_mean(o * o, ones_bd)
        o_ref[rows_out, :] = o * lax.rsqrt(ms + EPS) * g_ref[...] * _silu(gate_ref[...])


def gdn_scan(o0, qe, a, bm, gam, pb, norm_g):
    _, b, l, _ = o0.shape
    tm = ROW_TILE
    ns = l // tm

    def tile_of(dr, s):
        return jnp.where(s == 0, 0, jnp.where(dr == 0, s, ns - s))

    per_dir = pl.BlockSpec((None, None, tm, GDN_W), lambda i, dr, s: (dr, i, tile_of(dr, s), 0))
    gate_col = 3 * GDN_W // GDN_W
    return pl.pallas_call(
        _gdn_scan_kernel,
        grid=(b, 2, ns),
        in_specs=[per_dir] * 5 + [
            pl.BlockSpec((None, tm, GDN_W), lambda i, dr, s: (i, tile_of(dr, s), gate_col)),
            pl.BlockSpec((1, GDN_W), lambda i, dr, s: (0, 0))],
        out_specs=pl.BlockSpec((None, l, GDN_W), lambda i, dr, s: (i, 0, 0)),
        out_shape=jax.ShapeDtypeStruct((b, l, GDN_W), F32),
        scratch_shapes=[pltpu.VMEM((GDN_W, GDN_W), F32)],
        compiler_params=pltpu.CompilerParams(
            dimension_semantics=("parallel", "arbitrary", "arbitrary"), vmem_limit_bytes=VMEM_LIMIT),
        name="gdn_scan",
    )(o0, qe, a, bm, gam, pb, jnp.tile(norm_g, GDN_HEADS)[None, :])


def gdn_mixer(pb, conv_w, a_log, dt_bias, norm_g):
    q, k, v, gb = gdn_prep(pb, conv_w, a_log, dt_bias)
    o0, qe, a, bm, gam = gdn_chunks(q, k, v, gb)
    return gdn_scan(o0, qe, a, bm, gam, pb, norm_g)


def rms_norm(x, g):
    xf = x.astype(F32)
    y = xf * lax.rsqrt(jnp.mean(xf * xf, axis=-1, keepdims=True) + EPS)
    return (y * g.astype(F32)).astype(x.dtype)


def l2_normalize(t):
    return t * lax.rsqrt(jnp.sum(t * t, axis=-1, keepdims=True) + EPS)


def directional_scan(chunked_fn, seqs, consts, s0, reverse):
    if reverse:
        seqs = tuple(jnp.flip(t, axis=1) for t in seqs)
    o, s = chunked_fn(*seqs, *consts, s0)
    if reverse:
        o = jnp.flip(o, axis=1)
    return o, s


def prefix_bidirectional_scan(chunked_fn, seqs_c, seqs_x, consts, state_shape):
    out_c, out_x = 0.0, 0.0
    for d in range(2):
        rev = d == 1
        s0 = jnp.zeros(state_shape, F32)
        oc, s_ctx = directional_scan(chunked_fn, seqs_c[d], consts[d], s0, rev)
        ox, _ = directional_scan(chunked_fn, seqs_x[d], consts[d], s_ctx, rev)
        out_c = out_c + oc
        out_x = out_x + ox
    return out_c, out_x


def short_conv(x, w):
    k = w.shape[0]
    y = lax.conv_general_dilated(
        x, w[:, None, :].astype(x.dtype), window_strides=(1,), padding=[(k // 2, k // 2)],
        dimension_numbers=('NWC', 'WIO', 'NWC'), feature_group_count=x.shape[-1])
    return jax.nn.silu(y)


def gated_delta_chunked(q, k, v, g, beta, s0):
    b, l, h, _ = q.shape
    dv = v.shape[-1]
    cs = GDN_CHUNK
    nc = l // cs

    def chunks(t):
        return t.reshape(b, nc, cs, h, -1).transpose(1, 0, 3, 2, 4)

    qc, kc, vc = chunks(q), chunks(k), chunks(v)
    gc = jnp.cumsum(chunks(g[..., None])[..., 0], axis=-1)
    bc = chunks(beta[..., None])
    idx = jnp.arange(cs)
    lower = idx[:, None] >= idx[None, :]
    strict = idx[:, None] > idx[None, :]
    decay = jnp.exp(jnp.where(lower, gc[..., :, None] - gc[..., None, :], NEG_INF))
    kb = kc * bc
    lmat = jnp.where(strict, jnp.einsum('nbhid,nbhjd->nbhij', kb, kc) * decay, 0.0)
    a_mat = lmat + jnp.eye(cs, dtype=F32)
    u = lax.linalg.triangular_solve(a_mat, vc * bc, left_side=True, lower=True)
    w = lax.linalg.triangular_solve(a_mat, kb * jnp.exp(gc)[..., None], left_side=True, lower=True)
    attn = jnp.einsum('nbhid,nbhjd->nbhij', qc, kc) * decay

    def step(s, xs):
        q_i, k_i, u_i, w_i, a_i, g_i = xs
        v_new = u_i - jnp.einsum('bhck,bhkv->bhcv', w_i, s)
        o_i = (jnp.einsum('bhck,bhkv->bhcv', q_i * jnp.exp(g_i)[..., None], s)
               + jnp.einsum('bhij,bhjv->bhiv', a_i, v_new))
        g_last = g_i[..., -1:]
        s = (s * jnp.exp(g_last)[..., None]
             + jnp.einsum('bhck,bhcv->bhkv', k_i * jnp.exp(g_last - g_i)[..., None], v_new))
        return s, o_i

    s_fin, o = lax.scan(step, s0, (qc, kc, u, w, attn, gc))
    return o.transpose(1, 0, 3, 2, 4).reshape(b, l, h, dv), s_fin


def gdn_mixer_jax(pb, conv_w, a_log, dt_bias, norm_g):
    def prep(qkv, ab):
        b, l, _ = qkv.shape
        qkv = short_conv(qkv, conv_w).astype(F32)
        q, k, v = jnp.split(qkv, [GDN_HEADS * GDN_DK, 2 * GDN_HEADS * GDN_DK], axis=-1)
        q = l2_normalize(q.reshape(b, l, GDN_HEADS, GDN_DK)) * GDN_DK ** -0.5
        k = l2_normalize(k.reshape(b, l, GDN_HEADS, GDN_DK))
        v = v.reshape(b, l, GDN_HEADS, GDN_DV)
        ab = ab.astype(F32).reshape(b, l, 2, 2, GDN_HEADS)
        g = -jnp.exp(a_log.astype(F32)) * jax.nn.softplus(ab[:, :, :, 0] + dt_bias.astype(F32))
        beta = jax.nn.sigmoid(ab[:, :, :, 1])
        return [(q, k, v, g[:, :, d], beta[:, :, d]) for d in range(2)]

    def parts(rows):
        return rows[..., :768], rows[..., 768:1024], rows[..., 1024:1040]

    qkv_c, gate_c, ab_c = parts(pb[:, :CTX_LEN])
    qkv_x, gate_x, ab_x = parts(pb[:, CTX_LEN:])
    s_shape = (qkv_x.shape[0], GDN_HEADS, GDN_DK, GDN_DV)
    o_c, o_x = prefix_bidirectional_scan(gated_delta_chunked, prep(qkv_c, ab_c), prep(qkv_x, ab_x),
                                         [(), ()], s_shape)

    def gated_out(o, gate):
        b, l = gate.shape[:2]
        y = rms_norm(o, norm_g) * jax.nn.silu(gate.astype(F32)).reshape(o.shape)
        return y.reshape(b, l, GDN_HEADS * GDN_DV).astype(BF16)

    return jnp.concatenate([gated_out(o_c, gate_c), gated_out(o_x, gate_x)], axis=1)


def kernel(x, c, ctx, c_ctx, w_mod, b_mod, norm1, norm2, w_in, w_out, swa_sink, gdn_conv, gdn_a_log, gdn_dt_bias, gdn_norm, mla_q_norm, mla_kv_norm, mla_w_q_up, mla_w_kv_up, ret_log_decay, ret_norm, ffn_w_gate, ffn_w_up, ffn_w_down, moe_router, moe_w_gate, moe_w_up, moe_w_down, final_norm):
    b, n, d = x.shape
    depth = w_in.shape[0]
    cos_t, sin_t = rope_tables(n)
    cos_c, sin_c = cos_t[:, A_ROT_W:A_ROT_W + C_ROT_W], sin_t[:, A_ROT_W:A_ROT_W + C_ROT_W]
    silu_c = jax.nn.silu(c)
    silu_cc = jax.nn.silu(c_ctx)
    h = jnp.concatenate([ctx, x], axis=1)
    for layer in range(depth):
        mod_x = jnp.dot(silu_c, w_mod[layer], precision=lax.Precision.HIGHEST) + b_mod[layer]
        mod_c = jnp.dot(silu_cc, w_mod[layer], precision=lax.Precision.HIGHEST) + b_mod[layer]
        mod = jnp.stack([jnp.broadcast_to(mod_c, mod_x.shape), mod_x], axis=1).reshape(b, 2, 6, d)
        mod = jnp.pad(mod, ((0, 0), (0, 0), (0, SUBLANES - 6), (0, 0)))
        pa, pb, pc, pd = norm_proj(h, mod, norm1[layer][None, :], build_in_weight(w_in[layer]), cos_t, sin_t)
        mix_a = swa_mixer(pa, swa_sink[layer])
        mix_b = gdn_mixer(pb, gdn_conv[layer], gdn_a_log[layer], gdn_dt_bias[layer], gdn_norm[layer])
        mq, mk, mv = mla_prep(pc, mla_q_norm[layer], mla_kv_norm[layer], mla_w_q_up[layer], mla_w_kv_up[layer],
                              cos_c, sin_c)
        mix_c = mla_attention(mq, mk, mv)
        mix_d = retention_mixer(pd, ret_log_decay[layer], ret_norm[layer])
        mixes = (mix_a, mix_b, mix_c, mix_d)
        w_o = build_out_weight(w_out[layer])
        i = layer // 2
        if layer % 2 == 0:
            h, v = out_proj(mixes, h, mod, norm2[layer][None, :], w_o)
            h = dense_ffn(v, h, mod, ffn_w_gate[i].astype(BF16), ffn_w_up[i].astype(BF16),
                          ffn_w_down[i].astype(BF16))
        else:
            w_r = jnp.pad(moe_router[i], ((0, 0), (0, LANES - N_EXPERTS)))
            h, v, logits = out_proj(mixes, h, mod, norm2[layer][None, :], w_o, w_r)
            h = moe_ffn(v, logits, h, mod, moe_w_gate[i].astype(BF16), moe_w_up[i].astype(BF16),
                        moe_w_down[i].astype(BF16))
    return final_rms_norm(h, final_norm[None, :], CTX_LEN // ROW_TILE)
```

```python
import functools

import numpy as np
import jax
import jax.numpy as jnp
from jax import lax
from jax.experimental import pallas as pl
from jax.experimental.pallas import tpu as pltpu

D_MODEL = 1024
GRID_W = 64
CTX_LEN = 256
HEAD_DIM = 64
ROPE_THETA = 10000.0
EPS = 1e-6
NEG_INF = -1e30

SWA_WINDOW = 128
GDN_HEADS = 4
GDN_DK = 64
GDN_DV = 64
GDN_CHUNK = 64
MLA_HEADS = 4
MLA_NOPE = 64
MLA_ROPE = 32
MLA_V = 64
RET_HEADS = 4
RET_DK = 64
D_FF = 3584
N_EXPERTS = 8
TOP_K = 2

LANES = 128
SUBLANES = 8
VMEM_LIMIT = 56 * 1024 * 1024

ROW_TILE = 256
FF_CHUNK = 512
MOE_TILE = 512

A_W, B_W, C_W, D_W = 768, 1152, 512, 1024
A_ROT_W, C_ROT_W, D_ROT_W = 640, 128, 512
OFF_A = 0
OFF_B = OFF_A + A_W
OFF_C = OFF_B + B_W
OFF_D = OFF_C + C_W
OFF_AR = OFF_D + D_W
OFF_CR = OFF_AR + A_ROT_W
OFF_DR = OFF_CR + C_ROT_W
W_ALL = OFF_DR + D_ROT_W
ROPE_W = A_ROT_W + C_ROT_W + D_ROT_W

F32 = jnp.float32
BF16 = jnp.bfloat16
NT_DIMS = (((1,), (1,)), ((), ()))
TN_DIMS = (((0,), (0,)), ((), ()))


def _rms(x):
    return x * lax.rsqrt(jnp.mean(x * x, axis=-1, keepdims=True) + EPS)


def _silu(x):
    return x * (1.0 / (1.0 + jnp.exp(-x)))


def _dot(a, b):
    return jnp.dot(a, b, preferred_element_type=F32)


def _dot_nt(a, b):
    return lax.dot_general(a, b, NT_DIMS, preferred_element_type=F32)


def _dot_tn(a, b):
    return lax.dot_general(a, b, TN_DIMS, preferred_element_type=F32)


def _mod_spec(d):
    return pl.BlockSpec((None, None, SUBLANES, d), lambda i, t: (i, jnp.minimum(t, 1), 0, 0))


def _norm_proj_kernel(h_ref, mod_ref, g_ref, w_ref, cos_ref, sin_ref, a_ref, b_ref, c_ref, d_ref):
    x = h_ref[...]
    u = (_rms(x) * g_ref[...] * (1.0 + mod_ref[1:2, :]) + mod_ref[0:1, :]).astype(BF16)

    def mm(lo, width):
        return _dot(u, w_ref[:, lo:lo + width])

    a_main = mm(OFF_A, A_W)
    a_rot = mm(OFF_AR, A_ROT_W)
    a_ref[:, :A_ROT_W] = (a_main[:, :A_ROT_W] * cos_ref[:, :A_ROT_W] + a_rot * sin_ref[:, :A_ROT_W]).astype(BF16)
    a_ref[:, A_ROT_W:] = a_main[:, A_ROT_W:].astype(BF16)
    b_ref[...] = mm(OFF_B, B_W)
    c_main = mm(OFF_C, C_W)
    c_rot = mm(OFF_CR, C_ROT_W)
    lo, hi = A_ROT_W, A_ROT_W + C_ROT_W
    c_ref[:, :C_W - C_ROT_W] = c_main[:, :C_W - C_ROT_W]
    c_ref[:, C_W - C_ROT_W:] = c_main[:, C_W - C_ROT_W:] * cos_ref[:, lo:hi] + c_rot * sin_ref[:, lo:hi]
    d_main = mm(OFF_D, D_W)
    d_rot = mm(OFF_DR, D_ROT_W)
    d_ref[:, :D_ROT_W] = d_main[:, :D_ROT_W] * cos_ref[:, hi:] + d_rot * sin_ref[:, hi:]
    d_ref[:, D_ROT_W:] = d_main[:, D_ROT_W:]


def norm_proj(h, mod, gain, w, cos_t, sin_t):
    b, l, d = h.shape
    tm = ROW_TILE
    row = lambda i, t: (i, t, 0)
    return pl.pallas_call(
        _norm_proj_kernel,
        grid=(b, l // tm),
        in_specs=[
            pl.BlockSpec((None, tm, d), row),
            _mod_spec(d),
            pl.BlockSpec((1, d), lambda i, t: (0, 0)),
            pl.BlockSpec((d, W_ALL), lambda i, t: (0, 0), pipeline_mode=pl.Buffered(1)),
            pl.BlockSpec((tm, ROPE_W), lambda i, t: (t, 0)),
            pl.BlockSpec((tm, ROPE_W), lambda i, t: (t, 0)),
        ],
        out_specs=[pl.BlockSpec((None, tm, A_W), row), pl.BlockSpec((None, tm, B_W), row),
                   pl.BlockSpec((None, tm, C_W), row), pl.BlockSpec((None, tm, D_W), row)],
        out_shape=[jax.ShapeDtypeStruct((b, l, A_W), BF16), jax.ShapeDtypeStruct((b, l, B_W), F32),
                   jax.ShapeDtypeStruct((b, l, C_W), F32), jax.ShapeDtypeStruct((b, l, D_W), F32)],
        compiler_params=pltpu.CompilerParams(
            dimension_semantics=("parallel", "parallel"), vmem_limit_bytes=VMEM_LIMIT),
        name="norm_proj",
    )(h, mod, gain, w, cos_t, sin_t)


def _rot_cols(w, hd):
    x = w.reshape(w.shape[:-1] + (w.shape[-1] // hd, 4, hd // 4))
    x1, x2, x3, x4 = x[..., 0, :], x[..., 1, :], x[..., 2, :], x[..., 3, :]
    return jnp.stack([-x2, x1, -x4, x3], axis=-2).reshape(w.shape)


def _place_swa_q(q):
    z = jnp.zeros((q.shape[0], HEAD_DIM), q.dtype)
    blocks = []
    for h in range(4):
        qh = q[:, HEAD_DIM * h:HEAD_DIM * (h + 1)]
        blocks += [qh, z] if h // 2 == 0 else [z, qh]
    return jnp.concatenate(blocks, axis=1)


def build_in_weight(w):
    d = w.shape[0]
    o = [int(v) for v in np.cumsum((256, 128, 128, 768, 256, 16, 256, 128, 32, 256, 256, 256, 256))]
    aq, ak, av = w[:, :o[0]] * HEAD_DIM ** -0.5, w[:, o[0]:o[1]], w[:, o[1]:o[2]]
    b_main, b_ab = w[:, o[2]:o[4]], w[:, o[4]:o[5]]
    c_q, c_kv, c_kr = w[:, o[5]:o[6]], w[:, o[6]:o[7]], w[:, o[7]:o[8]]
    dq, dk, dvg = w[:, o[8]:o[9]], w[:, o[9]:o[10]] * RET_DK ** -0.5, w[:, o[10]:]
    z = lambda n: jnp.zeros((d, n), w.dtype)
    parts = [
        _place_swa_q(aq), ak, av,
        b_main, b_ab, z(LANES - b_ab.shape[1]),
        c_q, c_kv, z(64), c_kr, z(32),
        dq, dk, dvg,
        _place_swa_q(_rot_cols(aq, HEAD_DIM)), _rot_cols(ak, HEAD_DIM),
        z(64), _rot_cols(c_kr, MLA_ROPE), z(32),
        _rot_cols(dq, HEAD_DIM), _rot_cols(dk, HEAD_DIM),
    ]
    out = jnp.concatenate(parts, axis=1)
    assert out.shape[1] == W_ALL
    return out.astype(BF16)


def rope_tables(n):
    def axial(rot_dim):
        n_freq = rot_dim // 4
        inv_freq = ROPE_THETA ** (-jnp.arange(n_freq, dtype=F32) / n_freq)
        row = jnp.repeat(jnp.arange(n // GRID_W, dtype=F32), GRID_W)
        col = jnp.tile(jnp.arange(GRID_W, dtype=F32), n // GRID_W)
        ang_r = row[:, None] * inv_freq
        ang_c = col[:, None] * inv_freq
        ang = jnp.concatenate([ang_r, ang_r, ang_c, ang_c], axis=-1)
        return jnp.cos(ang), jnp.sin(ang)

    cos_h, sin_h = axial(HEAD_DIM)
    cos_r, sin_r = axial(MLA_ROPE)
    one, zero = jnp.ones((n, 1), F32), jnp.zeros((n, 1), F32)
    cos_c = jnp.concatenate([jnp.tile(one, (1, 64)), cos_r, jnp.tile(one, (1, 32))], axis=1)
    sin_c = jnp.concatenate([jnp.tile(zero, (1, 64)), sin_r, jnp.tile(zero, (1, 32))], axis=1)
    cos_t = jnp.concatenate([jnp.tile(cos_h, (1, A_ROT_W // HEAD_DIM)), cos_c,
                             jnp.tile(cos_h, (1, D_ROT_W // HEAD_DIM))], axis=1)
    sin_t = jnp.concatenate([jnp.tile(sin_h, (1, A_ROT_W // HEAD_DIM)), sin_c,
                             jnp.tile(sin_h, (1, D_ROT_W // HEAD_DIM))], axis=1)
    cos_t = jnp.concatenate([jnp.ones((CTX_LEN, ROPE_W), F32), cos_t], axis=0)
    sin_t = jnp.concatenate([jnp.zeros((CTX_LEN, ROPE_W), F32), sin_t], axis=0)
    return cos_t, sin_t


def _swa_kernel(sink_ref, q_ref, kp_ref, ko_ref, kn_ref, kc_ref, vp_ref, vo_ref, vn_ref, vc_ref, o_ref):
    t = pl.program_id(1)
    last = pl.num_programs(1) - 1
    tq = q_ref.shape[0]
    half = tq // 2

    def head_out(h, pieces):
        q = q_ref[:, LANES * h:LANES * (h + 1)]
        ss = []
        for k_ref, _, mask in pieces:
            s = _dot_nt(q, k_ref[...])
            ss.append(s if mask is None else jnp.where(mask, s, NEG_INF))
        sink = sink_ref[h]
        m = jnp.maximum(functools.reduce(jnp.maximum, [s.max(axis=-1, keepdims=True) for s in ss]), sink)
        ps = [jnp.exp(s - m) for s in ss]
        denom = functools.reduce(jnp.add, [p.sum(axis=-1, keepdims=True) for p in ps]) + jnp.exp(sink - m)
        o = functools.reduce(jnp.add, [_dot(p.astype(BF16), piece[1][...]) for p, piece in zip(ps, pieces)])
        return o / denom

    def write(pieces):
        outs = [head_out(h, pieces) for h in range(4)]
        lane = lax.broadcasted_iota(jnp.int32, (tq, LANES), 1)
        for r in range(2):
            o_ref[:, LANES * r:LANES * (r + 1)] = jnp.where(lane < HEAD_DIM, outs[r], outs[2 + r]).astype(o_ref.dtype)

    @pl.when(t == 0)
    def _():
        write([(kc_ref, vc_ref, None)])

    @pl.when(t > 0)
    def _():
        qi = lax.broadcasted_iota(jnp.int32, (tq, half), 0)
        kj = lax.broadcasted_iota(jnp.int32, (tq, half), 1)
        mask_prev = (kj >= qi) & (t > 1)
        mask_next = (kj <= qi - half) & (t < last)
        qo = lax.broadcasted_iota(jnp.int32, (tq, tq), 0)
        ko = lax.broadcasted_iota(jnp.int32, (tq, tq), 1)
        mask_own = jnp.abs(qo - ko) <= SWA_WINDOW
        write([(kp_ref, vp_ref, mask_prev), (ko_ref, vo_ref, mask_own), (kn_ref, vn_ref, mask_next),
               (kc_ref, vc_ref, None)])


def swa_mixer(pa, sink):
    b, l, _ = pa.shape
    tq = ROW_TILE
    nblk = l // SWA_WINDOW
    kcol, vcol = 4, 5
    prev = lambda c: (lambda i, t: (i, jnp.maximum(2 * t - 1, 2), c))
    nxt = lambda c: (lambda i, t: (i, jnp.minimum(2 * t + 2, nblk - 1), c))
    own = lambda c: (lambda i, t: (i, t, c))
    ctx = lambda c: (lambda i, t: (i, 0, c))
    kv_specs = lambda c: [pl.BlockSpec((None, SWA_WINDOW, LANES), prev(c)), pl.BlockSpec((None, tq, LANES), own(c)),
                          pl.BlockSpec((None, SWA_WINDOW, LANES), nxt(c)), pl.BlockSpec((None, tq, LANES), ctx(c))]
    return pl.pallas_call(
        _swa_kernel,
        grid=(b, l // tq),
        in_specs=[pl.BlockSpec(memory_space=pltpu.SMEM),
                  pl.BlockSpec((None, tq, 4 * LANES), lambda i, t: (i, t, 0))] + kv_specs(kcol) + kv_specs(vcol),
        out_specs=pl.BlockSpec((None, tq, 2 * LANES), lambda i, t: (i, t, 0)),
        out_shape=jax.ShapeDtypeStruct((b, l, 2 * LANES), BF16),
        compiler_params=pltpu.CompilerParams(
            dimension_semantics=("parallel", "parallel"), vmem_limit_bytes=VMEM_LIMIT),
        name="swa",
    )(sink, pa, pa, pa, pa, pa, pa, pa, pa, pa)


def _mla_prep_kernel(c_ref, qn_ref, kvn_ref, wq_ref, wqr_ref, wk_ref, wv_ref, cos_ref, sin_ref,
                     q_ref, k_ref, v_ref):
    cq = c_ref[:, 0:256]
    ckv = c_ref[:, 256:384]
    kr = c_ref[:, 384:512]
    nq = (_rms(cq) * qn_ref[...]).astype(BF16)
    nkv = (_rms(ckv) * kvn_ref[...]).astype(BF16)
    cos = jnp.concatenate([cos_ref[...]] * MLA_HEADS, axis=1)
    sin = jnp.concatenate([sin_ref[...]] * MLA_HEADS, axis=1)
    q_ref[...] = (_dot(nq, wq_ref[...]) * cos + _dot(nq, wqr_ref[...]) * sin).astype(BF16)
    k_ref[...] = (_dot(nkv, wk_ref[...]) + jnp.concatenate([kr] * MLA_HEADS, axis=1)).astype(BF16)
    v_ref[...] = _dot(nkv, wv_ref[...]).astype(BF16)


def mla_prep(pc, q_norm, kv_norm, w_q_up, w_kv_up, cos_c, sin_c):
    b, l, _ = pc.shape
    tm = ROW_TILE
    scale = (MLA_NOPE + MLA_ROPE) ** -0.5 * float(np.log2(np.e))
    wq = (w_q_up * scale).reshape(-1, MLA_HEADS, MLA_NOPE + MLA_ROPE)
    zq = jnp.zeros(wq.shape[:2] + (LANES - MLA_NOPE - MLA_ROPE,), F32)
    wq_main = jnp.concatenate([wq, zq], axis=-1).reshape(-1, MLA_HEADS * LANES)
    wq_rot = jnp.concatenate([jnp.zeros_like(wq[..., :MLA_NOPE]), _rot_cols(wq[..., MLA_NOPE:], MLA_ROPE), zq],
                             axis=-1).reshape(-1, MLA_HEADS * LANES)
    wkv = w_kv_up.reshape(-1, MLA_HEADS, MLA_NOPE + MLA_V)
    wk = jnp.concatenate([wkv[..., :MLA_NOPE], jnp.zeros_like(wkv[..., :LANES - MLA_NOPE])],
                         axis=-1).reshape(-1, MLA_HEADS * LANES)
    wv = wkv[..., MLA_NOPE:].reshape(-1, MLA_HEADS * MLA_V)
    row = lambda i, t: (i, t, 0)
    const = lambda i, t: (0, 0)
    full = lambda a: pl.BlockSpec(a.shape, const)
    args = [q_norm[None, :], kv_norm[None, :], wq_main.astype(BF16), wq_rot.astype(BF16), wk.astype(BF16),
            wv.astype(BF16)]
    return pl.pallas_call(
        _mla_prep_kernel,
        grid=(b, l // tm),
        in_specs=[pl.BlockSpec((None, tm, C_W), row)] + [full(a) for a in args]
        + [pl.BlockSpec((tm, LANES), lambda i, t: (t, 0)), pl.BlockSpec((tm, LANES), lambda i, t: (t, 0))],
        out_specs=[pl.BlockSpec((None, tm, 4 * LANES), row), pl.BlockSpec((None, tm, 4 * LANES), row),
                   pl.BlockSpec((None, tm, 2 * LANES), row)],
        out_shape=[jax.ShapeDtypeStruct((b, l, 4 * LANES), BF16), jax.ShapeDtypeStruct((b, l, 4 * LANES), BF16),
                   jax.ShapeDtypeStruct((b, l, 2 * LANES), BF16)],
        compiler_params=pltpu.CompilerParams(dimension_semantics=("parallel", "parallel")),
        name="mla_prep",
    )(pc, *args, cos_c, sin_c)


def _mla_attn_kernel(q_ref, k_ref, v_ref, o_ref):
    t = pl.program_id(2)
    tq = q_ref.shape[0]

    def attend(nk):
        v = v_ref[0:nk, :]
        v_lane = lax.broadcasted_iota(jnp.int32, v.shape, 1)
        outs = []
        for j in range(2):
            q = q_ref[:, LANES * j:LANES * (j + 1)]
            k = k_ref[0:nk, LANES * j:LANES * (j + 1)]
            s = _dot_nt(q, k)
            p = jnp.exp2(s - s.max(axis=-1, keepdims=True)).astype(BF16)
            other = (v_lane >= MLA_V) if j == 0 else (v_lane < MLA_V)
            o = _dot(p, jnp.where(other, jnp.ones_like(v), v))
            den_lane = MLA_V if j == 0 else 0
            outs.append(o / o[:, den_lane:den_lane + 1])
        lane = lax.broadcasted_iota(jnp.int32, (tq, LANES), 1)
        o_ref[...] = jnp.where(lane < MLA_V, outs[0], outs[1]).astype(o_ref.dtype)

    @pl.when(t == 0)
    def _():
        attend(CTX_LEN)

    @pl.when(t > 0)
    def _():
        attend(k_ref.shape[0])


def mla_attention(q, k, v):
    b, l, _ = q.shape
    tq = ROW_TILE
    return pl.pallas_call(
        _mla_attn_kernel,
        grid=(b, 2, l // tq),
        in_specs=[pl.BlockSpec((None, tq, 2 * LANES), lambda i, p, t: (i, t, p)),
                  pl.BlockSpec((None, l, 2 * LANES), lambda i, p, t: (i, 0, p)),
                  pl.BlockSpec((None, l, LANES), lambda i, p, t: (i, 0, p))],
        out_specs=pl.BlockSpec((None, tq, LANES), lambda i, p, t: (i, t, p)),
        out_shape=jax.ShapeDtypeStruct((b, l, 2 * LANES), BF16),
        compiler_params=pltpu.CompilerParams(
            dimension_semantics=("parallel", "parallel", "parallel"), vmem_limit_bytes=VMEM_LIMIT),
        name="mla_attn",
    )(q, k, v)


def _head_mean(x, ones_bd):
    hi = x.astype(BF16)
    lo = (x - hi.astype(F32)).astype(BF16)
    return (_dot(hi, ones_bd) + _dot(lo, ones_bd)) * (1.0 / HEAD_DIM)


def _ret_kernel(x_ref, lg_ref, g_ref, o_ref, s_ref, dec_ref):
    dr = pl.program_id(1)
    s = pl.program_id(2)
    ns = pl.num_programs(2)
    c = x_ref.shape[0]
    w = RET_HEADS * HEAD_DIM
    chunk = jnp.where(s == 0, 0, jnp.where(dr == 0, s, ns - s))
    lg = lg_ref[...]
    fwd = dr == 0
    row_h = lax.broadcasted_iota(jnp.int32, (w, w), 0) // HEAD_DIM
    col_h = lax.broadcasted_iota(jnp.int32, (w, w), 1) // HEAD_DIM
    same_head = row_h == col_h

    @pl.when(s == 0)
    def _():
        s_ref[...] = jnp.zeros_like(s_ref)
        i = lax.broadcasted_iota(jnp.int32, (c, c), 0)
        j = lax.broadcasted_iota(jnp.int32, (c, c), 1)
        rel = jnp.where(fwd, i - j, j - i)
        relf = jnp.maximum(rel, 0).astype(F32)
        for h in range(RET_HEADS):
            lg_h = lg_ref[0:1, HEAD_DIM * h:HEAD_DIM * h + 1]
            dec_ref[h] = jnp.where(rel >= 0, jnp.exp(lg_h * relf), 0.0)

    q = x_ref[:, 0:w]
    k = x_ref[:, w:2 * w]
    v = x_ref[:, 2 * w:3 * w].astype(BF16)
    kb = k.astype(BF16)
    pos = lax.broadcasted_iota(jnp.int32, (c, 1), 0).astype(F32)
    q_pow = jnp.where(fwd, pos + 1.0, c - pos)
    k_pow = jnp.where(fwd, c - 1.0 - pos, pos)
    lane_h = lax.broadcasted_iota(jnp.int32, (c, w), 1) // HEAD_DIM
    acc = _dot((q * jnp.exp(lg * q_pow)).astype(BF16), s_ref[...].astype(BF16))
    for h in range(RET_HEADS):
        qh = jnp.where(lane_h == h, q, 0.0).astype(BF16)
        a = _dot_nt(qh, kb) * dec_ref[h]
        acc = acc + jnp.where(lane_h == h, _dot(a.astype(BF16), v), 0.0)
    kv = _dot_tn((k * jnp.exp(lg * k_pow)).astype(BF16), v)
    s_ref[...] = s_ref[...] * jnp.exp(lg * float(c)) + jnp.where(same_head, kv, 0.0)

    rows = pl.ds(pl.multiple_of(chunk * c, c), c)

    @pl.when(dr == 0)
    def _():
        o_ref[rows, :] = acc

    @pl.when(dr == 1)
    def _():
        o = o_ref[rows, :] + acc
        ones_bd = jnp.where(same_head, 1.0, 0.0).astype(BF16)
        mu = _head_mean(o, ones_bd)
        var = _head_mean(jnp.square(o - mu), ones_bd)
        y = (o - mu) * lax.rsqrt(var + EPS) * g_ref[...]
        o_ref[rows, :] = y * _silu(x_ref[:, 3 * w:4 * w])


def retention_mixer(pd, log_decay, norm_g):
    b, l, _ = pd.shape
    c = ROW_TILE
    ns = l // c
    w = RET_HEADS * HEAD_DIM
    lg = jnp.repeat(-jnp.exp(log_decay.astype(F32)), HEAD_DIM, axis=-1)[:, None, :]

    def chunk_of(dr, s):
        return jnp.where(s == 0, 0, jnp.where(dr == 0, s, ns - s))

    return pl.pallas_call(
        _ret_kernel,
        grid=(b, 2, ns),
        in_specs=[pl.BlockSpec((None, c, D_W), lambda i, dr, s: (i, chunk_of(dr, s), 0)),
                  pl.BlockSpec((None, 1, w), lambda i, dr, s: (dr, 0, 0)),
                  pl.BlockSpec((1, w), lambda i, dr, s: (0, 0))],
        out_specs=pl.BlockSpec((None, l, w), lambda i, dr, s: (i, 0, 0)),
        out_shape=jax.ShapeDtypeStruct((b, l, w), F32),
        scratch_shapes=[pltpu.VMEM((w, w), F32), pltpu.VMEM((RET_HEADS, c, c), F32)],
        compiler_params=pltpu.CompilerParams(
            dimension_semantics=("parallel", "arbitrary", "arbitrary"), vmem_limit_bytes=VMEM_LIMIT),
        name="retention",
    )(pd, lg, norm_g[None, :])


def _out_proj_kernel(ma_ref, mb_ref, mc_ref, md_ref, h_ref, mod_ref, g_ref, w_ref, *rest, with_router):
    if with_router:
        wr_ref, hn_ref, v_ref, lg_ref = rest
    else:
        hn_ref, v_ref = rest
    gw = 2 * LANES
    mix = functools.reduce(jnp.add, [
        _dot(m_ref[...].astype(BF16), w_ref[gw * i:gw * (i + 1), :])
        for i, m_ref in enumerate((ma_ref, mb_ref, mc_ref, md_ref))])
    hn = h_ref[...] + mod_ref[2:3, :] * mix
    hn_ref[...] = hn
    v = _rms(hn) * g_ref[...] * (1.0 + mod_ref[4:5, :]) + mod_ref[3:4, :]
    v_ref[...] = v.astype(v_ref.dtype)
    if with_router:
        lg_ref[...] = jnp.dot(v, wr_ref[...], preferred_element_type=F32, precision=lax.Precision.HIGHEST)


def out_proj(mixes, h, mod, gain, w, w_router=None):
    b, l, d = h.shape
    tm = ROW_TILE
    with_router = w_router is not None
    row = lambda i, t: (i, t, 0)
    in_specs = [pl.BlockSpec((None, tm, 2 * LANES), row) for _ in mixes] + [
        pl.BlockSpec((None, tm, d), row),
        _mod_spec(d),
        pl.BlockSpec((1, d), lambda i, t: (0, 0)),
        pl.BlockSpec(w.shape, lambda i, t: (0, 0), pipeline_mode=pl.Buffered(1)),
    ]
    out_specs = [pl.BlockSpec((None, tm, d), row), pl.BlockSpec((None, tm, d), row)]
    out_shape = [jax.ShapeDtypeStruct((b, l, d), F32), jax.ShapeDtypeStruct((b, l, d), F32 if with_router else BF16)]
    args = list(mixes) + [h, mod, gain, w]
    if with_router:
        in_specs.append(pl.BlockSpec(w_router.shape, lambda i, t: (0, 0)))
        out_specs.append(pl.BlockSpec((None, tm, LANES), row))
        out_shape.append(jax.ShapeDtypeStruct((b, l, LANES), F32))
        args.append(w_router)
    return pl.pallas_call(
        functools.partial(_out_proj_kernel, with_router=with_router),
        grid=(b, l // tm),
        in_specs=in_specs,
        out_specs=out_specs,
        out_shape=out_shape,
        compiler_params=pltpu.CompilerParams(
            dimension_semantics=("parallel", "parallel"), vmem_limit_bytes=VMEM_LIMIT),
        name="out_proj",
    )(*args)


def build_out_weight(w):
    hd = HEAD_DIM
    perm = jnp.concatenate([w[0:hd], w[2 * hd:3 * hd], w[hd:2 * hd], w[3 * hd:4 * hd]], axis=0)
    return jnp.concatenate([perm, w[4 * hd:]], axis=0).astype(BF16)


def _ffn_kernel(v_ref, h_ref, mod_ref, wg_ref, wu_ref, wd_ref, o_ref):
    v = v_ref[...]
    acc = jnp.zeros(o_ref.shape, F32)
    for j in range(D_FF // FF_CHUNK):
        cols = slice(j * FF_CHUNK, (j + 1) * FF_CHUNK)
        a = _dot(v, wg_ref[:, cols])
        u = _dot(v, wu_ref[:, cols])
        mid = (_silu(a) * u).astype(BF16)
        acc = acc + _dot(mid, wd_ref[cols, :])
    o_ref[...] = h_ref[...] + mod_ref[5:6, :] * acc


def dense_ffn(v, h, mod, wg, wu, wd):
    b, l, d = h.shape
    tm = ROW_TILE
    row = lambda i, t: (i, t, 0)
    const = lambda i, t: (0, 0)
    return pl.pallas_call(
        _ffn_kernel,
        grid=(b, l // tm),
        in_specs=[
            pl.BlockSpec((None, tm, d), row),
            pl.BlockSpec((None, tm, d), row),
            _mod_spec(d),
            pl.BlockSpec(wg.shape, const, pipeline_mode=pl.Buffered(1)),
            pl.BlockSpec(wu.shape, const, pipeline_mode=pl.Buffered(1)),
            pl.BlockSpec(wd.shape, const, pipeline_mode=pl.Buffered(1)),
        ],
        out_specs=pl.BlockSpec((None, tm, d), row),
        out_shape=jax.ShapeDtypeStruct((b, l, d), F32),
        compiler_params=pltpu.CompilerParams(
            dimension_semantics=("parallel", "parallel"), vmem_limit_bytes=VMEM_LIMIT),
        name="dense_ffn",
    )(v, h, mod, wg, wu, wd)


def _moe_kernel(wt_ref, we_ref, lo_ref, hi_ref, first_ref, x_ref, wg_ref, wu_ref, wd_ref, o_ref, xm_ref, acc_ref):
    w = pl.program_id(0)
    j = pl.program_id(1)
    nj = pl.num_programs(1)
    tm = x_ref.shape[0]

    @pl.when(j == 0)
    def _():
        row = wt_ref[w] * tm + lax.broadcasted_iota(jnp.int32, (tm, 1), 0)
        keep = (row >= lo_ref[w]) & (row < hi_ref[w])
        xm_ref[...] = jnp.where(keep, x_ref[...], 0.0).astype(BF16)

    @pl.when((j == 0) & (first_ref[w] > 0))
    def _():
        acc_ref[...] = jnp.zeros_like(acc_ref)

    @pl.when(hi_ref[w] > lo_ref[w])
    def _():
        x = xm_ref[...]
        a = _dot(x, wg_ref[...])
        u = _dot(x, wu_ref[...])
        mid = (_silu(a) * u).astype(BF16)
        acc_ref[...] += _dot(mid, wd_ref[...])

    @pl.when(j == nj - 1)
    def _():
        o_ref[...] = acc_ref[...]


def moe_grouped_ffn(xs, items, wg, wu, wd):
    s, d = xs.shape
    tm = MOE_TILE
    nw = items[0].shape[0]
    nj = D_FF // FF_CHUNK
    grid_spec = pltpu.PrefetchScalarGridSpec(
        num_scalar_prefetch=5,
        grid=(nw, nj),
        in_specs=[
            pl.BlockSpec((tm, d), lambda w, j, wt, we, lo, hi, fi: (wt[w], 0)),
            pl.BlockSpec((None, d, FF_CHUNK), lambda w, j, wt, we, lo, hi, fi: (we[w], 0, j)),
            pl.BlockSpec((None, d, FF_CHUNK), lambda w, j, wt, we, lo, hi, fi: (we[w], 0, j)),
            pl.BlockSpec((None, FF_CHUNK, d), lambda w, j, wt, we, lo, hi, fi: (we[w], j, 0)),
        ],
        out_specs=pl.BlockSpec((tm, d), lambda w, j, wt, we, lo, hi, fi: (wt[w], 0)),
        scratch_shapes=[pltpu.VMEM((tm, d), BF16), pltpu.VMEM((tm, d), F32)],
    )
    return pl.pallas_call(
        _moe_kernel,
        grid_spec=grid_spec,
        out_shape=jax.ShapeDtypeStruct((s, d), F32),
        compiler_params=pltpu.CompilerParams(
            dimension_semantics=("arbitrary", "arbitrary"), vmem_limit_bytes=VMEM_LIMIT),
        name="moe_ffn",
    )(*items, xs, wg, wu, wd)


def _residual_kernel(h_ref, f0_ref, f1_ref, gate_ref, mod_ref, o_ref):
    f = gate_ref[:, 0:1] * f0_ref[...] + gate_ref[:, 1:2] * f1_ref[...]
    o_ref[...] = h_ref[...] + mod_ref[5:6, :] * f


def gated_residual(h, f0, f1, gates, mod):
    b, l, d = h.shape
    tm = ROW_TILE
    row = lambda i, t: (i, t, 0)
    return pl.pallas_call(
        _residual_kernel,
        grid=(b, l // tm),
        in_specs=[pl.BlockSpec((None, tm, d), row), pl.BlockSpec((None, tm, d), row), pl.BlockSpec((None, tm, d), row),
                  pl.BlockSpec((None, tm, LANES), row), _mod_spec(d)],
        out_specs=pl.BlockSpec((None, tm, d), row),
        out_shape=jax.ShapeDtypeStruct((b, l, d), F32),
        compiler_params=pltpu.CompilerParams(dimension_semantics=("parallel", "parallel")),
        name="gated_residual",
    )(h, f0, f1, gates, mod)


def moe_ffn(v, logits, h, mod, wg, wu, wd):
    b, l, d = h.shape
    t = b * l
    s = TOP_K * t
    tm = MOE_TILE
    nt = s // tm
    nw = nt + N_EXPERTS - 1
    i32 = jnp.int32
    lg = logits.reshape(t, LANES)[:, :N_EXPERTS]
    top_val, top_idx = lax.top_k(lg, TOP_K)
    gates = jax.nn.softmax(top_val, axis=-1)
    slot = jnp.arange(s, dtype=i32)
    skey = jnp.sort(top_idx.reshape(-1).astype(i32) * s + slot)
    order = skey % s
    _, inv = lax.sort_key_val(order, slot)
    cum = jnp.searchsorted(skey, (jnp.arange(N_EXPERTS, dtype=i32) + 1) * s, side="left").astype(i32)
    cum_prev = jnp.concatenate([jnp.zeros((1,), i32), cum[:-1]])
    tile_lo = jnp.arange(nt, dtype=i32) * tm
    e_first = jnp.searchsorted(cum, tile_lo, side="right").astype(i32)
    e_last = jnp.searchsorted(cum, tile_lo + tm - 1, side="right").astype(i32)
    n_items = e_last - e_first + 1
    item_end = jnp.cumsum(n_items)
    item_start = item_end - n_items
    w = jnp.arange(nw, dtype=i32)
    wt = jnp.minimum(jnp.searchsorted(item_end, w, side="right"), nt - 1).astype(i32)
    valid = w < item_end[-1]
    we = jnp.clip(e_first[wt] + w - item_start[wt], 0, N_EXPERTS - 1).astype(i32)
    lo = jnp.where(valid, cum_prev[we], 0).astype(i32)
    hi = jnp.where(valid, cum[we], 0).astype(i32)
    first = (valid & (w == item_start[wt])).astype(i32)
    xs = jnp.take(v.reshape(t, d), order // TOP_K, axis=0)
    ys = moe_grouped_ffn(xs, (wt, we, lo, hi, first), wg, wu, wd)
    dest = inv.reshape(t, TOP_K)
    f0 = jnp.take(ys, dest[:, 0], axis=0).reshape(b, l, d)
    f1 = jnp.take(ys, dest[:, 1], axis=0).reshape(b, l, d)
    gates_p = jnp.pad(gates, ((0, 0), (0, LANES - TOP_K))).reshape(b, l, LANES)
    return gated_residual(h, f0, f1, gates_p, mod)


def _final_norm_kernel(h_ref, g_ref, o_ref):
    o_ref[...] = _rms(h_ref[...]) * g_ref[...]


def final_rms_norm(h, gain, n_ctx_tiles):
    b, l, d = h.shape
    tm = ROW_TILE
    n = l - n_ctx_tiles * tm
    return pl.pallas_call(
        _final_norm_kernel,
        grid=(b, n // tm),
        in_specs=[
            pl.BlockSpec((None, tm, d), lambda i, t: (i, t + n_ctx_tiles, 0)),
            pl.BlockSpec((1, d), lambda i, t: (0, 0)),
        ],
        out_specs=pl.BlockSpec((None, tm, d), lambda i, t: (i, t, 0)),
        out_shape=jax.ShapeDtypeStruct((b, n, d), F32),
        compiler_params=pltpu.CompilerParams(dimension_semantics=("parallel", "parallel")),
        name="final_norm",
    )(h, gain)


GDN_W = GDN_HEADS * GDN_DK
GDN_CONV_K = 5
GDN_HALO = SUBLANES


def _split3(x):
    p0 = x.astype(BF16)
    r1 = x - p0.astype(F32)
    p1 = r1.astype(BF16)
    p2 = (r1 - p1.astype(F32)).astype(BF16)
    return p0, p1, p2


def _gdn_prep_kernel(x_ref, prev_ref, next_ref, cw_ref, par_ref, q_ref, k_ref, v_ref, gb_ref):
    t = pl.program_id(1)
    last = pl.num_programs(1) - 1
    tm = x_ref.shape[0]
    w3 = 3 * GDN_W
    has_prev = t > 1
    has_next = (t > 0) & (t < last)
    prev = jnp.where(has_prev, prev_ref[...], 0.0)
    nxt = jnp.where(has_next, next_ref[...], 0.0)
    xe = jnp.concatenate([prev, x_ref[:, :w3], nxt], axis=0)
    y = jnp.zeros((tm, w3), F32)
    for j in range(GDN_CONV_K):
        lo = GDN_HALO - GDN_CONV_K // 2 + j
        y = y + cw_ref[j:j + 1, :] * xe[lo:lo + tm, :]
    y = _silu(y)
    r = lax.broadcasted_iota(jnp.int32, (GDN_W, GDN_W), 0)
    c = lax.broadcasted_iota(jnp.int32, (GDN_W, GDN_W), 1)
    ones_bd = jnp.where(r // GDN_DK == c // GDN_DK, 1.0, 0.0).astype(BF16)

    def l2n(x):
        sq = x * x
        hi = sq.astype(BF16)
        lo = (sq - hi.astype(F32)).astype(BF16)
        return x * lax.rsqrt(_dot(hi, ones_bd) + _dot(lo, ones_bd) + EPS)

    q_ref[...] = l2n(y[:, :GDN_W]) * GDN_DK ** -0.5
    k_ref[...] = l2n(y[:, GDN_W:2 * GDN_W])
    v_ref[...] = y[:, 2 * GDN_W:]
    ab = x_ref[:, w3 + GDN_W:]
    lane = lax.broadcasted_iota(jnp.int32, ab.shape, 1)
    is_g = (lane % 8) < 4
    z = ab + par_ref[1:2, :]
    softplus = jnp.maximum(z, 0.0) + jnp.log1p(jnp.exp(-jnp.abs(z)))
    g = jnp.where(is_g, par_ref[0:1, :] * softplus, 0.0)
    beta = 1.0 / (1.0 + jnp.exp(-ab))
    i = lax.broadcasted_iota(jnp.int32, (tm, tm), 0)
    j = lax.broadcasted_iota(jnp.int32, (tm, tm), 1)
    same_chunk = i // GDN_CHUNK == j // GDN_CHUNK
    tri_f = jnp.where(same_chunk & (j <= i), 1.0, 0.0).astype(BF16)
    tri_b = jnp.where(same_chunk & (j >= i), 1.0, 0.0).astype(BF16)
    pieces = _split3(g)
    gc_f = functools.reduce(jnp.add, [_dot(tri_f, p) for p in pieces])
    gc_b = functools.reduce(jnp.add, [_dot(tri_b, p) for p in pieces])
    gb_ref[...] = jnp.where(is_g, jnp.where(lane < 8, gc_f, gc_b), beta)


def gdn_prep(pb, conv_w, a_log, dt_bias):
    b, l, _ = pb.shape
    tm = ROW_TILE
    w3 = 3 * GDN_W
    halo_blocks = tm // GDN_HALO
    n_halo = l // GDN_HALO
    cw = jnp.pad(conv_w, ((0, SUBLANES - GDN_CONV_K), (0, 0)))
    neg_a = jnp.pad(-jnp.exp(a_log.astype(F32)), ((0, 0), (0, 4))).reshape(-1)
    dtb = jnp.pad(dt_bias.astype(F32), ((0, 0), (0, 4))).reshape(-1)
    par = jnp.pad(jnp.stack([neg_a, dtb]), ((0, SUBLANES - 2), (0, LANES - 16)))
    row = lambda i, t: (i, t, 0)
    out = lambda w: pl.BlockSpec((None, tm, w), row)
    return pl.pallas_call(
        _gdn_prep_kernel,
        grid=(b, l // tm),
        in_specs=[pl.BlockSpec((None, tm, B_W), row),
                  pl.BlockSpec((None, GDN_HALO, w3), lambda i, t: (i, jnp.maximum(t * halo_blocks - 1, 0), 0)),
                  pl.BlockSpec((None, GDN_HALO, w3),
                               lambda i, t: (i, jnp.minimum((t + 1) * halo_blocks, n_halo - 1), 0)),
                  pl.BlockSpec(cw.shape, lambda i, t: (0, 0)),
                  pl.BlockSpec(par.shape, lambda i, t: (0, 0))],
        out_specs=[out(GDN_W), out(GDN_W), out(GDN_W), out(LANES)],
        out_shape=[jax.ShapeDtypeStruct((b, l, GDN_W), F32)] * 3 + [jax.ShapeDtypeStruct((b, l, LANES), F32)],
        compiler_params=pltpu.CompilerParams(
            dimension_semantics=("parallel", "parallel"), vmem_limit_bytes=VMEM_LIMIT),
        name="gdn_prep",
    )(pb, pb, pb, cw, par)


def _tile_heads(x):
    return jnp.concatenate([x] * GDN_HEADS, axis=0)


def _collapse_heads(x):
    c = GDN_CHUNK
    return x[0:c] + x[c:2 * c] + x[2 * c:3 * c] + x[3 * c:4 * c]


def _gdn_chunk_kernel(q_ref, k_ref, v_ref, gb_ref, o0_ref, qe_ref, a_ref, bm_ref, gam_ref):
    dr = pl.program_id(1)
    fwd = dr == 0
    n = GDN_W
    cs = GDN_CHUNK
    r = lax.broadcasted_iota(jnp.int32, (n, n), 0)
    c = lax.broadcasted_iota(jnp.int32, (n, n), 1)
    ri, ci = r % cs, c % cs
    head = r // cs == c // cs
    ahead = (ri - ci) * jnp.where(fwd, 1, -1)
    tri = head & (ahead >= 0)
    tri_strict = head & (ahead > 0)
    eye = jnp.where(r == c, 1.0, 0.0)
    blk = lambda s: r // s == c // s
    b8, b16, b32 = blk(8), blk(16), blk(32)
    lane = lax.broadcasted_iota(jnp.int32, (n, LANES), 1)
    row_head = lax.broadcasted_iota(jnp.int32, (n, LANES), 0) // cs
    sel_g = lane == dr * 8 + row_head
    sel_b = lane == dr * 8 + 4 + row_head
    pick = lambda sel, x: jnp.sum(jnp.where(sel, x, 0.0), axis=1, keepdims=True)
    mm = lambda a, b: _dot(a.astype(BF16), b.astype(BF16))

    rows = [slice(ch * cs, (ch + 1) * cs) for ch in range(q_ref.shape[0] // cs)]
    each = lambda f, *xs: [f(*a) for a in zip(*xs)]
    bf = lambda xs: [x.astype(BF16) for x in xs]
    spread = lambda ref: [jnp.where(head, _tile_heads(ref[rw, :]), 0.0) for rw in rows]
    kh, qh, vh = spread(k_ref), spread(q_ref), spread(v_ref)
    gb = [gb_ref[rw, :] for rw in rows]
    gb4 = [_tile_heads(x) for x in gb]
    gc = [pick(sel_g, x) for x in gb4]
    beta = [pick(sel_b, x) for x in gb4]
    gl = [pick(sel_g, jnp.broadcast_to(jnp.where(fwd, x[cs - 1:cs, :], x[0:1, :]), (n, LANES))) for x in gb]
    gc_b = [jnp.broadcast_to(x, (n, n)) for x in gc]
    decay = [jnp.exp(jnp.minimum(x - x.T, 0.0)) for x in gc_b]
    khb, qhb = bf(kh), bf(qh)
    kk = each(_dot_nt, khb, khb)
    qk = each(_dot_nt, qhb, khb)
    lmat = each(lambda b_, kk_, d_: jnp.where(tri_strict, b_ * kk_ * d_, 0.0), beta, kk, decay)
    attn = bf(each(lambda qk_, d_: jnp.where(tri, qk_ * d_, 0.0), qk, decay))
    nl = bf([jnp.where(b8, -x, 0.0) for x in lmat])
    n2 = bf(each(_dot, nl, nl))
    n4 = each(_dot, n2, n2)
    p1 = bf(each(lambda a, b_: _dot((eye + a).astype(BF16), (eye + b_).astype(BF16)), nl, n2))
    tinv = each(lambda p, x: _dot(p, (eye + x).astype(BF16)), p1, n4)
    for inner, outer in ((b8, b16), (b16, b32), (b32, head)):
        off = bf([jnp.where(outer & ~inner, x, 0.0) for x in lmat])
        tb = bf(tinv)
        to = bf(each(_dot, tb, off))
        tinv = each(lambda t_, to_, tb_: t_ - _dot(to_, tb_), tinv, to, tb)
    tb = bf(tinv)
    eg = [jnp.exp(x) for x in gc]
    u = bf(each(lambda t_, b_, v_: _dot(t_, (b_ * v_).astype(BF16)), tb, beta, vh))
    w = bf(each(lambda t_, b_, e_, k_: _dot(t_, ((b_ * e_) * k_).astype(BF16)), tb, beta, eg, kh))
    o0 = each(_dot, attn, u)
    qe = each(lambda q_, e_, a_, w_: q_ * e_ - _dot(a_, w_), qh, eg, attn, w)
    kg = bf(each(lambda k_, gl_, gc_: k_ * jnp.exp(gl_ - gc_), kh, gl, gc))
    a_mat = each(_dot_tn, kg, w)
    b_mat = each(_dot_tn, kg, u)
    for rw, o0_, qe_, a_, b_, gl_ in zip(rows, o0, qe, a_mat, b_mat, gl):
        o0_ref[rw, :] = _collapse_heads(o0_)
        qe_ref[rw, :] = _collapse_heads(qe_)
        a_ref[rw, :] = _collapse_heads(a_)
        bm_ref[rw, :] = _collapse_heads(b_)
        gam_ref[rw, :] = _collapse_heads(jnp.where(head, jnp.broadcast_to(jnp.exp(gl_), (n, n)), 0.0))


def gdn_chunks(q, k, v, gb):
    b, l, _ = q.shape
    tm = ROW_TILE
    row = lambda i, dr, t: (i, t, 0)
    out = pl.BlockSpec((None, None, tm, GDN_W), lambda i, dr, t: (dr, i, t, 0))
    return pl.pallas_call(
        _gdn_chunk_kernel,
        grid=(b, 2, l // tm),
        in_specs=[pl.BlockSpec((None, tm, GDN_W), row)] * 3 + [pl.BlockSpec((None, tm, LANES), row)],
        out_specs=[out] * 5,
        out_shape=[jax.ShapeDtypeStruct((2, b, l, GDN_W), F32)] * 5,
        compiler_params=pltpu.CompilerParams(
            dimension_semantics=("parallel", "parallel", "parallel"), vmem_limit_bytes=VMEM_LIMIT),
        name="gdn_chunk",
    )(q, k, v, gb)


def _gdn_scan_kernel(o0_ref, qe_ref, a_ref, bm_ref, gam_ref, gate_ref, g_ref, o_ref, s_ref):
    dr = pl.program_id(1)
    s = pl.program_id(2)
    ns = pl.num_programs(2)
    tm = o0_ref.shape[0]
    cs = GDN_CHUNK
    n = GDN_W
    nch = tm // cs
    tile = jnp.where(s == 0, 0, jnp.where(dr == 0, s, ns - s))
    r = lax.broadcasted_iota(jnp.int32, (n, n), 0)
    c = lax.broadcasted_iota(jnp.int32, (n, n), 1)
    head = r // cs == c // cs

    @pl.when(s == 0)
    def _():
        s_ref[...] = jnp.zeros_like(s_ref)

    def run(order):
        state = s_ref[...]
        outs = {}
        for ch in order:
            rows = slice(ch * cs, (ch + 1) * cs)
            sb = state.astype(BF16)
            outs[ch] = o0_ref[rows, :] + _dot(qe_ref[rows, :].astype(BF16), sb)
            a_full = jnp.where(head, _tile_heads(a_ref[rows, :]), 0.0).astype(BF16)
            b_full = jnp.where(head, _tile_heads(bm_ref[rows, :]), 0.0)
            state = _tile_heads(gam_ref[rows, :]) * state - _dot(a_full, sb) + b_full
        s_ref[...] = state
        return jnp.concatenate([outs[ch] for ch in range(nch)], axis=0)

    rows_out = pl.ds(pl.multiple_of(tile * tm, tm), tm)

    @pl.when(dr == 0)
    def _():
        o_ref[rows_out, :] = run(range(nch))

    @pl.when(dr == 1)
    def _():
        o = o_ref[rows_out, :] + run(range(nch - 1, -1, -1))
        ones_bd = jnp.where(head, 1.0, 0.0).astype(BF16)
        ms = _head_mean(o * o, ones_bd)
        o_ref[rows_out, :] = o * lax.rsqrt(ms + EPS) * g_ref[...] * _silu(gate_ref[...])


def gdn_scan(o0, qe, a, bm, gam, pb, norm_g):
    _, b, l, _ = o0.shape
    tm = ROW_TILE
    ns = l // tm

    def tile_of(dr, s):
        return jnp.where(s == 0, 0, jnp.where(dr == 0, s, ns - s))

    per_dir = pl.BlockSpec((None, None, tm, GDN_W), lambda i, dr, s: (dr, i, tile_of(dr, s), 0))
    gate_col = 3 * GDN_W // GDN_W
    return pl.pallas_call(
        _gdn_scan_kernel,
        grid=(b, 2, ns),
        in_specs=[per_dir] * 5 + [
            pl.BlockSpec((None, tm, GDN_W), lambda i, dr, s: (i, tile_of(dr, s), gate_col)),
            pl.BlockSpec((1, GDN_W), lambda i, dr, s: (0, 0))],
        out_specs=pl.BlockSpec((None, l, GDN_W), lambda i, dr, s: (i, 0, 0)),
        out_shape=jax.ShapeDtypeStruct((b, l, GDN_W), F32),
        scratch_shapes=[pltpu.VMEM((GDN_W, GDN_W), F32)],
        compiler_params=pltpu.CompilerParams(
            dimension_semantics=("parallel", "arbitrary", "arbitrary"), vmem_limit_bytes=VMEM_LIMIT),
        name="gdn_scan",
    )(o0, qe, a, bm, gam, pb, jnp.tile(norm_g, GDN_HEADS)[None, :])


def gdn_mixer(pb, conv_w, a_log, dt_bias, norm_g):
    q, k, v, gb = gdn_prep(pb, conv_w, a_log, dt_bias)
    o0, qe, a, bm, gam = gdn_chunks(q, k, v, gb)
    return gdn_scan(o0, qe, a, bm, gam, pb, norm_g)


def rms_norm(x, g):
    xf = x.astype(F32)
    y = xf * lax.rsqrt(jnp.mean(xf * xf, axis=-1, keepdims=True) + EPS)
    return (y * g.astype(F32)).astype(x.dtype)


def l2_normalize(t):
    return t * lax.rsqrt(jnp.sum(t * t, axis=-1, keepdims=True) + EPS)


def directional_scan(chunked_fn, seqs, consts, s0, reverse):
    if reverse:
        seqs = tuple(jnp.flip(t, axis=1) for t in seqs)
    o, s = chunked_fn(*seqs, *consts, s0)
    if reverse:
        o = jnp.flip(o, axis=1)
    return o, s


def prefix_bidirectional_scan(chunked_fn, seqs_c, seqs_x, consts, state_shape):
    out_c, out_x = 0.0, 0.0
    for d in range(2):
        rev = d == 1
        s0 = jnp.zeros(state_shape, F32)
        oc, s_ctx = directional_scan(chunked_fn, seqs_c[d], consts[d], s0, rev)
        ox, _ = directional_scan(chunked_fn, seqs_x[d], consts[d], s_ctx, rev)
        out_c = out_c + oc
        out_x = out_x + ox
    return out_c, out_x


def short_conv(x, w):
    k = w.shape[0]
    y = lax.conv_general_dilated(
        x, w[:, None, :].astype(x.dtype), window_strides=(1,), padding=[(k // 2, k // 2)],
        dimension_numbers=('NWC', 'WIO', 'NWC'), feature_group_count=x.shape[-1])
    return jax.nn.silu(y)


def gated_delta_chunked(q, k, v, g, beta, s0):
    b, l, h, _ = q.shape
    dv = v.shape[-1]
    cs = GDN_CHUNK
    nc = l // cs

    def chunks(t):
        return t.reshape(b, nc, cs, h, -1).transpose(1, 0, 3, 2, 4)

    qc, kc, vc = chunks(q), chunks(k), chunks(v)
    gc = jnp.cumsum(chunks(g[..., None])[..., 0], axis=-1)
    bc = chunks(beta[..., None])
    idx = jnp.arange(cs)
    lower = idx[:, None] >= idx[None, :]
    strict = idx[:, None] > idx[None, :]
    decay = jnp.exp(jnp.where(lower, gc[..., :, None] - gc[..., None, :], NEG_INF))
    kb = kc * bc
    lmat = jnp.where(strict, jnp.einsum('nbhid,nbhjd->nbhij', kb, kc) * decay, 0.0)
    a_mat = lmat + jnp.eye(cs, dtype=F32)
    u = lax.linalg.triangular_solve(a_mat, vc * bc, left_side=True, lower=True)
    w = lax.linalg.triangular_solve(a_mat, kb * jnp.exp(gc)[..., None], left_side=True, lower=True)
    attn = jnp.einsum('nbhid,nbhjd->nbhij', qc, kc) * decay

    def step(s, xs):
        q_i, k_i, u_i, w_i, a_i, g_i = xs
        v_new = u_i - jnp.einsum('bhck,bhkv->bhcv', w_i, s)
        o_i = (jnp.einsum('bhck,bhkv->bhcv', q_i * jnp.exp(g_i)[..., None], s)
               + jnp.einsum('bhij,bhjv->bhiv', a_i, v_new))
        g_last = g_i[..., -1:]
        s = (s * jnp.exp(g_last)[..., None]
             + jnp.einsum('bhck,bhcv->bhkv', k_i * jnp.exp(g_last - g_i)[..., None], v_new))
        return s, o_i

    s_fin, o = lax.scan(step, s0, (qc, kc, u, w, attn, gc))
    return o.transpose(1, 0, 3, 2, 4).reshape(b, l, h, dv), s_fin


def gdn_mixer_jax(pb, conv_w, a_log, dt_bias, norm_g):
    def prep(qkv, ab):
        b, l, _ = qkv.shape
        qkv = short_conv(qkv, conv_w).astype(F32)
        q, k, v = jnp.split(qkv, [GDN_HEADS * GDN_DK, 2 * GDN_HEADS * GDN_DK], axis=-1)
        q = l2_normalize(q.reshape(b, l, GDN_HEADS, GDN_DK)) * GDN_DK ** -0.5
        k = l2_normalize(k.reshape(b, l, GDN_HEADS, GDN_DK))
        v = v.reshape(b, l, GDN_HEADS, GDN_DV)
        ab = ab.astype(F32).reshape(b, l, 2, 2, GDN_HEADS)
        g = -jnp.exp(a_log.astype(F32)) * jax.nn.softplus(ab[:, :, :, 0] + dt_bias.astype(F32))
        beta = jax.nn.sigmoid(ab[:, :, :, 1])
        return [(q, k, v, g[:, :, d], beta[:, :, d]) for d in range(2)]

    def parts(rows):
        return rows[..., :768], rows[..., 768:1024], rows[..., 1024:1040]

    qkv_c, gate_c, ab_c = parts(pb[:, :CTX_LEN])
    qkv_x, gate_x, ab_x = parts(pb[:, CTX_LEN:])
    s_shape = (qkv_x.shape[0], GDN_HEADS, GDN_DK, GDN_DV)
    o_c, o_x = prefix_bidirectional_scan(gated_delta_chunked, prep(qkv_c, ab_c), prep(qkv_x, ab_x),
                                         [(), ()], s_shape)

    def gated_out(o, gate):
        b, l = gate.shape[:2]
        y = rms_norm(o, norm_g) * jax.nn.silu(gate.astype(F32)).reshape(o.shape)
        return y.reshape(b, l, GDN_HEADS * GDN_DV).astype(BF16)

    return jnp.concatenate([gated_out(o_c, gate_c), gated_out(o_x, gate_x)], axis=1)


def kernel(x, c, ctx, c_ctx, w_mod, b_mod, norm1, norm2, w_in, w_out, swa_sink, gdn_conv, gdn_a_log, gdn_dt_bias, gdn_norm, mla_q_norm, mla_kv_norm, mla_w_q_up, mla_w_kv_up, ret_log_decay, ret_norm, ffn_w_gate, ffn_w_up, ffn_w_down, moe_router, moe_w_gate, moe_w_up, moe_w_down, final_norm):
    b, n, d = x.shape
    depth = w_in.shape[0]
    cos_t, sin_t = rope_tables(n)
    cos_c, sin_c = cos_t[:, A_ROT_W:A_ROT_W + C_ROT_W], sin_t[:, A_ROT_W:A_ROT_W + C_ROT_W]
    silu_c = jax.nn.silu(c)
    silu_cc = jax.nn.silu(c_ctx)
    h = jnp.concatenate([ctx, x], axis=1)
    for layer in range(depth):
        mod_x = jnp.dot(silu_c, w_mod[layer], precision=lax.Precision.HIGHEST) + b_mod[layer]
        mod_c = jnp.dot(silu_cc, w_mod[layer], precision=lax.Precision.HIGHEST) + b_mod[layer]
        mod = jnp.stack([jnp.broadcast_to(mod_c, mod_x.shape), mod_x], axis=1).reshape(b, 2, 6, d)
        mod = jnp.pad(mod, ((0, 0), (0, 0), (0, SUBLANES - 6), (0, 0)))
        pa, pb, pc, pd = norm_proj(h, mod, norm1[layer][None, :], build_in_weight(w_in[layer]), cos_t, sin_t)
        mix_a = swa_mixer(pa, swa_sink[layer])
        mix_b = gdn_mixer(pb, gdn_conv[layer], gdn_a_log[layer], gdn_dt_bias[layer], gdn_norm[layer])
        mq, mk, mv = mla_prep(pc, mla_q_norm[layer], mla_kv_norm[layer], mla_w_q_up[layer], mla_w_kv_up[layer],
                              cos_c, sin_c)
        mix_c = mla_attention(mq, mk, mv)
        mix_d = retention_mixer(pd, ret_log_decay[layer], ret_norm[layer])
        mixes = (mix_a, mix_b, mix_c, mix_d)
        w_o = build_out_weight(w_out[layer])
        i = layer // 2
        if layer % 2 == 0:
            h, v = out_proj(mixes, h, mod, norm2[layer][None, :], w_o)
            h = dense_ffn(v, h, mod, ffn_w_gate[i].astype(BF16), ffn_w_up[i].astype(BF16),
                          ffn_w_down[i].astype(BF16))
        else:
            w_r = jnp.pad(moe_router[i], ((0, 0), (0, LANES - N_EXPERTS)))
            h, v, logits = out_proj(mixes, h, mod, norm2[layer][None, :], w_o, w_r)
            h = moe_ffn(v, logits, h, mod, moe_w_gate[i].astype(BF16), moe_w_up[i].astype(BF16),
                        moe_w_down[i].astype(BF16))
    return final_rms_norm(h, final_norm[None, :], CTX_LEN // ROW_TILE)
```

```python
import functools

import numpy as np
import jax
import jax.numpy as jnp
from jax import lax
from jax.experimental import pallas as pl
from jax.experimental.pallas import tpu as pltpu

D_MODEL = 1024
GRID_W = 64
CTX_LEN = 256
HEAD_DIM = 64
ROPE_THETA = 10000.0
EPS = 1e-6
NEG_INF = -1e30

SWA_WINDOW = 128
GDN_HEADS = 4
GDN_DK = 64
GDN_DV = 64
GDN_CHUNK = 64
MLA_HEADS = 4
MLA_NOPE = 64
MLA_ROPE = 32
MLA_V = 64
RET_HEADS = 4
RET_DK = 64
D_FF = 3584
N_EXPERTS = 8
TOP_K = 2

LANES = 128
SUBLANES = 8
VMEM_LIMIT = 56 * 1024 * 1024

ROW_TILE = 256
FF_CHUNK = 512
MOE_TILE = 512
MOE_FF_CHUNK = 1792

A_W, B_W, C_W, D_W = 768, 1152, 512, 1024
A_ROT_W, C_ROT_W, D_ROT_W = 640, 128, 512
OFF_A = 0
OFF_B = OFF_A + A_W
OFF_C = OFF_B + B_W
OFF_D = OFF_C + C_W
OFF_AR = OFF_D + D_W
OFF_CR = OFF_AR + A_ROT_W
OFF_DR = OFF_CR + C_ROT_W
W_ALL = OFF_DR + D_ROT_W
ROPE_W = A_ROT_W + C_ROT_W + D_ROT_W

F32 = jnp.float32
BF16 = jnp.bfloat16
NT_DIMS = (((1,), (1,)), ((), ()))
TN_DIMS = (((0,), (0,)), ((), ()))


def _rms(x):
    return x * lax.rsqrt(jnp.mean(x * x, axis=-1, keepdims=True) + EPS)


def _silu(x):
    return x * (1.0 / (1.0 + jnp.exp(-x)))


def _dot(a, b):
    return jnp.dot(a, b, preferred_element_type=F32)


def _dot_nt(a, b):
    return lax.dot_general(a, b, NT_DIMS, preferred_element_type=F32)


def _dot_tn(a, b):
    return lax.dot_general(a, b, TN_DIMS, preferred_element_type=F32)


def _mod_spec(d):
    return pl.BlockSpec((None, None, SUBLANES, d), lambda i, t: (i, jnp.minimum(t, 1), 0, 0))


def _norm_proj_kernel(h_ref, mod_ref, g_ref, w_ref, cos_ref, sin_ref, a_ref, b_ref, c_ref, d_ref):
    x = h_ref[...]
    u = (_rms(x) * g_ref[...] * (1.0 + mod_ref[1:2, :]) + mod_ref[0:1, :]).astype(BF16)

    def mm(lo, width):
        return _dot(u, w_ref[:, lo:lo + width])

    a_main = mm(OFF_A, A_W)
    a_rot = mm(OFF_AR, A_ROT_W)
    a_ref[:, :A_ROT_W] = (a_main[:, :A_ROT_W] * cos_ref[:, :A_ROT_W] + a_rot * sin_ref[:, :A_ROT_W]).astype(BF16)
    a_ref[:, A_ROT_W:] = a_main[:, A_ROT_W:].astype(BF16)
    b_ref[...] = mm(OFF_B, B_W)
    c_main = mm(OFF_C, C_W)
    c_rot = mm(OFF_CR, C_ROT_W)
    lo, hi = A_ROT_W, A_ROT_W + C_ROT_W
    c_ref[:, :C_W - C_ROT_W] = c_main[:, :C_W - C_ROT_W]
    c_ref[:, C_W - C_ROT_W:] = c_main[:, C_W - C_ROT_W:] * cos_ref[:, lo:hi] + c_rot * sin_ref[:, lo:hi]
    d_main = mm(OFF_D, D_W)
    d_rot = mm(OFF_DR, D_ROT_W)
    d_ref[:, :D_ROT_W] = d_main[:, :D_ROT_W] * cos_ref[:, hi:] + d_rot * sin_ref[:, hi:]
    d_ref[:, D_ROT_W:] = d_main[:, D_ROT_W:]


def norm_proj(h, mod, gain, w, cos_t, sin_t):
    b, l, d = h.shape
    tm = ROW_TILE
    row = lambda i, t: (i, t, 0)
    return pl.pallas_call(
        _norm_proj_kernel,
        grid=(b, l // tm),
        in_specs=[
            pl.BlockSpec((None, tm, d), row),
            _mod_spec(d),
            pl.BlockSpec((1, d), lambda i, t: (0, 0)),
            pl.BlockSpec((d, W_ALL), lambda i, t: (0, 0), pipeline_mode=pl.Buffered(1)),
            pl.BlockSpec((tm, ROPE_W), lambda i, t: (t, 0)),
            pl.BlockSpec((tm, ROPE_W), lambda i, t: (t, 0)),
        ],
        out_specs=[pl.BlockSpec((None, tm, A_W), row), pl.BlockSpec((None, tm, B_W), row),
                   pl.BlockSpec((None, tm, C_W), row), pl.BlockSpec((None, tm, D_W), row)],
        out_shape=[jax.ShapeDtypeStruct((b, l, A_W), BF16), jax.ShapeDtypeStruct((b, l, B_W), F32),
                   jax.ShapeDtypeStruct((b, l, C_W), F32), jax.ShapeDtypeStruct((b, l, D_W), F32)],
        compiler_params=pltpu.CompilerParams(
            dimension_semantics=("parallel", "parallel"), vmem_limit_bytes=VMEM_LIMIT),
        name="norm_proj",
    )(h, mod, gain, w, cos_t, sin_t)


def _rot_cols(w, hd):
    x = w.reshape(w.shape[:-1] + (w.shape[-1] // hd, 4, hd // 4))
    x1, x2, x3, x4 = x[..., 0, :], x[..., 1, :], x[..., 2, :], x[..., 3, :]
    return jnp.stack([-x2, x1, -x4, x3], axis=-2).reshape(w.shape)


def _place_swa_q(q):
    z = jnp.zeros((q.shape[0], HEAD_DIM), q.dtype)
    blocks = []
    for h in range(4):
        qh = q[:, HEAD_DIM * h:HEAD_DIM * (h + 1)]
        blocks += [qh, z] if h // 2 == 0 else [z, qh]
    return jnp.concatenate(blocks, axis=1)


def build_in_weight(w):
    d = w.shape[0]
    o = [int(v) for v in np.cumsum((256, 128, 128, 768, 256, 16, 256, 128, 32, 256, 256, 256, 256))]
    aq, ak, av = w[:, :o[0]] * HEAD_DIM ** -0.5, w[:, o[0]:o[1]], w[:, o[1]:o[2]]
    b_main, b_ab = w[:, o[2]:o[4]], w[:, o[4]:o[5]]
    c_q, c_kv, c_kr = w[:, o[5]:o[6]], w[:, o[6]:o[7]], w[:, o[7]:o[8]]
    dq, dk, dvg = w[:, o[8]:o[9]], w[:, o[9]:o[10]] * RET_DK ** -0.5, w[:, o[10]:]
    z = lambda n: jnp.zeros((d, n), w.dtype)
    parts = [
        _place_swa_q(aq), ak, av,
        b_main, b_ab, z(LANES - b_ab.shape[1]),
        c_q, c_kv, z(64), c_kr, z(32),
        dq, dk, dvg,
        _place_swa_q(_rot_cols(aq, HEAD_DIM)), _rot_cols(ak, HEAD_DIM),
        z(64), _rot_cols(c_kr, MLA_ROPE), z(32),
        _rot_cols(dq, HEAD_DIM), _rot_cols(dk, HEAD_DIM),
    ]
    out = jnp.concatenate(parts, axis=1)
    assert out.shape[1] == W_ALL
    return out.astype(BF16)


def rope_tables(n):
    def axial(rot_dim):
        n_freq = rot_dim // 4
        inv_freq = ROPE_THETA ** (-jnp.arange(n_freq, dtype=F32) / n_freq)
        row = jnp.repeat(jnp.arange(n // GRID_W, dtype=F32), GRID_W)
        col = jnp.tile(jnp.arange(GRID_W, dtype=F32), n // GRID_W)
        ang_r = row[:, None] * inv_freq
        ang_c = col[:, None] * inv_freq
        ang = jnp.concatenate([ang_r, ang_r, ang_c, ang_c], axis=-1)
        return jnp.cos(ang), jnp.sin(ang)

    cos_h, sin_h = axial(HEAD_DIM)
    cos_r, sin_r = axial(MLA_ROPE)
    one, zero = jnp.ones((n, 1), F32), jnp.zeros((n, 1), F32)
    cos_c = jnp.concatenate([jnp.tile(one, (1, 64)), cos_r, jnp.tile(one, (1, 32))], axis=1)
    sin_c = jnp.concatenate([jnp.tile(zero, (1, 64)), sin_r, jnp.tile(zero, (1, 32))], axis=1)
    cos_t = jnp.concatenate([jnp.tile(cos_h, (1, A_ROT_W // HEAD_DIM)), cos_c,
                             jnp.tile(cos_h, (1, D_ROT_W // HEAD_DIM))], axis=1)
    sin_t = jnp.concatenate([jnp.tile(sin_h, (1, A_ROT_W // HEAD_DIM)), sin_c,
                             jnp.tile(sin_h, (1, D_ROT_W // HEAD_DIM))], axis=1)
    cos_t = jnp.concatenate([jnp.ones((CTX_LEN, ROPE_W), F32), cos_t], axis=0)
    sin_t = jnp.concatenate([jnp.zeros((CTX_LEN, ROPE_W), F32), sin_t], axis=0)
    return cos_t, sin_t


def _swa_kernel(sink_ref, q_ref, kp_ref, ko_ref, kn_ref, kc_ref, vp_ref, vo_ref, vn_ref, vc_ref, o_ref):
    t = pl.program_id(1)
    last = pl.num_programs(1) - 1
    tq = q_ref.shape[0]
    half = tq // 2

    def head_out(h, pieces):
        q = q_ref[:, LANES * h:LANES * (h + 1)]
        ss = []
        for k_ref, _, mask in pieces:
            s = _dot_nt(q, k_ref[...])
            ss.append(s if mask is None else jnp.where(mask, s, NEG_INF))
        sink = sink_ref[h]
        m = jnp.maximum(functools.reduce(jnp.maximum, [s.max(axis=-1, keepdims=True) for s in ss]), sink)
        ps = [jnp.exp(s - m) for s in ss]
        denom = functools.reduce(jnp.add, [p.sum(axis=-1, keepdims=True) for p in ps]) + jnp.exp(sink - m)
        o = functools.reduce(jnp.add, [_dot(p.astype(BF16), piece[1][...]) for p, piece in zip(ps, pieces)])
        return o / denom

    def write(pieces):
        outs = [head_out(h, pieces) for h in range(4)]
        lane = lax.broadcasted_iota(jnp.int32, (tq, LANES), 1)
        for r in range(2):
            o_ref[:, LANES * r:LANES * (r + 1)] = jnp.where(lane < HEAD_DIM, outs[r], outs[2 + r]).astype(o_ref.dtype)

    @pl.when(t == 0)
    def _():
        write([(kc_ref, vc_ref, None)])

    @pl.when(t > 0)
    def _():
        qi = lax.broadcasted_iota(jnp.int32, (tq, half), 0)
        kj = lax.broadcasted_iota(jnp.int32, (tq, half), 1)
        mask_prev = (kj >= qi) & (t > 1)
        mask_next = (kj <= qi - half) & (t < last)
        qo = lax.broadcasted_iota(jnp.int32, (tq, tq), 0)
        ko = lax.broadcasted_iota(jnp.int32, (tq, tq), 1)
        mask_own = jnp.abs(qo - ko) <= SWA_WINDOW
        write([(kp_ref, vp_ref, mask_prev), (ko_ref, vo_ref, mask_own), (kn_ref, vn_ref, mask_next),
               (kc_ref, vc_ref, None)])


def swa_mixer(pa, sink):
    b, l, _ = pa.shape
    tq = ROW_TILE
    nblk = l // SWA_WINDOW
    kcol, vcol = 4, 5
    prev = lambda c: (lambda i, t: (i, jnp.maximum(2 * t - 1, 2), c))
    nxt = lambda c: (lambda i, t: (i, jnp.minimum(2 * t + 2, nblk - 1), c))
    own = lambda c: (lambda i, t: (i, t, c))
    ctx = lambda c: (lambda i, t: (i, 0, c))
    kv_specs = lambda c: [pl.BlockSpec((None, SWA_WINDOW, LANES), prev(c)), pl.BlockSpec((None, tq, LANES), own(c)),
                          pl.BlockSpec((None, SWA_WINDOW, LANES), nxt(c)), pl.BlockSpec((None, tq, LANES), ctx(c))]
    return pl.pallas_call(
        _swa_kernel,
        grid=(b, l // tq),
        in_specs=[pl.BlockSpec(memory_space=pltpu.SMEM),
                  pl.BlockSpec((None, tq, 4 * LANES), lambda i, t: (i, t, 0))] + kv_specs(kcol) + kv_specs(vcol),
        out_specs=pl.BlockSpec((None, tq, 2 * LANES), lambda i, t: (i, t, 0)),
        out_shape=jax.ShapeDtypeStruct((b, l, 2 * LANES), BF16),
        compiler_params=pltpu.CompilerParams(
            dimension_semantics=("parallel", "parallel"), vmem_limit_bytes=VMEM_LIMIT),
        name="swa",
    )(sink, pa, pa, pa, pa, pa, pa, pa, pa, pa)


def _mla_prep_kernel(c_ref, qn_ref, kvn_ref, wq_ref, wqr_ref, wk_ref, wv_ref, cos_ref, sin_ref,
                     q_ref, k_ref, v_ref):
    cq = c_ref[:, 0:256]
    ckv = c_ref[:, 256:384]
    kr = c_ref[:, 384:512]
    nq = (_rms(cq) * qn_ref[...]).astype(BF16)
    nkv = (_rms(ckv) * kvn_ref[...]).astype(BF16)
    cos = jnp.concatenate([cos_ref[...]] * MLA_HEADS, axis=1)
    sin = jnp.concatenate([sin_ref[...]] * MLA_HEADS, axis=1)
    q_ref[...] = (_dot(nq, wq_ref[...]) * cos + _dot(nq, wqr_ref[...]) * sin).astype(BF16)
    k_ref[...] = (_dot(nkv, wk_ref[...]) + jnp.concatenate([kr] * MLA_HEADS, axis=1)).astype(BF16)
    v_ref[...] = _dot(nkv, wv_ref[...]).astype(BF16)


def mla_prep(pc, q_norm, kv_norm, w_q_up, w_kv_up, cos_c, sin_c):
    b, l, _ = pc.shape
    tm = ROW_TILE
    scale = (MLA_NOPE + MLA_ROPE) ** -0.5 * float(np.log2(np.e))
    wq = (w_q_up * scale).reshape(-1, MLA_HEADS, MLA_NOPE + MLA_ROPE)
    zq = jnp.zeros(wq.shape[:2] + (LANES - MLA_NOPE - MLA_ROPE,), F32)
    wq_main = jnp.concatenate([wq, zq], axis=-1).reshape(-1, MLA_HEADS * LANES)
    wq_rot = jnp.concatenate([jnp.zeros_like(wq[..., :MLA_NOPE]), _rot_cols(wq[..., MLA_NOPE:], MLA_ROPE), zq],
                             axis=-1).reshape(-1, MLA_HEADS * LANES)
    wkv = w_kv_up.reshape(-1, MLA_HEADS, MLA_NOPE + MLA_V)
    wk = jnp.concatenate([wkv[..., :MLA_NOPE], jnp.zeros_like(wkv[..., :LANES - MLA_NOPE])],
                         axis=-1).reshape(-1, MLA_HEADS * LANES)
    wv = wkv[..., MLA_NOPE:].reshape(-1, MLA_HEADS * MLA_V)
    row = lambda i, t: (i, t, 0)
    const = lambda i, t: (0, 0)
    full = lambda a: pl.BlockSpec(a.shape, const)
    args = [q_norm[None, :], kv_norm[None, :], wq_main.astype(BF16), wq_rot.astype(BF16), wk.astype(BF16),
            wv.astype(BF16)]
    return pl.pallas_call(
        _mla_prep_kernel,
        grid=(b, l // tm),
        in_specs=[pl.BlockSpec((None, tm, C_W), row)] + [full(a) for a in args]
        + [pl.BlockSpec((tm, LANES), lambda i, t: (t, 0)), pl.BlockSpec((tm, LANES), lambda i, t: (t, 0))],
        out_specs=[pl.BlockSpec((None, tm, 4 * LANES), row), pl.BlockSpec((None, tm, 4 * LANES), row),
                   pl.BlockSpec((None, tm, 2 * LANES), row)],
        out_shape=[jax.ShapeDtypeStruct((b, l, 4 * LANES), BF16), jax.ShapeDtypeStruct((b, l, 4 * LANES), BF16),
                   jax.ShapeDtypeStruct((b, l, 2 * LANES), BF16)],
        compiler_params=pltpu.CompilerParams(dimension_semantics=("parallel", "parallel")),
        name="mla_prep",
    )(pc, *args, cos_c, sin_c)


def _mla_attn_kernel(q_ref, k_ref, v_ref, o_ref):
    t = pl.program_id(2)
    tq = q_ref.shape[0]

    def attend(nk):
        v = v_ref[0:nk, :]
        v_lane = lax.broadcasted_iota(jnp.int32, v.shape, 1)
        outs = []
        for j in range(2):
            q = q_ref[:, LANES * j:LANES * (j + 1)]
            k = k_ref[0:nk, LANES * j:LANES * (j + 1)]
            s = _dot_nt(q, k)
            p = jnp.exp2(s - s.max(axis=-1, keepdims=True)).astype(BF16)
            other = (v_lane >= MLA_V) if j == 0 else (v_lane < MLA_V)
            o = _dot(p, jnp.where(other, jnp.ones_like(v), v))
            den_lane = MLA_V if j == 0 else 0
            outs.append(o / o[:, den_lane:den_lane + 1])
        lane = lax.broadcasted_iota(jnp.int32, (tq, LANES), 1)
        o_ref[...] = jnp.where(lane < MLA_V, outs[0], outs[1]).astype(o_ref.dtype)

    @pl.when(t == 0)
    def _():
        attend(CTX_LEN)

    @pl.when(t > 0)
    def _():
        attend(k_ref.shape[0])


def mla_attention(q, k, v):
    b, l, _ = q.shape
    tq = ROW_TILE
    return pl.pallas_call(
        _mla_attn_kernel,
        grid=(b, 2, l // tq),
        in_specs=[pl.BlockSpec((None, tq, 2 * LANES), lambda i, p, t: (i, t, p)),
                  pl.BlockSpec((None, l, 2 * LANES), lambda i, p, t: (i, 0, p)),
                  pl.BlockSpec((None, l, LANES), lambda i, p, t: (i, 0, p))],
        out_specs=pl.BlockSpec((None, tq, LANES), lambda i, p, t: (i, t, p)),
        out_shape=jax.ShapeDtypeStruct((b, l, 2 * LANES), BF16),
        compiler_params=pltpu.CompilerParams(
            dimension_semantics=("parallel", "parallel", "parallel"), vmem_limit_bytes=VMEM_LIMIT),
        name="mla_attn",
    )(q, k, v)


def _head_mean(x, ones_bd):
    hi = x.astype(BF16)
    lo = (x - hi.astype(F32)).astype(BF16)
    return (_dot(hi, ones_bd) + _dot(lo, ones_bd)) * (1.0 / HEAD_DIM)


def _ret_kernel(x_ref, lg_ref, g_ref, o_ref, s_ref, dec_ref):
    dr = pl.program_id(1)
    s = pl.program_id(2)
    ns = pl.num_programs(2)
    c = x_ref.shape[0]
    w = RET_HEADS * HEAD_DIM
    chunk = jnp.where(s == 0, 0, jnp.where(dr == 0, s, ns - s))
    lg = lg_ref[...]
    fwd = dr == 0
    row_h = lax.broadcasted_iota(jnp.int32, (w, w), 0) // HEAD_DIM
    col_h = lax.broadcasted_iota(jnp.int32, (w, w), 1) // HEAD_DIM
    same_head = row_h == col_h

    @pl.when(s == 0)
    def _():
        s_ref[...] = jnp.zeros_like(s_ref)
        i = lax.broadcasted_iota(jnp.int32, (c, c), 0)
        j = lax.broadcasted_iota(jnp.int32, (c, c), 1)
        rel = jnp.where(fwd, i - j, j - i)
        relf = jnp.maximum(rel, 0).astype(F32)
        for h in range(RET_HEADS):
            lg_h = lg_ref[0:1, HEAD_DIM * h:HEAD_DIM * h + 1]
            dec_ref[h] = jnp.where(rel >= 0, jnp.exp(lg_h * relf), 0.0)

    q = x_ref[:, 0:w]
    k = x_ref[:, w:2 * w]
    v = x_ref[:, 2 * w:3 * w].astype(BF16)
    kb = k.astype(BF16)
    pos = lax.broadcasted_iota(jnp.int32, (c, 1), 0).astype(F32)
    q_pow = jnp.where(fwd, pos + 1.0, c - pos)
    k_pow = jnp.where(fwd, c - 1.0 - pos, pos)
    lane_h = lax.broadcasted_iota(jnp.int32, (c, w), 1) // HEAD_DIM
    acc = _dot((q * jnp.exp(lg * q_pow)).astype(BF16), s_ref[...].astype(BF16))
    for h in range(RET_HEADS):
        qh = jnp.where(lane_h == h, q, 0.0).astype(BF16)
        a = _dot_nt(qh, kb) * dec_ref[h]
        acc = acc + jnp.where(lane_h == h, _dot(a.astype(BF16), v), 0.0)
    kv = _dot_tn((k * jnp.exp(lg * k_pow)).astype(BF16), v)
    s_ref[...] = s_ref[...] * jnp.exp(lg * float(c)) + jnp.where(same_head, kv, 0.0)

    rows = pl.ds(pl.multiple_of(chunk * c, c), c)

    @pl.when(dr == 0)
    def _():
        o_ref[rows, :] = acc

    @pl.when(dr == 1)
    def _():
        o = o_ref[rows, :] + acc
        ones_bd = jnp.where(same_head, 1.0, 0.0).astype(BF16)
        mu = _head_mean(o, ones_bd)
        var = _head_mean(jnp.square(o - mu), ones_bd)
        y = (o - mu) * lax.rsqrt(var + EPS) * g_ref[...]
        o_ref[rows, :] = y * _silu(x_ref[:, 3 * w:4 * w])


def retention_mixer(pd, log_decay, norm_g):
    b, l, _ = pd.shape
    c = ROW_TILE
    ns = l // c
    w = RET_HEADS * HEAD_DIM
    lg = jnp.repeat(-jnp.exp(log_decay.astype(F32)), HEAD_DIM, axis=-1)[:, None, :]

    def chunk_of(dr, s):
        return jnp.where(s == 0, 0, jnp.where(dr == 0, s, ns - s))

    return pl.pallas_call(
        _ret_kernel,
        grid=(b, 2, ns),
        in_specs=[pl.BlockSpec((None, c, D_W), lambda i, dr, s: (i, chunk_of(dr, s), 0)),
                  pl.BlockSpec((None, 1, w), lambda i, dr, s: (dr, 0, 0)),
                  pl.BlockSpec((1, w), lambda i, dr, s: (0, 0))],
        out_specs=pl.BlockSpec((None, l, w), lambda i, dr, s: (i, 0, 0)),
        out_shape=jax.ShapeDtypeStruct((b, l, w), F32),
        scratch_shapes=[pltpu.VMEM((w, w), F32), pltpu.VMEM((RET_HEADS, c, c), F32)],
        compiler_params=pltpu.CompilerParams(
            dimension_semantics=("parallel", "arbitrary", "arbitrary"), vmem_limit_bytes=VMEM_LIMIT),
        name="retention",
    )(pd, lg, norm_g[None, :])


def _out_proj_kernel(ma_ref, mb_ref, mc_ref, md_ref, h_ref, mod_ref, g_ref, w_ref, *rest, with_router):
    if with_router:
        wr_ref, hn_ref, v_ref, lg_ref = rest
    else:
        hn_ref, v_ref = rest
    gw = 2 * LANES
    mix = functools.reduce(jnp.add, [
        _dot(m_ref[...].astype(BF16), w_ref[gw * i:gw * (i + 1), :])
        for i, m_ref in enumerate((ma_ref, mb_ref, mc_ref, md_ref))])
    hn = h_ref[...] + mod_ref[2:3, :] * mix
    hn_ref[...] = hn
    v = _rms(hn) * g_ref[...] * (1.0 + mod_ref[4:5, :]) + mod_ref[3:4, :]
    v_ref[...] = v.astype(v_ref.dtype)
    if with_router:
        lg_ref[...] = jnp.dot(v, wr_ref[...], preferred_element_type=F32, precision=lax.Precision.HIGHEST)


def out_proj(mixes, h, mod, gain, w, w_router=None):
    b, l, d = h.shape
    tm = ROW_TILE
    with_router = w_router is not None
    row = lambda i, t: (i, t, 0)
    in_specs = [pl.BlockSpec((None, tm, 2 * LANES), row) for _ in mixes] + [
        pl.BlockSpec((None, tm, d), row),
        _mod_spec(d),
        pl.BlockSpec((1, d), lambda i, t: (0, 0)),
        pl.BlockSpec(w.shape, lambda i, t: (0, 0), pipeline_mode=pl.Buffered(1)),
    ]
    out_specs = [pl.BlockSpec((None, tm, d), row), pl.BlockSpec((None, tm, d), row)]
    out_shape = [jax.ShapeDtypeStruct((b, l, d), F32), jax.ShapeDtypeStruct((b, l, d), F32 if with_router else BF16)]
    args = list(mixes) + [h, mod, gain, w]
    if with_router:
        in_specs.append(pl.BlockSpec(w_router.shape, lambda i, t: (0, 0)))
        out_specs.append(pl.BlockSpec((None, tm, LANES), row))
        out_shape.append(jax.ShapeDtypeStruct((b, l, LANES), F32))
        args.append(w_router)
    return pl.pallas_call(
        functools.partial(_out_proj_kernel, with_router=with_router),
        grid=(b, l // tm),
        in_specs=in_specs,
        out_specs=out_specs,
        out_shape=out_shape,
        compiler_params=pltpu.CompilerParams(
            dimension_semantics=("parallel", "parallel"), vmem_limit_bytes=VMEM_LIMIT),
        name="out_proj",
    )(*args)


def build_out_weight(w):
    hd = HEAD_DIM
    perm = jnp.concatenate([w[0:hd], w[2 * hd:3 * hd], w[hd:2 * hd], w[3 * hd:4 * hd]], axis=0)
    return jnp.concatenate([perm, w[4 * hd:]], axis=0).astype(BF16)


def _ffn_kernel(v_ref, h_ref, mod_ref, wg_ref, wu_ref, wd_ref, o_ref):
    v = v_ref[...]
    acc = jnp.zeros(o_ref.shape, F32)
    for j in range(D_FF // FF_CHUNK):
        cols = slice(j * FF_CHUNK, (j + 1) * FF_CHUNK)
        a = _dot(v, wg_ref[:, cols])
        u = _dot(v, wu_ref[:, cols])
        mid = (_silu(a) * u).astype(BF16)
        acc = acc + _dot(mid, wd_ref[cols, :])
    o_ref[...] = h_ref[...] + mod_ref[5:6, :] * acc


def dense_ffn(v, h, mod, wg, wu, wd):
    b, l, d = h.shape
    tm = ROW_TILE
    row = lambda i, t: (i, t, 0)
    const = lambda i, t: (0, 0)
    return pl.pallas_call(
        _ffn_kernel,
        grid=(b, l // tm),
        in_specs=[
            pl.BlockSpec((None, tm, d), row),
            pl.BlockSpec((None, tm, d), row),
            _mod_spec(d),
            pl.BlockSpec(wg.shape, const, pipeline_mode=pl.Buffered(1)),
            pl.BlockSpec(wu.shape, const, pipeline_mode=pl.Buffered(1)),
            pl.BlockSpec(wd.shape, const, pipeline_mode=pl.Buffered(1)),
        ],
        out_specs=pl.BlockSpec((None, tm, d), row),
        out_shape=jax.ShapeDtypeStruct((b, l, d), F32),
        compiler_params=pltpu.CompilerParams(
            dimension_semantics=("parallel", "parallel"), vmem_limit_bytes=VMEM_LIMIT),
        name="dense_ffn",
    )(v, h, mod, wg, wu, wd)


def _moe_kernel(wt_ref, we_ref, lo_ref, hi_ref, first_ref, x_ref, wg_ref, wu_ref, wd_ref, o_ref, xm_ref, acc_ref):
    w = pl.program_id(0)
    j = pl.program_id(1)
    nj = pl.num_programs(1)
    tm = x_ref.shape[0]

    @pl.when(j == 0)
    def _():
        row = wt_ref[w] * tm + lax.broadcasted_iota(jnp.int32, (tm, 1), 0)
        keep = (row >= lo_ref[w]) & (row < hi_ref[w])
        xm_ref[...] = jnp.where(keep, x_ref[...], 0.0).astype(BF16)

    @pl.when((j == 0) & (first_ref[w] > 0))
    def _():
        acc_ref[...] = jnp.zeros_like(acc_ref)

    half = tm // 2
    for part in range(2):
        part_lo = wt_ref[w] * tm + part * half

        @pl.when((hi_ref[w] > part_lo) & (lo_ref[w] < part_lo + half) & (hi_ref[w] > lo_ref[w]))
        def _():
            rows = slice(part * half, (part + 1) * half)
            x = xm_ref[rows, :]
            a = _dot(x, wg_ref[...])
            u = _dot(x, wu_ref[...])
            mid = (_silu(a) * u).astype(BF16)
            acc_ref[rows, :] += _dot(mid, wd_ref[...])

    @pl.when(j == nj - 1)
    def _():
        o_ref[...] = acc_ref[...]


def moe_grouped_ffn(xs, items, layer_idx, wg, wu, wd):
    s, d = xs.shape
    tm = MOE_TILE
    fc = MOE_FF_CHUNK
    nw = items[0].shape[0]
    nj = D_FF // fc
    grid_spec = pltpu.PrefetchScalarGridSpec(
        num_scalar_prefetch=5,
        grid=(nw, nj),
        in_specs=[
            pl.BlockSpec((tm, d), lambda w, j, wt, we, lo, hi, fi: (wt[w], 0)),
            pl.BlockSpec((None, None, d, fc), lambda w, j, wt, we, lo, hi, fi: (layer_idx, we[w], 0, j)),
            pl.BlockSpec((None, None, d, fc), lambda w, j, wt, we, lo, hi, fi: (layer_idx, we[w], 0, j)),
            pl.BlockSpec((None, None, fc, d), lambda w, j, wt, we, lo, hi, fi: (layer_idx, we[w], j, 0)),
        ],
        out_specs=pl.BlockSpec((tm, d), lambda w, j, wt, we, lo, hi, fi: (wt[w], 0)),
        scratch_shapes=[pltpu.VMEM((tm, d), BF16), pltpu.VMEM((tm, d), F32)],
    )
    return pl.pallas_call(
        _moe_kernel,
        grid_spec=grid_spec,
        out_shape=jax.ShapeDtypeStruct((s, d), F32),
        compiler_params=pltpu.CompilerParams(
            dimension_semantics=("arbitrary", "arbitrary"), vmem_limit_bytes=VMEM_LIMIT),
        name="moe_ffn",
    )(*items, xs, wg, wu, wd)


def _residual_kernel(h_ref, f0_ref, f1_ref, gate_ref, mod_ref, o_ref):
    f = gate_ref[:, 0:1] * f0_ref[...] + gate_ref[:, 1:2] * f1_ref[...]
    o_ref[...] = h_ref[...] + mod_ref[5:6, :] * f


def gated_residual(h, f0, f1, gates, mod):
    b, l, d = h.shape
    tm = ROW_TILE
    row = lambda i, t: (i, t, 0)
    return pl.pallas_call(
        _residual_kernel,
        grid=(b, l // tm),
        in_specs=[pl.BlockSpec((None, tm, d), row), pl.BlockSpec((None, tm, d), row), pl.BlockSpec((None, tm, d), row),
                  pl.BlockSpec((None, tm, LANES), row), _mod_spec(d)],
        out_specs=pl.BlockSpec((None, tm, d), row),
        out_shape=jax.ShapeDtypeStruct((b, l, d), F32),
        compiler_params=pltpu.CompilerParams(dimension_semantics=("parallel", "parallel")),
        name="gated_residual",
    )(h, f0, f1, gates, mod)


def moe_ffn(v, logits, h, mod, layer_idx, wg, wu, wd):
    b, l, d = h.shape
    t = b * l
    s = TOP_K * t
    tm = MOE_TILE
    nt = s // tm
    nw = nt + N_EXPERTS - 1
    i32 = jnp.int32
    lg = logits.reshape(t, LANES)[:, :N_EXPERTS]
    top_val, top_idx = lax.top_k(lg, TOP_K)
    gates = jax.nn.softmax(top_val, axis=-1)
    slot = jnp.arange(s, dtype=i32)
    skey = jnp.sort(top_idx.reshape(-1).astype(i32) * s + slot)
    order = skey % s
    _, inv = lax.sort_key_val(order, slot)
    bounds = (jnp.arange(N_EXPERTS, dtype=i32) + 1) * s
    cum = jnp.sum((skey[None, :] < bounds[:, None]).astype(i32), axis=1)
    cum_prev = jnp.concatenate([jnp.zeros((1,), i32), cum[:-1]])
    tile_lo = jnp.arange(nt, dtype=i32) * tm
    count_le = lambda edges, x: jnp.sum((edges[None, :] <= x[:, None]).astype(i32), axis=1)
    e_first = count_le(cum, tile_lo)
    e_last = count_le(cum, tile_lo + tm - 1)
    n_items = e_last - e_first + 1
    item_end = jnp.cumsum(n_items)
    item_start = item_end - n_items
    w = jnp.arange(nw, dtype=i32)
    wt = jnp.minimum(count_le(item_end, w), nt - 1)
    valid = w < item_end[-1]
    we = jnp.clip(e_first[wt] + w - item_start[wt], 0, N_EXPERTS - 1).astype(i32)
    lo = jnp.where(valid, cum_prev[we], 0).astype(i32)
    hi = jnp.where(valid, cum[we], 0).astype(i32)
    first = (valid & (w == item_start[wt])).astype(i32)
    rows_of = lambda a, idx: a.at[idx].get(mode="promise_in_bounds")
    xs = rows_of(v.reshape(t, d), order // TOP_K)
    ys = moe_grouped_ffn(xs, (wt, we, lo, hi, first), layer_idx, wg, wu, wd)
    dest = inv.reshape(t, TOP_K)
    f0 = rows_of(ys, dest[:, 0]).reshape(b, l, d)
    f1 = rows_of(ys, dest[:, 1]).reshape(b, l, d)
    gates_p = jnp.pad(gates, ((0, 0), (0, LANES - TOP_K))).reshape(b, l, LANES)
    return gated_residual(h, f0, f1, gates_p, mod)


def _final_norm_kernel(h_ref, g_ref, o_ref):
    o_ref[...] = _rms(h_ref[...]) * g_ref[...]


def final_rms_norm(h, gain, n_ctx_tiles):
    b, l, d = h.shape
    tm = ROW_TILE
    n = l - n_ctx_tiles * tm
    return pl.pallas_call(
        _final_norm_kernel,
        grid=(b, n // tm),
        in_specs=[
            pl.BlockSpec((None, tm, d), lambda i, t: (i, t + n_ctx_tiles, 0)),
            pl.BlockSpec((1, d), lambda i, t: (0, 0)),
        ],
        out_specs=pl.BlockSpec((None, tm, d), lambda i, t: (i, t, 0)),
        out_shape=jax.ShapeDtypeStruct((b, n, d), F32),
        compiler_params=pltpu.CompilerParams(dimension_semantics=("parallel", "parallel")),
        name="final_norm",
    )(h, gain)


GDN_W = GDN_HEADS * GDN_DK
GDN_CONV_K = 5
GDN_HALO = SUBLANES


def _split3(x):
    p0 = x.astype(BF16)
    r1 = x - p0.astype(F32)
    p1 = r1.astype(BF16)
    p2 = (r1 - p1.astype(F32)).astype(BF16)
    return p0, p1, p2


def _gdn_prep_kernel(x_ref, prev_ref, next_ref, cw_ref, par_ref, q_ref, k_ref, v_ref, gb_ref):
    t = pl.program_id(1)
    last = pl.num_programs(1) - 1
    tm = x_ref.shape[0]
    w3 = 3 * GDN_W
    has_prev = t > 1
    has_next = (t > 0) & (t < last)
    prev = jnp.where(has_prev, prev_ref[...], 0.0)
    nxt = jnp.where(has_next, next_ref[...], 0.0)
    xe = jnp.concatenate([prev, x_ref[:, :w3], nxt], axis=0)
    y = jnp.zeros((tm, w3), F32)
    for j in range(GDN_CONV_K):
        lo = GDN_HALO - GDN_CONV_K // 2 + j
        y = y + cw_ref[j:j + 1, :] * xe[lo:lo + tm, :]
    y = _silu(y)
    r = lax.broadcasted_iota(jnp.int32, (GDN_W, GDN_W), 0)
    c = lax.broadcasted_iota(jnp.int32, (GDN_W, GDN_W), 1)
    ones_bd = jnp.where(r // GDN_DK == c // GDN_DK, 1.0, 0.0).astype(BF16)

    def l2n(x):
        sq = x * x
        hi = sq.astype(BF16)
        lo = (sq - hi.astype(F32)).astype(BF16)
        return x * lax.rsqrt(_dot(hi, ones_bd) + _dot(lo, ones_bd) + EPS)

    q_ref[...] = l2n(y[:, :GDN_W]) * GDN_DK ** -0.5
    k_ref[...] = l2n(y[:, GDN_W:2 * GDN_W])
    v_ref[...] = y[:, 2 * GDN_W:]
    ab = x_ref[:, w3 + GDN_W:]
    lane = lax.broadcasted_iota(jnp.int32, ab.shape, 1)
    is_g = (lane % 8) < 4
    z = ab + par_ref[1:2, :]
    softplus = jnp.maximum(z, 0.0) + jnp.log1p(jnp.exp(-jnp.abs(z)))
    g = jnp.where(is_g, par_ref[0:1, :] * softplus, 0.0)
    beta = 1.0 / (1.0 + jnp.exp(-ab))
    i = lax.broadcasted_iota(jnp.int32, (tm, tm), 0)
    j = lax.broadcasted_iota(jnp.int32, (tm, tm), 1)
    same_chunk = i // GDN_CHUNK == j // GDN_CHUNK
    tri_f = jnp.where(same_chunk & (j <= i), 1.0, 0.0).astype(BF16)
    tri_b = jnp.where(same_chunk & (j >= i), 1.0, 0.0).astype(BF16)
    pieces = _split3(g)
    gc_f = functools.reduce(jnp.add, [_dot(tri_f, p) for p in pieces])
    gc_b = functools.reduce(jnp.add, [_dot(tri_b, p) for p in pieces])
    gb_ref[...] = jnp.where(is_g, jnp.where(lane < 8, gc_f, gc_b), beta)


def gdn_prep(pb, conv_w, a_log, dt_bias):
    b, l, _ = pb.shape
    tm = ROW_TILE
    w3 = 3 * GDN_W
    halo_blocks = tm // GDN_HALO
    n_halo = l // GDN_HALO
    cw = jnp.pad(conv_w, ((0, SUBLANES - GDN_CONV_K), (0, 0)))
    neg_a = jnp.pad(-jnp.exp(a_log.astype(F32)), ((0, 0), (0, 4))).reshape(-1)
    dtb = jnp.pad(dt_bias.astype(F32), ((0, 0), (0, 4))).reshape(-1)
    par = jnp.pad(jnp.stack([neg_a, dtb]), ((0, SUBLANES - 2), (0, LANES - 16)))
    row = lambda i, t: (i, t, 0)
    out = lambda w: pl.BlockSpec((None, tm, w), row)
    return pl.pallas_call(
        _gdn_prep_kernel,
        grid=(b, l // tm),
        in_specs=[pl.BlockSpec((None, tm, B_W), row),
                  pl.BlockSpec((None, GDN_HALO, w3), lambda i, t: (i, jnp.maximum(t * halo_blocks - 1, 0), 0)),
                  pl.BlockSpec((None, GDN_HALO, w3),
                               lambda i, t: (i, jnp.minimum((t + 1) * halo_blocks, n_halo - 1), 0)),
                  pl.BlockSpec(cw.shape, lambda i, t: (0, 0)),
                  pl.BlockSpec(par.shape, lambda i, t: (0, 0))],
        out_specs=[out(GDN_W), out(GDN_W), out(GDN_W), out(LANES)],
        out_shape=[jax.ShapeDtypeStruct((b, l, GDN_W), F32)] * 3 + [jax.ShapeDtypeStruct((b, l, LANES), F32)],
        compiler_params=pltpu.CompilerParams(
            dimension_semantics=("parallel", "parallel"), vmem_limit_bytes=VMEM_LIMIT),
        name="gdn_prep",
    )(pb, pb, pb, cw, par)


def _tile_heads(x):
    return jnp.concatenate([x] * GDN_HEADS, axis=0)


def _collapse_heads(x):
    c = GDN_CHUNK
    return x[0:c] + x[c:2 * c] + x[2 * c:3 * c] + x[3 * c:4 * c]


def _gdn_chunk_kernel(q_ref, k_ref, v_ref, gb_ref, o0_ref, qe_ref, a_ref, bm_ref, gam_ref):
    dr = pl.program_id(1)
    fwd = dr == 0
    n = GDN_W
    cs = GDN_CHUNK
    r = lax.broadcasted_iota(jnp.int32, (n, n), 0)
    c = lax.broadcasted_iota(jnp.int32, (n, n), 1)
    ri, ci = r % cs, c % cs
    head = r // cs == c // cs
    ahead = (ri - ci) * jnp.where(fwd, 1, -1)
    tri = head & (ahead >= 0)
    tri_strict = head & (ahead > 0)
    eye = jnp.where(r == c, 1.0, 0.0)
    blk = lambda s: r // s == c // s
    b8, b16, b32 = blk(8), blk(16), blk(32)
    lane = lax.broadcasted_iota(jnp.int32, (n, LANES), 1)
    row_head = lax.broadcasted_iota(jnp.int32, (n, LANES), 0) // cs
    sel_g = lane == dr * 8 + row_head
    sel_b = lane == dr * 8 + 4 + row_head
    pick = lambda sel, x: jnp.sum(jnp.where(sel, x, 0.0), axis=1, keepdims=True)
    mm = lambda a, b: _dot(a.astype(BF16), b.astype(BF16))

    rows = [slice(ch * cs, (ch + 1) * cs) for ch in range(q_ref.shape[0] // cs)]
    each = lambda f, *xs: [f(*a) for a in zip(*xs)]
    bf = lambda xs: [x.astype(BF16) for x in xs]
    spread = lambda ref: [jnp.where(head, _tile_heads(ref[rw, :]), 0.0) for rw in rows]
    kh, qh, vh = spread(k_ref), spread(q_ref), spread(v_ref)
    gb = [gb_ref[rw, :] for rw in rows]
    gb4 = [_tile_heads(x) for x in gb]
    gc = [pick(sel_g, x) for x in gb4]
    beta = [pick(sel_b, x) for x in gb4]
    gl = [pick(sel_g, jnp.broadcast_to(jnp.where(fwd, x[cs - 1:cs, :], x[0:1, :]), (n, LANES))) for x in gb]
    gc_b = [jnp.broadcast_to(x, (n, n)) for x in gc]
    decay = [jnp.exp(jnp.minimum(x - x.T, 0.0)) for x in gc_b]
    khb, qhb = bf(kh), bf(qh)
    kk = each(_dot_nt, khb, khb)
    qk = each(_dot_nt, qhb, khb)
    lmat = each(lambda b_, kk_, d_: jnp.where(tri_strict, b_ * kk_ * d_, 0.0), beta, kk, decay)
    attn = bf(each(lambda qk_, d_: jnp.where(tri, qk_ * d_, 0.0), qk, decay))
    nl = bf([jnp.where(b8, -x, 0.0) for x in lmat])
    n2 = bf(each(_dot, nl, nl))
    n4 = each(_dot, n2, n2)
    p1 = bf(each(lambda a, b_: _dot((eye + a).astype(BF16), (eye + b_).astype(BF16)), nl, n2))
    tinv = each(lambda p, x: _dot(p, (eye + x).astype(BF16)), p1, n4)
    for inner, outer in ((b8, b16), (b16, b32), (b32, head)):
        off = bf([jnp.where(outer & ~inner, x, 0.0) for x in lmat])
        tb = bf(tinv)
        to = bf(each(_dot, tb, off))
        tinv = each(lambda t_, to_, tb_: t_ - _dot(to_, tb_), tinv, to, tb)
    tb = bf(tinv)
    eg = [jnp.exp(x) for x in gc]
    u = bf(each(lambda t_, b_, v_: _dot(t_, (b_ * v_).astype(BF16)), tb, beta, vh))
    w = bf(each(lambda t_, b_, e_, k_: _dot(t_, ((b_ * e_) * k_).astype(BF16)), tb, beta, eg, kh))
    o0 = each(_dot, attn, u)
    qe = each(lambda q_, e_, a_, w_: q_ * e_ - _dot(a_, w_), qh, eg, attn, w)
    kg = bf(each(lambda k_, gl_, gc_: k_ * jnp.exp(gl_ - gc_), kh, gl, gc))
    a_mat = each(_dot_tn, kg, w)
    b_mat = each(_dot_tn, kg, u)
    for rw, o0_, qe_, a_, b_, gl_ in zip(rows, o0, qe, a_mat, b_mat, gl):
        o0_ref[rw, :] = _collapse_heads(o0_)
        qe_ref[rw, :] = _collapse_heads(qe_)
        a_ref[rw, :] = _collapse_heads(a_)
        bm_ref[rw, :] = _collapse_heads(b_)
        gam_ref[rw, :] = _collapse_heads(jnp.where(head, jnp.broadcast_to(jnp.exp(gl_), (n, n)), 0.0))


def gdn_chunks(q, k, v, gb):
    b, l, _ = q.shape
    tm = ROW_TILE
    row = lambda i, dr, t: (i, t, 0)
    out = pl.BlockSpec((None, None, tm, GDN_W), lambda i, dr, t: (dr, i, t, 0))
    return pl.pallas_call(
        _gdn_chunk_kernel,
        grid=(b, 2, l // tm),
        in_specs=[pl.BlockSpec((None, tm, GDN_W), row)] * 3 + [pl.BlockSpec((None, tm, LANES), row)],
        out_specs=[out] * 5,
        out_shape=[jax.ShapeDtypeStruct((2, b, l, GDN_W), F32)] * 5,
        compiler_params=pltpu.CompilerParams(
            dimension_semantics=("parallel", "parallel", "parallel"), vmem_limit_bytes=VMEM_LIMIT),
        name="gdn_chunk",
    )(q, k, v, gb)


def _gdn_scan_kernel(o0_ref, qe_ref, a_ref, bm_ref, gam_ref, gate_ref, g_ref, o_ref, s_ref):
    dr = pl.program_id(1)
    s = pl.program_id(2)
    ns = pl.num_programs(2)
    tm = o0_ref.shape[0]
    cs = GDN_CHUNK
    n = GDN_W
    nch = tm // cs
    tile = jnp.where(s == 0, 0, jnp.where(dr == 0, s, ns - s))
    r = lax.broadcasted_iota(jnp.int32, (n, n), 0)
    c = lax.broadcasted_iota(jnp.int32, (n, n), 1)
    head = r // cs == c // cs

    @pl.when(s == 0)
    def _():
        s_ref[...] = jnp.zeros_like(s_ref)

    def run(order):
        state = s_ref[...]
        outs = {}
        for ch in order:
            rows = slice(ch * cs, (ch + 1) * cs)
            sb = state.astype(BF16)
            outs[ch] = o0_ref[rows, :] + _dot(qe_ref[rows, :].astype(BF16), sb)
            a_full = jnp.where(head, _tile_heads(a_ref[rows, :]), 0.0).astype(BF16)
            b_full = jnp.where(head, _tile_heads(bm_ref[rows, :]), 0.0)
            state = _tile_heads(gam_ref[rows, :]) * state - _dot(a_full, sb) + b_full
        s_ref[...] = state
        return jnp.concatenate([outs[ch] for ch in range(nch)], axis=0)

    rows_out = pl.ds(pl.multiple_of(tile * tm, tm), tm)

    @pl.when(dr == 0)
    def _():
        o_ref[rows_out, :] = run(range(nch))

    @pl.when(dr == 1)
    def _():
        o = o_ref[rows_out, :] + run(range(nch - 1, -1, -1))
        ones_bd = jnp.where(head, 1.0, 0.0).astype(BF16)
        ms = _head_mean(o * o, ones_bd)
        o_ref[rows_out, :] = o * lax.rsqrt(ms + EPS) * g_ref[...] * _silu(gate_ref[...])


def gdn_scan(o0, qe, a, bm, gam, pb, norm_g):
    _, b, l, _ = o0.shape
    tm = ROW_TILE
    ns = l // tm

    def tile_of(dr, s):
        return jnp.where(s == 0, 0, jnp.where(dr == 0, s, ns - s))

    per_dir = pl.BlockSpec((None, None, tm, GDN_W), lambda i, dr, s: (dr, i, tile_of(dr, s), 0))
    gate_col = 3 * GDN_W // GDN_W
    return pl.pallas_call(
        _gdn_scan_kernel,
        grid=(b, 2, ns),
        in_specs=[per_dir] * 5 + [
            pl.BlockSpec((None, tm, GDN_W), lambda i, dr, s: (i, tile_of(dr, s), gate_col)),
            pl.BlockSpec((1, GDN_W), lambda i, dr, s: (0, 0))],
        out_specs=pl.BlockSpec((None, l, GDN_W), lambda i, dr, s: (i, 0, 0)),
        out_shape=jax.ShapeDtypeStruct((b, l, GDN_W), F32),
        scratch_shapes=[pltpu.VMEM((GDN_W, GDN_W), F32)],
        compiler_params=pltpu.CompilerParams(
            dimension_semantics=("parallel", "arbitrary", "arbitrary"), vmem_limit_bytes=VMEM_LIMIT),
        name="gdn_scan",
    )(o0, qe, a, bm, gam, pb, jnp.tile(norm_g, GDN_HEADS)[None, :])


def gdn_mixer(pb, conv_w, a_log, dt_bias, norm_g):
    q, k, v, gb = gdn_prep(pb, conv_w, a_log, dt_bias)
    o0, qe, a, bm, gam = gdn_chunks(q, k, v, gb)
    return gdn_scan(o0, qe, a, bm, gam, pb, norm_g)


def rms_norm(x, g):
    xf = x.astype(F32)
    y = xf * lax.rsqrt(jnp.mean(xf * xf, axis=-1, keepdims=True) + EPS)
    return (y * g.astype(F32)).astype(x.dtype)


def l2_normalize(t):
    return t * lax.rsqrt(jnp.sum(t * t, axis=-1, keepdims=True) + EPS)


def directional_scan(chunked_fn, seqs, consts, s0, reverse):
    if reverse:
        seqs = tuple(jnp.flip(t, axis=1) for t in seqs)
    o, s = chunked_fn(*seqs, *consts, s0)
    if reverse:
        o = jnp.flip(o, axis=1)
    return o, s


def prefix_bidirectional_scan(chunked_fn, seqs_c, seqs_x, consts, state_shape):
    out_c, out_x = 0.0, 0.0
    for d in range(2):
        rev = d == 1
        s0 = jnp.zeros(state_shape, F32)
        oc, s_ctx = directional_scan(chunked_fn, seqs_c[d], consts[d], s0, rev)
        ox, _ = directional_scan(chunked_fn, seqs_x[d], consts[d], s_ctx, rev)
        out_c = out_c + oc
        out_x = out_x + ox
    return out_c, out_x


def short_conv(x, w):
    k = w.shape[0]
    y = lax.conv_general_dilated(
        x, w[:, None, :].astype(x.dtype), window_strides=(1,), padding=[(k // 2, k // 2)],
        dimension_numbers=('NWC', 'WIO', 'NWC'), feature_group_count=x.shape[-1])
    return jax.nn.silu(y)


def gated_delta_chunked(q, k, v, g, beta, s0):
    b, l, h, _ = q.shape
    dv = v.shape[-1]
    cs = GDN_CHUNK
    nc = l // cs

    def chunks(t):
        return t.reshape(b, nc, cs, h, -1).transpose(1, 0, 3, 2, 4)

    qc, kc, vc = chunks(q), chunks(k), chunks(v)
    gc = jnp.cumsum(chunks(g[..., None])[..., 0], axis=-1)
    bc = chunks(beta[..., None])
    idx = jnp.arange(cs)
    lower = idx[:, None] >= idx[None, :]
    strict = idx[:, None] > idx[None, :]
    decay = jnp.exp(jnp.where(lower, gc[..., :, None] - gc[..., None, :], NEG_INF))
    kb = kc * bc
    lmat = jnp.where(strict, jnp.einsum('nbhid,nbhjd->nbhij', kb, kc) * decay, 0.0)
    a_mat = lmat + jnp.eye(cs, dtype=F32)
    u = lax.linalg.triangular_solve(a_mat, vc * bc, left_side=True, lower=True)
    w = lax.linalg.triangular_solve(a_mat, kb * jnp.exp(gc)[..., None], left_side=True, lower=True)
    attn = jnp.einsum('nbhid,nbhjd->nbhij', qc, kc) * decay

    def step(s, xs):
        q_i, k_i, u_i, w_i, a_i, g_i = xs
        v_new = u_i - jnp.einsum('bhck,bhkv->bhcv', w_i, s)
        o_i = (jnp.einsum('bhck,bhkv->bhcv', q_i * jnp.exp(g_i)[..., None], s)
               + jnp.einsum('bhij,bhjv->bhiv', a_i, v_new))
        g_last = g_i[..., -1:]
        s = (s * jnp.exp(g_last)[..., None]
             + jnp.einsum('bhck,bhcv->bhkv', k_i * jnp.exp(g_last - g_i)[..., None], v_new))
        return s, o_i

    s_fin, o = lax.scan(step, s0, (qc, kc, u, w, attn, gc))
    return o.transpose(1, 0, 3, 2, 4).reshape(b, l, h, dv), s_fin


def gdn_mixer_jax(pb, conv_w, a_log, dt_bias, norm_g):
    def prep(qkv, ab):
        b, l, _ = qkv.shape
        qkv = short_conv(qkv, conv_w).astype(F32)
        q, k, v = jnp.split(qkv, [GDN_HEADS * GDN_DK, 2 * GDN_HEADS * GDN_DK], axis=-1)
        q = l2_normalize(q.reshape(b, l, GDN_HEADS, GDN_DK)) * GDN_DK ** -0.5
        k = l2_normalize(k.reshape(b, l, GDN_HEADS, GDN_DK))
        v = v.reshape(b, l, GDN_HEADS, GDN_DV)
        ab = ab.astype(F32).reshape(b, l, 2, 2, GDN_HEADS)
        g = -jnp.exp(a_log.astype(F32)) * jax.nn.softplus(ab[:, :, :, 0] + dt_bias.astype(F32))
        beta = jax.nn.sigmoid(ab[:, :, :, 1])
        return [(q, k, v, g[:, :, d], beta[:, :, d]) for d in range(2)]

    def parts(rows):
        return rows[..., :768], rows[..., 768:1024], rows[..., 1024:1040]

    qkv_c, gate_c, ab_c = parts(pb[:, :CTX_LEN])
    qkv_x, gate_x, ab_x = parts(pb[:, CTX_LEN:])
    s_shape = (qkv_x.shape[0], GDN_HEADS, GDN_DK, GDN_DV)
    o_c, o_x = prefix_bidirectional_scan(gated_delta_chunked, prep(qkv_c, ab_c), prep(qkv_x, ab_x),
                                         [(), ()], s_shape)

    def gated_out(o, gate):
        b, l = gate.shape[:2]
        y = rms_norm(o, norm_g) * jax.nn.silu(gate.astype(F32)).reshape(o.shape)
        return y.reshape(b, l, GDN_HEADS * GDN_DV).astype(BF16)

    return jnp.concatenate([gated_out(o_c, gate_c), gated_out(o_x, gate_x)], axis=1)


def kernel(x, c, ctx, c_ctx, w_mod, b_mod, norm1, norm2, w_in, w_out, swa_sink, gdn_conv, gdn_a_log, gdn_dt_bias, gdn_norm, mla_q_norm, mla_kv_norm, mla_w_q_up, mla_w_kv_up, ret_log_decay, ret_norm, ffn_w_gate, ffn_w_up, ffn_w_down, moe_router, moe_w_gate, moe_w_up, moe_w_down, final_norm):
    b, n, d = x.shape
    depth = w_in.shape[0]
    cos_t, sin_t = rope_tables(n)
    cos_c, sin_c = cos_t[:, A_ROT_W:A_ROT_W + C_ROT_W], sin_t[:, A_ROT_W:A_ROT_W + C_ROT_W]
    silu_c = jax.nn.silu(c)
    silu_cc = jax.nn.silu(c_ctx)
    h = jnp.concatenate([ctx, x], axis=1)
    moe_wg, moe_wu, moe_wd = moe_w_gate.astype(BF16), moe_w_up.astype(BF16), moe_w_down.astype(BF16)
    for layer in range(depth):
        mod_x = jnp.dot(silu_c, w_mod[layer], precision=lax.Precision.HIGHEST) + b_mod[layer]
        mod_c = jnp.dot(silu_cc, w_mod[layer], precision=lax.Precision.HIGHEST) + b_mod[layer]
        mod = jnp.stack([jnp.broadcast_to(mod_c, mod_x.shape), mod_x], axis=1).reshape(b, 2, 6, d)
        mod = jnp.pad(mod, ((0, 0), (0, 0), (0, SUBLANES - 6), (0, 0)))
        pa, pb, pc, pd = norm_proj(h, mod, norm1[layer][None, :], build_in_weight(w_in[layer]), cos_t, sin_t)
        mix_a = swa_mixer(pa, swa_sink[layer])
        mix_b = gdn_mixer(pb, gdn_conv[layer], gdn_a_log[layer], gdn_dt_bias[layer], gdn_norm[layer])
        mq, mk, mv = mla_prep(pc, mla_q_norm[layer], mla_kv_norm[layer], mla_w_q_up[layer], mla_w_kv_up[layer],
                              cos_c, sin_c)
        mix_c = mla_attention(mq, mk, mv)
        mix_d = retention_mixer(pd, ret_log_decay[layer], ret_norm[layer])
        mixes = (mix_a, mix_b, mix_c, mix_d)
        w_o = build_out_weight(w_out[layer])
        i = layer // 2
        if layer % 2 == 0:
            h, v = out_proj(mixes, h, mod, norm2[layer][None, :], w_o)
            h = dense_ffn(v, h, mod, ffn_w_gate[i].astype(BF16), ffn_w_up[i].astype(BF16),
                          ffn_w_down[i].astype(BF16))
        else:
            w_r = jnp.pad(moe_router[i], ((0, 0), (0, LANES - N_EXPERTS)))
            h, v, logits = out_proj(mixes, h, mod, norm2[layer][None, :], w_o, w_r)
            h = moe_ffn(v, logits, h, mod, i, moe_wg, moe_wu, moe_wd)
    return final_rms_norm(h, final_norm[None, :], CTX_LEN // ROW_TILE)
```

```python
import functools

import numpy as np
import jax
import jax.numpy as jnp
from jax import lax
from jax.experimental import pallas as pl
from jax.experimental.pallas import tpu as pltpu

D_MODEL = 1024
GRID_W = 64
CTX_LEN = 256
HEAD_DIM = 64
ROPE_THETA = 10000.0
EPS = 1e-6
NEG_INF = -1e30

SWA_WINDOW = 128
GDN_HEADS = 4
GDN_DK = 64
GDN_DV = 64
GDN_CHUNK = 64
MLA_HEADS = 4
MLA_NOPE = 64
MLA_ROPE = 32
MLA_V = 64
RET_HEADS = 4
RET_DK = 64
D_FF = 3584
N_EXPERTS = 8
TOP_K = 2

LANES = 128
SUBLANES = 8
VMEM_LIMIT = 56 * 1024 * 1024

ROW_TILE = 256
FF_CHUNK = 512
MOE_TILE = 512
MOE_FF_CHUNK = 1792

A_W, B_W, C_W, D_W = 768, 1152, 512, 1024
A_ROT_W, C_ROT_W, D_ROT_W = 640, 128, 512
OFF_A = 0
OFF_B = OFF_A + A_W
OFF_C = OFF_B + B_W
OFF_D = OFF_C + C_W
OFF_AR = OFF_D + D_W
OFF_CR = OFF_AR + A_ROT_W
OFF_DR = OFF_CR + C_ROT_W
W_ALL = OFF_DR + D_ROT_W
ROPE_W = A_ROT_W + C_ROT_W + D_ROT_W

F32 = jnp.float32
BF16 = jnp.bfloat16
NT_DIMS = (((1,), (1,)), ((), ()))
TN_DIMS = (((0,), (0,)), ((), ()))


def _rms(x):
    return x * lax.rsqrt(jnp.mean(x * x, axis=-1, keepdims=True) + EPS)


def _silu(x):
    return x * (1.0 / (1.0 + jnp.exp(-x)))


def _dot(a, b):
    return jnp.dot(a, b, preferred_element_type=F32)


def _dot_nt(a, b):
    return lax.dot_general(a, b, NT_DIMS, preferred_element_type=F32)


def _dot_tn(a, b):
    return lax.dot_general(a, b, TN_DIMS, preferred_element_type=F32)


def _mod_spec(d):
    return pl.BlockSpec((None, None, SUBLANES, d), lambda i, t: (i, jnp.minimum(t, 1), 0, 0))


def _norm_proj_kernel(h_ref, mod_ref, g_ref, w_ref, cos_ref, sin_ref, a_ref, b_ref, c_ref, d_ref):
    x = h_ref[...]
    u = (_rms(x) * g_ref[...] * (1.0 + mod_ref[1:2, :]) + mod_ref[0:1, :]).astype(BF16)

    def mm(lo, width):
        return _dot(u, w_ref[:, lo:lo + width])

    a_main = mm(OFF_A, A_W)
    a_rot = mm(OFF_AR, A_ROT_W)
    a_ref[:, :A_ROT_W] = (a_main[:, :A_ROT_W] * cos_ref[:, :A_ROT_W] + a_rot * sin_ref[:, :A_ROT_W]).astype(BF16)
    a_ref[:, A_ROT_W:] = a_main[:, A_ROT_W:].astype(BF16)
    b_ref[...] = mm(OFF_B, B_W)
    c_main = mm(OFF_C, C_W)
    c_rot = mm(OFF_CR, C_ROT_W)
    lo, hi = A_ROT_W, A_ROT_W + C_ROT_W
    c_ref[:, :C_W - C_ROT_W] = c_main[:, :C_W - C_ROT_W]
    c_ref[:, C_W - C_ROT_W:] = c_main[:, C_W - C_ROT_W:] * cos_ref[:, lo:hi] + c_rot * sin_ref[:, lo:hi]
    d_main = mm(OFF_D, D_W)
    d_rot = mm(OFF_DR, D_ROT_W)
    d_ref[:, :D_ROT_W] = d_main[:, :D_ROT_W] * cos_ref[:, hi:] + d_rot * sin_ref[:, hi:]
    d_ref[:, D_ROT_W:] = d_main[:, D_ROT_W:]


def norm_proj(h, mod, gain, w, cos_t, sin_t):
    b, l, d = h.shape
    tm = ROW_TILE
    row = lambda i, t: (i, t, 0)
    return pl.pallas_call(
        _norm_proj_kernel,
        grid=(b, l // tm),
        in_specs=[
            pl.BlockSpec((None, tm, d), row),
            _mod_spec(d),
            pl.BlockSpec((1, d), lambda i, t: (0, 0)),
            pl.BlockSpec((d, W_ALL), lambda i, t: (0, 0), pipeline_mode=pl.Buffered(1)),
            pl.BlockSpec((tm, ROPE_W), lambda i, t: (t, 0)),
            pl.BlockSpec((tm, ROPE_W), lambda i, t: (t, 0)),
        ],
        out_specs=[pl.BlockSpec((None, tm, A_W), row), pl.BlockSpec((None, tm, B_W), row),
                   pl.BlockSpec((None, tm, C_W), row), pl.BlockSpec((None, tm, D_W), row)],
        out_shape=[jax.ShapeDtypeStruct((b, l, A_W), BF16), jax.ShapeDtypeStruct((b, l, B_W), F32),
                   jax.ShapeDtypeStruct((b, l, C_W), F32), jax.ShapeDtypeStruct((b, l, D_W), F32)],
        compiler_params=pltpu.CompilerParams(
            dimension_semantics=("parallel", "parallel"), vmem_limit_bytes=VMEM_LIMIT),
        name="norm_proj",
    )(h, mod, gain, w, cos_t, sin_t)


def _rot_cols(w, hd):
    x = w.reshape(w.shape[:-1] + (w.shape[-1] // hd, 4, hd // 4))
    x1, x2, x3, x4 = x[..., 0, :], x[..., 1, :], x[..., 2, :], x[..., 3, :]
    return jnp.stack([-x2, x1, -x4, x3], axis=-2).reshape(w.shape)


def _place_swa_q(q):
    z = jnp.zeros((q.shape[0], HEAD_DIM), q.dtype)
    blocks = []
    for h in range(4):
        qh = q[:, HEAD_DIM * h:HEAD_DIM * (h + 1)]
        blocks += [qh, z] if h // 2 == 0 else [z, qh]
    return jnp.concatenate(blocks, axis=1)


def build_in_weight(w):
    d = w.shape[0]
    o = [int(v) for v in np.cumsum((256, 128, 128, 768, 256, 16, 256, 128, 32, 256, 256, 256, 256))]
    aq, ak, av = w[:, :o[0]] * HEAD_DIM ** -0.5, w[:, o[0]:o[1]], w[:, o[1]:o[2]]
    b_main, b_ab = w[:, o[2]:o[4]], w[:, o[4]:o[5]]
    c_q, c_kv, c_kr = w[:, o[5]:o[6]], w[:, o[6]:o[7]], w[:, o[7]:o[8]]
    dq, dk, dvg = w[:, o[8]:o[9]], w[:, o[9]:o[10]] * RET_DK ** -0.5, w[:, o[10]:]
    z = lambda n: jnp.zeros((d, n), w.dtype)
    parts = [
        _place_swa_q(aq), ak, av,
        b_main, b_ab, z(LANES - b_ab.shape[1]),
        c_q, c_kv, z(64), c_kr, z(32),
        dq, dk, dvg,
        _place_swa_q(_rot_cols(aq, HEAD_DIM)), _rot_cols(ak, HEAD_DIM),
        z(64), _rot_cols(c_kr, MLA_ROPE), z(32),
        _rot_cols(dq, HEAD_DIM), _rot_cols(dk, HEAD_DIM),
    ]
    out = jnp.concatenate(parts, axis=1)
    assert out.shape[1] == W_ALL
    return out.astype(BF16)


def rope_tables(n):
    def axial(rot_dim):
        n_freq = rot_dim // 4
        inv_freq = ROPE_THETA ** (-jnp.arange(n_freq, dtype=F32) / n_freq)
        row = jnp.repeat(jnp.arange(n // GRID_W, dtype=F32), GRID_W)
        col = jnp.tile(jnp.arange(GRID_W, dtype=F32), n // GRID_W)
        ang_r = row[:, None] * inv_freq
        ang_c = col[:, None] * inv_freq
        ang = jnp.concatenate([ang_r, ang_r, ang_c, ang_c], axis=-1)
        return jnp.cos(ang), jnp.sin(ang)

    cos_h, sin_h = axial(HEAD_DIM)
    cos_r, sin_r = axial(MLA_ROPE)
    one, zero = jnp.ones((n, 1), F32), jnp.zeros((n, 1), F32)
    cos_c = jnp.concatenate([jnp.tile(one, (1, 64)), cos_r, jnp.tile(one, (1, 32))], axis=1)
    sin_c = jnp.concatenate([jnp.tile(zero, (1, 64)), sin_r, jnp.tile(zero, (1, 32))], axis=1)
    cos_t = jnp.concatenate([jnp.tile(cos_h, (1, A_ROT_W // HEAD_DIM)), cos_c,
                             jnp.tile(cos_h, (1, D_ROT_W // HEAD_DIM))], axis=1)
    sin_t = jnp.concatenate([jnp.tile(sin_h, (1, A_ROT_W // HEAD_DIM)), sin_c,
                             jnp.tile(sin_h, (1, D_ROT_W // HEAD_DIM))], axis=1)
    cos_t = jnp.concatenate([jnp.ones((CTX_LEN, ROPE_W), F32), cos_t], axis=0)
    sin_t = jnp.concatenate([jnp.zeros((CTX_LEN, ROPE_W), F32), sin_t], axis=0)
    return cos_t, sin_t


def _swa_kernel(sink_ref, q_ref, kp_ref, ko_ref, kn_ref, kc_ref, vp_ref, vo_ref, vn_ref, vc_ref, o_ref):
    t = pl.program_id(1)
    last = pl.num_programs(1) - 1
    tq = q_ref.shape[0]
    half = tq // 2

    def head_out(h, pieces):
        q = q_ref[:, LANES * h:LANES * (h + 1)]
        ss = []
        for k_ref, _, mask in pieces:
            s = _dot_nt(q, k_ref[...])
            ss.append(s if mask is None else jnp.where(mask, s, NEG_INF))
        sink = sink_ref[h]
        m = jnp.maximum(functools.reduce(jnp.maximum, [s.max(axis=-1, keepdims=True) for s in ss]), sink)
        ps = [jnp.exp(s - m) for s in ss]
        denom = functools.reduce(jnp.add, [p.sum(axis=-1, keepdims=True) for p in ps]) + jnp.exp(sink - m)
        o = functools.reduce(jnp.add, [_dot(p.astype(BF16), piece[1][...]) for p, piece in zip(ps, pieces)])
        return o / denom

    def write(pieces):
        outs = [head_out(h, pieces) for h in range(4)]
        lane = lax.broadcasted_iota(jnp.int32, (tq, LANES), 1)
        for r in range(2):
            o_ref[:, LANES * r:LANES * (r + 1)] = jnp.where(lane < HEAD_DIM, outs[r], outs[2 + r]).astype(o_ref.dtype)

    @pl.when(t == 0)
    def _():
        write([(kc_ref, vc_ref, None)])

    @pl.when(t > 0)
    def _():
        qi = lax.broadcasted_iota(jnp.int32, (tq, half), 0)
        kj = lax.broadcasted_iota(jnp.int32, (tq, half), 1)
        mask_prev = (kj >= qi) & (t > 1)
        mask_next = (kj <= qi - half) & (t < last)
        qo = lax.broadcasted_iota(jnp.int32, (tq, tq), 0)
        ko = lax.broadcasted_iota(jnp.int32, (tq, tq), 1)
        mask_own = jnp.abs(qo - ko) <= SWA_WINDOW
        write([(kp_ref, vp_ref, mask_prev), (ko_ref, vo_ref, mask_own), (kn_ref, vn_ref, mask_next),
               (kc_ref, vc_ref, None)])


def swa_mixer(pa, sink):
    b, l, _ = pa.shape
    tq = ROW_TILE
    nblk = l // SWA_WINDOW
    kcol, vcol = 4, 5
    prev = lambda c: (lambda i, t: (i, jnp.maximum(2 * t - 1, 2), c))
    nxt = lambda c: (lambda i, t: (i, jnp.minimum(2 * t + 2, nblk - 1), c))
    own = lambda c: (lambda i, t: (i, t, c))
    ctx = lambda c: (lambda i, t: (i, 0, c))
    kv_specs = lambda c: [pl.BlockSpec((None, SWA_WINDOW, LANES), prev(c)), pl.BlockSpec((None, tq, LANES), own(c)),
                          pl.BlockSpec((None, SWA_WINDOW, LANES), nxt(c)), pl.BlockSpec((None, tq, LANES), ctx(c))]
    return pl.pallas_call(
        _swa_kernel,
        grid=(b, l // tq),
        in_specs=[pl.BlockSpec(memory_space=pltpu.SMEM),
                  pl.BlockSpec((None, tq, 4 * LANES), lambda i, t: (i, t, 0))] + kv_specs(kcol) + kv_specs(vcol),
        out_specs=pl.BlockSpec((None, tq, 2 * LANES), lambda i, t: (i, t, 0)),
        out_shape=jax.ShapeDtypeStruct((b, l, 2 * LANES), BF16),
        compiler_params=pltpu.CompilerParams(
            dimension_semantics=("parallel", "parallel"), vmem_limit_bytes=VMEM_LIMIT),
        name="swa",
    )(sink, pa, pa, pa, pa, pa, pa, pa, pa, pa)


def _mla_prep_kernel(c_ref, qn_ref, kvn_ref, wq_ref, wqr_ref, wk_ref, wv_ref, cos_ref, sin_ref,
                     q_ref, k_ref, v_ref):
    cq = c_ref[:, 0:256]
    ckv = c_ref[:, 256:384]
    kr = c_ref[:, 384:512]
    nq = (_rms(cq) * qn_ref[...]).astype(BF16)
    nkv = (_rms(ckv) * kvn_ref[...]).astype(BF16)
    cos = jnp.concatenate([cos_ref[...]] * MLA_HEADS, axis=1)
    sin = jnp.concatenate([sin_ref[...]] * MLA_HEADS, axis=1)
    q_ref[...] = (_dot(nq, wq_ref[...]) * cos + _dot(nq, wqr_ref[...]) * sin).astype(BF16)
    k_ref[...] = (_dot(nkv, wk_ref[...]) + jnp.concatenate([kr] * MLA_HEADS, axis=1)).astype(BF16)
    v_ref[...] = _dot(nkv, wv_ref[...]).astype(BF16)


def mla_prep(pc, q_norm, kv_norm, w_q_up, w_kv_up, cos_c, sin_c):
    b, l, _ = pc.shape
    tm = ROW_TILE
    scale = (MLA_NOPE + MLA_ROPE) ** -0.5 * float(np.log2(np.e))
    wq = (w_q_up * scale).reshape(-1, MLA_HEADS, MLA_NOPE + MLA_ROPE)
    zq = jnp.zeros(wq.shape[:2] + (LANES - MLA_NOPE - MLA_ROPE,), F32)
    wq_main = jnp.concatenate([wq, zq], axis=-1).reshape(-1, MLA_HEADS * LANES)
    wq_rot = jnp.concatenate([jnp.zeros_like(wq[..., :MLA_NOPE]), _rot_cols(wq[..., MLA_NOPE:], MLA_ROPE), zq],
                             axis=-1).reshape(-1, MLA_HEADS * LANES)
    wkv = w_kv_up.reshape(-1, MLA_HEADS, MLA_NOPE + MLA_V)
    wk = jnp.concatenate([wkv[..., :MLA_NOPE], jnp.zeros_like(wkv[..., :LANES - MLA_NOPE])],
                         axis=-1).reshape(-1, MLA_HEADS * LANES)
    wv = wkv[..., MLA_NOPE:].reshape(-1, MLA_HEADS * MLA_V)
    row = lambda i, t: (i, t, 0)
    const = lambda i, t: (0, 0)
    full = lambda a: pl.BlockSpec(a.shape, const)
    args = [q_norm[None, :], kv_norm[None, :], wq_main.astype(BF16), wq_rot.astype(BF16), wk.astype(BF16),
            wv.astype(BF16)]
    return pl.pallas_call(
        _mla_prep_kernel,
        grid=(b, l // tm),
        in_specs=[pl.BlockSpec((None, tm, C_W), row)] + [full(a) for a in args]
        + [pl.BlockSpec((tm, LANES), lambda i, t: (t, 0)), pl.BlockSpec((tm, LANES), lambda i, t: (t, 0))],
        out_specs=[pl.BlockSpec((None, tm, 4 * LANES), row), pl.BlockSpec((None, tm, 4 * LANES), row),
                   pl.BlockSpec((None, tm, 2 * LANES), row)],
        out_shape=[jax.ShapeDtypeStruct((b, l, 4 * LANES), BF16), jax.ShapeDtypeStruct((b, l, 4 * LANES), BF16),
                   jax.ShapeDtypeStruct((b, l, 2 * LANES), BF16)],
        compiler_params=pltpu.CompilerParams(dimension_semantics=("parallel", "parallel")),
        name="mla_prep",
    )(pc, *args, cos_c, sin_c)


def _mla_attn_kernel(q_ref, k_ref, v_ref, o_ref):
    t = pl.program_id(2)
    tq = q_ref.shape[0]

    def attend(nk):
        v = v_ref[0:nk, :]
        v_lane = lax.broadcasted_iota(jnp.int32, v.shape, 1)
        outs = []
        for j in range(2):
            q = q_ref[:, LANES * j:LANES * (j + 1)]
            k = k_ref[0:nk, LANES * j:LANES * (j + 1)]
            s = _dot_nt(q, k)
            p = jnp.exp2(s - s.max(axis=-1, keepdims=True)).astype(BF16)
            other = (v_lane >= MLA_V) if j == 0 else (v_lane < MLA_V)
            o = _dot(p, jnp.where(other, jnp.ones_like(v), v))
            den_lane = MLA_V if j == 0 else 0
            outs.append(o / o[:, den_lane:den_lane + 1])
        lane = lax.broadcasted_iota(jnp.int32, (tq, LANES), 1)
        o_ref[...] = jnp.where(lane < MLA_V, outs[0], outs[1]).astype(o_ref.dtype)

    @pl.when(t == 0)
    def _():
        attend(CTX_LEN)

    @pl.when(t > 0)
    def _():
        attend(k_ref.shape[0])


def mla_attention(q, k, v):
    b, l, _ = q.shape
    tq = ROW_TILE
    return pl.pallas_call(
        _mla_attn_kernel,
        grid=(b, 2, l // tq),
        in_specs=[pl.BlockSpec((None, tq, 2 * LANES), lambda i, p, t: (i, t, p)),
                  pl.BlockSpec((None, l, 2 * LANES), lambda i, p, t: (i, 0, p)),
                  pl.BlockSpec((None, l, LANES), lambda i, p, t: (i, 0, p))],
        out_specs=pl.BlockSpec((None, tq, LANES), lambda i, p, t: (i, t, p)),
        out_shape=jax.ShapeDtypeStruct((b, l, 2 * LANES), BF16),
        compiler_params=pltpu.CompilerParams(
            dimension_semantics=("parallel", "parallel", "parallel"), vmem_limit_bytes=VMEM_LIMIT),
        name="mla_attn",
    )(q, k, v)


def _head_mean(x, ones_bd):
    hi = x.astype(BF16)
    lo = (x - hi.astype(F32)).astype(BF16)
    return (_dot(hi, ones_bd) + _dot(lo, ones_bd)) * (1.0 / HEAD_DIM)


def _ret_kernel(x_ref, lg_ref, g_ref, o_ref, s_ref, dec_ref, part_ref):
    dr = pl.program_id(0)
    s = pl.program_id(1)
    ns = pl.num_programs(1)
    nb, c = x_ref.shape[0], x_ref.shape[1]
    w = RET_HEADS * HEAD_DIM
    chunk = jnp.where(s == 0, 0, jnp.where(dr == 0, s, ns - s))
    lg = lg_ref[...]
    fwd = dr == 0
    row_h = lax.broadcasted_iota(jnp.int32, (w, w), 0) // HEAD_DIM
    col_h = lax.broadcasted_iota(jnp.int32, (w, w), 1) // HEAD_DIM
    same_head = row_h == col_h

    @pl.when(s == 0)
    def _():
        s_ref[...] = jnp.zeros_like(s_ref)
        i = lax.broadcasted_iota(jnp.int32, (c, c), 0)
        j = lax.broadcasted_iota(jnp.int32, (c, c), 1)
        rel = jnp.where(fwd, i - j, j - i)
        relf = jnp.maximum(rel, 0).astype(F32)
        for h in range(RET_HEADS):
            lg_h = lg_ref[0:1, HEAD_DIM * h:HEAD_DIM * h + 1]
            dec_ref[h] = jnp.where(rel >= 0, jnp.exp(lg_h * relf), 0.0)

    pos = lax.broadcasted_iota(jnp.int32, (c, 1), 0).astype(F32)
    q_dec = jnp.exp(lg * jnp.where(fwd, pos + 1.0, c - pos))
    k_dec = jnp.exp(lg * jnp.where(fwd, c - 1.0 - pos, pos))
    lane_h = lax.broadcasted_iota(jnp.int32, (c, w), 1) // HEAD_DIM
    bs = range(nb)
    q = [x_ref[i, :, 0:w] for i in bs]
    kf = [x_ref[i, :, w:2 * w] for i in bs]
    v = [x_ref[i, :, 2 * w:3 * w].astype(BF16) for i in bs]
    kb = [x.astype(BF16) for x in kf]
    acc = [_dot((q[i] * q_dec).astype(BF16), s_ref[i].astype(BF16)) for i in bs]
    for h in range(RET_HEADS):
        qh = [jnp.where(lane_h == h, q[i], 0.0).astype(BF16) for i in bs]
        a = [(_dot_nt(qh[i], kb[i]) * dec_ref[h]).astype(BF16) for i in bs]
        acc = [acc[i] + jnp.where(lane_h == h, _dot(a[i], v[i]), 0.0) for i in bs]
    kv = [_dot_tn((kf[i] * k_dec).astype(BF16), v[i]) for i in bs]
    chunk_dec = jnp.exp(lg * float(c))
    for i in bs:
        s_ref[i] = s_ref[i] * chunk_dec + jnp.where(same_head, kv[i], 0.0)

    rows = pl.ds(pl.multiple_of(chunk * c, c), c)

    @pl.when(dr == 0)
    def _():
        for i in bs:
            part_ref[i, rows, :] = acc[i]

    @pl.when(dr == 1)
    def _():
        ones_bd = jnp.where(same_head, 1.0, 0.0).astype(BF16)
        for i in bs:
            o = part_ref[i, rows, :] + acc[i]
            mu = _head_mean(o, ones_bd)
            var = _head_mean(jnp.square(o - mu), ones_bd)
            y = (o - mu) * lax.rsqrt(var + EPS) * g_ref[...]
            o_ref[i] = (y * _silu(x_ref[i, :, 3 * w:4 * w])).astype(o_ref.dtype)


def retention_mixer(pd, log_decay, norm_g):
    b, l, _ = pd.shape
    c = ROW_TILE
    ns = l // c
    w = RET_HEADS * HEAD_DIM
    lg = jnp.repeat(-jnp.exp(log_decay.astype(F32)), HEAD_DIM, axis=-1)[:, None, :]

    def chunk_of(dr, s):
        return jnp.where(s == 0, 0, jnp.where(dr == 0, s, ns - s))

    return pl.pallas_call(
        _ret_kernel,
        grid=(2, ns),
        in_specs=[pl.BlockSpec((b, c, D_W), lambda dr, s: (0, chunk_of(dr, s), 0)),
                  pl.BlockSpec((None, 1, w), lambda dr, s: (dr, 0, 0)),
                  pl.BlockSpec((1, w), lambda dr, s: (0, 0))],
        out_specs=pl.BlockSpec((b, c, w), lambda dr, s: (0, jnp.where(dr == 0, 0, chunk_of(dr, s)), 0)),
        out_shape=jax.ShapeDtypeStruct((b, l, w), BF16),
        scratch_shapes=[pltpu.VMEM((b, w, w), F32), pltpu.VMEM((RET_HEADS, c, c), F32), pltpu.VMEM((b, l, w), F32)],
        compiler_params=pltpu.CompilerParams(
            dimension_semantics=("arbitrary", "arbitrary"), vmem_limit_bytes=VMEM_LIMIT),
        name="retention",
    )(pd, lg, norm_g[None, :])


def _out_proj_kernel(ma_ref, mb_ref, mc_ref, md_ref, h_ref, mod_ref, g_ref, w_ref, *rest, with_router):
    if with_router:
        wr_ref, hn_ref, v_ref, lg_ref = rest
    else:
        hn_ref, v_ref = rest
    gw = 2 * LANES
    mix = functools.reduce(jnp.add, [
        _dot(m_ref[...].astype(BF16), w_ref[gw * i:gw * (i + 1), :])
        for i, m_ref in enumerate((ma_ref, mb_ref, mc_ref, md_ref))])
    hn = h_ref[...] + mod_ref[2:3, :] * mix
    hn_ref[...] = hn
    v = _rms(hn) * g_ref[...] * (1.0 + mod_ref[4:5, :]) + mod_ref[3:4, :]
    v_ref[...] = v.astype(v_ref.dtype)
    if with_router:
        lg_ref[...] = jnp.dot(v, wr_ref[...], preferred_element_type=F32, precision=lax.Precision.HIGHEST)


def out_proj(mixes, h, mod, gain, w, w_router=None):
    b, l, d = h.shape
    tm = ROW_TILE
    with_router = w_router is not None
    row = lambda i, t: (i, t, 0)
    in_specs = [pl.BlockSpec((None, tm, 2 * LANES), row) for _ in mixes] + [
        pl.BlockSpec((None, tm, d), row),
        _mod_spec(d),
        pl.BlockSpec((1, d), lambda i, t: (0, 0)),
        pl.BlockSpec(w.shape, lambda i, t: (0, 0), pipeline_mode=pl.Buffered(1)),
    ]
    out_specs = [pl.BlockSpec((None, tm, d), row), pl.BlockSpec((None, tm, d), row)]
    out_shape = [jax.ShapeDtypeStruct((b, l, d), F32), jax.ShapeDtypeStruct((b, l, d), F32 if with_router else BF16)]
    args = list(mixes) + [h, mod, gain, w]
    if with_router:
        in_specs.append(pl.BlockSpec(w_router.shape, lambda i, t: (0, 0)))
        out_specs.append(pl.BlockSpec((None, tm, LANES), row))
        out_shape.append(jax.ShapeDtypeStruct((b, l, LANES), F32))
        args.append(w_router)
    return pl.pallas_call(
        functools.partial(_out_proj_kernel, with_router=with_router),
        grid=(b, l // tm),
        in_specs=in_specs,
        out_specs=out_specs,
        out_shape=out_shape,
        compiler_params=pltpu.CompilerParams(
            dimension_semantics=("parallel", "parallel"), vmem_limit_bytes=VMEM_LIMIT),
        name="out_proj",
    )(*args)


def build_out_weight(w):
    hd = HEAD_DIM
    perm = jnp.concatenate([w[0:hd], w[2 * hd:3 * hd], w[hd:2 * hd], w[3 * hd:4 * hd]], axis=0)
    return jnp.concatenate([perm, w[4 * hd:]], axis=0).astype(BF16)


def _ffn_kernel(v_ref, h_ref, mod_ref, wg_ref, wu_ref, wd_ref, o_ref):
    v = v_ref[...]
    acc = jnp.zeros(o_ref.shape, F32)
    for j in range(D_FF // FF_CHUNK):
        cols = slice(j * FF_CHUNK, (j + 1) * FF_CHUNK)
        a = _dot(v, wg_ref[:, cols])
        u = _dot(v, wu_ref[:, cols])
        mid = (_silu(a) * u).astype(BF16)
        acc = acc + _dot(mid, wd_ref[cols, :])
    o_ref[...] = h_ref[...] + mod_ref[5:6, :] * acc


def dense_ffn(v, h, mod, wg, wu, wd):
    b, l, d = h.shape
    tm = ROW_TILE
    row = lambda i, t: (i, t, 0)
    const = lambda i, t: (0, 0)
    return pl.pallas_call(
        _ffn_kernel,
        grid=(b, l // tm),
        in_specs=[
            pl.BlockSpec((None, tm, d), row),
            pl.BlockSpec((None, tm, d), row),
            _mod_spec(d),
            pl.BlockSpec(wg.shape, const, pipeline_mode=pl.Buffered(1)),
            pl.BlockSpec(wu.shape, const, pipeline_mode=pl.Buffered(1)),
            pl.BlockSpec(wd.shape, const, pipeline_mode=pl.Buffered(1)),
        ],
        out_specs=pl.BlockSpec((None, tm, d), row),
        out_shape=jax.ShapeDtypeStruct((b, l, d), F32),
        compiler_params=pltpu.CompilerParams(
            dimension_semantics=("parallel", "parallel"), vmem_limit_bytes=VMEM_LIMIT),
        name="dense_ffn",
    )(v, h, mod, wg, wu, wd)


def _moe_kernel(wt_ref, we_ref, lo_ref, hi_ref, first_ref, x_ref, wg_ref, wu_ref, wd_ref, o_ref, xm_ref, acc_ref):
    w = pl.program_id(0)
    j = pl.program_id(1)
    nj = pl.num_programs(1)
    tm = x_ref.shape[0]

    @pl.when(j == 0)
    def _():
        row = wt_ref[w] * tm + lax.broadcasted_iota(jnp.int32, (tm, 1), 0)
        keep = (row >= lo_ref[w]) & (row < hi_ref[w])
        xm_ref[...] = jnp.where(keep, x_ref[...], 0.0).astype(BF16)

    @pl.when((j == 0) & (first_ref[w] > 0))
    def _():
        acc_ref[...] = jnp.zeros_like(acc_ref)

    half = tm // 2
    for part in range(2):
        part_lo = wt_ref[w] * tm + part * half

        @pl.when((hi_ref[w] > part_lo) & (lo_ref[w] < part_lo + half) & (hi_ref[w] > lo_ref[w]))
        def _():
            rows = slice(part * half, (part + 1) * half)
            x = xm_ref[rows, :]
            a = _dot(x, wg_ref[...])
            u = _dot(x, wu_ref[...])
            mid = (_silu(a) * u).astype(BF16)
            acc_ref[rows, :] += _dot(mid, wd_ref[...])

    @pl.when(j == nj - 1)
    def _():
        o_ref[...] = acc_ref[...]


def moe_grouped_ffn(xs, items, layer_idx, wg, wu, wd):
    s, d = xs.shape
    tm = MOE_TILE
    fc = MOE_FF_CHUNK
    nw = items[0].shape[0]
    nj = D_FF // fc
    grid_spec = pltpu.PrefetchScalarGridSpec(
        num_scalar_prefetch=5,
        grid=(nw, nj),
        in_specs=[
            pl.BlockSpec((tm, d), lambda w, j, wt, we, lo, hi, fi: (wt[w], 0)),
            pl.BlockSpec((None, None, d, fc), lambda w, j, wt, we, lo, hi, fi: (layer_idx, we[w], 0, j)),
            pl.BlockSpec((None, None, d, fc), lambda w, j, wt, we, lo, hi, fi: (layer_idx, we[w], 0, j)),
            pl.BlockSpec((None, None, fc, d), lambda w, j, wt, we, lo, hi, fi: (layer_idx, we[w], j, 0)),
        ],
        out_specs=pl.BlockSpec((tm, d), lambda w, j, wt, we, lo, hi, fi: (wt[w], 0)),
        scratch_shapes=[pltpu.VMEM((tm, d), BF16), pltpu.VMEM((tm, d), F32)],
    )
    return pl.pallas_call(
        _moe_kernel,
        grid_spec=grid_spec,
        out_shape=jax.ShapeDtypeStruct((s, d), F32),
        compiler_params=pltpu.CompilerParams(
            dimension_semantics=("arbitrary", "arbitrary"), vmem_limit_bytes=VMEM_LIMIT),
        name="moe_ffn",
    )(*items, xs, wg, wu, wd)


def _residual_kernel(h_ref, f0_ref, f1_ref, gate_ref, mod_ref, o_ref):
    f = gate_ref[:, 0:1] * f0_ref[...] + gate_ref[:, 1:2] * f1_ref[...]
    o_ref[...] = h_ref[...] + mod_ref[5:6, :] * f


def gated_residual(h, f0, f1, gates, mod):
    b, l, d = h.shape
    tm = ROW_TILE
    row = lambda i, t: (i, t, 0)
    return pl.pallas_call(
        _residual_kernel,
        grid=(b, l // tm),
        in_specs=[pl.BlockSpec((None, tm, d), row), pl.BlockSpec((None, tm, d), row), pl.BlockSpec((None, tm, d), row),
                  pl.BlockSpec((None, tm, LANES), row), _mod_spec(d)],
        out_specs=pl.BlockSpec((None, tm, d), row),
        out_shape=jax.ShapeDtypeStruct((b, l, d), F32),
        compiler_params=pltpu.CompilerParams(dimension_semantics=("parallel", "parallel")),
        name="gated_residual",
    )(h, f0, f1, gates, mod)


def moe_ffn(v, logits, h, mod, layer_idx, wg, wu, wd):
    b, l, d = h.shape
    t = b * l
    s = TOP_K * t
    tm = MOE_TILE
    nt = s // tm
    nw = nt + N_EXPERTS - 1
    i32 = jnp.int32
    lg = logits.reshape(t, LANES)[:, :N_EXPERTS]
    top_val, top_idx = lax.top_k(lg, TOP_K)
    gates = jax.nn.softmax(top_val, axis=-1)
    slot = jnp.arange(s, dtype=i32)
    skey = jnp.sort(top_idx.reshape(-1).astype(i32) * s + slot)
    order = skey % s
    _, inv = lax.sort_key_val(order, slot)
    bounds = (jnp.arange(N_EXPERTS, dtype=i32) + 1) * s
    cum = jnp.sum((skey[None, :] < bounds[:, None]).astype(i32), axis=1)
    cum_prev = jnp.concatenate([jnp.zeros((1,), i32), cum[:-1]])
    tile_lo = jnp.arange(nt, dtype=i32) * tm
    count_le = lambda edges, x: jnp.sum((edges[None, :] <= x[:, None]).astype(i32), axis=1)
    e_first = count_le(cum, tile_lo)
    e_last = count_le(cum, tile_lo + tm - 1)
    n_items = e_last - e_first + 1
    item_end = jnp.cumsum(n_items)
    item_start = item_end - n_items
    w = jnp.arange(nw, dtype=i32)
    wt = jnp.minimum(count_le(item_end, w), nt - 1)
    valid = w < item_end[-1]
    we = jnp.clip(e_first[wt] + w - item_start[wt], 0, N_EXPERTS - 1).astype(i32)
    lo = jnp.where(valid, cum_prev[we], 0).astype(i32)
    hi = jnp.where(valid, cum[we], 0).astype(i32)
    first = (valid & (w == item_start[wt])).astype(i32)
    rows_of = lambda a, idx: a.at[idx].get(mode="promise_in_bounds")
    xs = rows_of(v.reshape(t, d), order // TOP_K)
    ys = moe_grouped_ffn(xs, (wt, we, lo, hi, first), layer_idx, wg, wu, wd)
    dest = inv.reshape(t, TOP_K)
    f0 = rows_of(ys, dest[:, 0]).reshape(b, l, d)
    f1 = rows_of(ys, dest[:, 1]).reshape(b, l, d)
    gates_p = jnp.pad(gates, ((0, 0), (0, LANES - TOP_K))).reshape(b, l, LANES)
    return gated_residual(h, f0, f1, gates_p, mod)


def _final_norm_kernel(h_ref, g_ref, o_ref):
    o_ref[...] = _rms(h_ref[...]) * g_ref[...]


def final_rms_norm(h, gain, n_ctx_tiles):
    b, l, d = h.shape
    tm = ROW_TILE
    n = l - n_ctx_tiles * tm
    return pl.pallas_call(
        _final_norm_kernel,
        grid=(b, n // tm),
        in_specs=[
            pl.BlockSpec((None, tm, d), lambda i, t: (i, t + n_ctx_tiles, 0)),
            pl.BlockSpec((1, d), lambda i, t: (0, 0)),
        ],
        out_specs=pl.BlockSpec((None, tm, d), lambda i, t: (i, t, 0)),
        out_shape=jax.ShapeDtypeStruct((b, n, d), F32),
        compiler_params=pltpu.CompilerParams(dimension_semantics=("parallel", "parallel")),
        name="final_norm",
    )(h, gain)


GDN_W = GDN_HEADS * GDN_DK
GDN_CONV_K = 5
GDN_HALO = SUBLANES


def _split3(x):
    p0 = x.astype(BF16)
    r1 = x - p0.astype(F32)
    p1 = r1.astype(BF16)
    p2 = (r1 - p1.astype(F32)).astype(BF16)
    return p0, p1, p2


def _gdn_prep_kernel(x_ref, prev_ref, next_ref, cw_ref, par_ref, q_ref, k_ref, v_ref, gb_ref):
    t = pl.program_id(1)
    last = pl.num_programs(1) - 1
    tm = x_ref.shape[0]
    w3 = 3 * GDN_W
    has_prev = t > 1
    has_next = (t > 0) & (t < last)
    prev = jnp.where(has_prev, prev_ref[...], 0.0)
    nxt = jnp.where(has_next, next_ref[...], 0.0)
    xe = jnp.concatenate([prev, x_ref[:, :w3], nxt], axis=0)
    y = jnp.zeros((tm, w3), F32)
    for j in range(GDN_CONV_K):
        lo = GDN_HALO - GDN_CONV_K // 2 + j
        y = y + cw_ref[j:j + 1, :] * xe[lo:lo + tm, :]
    y = _silu(y)
    r = lax.broadcasted_iota(jnp.int32, (GDN_W, GDN_W), 0)
    c = lax.broadcasted_iota(jnp.int32, (GDN_W, GDN_W), 1)
    ones_bd = jnp.where(r // GDN_DK == c // GDN_DK, 1.0, 0.0).astype(BF16)

    def l2n(x):
        sq = x * x
        hi = sq.astype(BF16)
        lo = (sq - hi.astype(F32)).astype(BF16)
        return x * lax.rsqrt(_dot(hi, ones_bd) + _dot(lo, ones_bd) + EPS)

    q_ref[...] = l2n(y[:, :GDN_W]) * GDN_DK ** -0.5
    k_ref[...] = l2n(y[:, GDN_W:2 * GDN_W])
    v_ref[...] = y[:, 2 * GDN_W:]
    ab = x_ref[:, w3 + GDN_W:]
    lane = lax.broadcasted_iota(jnp.int32, ab.shape, 1)
    is_g = (lane % 8) < 4
    z = ab + par_ref[1:2, :]
    softplus = jnp.maximum(z, 0.0) + jnp.log1p(jnp.exp(-jnp.abs(z)))
    g = jnp.where(is_g, par_ref[0:1, :] * softplus, 0.0)
    beta = 1.0 / (1.0 + jnp.exp(-ab))
    i = lax.broadcasted_iota(jnp.int32, (tm, tm), 0)
    j = lax.broadcasted_iota(jnp.int32, (tm, tm), 1)
    same_chunk = i // GDN_CHUNK == j // GDN_CHUNK
    tri_f = jnp.where(same_chunk & (j <= i), 1.0, 0.0).astype(BF16)
    tri_b = jnp.where(same_chunk & (j >= i), 1.0, 0.0).astype(BF16)
    pieces = _split3(g)
    gc_f = functools.reduce(jnp.add, [_dot(tri_f, p) for p in pieces])
    gc_b = functools.reduce(jnp.add, [_dot(tri_b, p) for p in pieces])
    gb_ref[...] = jnp.where(is_g, jnp.where(lane < 8, gc_f, gc_b), beta)


def gdn_prep(pb, conv_w, a_log, dt_bias):
    b, l, _ = pb.shape
    tm = ROW_TILE
    w3 = 3 * GDN_W
    halo_blocks = tm // GDN_HALO
    n_halo = l // GDN_HALO
    cw = jnp.pad(conv_w, ((0, SUBLANES - GDN_CONV_K), (0, 0)))
    neg_a = jnp.pad(-jnp.exp(a_log.astype(F32)), ((0, 0), (0, 4))).reshape(-1)
    dtb = jnp.pad(dt_bias.astype(F32), ((0, 0), (0, 4))).reshape(-1)
    par = jnp.pad(jnp.stack([neg_a, dtb]), ((0, SUBLANES - 2), (0, LANES - 16)))
    row = lambda i, t: (i, t, 0)
    out = lambda w: pl.BlockSpec((None, tm, w), row)
    return pl.pallas_call(
        _gdn_prep_kernel,
        grid=(b, l // tm),
        in_specs=[pl.BlockSpec((None, tm, B_W), row),
                  pl.BlockSpec((None, GDN_HALO, w3), lambda i, t: (i, jnp.maximum(t * halo_blocks - 1, 0), 0)),
                  pl.BlockSpec((None, GDN_HALO, w3),
                               lambda i, t: (i, jnp.minimum((t + 1) * halo_blocks, n_halo - 1), 0)),
                  pl.BlockSpec(cw.shape, lambda i, t: (0, 0)),
                  pl.BlockSpec(par.shape, lambda i, t: (0, 0))],
        out_specs=[out(GDN_W), out(GDN_W), out(GDN_W), out(LANES)],
        out_shape=[jax.ShapeDtypeStruct((b, l, GDN_W), F32)] * 3 + [jax.ShapeDtypeStruct((b, l, LANES), F32)],
        compiler_params=pltpu.CompilerParams(
            dimension_semantics=("parallel", "parallel"), vmem_limit_bytes=VMEM_LIMIT),
        name="gdn_prep",
    )(pb, pb, pb, cw, par)


def _tile_heads(x):
    return jnp.concatenate([x] * GDN_HEADS, axis=0)


def _collapse_heads(x):
    c = GDN_CHUNK
    return x[0:c] + x[c:2 * c] + x[2 * c:3 * c] + x[3 * c:4 * c]


def _gdn_chunk_kernel(q_ref, k_ref, v_ref, gb_ref, *out_refs):
    dr = pl.program_id(1)

    @pl.when(dr == 0)
    def _():
        _gdn_chunk_body(True, q_ref, k_ref, v_ref, gb_ref, *out_refs)

    @pl.when(dr == 1)
    def _():
        _gdn_chunk_body(False, q_ref, k_ref, v_ref, gb_ref, *out_refs)


def _gdn_chunk_body(fwd, q_ref, k_ref, v_ref, gb_ref, o0_ref, qe_ref, a_ref, bm_ref, gam_ref):
    n = GDN_W
    cs = GDN_CHUNK
    r = lax.broadcasted_iota(jnp.int32, (n, n), 0)
    c = lax.broadcasted_iota(jnp.int32, (n, n), 1)
    ri, ci = r % cs, c % cs
    head = r // cs == c // cs
    ahead = ri - ci if fwd else ci - ri
    tri = head & (ahead >= 0)
    tri_strict = head & (ahead > 0)
    eye = jnp.where(r == c, 1.0, 0.0)
    blk = lambda s: r // s == c // s
    b8, b16, b32 = blk(8), blk(16), blk(32)
    lane = lax.broadcasted_iota(jnp.int32, (n, LANES), 1)
    row_head = lax.broadcasted_iota(jnp.int32, (n, LANES), 0) // cs
    lane0 = 0 if fwd else 8
    sel_g = lane == lane0 + row_head
    sel_b = lane == lane0 + 4 + row_head
    pick = lambda sel, x: jnp.sum(jnp.where(sel, x, 0.0), axis=1, keepdims=True)

    rows = [slice(ch * cs, (ch + 1) * cs) for ch in range(q_ref.shape[0] // cs)]
    each = lambda f, *xs: [f(*a) for a in zip(*xs)]
    bf = lambda xs: [x.astype(BF16) for x in xs]
    spread = lambda ref: [jnp.where(head, _tile_heads(ref[rw, :]), 0.0) for rw in rows]
    kh, qh, vh = spread(k_ref), spread(q_ref), spread(v_ref)
    gb = [gb_ref[rw, :] for rw in rows]
    gb4 = [_tile_heads(x) for x in gb]
    gc = [pick(sel_g, x) for x in gb4]
    beta = [pick(sel_b, x) for x in gb4]
    end = cs - 1 if fwd else 0
    gl = [pick(sel_g, jnp.broadcast_to(x[end:end + 1, :], (n, LANES))) for x in gb]
    gc_b = [jnp.broadcast_to(x, (n, n)) for x in gc]
    decay = [jnp.exp(jnp.minimum(x - x.T, 0.0)) for x in gc_b]
    khb, qhb = bf(kh), bf(qh)
    kk = each(_dot_nt, khb, khb)
    qk = each(_dot_nt, qhb, khb)
    lmat = each(lambda b_, kk_, d_: jnp.where(tri_strict, b_ * kk_ * d_, 0.0), beta, kk, decay)
    attn = bf(each(lambda qk_, d_: jnp.where(tri, qk_ * d_, 0.0), qk, decay))
    nl = bf([jnp.where(b8, -x, 0.0) for x in lmat])
    n2 = bf(each(_dot, nl, nl))
    n4 = each(_dot, n2, n2)
    p1 = bf(each(lambda a, b_: _dot((eye + a).astype(BF16), (eye + b_).astype(BF16)), nl, n2))
    tinv = each(lambda p, x: _dot(p, (eye + x).astype(BF16)), p1, n4)
    def moving_rows(x, sz):
        return jnp.concatenate([x[i:i + sz] for i in range(sz if fwd else 0, n, 2 * sz)], axis=0)

    def with_moving_rows(x, new, sz):
        pieces = []
        for j, i in enumerate(range(0, n, 2 * sz)):
            kept = x[i:i + sz] if fwd else x[i + sz:i + 2 * sz]
            moved = new[j * sz:(j + 1) * sz]
            pieces += [kept, moved] if fwd else [moved, kept]
        return jnp.concatenate(pieces, axis=0)

    for sz, inner, outer in ((8, b8, b16), (16, b16, b32), (32, b32, head)):
        off = bf([jnp.where(outer & ~inner, x, 0.0) for x in lmat])
        tb = bf(tinv)
        t_mv = [moving_rows(x, sz) for x in tinv]
        to = bf(each(_dot, bf(t_mv), off))
        tinv = each(lambda t_, tm_, to_, tb_: with_moving_rows(t_, tm_ - _dot(to_, tb_), sz), tinv, t_mv, to, tb)
    tb = bf(tinv)
    eg = [jnp.exp(x) for x in gc]
    u = bf(each(lambda t_, b_, v_: _dot(t_, (b_ * v_).astype(BF16)), tb, beta, vh))
    w = bf(each(lambda t_, b_, e_, k_: _dot(t_, ((b_ * e_) * k_).astype(BF16)), tb, beta, eg, kh))
    o0 = each(_dot, attn, u)
    qe = each(lambda q_, e_, a_, w_: q_ * e_ - _dot(a_, w_), qh, eg, attn, w)
    kg = bf(each(lambda k_, gl_, gc_: k_ * jnp.exp(gl_ - gc_), kh, gl, gc))
    a_mat = each(_dot_tn, kg, w)
    b_mat = each(_dot_tn, kg, u)
    for rw, o0_, qe_, a_, b_, gl_ in zip(rows, o0, qe, a_mat, b_mat, gl):
        o0_ref[rw, :] = _collapse_heads(o0_)
        qe_ref[rw, :] = _collapse_heads(qe_)
        a_ref[rw, :] = _collapse_heads(a_)
        bm_ref[rw, :] = _collapse_heads(b_)
        gam_ref[rw, :] = _collapse_heads(jnp.where(head, jnp.broadcast_to(jnp.exp(gl_), (n, n)), 0.0))


def gdn_chunks(q, k, v, gb):
    b, l, _ = q.shape
    tm = ROW_TILE
    row = lambda i, dr, t: (i, t, 0)
    out = pl.BlockSpec((None, None, tm, GDN_W), lambda i, dr, t: (dr, i, t, 0))
    return pl.pallas_call(
        _gdn_chunk_kernel,
        grid=(b, 2, l // tm),
        in_specs=[pl.BlockSpec((None, tm, GDN_W), row)] * 3 + [pl.BlockSpec((None, tm, LANES), row)],
        out_specs=[out] * 5,
        out_shape=[jax.ShapeDtypeStruct((2, b, l, GDN_W), F32)] * 5,
        compiler_params=pltpu.CompilerParams(
            dimension_semantics=("parallel", "parallel", "parallel"), vmem_limit_bytes=VMEM_LIMIT),
        name="gdn_chunk",
    )(q, k, v, gb)


def _gdn_scan_kernel(o0_ref, qe_ref, a_ref, bm_ref, gam_ref, gate_ref, g_ref, o_ref, s_ref, part_ref):
    dr = pl.program_id(0)
    s = pl.program_id(1)
    ns = pl.num_programs(1)
    nb, tm = o0_ref.shape[0], o0_ref.shape[1]
    cs = GDN_CHUNK
    n = GDN_W
    nch = tm // cs
    bs = range(nb)
    tile = jnp.where(s == 0, 0, jnp.where(dr == 0, s, ns - s))
    r = lax.broadcasted_iota(jnp.int32, (n, n), 0)
    c = lax.broadcasted_iota(jnp.int32, (n, n), 1)
    head = r // cs == c // cs

    @pl.when(s == 0)
    def _():
        s_ref[...] = jnp.zeros_like(s_ref)

    def run(order):
        state = [s_ref[i] for i in bs]
        outs = [{} for _ in bs]
        for ch in order:
            rows = slice(ch * cs, (ch + 1) * cs)
            sb = [x.astype(BF16) for x in state]
            for i in bs:
                outs[i][ch] = o0_ref[i, rows, :] + _dot(qe_ref[i, rows, :].astype(BF16), sb[i])
            a_full = [jnp.where(head, _tile_heads(a_ref[i, rows, :]), 0.0).astype(BF16) for i in bs]
            state = [_tile_heads(gam_ref[i, rows, :]) * state[i] - _dot(a_full[i], sb[i])
                     + jnp.where(head, _tile_heads(bm_ref[i, rows, :]), 0.0) for i in bs]
        for i in bs:
            s_ref[i] = state[i]
        return [jnp.concatenate([outs[i][ch] for ch in range(nch)], axis=0) for i in bs]

    rows_out = pl.ds(pl.multiple_of(tile * tm, tm), tm)

    @pl.when(dr == 0)
    def _():
        for i, o in enumerate(run(range(nch))):
            part_ref[i, rows_out, :] = o

    @pl.when(dr == 1)
    def _():
        ones_bd = jnp.where(head, 1.0, 0.0).astype(BF16)
        for i, o_bwd in enumerate(run(range(nch - 1, -1, -1))):
            o = part_ref[i, rows_out, :] + o_bwd
            ms = _head_mean(o * o, ones_bd)
            o_ref[i] = (o * lax.rsqrt(ms + EPS) * g_ref[...] * _silu(gate_ref[i])).astype(o_ref.dtype)


def gdn_scan(o0, qe, a, bm, gam, pb, norm_g):
    _, b, l, _ = o0.shape
    tm = ROW_TILE
    ns = l // tm

    def tile_of(dr, s):
        return jnp.where(s == 0, 0, jnp.where(dr == 0, s, ns - s))

    per_dir = pl.BlockSpec((None, b, tm, GDN_W), lambda dr, s: (dr, 0, tile_of(dr, s), 0))
    gate_col = 3 * GDN_W // GDN_W
    return pl.pallas_call(
        _gdn_scan_kernel,
        grid=(2, ns),
        in_specs=[per_dir] * 5 + [
            pl.BlockSpec((b, tm, GDN_W), lambda dr, s: (0, tile_of(dr, s), gate_col)),
            pl.BlockSpec((1, GDN_W), lambda dr, s: (0, 0))],
        out_specs=pl.BlockSpec((b, tm, GDN_W), lambda dr, s: (0, jnp.where(dr == 0, 0, tile_of(dr, s)), 0)),
        out_shape=jax.ShapeDtypeStruct((b, l, GDN_W), BF16),
        scratch_shapes=[pltpu.VMEM((b, GDN_W, GDN_W), F32), pltpu.VMEM((b, l, GDN_W), F32)],
        compiler_params=pltpu.CompilerParams(
            dimension_semantics=("arbitrary", "arbitrary"), vmem_limit_bytes=VMEM_LIMIT),
        name="gdn_scan",
    )(o0, qe, a, bm, gam, pb, jnp.tile(norm_g, GDN_HEADS)[None, :])


def gdn_mixer(pb, conv_w, a_log, dt_bias, norm_g):
    q, k, v, gb = gdn_prep(pb, conv_w, a_log, dt_bias)
    o0, qe, a, bm, gam = gdn_chunks(q, k, v, gb)
    return gdn_scan(o0, qe, a, bm, gam, pb, norm_g)


def rms_norm(x, g):
    xf = x.astype(F32)
    y = xf * lax.rsqrt(jnp.mean(xf * xf, axis=-1, keepdims=True) + EPS)
    return (y * g.astype(F32)).astype(x.dtype)


def l2_normalize(t):
    return t * lax.rsqrt(jnp.sum(t * t, axis=-1, keepdims=True) + EPS)


def directional_scan(chunked_fn, seqs, consts, s0, reverse):
    if reverse:
        seqs = tuple(jnp.flip(t, axis=1) for t in seqs)
    o, s = chunked_fn(*seqs, *consts, s0)
    if reverse:
        o = jnp.flip(o, axis=1)
    return o, s


def prefix_bidirectional_scan(chunked_fn, seqs_c, seqs_x, consts, state_shape):
    out_c, out_x = 0.0, 0.0
    for d in range(2):
        rev = d == 1
        s0 = jnp.zeros(state_shape, F32)
        oc, s_ctx = directional_scan(chunked_fn, seqs_c[d], consts[d], s0, rev)
        ox, _ = directional_scan(chunked_fn, seqs_x[d], consts[d], s_ctx, rev)
        out_c = out_c + oc
        out_x = out_x + ox
    return out_c, out_x


def short_conv(x, w):
    k = w.shape[0]
    y = lax.conv_general_dilated(
        x, w[:, None, :].astype(x.dtype), window_strides=(1,), padding=[(k // 2, k // 2)],
        dimension_numbers=('NWC', 'WIO', 'NWC'), feature_group_count=x.shape[-1])
    return jax.nn.silu(y)


def gated_delta_chunked(q, k, v, g, beta, s0):
    b, l, h, _ = q.shape
    dv = v.shape[-1]
    cs = GDN_CHUNK
    nc = l // cs

    def chunks(t):
        return t.reshape(b, nc, cs, h, -1).transpose(1, 0, 3, 2, 4)

    qc, kc, vc = chunks(q), chunks(k), chunks(v)
    gc = jnp.cumsum(chunks(g[..., None])[..., 0], axis=-1)
    bc = chunks(beta[..., None])
    idx = jnp.arange(cs)
    lower = idx[:, None] >= idx[None, :]
    strict = idx[:, None] > idx[None, :]
    decay = jnp.exp(jnp.where(lower, gc[..., :, None] - gc[..., None, :], NEG_INF))
    kb = kc * bc
    lmat = jnp.where(strict, jnp.einsum('nbhid,nbhjd->nbhij', kb, kc) * decay, 0.0)
    a_mat = lmat + jnp.eye(cs, dtype=F32)
    u = lax.linalg.triangular_solve(a_mat, vc * bc, left_side=True, lower=True)
    w = lax.linalg.triangular_solve(a_mat, kb * jnp.exp(gc)[..., None], left_side=True, lower=True)
    attn = jnp.einsum('nbhid,nbhjd->nbhij', qc, kc) * decay

    def step(s, xs):
        q_i, k_i, u_i, w_i, a_i, g_i = xs
        v_new = u_i - jnp.einsum('bhck,bhkv->bhcv', w_i, s)
        o_i = (jnp.einsum('bhck,bhkv->bhcv', q_i * jnp.exp(g_i)[..., None], s)
               + jnp.einsum('bhij,bhjv->bhiv', a_i, v_new))
        g_last = g_i[..., -1:]
        s = (s * jnp.exp(g_last)[..., None]
             + jnp.einsum('bhck,bhcv->bhkv', k_i * jnp.exp(g_last - g_i)[..., None], v_new))
        return s, o_i

    s_fin, o = lax.scan(step, s0, (qc, kc, u, w, attn, gc))
    return o.transpose(1, 0, 3, 2, 4).reshape(b, l, h, dv), s_fin


def gdn_mixer_jax(pb, conv_w, a_log, dt_bias, norm_g):
    def prep(qkv, ab):
        b, l, _ = qkv.shape
        qkv = short_conv(qkv, conv_w).astype(F32)
        q, k, v = jnp.split(qkv, [GDN_HEADS * GDN_DK, 2 * GDN_HEADS * GDN_DK], axis=-1)
        q = l2_normalize(q.reshape(b, l, GDN_HEADS, GDN_DK)) * GDN_DK ** -0.5
        k = l2_normalize(k.reshape(b, l, GDN_HEADS, GDN_DK))
        v = v.reshape(b, l, GDN_HEADS, GDN_DV)
        ab = ab.astype(F32).reshape(b, l, 2, 2, GDN_HEADS)
        g = -jnp.exp(a_log.astype(F32)) * jax.nn.softplus(ab[:, :, :, 0] + dt_bias.astype(F32))
        beta = jax.nn.sigmoid(ab[:, :, :, 1])
        return [(q, k, v, g[:, :, d], beta[:, :, d]) for d in range(2)]

    def parts(rows):
        return rows[..., :768], rows[..., 768:1024], rows[..., 1024:1040]

    qkv_c, gate_c, ab_c = parts(pb[:, :CTX_LEN])
    qkv_x, gate_x, ab_x = parts(pb[:, CTX_LEN:])
    s_shape = (qkv_x.shape[0], GDN_HEADS, GDN_DK, GDN_DV)
    o_c, o_x = prefix_bidirectional_scan(gated_delta_chunked, prep(qkv_c, ab_c), prep(qkv_x, ab_x),
                                         [(), ()], s_shape)

    def gated_out(o, gate):
        b, l = gate.shape[:2]
        y = rms_norm(o, norm_g) * jax.nn.silu(gate.astype(F32)).reshape(o.shape)
        return y.reshape(b, l, GDN_HEADS * GDN_DV).astype(BF16)

    return jnp.concatenate([gated_out(o_c, gate_c), gated_out(o_x, gate_x)], axis=1)


def kernel(x, c, ctx, c_ctx, w_mod, b_mod, norm1, norm2, w_in, w_out, swa_sink, gdn_conv, gdn_a_log, gdn_dt_bias, gdn_norm, mla_q_norm, mla_kv_norm, mla_w_q_up, mla_w_kv_up, ret_log_decay, ret_norm, ffn_w_gate, ffn_w_up, ffn_w_down, moe_router, moe_w_gate, moe_w_up, moe_w_down, final_norm):
    b, n, d = x.shape
    depth = w_in.shape[0]
    cos_t, sin_t = rope_tables(n)
    cos_c, sin_c = cos_t[:, A_ROT_W:A_ROT_W + C_ROT_W], sin_t[:, A_ROT_W:A_ROT_W + C_ROT_W]
    silu_c = jax.nn.silu(c)
    silu_cc = jax.nn.silu(c_ctx)
    h = jnp.concatenate([ctx, x], axis=1)
    moe_wg, moe_wu, moe_wd = moe_w_gate.astype(BF16), moe_w_up.astype(BF16), moe_w_down.astype(BF16)
    for layer in range(depth):
        mod_x = jnp.dot(silu_c, w_mod[layer], precision=lax.Precision.HIGHEST) + b_mod[layer]
        mod_c = jnp.dot(silu_cc, w_mod[layer], precision=lax.Precision.HIGHEST) + b_mod[layer]
        mod = jnp.stack([jnp.broadcast_to(mod_c, mod_x.shape), mod_x], axis=1).reshape(b, 2, 6, d)
        mod = jnp.pad(mod, ((0, 0), (0, 0), (0, SUBLANES - 6), (0, 0)))
        pa, pb, pc, pd = norm_proj(h, mod, norm1[layer][None, :], build_in_weight(w_in[layer]), cos_t, sin_t)
        mix_a = swa_mixer(pa, swa_sink[layer])
        mix_b = gdn_mixer(pb, gdn_conv[layer], gdn_a_log[layer], gdn_dt_bias[layer], gdn_norm[layer])
        mq, mk, mv = mla_prep(pc, mla_q_norm[layer], mla_kv_norm[layer], mla_w_q_up[layer], mla_w_kv_up[layer],
                              cos_c, sin_c)
        mix_c = mla_attention(mq, mk, mv)
        mix_d = retention_mixer(pd, ret_log_decay[layer], ret_norm[layer])
        mixes = (mix_a, mix_b, mix_c, mix_d)
        w_o = build_out_weight(w_out[layer])
        i = layer // 2
        if layer % 2 == 0:
            h, v = out_proj(mixes, h, mod, norm2[layer][None, :], w_o)
            h = dense_ffn(v, h, mod, ffn_w_gate[i].astype(BF16), ffn_w_up[i].astype(BF16),
                          ffn_w_down[i].astype(BF16))
        else:
            w_r = jnp.pad(moe_router[i], ((0, 0), (0, LANES - N_EXPERTS)))
            h, v, logits = out_proj(mixes, h, mod, norm2[layer][None, :], w_o, w_r)
            h = moe_ffn(v, logits, h, mod, i, moe_wg, moe_wu, moe_wd)
    return final_rms_norm(h, final_norm[None, :], CTX_LEN // ROW_TILE)
```

```python
import functools

import numpy as np
import jax
import jax.numpy as jnp
from jax import lax
from jax.experimental import pallas as pl
from jax.experimental.pallas import tpu as pltpu

D_MODEL = 1024
GRID_W = 64
CTX_LEN = 256
HEAD_DIM = 64
ROPE_THETA = 10000.0
EPS = 1e-6
NEG_INF = -1e30

SWA_WINDOW = 128
GDN_HEADS = 4
GDN_DK = 64
GDN_DV = 64
GDN_CHUNK = 64
MLA_HEADS = 4
MLA_NOPE = 64
MLA_ROPE = 32
MLA_V = 64
RET_HEADS = 4
RET_DK = 64
D_FF = 3584
N_EXPERTS = 8
TOP_K = 2

LANES = 128
SUBLANES = 8
VMEM_LIMIT = 56 * 1024 * 1024

ROW_TILE = 256
FF_CHUNK = 512
MOE_TILE = 512
MOE_FF_CHUNK = 1792

A_W, B_W, C_W, D_W = 768, 1152, 512, 1024
A_ROT_W, C_ROT_W, D_ROT_W = 640, 128, 512
OFF_A = 0
OFF_B = OFF_A + A_W
OFF_C = OFF_B + B_W
OFF_D = OFF_C + C_W
OFF_AR = OFF_D + D_W
OFF_CR = OFF_AR + A_ROT_W
OFF_DR = OFF_CR + C_ROT_W
W_ALL = OFF_DR + D_ROT_W
ROPE_W = A_ROT_W + C_ROT_W + D_ROT_W

F32 = jnp.float32
BF16 = jnp.bfloat16
NT_DIMS = (((1,), (1,)), ((), ()))
TN_DIMS = (((0,), (0,)), ((), ()))


def _rms(x):
    return x * lax.rsqrt(jnp.mean(x * x, axis=-1, keepdims=True) + EPS)


def _silu(x):
    return x * (1.0 / (1.0 + jnp.exp(-x)))


def _dot(a, b):
    return jnp.dot(a, b, preferred_element_type=F32)


def _dot_nt(a, b):
    return lax.dot_general(a, b, NT_DIMS, preferred_element_type=F32)


def _dot_tn(a, b):
    return lax.dot_general(a, b, TN_DIMS, preferred_element_type=F32)


def _mod_spec(d):
    return pl.BlockSpec((None, None, SUBLANES, d), lambda i, t: (i, jnp.minimum(t, 1), 0, 0))


WIDE = 2


def _sub_tile_specs(block, tiles_per_seq, index_of):
    def spec(k):
        return pl.BlockSpec(block, lambda j: index_of((WIDE * j + k) // tiles_per_seq, (WIDE * j + k) % tiles_per_seq))
    return [spec(k) for k in range(WIDE)]


def _sub_mod_specs(d, tiles_per_seq):
    return _sub_tile_specs((None, None, SUBLANES, d), tiles_per_seq, lambda b, t: (b, jnp.minimum(t, 1), 0, 0))


def _sub_rows(k):
    return slice(k * ROW_TILE, (k + 1) * ROW_TILE)


def _norm_proj_kernel(h_ref, *refs):
    mods, (g_ref, w_ref), tabs = refs[:WIDE], refs[WIDE:WIDE + 2], refs[WIDE + 2:3 * WIDE + 2]
    a_ref, b_ref, c_ref, d_ref = refs[3 * WIDE + 2:]
    y = _rms(h_ref[...]) * g_ref[...]
    u = jnp.concatenate([y[_sub_rows(k)] * (1.0 + m[1:2, :]) + m[0:1, :] for k, m in enumerate(mods)],
                        axis=0).astype(BF16)

    def mm(lo, width):
        return _dot(u, w_ref[:, lo:lo + width])

    a_main = mm(OFF_A, A_W)
    a_rot = mm(OFF_AR, A_ROT_W)
    b_ref[...] = mm(OFF_B, B_W)
    c_main = mm(OFF_C, C_W)
    c_rot = mm(OFF_CR, C_ROT_W)
    d_main = mm(OFF_D, D_W)
    d_rot = mm(OFF_DR, D_ROT_W)
    lo, hi = A_ROT_W, A_ROT_W + C_ROT_W
    a_ref[:, A_ROT_W:] = a_main[:, A_ROT_W:].astype(BF16)
    c_ref[:, :C_W - C_ROT_W] = c_main[:, :C_W - C_ROT_W]
    d_ref[:, D_ROT_W:] = d_main[:, D_ROT_W:]
    for k in range(WIDE):
        cos_ref, sin_ref = tabs[2 * k], tabs[2 * k + 1]
        r = _sub_rows(k)
        a_ref[r, :A_ROT_W] = (a_main[r, :A_ROT_W] * cos_ref[:, :A_ROT_W] + a_rot[r] * sin_ref[:, :A_ROT_W]).astype(BF16)
        c_ref[r, C_W - C_ROT_W:] = c_main[r, C_W - C_ROT_W:] * cos_ref[:, lo:hi] + c_rot[r] * sin_ref[:, lo:hi]
        d_ref[r, :D_ROT_W] = d_main[r, :D_ROT_W] * cos_ref[:, hi:] + d_rot[r] * sin_ref[:, hi:]


def norm_proj(h, mod, gain, w, cos_t, sin_t):
    b, l, d = h.shape
    tm = WIDE * ROW_TILE
    tps = l // ROW_TILE
    row = lambda j: (j, 0)
    const = lambda j: (0, 0)
    tabs = _sub_tile_specs((ROW_TILE, ROPE_W), tps, lambda bi, t: (t, 0))
    tab_specs = [s for pair in zip(tabs, _sub_tile_specs((ROW_TILE, ROPE_W), tps, lambda bi, t: (t, 0))) for s in pair]
    outs = pl.pallas_call(
        _norm_proj_kernel,
        grid=(b * l // tm,),
        in_specs=[pl.BlockSpec((tm, d), row)] + _sub_mod_specs(d, tps) + [
            pl.BlockSpec((1, d), const),
            pl.BlockSpec((d, W_ALL), const, pipeline_mode=pl.Buffered(1))] + tab_specs,
        out_specs=[pl.BlockSpec((tm, A_W), row), pl.BlockSpec((tm, B_W), row),
                   pl.BlockSpec((tm, C_W), row), pl.BlockSpec((tm, D_W), row)],
        out_shape=[jax.ShapeDtypeStruct((b * l, A_W), BF16), jax.ShapeDtypeStruct((b * l, B_W), F32),
                   jax.ShapeDtypeStruct((b * l, C_W), F32), jax.ShapeDtypeStruct((b * l, D_W), F32)],
        compiler_params=pltpu.CompilerParams(dimension_semantics=("parallel",), vmem_limit_bytes=VMEM_LIMIT),
        name="norm_proj",
    )(h.reshape(b * l, d), *([mod] * WIDE), gain, w, *([cos_t, sin_t] * WIDE))
    return [o.reshape(b, l, -1) for o in outs]


def _rot_cols(w, hd):
    x = w.reshape(w.shape[:-1] + (w.shape[-1] // hd, 4, hd // 4))
    x1, x2, x3, x4 = x[..., 0, :], x[..., 1, :], x[..., 2, :], x[..., 3, :]
    return jnp.stack([-x2, x1, -x4, x3], axis=-2).reshape(w.shape)


def _place_swa_q(q):
    z = jnp.zeros((q.shape[0], HEAD_DIM), q.dtype)
    blocks = []
    for h in range(4):
        qh = q[:, HEAD_DIM * h:HEAD_DIM * (h + 1)]
        blocks += [qh, z] if h // 2 == 0 else [z, qh]
    return jnp.concatenate(blocks, axis=1)


def build_in_weight(w):
    d = w.shape[0]
    o = [int(v) for v in np.cumsum((256, 128, 128, 768, 256, 16, 256, 128, 32, 256, 256, 256, 256))]
    aq, ak, av = w[:, :o[0]] * HEAD_DIM ** -0.5, w[:, o[0]:o[1]], w[:, o[1]:o[2]]
    b_main, b_ab = w[:, o[2]:o[4]], w[:, o[4]:o[5]]
    c_q, c_kv, c_kr = w[:, o[5]:o[6]], w[:, o[6]:o[7]], w[:, o[7]:o[8]]
    dq, dk, dvg = w[:, o[8]:o[9]], w[:, o[9]:o[10]] * RET_DK ** -0.5, w[:, o[10]:]
    z = lambda n: jnp.zeros((d, n), w.dtype)
    parts = [
        _place_swa_q(aq), ak, av,
        b_main, b_ab, z(LANES - b_ab.shape[1]),
        c_q, c_kv, z(64), c_kr, z(32),
        dq, dk, dvg,
        _place_swa_q(_rot_cols(aq, HEAD_DIM)), _rot_cols(ak, HEAD_DIM),
        z(64), _rot_cols(c_kr, MLA_ROPE), z(32),
        _rot_cols(dq, HEAD_DIM), _rot_cols(dk, HEAD_DIM),
    ]
    out = jnp.concatenate(parts, axis=1)
    assert out.shape[1] == W_ALL
    return out.astype(BF16)


def rope_tables(n):
    def axial(rot_dim):
        n_freq = rot_dim // 4
        inv_freq = ROPE_THETA ** (-jnp.arange(n_freq, dtype=F32) / n_freq)
        row = jnp.repeat(jnp.arange(n // GRID_W, dtype=F32), GRID_W)
        col = jnp.tile(jnp.arange(GRID_W, dtype=F32), n // GRID_W)
        ang_r = row[:, None] * inv_freq
        ang_c = col[:, None] * inv_freq
        ang = jnp.concatenate([ang_r, ang_r, ang_c, ang_c], axis=-1)
        return jnp.cos(ang), jnp.sin(ang)

    cos_h, sin_h = axial(HEAD_DIM)
    cos_r, sin_r = axial(MLA_ROPE)
    one, zero = jnp.ones((n, 1), F32), jnp.zeros((n, 1), F32)
    cos_c = jnp.concatenate([jnp.tile(one, (1, 64)), cos_r, jnp.tile(one, (1, 32))], axis=1)
    sin_c = jnp.concatenate([jnp.tile(zero, (1, 64)), sin_r, jnp.tile(zero, (1, 32))], axis=1)
    cos_t = jnp.concatenate([jnp.tile(cos_h, (1, A_ROT_W // HEAD_DIM)), cos_c,
                             jnp.tile(cos_h, (1, D_ROT_W // HEAD_DIM))], axis=1)
    sin_t = jnp.concatenate([jnp.tile(sin_h, (1, A_ROT_W // HEAD_DIM)), sin_c,
                             jnp.tile(sin_h, (1, D_ROT_W // HEAD_DIM))], axis=1)
    cos_t = jnp.concatenate([jnp.ones((CTX_LEN, ROPE_W), F32), cos_t], axis=0)
    sin_t = jnp.concatenate([jnp.zeros((CTX_LEN, ROPE_W), F32), sin_t], axis=0)
    return cos_t, sin_t


def _swa_kernel(sink_ref, q_ref, kp_ref, ko_ref, kn_ref, kc_ref, vp_ref, vo_ref, vn_ref, vc_ref, o_ref):
    t = pl.program_id(1)
    last = pl.num_programs(1) - 1
    tq = q_ref.shape[0]
    half = tq // 2

    def head_out(h, pieces):
        q = q_ref[:, LANES * h:LANES * (h + 1)]
        ss = []
        for k_ref, _, mask in pieces:
            s = _dot_nt(q, k_ref[...])
            ss.append(s if mask is None else jnp.where(mask, s, NEG_INF))
        sink = sink_ref[h]
        m = jnp.maximum(functools.reduce(jnp.maximum, [s.max(axis=-1, keepdims=True) for s in ss]), sink)
        ps = [jnp.exp(s - m) for s in ss]
        denom = functools.reduce(jnp.add, [p.sum(axis=-1, keepdims=True) for p in ps]) + jnp.exp(sink - m)
        o = functools.reduce(jnp.add, [_dot(p.astype(BF16), piece[1][...]) for p, piece in zip(ps, pieces)])
        return o / denom

    def write(pieces):
        outs = [head_out(h, pieces) for h in range(4)]
        lane = lax.broadcasted_iota(jnp.int32, (tq, LANES), 1)
        for r in range(2):
            o_ref[:, LANES * r:LANES * (r + 1)] = jnp.where(lane < HEAD_DIM, outs[r], outs[2 + r]).astype(o_ref.dtype)

    @pl.when(t == 0)
    def _():
        write([(kc_ref, vc_ref, None)])

    @pl.when(t > 0)
    def _():
        qi = lax.broadcasted_iota(jnp.int32, (tq, half), 0)
        kj = lax.broadcasted_iota(jnp.int32, (tq, half), 1)
        mask_prev = (kj >= qi) & (t > 1)
        mask_next = (kj <= qi - half) & (t < last)
        qo = lax.broadcasted_iota(jnp.int32, (tq, tq), 0)
        ko = lax.broadcasted_iota(jnp.int32, (tq, tq), 1)
        mask_own = jnp.abs(qo - ko) <= SWA_WINDOW
        write([(kp_ref, vp_ref, mask_prev), (ko_ref, vo_ref, mask_own), (kn_ref, vn_ref, mask_next),
               (kc_ref, vc_ref, None)])


def swa_mixer(pa, sink):
    b, l, _ = pa.shape
    tq = ROW_TILE
    nblk = l // SWA_WINDOW
    kcol, vcol = 4, 5
    prev = lambda c: (lambda i, t: (i, jnp.maximum(2 * t - 1, 2), c))
    nxt = lambda c: (lambda i, t: (i, jnp.minimum(2 * t + 2, nblk - 1), c))
    own = lambda c: (lambda i, t: (i, t, c))
    ctx = lambda c: (lambda i, t: (i, 0, c))
    kv_specs = lambda c: [pl.BlockSpec((None, SWA_WINDOW, LANES), prev(c)), pl.BlockSpec((None, tq, LANES), own(c)),
                          pl.BlockSpec((None, SWA_WINDOW, LANES), nxt(c)), pl.BlockSpec((None, tq, LANES), ctx(c))]
    return pl.pallas_call(
        _swa_kernel,
        grid=(b, l // tq),
        in_specs=[pl.BlockSpec(memory_space=pltpu.SMEM),
                  pl.BlockSpec((None, tq, 4 * LANES), lambda i, t: (i, t, 0))] + kv_specs(kcol) + kv_specs(vcol),
        out_specs=pl.BlockSpec((None, tq, 2 * LANES), lambda i, t: (i, t, 0)),
        out_shape=jax.ShapeDtypeStruct((b, l, 2 * LANES), BF16),
        compiler_params=pltpu.CompilerParams(
            dimension_semantics=("parallel", "parallel"), vmem_limit_bytes=VMEM_LIMIT),
        name="swa",
    )(sink, pa, pa, pa, pa, pa, pa, pa, pa, pa)


def _mla_prep_kernel(c_ref, qn_ref, kvn_ref, wq_ref, wqr_ref, wk_ref, wv_ref, cos_ref, sin_ref,
                     q_ref, k_ref, v_ref):
    cq = c_ref[:, 0:256]
    ckv = c_ref[:, 256:384]
    kr = c_ref[:, 384:512]
    nq = (_rms(cq) * qn_ref[...]).astype(BF16)
    nkv = (_rms(ckv) * kvn_ref[...]).astype(BF16)
    cos = jnp.concatenate([cos_ref[...]] * MLA_HEADS, axis=1)
    sin = jnp.concatenate([sin_ref[...]] * MLA_HEADS, axis=1)
    q_ref[...] = (_dot(nq, wq_ref[...]) * cos + _dot(nq, wqr_ref[...]) * sin).astype(BF16)
    k_ref[...] = (_dot(nkv, wk_ref[...]) + jnp.concatenate([kr] * MLA_HEADS, axis=1)).astype(BF16)
    v_ref[...] = _dot(nkv, wv_ref[...]).astype(BF16)


def mla_prep(pc, q_norm, kv_norm, w_q_up, w_kv_up, cos_c, sin_c):
    b, l, _ = pc.shape
    tm = ROW_TILE
    scale = (MLA_NOPE + MLA_ROPE) ** -0.5 * float(np.log2(np.e))
    wq = (w_q_up * scale).reshape(-1, MLA_HEADS, MLA_NOPE + MLA_ROPE)
    zq = jnp.zeros(wq.shape[:2] + (LANES - MLA_NOPE - MLA_ROPE,), F32)
    wq_main = jnp.concatenate([wq, zq], axis=-1).reshape(-1, MLA_HEADS * LANES)
    wq_rot = jnp.concatenate([jnp.zeros_like(wq[..., :MLA_NOPE]), _rot_cols(wq[..., MLA_NOPE:], MLA_ROPE), zq],
                             axis=-1).reshape(-1, MLA_HEADS * LANES)
    wkv = w_kv_up.reshape(-1, MLA_HEADS, MLA_NOPE + MLA_V)
    wk = jnp.concatenate([wkv[..., :MLA_NOPE], jnp.zeros_like(wkv[..., :LANES - MLA_NOPE])],
                         axis=-1).reshape(-1, MLA_HEADS * LANES)
    wv = wkv[..., MLA_NOPE:].reshape(-1, MLA_HEADS * MLA_V)
    row = lambda i, t: (i, t, 0)
    const = lambda i, t: (0, 0)
    full = lambda a: pl.BlockSpec(a.shape, const)
    args = [q_norm[None, :], kv_norm[None, :], wq_main.astype(BF16), wq_rot.astype(BF16), wk.astype(BF16),
            wv.astype(BF16)]
    return pl.pallas_call(
        _mla_prep_kernel,
        grid=(b, l // tm),
        in_specs=[pl.BlockSpec((None, tm, C_W), row)] + [full(a) for a in args]
        + [pl.BlockSpec((tm, LANES), lambda i, t: (t, 0)), pl.BlockSpec((tm, LANES), lambda i, t: (t, 0))],
        out_specs=[pl.BlockSpec((None, tm, 4 * LANES), row), pl.BlockSpec((None, tm, 4 * LANES), row),
                   pl.BlockSpec((None, tm, 2 * LANES), row)],
        out_shape=[jax.ShapeDtypeStruct((b, l, 4 * LANES), BF16), jax.ShapeDtypeStruct((b, l, 4 * LANES), BF16),
                   jax.ShapeDtypeStruct((b, l, 2 * LANES), BF16)],
        compiler_params=pltpu.CompilerParams(dimension_semantics=("parallel", "parallel")),
        name="mla_prep",
    )(pc, *args, cos_c, sin_c)


def _mla_attn_kernel(q_ref, k_ref, v_ref, o_ref):
    t = pl.program_id(2)
    tq = q_ref.shape[0]

    def attend(nk):
        v = v_ref[0:nk, :]
        v_lane = lax.broadcasted_iota(jnp.int32, v.shape, 1)
        outs = []
        for j in range(2):
            q = q_ref[:, LANES * j:LANES * (j + 1)]
            k = k_ref[0:nk, LANES * j:LANES * (j + 1)]
            s = _dot_nt(q, k)
            p = jnp.exp2(s - s.max(axis=-1, keepdims=True)).astype(BF16)
            other = (v_lane >= MLA_V) if j == 0 else (v_lane < MLA_V)
            o = _dot(p, jnp.where(other, jnp.ones_like(v), v))
            den_lane = MLA_V if j == 0 else 0
            outs.append(o / o[:, den_lane:den_lane + 1])
        lane = lax.broadcasted_iota(jnp.int32, (tq, LANES), 1)
        o_ref[...] = jnp.where(lane < MLA_V, outs[0], outs[1]).astype(o_ref.dtype)

    @pl.when(t == 0)
    def _():
        attend(CTX_LEN)

    @pl.when(t > 0)
    def _():
        attend(k_ref.shape[0])


def mla_attention(q, k, v):
    b, l, _ = q.shape
    tq = ROW_TILE
    return pl.pallas_call(
        _mla_attn_kernel,
        grid=(b, 2, l // tq),
        in_specs=[pl.BlockSpec((None, tq, 2 * LANES), lambda i, p, t: (i, t, p)),
                  pl.BlockSpec((None, l, 2 * LANES), lambda i, p, t: (i, 0, p)),
                  pl.BlockSpec((None, l, LANES), lambda i, p, t: (i, 0, p))],
        out_specs=pl.BlockSpec((None, tq, LANES), lambda i, p, t: (i, t, p)),
        out_shape=jax.ShapeDtypeStruct((b, l, 2 * LANES), BF16),
        compiler_params=pltpu.CompilerParams(
            dimension_semantics=("parallel", "parallel", "parallel"), vmem_limit_bytes=VMEM_LIMIT),
        name="mla_attn",
    )(q, k, v)


def _head_mean(x, ones_bd):
    hi = x.astype(BF16)
    lo = (x - hi.astype(F32)).astype(BF16)
    return (_dot(hi, ones_bd) + _dot(lo, ones_bd)) * (1.0 / HEAD_DIM)


def _ret_kernel(x_ref, lg_ref, g_ref, o_ref, s_ref, dec_ref, part_ref):
    dr = pl.program_id(0)
    s = pl.program_id(1)
    ns = pl.num_programs(1)
    nb, c = x_ref.shape[0], x_ref.shape[1]
    w = RET_HEADS * HEAD_DIM
    chunk = jnp.where(s == 0, 0, jnp.where(dr == 0, s, ns - s))
    lg = lg_ref[...]
    fwd = dr == 0
    row_h = lax.broadcasted_iota(jnp.int32, (w, w), 0) // HEAD_DIM
    col_h = lax.broadcasted_iota(jnp.int32, (w, w), 1) // HEAD_DIM
    same_head = row_h == col_h

    @pl.when(s == 0)
    def _():
        s_ref[...] = jnp.zeros_like(s_ref)
        i = lax.broadcasted_iota(jnp.int32, (c, c), 0)
        j = lax.broadcasted_iota(jnp.int32, (c, c), 1)
        rel = jnp.where(fwd, i - j, j - i)
        relf = jnp.maximum(rel, 0).astype(F32)
        for h in range(RET_HEADS):
            lg_h = lg_ref[0:1, HEAD_DIM * h:HEAD_DIM * h + 1]
            dec_ref[h] = jnp.where(rel >= 0, jnp.exp(lg_h * relf), 0.0)

    pos = lax.broadcasted_iota(jnp.int32, (c, 1), 0).astype(F32)
    q_dec = jnp.exp(lg * jnp.where(fwd, pos + 1.0, c - pos))
    k_dec = jnp.exp(lg * jnp.where(fwd, c - 1.0 - pos, pos))
    lane_h = lax.broadcasted_iota(jnp.int32, (c, w), 1) // HEAD_DIM
    bs = range(nb)
    q = [x_ref[i, :, 0:w] for i in bs]
    kf = [x_ref[i, :, w:2 * w] for i in bs]
    v = [x_ref[i, :, 2 * w:3 * w].astype(BF16) for i in bs]
    kb = [x.astype(BF16) for x in kf]
    acc = [_dot((q[i] * q_dec).astype(BF16), s_ref[i].astype(BF16)) for i in bs]
    for h in range(RET_HEADS):
        qh = [jnp.where(lane_h == h, q[i], 0.0).astype(BF16) for i in bs]
        a = [(_dot_nt(qh[i], kb[i]) * dec_ref[h]).astype(BF16) for i in bs]
        acc = [acc[i] + jnp.where(lane_h == h, _dot(a[i], v[i]), 0.0) for i in bs]
    kv = [_dot_tn((kf[i] * k_dec).astype(BF16), v[i]) for i in bs]
    chunk_dec = jnp.exp(lg * float(c))
    for i in bs:
        s_ref[i] = s_ref[i] * chunk_dec + jnp.where(same_head, kv[i], 0.0)

    rows = pl.ds(pl.multiple_of(chunk * c, c), c)

    @pl.when(dr == 0)
    def _():
        for i in bs:
            part_ref[i, rows, :] = acc[i]

    @pl.when(dr == 1)
    def _():
        ones_bd = jnp.where(same_head, 1.0, 0.0).astype(BF16)
        for i in bs:
            o = part_ref[i, rows, :] + acc[i]
            mu = _head_mean(o, ones_bd)
            var = _head_mean(jnp.square(o - mu), ones_bd)
            y = (o - mu) * lax.rsqrt(var + EPS) * g_ref[...]
            o_ref[i] = (y * _silu(x_ref[i, :, 3 * w:4 * w])).astype(o_ref.dtype)


def retention_mixer(pd, log_decay, norm_g):
    b, l, _ = pd.shape
    c = ROW_TILE
    ns = l // c
    w = RET_HEADS * HEAD_DIM
    lg = jnp.repeat(-jnp.exp(log_decay.astype(F32)), HEAD_DIM, axis=-1)[:, None, :]

    def chunk_of(dr, s):
        return jnp.where(s == 0, 0, jnp.where(dr == 0, s, ns - s))

    return pl.pallas_call(
        _ret_kernel,
        grid=(2, ns),
        in_specs=[pl.BlockSpec((b, c, D_W), lambda dr, s: (0, chunk_of(dr, s), 0)),
                  pl.BlockSpec((None, 1, w), lambda dr, s: (dr, 0, 0)),
                  pl.BlockSpec((1, w), lambda dr, s: (0, 0))],
        out_specs=pl.BlockSpec((b, c, w), lambda dr, s: (0, jnp.where(dr == 0, 0, chunk_of(dr, s)), 0)),
        out_shape=jax.ShapeDtypeStruct((b, l, w), BF16),
        scratch_shapes=[pltpu.VMEM((b, w, w), F32), pltpu.VMEM((RET_HEADS, c, c), F32), pltpu.VMEM((b, l, w), F32)],
        compiler_params=pltpu.CompilerParams(
            dimension_semantics=("arbitrary", "arbitrary"), vmem_limit_bytes=VMEM_LIMIT),
        name="retention",
    )(pd, lg, norm_g[None, :])


def _out_proj_kernel(ma_ref, mb_ref, mc_ref, md_ref, h_ref, *refs, with_router):
    mods, (g_ref, w_ref), rest = refs[:WIDE], refs[WIDE:WIDE + 2], refs[WIDE + 2:]
    if with_router:
        wr_ref, hn_ref, v_ref, lg_ref = rest
    else:
        hn_ref, v_ref = rest
    gw = 2 * LANES
    mix = functools.reduce(jnp.add, [
        _dot(m_ref[...].astype(BF16), w_ref[gw * i:gw * (i + 1), :])
        for i, m_ref in enumerate((ma_ref, mb_ref, mc_ref, md_ref))])
    for k, mod_ref in enumerate(mods):
        r = _sub_rows(k)
        hn = h_ref[r, :] + mod_ref[2:3, :] * mix[r]
        hn_ref[r, :] = hn
        v = _rms(hn) * g_ref[...] * (1.0 + mod_ref[4:5, :]) + mod_ref[3:4, :]
        v_ref[r, :] = v.astype(v_ref.dtype)
        if with_router:
            lg_ref[r, :] = jnp.dot(v, wr_ref[...], preferred_element_type=F32, precision=lax.Precision.HIGHEST)


def out_proj(mixes, h, mod, gain, w, w_router=None):
    b, l, d = h.shape
    tm = WIDE * ROW_TILE
    n = b * l
    with_router = w_router is not None
    row = lambda j: (j, 0)
    const = lambda j: (0, 0)
    in_specs = [pl.BlockSpec((tm, 2 * LANES), row) for _ in mixes] + [pl.BlockSpec((tm, d), row)] + _sub_mod_specs(
        d, l // ROW_TILE) + [pl.BlockSpec((1, d), const), pl.BlockSpec(w.shape, const, pipeline_mode=pl.Buffered(1))]
    out_specs = [pl.BlockSpec((tm, d), row), pl.BlockSpec((tm, d), row)]
    out_shape = [jax.ShapeDtypeStruct((n, d), F32), jax.ShapeDtypeStruct((n, d), F32 if with_router else BF16)]
    args = [m.reshape(n, 2 * LANES) for m in mixes] + [h.reshape(n, d)] + [mod] * WIDE + [gain, w]
    if with_router:
        in_specs.append(pl.BlockSpec(w_router.shape, const))
        out_specs.append(pl.BlockSpec((tm, LANES), row))
        out_shape.append(jax.ShapeDtypeStruct((n, LANES), F32))
        args.append(w_router)
    outs = pl.pallas_call(
        functools.partial(_out_proj_kernel, with_router=with_router),
        grid=(n // tm,),
        in_specs=in_specs,
        out_specs=out_specs,
        out_shape=out_shape,
        compiler_params=pltpu.CompilerParams(dimension_semantics=("parallel",), vmem_limit_bytes=VMEM_LIMIT),
        name="out_proj",
    )(*args)
    return [o.reshape(b, l, -1) for o in outs]


def build_out_weight(w):
    hd = HEAD_DIM
    perm = jnp.concatenate([w[0:hd], w[2 * hd:3 * hd], w[hd:2 * hd], w[3 * hd:4 * hd]], axis=0)
    return jnp.concatenate([perm, w[4 * hd:]], axis=0).astype(BF16)


def _ffn_kernel(v_ref, h_ref, *refs):
    mods, (wg_ref, wu_ref, wd_ref, o_ref) = refs[:WIDE], refs[WIDE:]
    v = v_ref[...]
    acc = jnp.zeros(o_ref.shape, F32)
    for j in range(D_FF // FF_CHUNK):
        cols = slice(j * FF_CHUNK, (j + 1) * FF_CHUNK)
        a = _dot(v, wg_ref[:, cols])
        u = _dot(v, wu_ref[:, cols])
        mid = (_silu(a) * u).astype(BF16)
        acc = acc + _dot(mid, wd_ref[cols, :])
    for k, mod_ref in enumerate(mods):
        r = _sub_rows(k)
        o_ref[r, :] = h_ref[r, :] + mod_ref[5:6, :] * acc[r]


def dense_ffn(v, h, mod, wg, wu, wd):
    b, l, d = h.shape
    tm = WIDE * ROW_TILE
    n = b * l
    row = lambda j: (j, 0)
    const = lambda j: (0, 0)
    return pl.pallas_call(
        _ffn_kernel,
        grid=(n // tm,),
        in_specs=[pl.BlockSpec((tm, d), row), pl.BlockSpec((tm, d), row)] + _sub_mod_specs(d, l // ROW_TILE) + [
            pl.BlockSpec(wg.shape, const, pipeline_mode=pl.Buffered(1)),
            pl.BlockSpec(wu.shape, const, pipeline_mode=pl.Buffered(1)),
            pl.BlockSpec(wd.shape, const, pipeline_mode=pl.Buffered(1)),
        ],
        out_specs=pl.BlockSpec((tm, d), row),
        out_shape=jax.ShapeDtypeStruct((n, d), F32),
        compiler_params=pltpu.CompilerParams(dimension_semantics=("parallel",), vmem_limit_bytes=VMEM_LIMIT),
        name="dense_ffn",
    )(v.reshape(n, d), h.reshape(n, d), *([mod] * WIDE), wg, wu, wd).reshape(b, l, d)


def _moe_kernel(wt_ref, we_ref, lo_ref, hi_ref, first_ref, x_ref, wg_ref, wu_ref, wd_ref, o_ref, xm_ref, acc_ref):
    w = pl.program_id(0)
    j = pl.program_id(1)
    nj = pl.num_programs(1)
    tm = x_ref.shape[0]

    @pl.when(j == 0)
    def _():
        row = wt_ref[w] * tm + lax.broadcasted_iota(jnp.int32, (tm, 1), 0)
        keep = (row >= lo_ref[w]) & (row < hi_ref[w])
        xm_ref[...] = jnp.where(keep, x_ref[...], 0.0).astype(BF16)

    @pl.when((j == 0) & (first_ref[w] > 0))
    def _():
        acc_ref[...] = jnp.zeros_like(acc_ref)

    def swiglu_rows(rows):
        x = xm_ref[rows, :]
        a = _dot(x, wg_ref[...])
        u = _dot(x, wu_ref[...])
        mid = (_silu(a) * u).astype(BF16)
        acc_ref[rows, :] += _dot(mid, wd_ref[...])

    tile_lo = wt_ref[w] * tm
    whole = (lo_ref[w] <= tile_lo) & (hi_ref[w] >= tile_lo + tm)

    @pl.when(whole)
    def _():
        swiglu_rows(slice(0, tm))

    half = tm // 2
    for part in range(2):
        part_lo = tile_lo + part * half

        @pl.when(jnp.logical_not(whole) & (hi_ref[w] > jnp.maximum(part_lo, lo_ref[w])) & (lo_ref[w] < part_lo + half))
        def _():
            swiglu_rows(slice(part * half, (part + 1) * half))

    @pl.when(j == nj - 1)
    def _():
        o_ref[...] = acc_ref[...]


def moe_grouped_ffn(xs, items, layer_idx, wg, wu, wd):
    s, d = xs.shape
    tm = MOE_TILE
    fc = MOE_FF_CHUNK
    nw = items[0].shape[0]
    nj = D_FF // fc
    grid_spec = pltpu.PrefetchScalarGridSpec(
        num_scalar_prefetch=5,
        grid=(nw, nj),
        in_specs=[
            pl.BlockSpec((tm, d), lambda w, j, wt, we, lo, hi, fi: (wt[w], 0)),
            pl.BlockSpec((None, None, d, fc), lambda w, j, wt, we, lo, hi, fi: (layer_idx, we[w], 0, j)),
            pl.BlockSpec((None, None, d, fc), lambda w, j, wt, we, lo, hi, fi: (layer_idx, we[w], 0, j)),
            pl.BlockSpec((None, None, fc, d), lambda w, j, wt, we, lo, hi, fi: (layer_idx, we[w], j, 0)),
        ],
        out_specs=pl.BlockSpec((tm, d), lambda w, j, wt, we, lo, hi, fi: (wt[w], 0)),
        scratch_shapes=[pltpu.VMEM((tm, d), BF16), pltpu.VMEM((tm, d), F32)],
    )
    return pl.pallas_call(
        _moe_kernel,
        grid_spec=grid_spec,
        out_shape=jax.ShapeDtypeStruct((s, d), F32),
        compiler_params=pltpu.CompilerParams(
            dimension_semantics=("arbitrary", "arbitrary"), vmem_limit_bytes=VMEM_LIMIT),
        name="moe_ffn",
    )(*items, xs, wg, wu, wd)


def _residual_kernel(h_ref, f0_ref, f1_ref, gate_ref, mod_ref, o_ref):
    f = gate_ref[:, 0:1] * f0_ref[...] + gate_ref[:, 1:2] * f1_ref[...]
    o_ref[...] = h_ref[...] + mod_ref[5:6, :] * f


def gated_residual(h, f0, f1, gates, mod):
    b, l, d = h.shape
    tm = ROW_TILE
    row = lambda i, t: (i, t, 0)
    return pl.pallas_call(
        _residual_kernel,
        grid=(b, l // tm),
        in_specs=[pl.BlockSpec((None, tm, d), row), pl.BlockSpec((None, tm, d), row), pl.BlockSpec((None, tm, d), row),
                  pl.BlockSpec((None, tm, LANES), row), _mod_spec(d)],
        out_specs=pl.BlockSpec((None, tm, d), row),
        out_shape=jax.ShapeDtypeStruct((b, l, d), F32),
        compiler_params=pltpu.CompilerParams(dimension_semantics=("parallel", "parallel")),
        name="gated_residual",
    )(h, f0, f1, gates, mod)


def moe_ffn(v, logits, h, mod, layer_idx, wg, wu, wd):
    b, l, d = h.shape
    t = b * l
    s = TOP_K * t
    tm = MOE_TILE
    nt = s // tm
    nw = nt + N_EXPERTS - 1
    i32 = jnp.int32
    lg = logits.reshape(t, LANES)[:, :N_EXPERTS]
    top_val, top_idx = lax.top_k(lg, TOP_K)
    gates = jax.nn.softmax(top_val, axis=-1)
    slot = jnp.arange(s, dtype=i32)
    skey = jnp.sort(top_idx.reshape(-1).astype(i32) * s + slot)
    order = skey % s
    _, inv = lax.sort_key_val(order, slot)
    bounds = (jnp.arange(N_EXPERTS, dtype=i32) + 1) * s
    cum = jnp.sum((skey[None, :] < bounds[:, None]).astype(i32), axis=1)
    cum_prev = jnp.concatenate([jnp.zeros((1,), i32), cum[:-1]])
    tile_lo = jnp.arange(nt, dtype=i32) * tm
    count_le = lambda edges, x: jnp.sum((edges[None, :] <= x[:, None]).astype(i32), axis=1)
    e_first = count_le(cum, tile_lo)
    e_last = count_le(cum, tile_lo + tm - 1)
    n_items = e_last - e_first + 1
    item_end = jnp.cumsum(n_items)
    item_start = item_end - n_items
    w = jnp.arange(nw, dtype=i32)
    wt = jnp.minimum(count_le(item_end, w), nt - 1)
    valid = w < item_end[-1]
    we = jnp.clip(e_first[wt] + w - item_start[wt], 0, N_EXPERTS - 1).astype(i32)
    lo = jnp.where(valid, cum_prev[we], 0).astype(i32)
    hi = jnp.where(valid, cum[we], 0).astype(i32)
    first = (valid & (w == item_start[wt])).astype(i32)
    rows_of = lambda a, idx: a.at[idx].get(mode="promise_in_bounds")
    xs = rows_of(v.reshape(t, d), order // TOP_K)
    ys = moe_grouped_ffn(xs, (wt, we, lo, hi, first), layer_idx, wg, wu, wd)
    dest = inv.reshape(t, TOP_K)
    f0 = rows_of(ys, dest[:, 0]).reshape(b, l, d)
    f1 = rows_of(ys, dest[:, 1]).reshape(b, l, d)
    gates_p = jnp.pad(gates, ((0, 0), (0, LANES - TOP_K))).reshape(b, l, LANES)
    return gated_residual(h, f0, f1, gates_p, mod)


def _final_norm_kernel(h_ref, g_ref, o_ref):
    o_ref[...] = _rms(h_ref[...]) * g_ref[...]


def final_rms_norm(h, gain, n_ctx_tiles):
    b, l, d = h.shape
    tm = ROW_TILE
    n = l - n_ctx_tiles * tm
    return pl.pallas_call(
        _final_norm_kernel,
        grid=(b, n // tm),
        in_specs=[
            pl.BlockSpec((None, tm, d), lambda i, t: (i, t + n_ctx_tiles, 0)),
            pl.BlockSpec((1, d), lambda i, t: (0, 0)),
        ],
        out_specs=pl.BlockSpec((None, tm, d), lambda i, t: (i, t, 0)),
        out_shape=jax.ShapeDtypeStruct((b, n, d), F32),
        compiler_params=pltpu.CompilerParams(dimension_semantics=("parallel", "parallel")),
        name="final_norm",
    )(h, gain)


GDN_W = GDN_HEADS * GDN_DK
GDN_CONV_K = 5
GDN_HALO = SUBLANES


def _split3(x):
    p0 = x.astype(BF16)
    r1 = x - p0.astype(F32)
    p1 = r1.astype(BF16)
    p2 = (r1 - p1.astype(F32)).astype(BF16)
    return p0, p1, p2


def _gdn_prep_kernel(x_ref, prev_ref, next_ref, cw_ref, par_ref, q_ref, k_ref, v_ref, gb_ref):
    t = pl.program_id(1)
    last = pl.num_programs(1) - 1
    tm = x_ref.shape[0]
    w3 = 3 * GDN_W
    has_prev = t > 1
    has_next = (t > 0) & (t < last)
    prev = jnp.where(has_prev, prev_ref[...], 0.0)
    nxt = jnp.where(has_next, next_ref[...], 0.0)
    xe = jnp.concatenate([prev, x_ref[:, :w3], nxt], axis=0)
    y = jnp.zeros((tm, w3), F32)
    for j in range(GDN_CONV_K):
        lo = GDN_HALO - GDN_CONV_K // 2 + j
        y = y + cw_ref[j:j + 1, :] * xe[lo:lo + tm, :]
    y = _silu(y)
    r = lax.broadcasted_iota(jnp.int32, (GDN_W, GDN_W), 0)
    c = lax.broadcasted_iota(jnp.int32, (GDN_W, GDN_W), 1)
    ones_bd = jnp.where(r // GDN_DK == c // GDN_DK, 1.0, 0.0).astype(BF16)

    def l2n(x):
        sq = x * x
        hi = sq.astype(BF16)
        lo = (sq - hi.astype(F32)).astype(BF16)
        return x * lax.rsqrt(_dot(hi, ones_bd) + _dot(lo, ones_bd) + EPS)

    q_ref[...] = l2n(y[:, :GDN_W]) * GDN_DK ** -0.5
    k_ref[...] = l2n(y[:, GDN_W:2 * GDN_W])
    v_ref[...] = y[:, 2 * GDN_W:]
    ab = x_ref[:, w3 + GDN_W:]
    lane = lax.broadcasted_iota(jnp.int32, ab.shape, 1)
    is_g = (lane % 8) < 4
    z = ab + par_ref[1:2, :]
    softplus = jnp.maximum(z, 0.0) + jnp.log1p(jnp.exp(-jnp.abs(z)))
    g = jnp.where(is_g, par_ref[0:1, :] * softplus, 0.0)
    beta = 1.0 / (1.0 + jnp.exp(-ab))
    i = lax.broadcasted_iota(jnp.int32, (tm, tm), 0)
    j = lax.broadcasted_iota(jnp.int32, (tm, tm), 1)
    same_chunk = i // GDN_CHUNK == j // GDN_CHUNK
    tri_f = jnp.where(same_chunk & (j <= i), 1.0, 0.0).astype(BF16)
    tri_b = jnp.where(same_chunk & (j >= i), 1.0, 0.0).astype(BF16)
    pieces = _split3(g)
    gc_f = functools.reduce(jnp.add, [_dot(tri_f, p) for p in pieces])
    gc_b = functools.reduce(jnp.add, [_dot(tri_b, p) for p in pieces])
    gb_ref[...] = jnp.where(is_g, jnp.where(lane < 8, gc_f, gc_b), beta)


def gdn_prep(pb, conv_w, a_log, dt_bias):
    b, l, _ = pb.shape
    tm = ROW_TILE
    w3 = 3 * GDN_W
    halo_blocks = tm // GDN_HALO
    n_halo = l // GDN_HALO
    cw = jnp.pad(conv_w, ((0, SUBLANES - GDN_CONV_K), (0, 0)))
    neg_a = jnp.pad(-jnp.exp(a_log.astype(F32)), ((0, 0), (0, 4))).reshape(-1)
    dtb = jnp.pad(dt_bias.astype(F32), ((0, 0), (0, 4))).reshape(-1)
    par = jnp.pad(jnp.stack([neg_a, dtb]), ((0, SUBLANES - 2), (0, LANES - 16)))
    row = lambda i, t: (i, t, 0)
    out = lambda w: pl.BlockSpec((None, tm, w), row)
    return pl.pallas_call(
        _gdn_prep_kernel,
        grid=(b, l // tm),
        in_specs=[pl.BlockSpec((None, tm, B_W), row),
                  pl.BlockSpec((None, GDN_HALO, w3), lambda i, t: (i, jnp.maximum(t * halo_blocks - 1, 0), 0)),
                  pl.BlockSpec((None, GDN_HALO, w3),
                               lambda i, t: (i, jnp.minimum((t + 1) * halo_blocks, n_halo - 1), 0)),
                  pl.BlockSpec(cw.shape, lambda i, t: (0, 0)),
                  pl.BlockSpec(par.shape, lambda i, t: (0, 0))],
        out_specs=[out(GDN_W), out(GDN_W), out(GDN_W), out(LANES)],
        out_shape=[jax.ShapeDtypeStruct((b, l, GDN_W), F32)] * 3 + [jax.ShapeDtypeStruct((b, l, LANES), F32)],
        compiler_params=pltpu.CompilerParams(
            dimension_semantics=("parallel", "parallel"), vmem_limit_bytes=VMEM_LIMIT),
        name="gdn_prep",
    )(pb, pb, pb, cw, par)


def _tile_heads(x):
    return jnp.concatenate([x] * GDN_HEADS, axis=0)


def _collapse_heads(x):
    c = GDN_CHUNK
    return x[0:c] + x[c:2 * c] + x[2 * c:3 * c] + x[3 * c:4 * c]


def _gdn_chunk_kernel(q_ref, k_ref, v_ref, gb_ref, *out_refs):
    dr = pl.program_id(1)

    @pl.when(dr == 0)
    def _():
        _gdn_chunk_body(True, q_ref, k_ref, v_ref, gb_ref, *out_refs)

    @pl.when(dr == 1)
    def _():
        _gdn_chunk_body(False, q_ref, k_ref, v_ref, gb_ref, *out_refs)


def _gdn_chunk_body(fwd, q_ref, k_ref, v_ref, gb_ref, o0_ref, qe_ref, a_ref, bm_ref, gam_ref):
    n = GDN_W
    cs = GDN_CHUNK
    r = lax.broadcasted_iota(jnp.int32, (n, n), 0)
    c = lax.broadcasted_iota(jnp.int32, (n, n), 1)
    ri, ci = r % cs, c % cs
    head = r // cs == c // cs
    ahead = ri - ci if fwd else ci - ri
    tri = head & (ahead >= 0)
    tri_strict = head & (ahead > 0)
    eye = jnp.where(r == c, 1.0, 0.0)
    blk = lambda s: r // s == c // s
    b8, b16, b32 = blk(8), blk(16), blk(32)
    lane = lax.broadcasted_iota(jnp.int32, (n, LANES), 1)
    row_head = lax.broadcasted_iota(jnp.int32, (n, LANES), 0) // cs
    lane0 = 0 if fwd else 8
    sel_g = lane == lane0 + row_head
    sel_b = lane == lane0 + 4 + row_head
    pick = lambda sel, x: jnp.sum(jnp.where(sel, x, 0.0), axis=1, keepdims=True)

    rows = [slice(ch * cs, (ch + 1) * cs) for ch in range(q_ref.shape[0] // cs)]
    each = lambda f, *xs: [f(*a) for a in zip(*xs)]
    bf = lambda xs: [x.astype(BF16) for x in xs]
    spread = lambda ref: [jnp.where(head, _tile_heads(ref[rw, :]), 0.0) for rw in rows]
    kh, qh, vh = spread(k_ref), spread(q_ref), spread(v_ref)
    gb = [gb_ref[rw, :] for rw in rows]
    gb4 = [_tile_heads(x) for x in gb]
    gc = [pick(sel_g, x) for x in gb4]
    beta = [pick(sel_b, x) for x in gb4]
    end = cs - 1 if fwd else 0
    gl = [pick(sel_g, jnp.broadcast_to(x[end:end + 1, :], (n, LANES))) for x in gb]
    gc_b = [jnp.broadcast_to(x, (n, n)) for x in gc]
    decay = [jnp.exp(jnp.minimum(x - x.T, 0.0)) for x in gc_b]
    khb, qhb = bf(kh), bf(qh)
    kk = each(_dot_nt, khb, khb)
    qk = each(_dot_nt, qhb, khb)
    lmat = each(lambda b_, kk_, d_: jnp.where(tri_strict, b_ * kk_ * d_, 0.0), beta, kk, decay)
    attn = bf(each(lambda qk_, d_: jnp.where(tri, qk_ * d_, 0.0), qk, decay))
    nl = bf([jnp.where(b8, -x, 0.0) for x in lmat])
    n2 = bf(each(_dot, nl, nl))
    n4 = each(_dot, n2, n2)
    p1 = bf(each(lambda a, b_: _dot((eye + a).astype(BF16), (eye + b_).astype(BF16)), nl, n2))
    tinv = each(lambda p, x: _dot(p, (eye + x).astype(BF16)), p1, n4)
    def moving_rows(x, sz):
        return jnp.concatenate([x[i:i + sz] for i in range(sz if fwd else 0, n, 2 * sz)], axis=0)

    def with_moving_rows(x, new, sz):
        pieces = []
        for j, i in enumerate(range(0, n, 2 * sz)):
            kept = x[i:i + sz] if fwd else x[i + sz:i + 2 * sz]
            moved = new[j * sz:(j + 1) * sz]
            pieces += [kept, moved] if fwd else [moved, kept]
        return jnp.concatenate(pieces, axis=0)

    for sz, inner, outer in ((8, b8, b16), (16, b16, b32), (32, b32, head)):
        off = bf([jnp.where(outer & ~inner, x, 0.0) for x in lmat])
        tb = bf(tinv)
        t_mv = [moving_rows(x, sz) for x in tinv]
        to = bf(each(_dot, bf(t_mv), off))
        tinv = each(lambda t_, tm_, to_, tb_: with_moving_rows(t_, tm_ - _dot(to_, tb_), sz), tinv, t_mv, to, tb)
    tb = bf(tinv)
    eg = [jnp.exp(x) for x in gc]
    u = bf(each(lambda t_, b_, v_: _dot(t_, (b_ * v_).astype(BF16)), tb, beta, vh))
    w = bf(each(lambda t_, b_, e_, k_: _dot(t_, ((b_ * e_) * k_).astype(BF16)), tb, beta, eg, kh))
    o0 = each(_dot, attn, u)
    qe = each(lambda q_, e_, a_, w_: q_ * e_ - _dot(a_, w_), qh, eg, attn, w)
    kg = bf(each(lambda k_, gl_, gc_: k_ * jnp.exp(gl_ - gc_), kh, gl, gc))
    a_mat = each(_dot_tn, kg, w)
    b_mat = each(_dot_tn, kg, u)
    for rw, o0_, qe_, a_, b_, gl_ in zip(rows, o0, qe, a_mat, b_mat, gl):
        o0_ref[rw, :] = _collapse_heads(o0_)
        qe_ref[rw, :] = _collapse_heads(qe_)
        a_ref[rw, :] = _collapse_heads(a_)
        bm_ref[rw, :] = _collapse_heads(b_)
        gam_ref[rw, :] = _collapse_heads(jnp.where(head, jnp.broadcast_to(jnp.exp(gl_), (n, n)), 0.0))


def gdn_chunks(q, k, v, gb):
    b, l, _ = q.shape
    tm = ROW_TILE
    row = lambda i, dr, t: (i, t, 0)
    out = pl.BlockSpec((None, None, tm, GDN_W), lambda i, dr, t: (dr, i, t, 0))
    return pl.pallas_call(
        _gdn_chunk_kernel,
        grid=(b, 2, l // tm),
        in_specs=[pl.BlockSpec((None, tm, GDN_W), row)] * 3 + [pl.BlockSpec((None, tm, LANES), row)],
        out_specs=[out] * 5,
        out_shape=[jax.ShapeDtypeStruct((2, b, l, GDN_W), F32)] * 5,
        compiler_params=pltpu.CompilerParams(
            dimension_semantics=("parallel", "parallel", "parallel"), vmem_limit_bytes=VMEM_LIMIT),
        name="gdn_chunk",
    )(q, k, v, gb)


def _gdn_scan_kernel(o0_ref, qe_ref, a_ref, bm_ref, gam_ref, gate_ref, g_ref, o_ref, s_ref, part_ref):
    dr = pl.program_id(0)
    s = pl.program_id(1)
    ns = pl.num_programs(1)
    nb, tm = o0_ref.shape[0], o0_ref.shape[1]
    cs = GDN_CHUNK
    n = GDN_W
    nch = tm // cs
    bs = range(nb)
    tile = jnp.where(s == 0, 0, jnp.where(dr == 0, s, ns - s))
    r = lax.broadcasted_iota(jnp.int32, (n, n), 0)
    c = lax.broadcasted_iota(jnp.int32, (n, n), 1)
    head = r // cs == c // cs

    @pl.when(s == 0)
    def _():
        s_ref[...] = jnp.zeros_like(s_ref)

    def run(order):
        state = [s_ref[i] for i in bs]
        outs = [{} for _ in bs]
        for ch in order:
            rows = slice(ch * cs, (ch + 1) * cs)
            sb = [x.astype(BF16) for x in state]
            for i in bs:
                outs[i][ch] = o0_ref[i, rows, :] + _dot(qe_ref[i, rows, :].astype(BF16), sb[i])
            a_full = [jnp.where(head, _tile_heads(a_ref[i, rows, :]), 0.0).astype(BF16) for i in bs]
            state = [_tile_heads(gam_ref[i, rows, :]) * state[i] - _dot(a_full[i], sb[i])
                     + jnp.where(head, _tile_heads(bm_ref[i, rows, :]), 0.0) for i in bs]
        for i in bs:
            s_ref[i] = state[i]
        return [jnp.concatenate([outs[i][ch] for ch in range(nch)], axis=0) for i in bs]

    rows_out = pl.ds(pl.multiple_of(tile * tm, tm), tm)

    @pl.when(dr == 0)
    def _():
        for i, o in enumerate(run(range(nch))):
            part_ref[i, rows_out, :] = o

    @pl.when(dr == 1)
    def _():
        ones_bd = jnp.where(head, 1.0, 0.0).astype(BF16)
        for i, o_bwd in enumerate(run(range(nch - 1, -1, -1))):
            o = part_ref[i, rows_out, :] + o_bwd
            ms = _head_mean(o * o, ones_bd)
            o_ref[i] = (o * lax.rsqrt(ms + EPS) * g_ref[...] * _silu(gate_ref[i])).astype(o_ref.dtype)


def gdn_scan(o0, qe, a, bm, gam, pb, norm_g):
    _, b, l, _ = o0.shape
    tm = ROW_TILE
    ns = l // tm

    def tile_of(dr, s):
        return jnp.where(s == 0, 0, jnp.where(dr == 0, s, ns - s))

    per_dir = pl.BlockSpec((None, b, tm, GDN_W), lambda dr, s: (dr, 0, tile_of(dr, s), 0))
    gate_col = 3 * GDN_W // GDN_W
    return pl.pallas_call(
        _gdn_scan_kernel,
        grid=(2, ns),
        in_specs=[per_dir] * 5 + [
            pl.BlockSpec((b, tm, GDN_W), lambda dr, s: (0, tile_of(dr, s), gate_col)),
            pl.BlockSpec((1, GDN_W), lambda dr, s: (0, 0))],
        out_specs=pl.BlockSpec((b, tm, GDN_W), lambda dr, s: (0, jnp.where(dr == 0, 0, tile_of(dr, s)), 0)),
        out_shape=jax.ShapeDtypeStruct((b, l, GDN_W), BF16),
        scratch_shapes=[pltpu.VMEM((b, GDN_W, GDN_W), F32), pltpu.VMEM((b, l, GDN_W), F32)],
        compiler_params=pltpu.CompilerParams(
            dimension_semantics=("arbitrary", "arbitrary"), vmem_limit_bytes=VMEM_LIMIT),
        name="gdn_scan",
    )(o0, qe, a, bm, gam, pb, jnp.tile(norm_g, GDN_HEADS)[None, :])


def gdn_mixer(pb, conv_w, a_log, dt_bias, norm_g):
    q, k, v, gb = gdn_prep(pb, conv_w, a_log, dt_bias)
    o0, qe, a, bm, gam = gdn_chunks(q, k, v, gb)
    return gdn_scan(o0, qe, a, bm, gam, pb, norm_g)


def rms_norm(x, g):
    xf = x.astype(F32)
    y = xf * lax.rsqrt(jnp.mean(xf * xf, axis=-1, keepdims=True) + EPS)
    return (y * g.astype(F32)).astype(x.dtype)


def l2_normalize(t):
    return t * lax.rsqrt(jnp.sum(t * t, axis=-1, keepdims=True) + EPS)


def directional_scan(chunked_fn, seqs, consts, s0, reverse):
    if reverse:
        seqs = tuple(jnp.flip(t, axis=1) for t in seqs)
    o, s = chunked_fn(*seqs, *consts, s0)
    if reverse:
        o = jnp.flip(o, axis=1)
    return o, s


def prefix_bidirectional_scan(chunked_fn, seqs_c, seqs_x, consts, state_shape):
    out_c, out_x = 0.0, 0.0
    for d in range(2):
        rev = d == 1
        s0 = jnp.zeros(state_shape, F32)
        oc, s_ctx = directional_scan(chunked_fn, seqs_c[d], consts[d], s0, rev)
        ox, _ = directional_scan(chunked_fn, seqs_x[d], consts[d], s_ctx, rev)
        out_c = out_c + oc
        out_x = out_x + ox
    return out_c, out_x


def short_conv(x, w):
    k = w.shape[0]
    y = lax.conv_general_dilated(
        x, w[:, None, :].astype(x.dtype), window_strides=(1,), padding=[(k // 2, k // 2)],
        dimension_numbers=('NWC', 'WIO', 'NWC'), feature_group_count=x.shape[-1])
    return jax.nn.silu(y)


def gated_delta_chunked(q, k, v, g, beta, s0):
    b, l, h, _ = q.shape
    dv = v.shape[-1]
    cs = GDN_CHUNK
    nc = l // cs

    def chunks(t):
        return t.reshape(b, nc, cs, h, -1).transpose(1, 0, 3, 2, 4)

    qc, kc, vc = chunks(q), chunks(k), chunks(v)
    gc = jnp.cumsum(chunks(g[..., None])[..., 0], axis=-1)
    bc = chunks(beta[..., None])
    idx = jnp.arange(cs)
    lower = idx[:, None] >= idx[None, :]
    strict = idx[:, None] > idx[None, :]
    decay = jnp.exp(jnp.where(lower, gc[..., :, None] - gc[..., None, :], NEG_INF))
    kb = kc * bc
    lmat = jnp.where(strict, jnp.einsum('nbhid,nbhjd->nbhij', kb, kc) * decay, 0.0)
    a_mat = lmat + jnp.eye(cs, dtype=F32)
    u = lax.linalg.triangular_solve(a_mat, vc * bc, left_side=True, lower=True)
    w = lax.linalg.triangular_solve(a_mat, kb * jnp.exp(gc)[..., None], left_side=True, lower=True)
    attn = jnp.einsum('nbhid,nbhjd->nbhij', qc, kc) * decay

    def step(s, xs):
        q_i, k_i, u_i, w_i, a_i, g_i = xs
        v_new = u_i - jnp.einsum('bhck,bhkv->bhcv', w_i, s)
        o_i = (jnp.einsum('bhck,bhkv->bhcv', q_i * jnp.exp(g_i)[..., None], s)
               + jnp.einsum('bhij,bhjv->bhiv', a_i, v_new))
        g_last = g_i[..., -1:]
        s = (s * jnp.exp(g_last)[..., None]
             + jnp.einsum('bhck,bhcv->bhkv', k_i * jnp.exp(g_last - g_i)[..., None], v_new))
        return s, o_i

    s_fin, o = lax.scan(step, s0, (qc, kc, u, w, attn, gc))
    return o.transpose(1, 0, 3, 2, 4).reshape(b, l, h, dv), s_fin


def gdn_mixer_jax(pb, conv_w, a_log, dt_bias, norm_g):
    def prep(qkv, ab):
        b, l, _ = qkv.shape
        qkv = short_conv(qkv, conv_w).astype(F32)
        q, k, v = jnp.split(qkv, [GDN_HEADS * GDN_DK, 2 * GDN_HEADS * GDN_DK], axis=-1)
        q = l2_normalize(q.reshape(b, l, GDN_HEADS, GDN_DK)) * GDN_DK ** -0.5
        k = l2_normalize(k.reshape(b, l, GDN_HEADS, GDN_DK))
        v = v.reshape(b, l, GDN_HEADS, GDN_DV)
        ab = ab.astype(F32).reshape(b, l, 2, 2, GDN_HEADS)
        g = -jnp.exp(a_log.astype(F32)) * jax.nn.softplus(ab[:, :, :, 0] + dt_bias.astype(F32))
        beta = jax.nn.sigmoid(ab[:, :, :, 1])
        return [(q, k, v, g[:, :, d], beta[:, :, d]) for d in range(2)]

    def parts(rows):
        return rows[..., :768], rows[..., 768:1024], rows[..., 1024:1040]

    qkv_c, gate_c, ab_c = parts(pb[:, :CTX_LEN])
    qkv_x, gate_x, ab_x = parts(pb[:, CTX_LEN:])
    s_shape = (qkv_x.shape[0], GDN_HEADS, GDN_DK, GDN_DV)
    o_c, o_x = prefix_bidirectional_scan(gated_delta_chunked, prep(qkv_c, ab_c), prep(qkv_x, ab_x),
                                         [(), ()], s_shape)

    def gated_out(o, gate):
        b, l = gate.shape[:2]
        y = rms_norm(o, norm_g) * jax.nn.silu(gate.astype(F32)).reshape(o.shape)
        return y.reshape(b, l, GDN_HEADS * GDN_DV).astype(BF16)

    return jnp.concatenate([gated_out(o_c, gate_c), gated_out(o_x, gate_x)], axis=1)


def kernel(x, c, ctx, c_ctx, w_mod, b_mod, norm1, norm2, w_in, w_out, swa_sink, gdn_conv, gdn_a_log, gdn_dt_bias, gdn_norm, mla_q_norm, mla_kv_norm, mla_w_q_up, mla_w_kv_up, ret_log_decay, ret_norm, ffn_w_gate, ffn_w_up, ffn_w_down, moe_router, moe_w_gate, moe_w_up, moe_w_down, final_norm):
    b, n, d = x.shape
    depth = w_in.shape[0]
    cos_t, sin_t = rope_tables(n)
    cos_c, sin_c = cos_t[:, A_ROT_W:A_ROT_W + C_ROT_W], sin_t[:, A_ROT_W:A_ROT_W + C_ROT_W]
    silu_c = jax.nn.silu(c)
    silu_cc = jax.nn.silu(c_ctx)
    h = jnp.concatenate([ctx, x], axis=1)
    moe_wg, moe_wu, moe_wd = moe_w_gate.astype(BF16), moe_w_up.astype(BF16), moe_w_down.astype(BF16)
    for layer in range(depth):
        mod_x = jnp.dot(silu_c, w_mod[layer], precision=lax.Precision.HIGHEST) + b_mod[layer]
        mod_c = jnp.dot(silu_cc, w_mod[layer], precision=lax.Precision.HIGHEST) + b_mod[layer]
        mod = jnp.stack([jnp.broadcast_to(mod_c, mod_x.shape), mod_x], axis=1).reshape(b, 2, 6, d)
        mod = jnp.pad(mod, ((0, 0), (0, 0), (0, SUBLANES - 6), (0, 0)))
        pa, pb, pc, pd = norm_proj(h, mod, norm1[layer][None, :], build_in_weight(w_in[layer]), cos_t, sin_t)
        mix_a = swa_mixer(pa, swa_sink[layer])
        mix_b = gdn_mixer(pb, gdn_conv[layer], gdn_a_log[layer], gdn_dt_bias[layer], gdn_norm[layer])
        mq, mk, mv = mla_prep(pc, mla_q_norm[layer], mla_kv_norm[layer], mla_w_q_up[layer], mla_w_kv_up[layer],
                              cos_c, sin_c)
        mix_c = mla_attention(mq, mk, mv)
        mix_d = retention_mixer(pd, ret_log_decay[layer], ret_norm[layer])
        mixes = (mix_a, mix_b, mix_c, mix_d)
        w_o = build_out_weight(w_out[layer])
        i = layer // 2
        if layer % 2 == 0:
            h, v = out_proj(mixes, h, mod, norm2[layer][None, :], w_o)
            h = dense_ffn(v, h, mod, ffn_w_gate[i].astype(BF16), ffn_w_up[i].astype(BF16),
                          ffn_w_down[i].astype(BF16))
        else:
            w_r = jnp.pad(moe_router[i], ((0, 0), (0, LANES - N_EXPERTS)))
            h, v, logits = out_proj(mixes, h, mod, norm2[layer][None, :], w_o, w_r)
            h = moe_ffn(v, logits, h, mod, i, moe_wg, moe_wu, moe_wd)
    return final_rms_norm(h, final_norm[None, :], CTX_LEN // ROW_TILE)
```

```python
import functools

import numpy as np
import jax
import jax.numpy as jnp
from jax import lax
from jax.experimental import pallas as pl
from jax.experimental.pallas import tpu as pltpu

D_MODEL = 1024
GRID_W = 64
CTX_LEN = 256
HEAD_DIM = 64
ROPE_THETA = 10000.0
EPS = 1e-6
NEG_INF = -1e30

SWA_WINDOW = 128
GDN_HEADS = 4
GDN_DK = 64
GDN_DV = 64
GDN_CHUNK = 64
MLA_HEADS = 4
MLA_NOPE = 64
MLA_ROPE = 32
MLA_V = 64
RET_HEADS = 4
RET_DK = 64
D_FF = 3584
N_EXPERTS = 8
TOP_K = 2

LANES = 128
SUBLANES = 8
VMEM_LIMIT = 56 * 1024 * 1024

ROW_TILE = 256
FF_CHUNK = 512
MOE_TILE = 512
MOE_FF_CHUNK = 1792

A_W, B_W, C_W, D_W = 768, 1152, 512, 1024
A_ROT_W, C_ROT_W, D_ROT_W = 640, 128, 512
OFF_A = 0
OFF_B = OFF_A + A_W
OFF_C = OFF_B + B_W
OFF_D = OFF_C + C_W
OFF_AR = OFF_D + D_W
OFF_CR = OFF_AR + A_ROT_W
OFF_DR = OFF_CR + C_ROT_W
W_ALL = OFF_DR + D_ROT_W
ROPE_W = A_ROT_W + C_ROT_W + D_ROT_W

F32 = jnp.float32
BF16 = jnp.bfloat16
NT_DIMS = (((1,), (1,)), ((), ()))
TN_DIMS = (((0,), (0,)), ((), ()))


def _rms(x):
    return x * lax.rsqrt(jnp.mean(x * x, axis=-1, keepdims=True) + EPS)


def _silu(x):
    return x * (1.0 / (1.0 + jnp.exp(-x)))


def _dot(a, b):
    return jnp.dot(a, b, preferred_element_type=F32)


def _dot_nt(a, b):
    return lax.dot_general(a, b, NT_DIMS, preferred_element_type=F32)


def _dot_tn(a, b):
    return lax.dot_general(a, b, TN_DIMS, preferred_element_type=F32)


def _mod_spec(d):
    return pl.BlockSpec((None, None, SUBLANES, d), lambda i, t: (i, jnp.minimum(t, 1), 0, 0))


WIDE = 2


def _sub_tile_specs(block, tiles_per_seq, index_of):
    def spec(k):
        return pl.BlockSpec(block, lambda j: index_of((WIDE * j + k) // tiles_per_seq, (WIDE * j + k) % tiles_per_seq))
    return [spec(k) for k in range(WIDE)]


def _sub_mod_specs(d, tiles_per_seq):
    return _sub_tile_specs((None, None, SUBLANES, d), tiles_per_seq, lambda b, t: (b, jnp.minimum(t, 1), 0, 0))


def _sub_rows(k):
    return slice(k * ROW_TILE, (k + 1) * ROW_TILE)


def _norm_proj_kernel(h_ref, *refs):
    mods, (g_ref, w_ref), tabs = refs[:WIDE], refs[WIDE:WIDE + 2], refs[WIDE + 2:3 * WIDE + 2]
    a_ref, b_ref, c_ref, d_ref = refs[3 * WIDE + 2:]
    y = _rms(h_ref[...]) * g_ref[...]
    u = jnp.concatenate([y[_sub_rows(k)] * (1.0 + m[1:2, :]) + m[0:1, :] for k, m in enumerate(mods)],
                        axis=0).astype(BF16)

    def mm(lo, width):
        return _dot(u, w_ref[:, lo:lo + width])

    a_main = mm(OFF_A, A_W)
    a_rot = mm(OFF_AR, A_ROT_W)
    b_ref[...] = mm(OFF_B, B_W)
    c_main = mm(OFF_C, C_W)
    c_rot = mm(OFF_CR, C_ROT_W)
    d_main = mm(OFF_D, D_W)
    d_rot = mm(OFF_DR, D_ROT_W)
    lo, hi = A_ROT_W, A_ROT_W + C_ROT_W
    a_ref[:, A_ROT_W:] = a_main[:, A_ROT_W:].astype(BF16)
    c_ref[:, :C_W - C_ROT_W] = c_main[:, :C_W - C_ROT_W]
    d_ref[:, D_ROT_W:] = d_main[:, D_ROT_W:]
    for k in range(WIDE):
        cos_ref, sin_ref = tabs[2 * k], tabs[2 * k + 1]
        r = _sub_rows(k)
        a_ref[r, :A_ROT_W] = (a_main[r, :A_ROT_W] * cos_ref[:, :A_ROT_W] + a_rot[r] * sin_ref[:, :A_ROT_W]).astype(BF16)
        c_ref[r, C_W - C_ROT_W:] = c_main[r, C_W - C_ROT_W:] * cos_ref[:, lo:hi] + c_rot[r] * sin_ref[:, lo:hi]
        d_ref[r, :D_ROT_W] = d_main[r, :D_ROT_W] * cos_ref[:, hi:] + d_rot[r] * sin_ref[:, hi:]


def _layer_weight_spec(w, layer):
    return pl.BlockSpec((None,) + w.shape[1:], lambda j: (layer, 0, 0), pipeline_mode=pl.Buffered(1))


def norm_proj(h, mod, gain, w, layer, cos_t, sin_t):
    b, l, d = h.shape
    tm = WIDE * ROW_TILE
    tps = l // ROW_TILE
    row = lambda j: (j, 0)
    const = lambda j: (0, 0)
    tabs = _sub_tile_specs((ROW_TILE, ROPE_W), tps, lambda bi, t: (t, 0))
    tab_specs = [s for pair in zip(tabs, _sub_tile_specs((ROW_TILE, ROPE_W), tps, lambda bi, t: (t, 0))) for s in pair]
    outs = pl.pallas_call(
        _norm_proj_kernel,
        grid=(b * l // tm,),
        in_specs=[pl.BlockSpec((tm, d), row)] + _sub_mod_specs(d, tps) + [
            pl.BlockSpec((1, d), const),
            _layer_weight_spec(w, layer)] + tab_specs,
        out_specs=[pl.BlockSpec((tm, A_W), row), pl.BlockSpec((tm, B_W), row),
                   pl.BlockSpec((tm, C_W), row), pl.BlockSpec((tm, D_W), row)],
        out_shape=[jax.ShapeDtypeStruct((b * l, A_W), BF16), jax.ShapeDtypeStruct((b * l, B_W), F32),
                   jax.ShapeDtypeStruct((b * l, C_W), F32), jax.ShapeDtypeStruct((b * l, D_W), F32)],
        compiler_params=pltpu.CompilerParams(dimension_semantics=("parallel",), vmem_limit_bytes=VMEM_LIMIT),
        name="norm_proj",
    )(h.reshape(b * l, d), *([mod] * WIDE), gain, w, *([cos_t, sin_t] * WIDE))
    return [o.reshape(b, l, -1) for o in outs]


def _rot_cols(w, hd):
    x = w.reshape(w.shape[:-1] + (w.shape[-1] // hd, 4, hd // 4))
    x1, x2, x3, x4 = x[..., 0, :], x[..., 1, :], x[..., 2, :], x[..., 3, :]
    return jnp.stack([-x2, x1, -x4, x3], axis=-2).reshape(w.shape)


def _place_swa_q(q):
    z = jnp.zeros(q.shape[:-1] + (HEAD_DIM,), q.dtype)
    blocks = []
    for h in range(4):
        qh = q[..., HEAD_DIM * h:HEAD_DIM * (h + 1)]
        blocks += [qh, z] if h // 2 == 0 else [z, qh]
    return jnp.concatenate(blocks, axis=-1)


def build_in_weight(w):
    o = [int(v) for v in np.cumsum((256, 128, 128, 768, 256, 16, 256, 128, 32, 256, 256, 256, 256))]
    aq, ak, av = w[..., :o[0]] * HEAD_DIM ** -0.5, w[..., o[0]:o[1]], w[..., o[1]:o[2]]
    b_main, b_ab = w[..., o[2]:o[4]], w[..., o[4]:o[5]]
    c_q, c_kv, c_kr = w[..., o[5]:o[6]], w[..., o[6]:o[7]], w[..., o[7]:o[8]]
    dq, dk, dvg = w[..., o[8]:o[9]], w[..., o[9]:o[10]] * RET_DK ** -0.5, w[..., o[10]:]
    z = lambda n: jnp.zeros(w.shape[:-1] + (n,), w.dtype)
    parts = [
        _place_swa_q(aq), ak, av,
        b_main, b_ab, z(LANES - b_ab.shape[-1]),
        c_q, c_kv, z(64), c_kr, z(32),
        dq, dk, dvg,
        _place_swa_q(_rot_cols(aq, HEAD_DIM)), _rot_cols(ak, HEAD_DIM),
        z(64), _rot_cols(c_kr, MLA_ROPE), z(32),
        _rot_cols(dq, HEAD_DIM), _rot_cols(dk, HEAD_DIM),
    ]
    out = jnp.concatenate(parts, axis=-1)
    assert out.shape[-1] == W_ALL
    return out.astype(BF16)


def rope_tables(n):
    lat = jnp.arange(CTX_LEN + n, dtype=jnp.int32) - CTX_LEN
    grid_row = jnp.where(lat >= 0, lat // GRID_W, 0).astype(F32)[:, None]
    grid_col = jnp.where(lat >= 0, lat % GRID_W, 0).astype(F32)[:, None]
    narrow, col_of = [], {}
    for rot_dim in (HEAD_DIM, MLA_ROPE):
        n_freq = rot_dim // 4
        inv_freq = ROPE_THETA ** (-jnp.arange(n_freq, dtype=F32) / n_freq)
        for axis, pos in enumerate((grid_row, grid_col)):
            col_of[rot_dim, axis] = sum(a.shape[1] for a in narrow)
            narrow.append(pos * inv_freq)
    ang = jnp.concatenate(narrow, axis=1)
    identity_col = ang.shape[1]
    sel = np.zeros((identity_col + 1, ROPE_W), np.float32)

    def plan(lane0, width, group, rot_lo, rot_dim):
        for c in range(width):
            j = c % group - rot_lo
            if 0 <= j < rot_dim:
                quarter, f = divmod(j, rot_dim // 4)
                sel[col_of[rot_dim, quarter // 2] + f, lane0 + c] = 1.0
            else:
                sel[identity_col, lane0 + c] = 1.0

    plan(0, A_ROT_W, HEAD_DIM, 0, HEAD_DIM)
    plan(A_ROT_W, C_ROT_W, C_ROT_W, 64, MLA_ROPE)
    plan(A_ROT_W + C_ROT_W, D_ROT_W, HEAD_DIM, 0, HEAD_DIM)
    spread = lambda t: jnp.dot(t, jnp.asarray(sel), precision=lax.Precision.HIGHEST)
    ones, zeros = jnp.ones_like(grid_row), jnp.zeros_like(grid_row)
    return (spread(jnp.concatenate([jnp.cos(ang), ones], axis=1)),
            spread(jnp.concatenate([jnp.sin(ang), zeros], axis=1)))


def _swa_kernel(sink_ref, q_ref, kp_ref, ko_ref, kn_ref, kc_ref, vp_ref, vo_ref, vn_ref, vc_ref, o_ref):
    t = pl.program_id(1)
    last = pl.num_programs(1) - 1
    tq = q_ref.shape[0]
    half = tq // 2

    def head_out(h, pieces):
        q = q_ref[:, LANES * h:LANES * (h + 1)]
        ss = []
        for k_ref, _, mask in pieces:
            s = _dot_nt(q, k_ref[...])
            ss.append(s if mask is None else jnp.where(mask, s, NEG_INF))
        sink = sink_ref[h]
        m = jnp.maximum(functools.reduce(jnp.maximum, [s.max(axis=-1, keepdims=True) for s in ss]), sink)
        ps = [jnp.exp(s - m) for s in ss]
        denom = functools.reduce(jnp.add, [p.sum(axis=-1, keepdims=True) for p in ps]) + jnp.exp(sink - m)
        o = functools.reduce(jnp.add, [_dot(p.astype(BF16), piece[1][...]) for p, piece in zip(ps, pieces)])
        return o / denom

    def write(pieces):
        outs = [head_out(h, pieces) for h in range(4)]
        lane = lax.broadcasted_iota(jnp.int32, (tq, LANES), 1)
        for r in range(2):
            o_ref[:, LANES * r:LANES * (r + 1)] = jnp.where(lane < HEAD_DIM, outs[r], outs[2 + r]).astype(o_ref.dtype)

    @pl.when(t == 0)
    def _():
        write([(kc_ref, vc_ref, None)])

    @pl.when(t > 0)
    def _():
        qi = lax.broadcasted_iota(jnp.int32, (tq, half), 0)
        kj = lax.broadcasted_iota(jnp.int32, (tq, half), 1)
        mask_prev = (kj >= qi) & (t > 1)
        mask_next = (kj <= qi - half) & (t < last)
        qo = lax.broadcasted_iota(jnp.int32, (tq, tq), 0)
        ko = lax.broadcasted_iota(jnp.int32, (tq, tq), 1)
        mask_own = jnp.abs(qo - ko) <= SWA_WINDOW
        write([(kp_ref, vp_ref, mask_prev), (ko_ref, vo_ref, mask_own), (kn_ref, vn_ref, mask_next),
               (kc_ref, vc_ref, None)])


def swa_mixer(pa, sink):
    b, l, _ = pa.shape
    tq = ROW_TILE
    nblk = l // SWA_WINDOW
    kcol, vcol = 4, 5
    prev = lambda c: (lambda i, t: (i, jnp.maximum(2 * t - 1, 2), c))
    nxt = lambda c: (lambda i, t: (i, jnp.minimum(2 * t + 2, nblk - 1), c))
    own = lambda c: (lambda i, t: (i, t, c))
    ctx = lambda c: (lambda i, t: (i, 0, c))
    kv_specs = lambda c: [pl.BlockSpec((None, SWA_WINDOW, LANES), prev(c)), pl.BlockSpec((None, tq, LANES), own(c)),
                          pl.BlockSpec((None, SWA_WINDOW, LANES), nxt(c)), pl.BlockSpec((None, tq, LANES), ctx(c))]
    return pl.pallas_call(
        _swa_kernel,
        grid=(b, l // tq),
        in_specs=[pl.BlockSpec(memory_space=pltpu.SMEM),
                  pl.BlockSpec((None, tq, 4 * LANES), lambda i, t: (i, t, 0))] + kv_specs(kcol) + kv_specs(vcol),
        out_specs=pl.BlockSpec((None, tq, 2 * LANES), lambda i, t: (i, t, 0)),
        out_shape=jax.ShapeDtypeStruct((b, l, 2 * LANES), BF16),
        compiler_params=pltpu.CompilerParams(
            dimension_semantics=("parallel", "parallel"), vmem_limit_bytes=VMEM_LIMIT),
        name="swa",
    )(sink, pa, pa, pa, pa, pa, pa, pa, pa, pa)


def _mla_prep_kernel(c_ref, qn_ref, kvn_ref, wq_ref, wqr_ref, wk_ref, wv_ref, cos_ref, sin_ref,
                     q_ref, k_ref, v_ref):
    cq = c_ref[:, 0:256]
    ckv = c_ref[:, 256:384]
    kr = c_ref[:, 384:512]
    nq = (_rms(cq) * qn_ref[...]).astype(BF16)
    nkv = (_rms(ckv) * kvn_ref[...]).astype(BF16)
    cos = jnp.concatenate([cos_ref[...]] * MLA_HEADS, axis=1)
    sin = jnp.concatenate([sin_ref[...]] * MLA_HEADS, axis=1)
    q_ref[...] = (_dot(nq, wq_ref[...]) * cos + _dot(nq, wqr_ref[...]) * sin).astype(BF16)
    k_ref[...] = (_dot(nkv, wk_ref[...]) + jnp.concatenate([kr] * MLA_HEADS, axis=1)).astype(BF16)
    v_ref[...] = _dot(nkv, wv_ref[...]).astype(BF16)


def mla_prep(pc, q_norm, kv_norm, w_q_up, w_kv_up, cos_c, sin_c):
    b, l, _ = pc.shape
    tm = ROW_TILE
    scale = (MLA_NOPE + MLA_ROPE) ** -0.5 * float(np.log2(np.e))
    wq = (w_q_up * scale).reshape(-1, MLA_HEADS, MLA_NOPE + MLA_ROPE)
    zq = jnp.zeros(wq.shape[:2] + (LANES - MLA_NOPE - MLA_ROPE,), F32)
    wq_main = jnp.concatenate([wq, zq], axis=-1).reshape(-1, MLA_HEADS * LANES)
    wq_rot = jnp.concatenate([jnp.zeros_like(wq[..., :MLA_NOPE]), _rot_cols(wq[..., MLA_NOPE:], MLA_ROPE), zq],
                             axis=-1).reshape(-1, MLA_HEADS * LANES)
    wkv = w_kv_up.reshape(-1, MLA_HEADS, MLA_NOPE + MLA_V)
    wk = jnp.concatenate([wkv[..., :MLA_NOPE], jnp.zeros_like(wkv[..., :LANES - MLA_NOPE])],
                         axis=-1).reshape(-1, MLA_HEADS * LANES)
    wv = wkv[..., MLA_NOPE:].reshape(-1, MLA_HEADS * MLA_V)
    row = lambda i, t: (i, t, 0)
    const = lambda i, t: (0, 0)
    full = lambda a: pl.BlockSpec(a.shape, const)
    args = [q_norm[None, :], kv_norm[None, :], wq_main.astype(BF16), wq_rot.astype(BF16), wk.astype(BF16),
            wv.astype(BF16)]
    return pl.pallas_call(
        _mla_prep_kernel,
        grid=(b, l // tm),
        in_specs=[pl.BlockSpec((None, tm, C_W), row)] + [full(a) for a in args]
        + [pl.BlockSpec((tm, LANES), lambda i, t: (t, 0)), pl.BlockSpec((tm, LANES), lambda i, t: (t, 0))],
        out_specs=[pl.BlockSpec((None, tm, 4 * LANES), row), pl.BlockSpec((None, tm, 4 * LANES), row),
                   pl.BlockSpec((None, tm, 2 * LANES), row)],
        out_shape=[jax.ShapeDtypeStruct((b, l, 4 * LANES), BF16), jax.ShapeDtypeStruct((b, l, 4 * LANES), BF16),
                   jax.ShapeDtypeStruct((b, l, 2 * LANES), BF16)],
        compiler_params=pltpu.CompilerParams(dimension_semantics=("parallel", "parallel")),
        name="mla_prep",
    )(pc, *args, cos_c, sin_c)


def _mla_attn_kernel(q_ref, k_ref, v_ref, o_ref):
    t = pl.program_id(2)
    tq = q_ref.shape[0]

    def attend(nk):
        v = v_ref[0:nk, :]
        v_lane = lax.broadcasted_iota(jnp.int32, v.shape, 1)
        outs = []
        for j in range(2):
            q = q_ref[:, LANES * j:LANES * (j + 1)]
            k = k_ref[0:nk, LANES * j:LANES * (j + 1)]
            s = _dot_nt(q, k)
            p = jnp.exp2(s - s.max(axis=-1, keepdims=True)).astype(BF16)
            other = (v_lane >= MLA_V) if j == 0 else (v_lane < MLA_V)
            o = _dot(p, jnp.where(other, jnp.ones_like(v), v))
            den_lane = MLA_V if j == 0 else 0
            outs.append(o / o[:, den_lane:den_lane + 1])
        lane = lax.broadcasted_iota(jnp.int32, (tq, LANES), 1)
        o_ref[...] = jnp.where(lane < MLA_V, outs[0], outs[1]).astype(o_ref.dtype)

    @pl.when(t == 0)
    def _():
        attend(CTX_LEN)

    @pl.when(t > 0)
    def _():
        attend(k_ref.shape[0])


def mla_attention(q, k, v):
    b, l, _ = q.shape
    tq = ROW_TILE
    return pl.pallas_call(
        _mla_attn_kernel,
        grid=(b, 2, l // tq),
        in_specs=[pl.BlockSpec((None, tq, 2 * LANES), lambda i, p, t: (i, t, p)),
                  pl.BlockSpec((None, l, 2 * LANES), lambda i, p, t: (i, 0, p)),
                  pl.BlockSpec((None, l, LANES), lambda i, p, t: (i, 0, p))],
        out_specs=pl.BlockSpec((None, tq, LANES), lambda i, p, t: (i, t, p)),
        out_shape=jax.ShapeDtypeStruct((b, l, 2 * LANES), BF16),
        compiler_params=pltpu.CompilerParams(
            dimension_semantics=("parallel", "parallel", "parallel"), vmem_limit_bytes=VMEM_LIMIT),
        name="mla_attn",
    )(q, k, v)


def _head_mean(x, ones_bd):
    hi = x.astype(BF16)
    lo = (x - hi.astype(F32)).astype(BF16)
    return (_dot(hi, ones_bd) + _dot(lo, ones_bd)) * (1.0 / HEAD_DIM)


def _ret_kernel(x_ref, lg_ref, g_ref, o_ref, s_ref, dec_ref, part_ref):
    dr = pl.program_id(0)
    s = pl.program_id(1)
    ns = pl.num_programs(1)
    nb, c = x_ref.shape[0], x_ref.shape[1]
    w = RET_HEADS * HEAD_DIM
    chunk = jnp.where(s == 0, 0, jnp.where(dr == 0, s, ns - s))
    lg = lg_ref[...]
    fwd = dr == 0
    row_h = lax.broadcasted_iota(jnp.int32, (w, w), 0) // HEAD_DIM
    col_h = lax.broadcasted_iota(jnp.int32, (w, w), 1) // HEAD_DIM
    same_head = row_h == col_h

    @pl.when(s == 0)
    def _():
        s_ref[...] = jnp.zeros_like(s_ref)
        i = lax.broadcasted_iota(jnp.int32, (c, c), 0)
        j = lax.broadcasted_iota(jnp.int32, (c, c), 1)
        rel = jnp.where(fwd, i - j, j - i)
        relf = jnp.maximum(rel, 0).astype(F32)
        for h in range(RET_HEADS):
            lg_h = lg_ref[0:1, HEAD_DIM * h:HEAD_DIM * h + 1]
            dec_ref[h] = jnp.where(rel >= 0, jnp.exp(lg_h * relf), 0.0)

    pos = lax.broadcasted_iota(jnp.int32, (c, 1), 0).astype(F32)
    q_dec = jnp.exp(lg * jnp.where(fwd, pos + 1.0, c - pos))
    k_dec = jnp.exp(lg * jnp.where(fwd, c - 1.0 - pos, pos))
    lane_h = lax.broadcasted_iota(jnp.int32, (c, w), 1) // HEAD_DIM
    bs = range(nb)
    q = [x_ref[i, :, 0:w] for i in bs]
    kf = [x_ref[i, :, w:2 * w] for i in bs]
    v = [x_ref[i, :, 2 * w:3 * w].astype(BF16) for i in bs]
    kb = [x.astype(BF16) for x in kf]
    acc = [_dot((q[i] * q_dec).astype(BF16), s_ref[i].astype(BF16)) for i in bs]
    for h in range(RET_HEADS):
        qh = [jnp.where(lane_h == h, q[i], 0.0).astype(BF16) for i in bs]
        a = [(_dot_nt(qh[i], kb[i]) * dec_ref[h]).astype(BF16) for i in bs]
        acc = [acc[i] + jnp.where(lane_h == h, _dot(a[i], v[i]), 0.0) for i in bs]
    kv = [_dot_tn((kf[i] * k_dec).astype(BF16), v[i]) for i in bs]
    chunk_dec = jnp.exp(lg * float(c))
    for i in bs:
        s_ref[i] = s_ref[i] * chunk_dec + jnp.where(same_head, kv[i], 0.0)

    rows = pl.ds(pl.multiple_of(chunk * c, c), c)

    @pl.when(dr == 0)
    def _():
        for i in bs:
            part_ref[i, rows, :] = acc[i]

    @pl.when(dr == 1)
    def _():
        ones_bd = jnp.where(same_head, 1.0, 0.0).astype(BF16)
        for i in bs:
            o = part_ref[i, rows, :] + acc[i]
            mu = _head_mean(o, ones_bd)
            var = _head_mean(jnp.square(o - mu), ones_bd)
            y = (o - mu) * lax.rsqrt(var + EPS) * g_ref[...]
            o_ref[i] = (y * _silu(x_ref[i, :, 3 * w:4 * w])).astype(o_ref.dtype)


def retention_mixer(pd, log_decay, norm_g):
    b, l, _ = pd.shape
    c = ROW_TILE
    ns = l // c
    w = RET_HEADS * HEAD_DIM
    lg = jnp.repeat(-jnp.exp(log_decay.astype(F32)), HEAD_DIM, axis=-1)[:, None, :]

    def chunk_of(dr, s):
        return jnp.where(s == 0, 0, jnp.where(dr == 0, s, ns - s))

    return pl.pallas_call(
        _ret_kernel,
        grid=(2, ns),
        in_specs=[pl.BlockSpec((b, c, D_W), lambda dr, s: (0, chunk_of(dr, s), 0)),
                  pl.BlockSpec((None, 1, w), lambda dr, s: (dr, 0, 0)),
                  pl.BlockSpec((1, w), lambda dr, s: (0, 0))],
        out_specs=pl.BlockSpec((b, c, w), lambda dr, s: (0, jnp.where(dr == 0, 0, chunk_of(dr, s)), 0)),
        out_shape=jax.ShapeDtypeStruct((b, l, w), BF16),
        scratch_shapes=[pltpu.VMEM((b, w, w), F32), pltpu.VMEM((RET_HEADS, c, c), F32), pltpu.VMEM((b, l, w), F32)],
        compiler_params=pltpu.CompilerParams(
            dimension_semantics=("arbitrary", "arbitrary"), vmem_limit_bytes=VMEM_LIMIT),
        name="retention",
    )(pd, lg, norm_g[None, :])


def _out_proj_kernel(ma_ref, mb_ref, mc_ref, md_ref, h_ref, *refs, with_router):
    mods, (g_ref, w_ref), rest = refs[:WIDE], refs[WIDE:WIDE + 2], refs[WIDE + 2:]
    if with_router:
        wr_ref, hn_ref, v_ref, lg_ref = rest
    else:
        hn_ref, v_ref = rest
    gw = 2 * LANES
    mix = functools.reduce(jnp.add, [
        _dot(m_ref[...].astype(BF16), w_ref[gw * i:gw * (i + 1), :])
        for i, m_ref in enumerate((ma_ref, mb_ref, mc_ref, md_ref))])
    for k, mod_ref in enumerate(mods):
        r = _sub_rows(k)
        hn = h_ref[r, :] + mod_ref[2:3, :] * mix[r]
        hn_ref[r, :] = hn
        v = _rms(hn) * g_ref[...] * (1.0 + mod_ref[4:5, :]) + mod_ref[3:4, :]
        v_ref[r, :] = v.astype(v_ref.dtype)
        if with_router:
            lg_ref[r, :] = jnp.dot(v, wr_ref[...], preferred_element_type=F32, precision=lax.Precision.HIGHEST)


def out_proj(mixes, h, mod, gain, w, layer, w_router=None):
    b, l, d = h.shape
    tm = WIDE * ROW_TILE
    n = b * l
    with_router = w_router is not None
    row = lambda j: (j, 0)
    const = lambda j: (0, 0)
    in_specs = [pl.BlockSpec((tm, 2 * LANES), row) for _ in mixes] + [pl.BlockSpec((tm, d), row)] + _sub_mod_specs(
        d, l // ROW_TILE) + [pl.BlockSpec((1, d), const), _layer_weight_spec(w, layer)]
    out_specs = [pl.BlockSpec((tm, d), row), pl.BlockSpec((tm, d), row)]
    out_shape = [jax.ShapeDtypeStruct((n, d), F32), jax.ShapeDtypeStruct((n, d), F32 if with_router else BF16)]
    args = [m.reshape(n, 2 * LANES) for m in mixes] + [h.reshape(n, d)] + [mod] * WIDE + [gain, w]
    if with_router:
        in_specs.append(pl.BlockSpec(w_router.shape, const))
        out_specs.append(pl.BlockSpec((tm, LANES), row))
        out_shape.append(jax.ShapeDtypeStruct((n, LANES), F32))
        args.append(w_router)
    outs = pl.pallas_call(
        functools.partial(_out_proj_kernel, with_router=with_router),
        grid=(n // tm,),
        in_specs=in_specs,
        out_specs=out_specs,
        out_shape=out_shape,
        compiler_params=pltpu.CompilerParams(dimension_semantics=("parallel",), vmem_limit_bytes=VMEM_LIMIT),
        name="out_proj",
    )(*args)
    return [o.reshape(b, l, -1) for o in outs]


def build_out_weight(w):
    hd = HEAD_DIM
    rows = lambda lo, hi: w[..., lo:hi, :]
    return jnp.concatenate([rows(0, hd), rows(2 * hd, 3 * hd), rows(hd, 2 * hd), rows(3 * hd, None)],
                           axis=-2).astype(BF16)


def _ffn_kernel(v_ref, h_ref, *refs):
    mods, (wg_ref, wu_ref, wd_ref, o_ref) = refs[:WIDE], refs[WIDE:]
    v = v_ref[...]
    acc = jnp.zeros(o_ref.shape, F32)
    for j in range(D_FF // FF_CHUNK):
        cols = slice(j * FF_CHUNK, (j + 1) * FF_CHUNK)
        a = _dot(v, wg_ref[:, cols])
        u = _dot(v, wu_ref[:, cols])
        mid = (_silu(a) * u).astype(BF16)
        acc = acc + _dot(mid, wd_ref[cols, :])
    for k, mod_ref in enumerate(mods):
        r = _sub_rows(k)
        o_ref[r, :] = h_ref[r, :] + mod_ref[5:6, :] * acc[r]


def dense_ffn(v, h, mod, layer, wg, wu, wd):
    b, l, d = h.shape
    tm = WIDE * ROW_TILE
    n = b * l
    row = lambda j: (j, 0)
    const = lambda j: (0, 0)
    return pl.pallas_call(
        _ffn_kernel,
        grid=(n // tm,),
        in_specs=[pl.BlockSpec((tm, d), row), pl.BlockSpec((tm, d), row)] + _sub_mod_specs(d, l // ROW_TILE) + [
            _layer_weight_spec(wg, layer), _layer_weight_spec(wu, layer), _layer_weight_spec(wd, layer)],
        out_specs=pl.BlockSpec((tm, d), row),
        out_shape=jax.ShapeDtypeStruct((n, d), F32),
        compiler_params=pltpu.CompilerParams(dimension_semantics=("parallel",), vmem_limit_bytes=VMEM_LIMIT),
        name="dense_ffn",
    )(v.reshape(n, d), h.reshape(n, d), *([mod] * WIDE), wg, wu, wd).reshape(b, l, d)


def _moe_kernel(wt_ref, we_ref, lo_ref, hi_ref, first_ref, x_ref, wg_ref, wu_ref, wd_ref, o_ref, xm_ref, acc_ref):
    w = pl.program_id(0)
    j = pl.program_id(1)
    nj = pl.num_programs(1)
    tm = x_ref.shape[0]

    @pl.when(j == 0)
    def _():
        row = wt_ref[w] * tm + lax.broadcasted_iota(jnp.int32, (tm, 1), 0)
        keep = (row >= lo_ref[w]) & (row < hi_ref[w])
        xm_ref[...] = jnp.where(keep, x_ref[...], 0.0).astype(BF16)

    @pl.when((j == 0) & (first_ref[w] > 0))
    def _():
        acc_ref[...] = jnp.zeros_like(acc_ref)

    def swiglu_rows(rows):
        x = xm_ref[rows, :]
        a = _dot(x, wg_ref[...])
        u = _dot(x, wu_ref[...])
        mid = (_silu(a) * u).astype(BF16)
        acc_ref[rows, :] += _dot(mid, wd_ref[...])

    tile_lo = wt_ref[w] * tm
    whole = (lo_ref[w] <= tile_lo) & (hi_ref[w] >= tile_lo + tm)

    @pl.when(whole)
    def _():
        swiglu_rows(slice(0, tm))

    half = tm // 2
    for part in range(2):
        part_lo = tile_lo + part * half

        @pl.when(jnp.logical_not(whole) & (hi_ref[w] > jnp.maximum(part_lo, lo_ref[w])) & (lo_ref[w] < part_lo + half))
        def _():
            swiglu_rows(slice(part * half, (part + 1) * half))

    @pl.when(j == nj - 1)
    def _():
        o_ref[...] = acc_ref[...]


def moe_grouped_ffn(xs, items, layer_idx, wg, wu, wd):
    s, d = xs.shape
    tm = MOE_TILE
    fc = MOE_FF_CHUNK
    nw = items[0].shape[0]
    nj = D_FF // fc
    grid_spec = pltpu.PrefetchScalarGridSpec(
        num_scalar_prefetch=5,
        grid=(nw, nj),
        in_specs=[
            pl.BlockSpec((tm, d), lambda w, j, wt, we, lo, hi, fi: (wt[w], 0)),
            pl.BlockSpec((None, None, d, fc), lambda w, j, wt, we, lo, hi, fi: (layer_idx, we[w], 0, j)),
            pl.BlockSpec((None, None, d, fc), lambda w, j, wt, we, lo, hi, fi: (layer_idx, we[w], 0, j)),
            pl.BlockSpec((None, None, fc, d), lambda w, j, wt, we, lo, hi, fi: (layer_idx, we[w], j, 0)),
        ],
        out_specs=pl.BlockSpec((tm, d), lambda w, j, wt, we, lo, hi, fi: (wt[w], 0)),
        scratch_shapes=[pltpu.VMEM((tm, d), BF16), pltpu.VMEM((tm, d), F32)],
    )
    return pl.pallas_call(
        _moe_kernel,
        grid_spec=grid_spec,
        out_shape=jax.ShapeDtypeStruct((s, d), F32),
        compiler_params=pltpu.CompilerParams(
            dimension_semantics=("arbitrary", "arbitrary"), vmem_limit_bytes=VMEM_LIMIT),
        name="moe_ffn",
    )(*items, xs, wg, wu, wd)


def _residual_kernel(h_ref, f0_ref, f1_ref, gate_ref, mod_ref, o_ref):
    f = gate_ref[:, 0:1] * f0_ref[...] + gate_ref[:, 1:2] * f1_ref[...]
    o_ref[...] = h_ref[...] + mod_ref[5:6, :] * f


def gated_residual(h, f0, f1, gates, mod):
    b, l, d = h.shape
    tm = ROW_TILE
    row = lambda i, t: (i, t, 0)
    return pl.pallas_call(
        _residual_kernel,
        grid=(b, l // tm),
        in_specs=[pl.BlockSpec((None, tm, d), row), pl.BlockSpec((None, tm, d), row), pl.BlockSpec((None, tm, d), row),
                  pl.BlockSpec((None, tm, LANES), row), _mod_spec(d)],
        out_specs=pl.BlockSpec((None, tm, d), row),
        out_shape=jax.ShapeDtypeStruct((b, l, d), F32),
        compiler_params=pltpu.CompilerParams(dimension_semantics=("parallel", "parallel")),
        name="gated_residual",
    )(h, f0, f1, gates, mod)


def moe_ffn(v, logits, h, mod, layer_idx, wg, wu, wd):
    b, l, d = h.shape
    t = b * l
    s = TOP_K * t
    tm = MOE_TILE
    nt = s // tm
    nw = nt + N_EXPERTS - 1
    i32 = jnp.int32
    lg = logits.reshape(t, LANES)[:, :N_EXPERTS]
    top_val, top_idx = lax.top_k(lg, TOP_K)
    gates = jax.nn.softmax(top_val, axis=-1)
    slot = jnp.arange(s, dtype=i32)
    skey = jnp.sort(top_idx.reshape(-1).astype(i32) * s + slot)
    order = skey % s
    _, inv = lax.sort_key_val(order, slot)
    bounds = (jnp.arange(N_EXPERTS, dtype=i32) + 1) * s
    cum = jnp.sum((skey[None, :] < bounds[:, None]).astype(i32), axis=1)
    cum_prev = jnp.concatenate([jnp.zeros((1,), i32), cum[:-1]])
    tile_lo = jnp.arange(nt, dtype=i32) * tm
    count_le = lambda edges, x: jnp.sum((edges[None, :] <= x[:, None]).astype(i32), axis=1)
    e_first = count_le(cum, tile_lo)
    e_last = count_le(cum, tile_lo + tm - 1)
    n_items = e_last - e_first + 1
    item_end = jnp.cumsum(n_items)
    item_start = item_end - n_items
    w = jnp.arange(nw, dtype=i32)
    wt = jnp.minimum(count_le(item_end, w), nt - 1)
    valid = w < item_end[-1]
    we = jnp.clip(e_first[wt] + w - item_start[wt], 0, N_EXPERTS - 1).astype(i32)
    lo = jnp.where(valid, cum_prev[we], 0).astype(i32)
    hi = jnp.where(valid, cum[we], 0).astype(i32)
    first = (valid & (w == item_start[wt])).astype(i32)
    rows_of = lambda a, idx: a.at[idx].get(mode="promise_in_bounds")
    xs = rows_of(v.reshape(t, d), order // TOP_K)
    ys = moe_grouped_ffn(xs, (wt, we, lo, hi, first), layer_idx, wg, wu, wd)
    dest = inv.reshape(t, TOP_K)
    f0 = rows_of(ys, dest[:, 0]).reshape(b, l, d)
    f1 = rows_of(ys, dest[:, 1]).reshape(b, l, d)
    gates_p = jnp.pad(gates, ((0, 0), (0, LANES - TOP_K))).reshape(b, l, LANES)
    return gated_residual(h, f0, f1, gates_p, mod)


def _final_norm_kernel(h_ref, g_ref, o_ref):
    o_ref[...] = _rms(h_ref[...]) * g_ref[...]


def final_rms_norm(h, gain, n_ctx_tiles):
    b, l, d = h.shape
    tm = ROW_TILE
    n = l - n_ctx_tiles * tm
    return pl.pallas_call(
        _final_norm_kernel,
        grid=(b, n // tm),
        in_specs=[
            pl.BlockSpec((None, tm, d), lambda i, t: (i, t + n_ctx_tiles, 0)),
            pl.BlockSpec((1, d), lambda i, t: (0, 0)),
        ],
        out_specs=pl.BlockSpec((None, tm, d), lambda i, t: (i, t, 0)),
        out_shape=jax.ShapeDtypeStruct((b, n, d), F32),
        compiler_params=pltpu.CompilerParams(dimension_semantics=("parallel", "parallel")),
        name="final_norm",
    )(h, gain)


GDN_W = GDN_HEADS * GDN_DK
GDN_CONV_K = 5
GDN_HALO = SUBLANES


def _split3(x):
    p0 = x.astype(BF16)
    r1 = x - p0.astype(F32)
    p1 = r1.astype(BF16)
    p2 = (r1 - p1.astype(F32)).astype(BF16)
    return p0, p1, p2


def _gdn_prep_kernel(x_ref, prev_ref, next_ref, cw_ref, par_ref, q_ref, k_ref, v_ref, gb_ref):
    t = pl.program_id(1)
    last = pl.num_programs(1) - 1
    tm = x_ref.shape[0]
    w3 = 3 * GDN_W
    has_prev = t > 1
    has_next = (t > 0) & (t < last)
    prev = jnp.where(has_prev, prev_ref[...], 0.0)
    nxt = jnp.where(has_next, next_ref[...], 0.0)
    xe = jnp.concatenate([prev, x_ref[:, :w3], nxt], axis=0)
    y = jnp.zeros((tm, w3), F32)
    for j in range(GDN_CONV_K):
        lo = GDN_HALO - GDN_CONV_K // 2 + j
        y = y + cw_ref[j:j + 1, :] * xe[lo:lo + tm, :]
    y = _silu(y)
    r = lax.broadcasted_iota(jnp.int32, (GDN_W, GDN_W), 0)
    c = lax.broadcasted_iota(jnp.int32, (GDN_W, GDN_W), 1)
    ones_bd = jnp.where(r // GDN_DK == c // GDN_DK, 1.0, 0.0).astype(BF16)

    def l2n(x):
        sq = x * x
        hi = sq.astype(BF16)
        lo = (sq - hi.astype(F32)).astype(BF16)
        return x * lax.rsqrt(_dot(hi, ones_bd) + _dot(lo, ones_bd) + EPS)

    q_ref[...] = l2n(y[:, :GDN_W]) * GDN_DK ** -0.5
    k_ref[...] = l2n(y[:, GDN_W:2 * GDN_W])
    v_ref[...] = y[:, 2 * GDN_W:]
    ab = x_ref[:, w3 + GDN_W:]
    lane = lax.broadcasted_iota(jnp.int32, ab.shape, 1)
    is_g = (lane % 8) < 4
    z = ab + par_ref[1:2, :]
    softplus = jnp.maximum(z, 0.0) + jnp.log1p(jnp.exp(-jnp.abs(z)))
    g = jnp.where(is_g, par_ref[0:1, :] * softplus, 0.0)
    beta = 1.0 / (1.0 + jnp.exp(-ab))
    i = lax.broadcasted_iota(jnp.int32, (tm, tm), 0)
    j = lax.broadcasted_iota(jnp.int32, (tm, tm), 1)
    same_chunk = i // GDN_CHUNK == j // GDN_CHUNK
    tri_f = jnp.where(same_chunk & (j <= i), 1.0, 0.0).astype(BF16)
    tri_b = jnp.where(same_chunk & (j >= i), 1.0, 0.0).astype(BF16)
    pieces = _split3(g)
    gc_f = functools.reduce(jnp.add, [_dot(tri_f, p) for p in pieces])
    gc_b = functools.reduce(jnp.add, [_dot(tri_b, p) for p in pieces])
    gb_ref[...] = jnp.where(is_g, jnp.where(lane < 8, gc_f, gc_b), beta)


def gdn_prep(pb, conv_w, a_log, dt_bias):
    b, l, _ = pb.shape
    tm = ROW_TILE
    w3 = 3 * GDN_W
    halo_blocks = tm // GDN_HALO
    n_halo = l // GDN_HALO
    cw = jnp.pad(conv_w, ((0, SUBLANES - GDN_CONV_K), (0, 0)))
    neg_a = jnp.pad(-jnp.exp(a_log.astype(F32)), ((0, 0), (0, 4))).reshape(-1)
    dtb = jnp.pad(dt_bias.astype(F32), ((0, 0), (0, 4))).reshape(-1)
    par = jnp.pad(jnp.stack([neg_a, dtb]), ((0, SUBLANES - 2), (0, LANES - 16)))
    row = lambda i, t: (i, t, 0)
    out = lambda w: pl.BlockSpec((None, tm, w), row)
    return pl.pallas_call(
        _gdn_prep_kernel,
        grid=(b, l // tm),
        in_specs=[pl.BlockSpec((None, tm, B_W), row),
                  pl.BlockSpec((None, GDN_HALO, w3), lambda i, t: (i, jnp.maximum(t * halo_blocks - 1, 0), 0)),
                  pl.BlockSpec((None, GDN_HALO, w3),
                               lambda i, t: (i, jnp.minimum((t + 1) * halo_blocks, n_halo - 1), 0)),
                  pl.BlockSpec(cw.shape, lambda i, t: (0, 0)),
                  pl.BlockSpec(par.shape, lambda i, t: (0, 0))],
        out_specs=[out(GDN_W), out(GDN_W), out(GDN_W), out(LANES)],
        out_shape=[jax.ShapeDtypeStruct((b, l, GDN_W), F32)] * 3 + [jax.ShapeDtypeStruct((b, l, LANES), F32)],
        compiler_params=pltpu.CompilerParams(
            dimension_semantics=("parallel", "parallel"), vmem_limit_bytes=VMEM_LIMIT),
        name="gdn_prep",
    )(pb, pb, pb, cw, par)


def _tile_heads(x):
    return jnp.concatenate([x] * GDN_HEADS, axis=0)


def _collapse_heads(x):
    c = GDN_CHUNK
    return x[0:c] + x[c:2 * c] + x[2 * c:3 * c] + x[3 * c:4 * c]


def _gdn_chunk_kernel(q_ref, k_ref, v_ref, gb_ref, o0_ref, qe_ref, a_ref, bm_ref, gam_ref):
    n = GDN_W
    cs = GDN_CHUNK
    r = lax.broadcasted_iota(jnp.int32, (n, n), 0)
    c = lax.broadcasted_iota(jnp.int32, (n, n), 1)
    ri, ci = r % cs, c % cs
    head = r // cs == c // cs
    eye = jnp.where(r == c, 1.0, 0.0)
    blk = lambda s: r // s == c // s
    b8, b16, b32 = blk(8), blk(16), blk(32)
    lane = lax.broadcasted_iota(jnp.int32, (n, LANES), 1)
    row_head = lax.broadcasted_iota(jnp.int32, (n, LANES), 0) // cs
    pick = lambda sel, x: jnp.sum(jnp.where(sel, x, 0.0), axis=1, keepdims=True)
    tri, tri_strict, sel_g, sel_b = {}, {}, {}, {}
    for fwd in (True, False):
        ahead = ri - ci if fwd else ci - ri
        tri[fwd] = head & (ahead >= 0)
        tri_strict[fwd] = head & (ahead > 0)
        lane0 = 0 if fwd else 8
        sel_g[fwd] = lane == lane0 + row_head
        sel_b[fwd] = lane == lane0 + 4 + row_head

    n_chunks = q_ref.shape[0] // cs
    rows = [slice(ch * cs, (ch + 1) * cs) for ch in range(n_chunks)]
    items = [(fwd, ch) for fwd in (True, False) for ch in range(n_chunks)]
    dirs = [fwd for fwd, _ in items]
    per_item = lambda xs: [xs[ch] for _, ch in items]
    each = lambda f, *xs: [f(*a) for a in zip(*xs)]
    bf = lambda xs: [x.astype(BF16) for x in xs]
    spread = lambda ref: [jnp.where(head, _tile_heads(ref[rw, :]), 0.0) for rw in rows]
    kh_c, qh_c, vh_c = spread(k_ref), spread(q_ref), spread(v_ref)
    khb_c, qhb_c = bf(kh_c), bf(qh_c)
    kk = per_item(each(_dot_nt, khb_c, khb_c))
    qk = per_item(each(_dot_nt, qhb_c, khb_c))
    kh, qh, vh = per_item(kh_c), per_item(qh_c), per_item(vh_c)
    gb = per_item([gb_ref[rw, :] for rw in rows])
    gb4 = [_tile_heads(x) for x in gb]
    gc = [pick(sel_g[f], x) for f, x in zip(dirs, gb4)]
    beta = [pick(sel_b[f], x) for f, x in zip(dirs, gb4)]
    ends = [cs - 1 if f else 0 for f in dirs]
    gl = [pick(sel_g[f], jnp.broadcast_to(x[e:e + 1, :], (n, LANES))) for f, x, e in zip(dirs, gb, ends)]
    gc_b = [jnp.broadcast_to(x, (n, n)) for x in gc]
    decay = [jnp.exp(jnp.minimum(x - x.T, 0.0)) for x in gc_b]
    lmat = [jnp.where(tri_strict[f], b_ * kk_ * d_, 0.0) for f, b_, kk_, d_ in zip(dirs, beta, kk, decay)]
    attn = bf([jnp.where(tri[f], qk_ * d_, 0.0) for f, qk_, d_ in zip(dirs, qk, decay)])
    nl = bf([jnp.where(b8, -x, 0.0) for x in lmat])
    n2 = bf(each(_dot, nl, nl))
    n4 = each(_dot, n2, n2)
    p1 = bf(each(lambda a, b_: _dot((eye + a).astype(BF16), (eye + b_).astype(BF16)), nl, n2))
    tinv = each(lambda p, x: _dot(p, (eye + x).astype(BF16)), p1, n4)

    def moving_rows(fwd, x, sz):
        return jnp.concatenate([x[i:i + sz] for i in range(sz if fwd else 0, n, 2 * sz)], axis=0)

    def with_moving_rows(fwd, x, new, sz):
        pieces = []
        for j, i in enumerate(range(0, n, 2 * sz)):
            kept = x[i:i + sz] if fwd else x[i + sz:i + 2 * sz]
            moved = new[j * sz:(j + 1) * sz]
            pieces += [kept, moved] if fwd else [moved, kept]
        return jnp.concatenate(pieces, axis=0)

    for sz, inner, outer in ((8, b8, b16), (16, b16, b32), (32, b32, head)):
        off = bf([jnp.where(outer & ~inner, x, 0.0) for x in lmat])
        tb = bf(tinv)
        t_mv = [moving_rows(f, x, sz) for f, x in zip(dirs, tinv)]
        to = bf(each(_dot, bf(t_mv), off))
        tinv = [with_moving_rows(f, t_, tm_ - _dot(to_, tb_), sz)
                for f, t_, tm_, to_, tb_ in zip(dirs, tinv, t_mv, to, tb)]
    tb = bf(tinv)
    eg = [jnp.exp(x) for x in gc]
    u = bf(each(lambda t_, b_, v_: _dot(t_, (b_ * v_).astype(BF16)), tb, beta, vh))
    w = bf(each(lambda t_, b_, e_, k_: _dot(t_, ((b_ * e_) * k_).astype(BF16)), tb, beta, eg, kh))
    o0 = each(_dot, attn, u)
    qe = each(lambda q_, e_, a_, w_: q_ * e_ - _dot(a_, w_), qh, eg, attn, w)
    kg = bf(each(lambda k_, gl_, gc_: k_ * jnp.exp(gl_ - gc_), kh, gl, gc))
    a_mat = each(_dot_tn, kg, w)
    b_mat = each(_dot_tn, kg, u)
    for (fwd, ch), o0_, qe_, a_, b_, gl_ in zip(items, o0, qe, a_mat, b_mat, gl):
        dr, rw = 0 if fwd else 1, rows[ch]
        o0_ref[dr, rw, :] = _collapse_heads(o0_)
        qe_ref[dr, rw, :] = _collapse_heads(qe_)
        a_ref[dr, rw, :] = _collapse_heads(a_)
        bm_ref[dr, rw, :] = _collapse_heads(b_)
        gam_ref[dr, rw, :] = _collapse_heads(jnp.where(head, jnp.broadcast_to(jnp.exp(gl_), (n, n)), 0.0))


def gdn_chunks(q, k, v, gb):
    b, l, _ = q.shape
    tm = ROW_TILE
    row = lambda i, t: (i, t, 0)
    out = pl.BlockSpec((2, None, tm, GDN_W), lambda i, t: (0, i, t, 0))
    return pl.pallas_call(
        _gdn_chunk_kernel,
        grid=(b, l // tm),
        in_specs=[pl.BlockSpec((None, tm, GDN_W), row)] * 3 + [pl.BlockSpec((None, tm, LANES), row)],
        out_specs=[out] * 5,
        out_shape=[jax.ShapeDtypeStruct((2, b, l, GDN_W), F32)] * 5,
        compiler_params=pltpu.CompilerParams(
            dimension_semantics=("parallel", "parallel"), vmem_limit_bytes=VMEM_LIMIT),
        name="gdn_chunk",
    )(q, k, v, gb)


def _gdn_scan_kernel(o0_ref, qe_ref, a_ref, bm_ref, gam_ref, gate_ref, g_ref, o_ref, s_ref, part_ref):
    dr = pl.program_id(0)
    s = pl.program_id(1)
    ns = pl.num_programs(1)
    nb, tm = o0_ref.shape[0], o0_ref.shape[1]
    cs = GDN_CHUNK
    n = GDN_W
    nch = tm // cs
    bs = range(nb)
    tile = jnp.where(s == 0, 0, jnp.where(dr == 0, s, ns - s))
    r = lax.broadcasted_iota(jnp.int32, (n, n), 0)
    c = lax.broadcasted_iota(jnp.int32, (n, n), 1)
    head = r // cs == c // cs

    @pl.when(s == 0)
    def _():
        s_ref[...] = jnp.zeros_like(s_ref)

    def run(order):
        state = [s_ref[i] for i in bs]
        outs = [{} for _ in bs]
        for ch in order:
            rows = slice(ch * cs, (ch + 1) * cs)
            sb = [x.astype(BF16) for x in state]
            for i in bs:
                outs[i][ch] = o0_ref[i, rows, :] + _dot(qe_ref[i, rows, :].astype(BF16), sb[i])
            a_full = [jnp.where(head, _tile_heads(a_ref[i, rows, :]), 0.0).astype(BF16) for i in bs]
            state = [_tile_heads(gam_ref[i, rows, :]) * state[i] - _dot(a_full[i], sb[i])
                     + jnp.where(head, _tile_heads(bm_ref[i, rows, :]), 0.0) for i in bs]
        for i in bs:
            s_ref[i] = state[i]
        return [jnp.concatenate([outs[i][ch] for ch in range(nch)], axis=0) for i in bs]

    rows_out = pl.ds(pl.multiple_of(tile * tm, tm), tm)

    @pl.when(dr == 0)
    def _():
        for i, o in enumerate(run(range(nch))):
            part_ref[i, rows_out, :] = o

    @pl.when(dr == 1)
    def _():
        ones_bd = jnp.where(head, 1.0, 0.0).astype(BF16)
        for i, o_bwd in enumerate(run(range(nch - 1, -1, -1))):
            o = part_ref[i, rows_out, :] + o_bwd
            ms = _head_mean(o * o, ones_bd)
            o_ref[i] = (o * lax.rsqrt(ms + EPS) * g_ref[...] * _silu(gate_ref[i])).astype(o_ref.dtype)


def gdn_scan(o0, qe, a, bm, gam, pb, norm_g):
    _, b, l, _ = o0.shape
    tm = ROW_TILE
    ns = l // tm

    def tile_of(dr, s):
        return jnp.where(s == 0, 0, jnp.where(dr == 0, s, ns - s))

    per_dir = pl.BlockSpec((None, b, tm, GDN_W), lambda dr, s: (dr, 0, tile_of(dr, s), 0))
    gate_col = 3 * GDN_W // GDN_W
    return pl.pallas_call(
        _gdn_scan_kernel,
        grid=(2, ns),
        in_specs=[per_dir] * 5 + [
            pl.BlockSpec((b, tm, GDN_W), lambda dr, s: (0, tile_of(dr, s), gate_col)),
            pl.BlockSpec((1, GDN_W), lambda dr, s: (0, 0))],
        out_specs=pl.BlockSpec((b, tm, GDN_W), lambda dr, s: (0, jnp.where(dr == 0, 0, tile_of(dr, s)), 0)),
        out_shape=jax.ShapeDtypeStruct((b, l, GDN_W), BF16),
        scratch_shapes=[pltpu.VMEM((b, GDN_W, GDN_W), F32), pltpu.VMEM((b, l, GDN_W), F32)],
        compiler_params=pltpu.CompilerParams(
            dimension_semantics=("arbitrary", "arbitrary"), vmem_limit_bytes=VMEM_LIMIT),
        name="gdn_scan",
    )(o0, qe, a, bm, gam, pb, jnp.tile(norm_g, GDN_HEADS)[None, :])


def gdn_mixer(pb, conv_w, a_log, dt_bias, norm_g):
    q, k, v, gb = gdn_prep(pb, conv_w, a_log, dt_bias)
    o0, qe, a, bm, gam = gdn_chunks(q, k, v, gb)
    return gdn_scan(o0, qe, a, bm, gam, pb, norm_g)


def kernel(x, c, ctx, c_ctx, w_mod, b_mod, norm1, norm2, w_in, w_out, swa_sink, gdn_conv, gdn_a_log, gdn_dt_bias, gdn_norm, mla_q_norm, mla_kv_norm, mla_w_q_up, mla_w_kv_up, ret_log_decay, ret_norm, ffn_w_gate, ffn_w_up, ffn_w_down, moe_router, moe_w_gate, moe_w_up, moe_w_down, final_norm):
    b, n, d = x.shape
    depth = w_in.shape[0]
    cos_t, sin_t = rope_tables(n)
    cos_c, sin_c = cos_t[:, A_ROT_W:A_ROT_W + C_ROT_W], sin_t[:, A_ROT_W:A_ROT_W + C_ROT_W]
    h = jnp.concatenate([ctx, x], axis=1)
    cond = jnp.concatenate([jax.nn.silu(c_ctx)[None, :], jax.nn.silu(c)], axis=0)
    mods = jnp.einsum("bd,ldk->lbk", cond, w_mod, precision=lax.Precision.HIGHEST) + b_mod[:, None, :]
    mods = mods.reshape(depth, 1 + b, 6, d)
    mods = jnp.stack([jnp.broadcast_to(mods[:, :1], (depth, b, 6, d)), mods[:, 1:]], axis=2)
    mods = jnp.pad(mods, ((0, 0), (0, 0), (0, 0), (0, SUBLANES - 6), (0, 0)))
    w_in_all = build_in_weight(w_in)
    w_out_all = build_out_weight(w_out)
    ffn_wg, ffn_wu, ffn_wd = ffn_w_gate.astype(BF16), ffn_w_up.astype(BF16), ffn_w_down.astype(BF16)
    moe_wg, moe_wu, moe_wd = moe_w_gate.astype(BF16), moe_w_up.astype(BF16), moe_w_down.astype(BF16)
    for layer in range(depth):
        mod = mods[layer]
        pa, pb, pc, pd = norm_proj(h, mod, norm1[layer][None, :], w_in_all, layer, cos_t, sin_t)
        mix_a = swa_mixer(pa, swa_sink[layer])
        mix_b = gdn_mixer(pb, gdn_conv[layer], gdn_a_log[layer], gdn_dt_bias[layer], gdn_norm[layer])
        mq, mk, mv = mla_prep(pc, mla_q_norm[layer], mla_kv_norm[layer], mla_w_q_up[layer], mla_w_kv_up[layer],
                              cos_c, sin_c)
        mix_c = mla_attention(mq, mk, mv)
        mix_d = retention_mixer(pd, ret_log_decay[layer], ret_norm[layer])
        mixes = (mix_a, mix_b, mix_c, mix_d)
        i = layer // 2
        if layer % 2 == 0:
            h, v = out_proj(mixes, h, mod, norm2[layer][None, :], w_out_all, layer)
            h = dense_ffn(v, h, mod, i, ffn_wg, ffn_wu, ffn_wd)
        else:
            w_r = jnp.pad(moe_router[i], ((0, 0), (0, LANES - N_EXPERTS)))
            h, v, logits = out_proj(mixes, h, mod, norm2[layer][None, :], w_out_all, layer, w_r)
            h = moe_ffn(v, logits, h, mod, i, moe_wg, moe_wu, moe_wd)
    return final_rms_norm(h, final_norm[None, :], CTX_LEN // ROW_TILE)
```

```python
import functools

import numpy as np
import jax
import jax.numpy as jnp
from jax import lax
from jax.experimental import pallas as pl
from jax.experimental.pallas import tpu as pltpu

D_MODEL = 1024
GRID_W = 64
CTX_LEN = 256
HEAD_DIM = 64
ROPE_THETA = 10000.0
EPS = 1e-6
NEG_INF = -1e30

SWA_WINDOW = 128
GDN_HEADS = 4
GDN_DK = 64
GDN_DV = 64
GDN_CHUNK = 64
MLA_HEADS = 4
MLA_NOPE = 64
MLA_ROPE = 32
MLA_V = 64
RET_HEADS = 4
RET_DK = 64
D_FF = 3584
N_EXPERTS = 8
TOP_K = 2

LANES = 128
SUBLANES = 8
VMEM_LIMIT = 56 * 1024 * 1024

ROW_TILE = 256
FF_CHUNK = 512
MOE_TILE = 512
MOE_FF_CHUNK = 1792

A_W, B_W, C_W, D_W = 768, 1152, 512, 1024
A_ROT_W, C_ROT_W, D_ROT_W = 640, 128, 512
OFF_A = 0
OFF_B = OFF_A + A_W
OFF_C = OFF_B + B_W
OFF_D = OFF_C + C_W
OFF_AR = OFF_D + D_W
OFF_CR = OFF_AR + A_ROT_W
OFF_DR = OFF_CR + C_ROT_W
W_ALL = OFF_DR + D_ROT_W
ROPE_W = A_ROT_W + C_ROT_W + D_ROT_W

F32 = jnp.float32
BF16 = jnp.bfloat16
NT_DIMS = (((1,), (1,)), ((), ()))
TN_DIMS = (((0,), (0,)), ((), ()))


def _rms(x):
    return x * lax.rsqrt(jnp.mean(x * x, axis=-1, keepdims=True) + EPS)


def _silu(x):
    return x * (1.0 / (1.0 + jnp.exp(-x)))


def _dot(a, b):
    return jnp.dot(a, b, preferred_element_type=F32)


def _dot_nt(a, b):
    return lax.dot_general(a, b, NT_DIMS, preferred_element_type=F32)


def _dot_tn(a, b):
    return lax.dot_general(a, b, TN_DIMS, preferred_element_type=F32)


WIDE = 2


def _sub_tile_specs(block, tiles_per_seq, index_of):
    def spec(k):
        return pl.BlockSpec(block, lambda j: index_of((WIDE * j + k) // tiles_per_seq, (WIDE * j + k) % tiles_per_seq))
    return [spec(k) for k in range(WIDE)]


def _sub_mod_specs(d, tiles_per_seq):
    return _sub_tile_specs((None, None, SUBLANES, d), tiles_per_seq, lambda b, t: (b, jnp.minimum(t, 1), 0, 0))


def _sub_rows(k):
    return slice(k * ROW_TILE, (k + 1) * ROW_TILE)


def _norm_proj_kernel(h_ref, *refs):
    mods, (g_ref, w_ref), tabs = refs[:WIDE], refs[WIDE:WIDE + 2], refs[WIDE + 2:3 * WIDE + 2]
    a_ref, b_ref, c_ref, d_ref = refs[3 * WIDE + 2:]
    y = _rms(h_ref[...]) * g_ref[...]
    u = jnp.concatenate([y[_sub_rows(k)] * (1.0 + m[1:2, :]) + m[0:1, :] for k, m in enumerate(mods)],
                        axis=0).astype(BF16)

    def mm(lo, width):
        return _dot(u, w_ref[:, lo:lo + width])

    a_main = mm(OFF_A, A_W)
    a_rot = mm(OFF_AR, A_ROT_W)
    b_ref[...] = mm(OFF_B, B_W)
    c_main = mm(OFF_C, C_W)
    c_rot = mm(OFF_CR, C_ROT_W)
    d_main = mm(OFF_D, D_W)
    d_rot = mm(OFF_DR, D_ROT_W)
    lo, hi = A_ROT_W, A_ROT_W + C_ROT_W
    a_ref[:, A_ROT_W:] = a_main[:, A_ROT_W:].astype(BF16)
    c_ref[:, :C_W - C_ROT_W] = c_main[:, :C_W - C_ROT_W]
    d_ref[:, D_ROT_W:] = d_main[:, D_ROT_W:]
    for k in range(WIDE):
        cos_ref, sin_ref = tabs[2 * k], tabs[2 * k + 1]
        r = _sub_rows(k)
        a_ref[r, :A_ROT_W] = (a_main[r, :A_ROT_W] * cos_ref[:, :A_ROT_W] + a_rot[r] * sin_ref[:, :A_ROT_W]).astype(BF16)
        c_ref[r, C_W - C_ROT_W:] = c_main[r, C_W - C_ROT_W:] * cos_ref[:, lo:hi] + c_rot[r] * sin_ref[:, lo:hi]
        d_ref[r, :D_ROT_W] = d_main[r, :D_ROT_W] * cos_ref[:, hi:] + d_rot[r] * sin_ref[:, hi:]


def _layer_weight_spec(w, layer):
    return pl.BlockSpec((None,) + w.shape[1:], lambda j: (layer, 0, 0), pipeline_mode=pl.Buffered(1))


def norm_proj(h, mod, gain, w, layer, cos_t, sin_t):
    b, l, d = h.shape
    tm = WIDE * ROW_TILE
    tps = l // ROW_TILE
    row = lambda j: (j, 0)
    const = lambda j: (0, 0)
    tabs = _sub_tile_specs((ROW_TILE, ROPE_W), tps, lambda bi, t: (t, 0))
    tab_specs = [s for pair in zip(tabs, _sub_tile_specs((ROW_TILE, ROPE_W), tps, lambda bi, t: (t, 0))) for s in pair]
    outs = pl.pallas_call(
        _norm_proj_kernel,
        grid=(b * l // tm,),
        in_specs=[pl.BlockSpec((tm, d), row)] + _sub_mod_specs(d, tps) + [
            pl.BlockSpec((1, d), const),
            _layer_weight_spec(w, layer)] + tab_specs,
        out_specs=[pl.BlockSpec((tm, A_W), row), pl.BlockSpec((tm, B_W), row),
                   pl.BlockSpec((tm, C_W), row), pl.BlockSpec((tm, D_W), row)],
        out_shape=[jax.ShapeDtypeStruct((b * l, A_W), BF16), jax.ShapeDtypeStruct((b * l, B_W), F32),
                   jax.ShapeDtypeStruct((b * l, C_W), F32), jax.ShapeDtypeStruct((b * l, D_W), F32)],
        compiler_params=pltpu.CompilerParams(dimension_semantics=("parallel",), vmem_limit_bytes=VMEM_LIMIT),
        name="norm_proj",
    )(h.reshape(b * l, d), *([mod] * WIDE), gain, w, *([cos_t, sin_t] * WIDE))
    return [o.reshape(b, l, -1) for o in outs]


def _rot_cols(w, hd):
    x = w.reshape(w.shape[:-1] + (w.shape[-1] // hd, 4, hd // 4))
    x1, x2, x3, x4 = x[..., 0, :], x[..., 1, :], x[..., 2, :], x[..., 3, :]
    return jnp.stack([-x2, x1, -x4, x3], axis=-2).reshape(w.shape)


def _place_swa_q(q):
    z = jnp.zeros(q.shape[:-1] + (HEAD_DIM,), q.dtype)
    blocks = []
    for h in range(4):
        qh = q[..., HEAD_DIM * h:HEAD_DIM * (h + 1)]
        blocks += [qh, z] if h // 2 == 0 else [z, qh]
    return jnp.concatenate(blocks, axis=-1)


def build_in_weight(w):
    o = [int(v) for v in np.cumsum((256, 128, 128, 768, 256, 16, 256, 128, 32, 256, 256, 256, 256))]
    aq, ak, av = w[..., :o[0]] * HEAD_DIM ** -0.5, w[..., o[0]:o[1]], w[..., o[1]:o[2]]
    b_main, b_ab = w[..., o[2]:o[4]], w[..., o[4]:o[5]]
    c_q, c_kv, c_kr = w[..., o[5]:o[6]], w[..., o[6]:o[7]], w[..., o[7]:o[8]]
    dq, dk, dvg = w[..., o[8]:o[9]], w[..., o[9]:o[10]] * RET_DK ** -0.5, w[..., o[10]:]
    z = lambda n: jnp.zeros(w.shape[:-1] + (n,), w.dtype)
    parts = [
        _place_swa_q(aq), ak, av,
        b_main, b_ab, z(LANES - b_ab.shape[-1]),
        c_q, c_kv, z(64), c_kr, z(32),
        dq, dk, dvg,
        _place_swa_q(_rot_cols(aq, HEAD_DIM)), _rot_cols(ak, HEAD_DIM),
        z(64), _rot_cols(c_kr, MLA_ROPE), z(32),
        _rot_cols(dq, HEAD_DIM), _rot_cols(dk, HEAD_DIM),
    ]
    out = jnp.concatenate(parts, axis=-1)
    assert out.shape[-1] == W_ALL
    return out.astype(BF16)


def rope_tables(n):
    lat = jnp.arange(CTX_LEN + n, dtype=jnp.int32) - CTX_LEN
    grid_row = jnp.where(lat >= 0, lat // GRID_W, 0).astype(F32)[:, None]
    grid_col = jnp.where(lat >= 0, lat % GRID_W, 0).astype(F32)[:, None]
    narrow, col_of = [], {}
    for rot_dim in (HEAD_DIM, MLA_ROPE):
        n_freq = rot_dim // 4
        inv_freq = ROPE_THETA ** (-jnp.arange(n_freq, dtype=F32) / n_freq)
        for axis, pos in enumerate((grid_row, grid_col)):
            col_of[rot_dim, axis] = sum(a.shape[1] for a in narrow)
            narrow.append(pos * inv_freq)
    ang = jnp.concatenate(narrow, axis=1)
    identity_col = ang.shape[1]
    sel = np.zeros((identity_col + 1, ROPE_W), np.float32)

    def plan(lane0, width, group, rot_lo, rot_dim):
        for c in range(width):
            j = c % group - rot_lo
            if 0 <= j < rot_dim:
                quarter, f = divmod(j, rot_dim // 4)
                sel[col_of[rot_dim, quarter // 2] + f, lane0 + c] = 1.0
            else:
                sel[identity_col, lane0 + c] = 1.0

    plan(0, A_ROT_W, HEAD_DIM, 0, HEAD_DIM)
    plan(A_ROT_W, C_ROT_W, C_ROT_W, 64, MLA_ROPE)
    plan(A_ROT_W + C_ROT_W, D_ROT_W, HEAD_DIM, 0, HEAD_DIM)
    spread = lambda t: jnp.dot(t, jnp.asarray(sel), precision=lax.Precision.HIGHEST)
    ones, zeros = jnp.ones_like(grid_row), jnp.zeros_like(grid_row)
    return (spread(jnp.concatenate([jnp.cos(ang), ones], axis=1)),
            spread(jnp.concatenate([jnp.sin(ang), zeros], axis=1)))


def _swa_kernel(sink_ref, q_ref, kp_ref, ko_ref, kn_ref, kc_ref, vp_ref, vo_ref, vn_ref, vc_ref, o_ref):
    t = pl.program_id(1)
    last = pl.num_programs(1) - 1
    tq = q_ref.shape[0]
    half = tq // 2

    def head_out(h, pieces):
        q = q_ref[:, LANES * h:LANES * (h + 1)]
        ss = []
        for k_ref, _, mask in pieces:
            s = _dot_nt(q, k_ref[...])
            ss.append(s if mask is None else jnp.where(mask, s, NEG_INF))
        sink = sink_ref[h]
        m = jnp.maximum(functools.reduce(jnp.maximum, [s.max(axis=-1, keepdims=True) for s in ss]), sink)
        ps = [jnp.exp(s - m) for s in ss]
        denom = functools.reduce(jnp.add, [p.sum(axis=-1, keepdims=True) for p in ps]) + jnp.exp(sink - m)
        o = functools.reduce(jnp.add, [_dot(p.astype(BF16), piece[1][...]) for p, piece in zip(ps, pieces)])
        return o / denom

    def write(pieces):
        outs = [head_out(h, pieces) for h in range(4)]
        lane = lax.broadcasted_iota(jnp.int32, (tq, LANES), 1)
        for r in range(2):
            o_ref[:, LANES * r:LANES * (r + 1)] = jnp.where(lane < HEAD_DIM, outs[r], outs[2 + r]).astype(o_ref.dtype)

    @pl.when(t == 0)
    def _():
        write([(kc_ref, vc_ref, None)])

    @pl.when(t > 0)
    def _():
        qi = lax.broadcasted_iota(jnp.int32, (tq, half), 0)
        kj = lax.broadcasted_iota(jnp.int32, (tq, half), 1)
        mask_prev = (kj >= qi) & (t > 1)
        mask_next = (kj <= qi - half) & (t < last)
        qo = lax.broadcasted_iota(jnp.int32, (tq, tq), 0)
        ko = lax.broadcasted_iota(jnp.int32, (tq, tq), 1)
        mask_own = jnp.abs(qo - ko) <= SWA_WINDOW
        write([(kp_ref, vp_ref, mask_prev), (ko_ref, vo_ref, mask_own), (kn_ref, vn_ref, mask_next),
               (kc_ref, vc_ref, None)])


def swa_mixer(pa, sink):
    b, l, _ = pa.shape
    tq = ROW_TILE
    nblk = l // SWA_WINDOW
    kcol, vcol = 4, 5
    prev = lambda c: (lambda i, t: (i, jnp.maximum(2 * t - 1, 2), c))
    nxt = lambda c: (lambda i, t: (i, jnp.minimum(2 * t + 2, nblk - 1), c))
    own = lambda c: (lambda i, t: (i, t, c))
    ctx = lambda c: (lambda i, t: (i, 0, c))
    kv_specs = lambda c: [pl.BlockSpec((None, SWA_WINDOW, LANES), prev(c)), pl.BlockSpec((None, tq, LANES), own(c)),
                          pl.BlockSpec((None, SWA_WINDOW, LANES), nxt(c)), pl.BlockSpec((None, tq, LANES), ctx(c))]
    return pl.pallas_call(
        _swa_kernel,
        grid=(b, l // tq),
        in_specs=[pl.BlockSpec(memory_space=pltpu.SMEM),
                  pl.BlockSpec((None, tq, 4 * LANES), lambda i, t: (i, t, 0))] + kv_specs(kcol) + kv_specs(vcol),
        out_specs=pl.BlockSpec((None, tq, 2 * LANES), lambda i, t: (i, t, 0)),
        out_shape=jax.ShapeDtypeStruct((b, l, 2 * LANES), BF16),
        compiler_params=pltpu.CompilerParams(
            dimension_semantics=("parallel", "parallel"), vmem_limit_bytes=VMEM_LIMIT),
        name="swa",
    )(sink, pa, pa, pa, pa, pa, pa, pa, pa, pa)


def _mla_prep_kernel(c_ref, qn_ref, kvn_ref, wq_ref, wqr_ref, wk_ref, wv_ref, cos_ref, sin_ref,
                     q_ref, k_ref, v_ref):
    cq = c_ref[:, 0:256]
    ckv = c_ref[:, 256:384]
    kr = c_ref[:, 384:512]
    nq = (_rms(cq) * qn_ref[...]).astype(BF16)
    nkv = (_rms(ckv) * kvn_ref[...]).astype(BF16)
    cos = jnp.concatenate([cos_ref[...]] * MLA_HEADS, axis=1)
    sin = jnp.concatenate([sin_ref[...]] * MLA_HEADS, axis=1)
    q_ref[...] = (_dot(nq, wq_ref[...]) * cos + _dot(nq, wqr_ref[...]) * sin).astype(BF16)
    k_ref[...] = (_dot(nkv, wk_ref[...]) + jnp.concatenate([kr] * MLA_HEADS, axis=1)).astype(BF16)
    v_ref[...] = _dot(nkv, wv_ref[...]).astype(BF16)


def mla_prep(pc, q_norm, kv_norm, w_q_up, w_kv_up, cos_c, sin_c):
    b, l, _ = pc.shape
    tm = ROW_TILE
    scale = (MLA_NOPE + MLA_ROPE) ** -0.5 * float(np.log2(np.e))
    wq = (w_q_up * scale).reshape(-1, MLA_HEADS, MLA_NOPE + MLA_ROPE)
    zq = jnp.zeros(wq.shape[:2] + (LANES - MLA_NOPE - MLA_ROPE,), F32)
    wq_main = jnp.concatenate([wq, zq], axis=-1).reshape(-1, MLA_HEADS * LANES)
    wq_rot = jnp.concatenate([jnp.zeros_like(wq[..., :MLA_NOPE]), _rot_cols(wq[..., MLA_NOPE:], MLA_ROPE), zq],
                             axis=-1).reshape(-1, MLA_HEADS * LANES)
    wkv = w_kv_up.reshape(-1, MLA_HEADS, MLA_NOPE + MLA_V)
    wk = jnp.concatenate([wkv[..., :MLA_NOPE], jnp.zeros_like(wkv[..., :LANES - MLA_NOPE])],
                         axis=-1).reshape(-1, MLA_HEADS * LANES)
    wv = wkv[..., MLA_NOPE:].reshape(-1, MLA_HEADS * MLA_V)
    row = lambda i, t: (i, t, 0)
    const = lambda i, t: (0, 0)
    full = lambda a: pl.BlockSpec(a.shape, const)
    args = [q_norm[None, :], kv_norm[None, :], wq_main.astype(BF16), wq_rot.astype(BF16), wk.astype(BF16),
            wv.astype(BF16)]
    return pl.pallas_call(
        _mla_prep_kernel,
        grid=(b, l // tm),
        in_specs=[pl.BlockSpec((None, tm, C_W), row)] + [full(a) for a in args]
        + [pl.BlockSpec((tm, LANES), lambda i, t: (t, 0)), pl.BlockSpec((tm, LANES), lambda i, t: (t, 0))],
        out_specs=[pl.BlockSpec((None, tm, 4 * LANES), row), pl.BlockSpec((None, tm, 4 * LANES), row),
                   pl.BlockSpec((None, tm, 2 * LANES), row)],
        out_shape=[jax.ShapeDtypeStruct((b, l, 4 * LANES), BF16), jax.ShapeDtypeStruct((b, l, 4 * LANES), BF16),
                   jax.ShapeDtypeStruct((b, l, 2 * LANES), BF16)],
        compiler_params=pltpu.CompilerParams(dimension_semantics=("parallel", "parallel")),
        name="mla_prep",
    )(pc, *args, cos_c, sin_c)


def _mla_attn_kernel(q_ref, k_ref, v_ref, o_ref):
    t = pl.program_id(2)
    tq = q_ref.shape[0]

    def attend(nk):
        v = v_ref[0:nk, :]
        v_lane = lax.broadcasted_iota(jnp.int32, v.shape, 1)
        outs = []
        for j in range(2):
            q = q_ref[:, LANES * j:LANES * (j + 1)]
            k = k_ref[0:nk, LANES * j:LANES * (j + 1)]
            s = _dot_nt(q, k)
            p = jnp.exp2(s - s.max(axis=-1, keepdims=True)).astype(BF16)
            other = (v_lane >= MLA_V) if j == 0 else (v_lane < MLA_V)
            o = _dot(p, jnp.where(other, jnp.ones_like(v), v))
            den_lane = MLA_V if j == 0 else 0
            outs.append(o / o[:, den_lane:den_lane + 1])
        lane = lax.broadcasted_iota(jnp.int32, (tq, LANES), 1)
        o_ref[...] = jnp.where(lane < MLA_V, outs[0], outs[1]).astype(o_ref.dtype)

    @pl.when(t == 0)
    def _():
        attend(CTX_LEN)

    @pl.when(t > 0)
    def _():
        attend(k_ref.shape[0])


def mla_attention(q, k, v):
    b, l, _ = q.shape
    tq = ROW_TILE
    return pl.pallas_call(
        _mla_attn_kernel,
        grid=(b, 2, l // tq),
        in_specs=[pl.BlockSpec((None, tq, 2 * LANES), lambda i, p, t: (i, t, p)),
                  pl.BlockSpec((None, l, 2 * LANES), lambda i, p, t: (i, 0, p)),
                  pl.BlockSpec((None, l, LANES), lambda i, p, t: (i, 0, p))],
        out_specs=pl.BlockSpec((None, tq, LANES), lambda i, p, t: (i, t, p)),
        out_shape=jax.ShapeDtypeStruct((b, l, 2 * LANES), BF16),
        compiler_params=pltpu.CompilerParams(
            dimension_semantics=("parallel", "parallel", "parallel"), vmem_limit_bytes=VMEM_LIMIT),
        name="mla_attn",
    )(q, k, v)


def _head_mean(x, ones_bd):
    hi = x.astype(BF16)
    lo = (x - hi.astype(F32)).astype(BF16)
    return (_dot(hi, ones_bd) + _dot(lo, ones_bd)) * (1.0 / HEAD_DIM)


def _ret_kernel(x_ref, lg_ref, g_ref, o_ref, s_ref, dec_ref, part_ref):
    dr = pl.program_id(0)
    s = pl.program_id(1)
    ns = pl.num_programs(1)
    nb, c = x_ref.shape[0], x_ref.shape[1]
    w = RET_HEADS * HEAD_DIM
    chunk = jnp.where(s == 0, 0, jnp.where(dr == 0, s, ns - s))
    lg = lg_ref[...]
    fwd = dr == 0
    row_h = lax.broadcasted_iota(jnp.int32, (w, w), 0) // HEAD_DIM
    col_h = lax.broadcasted_iota(jnp.int32, (w, w), 1) // HEAD_DIM
    same_head = row_h == col_h

    @pl.when(s == 0)
    def _():
        s_ref[...] = jnp.zeros_like(s_ref)
        i = lax.broadcasted_iota(jnp.int32, (c, c), 0)
        j = lax.broadcasted_iota(jnp.int32, (c, c), 1)
        rel = jnp.where(fwd, i - j, j - i)
        relf = jnp.maximum(rel, 0).astype(F32)
        for h in range(RET_HEADS):
            lg_h = lg_ref[0:1, HEAD_DIM * h:HEAD_DIM * h + 1]
            dec_ref[h] = jnp.where(rel >= 0, jnp.exp(lg_h * relf), 0.0)

    pos = lax.broadcasted_iota(jnp.int32, (c, 1), 0).astype(F32)
    q_dec = jnp.exp(lg * jnp.where(fwd, pos + 1.0, c - pos))
    k_dec = jnp.exp(lg * jnp.where(fwd, c - 1.0 - pos, pos))
    lane_h = lax.broadcasted_iota(jnp.int32, (c, w), 1) // HEAD_DIM
    bs = range(nb)
    q = [x_ref[i, :, 0:w] for i in bs]
    kf = [x_ref[i, :, w:2 * w] for i in bs]
    v = [x_ref[i, :, 2 * w:3 * w].astype(BF16) for i in bs]
    kb = [x.astype(BF16) for x in kf]
    acc = [_dot((q[i] * q_dec).astype(BF16), s_ref[i].astype(BF16)) for i in bs]
    for h in range(RET_HEADS):
        qh = [jnp.where(lane_h == h, q[i], 0.0).astype(BF16) for i in bs]
        a = [(_dot_nt(qh[i], kb[i]) * dec_ref[h]).astype(BF16) for i in bs]
        acc = [acc[i] + jnp.where(lane_h == h, _dot(a[i], v[i]), 0.0) for i in bs]
    kv = [_dot_tn((kf[i] * k_dec).astype(BF16), v[i]) for i in bs]
    chunk_dec = jnp.exp(lg * float(c))
    for i in bs:
        s_ref[i] = s_ref[i] * chunk_dec + jnp.where(same_head, kv[i], 0.0)

    rows = pl.ds(pl.multiple_of(chunk * c, c), c)

    @pl.when(dr == 0)
    def _():
        for i in bs:
            part_ref[i, rows, :] = acc[i]

    @pl.when(dr == 1)
    def _():
        ones_bd = jnp.where(same_head, 1.0, 0.0).astype(BF16)
        for i in bs:
            o = part_ref[i, rows, :] + acc[i]
            mu = _head_mean(o, ones_bd)
            var = _head_mean(jnp.square(o - mu), ones_bd)
            y = (o - mu) * lax.rsqrt(var + EPS) * g_ref[...]
            o_ref[i] = (y * _silu(x_ref[i, :, 3 * w:4 * w])).astype(o_ref.dtype)


def retention_mixer(pd, log_decay, norm_g):
    b, l, _ = pd.shape
    c = ROW_TILE
    ns = l // c
    w = RET_HEADS * HEAD_DIM
    lg = jnp.repeat(-jnp.exp(log_decay.astype(F32)), HEAD_DIM, axis=-1)[:, None, :]

    def chunk_of(dr, s):
        return jnp.where(s == 0, 0, jnp.where(dr == 0, s, ns - s))

    return pl.pallas_call(
        _ret_kernel,
        grid=(2, ns),
        in_specs=[pl.BlockSpec((b, c, D_W), lambda dr, s: (0, chunk_of(dr, s), 0)),
                  pl.BlockSpec((None, 1, w), lambda dr, s: (dr, 0, 0)),
                  pl.BlockSpec((1, w), lambda dr, s: (0, 0))],
        out_specs=pl.BlockSpec((b, c, w), lambda dr, s: (0, jnp.where(dr == 0, 0, chunk_of(dr, s)), 0)),
        out_shape=jax.ShapeDtypeStruct((b, l, w), BF16),
        scratch_shapes=[pltpu.VMEM((b, w, w), F32), pltpu.VMEM((RET_HEADS, c, c), F32), pltpu.VMEM((b, l, w), F32)],
        compiler_params=pltpu.CompilerParams(
            dimension_semantics=("arbitrary", "arbitrary"), vmem_limit_bytes=VMEM_LIMIT),
        name="retention",
    )(pd, lg, norm_g[None, :])


def _out_proj_kernel(ma_ref, mb_ref, mc_ref, md_ref, h_ref, *refs, with_router):
    mods, (g_ref, w_ref), rest = refs[:WIDE], refs[WIDE:WIDE + 2], refs[WIDE + 2:]
    if with_router:
        wr_ref, hn_ref, v_ref, lg_ref = rest
    else:
        hn_ref, v_ref = rest
    gw = 2 * LANES
    mix = functools.reduce(jnp.add, [
        _dot(m_ref[...].astype(BF16), w_ref[gw * i:gw * (i + 1), :])
        for i, m_ref in enumerate((ma_ref, mb_ref, mc_ref, md_ref))])
    for k, mod_ref in enumerate(mods):
        r = _sub_rows(k)
        hn = h_ref[r, :] + mod_ref[2:3, :] * mix[r]
        hn_ref[r, :] = hn
        v = _rms(hn) * g_ref[...] * (1.0 + mod_ref[4:5, :]) + mod_ref[3:4, :]
        v_ref[r, :] = v.astype(v_ref.dtype)
        if with_router:
            lg_ref[r, :] = jnp.dot(v, wr_ref[...], preferred_element_type=F32, precision=lax.Precision.HIGHEST)


def out_proj(mixes, h, mod, gain, w, layer, w_router=None):
    b, l, d = h.shape
    tm = WIDE * ROW_TILE
    n = b * l
    with_router = w_router is not None
    row = lambda j: (j, 0)
    const = lambda j: (0, 0)
    in_specs = [pl.BlockSpec((tm, 2 * LANES), row) for _ in mixes] + [pl.BlockSpec((tm, d), row)] + _sub_mod_specs(
        d, l // ROW_TILE) + [pl.BlockSpec((1, d), const), _layer_weight_spec(w, layer)]
    out_specs = [pl.BlockSpec((tm, d), row), pl.BlockSpec((tm, d), row)]
    out_shape = [jax.ShapeDtypeStruct((n, d), F32), jax.ShapeDtypeStruct((n, d), F32 if with_router else BF16)]
    args = [m.reshape(n, 2 * LANES) for m in mixes] + [h.reshape(n, d)] + [mod] * WIDE + [gain, w]
    if with_router:
        in_specs.append(pl.BlockSpec(w_router.shape, const))
        out_specs.append(pl.BlockSpec((tm, LANES), row))
        out_shape.append(jax.ShapeDtypeStruct((n, LANES), F32))
        args.append(w_router)
    outs = pl.pallas_call(
        functools.partial(_out_proj_kernel, with_router=with_router),
        grid=(n // tm,),
        in_specs=in_specs,
        out_specs=out_specs,
        out_shape=out_shape,
        compiler_params=pltpu.CompilerParams(dimension_semantics=("parallel",), vmem_limit_bytes=VMEM_LIMIT),
        name="out_proj",
    )(*args)
    return [o.reshape(b, l, -1) for o in outs]


def build_out_weight(w):
    hd = HEAD_DIM
    rows = lambda lo, hi: w[..., lo:hi, :]
    return jnp.concatenate([rows(0, hd), rows(2 * hd, 3 * hd), rows(hd, 2 * hd), rows(3 * hd, None)],
                           axis=-2).astype(BF16)


def _ffn_kernel(v_ref, h_ref, *refs):
    mods, (wg_ref, wu_ref, wd_ref, o_ref) = refs[:WIDE], refs[WIDE:]
    v = v_ref[...]
    acc = jnp.zeros(o_ref.shape, F32)
    for j in range(D_FF // FF_CHUNK):
        cols = slice(j * FF_CHUNK, (j + 1) * FF_CHUNK)
        a = _dot(v, wg_ref[:, cols])
        u = _dot(v, wu_ref[:, cols])
        mid = (_silu(a) * u).astype(BF16)
        acc = acc + _dot(mid, wd_ref[cols, :])
    for k, mod_ref in enumerate(mods):
        r = _sub_rows(k)
        o_ref[r, :] = h_ref[r, :] + mod_ref[5:6, :] * acc[r]


def dense_ffn(v, h, mod, layer, wg, wu, wd):
    b, l, d = h.shape
    tm = WIDE * ROW_TILE
    n = b * l
    row = lambda j: (j, 0)
    const = lambda j: (0, 0)
    return pl.pallas_call(
        _ffn_kernel,
        grid=(n // tm,),
        in_specs=[pl.BlockSpec((tm, d), row), pl.BlockSpec((tm, d), row)] + _sub_mod_specs(d, l // ROW_TILE) + [
            _layer_weight_spec(wg, layer), _layer_weight_spec(wu, layer), _layer_weight_spec(wd, layer)],
        out_specs=pl.BlockSpec((tm, d), row),
        out_shape=jax.ShapeDtypeStruct((n, d), F32),
        compiler_params=pltpu.CompilerParams(dimension_semantics=("parallel",), vmem_limit_bytes=VMEM_LIMIT),
        name="dense_ffn",
    )(v.reshape(n, d), h.reshape(n, d), *([mod] * WIDE), wg, wu, wd).reshape(b, l, d)


def _moe_kernel(wt_ref, we_ref, lo_ref, hi_ref, first_ref, x_ref, wg_ref, wu_ref, wd_ref, o_ref, xm_ref, acc_ref):
    w = pl.program_id(0)
    j = pl.program_id(1)
    nj = pl.num_programs(1)
    tm = x_ref.shape[0]

    @pl.when(j == 0)
    def _():
        row = wt_ref[w] * tm + lax.broadcasted_iota(jnp.int32, (tm, 1), 0)
        keep = (row >= lo_ref[w]) & (row < hi_ref[w])
        xm_ref[...] = jnp.where(keep, x_ref[...], 0.0).astype(BF16)

    @pl.when((j == 0) & (first_ref[w] > 0))
    def _():
        acc_ref[...] = jnp.zeros_like(acc_ref)

    def swiglu_rows(rows):
        x = xm_ref[rows, :]
        a = _dot(x, wg_ref[...])
        u = _dot(x, wu_ref[...])
        mid = (_silu(a) * u).astype(BF16)
        acc_ref[rows, :] += _dot(mid, wd_ref[...])

    tile_lo = wt_ref[w] * tm
    whole = (lo_ref[w] <= tile_lo) & (hi_ref[w] >= tile_lo + tm)

    @pl.when(whole)
    def _():
        swiglu_rows(slice(0, tm))

    half = tm // 2
    for part in range(2):
        part_lo = tile_lo + part * half

        @pl.when(jnp.logical_not(whole) & (hi_ref[w] > jnp.maximum(part_lo, lo_ref[w])) & (lo_ref[w] < part_lo + half))
        def _():
            swiglu_rows(slice(part * half, (part + 1) * half))

    @pl.when(j == nj - 1)
    def _():
        o_ref[...] = acc_ref[...]


def moe_grouped_ffn(xs, items, layer_idx, wg, wu, wd):
    s, d = xs.shape
    tm = MOE_TILE
    fc = MOE_FF_CHUNK
    nw = items[0].shape[0]
    nj = D_FF // fc
    grid_spec = pltpu.PrefetchScalarGridSpec(
        num_scalar_prefetch=5,
        grid=(nw, nj),
        in_specs=[
            pl.BlockSpec((tm, d), lambda w, j, wt, we, lo, hi, fi: (wt[w], 0)),
            pl.BlockSpec((None, None, d, fc), lambda w, j, wt, we, lo, hi, fi: (layer_idx, we[w], 0, j)),
            pl.BlockSpec((None, None, d, fc), lambda w, j, wt, we, lo, hi, fi: (layer_idx, we[w], 0, j)),
            pl.BlockSpec((None, None, fc, d), lambda w, j, wt, we, lo, hi, fi: (layer_idx, we[w], j, 0)),
        ],
        out_specs=pl.BlockSpec((tm, d), lambda w, j, wt, we, lo, hi, fi: (wt[w], 0)),
        scratch_shapes=[pltpu.VMEM((tm, d), BF16), pltpu.VMEM((tm, d), F32)],
    )
    return pl.pallas_call(
        _moe_kernel,
        grid_spec=grid_spec,
        out_shape=jax.ShapeDtypeStruct((s, d), F32),
        compiler_params=pltpu.CompilerParams(
            dimension_semantics=("arbitrary", "arbitrary"), vmem_limit_bytes=VMEM_LIMIT),
        name="moe_ffn",
    )(*items, xs, wg, wu, wd)


def _residual_kernel(h_ref, f0_ref, f1_ref, gate_ref, mod_ref, *rest):
    f = gate_ref[:, 0:1] * f0_ref[...] + gate_ref[:, 1:2] * f1_ref[...]
    hn = h_ref[...] + mod_ref[5:6, :] * f
    if len(rest) == 2:
        gain_ref, o_ref = rest
        o_ref[...] = _rms(hn) * gain_ref[...]
    else:
        rest[0][...] = hn


def gated_residual(h, f0, f1, gates, mod, final_gain=None):
    b, l, d = h.shape
    tm = ROW_TILE
    skip = 0 if final_gain is None else CTX_LEN // tm
    row = lambda i, t: (i, t + skip, 0)
    in_specs = [pl.BlockSpec((None, tm, d), row), pl.BlockSpec((None, tm, d), row), pl.BlockSpec((None, tm, d), row),
                pl.BlockSpec((None, tm, LANES), row),
                pl.BlockSpec((None, None, SUBLANES, d), lambda i, t: (i, jnp.minimum(t + skip, 1), 0, 0))]
    args = [h, f0, f1, gates, mod]
    if final_gain is not None:
        in_specs.append(pl.BlockSpec((1, d), lambda i, t: (0, 0)))
        args.append(final_gain)
    return pl.pallas_call(
        _residual_kernel,
        grid=(b, l // tm - skip),
        in_specs=in_specs,
        out_specs=pl.BlockSpec((None, tm, d), lambda i, t: (i, t, 0)),
        out_shape=jax.ShapeDtypeStruct((b, l - skip * tm, d), F32),
        compiler_params=pltpu.CompilerParams(dimension_semantics=("parallel", "parallel")),
        name="gated_residual",
    )(*args)


def moe_ffn(v, logits, h, mod, layer_idx, wg, wu, wd, final_gain=None):
    b, l, d = h.shape
    t = b * l
    s = TOP_K * t
    tm = MOE_TILE
    nt = s // tm
    nw = nt + N_EXPERTS - 1
    i32 = jnp.int32
    lg = logits.reshape(t, LANES)[:, :N_EXPERTS]
    top_val, top_idx = lax.top_k(lg, TOP_K)
    gates = jax.nn.softmax(top_val, axis=-1)
    slot = jnp.arange(s, dtype=i32)
    skey = jnp.sort(top_idx.reshape(-1).astype(i32) * s + slot)
    order = skey % s
    _, inv = lax.sort_key_val(order, slot)
    bounds = (jnp.arange(N_EXPERTS, dtype=i32) + 1) * s
    cum = jnp.sum((skey[None, :] < bounds[:, None]).astype(i32), axis=1)
    cum_prev = jnp.concatenate([jnp.zeros((1,), i32), cum[:-1]])
    tile_lo = jnp.arange(nt, dtype=i32) * tm
    count_le = lambda edges, x: jnp.sum((edges[None, :] <= x[:, None]).astype(i32), axis=1)
    e_first = count_le(cum, tile_lo)
    e_last = count_le(cum, tile_lo + tm - 1)
    n_items = e_last - e_first + 1
    item_end = jnp.cumsum(n_items)
    item_start = item_end - n_items
    w = jnp.arange(nw, dtype=i32)
    wt = jnp.minimum(count_le(item_end, w), nt - 1)
    valid = w < item_end[-1]
    we = jnp.clip(e_first[wt] + w - item_start[wt], 0, N_EXPERTS - 1).astype(i32)
    lo = jnp.where(valid, cum_prev[we], 0).astype(i32)
    hi = jnp.where(valid, cum[we], 0).astype(i32)
    first = (valid & (w == item_start[wt])).astype(i32)
    rows_of = lambda a, idx: a.at[idx].get(mode="promise_in_bounds")
    xs = rows_of(v.reshape(t, d), order // TOP_K)
    ys = moe_grouped_ffn(xs, (wt, we, lo, hi, first), layer_idx, wg, wu, wd)
    dest = inv.reshape(t, TOP_K)
    f0 = rows_of(ys, dest[:, 0]).reshape(b, l, d)
    f1 = rows_of(ys, dest[:, 1]).reshape(b, l, d)
    gates_p = jnp.pad(gates, ((0, 0), (0, LANES - TOP_K))).reshape(b, l, LANES)
    return gated_residual(h, f0, f1, gates_p, mod, final_gain)


def _final_norm_kernel(h_ref, g_ref, o_ref):
    o_ref[...] = _rms(h_ref[...]) * g_ref[...]


def final_rms_norm(h, gain, n_ctx_tiles):
    b, l, d = h.shape
    tm = ROW_TILE
    n = l - n_ctx_tiles * tm
    return pl.pallas_call(
        _final_norm_kernel,
        grid=(b, n // tm),
        in_specs=[
            pl.BlockSpec((None, tm, d), lambda i, t: (i, t + n_ctx_tiles, 0)),
            pl.BlockSpec((1, d), lambda i, t: (0, 0)),
        ],
        out_specs=pl.BlockSpec((None, tm, d), lambda i, t: (i, t, 0)),
        out_shape=jax.ShapeDtypeStruct((b, n, d), F32),
        compiler_params=pltpu.CompilerParams(dimension_semantics=("parallel", "parallel")),
        name="final_norm",
    )(h, gain)


GDN_W = GDN_HEADS * GDN_DK
GDN_CONV_K = 5
GDN_HALO = SUBLANES


def _split3(x):
    p0 = x.astype(BF16)
    r1 = x - p0.astype(F32)
    p1 = r1.astype(BF16)
    p2 = (r1 - p1.astype(F32)).astype(BF16)
    return p0, p1, p2


def _gdn_prep_kernel(x_ref, prev_ref, next_ref, cw_ref, par_ref, q_ref, k_ref, v_ref, gb_ref):
    t = pl.program_id(1)
    last = pl.num_programs(1) - 1
    tm = x_ref.shape[0]
    w3 = 3 * GDN_W
    has_prev = t > 1
    has_next = (t > 0) & (t < last)
    prev = jnp.where(has_prev, prev_ref[...], 0.0)
    nxt = jnp.where(has_next, next_ref[...], 0.0)
    xe = jnp.concatenate([prev, x_ref[:, :w3], nxt], axis=0)
    y = jnp.zeros((tm, w3), F32)
    for j in range(GDN_CONV_K):
        lo = GDN_HALO - GDN_CONV_K // 2 + j
        y = y + cw_ref[j:j + 1, :] * xe[lo:lo + tm, :]
    y = _silu(y)
    r = lax.broadcasted_iota(jnp.int32, (GDN_W, GDN_W), 0)
    c = lax.broadcasted_iota(jnp.int32, (GDN_W, GDN_W), 1)
    ones_bd = jnp.where(r // GDN_DK == c // GDN_DK, 1.0, 0.0).astype(BF16)

    def l2n(x):
        sq = x * x
        hi = sq.astype(BF16)
        lo = (sq - hi.astype(F32)).astype(BF16)
        return x * lax.rsqrt(_dot(hi, ones_bd) + _dot(lo, ones_bd) + EPS)

    q_ref[...] = l2n(y[:, :GDN_W]) * GDN_DK ** -0.5
    k_ref[...] = l2n(y[:, GDN_W:2 * GDN_W])
    v_ref[...] = y[:, 2 * GDN_W:]
    ab = x_ref[:, w3 + GDN_W:]
    lane = lax.broadcasted_iota(jnp.int32, ab.shape, 1)
    is_g = (lane % 8) < 4
    z = ab + par_ref[1:2, :]
    softplus = jnp.maximum(z, 0.0) + jnp.log1p(jnp.exp(-jnp.abs(z)))
    g = jnp.where(is_g, par_ref[0:1, :] * softplus, 0.0)
    beta = 1.0 / (1.0 + jnp.exp(-ab))
    i = lax.broadcasted_iota(jnp.int32, (tm, tm), 0)
    j = lax.broadcasted_iota(jnp.int32, (tm, tm), 1)
    same_chunk = i // GDN_CHUNK == j // GDN_CHUNK
    tri_f = jnp.where(same_chunk & (j <= i), 1.0, 0.0).astype(BF16)
    tri_b = jnp.where(same_chunk & (j >= i), 1.0, 0.0).astype(BF16)
    pieces = _split3(g)
    gc_f = functools.reduce(jnp.add, [_dot(tri_f, p) for p in pieces])
    gc_b = functools.reduce(jnp.add, [_dot(tri_b, p) for p in pieces])
    gb_ref[...] = jnp.where(is_g, jnp.where(lane < 8, gc_f, gc_b), beta)


def gdn_prep(pb, conv_w, a_log, dt_bias):
    b, l, _ = pb.shape
    tm = ROW_TILE
    w3 = 3 * GDN_W
    halo_blocks = tm // GDN_HALO
    n_halo = l // GDN_HALO
    cw = jnp.pad(conv_w, ((0, SUBLANES - GDN_CONV_K), (0, 0)))
    neg_a = jnp.pad(-jnp.exp(a_log.astype(F32)), ((0, 0), (0, 4))).reshape(-1)
    dtb = jnp.pad(dt_bias.astype(F32), ((0, 0), (0, 4))).reshape(-1)
    par = jnp.pad(jnp.stack([neg_a, dtb]), ((0, SUBLANES - 2), (0, LANES - 16)))
    row = lambda i, t: (i, t, 0)
    out = lambda w: pl.BlockSpec((None, tm, w), row)
    return pl.pallas_call(
        _gdn_prep_kernel,
        grid=(b, l // tm),
        in_specs=[pl.BlockSpec((None, tm, B_W), row),
                  pl.BlockSpec((None, GDN_HALO, w3), lambda i, t: (i, jnp.maximum(t * halo_blocks - 1, 0), 0)),
                  pl.BlockSpec((None, GDN_HALO, w3),
                               lambda i, t: (i, jnp.minimum((t + 1) * halo_blocks, n_halo - 1), 0)),
                  pl.BlockSpec(cw.shape, lambda i, t: (0, 0)),
                  pl.BlockSpec(par.shape, lambda i, t: (0, 0))],
        out_specs=[out(GDN_W), out(GDN_W), out(GDN_W), out(LANES)],
        out_shape=[jax.ShapeDtypeStruct((b, l, GDN_W), F32)] * 3 + [jax.ShapeDtypeStruct((b, l, LANES), F32)],
        compiler_params=pltpu.CompilerParams(
            dimension_semantics=("parallel", "parallel"), vmem_limit_bytes=VMEM_LIMIT),
        name="gdn_prep",
    )(pb, pb, pb, cw, par)


def _tile_heads(x):
    return jnp.concatenate([x] * GDN_HEADS, axis=0)


def _collapse_heads(x):
    c = GDN_CHUNK
    return x[0:c] + x[c:2 * c] + x[2 * c:3 * c] + x[3 * c:4 * c]


def _gdn_chunk_kernel(q_ref, k_ref, v_ref, gb_ref, o0_ref, qe_ref, a_ref, bm_ref, gam_ref):
    n = GDN_W
    cs = GDN_CHUNK
    r = lax.broadcasted_iota(jnp.int32, (n, n), 0)
    c = lax.broadcasted_iota(jnp.int32, (n, n), 1)
    ri, ci = r % cs, c % cs
    head = r // cs == c // cs
    eye = jnp.where(r == c, 1.0, 0.0)
    blk = lambda s: r // s == c // s
    b8, b16, b32 = blk(8), blk(16), blk(32)
    lane = lax.broadcasted_iota(jnp.int32, (n, LANES), 1)
    row_head = lax.broadcasted_iota(jnp.int32, (n, LANES), 0) // cs
    pick = lambda sel, x: jnp.sum(jnp.where(sel, x, 0.0), axis=1, keepdims=True)
    src_lane = lax.broadcasted_iota(jnp.int32, (LANES, n), 0)
    dst_head = lax.broadcasted_iota(jnp.int32, (LANES, n), 1) // cs
    tri, tri_strict, sel_g, sel_b, widen_g, widen_b = {}, {}, {}, {}, {}, {}
    for fwd in (True, False):
        ahead = ri - ci if fwd else ci - ri
        tri[fwd] = head & (ahead >= 0)
        tri_strict[fwd] = head & (ahead > 0)
        lane0 = 0 if fwd else 8
        sel_g[fwd] = lane == lane0 + row_head
        sel_b[fwd] = lane == lane0 + 4 + row_head
        widen_g[fwd] = jnp.where(src_lane == lane0 + dst_head, 1.0, 0.0).astype(BF16)
        widen_b[fwd] = jnp.where(src_lane == lane0 + 4 + dst_head, 1.0, 0.0).astype(BF16)

    n_chunks = q_ref.shape[0] // cs
    rows = [slice(ch * cs, (ch + 1) * cs) for ch in range(n_chunks)]
    items = [(fwd, ch) for fwd in (True, False) for ch in range(n_chunks)]
    dirs = [fwd for fwd, _ in items]
    per_item = lambda xs: [xs[ch] for _, ch in items]
    each = lambda f, *xs: [f(*a) for a in zip(*xs)]
    bf = lambda xs: [x.astype(BF16) for x in xs]
    widen = lambda x, e: functools.reduce(jnp.add, [_dot(p, e) for p in _split3(x)])
    zero_bf = jnp.zeros((n, n), BF16)
    spread = lambda xs: [jnp.where(head, _tile_heads(x.astype(BF16)), zero_bf) for x in xs]
    k_c, q_c, v_c = ([ref[rw, :] for rw in rows] for ref in (k_ref, q_ref, v_ref))
    khb_c, qhb_c = spread(k_c), spread(q_c)
    kk = per_item(each(_dot_nt, khb_c, khb_c))
    qk = per_item(each(_dot_nt, qhb_c, khb_c))
    k_t, q_t, v_t = per_item(k_c), per_item(q_c), per_item(v_c)
    gb = per_item([gb_ref[rw, :] for rw in rows])
    gb4 = [_tile_heads(x) for x in gb]
    gc = [pick(sel_g[f], x) for f, x in zip(dirs, gb4)]
    beta = [pick(sel_b[f], x) for f, x in zip(dirs, gb4)]
    gc_t = [widen(x, widen_g[f]) for f, x in zip(dirs, gb)]
    beta_t = [widen(x, widen_b[f]) for f, x in zip(dirs, gb)]
    ends = [cs - 1 if f else 0 for f in dirs]
    gl_t = [jnp.broadcast_to(x[e:e + 1, :], x.shape) for x, e in zip(gc_t, ends)]
    gc_b = [jnp.broadcast_to(x, (n, n)) for x in gc]
    decay = [jnp.exp(jnp.minimum(x - x.T, 0.0)) for x in gc_b]
    lmat = [jnp.where(tri_strict[f], b_ * kk_ * d_, 0.0) for f, b_, kk_, d_ in zip(dirs, beta, kk, decay)]
    attn = bf([jnp.where(tri[f], qk_ * d_, 0.0) for f, qk_, d_ in zip(dirs, qk, decay)])
    nl = bf([jnp.where(b8, -x, 0.0) for x in lmat])
    n2 = bf(each(_dot, nl, nl))
    n4 = each(_dot, n2, n2)
    p1 = bf(each(lambda a, b_: _dot((eye + a).astype(BF16), (eye + b_).astype(BF16)), nl, n2))
    tinv = each(lambda p, x: _dot(p, (eye + x).astype(BF16)), p1, n4)

    def moving_rows(fwd, x, sz):
        return jnp.concatenate([x[i:i + sz] for i in range(sz if fwd else 0, n, 2 * sz)], axis=0)

    def with_moving_rows(fwd, x, new, sz):
        pieces = []
        for j, i in enumerate(range(0, n, 2 * sz)):
            kept = x[i:i + sz] if fwd else x[i + sz:i + 2 * sz]
            moved = new[j * sz:(j + 1) * sz]
            pieces += [kept, moved] if fwd else [moved, kept]
        return jnp.concatenate(pieces, axis=0)

    for sz, inner, outer in ((8, b8, b16), (16, b16, b32), (32, b32, head)):
        off = bf([jnp.where(outer & ~inner, x, 0.0) for x in lmat])
        tb = bf(tinv)
        t_mv = [moving_rows(f, x, sz) for f, x in zip(dirs, tinv)]
        to = bf(each(_dot, bf(t_mv), off))
        tinv = [with_moving_rows(f, t_, tm_ - _dot(to_, tb_), sz)
                for f, t_, tm_, to_, tb_ in zip(dirs, tinv, t_mv, to, tb)]
    tb = bf(tinv)
    eg_t = [jnp.exp(x) for x in gc_t]
    u = bf(each(_dot, tb, spread(each(lambda b_, v_: b_ * v_, beta_t, v_t))))
    w = bf(each(_dot, tb, spread(each(lambda b_, e_, k_: (b_ * e_) * k_, beta_t, eg_t, k_t))))
    o0 = each(_dot, attn, u)
    aw = each(_dot, attn, w)
    kg = spread(each(lambda k_, gl_, gc_: k_ * jnp.exp(gl_ - gc_), k_t, gl_t, gc_t))
    a_mat = each(_dot_tn, kg, w)
    b_mat = each(_dot_tn, kg, u)
    for (fwd, ch), o0_, aw_, a_, b_, q_, e_, gl_ in zip(items, o0, aw, a_mat, b_mat, q_t, eg_t, gl_t):
        dr, rw = 0 if fwd else 1, rows[ch]
        o0_ref[dr, rw, :] = _collapse_heads(o0_)
        qe_ref[dr, rw, :] = q_ * e_ - _collapse_heads(aw_)
        a_ref[dr, rw, :] = _collapse_heads(a_)
        bm_ref[dr, rw, :] = _collapse_heads(b_)
        gam_ref[dr, rw, :] = jnp.exp(gl_)


def gdn_chunks(q, k, v, gb):
    b, l, _ = q.shape
    tm = ROW_TILE
    row = lambda i, t: (i, t, 0)
    out = pl.BlockSpec((2, None, tm, GDN_W), lambda i, t: (0, i, t, 0))
    return pl.pallas_call(
        _gdn_chunk_kernel,
        grid=(b, l // tm),
        in_specs=[pl.BlockSpec((None, tm, GDN_W), row)] * 3 + [pl.BlockSpec((None, tm, LANES), row)],
        out_specs=[out] * 5,
        out_shape=[jax.ShapeDtypeStruct((2, b, l, GDN_W), F32)] * 5,
        compiler_params=pltpu.CompilerParams(
            dimension_semantics=("parallel", "parallel"), vmem_limit_bytes=VMEM_LIMIT),
        name="gdn_chunk",
    )(q, k, v, gb)


def _gdn_scan_kernel(o0_ref, qe_ref, a_ref, bm_ref, gam_ref, gate_ref, g_ref, o_ref, s_ref, part_ref):
    dr = pl.program_id(0)
    s = pl.program_id(1)
    ns = pl.num_programs(1)
    nb, tm = o0_ref.shape[0], o0_ref.shape[1]
    cs = GDN_CHUNK
    n = GDN_W
    nch = tm // cs
    bs = range(nb)
    tile = jnp.where(s == 0, 0, jnp.where(dr == 0, s, ns - s))
    r = lax.broadcasted_iota(jnp.int32, (n, n), 0)
    c = lax.broadcasted_iota(jnp.int32, (n, n), 1)
    head = r // cs == c // cs

    @pl.when(s == 0)
    def _():
        s_ref[...] = jnp.zeros_like(s_ref)

    def run(order):
        state = [s_ref[i] for i in bs]
        outs = [{} for _ in bs]
        for ch in order:
            rows = slice(ch * cs, (ch + 1) * cs)
            sb = [x.astype(BF16) for x in state]
            for i in bs:
                outs[i][ch] = o0_ref[i, rows, :] + _dot(qe_ref[i, rows, :].astype(BF16), sb[i])
            a_full = [jnp.where(head, _tile_heads(a_ref[i, rows, :]), 0.0).astype(BF16) for i in bs]
            state = [_tile_heads(gam_ref[i, rows, :]) * state[i] - _dot(a_full[i], sb[i])
                     + jnp.where(head, _tile_heads(bm_ref[i, rows, :]), 0.0) for i in bs]
        for i in bs:
            s_ref[i] = state[i]
        return [jnp.concatenate([outs[i][ch] for ch in range(nch)], axis=0) for i in bs]

    rows_out = pl.ds(pl.multiple_of(tile * tm, tm), tm)

    @pl.when(dr == 0)
    def _():
        for i, o in enumerate(run(range(nch))):
            part_ref[i, rows_out, :] = o

    @pl.when(dr == 1)
    def _():
        ones_bd = jnp.where(head, 1.0, 0.0).astype(BF16)
        for i, o_bwd in enumerate(run(range(nch - 1, -1, -1))):
            o = part_ref[i, rows_out, :] + o_bwd
            ms = _head_mean(o * o, ones_bd)
            o_ref[i] = (o * lax.rsqrt(ms + EPS) * g_ref[...] * _silu(gate_ref[i])).astype(o_ref.dtype)


def gdn_scan(o0, qe, a, bm, gam, pb, norm_g):
    _, b, l, _ = o0.shape
    tm = ROW_TILE
    ns = l // tm

    def tile_of(dr, s):
        return jnp.where(s == 0, 0, jnp.where(dr == 0, s, ns - s))

    per_dir = pl.BlockSpec((None, b, tm, GDN_W), lambda dr, s: (dr, 0, tile_of(dr, s), 0))
    gate_col = 3 * GDN_W // GDN_W
    return pl.pallas_call(
        _gdn_scan_kernel,
        grid=(2, ns),
        in_specs=[per_dir] * 5 + [
            pl.BlockSpec((b, tm, GDN_W), lambda dr, s: (0, tile_of(dr, s), gate_col)),
            pl.BlockSpec((1, GDN_W), lambda dr, s: (0, 0))],
        out_specs=pl.BlockSpec((b, tm, GDN_W), lambda dr, s: (0, jnp.where(dr == 0, 0, tile_of(dr, s)), 0)),
        out_shape=jax.ShapeDtypeStruct((b, l, GDN_W), BF16),
        scratch_shapes=[pltpu.VMEM((b, GDN_W, GDN_W), F32), pltpu.VMEM((b, l, GDN_W), F32)],
        compiler_params=pltpu.CompilerParams(
            dimension_semantics=("arbitrary", "arbitrary"), vmem_limit_bytes=VMEM_LIMIT),
        name="gdn_scan",
    )(o0, qe, a, bm, gam, pb, jnp.tile(norm_g, GDN_HEADS)[None, :])


def gdn_mixer(pb, conv_w, a_log, dt_bias, norm_g):
    q, k, v, gb = gdn_prep(pb, conv_w, a_log, dt_bias)
    o0, qe, a, bm, gam = gdn_chunks(q, k, v, gb)
    return gdn_scan(o0, qe, a, bm, gam, pb, norm_g)


def kernel(x, c, ctx, c_ctx, w_mod, b_mod, norm1, norm2, w_in, w_out, swa_sink, gdn_conv, gdn_a_log, gdn_dt_bias, gdn_norm, mla_q_norm, mla_kv_norm, mla_w_q_up, mla_w_kv_up, ret_log_decay, ret_norm, ffn_w_gate, ffn_w_up, ffn_w_down, moe_router, moe_w_gate, moe_w_up, moe_w_down, final_norm):
    b, n, d = x.shape
    depth = w_in.shape[0]
    cos_t, sin_t = rope_tables(n)
    cos_c, sin_c = cos_t[:, A_ROT_W:A_ROT_W + C_ROT_W], sin_t[:, A_ROT_W:A_ROT_W + C_ROT_W]
    h = jnp.concatenate([ctx, x], axis=1)
    cond = jnp.concatenate([jax.nn.silu(c_ctx)[None, :], jax.nn.silu(c)], axis=0)
    mods = jnp.einsum("bd,ldk->lbk", cond, w_mod, precision=lax.Precision.HIGHEST) + b_mod[:, None, :]
    mods = mods.reshape(depth, 1 + b, 6, d)
    mods = jnp.stack([jnp.broadcast_to(mods[:, :1], (depth, b, 6, d)), mods[:, 1:]], axis=2)
    mods = jnp.pad(mods, ((0, 0), (0, 0), (0, 0), (0, SUBLANES - 6), (0, 0)))
    w_in_all = build_in_weight(w_in)
    w_out_all = build_out_weight(w_out)
    ffn_wg, ffn_wu, ffn_wd = ffn_w_gate.astype(BF16), ffn_w_up.astype(BF16), ffn_w_down.astype(BF16)
    moe_wg, moe_wu, moe_wd = moe_w_gate.astype(BF16), moe_w_up.astype(BF16), moe_w_down.astype(BF16)
    for layer in range(depth):
        mod = mods[layer]
        pa, pb, pc, pd = norm_proj(h, mod, norm1[layer][None, :], w_in_all, layer, cos_t, sin_t)
        mix_a = swa_mixer(pa, swa_sink[layer])
        mix_b = gdn_mixer(pb, gdn_conv[layer], gdn_a_log[layer], gdn_dt_bias[layer], gdn_norm[layer])
        mq, mk, mv = mla_prep(pc, mla_q_norm[layer], mla_kv_norm[layer], mla_w_q_up[layer], mla_w_kv_up[layer],
                              cos_c, sin_c)
        mix_c = mla_attention(mq, mk, mv)
        mix_d = retention_mixer(pd, ret_log_decay[layer], ret_norm[layer])
        mixes = (mix_a, mix_b, mix_c, mix_d)
        i = layer // 2
        if layer % 2 == 0:
            h, v = out_proj(mixes, h, mod, norm2[layer][None, :], w_out_all, layer)
            h = dense_ffn(v, h, mod, i, ffn_wg, ffn_wu, ffn_wd)
        else:
            w_r = jnp.pad(moe_router[i], ((0, 0), (0, LANES - N_EXPERTS)))
            h, v, logits = out_proj(mixes, h, mod, norm2[layer][None, :], w_out_all, layer, w_r)
            if layer == depth - 1:
                return moe_ffn(v, logits, h, mod, i, moe_wg, moe_wu, moe_wd, final_norm[None, :])
            h = moe_ffn(v, logits, h, mod, i, moe_wg, moe_wu, moe_wd)
    return final_rms_norm(h, final_norm[None, :], CTX_LEN // ROW_TILE)
```

```python
import functools

import numpy as np
import jax
import jax.numpy as jnp
from jax import lax
from jax.experimental import pallas as pl
from jax.experimental.pallas import tpu as pltpu

D_MODEL = 1024
GRID_W = 64
CTX_LEN = 256
HEAD_DIM = 64
ROPE_THETA = 10000.0
EPS = 1e-6
NEG_INF = -1e30

SWA_WINDOW = 128
GDN_HEADS = 4
GDN_DK = 64
GDN_DV = 64
GDN_CHUNK = 64
MLA_HEADS = 4
MLA_NOPE = 64
MLA_ROPE = 32
MLA_V = 64
RET_HEADS = 4
RET_DK = 64
D_FF = 3584
N_EXPERTS = 8
TOP_K = 2

LANES = 128
SUBLANES = 8
VMEM_LIMIT = 56 * 1024 * 1024

ROW_TILE = 256
FF_CHUNK = 512
MOE_TILE = 512
MOE_FF_CHUNK = 1792

A_W, B_W, C_W, D_W = 768, 1152, 512, 1024
A_ROT_W, C_ROT_W, D_ROT_W = 640, 128, 512
OFF_A = 0
OFF_B = OFF_A + A_W
OFF_C = OFF_B + B_W
OFF_D = OFF_C + C_W
OFF_AR = OFF_D + D_W
OFF_CR = OFF_AR + A_ROT_W
OFF_DR = OFF_CR + C_ROT_W
W_ALL = OFF_DR + D_ROT_W
ROPE_W = A_ROT_W + C_ROT_W + D_ROT_W

LOG2_E = float(np.log2(np.e))
F32 = jnp.float32
BF16 = jnp.bfloat16
NT_DIMS = (((1,), (1,)), ((), ()))
TN_DIMS = (((0,), (0,)), ((), ()))


def _rms(x):
    return x * lax.rsqrt(jnp.mean(x * x, axis=-1, keepdims=True) + EPS)


def _silu(x):
    return x * (1.0 / (1.0 + jnp.exp(-x)))


def _dot(a, b):
    return jnp.dot(a, b, preferred_element_type=F32)


def _dot_nt(a, b):
    return lax.dot_general(a, b, NT_DIMS, preferred_element_type=F32)


def _dot_tn(a, b):
    return lax.dot_general(a, b, TN_DIMS, preferred_element_type=F32)


WIDE = 2


def _sub_tile_specs(block, tiles_per_seq, index_of):
    def spec(k):
        return pl.BlockSpec(block, lambda j: index_of((WIDE * j + k) // tiles_per_seq, (WIDE * j + k) % tiles_per_seq))
    return [spec(k) for k in range(WIDE)]


def _sub_mod_specs(d, tiles_per_seq):
    return _sub_tile_specs((None, None, SUBLANES, d), tiles_per_seq, lambda b, t: (b, jnp.minimum(t, 1), 0, 0))


def _sub_rows(k):
    return slice(k * ROW_TILE, (k + 1) * ROW_TILE)


def _norm_proj_kernel(h_ref, *refs):
    mods, (g_ref, w_ref), tabs = refs[:WIDE], refs[WIDE:WIDE + 2], refs[WIDE + 2:3 * WIDE + 2]
    a_ref, b_ref, c_ref, d_ref = refs[3 * WIDE + 2:]
    y = _rms(h_ref[...]) * g_ref[...]
    u = jnp.concatenate([y[_sub_rows(k)] * (1.0 + m[1:2, :]) + m[0:1, :] for k, m in enumerate(mods)],
                        axis=0).astype(BF16)

    def mm(lo, width):
        return _dot(u, w_ref[:, lo:lo + width])

    a_main = mm(OFF_A, A_W)
    a_rot = mm(OFF_AR, A_ROT_W)
    b_ref[...] = mm(OFF_B, B_W)
    c_main = mm(OFF_C, C_W)
    c_rot = mm(OFF_CR, C_ROT_W)
    d_main = mm(OFF_D, D_W)
    d_rot = mm(OFF_DR, D_ROT_W)
    lo, hi = A_ROT_W, A_ROT_W + C_ROT_W
    a_ref[:, A_ROT_W:] = a_main[:, A_ROT_W:].astype(BF16)
    c_ref[:, :C_W - C_ROT_W] = c_main[:, :C_W - C_ROT_W]
    d_ref[:, D_ROT_W:] = d_main[:, D_ROT_W:]
    for k in range(WIDE):
        cos_ref, sin_ref = tabs[2 * k], tabs[2 * k + 1]
        r = _sub_rows(k)
        a_ref[r, :A_ROT_W] = (a_main[r, :A_ROT_W] * cos_ref[:, :A_ROT_W] + a_rot[r] * sin_ref[:, :A_ROT_W]).astype(BF16)
        c_ref[r, C_W - C_ROT_W:] = c_main[r, C_W - C_ROT_W:] * cos_ref[:, lo:hi] + c_rot[r] * sin_ref[:, lo:hi]
        d_ref[r, :D_ROT_W] = d_main[r, :D_ROT_W] * cos_ref[:, hi:] + d_rot[r] * sin_ref[:, hi:]


def _layer_weight_spec(w, layer):
    return pl.BlockSpec((None,) + w.shape[1:], lambda j: (layer, 0, 0), pipeline_mode=pl.Buffered(1))


def norm_proj(h, mod, gain, w, layer, cos_t, sin_t):
    b, l, d = h.shape
    tm = WIDE * ROW_TILE
    tps = l // ROW_TILE
    row = lambda j: (j, 0)
    const = lambda j: (0, 0)
    tabs = _sub_tile_specs((ROW_TILE, ROPE_W), tps, lambda bi, t: (t, 0))
    tab_specs = [s for pair in zip(tabs, _sub_tile_specs((ROW_TILE, ROPE_W), tps, lambda bi, t: (t, 0))) for s in pair]
    outs = pl.pallas_call(
        _norm_proj_kernel,
        grid=(b * l // tm,),
        in_specs=[pl.BlockSpec((tm, d), row)] + _sub_mod_specs(d, tps) + [
            pl.BlockSpec((1, d), const),
            _layer_weight_spec(w, layer)] + tab_specs,
        out_specs=[pl.BlockSpec((tm, A_W), row), pl.BlockSpec((tm, B_W), row),
                   pl.BlockSpec((tm, C_W), row), pl.BlockSpec((tm, D_W), row)],
        out_shape=[jax.ShapeDtypeStruct((b * l, A_W), BF16), jax.ShapeDtypeStruct((b * l, B_W), F32),
                   jax.ShapeDtypeStruct((b * l, C_W), F32), jax.ShapeDtypeStruct((b * l, D_W), F32)],
        compiler_params=pltpu.CompilerParams(dimension_semantics=("parallel",), vmem_limit_bytes=VMEM_LIMIT),
        name="norm_proj",
    )(h.reshape(b * l, d), *([mod] * WIDE), gain, w, *([cos_t, sin_t] * WIDE))
    return [o.reshape(b, l, -1) for o in outs]


def _rot_cols(w, hd):
    x = w.reshape(w.shape[:-1] + (w.shape[-1] // hd, 4, hd // 4))
    x1, x2, x3, x4 = x[..., 0, :], x[..., 1, :], x[..., 2, :], x[..., 3, :]
    return jnp.stack([-x2, x1, -x4, x3], axis=-2).reshape(w.shape)


def _place_swa_q(q):
    z = jnp.zeros(q.shape[:-1] + (HEAD_DIM,), q.dtype)
    blocks = []
    for h in range(4):
        qh = q[..., HEAD_DIM * h:HEAD_DIM * (h + 1)]
        blocks += [qh, z] if h // 2 == 0 else [z, qh]
    return jnp.concatenate(blocks, axis=-1)


def build_in_weight(w):
    o = [int(v) for v in np.cumsum((256, 128, 128, 768, 256, 16, 256, 128, 32, 256, 256, 256, 256))]
    aq, ak, av = w[..., :o[0]] * (HEAD_DIM ** -0.5 * LOG2_E), w[..., o[0]:o[1]], w[..., o[1]:o[2]]
    b_main, b_ab = w[..., o[2]:o[4]], w[..., o[4]:o[5]]
    c_q, c_kv, c_kr = w[..., o[5]:o[6]], w[..., o[6]:o[7]], w[..., o[7]:o[8]]
    dq, dk, dvg = w[..., o[8]:o[9]], w[..., o[9]:o[10]] * RET_DK ** -0.5, w[..., o[10]:]
    z = lambda n: jnp.zeros(w.shape[:-1] + (n,), w.dtype)
    parts = [
        _place_swa_q(aq), ak, av,
        b_main, b_ab, z(LANES - b_ab.shape[-1]),
        c_q, c_kv, z(64), c_kr, z(32),
        dq, dk, dvg,
        _place_swa_q(_rot_cols(aq, HEAD_DIM)), _rot_cols(ak, HEAD_DIM),
        z(64), _rot_cols(c_kr, MLA_ROPE), z(32),
        _rot_cols(dq, HEAD_DIM), _rot_cols(dk, HEAD_DIM),
    ]
    out = jnp.concatenate(parts, axis=-1)
    assert out.shape[-1] == W_ALL
    return out.astype(BF16)


def rope_tables(n):
    lat = jnp.arange(CTX_LEN + n, dtype=jnp.int32) - CTX_LEN
    grid_row = jnp.where(lat >= 0, lat // GRID_W, 0).astype(F32)[:, None]
    grid_col = jnp.where(lat >= 0, lat % GRID_W, 0).astype(F32)[:, None]
    narrow, col_of = [], {}
    for rot_dim in (HEAD_DIM, MLA_ROPE):
        n_freq = rot_dim // 4
        inv_freq = ROPE_THETA ** (-jnp.arange(n_freq, dtype=F32) / n_freq)
        for axis, pos in enumerate((grid_row, grid_col)):
            col_of[rot_dim, axis] = sum(a.shape[1] for a in narrow)
            narrow.append(pos * inv_freq)
    ang = jnp.concatenate(narrow, axis=1)
    identity_col = ang.shape[1]
    sel = np.zeros((identity_col + 1, ROPE_W), np.float32)

    def plan(lane0, width, group, rot_lo, rot_dim):
        for c in range(width):
            j = c % group - rot_lo
            if 0 <= j < rot_dim:
                quarter, f = divmod(j, rot_dim // 4)
                sel[col_of[rot_dim, quarter // 2] + f, lane0 + c] = 1.0
            else:
                sel[identity_col, lane0 + c] = 1.0

    plan(0, A_ROT_W, HEAD_DIM, 0, HEAD_DIM)
    plan(A_ROT_W, C_ROT_W, C_ROT_W, 64, MLA_ROPE)
    plan(A_ROT_W + C_ROT_W, D_ROT_W, HEAD_DIM, 0, HEAD_DIM)
    spread = lambda t: jnp.dot(t, jnp.asarray(sel), precision=lax.Precision.HIGHEST)
    ones, zeros = jnp.ones_like(grid_row), jnp.zeros_like(grid_row)
    return (spread(jnp.concatenate([jnp.cos(ang), ones], axis=1)),
            spread(jnp.concatenate([jnp.sin(ang), zeros], axis=1)))


def _swa_kernel(sink_ref, q_ref, kp_ref, ko_ref, kn_ref, kc_ref, vp_ref, vo_ref, vn_ref, vc_ref, o_ref):
    t = pl.program_id(1)
    last = pl.num_programs(1) - 1
    tq = q_ref.shape[0]
    half = tq // 2

    def attend(k, v, mask):
        v_lane = lax.broadcasted_iota(jnp.int32, v.shape, 1)
        v_ones = [jnp.where(v_lane >= HEAD_DIM, jnp.ones_like(v), v), jnp.where(v_lane < HEAD_DIM, jnp.ones_like(v), v)]
        heads = range(4)
        s = [_dot_nt(q_ref[:, LANES * h:LANES * (h + 1)], k) for h in heads]
        if mask is not None:
            s = [jnp.where(mask, x, NEG_INF) for x in s]
        sink = [sink_ref[h] * LOG2_E for h in heads]
        m = [jnp.maximum(s[h].max(axis=-1, keepdims=True), sink[h]) for h in heads]
        p = [jnp.exp2(s[h] - m[h]).astype(BF16) for h in heads]
        o = [_dot(p[h], v_ones[h // 2]) for h in heads]
        den_lane = [HEAD_DIM if h // 2 == 0 else 0 for h in heads]
        outs = [o[h] / (o[h][:, den_lane[h]:den_lane[h] + 1] + jnp.exp2(sink[h] - m[h])) for h in heads]
        lane = lax.broadcasted_iota(jnp.int32, (tq, LANES), 1)
        for r in range(2):
            o_ref[:, LANES * r:LANES * (r + 1)] = jnp.where(lane < HEAD_DIM, outs[r], outs[2 + r]).astype(o_ref.dtype)

    @pl.when(t == 0)
    def _():
        attend(kc_ref[...], vc_ref[...], None)

    @pl.when(t > 0)
    def _():
        band = 2 * half + tq
        qi = lax.broadcasted_iota(jnp.int32, (tq, band + CTX_LEN), 0)
        col = lax.broadcasted_iota(jnp.int32, (tq, band + CTX_LEN), 1)
        in_window = jnp.abs(col - half - qi) <= SWA_WINDOW
        exists = ((col >= half) | (t > 1)) & ((col < half + tq) | (t < last))
        mask = (col >= band) | (in_window & exists)
        attend(jnp.concatenate([kp_ref[...], ko_ref[...], kn_ref[...], kc_ref[...]], axis=0),
               jnp.concatenate([vp_ref[...], vo_ref[...], vn_ref[...], vc_ref[...]], axis=0), mask)


def swa_mixer(pa, sink):
    b, l, _ = pa.shape
    tq = ROW_TILE
    nblk = l // SWA_WINDOW
    kcol, vcol = 4, 5
    prev = lambda c: (lambda i, t: (i, jnp.maximum(2 * t - 1, 2), c))
    nxt = lambda c: (lambda i, t: (i, jnp.minimum(2 * t + 2, nblk - 1), c))
    own = lambda c: (lambda i, t: (i, t, c))
    ctx = lambda c: (lambda i, t: (i, 0, c))
    kv_specs = lambda c: [pl.BlockSpec((None, SWA_WINDOW, LANES), prev(c)), pl.BlockSpec((None, tq, LANES), own(c)),
                          pl.BlockSpec((None, SWA_WINDOW, LANES), nxt(c)), pl.BlockSpec((None, tq, LANES), ctx(c))]
    return pl.pallas_call(
        _swa_kernel,
        grid=(b, l // tq),
        in_specs=[pl.BlockSpec(memory_space=pltpu.SMEM),
                  pl.BlockSpec((None, tq, 4 * LANES), lambda i, t: (i, t, 0))] + kv_specs(kcol) + kv_specs(vcol),
        out_specs=pl.BlockSpec((None, tq, 2 * LANES), lambda i, t: (i, t, 0)),
        out_shape=jax.ShapeDtypeStruct((b, l, 2 * LANES), BF16),
        compiler_params=pltpu.CompilerParams(
            dimension_semantics=("parallel", "parallel"), vmem_limit_bytes=VMEM_LIMIT),
        name="swa",
    )(sink, pa, pa, pa, pa, pa, pa, pa, pa, pa)


def _mla_prep_kernel(c_ref, qn_ref, kvn_ref, wq_ref, wqr_ref, wk_ref, wv_ref, cos_ref, sin_ref,
                     q_ref, k_ref, v_ref):
    cq = c_ref[:, 0:256]
    ckv = c_ref[:, 256:384]
    kr = c_ref[:, 384:512]
    nq = (_rms(cq) * qn_ref[...]).astype(BF16)
    nkv = (_rms(ckv) * kvn_ref[...]).astype(BF16)
    cos = jnp.concatenate([cos_ref[...]] * MLA_HEADS, axis=1)
    sin = jnp.concatenate([sin_ref[...]] * MLA_HEADS, axis=1)
    q_ref[...] = (_dot(nq, wq_ref[...]) * cos + _dot(nq, wqr_ref[...]) * sin).astype(BF16)
    k_ref[...] = (_dot(nkv, wk_ref[...]) + jnp.concatenate([kr] * MLA_HEADS, axis=1)).astype(BF16)
    v_ref[...] = _dot(nkv, wv_ref[...]).astype(BF16)


def mla_prep(pc, q_norm, kv_norm, w_q_up, w_kv_up, cos_c, sin_c):
    b, l, _ = pc.shape
    tm = ROW_TILE
    scale = (MLA_NOPE + MLA_ROPE) ** -0.5 * LOG2_E
    wq = (w_q_up * scale).reshape(-1, MLA_HEADS, MLA_NOPE + MLA_ROPE)
    zq = jnp.zeros(wq.shape[:2] + (LANES - MLA_NOPE - MLA_ROPE,), F32)
    wq_main = jnp.concatenate([wq, zq], axis=-1).reshape(-1, MLA_HEADS * LANES)
    wq_rot = jnp.concatenate([jnp.zeros_like(wq[..., :MLA_NOPE]), _rot_cols(wq[..., MLA_NOPE:], MLA_ROPE), zq],
                             axis=-1).reshape(-1, MLA_HEADS * LANES)
    wkv = w_kv_up.reshape(-1, MLA_HEADS, MLA_NOPE + MLA_V)
    wk = jnp.concatenate([wkv[..., :MLA_NOPE], jnp.zeros_like(wkv[..., :LANES - MLA_NOPE])],
                         axis=-1).reshape(-1, MLA_HEADS * LANES)
    wv = wkv[..., MLA_NOPE:].reshape(-1, MLA_HEADS * MLA_V)
    row = lambda i, t: (i, t, 0)
    const = lambda i, t: (0, 0)
    full = lambda a: pl.BlockSpec(a.shape, const)
    args = [q_norm[None, :], kv_norm[None, :], wq_main.astype(BF16), wq_rot.astype(BF16), wk.astype(BF16),
            wv.astype(BF16)]
    return pl.pallas_call(
        _mla_prep_kernel,
        grid=(b, l // tm),
        in_specs=[pl.BlockSpec((None, tm, C_W), row)] + [full(a) for a in args]
        + [pl.BlockSpec((tm, LANES), lambda i, t: (t, 0)), pl.BlockSpec((tm, LANES), lambda i, t: (t, 0))],
        out_specs=[pl.BlockSpec((None, tm, 4 * LANES), row), pl.BlockSpec((None, tm, 4 * LANES), row),
                   pl.BlockSpec((None, tm, 2 * LANES), row)],
        out_shape=[jax.ShapeDtypeStruct((b, l, 4 * LANES), BF16), jax.ShapeDtypeStruct((b, l, 4 * LANES), BF16),
                   jax.ShapeDtypeStruct((b, l, 2 * LANES), BF16)],
        compiler_params=pltpu.CompilerParams(dimension_semantics=("parallel", "parallel")),
        name="mla_prep",
    )(pc, *args, cos_c, sin_c)


def _mla_attn_kernel(q_ref, k_ref, v_ref, o_ref):
    t = pl.program_id(2)
    tq = q_ref.shape[0]

    def attend(nk):
        v = v_ref[0:nk, :]
        v_lane = lax.broadcasted_iota(jnp.int32, v.shape, 1)
        v_ones = [jnp.where(v_lane >= MLA_V, jnp.ones_like(v), v), jnp.where(v_lane < MLA_V, jnp.ones_like(v), v)]
        pair = range(2)
        s = [_dot_nt(q_ref[:, LANES * j:LANES * (j + 1)], k_ref[0:nk, LANES * j:LANES * (j + 1)]) for j in pair]
        p = [jnp.exp2(s[j] - s[j].max(axis=-1, keepdims=True)).astype(BF16) for j in pair]
        o = [_dot(p[j], v_ones[j]) for j in pair]
        outs = [o[j] / o[j][:, (MLA_V, 0)[j]:(MLA_V, 0)[j] + 1] for j in pair]
        lane = lax.broadcasted_iota(jnp.int32, (tq, LANES), 1)
        o_ref[...] = jnp.where(lane < MLA_V, outs[0], outs[1]).astype(o_ref.dtype)

    @pl.when(t == 0)
    def _():
        attend(CTX_LEN)

    @pl.when(t > 0)
    def _():
        attend(k_ref.shape[0])


def mla_attention(q, k, v):
    b, l, _ = q.shape
    tq = ROW_TILE
    return pl.pallas_call(
        _mla_attn_kernel,
        grid=(b, 2, l // tq),
        in_specs=[pl.BlockSpec((None, tq, 2 * LANES), lambda i, p, t: (i, t, p)),
                  pl.BlockSpec((None, l, 2 * LANES), lambda i, p, t: (i, 0, p)),
                  pl.BlockSpec((None, l, LANES), lambda i, p, t: (i, 0, p))],
        out_specs=pl.BlockSpec((None, tq, LANES), lambda i, p, t: (i, t, p)),
        out_shape=jax.ShapeDtypeStruct((b, l, 2 * LANES), BF16),
        compiler_params=pltpu.CompilerParams(
            dimension_semantics=("parallel", "parallel", "parallel"), vmem_limit_bytes=VMEM_LIMIT),
        name="mla_attn",
    )(q, k, v)


def _head_mean(x, ones_bd):
    hi = x.astype(BF16)
    lo = (x - hi.astype(F32)).astype(BF16)
    return (_dot(hi, ones_bd) + _dot(lo, ones_bd)) * (1.0 / HEAD_DIM)


def _ret_kernel(x_ref, lg_ref, g_ref, o_ref, s_ref, dec_ref, part_ref):
    dr = pl.program_id(0)
    s = pl.program_id(1)
    ns = pl.num_programs(1)
    nb, c = x_ref.shape[0], x_ref.shape[1]
    w = RET_HEADS * HEAD_DIM
    chunk = jnp.where(s == 0, 0, jnp.where(dr == 0, s, ns - s))
    lg = lg_ref[...]
    fwd = dr == 0
    row_h = lax.broadcasted_iota(jnp.int32, (w, w), 0) // HEAD_DIM
    col_h = lax.broadcasted_iota(jnp.int32, (w, w), 1) // HEAD_DIM
    same_head = row_h == col_h

    @pl.when(s == 0)
    def _():
        s_ref[...] = jnp.zeros_like(s_ref)
        i = lax.broadcasted_iota(jnp.int32, (c, c), 0)
        j = lax.broadcasted_iota(jnp.int32, (c, c), 1)
        rel = jnp.where(fwd, i - j, j - i)
        relf = jnp.maximum(rel, 0).astype(F32)
        for h in range(RET_HEADS):
            lg_h = lg_ref[0:1, HEAD_DIM * h:HEAD_DIM * h + 1]
            dec_ref[h] = jnp.where(rel >= 0, jnp.exp(lg_h * relf), 0.0)

    pos = lax.broadcasted_iota(jnp.int32, (c, 1), 0).astype(F32)
    q_dec = jnp.exp(lg * jnp.where(fwd, pos + 1.0, c - pos))
    k_dec = jnp.exp(lg * jnp.where(fwd, c - 1.0 - pos, pos))
    lane_h = lax.broadcasted_iota(jnp.int32, (c, w), 1) // HEAD_DIM
    bs = range(nb)
    q = [x_ref[i, :, 0:w] for i in bs]
    kf = [x_ref[i, :, w:2 * w] for i in bs]
    v = [x_ref[i, :, 2 * w:3 * w].astype(BF16) for i in bs]
    kb = [x.astype(BF16) for x in kf]
    acc = [_dot((q[i] * q_dec).astype(BF16), s_ref[i].astype(BF16)) for i in bs]
    for h in range(RET_HEADS):
        qh = [jnp.where(lane_h == h, q[i], 0.0).astype(BF16) for i in bs]
        a = [(_dot_nt(qh[i], kb[i]) * dec_ref[h]).astype(BF16) for i in bs]
        acc = [acc[i] + jnp.where(lane_h == h, _dot(a[i], v[i]), 0.0) for i in bs]
    kv = [_dot_tn((kf[i] * k_dec).astype(BF16), v[i]) for i in bs]
    chunk_dec = jnp.exp(lg * float(c))
    for i in bs:
        s_ref[i] = s_ref[i] * chunk_dec + jnp.where(same_head, kv[i], 0.0)

    rows = pl.ds(pl.multiple_of(chunk * c, c), c)

    @pl.when(dr == 0)
    def _():
        for i in bs:
            part_ref[i, rows, :] = acc[i]

    @pl.when(dr == 1)
    def _():
        ones_bd = jnp.where(same_head, 1.0, 0.0).astype(BF16)
        for i in bs:
            o = part_ref[i, rows, :] + acc[i]
            mu = _head_mean(o, ones_bd)
            var = _head_mean(jnp.square(o - mu), ones_bd)
            y = (o - mu) * lax.rsqrt(var + EPS) * g_ref[...]
            o_ref[i] = (y * _silu(x_ref[i, :, 3 * w:4 * w])).astype(o_ref.dtype)


def retention_mixer(pd, log_decay, norm_g):
    b, l, _ = pd.shape
    c = ROW_TILE
    ns = l // c
    w = RET_HEADS * HEAD_DIM
    lg = jnp.repeat(-jnp.exp(log_decay.astype(F32)), HEAD_DIM, axis=-1)[:, None, :]

    def chunk_of(dr, s):
        return jnp.where(s == 0, 0, jnp.where(dr == 0, s, ns - s))

    return pl.pallas_call(
        _ret_kernel,
        grid=(2, ns),
        in_specs=[pl.BlockSpec((b, c, D_W), lambda dr, s: (0, chunk_of(dr, s), 0)),
                  pl.BlockSpec((None, 1, w), lambda dr, s: (dr, 0, 0)),
                  pl.BlockSpec((1, w), lambda dr, s: (0, 0))],
        out_specs=pl.BlockSpec((b, c, w), lambda dr, s: (0, jnp.where(dr == 0, 0, chunk_of(dr, s)), 0)),
        out_shape=jax.ShapeDtypeStruct((b, l, w), BF16),
        scratch_shapes=[pltpu.VMEM((b, w, w), F32), pltpu.VMEM((RET_HEADS, c, c), F32), pltpu.VMEM((b, l, w), F32)],
        compiler_params=pltpu.CompilerParams(
            dimension_semantics=("arbitrary", "arbitrary"), vmem_limit_bytes=VMEM_LIMIT),
        name="retention",
    )(pd, lg, norm_g[None, :])


def _out_proj_kernel(ma_ref, mb_ref, mc_ref, md_ref, h_ref, *refs, with_router):
    mods, (g_ref, w_ref), rest = refs[:WIDE], refs[WIDE:WIDE + 2], refs[WIDE + 2:]
    if with_router:
        wr_ref, hn_ref, v_ref, lg_ref = rest
    else:
        hn_ref, v_ref = rest
    gw = 2 * LANES
    mix = functools.reduce(jnp.add, [
        _dot(m_ref[...].astype(BF16), w_ref[gw * i:gw * (i + 1), :])
        for i, m_ref in enumerate((ma_ref, mb_ref, mc_ref, md_ref))])
    for k, mod_ref in enumerate(mods):
        r = _sub_rows(k)
        hn = h_ref[r, :] + mod_ref[2:3, :] * mix[r]
        hn_ref[r, :] = hn
        v = _rms(hn) * g_ref[...] * (1.0 + mod_ref[4:5, :]) + mod_ref[3:4, :]
        v_ref[r, :] = v.astype(v_ref.dtype)
        if with_router:
            lg_ref[r, :] = jnp.dot(v, wr_ref[...], preferred_element_type=F32, precision=lax.Precision.HIGHEST)


def out_proj(mixes, h, mod, gain, w, layer, w_router=None):
    b, l, d = h.shape
    tm = WIDE * ROW_TILE
    n = b * l
    with_router = w_router is not None
    row = lambda j: (j, 0)
    const = lambda j: (0, 0)
    in_specs = [pl.BlockSpec((tm, 2 * LANES), row) for _ in mixes] + [pl.BlockSpec((tm, d), row)] + _sub_mod_specs(
        d, l // ROW_TILE) + [pl.BlockSpec((1, d), const), _layer_weight_spec(w, layer)]
    out_specs = [pl.BlockSpec((tm, d), row), pl.BlockSpec((tm, d), row)]
    out_shape = [jax.ShapeDtypeStruct((n, d), F32), jax.ShapeDtypeStruct((n, d), F32 if with_router else BF16)]
    args = [m.reshape(n, 2 * LANES) for m in mixes] + [h.reshape(n, d)] + [mod] * WIDE + [gain, w]
    if with_router:
        in_specs.append(pl.BlockSpec(w_router.shape, const))
        out_specs.append(pl.BlockSpec((tm, LANES), row))
        out_shape.append(jax.ShapeDtypeStruct((n, LANES), F32))
        args.append(w_router)
    outs = pl.pallas_call(
        functools.partial(_out_proj_kernel, with_router=with_router),
        grid=(n // tm,),
        in_specs=in_specs,
        out_specs=out_specs,
        out_shape=out_shape,
        compiler_params=pltpu.CompilerParams(dimension_semantics=("parallel",), vmem_limit_bytes=VMEM_LIMIT),
        name="out_proj",
    )(*args)
    return [o.reshape(b, l, -1) for o in outs]


def build_out_weight(w):
    hd = HEAD_DIM
    rows = lambda lo, hi: w[..., lo:hi, :]
    return jnp.concatenate([rows(0, hd), rows(2 * hd, 3 * hd), rows(hd, 2 * hd), rows(3 * hd, None)],
                           axis=-2).astype(BF16)


def _ffn_kernel(v_ref, h_ref, *refs):
    mods, (wg_ref, wu_ref, wd_ref, o_ref) = refs[:WIDE], refs[WIDE:]
    v = v_ref[...]
    acc = jnp.zeros(o_ref.shape, F32)
    for j in range(D_FF // FF_CHUNK):
        cols = slice(j * FF_CHUNK, (j + 1) * FF_CHUNK)
        a = _dot(v, wg_ref[:, cols])
        u = _dot(v, wu_ref[:, cols])
        mid = (_silu(a) * u).astype(BF16)
        acc = acc + _dot(mid, wd_ref[cols, :])
    for k, mod_ref in enumerate(mods):
        r = _sub_rows(k)
        o_ref[r, :] = h_ref[r, :] + mod_ref[5:6, :] * acc[r]


def dense_ffn(v, h, mod, layer, wg, wu, wd):
    b, l, d = h.shape
    tm = WIDE * ROW_TILE
    n = b * l
    row = lambda j: (j, 0)
    const = lambda j: (0, 0)
    return pl.pallas_call(
        _ffn_kernel,
        grid=(n // tm,),
        in_specs=[pl.BlockSpec((tm, d), row), pl.BlockSpec((tm, d), row)] + _sub_mod_specs(d, l // ROW_TILE) + [
            _layer_weight_spec(wg, layer), _layer_weight_spec(wu, layer), _layer_weight_spec(wd, layer)],
        out_specs=pl.BlockSpec((tm, d), row),
        out_shape=jax.ShapeDtypeStruct((n, d), F32),
        compiler_params=pltpu.CompilerParams(dimension_semantics=("parallel",), vmem_limit_bytes=VMEM_LIMIT),
        name="dense_ffn",
    )(v.reshape(n, d), h.reshape(n, d), *([mod] * WIDE), wg, wu, wd).reshape(b, l, d)


def _moe_kernel(wt_ref, we_ref, lo_ref, hi_ref, first_ref, x_ref, wg_ref, wu_ref, wd_ref, o_ref, xm_ref, acc_ref):
    w = pl.program_id(0)
    j = pl.program_id(1)
    nj = pl.num_programs(1)
    tm = x_ref.shape[0]

    @pl.when(j == 0)
    def _():
        row = wt_ref[w] * tm + lax.broadcasted_iota(jnp.int32, (tm, 1), 0)
        keep = (row >= lo_ref[w]) & (row < hi_ref[w])
        xm_ref[...] = jnp.where(keep, x_ref[...], 0.0).astype(BF16)

    @pl.when((j == 0) & (first_ref[w] > 0))
    def _():
        acc_ref[...] = jnp.zeros_like(acc_ref)

    def swiglu_rows(rows):
        x = xm_ref[rows, :]
        a = _dot(x, wg_ref[...])
        u = _dot(x, wu_ref[...])
        mid = (_silu(a) * u).astype(BF16)
        acc_ref[rows, :] += _dot(mid, wd_ref[...])

    tile_lo = wt_ref[w] * tm
    whole = (lo_ref[w] <= tile_lo) & (hi_ref[w] >= tile_lo + tm)

    @pl.when(whole)
    def _():
        swiglu_rows(slice(0, tm))

    half = tm // 2
    for part in range(2):
        part_lo = tile_lo + part * half

        @pl.when(jnp.logical_not(whole) & (hi_ref[w] > jnp.maximum(part_lo, lo_ref[w])) & (lo_ref[w] < part_lo + half))
        def _():
            swiglu_rows(slice(part * half, (part + 1) * half))

    @pl.when(j == nj - 1)
    def _():
        o_ref[...] = acc_ref[...]


def moe_grouped_ffn(xs, items, layer_idx, wg, wu, wd):
    s, d = xs.shape
    tm = MOE_TILE
    fc = MOE_FF_CHUNK
    nw = items[0].shape[0]
    nj = D_FF // fc
    grid_spec = pltpu.PrefetchScalarGridSpec(
        num_scalar_prefetch=5,
        grid=(nw, nj),
        in_specs=[
            pl.BlockSpec((tm, d), lambda w, j, wt, we, lo, hi, fi: (wt[w], 0)),
            pl.BlockSpec((None, None, d, fc), lambda w, j, wt, we, lo, hi, fi: (layer_idx, we[w], 0, j)),
            pl.BlockSpec((None, None, d, fc), lambda w, j, wt, we, lo, hi, fi: (layer_idx, we[w], 0, j)),
            pl.BlockSpec((None, None, fc, d), lambda w, j, wt, we, lo, hi, fi: (layer_idx, we[w], j, 0)),
        ],
        out_specs=pl.BlockSpec((tm, d), lambda w, j, wt, we, lo, hi, fi: (wt[w], 0)),
        scratch_shapes=[pltpu.VMEM((tm, d), BF16), pltpu.VMEM((tm, d), F32)],
    )
    return pl.pallas_call(
        _moe_kernel,
        grid_spec=grid_spec,
        out_shape=jax.ShapeDtypeStruct((s, d), F32),
        compiler_params=pltpu.CompilerParams(
            dimension_semantics=("arbitrary", "arbitrary"), vmem_limit_bytes=VMEM_LIMIT),
        name="moe_ffn",
    )(*items, xs, wg, wu, wd)


def _residual_kernel(h_ref, f0_ref, f1_ref, gate_ref, mod_ref, *rest):
    f = gate_ref[:, 0:1] * f0_ref[...] + gate_ref[:, 1:2] * f1_ref[...]
    hn = h_ref[...] + mod_ref[5:6, :] * f
    if len(rest) == 2:
        gain_ref, o_ref = rest
        o_ref[...] = _rms(hn) * gain_ref[...]
    else:
        rest[0][...] = hn


def gated_residual(h, f0, f1, gates, mod, final_gain=None):
    b, l, d = h.shape
    tm = ROW_TILE
    skip = 0 if final_gain is None else CTX_LEN // tm
    row = lambda i, t: (i, t + skip, 0)
    in_specs = [pl.BlockSpec((None, tm, d), row), pl.BlockSpec((None, tm, d), row), pl.BlockSpec((None, tm, d), row),
                pl.BlockSpec((None, tm, LANES), row),
                pl.BlockSpec((None, None, SUBLANES, d), lambda i, t: (i, jnp.minimum(t + skip, 1), 0, 0))]
    args = [h, f0, f1, gates, mod]
    if final_gain is not None:
        in_specs.append(pl.BlockSpec((1, d), lambda i, t: (0, 0)))
        args.append(final_gain)
    return pl.pallas_call(
        _residual_kernel,
        grid=(b, l // tm - skip),
        in_specs=in_specs,
        out_specs=pl.BlockSpec((None, tm, d), lambda i, t: (i, t, 0)),
        out_shape=jax.ShapeDtypeStruct((b, l - skip * tm, d), F32),
        compiler_params=pltpu.CompilerParams(dimension_semantics=("parallel", "parallel")),
        name="gated_residual",
    )(*args)


def moe_ffn(v, logits, h, mod, layer_idx, wg, wu, wd, final_gain=None):
    b, l, d = h.shape
    t = b * l
    s = TOP_K * t
    tm = MOE_TILE
    nt = s // tm
    nw = nt + N_EXPERTS - 1
    i32 = jnp.int32
    lg = logits.reshape(t, LANES)[:, :N_EXPERTS]
    top_val, top_idx = lax.top_k(lg, TOP_K)
    gates = jax.nn.softmax(top_val, axis=-1)
    slot = jnp.arange(s, dtype=i32)
    skey = jnp.sort(top_idx.reshape(-1).astype(i32) * s + slot)
    order = skey % s
    _, inv = lax.sort_key_val(order, slot)
    bounds = (jnp.arange(N_EXPERTS, dtype=i32) + 1) * s
    cum = jnp.sum((skey[None, :] < bounds[:, None]).astype(i32), axis=1)
    cum_prev = jnp.concatenate([jnp.zeros((1,), i32), cum[:-1]])
    tile_lo = jnp.arange(nt, dtype=i32) * tm
    count_le = lambda edges, x: jnp.sum((edges[None, :] <= x[:, None]).astype(i32), axis=1)
    e_first = count_le(cum, tile_lo)
    e_last = count_le(cum, tile_lo + tm - 1)
    n_items = e_last - e_first + 1
    item_end = jnp.cumsum(n_items)
    item_start = item_end - n_items
    w = jnp.arange(nw, dtype=i32)
    wt = jnp.minimum(count_le(item_end, w), nt - 1)
    valid = w < item_end[-1]
    we = jnp.clip(e_first[wt] + w - item_start[wt], 0, N_EXPERTS - 1).astype(i32)
    lo = jnp.where(valid, cum_prev[we], 0).astype(i32)
    hi = jnp.where(valid, cum[we], 0).astype(i32)
    first = (valid & (w == item_start[wt])).astype(i32)
    rows_of = lambda a, idx: a.at[idx].get(mode="promise_in_bounds")
    xs = rows_of(v.reshape(t, d), order // TOP_K)
    ys = moe_grouped_ffn(xs, (wt, we, lo, hi, first), layer_idx, wg, wu, wd)
    dest = inv.reshape(t, TOP_K)
    f0 = rows_of(ys, dest[:, 0]).reshape(b, l, d)
    f1 = rows_of(ys, dest[:, 1]).reshape(b, l, d)
    gates_p = jnp.pad(gates, ((0, 0), (0, LANES - TOP_K))).reshape(b, l, LANES)
    return gated_residual(h, f0, f1, gates_p, mod, final_gain)


def _final_norm_kernel(h_ref, g_ref, o_ref):
    o_ref[...] = _rms(h_ref[...]) * g_ref[...]


def final_rms_norm(h, gain, n_ctx_tiles):
    b, l, d = h.shape
    tm = ROW_TILE
    n = l - n_ctx_tiles * tm
    return pl.pallas_call(
        _final_norm_kernel,
        grid=(b, n // tm),
        in_specs=[
            pl.BlockSpec((None, tm, d), lambda i, t: (i, t + n_ctx_tiles, 0)),
            pl.BlockSpec((1, d), lambda i, t: (0, 0)),
        ],
        out_specs=pl.BlockSpec((None, tm, d), lambda i, t: (i, t, 0)),
        out_shape=jax.ShapeDtypeStruct((b, n, d), F32),
        compiler_params=pltpu.CompilerParams(dimension_semantics=("parallel", "parallel")),
        name="final_norm",
    )(h, gain)


GDN_W = GDN_HEADS * GDN_DK
GDN_CONV_K = 5
GDN_HALO = SUBLANES


def _split3(x):
    p0 = x.astype(BF16)
    r1 = x - p0.astype(F32)
    p1 = r1.astype(BF16)
    p2 = (r1 - p1.astype(F32)).astype(BF16)
    return p0, p1, p2


def _gdn_prep_kernel(x_ref, prev_ref, next_ref, cw_ref, par_ref, q_ref, k_ref, v_ref, gb_ref):
    t = pl.program_id(1)
    last = pl.num_programs(1) - 1
    tm = x_ref.shape[0]
    w3 = 3 * GDN_W
    has_prev = t > 1
    has_next = (t > 0) & (t < last)
    prev = jnp.where(has_prev, prev_ref[...], 0.0)
    nxt = jnp.where(has_next, next_ref[...], 0.0)
    xe = jnp.concatenate([prev, x_ref[:, :w3], nxt], axis=0)
    y = jnp.zeros((tm, w3), F32)
    for j in range(GDN_CONV_K):
        lo = GDN_HALO - GDN_CONV_K // 2 + j
        y = y + cw_ref[j:j + 1, :] * xe[lo:lo + tm, :]
    y = _silu(y)
    r = lax.broadcasted_iota(jnp.int32, (GDN_W, GDN_W), 0)
    c = lax.broadcasted_iota(jnp.int32, (GDN_W, GDN_W), 1)
    ones_bd = jnp.where(r // GDN_DK == c // GDN_DK, 1.0, 0.0).astype(BF16)

    def l2n(x):
        sq = x * x
        hi = sq.astype(BF16)
        lo = (sq - hi.astype(F32)).astype(BF16)
        return x * lax.rsqrt(_dot(hi, ones_bd) + _dot(lo, ones_bd) + EPS)

    q_ref[...] = l2n(y[:, :GDN_W]) * GDN_DK ** -0.5
    k_ref[...] = l2n(y[:, GDN_W:2 * GDN_W])
    v_ref[...] = y[:, 2 * GDN_W:]
    ab = x_ref[:, w3 + GDN_W:]
    lane = lax.broadcasted_iota(jnp.int32, ab.shape, 1)
    is_g = (lane % 8) < 4
    z = ab + par_ref[1:2, :]
    softplus = jnp.maximum(z, 0.0) + jnp.log1p(jnp.exp(-jnp.abs(z)))
    g = jnp.where(is_g, par_ref[0:1, :] * softplus, 0.0)
    beta = 1.0 / (1.0 + jnp.exp(-ab))
    i = lax.broadcasted_iota(jnp.int32, (tm, tm), 0)
    j = lax.broadcasted_iota(jnp.int32, (tm, tm), 1)
    same_chunk = i // GDN_CHUNK == j // GDN_CHUNK
    tri_f = jnp.where(same_chunk & (j <= i), 1.0, 0.0).astype(BF16)
    tri_b = jnp.where(same_chunk & (j >= i), 1.0, 0.0).astype(BF16)
    pieces = _split3(g)
    gc_f = functools.reduce(jnp.add, [_dot(tri_f, p) for p in pieces])
    gc_b = functools.reduce(jnp.add, [_dot(tri_b, p) for p in pieces])
    gb_ref[...] = jnp.where(is_g, jnp.where(lane < 8, gc_f, gc_b), beta)


def gdn_prep(pb, conv_w, a_log, dt_bias):
    b, l, _ = pb.shape
    tm = ROW_TILE
    w3 = 3 * GDN_W
    halo_blocks = tm // GDN_HALO
    n_halo = l // GDN_HALO
    cw = jnp.pad(conv_w, ((0, SUBLANES - GDN_CONV_K), (0, 0)))
    neg_a = jnp.pad(-jnp.exp(a_log.astype(F32)), ((0, 0), (0, 4))).reshape(-1)
    dtb = jnp.pad(dt_bias.astype(F32), ((0, 0), (0, 4))).reshape(-1)
    par = jnp.pad(jnp.stack([neg_a, dtb]), ((0, SUBLANES - 2), (0, LANES - 16)))
    row = lambda i, t: (i, t, 0)
    out = lambda w: pl.BlockSpec((None, tm, w), row)
    return pl.pallas_call(
        _gdn_prep_kernel,
        grid=(b, l // tm),
        in_specs=[pl.BlockSpec((None, tm, B_W), row),
                  pl.BlockSpec((None, GDN_HALO, w3), lambda i, t: (i, jnp.maximum(t * halo_blocks - 1, 0), 0)),
                  pl.BlockSpec((None, GDN_HALO, w3),
                               lambda i, t: (i, jnp.minimum((t + 1) * halo_blocks, n_halo - 1), 0)),
                  pl.BlockSpec(cw.shape, lambda i, t: (0, 0)),
                  pl.BlockSpec(par.shape, lambda i, t: (0, 0))],
        out_specs=[out(GDN_W), out(GDN_W), out(GDN_W), out(LANES)],
        out_shape=[jax.ShapeDtypeStruct((b, l, GDN_W), F32)] * 3 + [jax.ShapeDtypeStruct((b, l, LANES), F32)],
        compiler_params=pltpu.CompilerParams(
            dimension_semantics=("parallel", "parallel"), vmem_limit_bytes=VMEM_LIMIT),
        name="gdn_prep",
    )(pb, pb, pb, cw, par)


def _tile_heads(x):
    return jnp.concatenate([x] * GDN_HEADS, axis=0)


def _collapse_heads(x):
    c = GDN_CHUNK
    return x[0:c] + x[c:2 * c] + x[2 * c:3 * c] + x[3 * c:4 * c]


def _gdn_chunk_kernel(q_ref, k_ref, v_ref, gb_ref, o0_ref, qe_ref, a_ref, bm_ref, gam_ref):
    n = GDN_W
    cs = GDN_CHUNK
    r = lax.broadcasted_iota(jnp.int32, (n, n), 0)
    c = lax.broadcasted_iota(jnp.int32, (n, n), 1)
    ri, ci = r % cs, c % cs
    head = r // cs == c // cs
    eye = jnp.where(r == c, 1.0, 0.0)
    blk = lambda s: r // s == c // s
    b8, b16, b32 = blk(8), blk(16), blk(32)
    lane = lax.broadcasted_iota(jnp.int32, (n, LANES), 1)
    row_head = lax.broadcasted_iota(jnp.int32, (n, LANES), 0) // cs
    pick = lambda sel, x: jnp.sum(jnp.where(sel, x, 0.0), axis=1, keepdims=True)
    src_lane = lax.broadcasted_iota(jnp.int32, (LANES, n), 0)
    dst_head = lax.broadcasted_iota(jnp.int32, (LANES, n), 1) // cs
    tri, tri_strict, sel_g, sel_b, widen_g, widen_b = {}, {}, {}, {}, {}, {}
    for fwd in (True, False):
        ahead = ri - ci if fwd else ci - ri
        tri[fwd] = head & (ahead >= 0)
        tri_strict[fwd] = head & (ahead > 0)
        lane0 = 0 if fwd else 8
        sel_g[fwd] = lane == lane0 + row_head
        sel_b[fwd] = lane == lane0 + 4 + row_head
        widen_g[fwd] = jnp.where(src_lane == lane0 + dst_head, 1.0, 0.0).astype(BF16)
        widen_b[fwd] = jnp.where(src_lane == lane0 + 4 + dst_head, 1.0, 0.0).astype(BF16)

    n_chunks = q_ref.shape[0] // cs
    rows = [slice(ch * cs, (ch + 1) * cs) for ch in range(n_chunks)]
    items = [(fwd, ch) for fwd in (True, False) for ch in range(n_chunks)]
    dirs = [fwd for fwd, _ in items]
    per_item = lambda xs: [xs[ch] for _, ch in items]
    each = lambda f, *xs: [f(*a) for a in zip(*xs)]
    bf = lambda xs: [x.astype(BF16) for x in xs]
    widen = lambda x, e: functools.reduce(jnp.add, [_dot(p, e) for p in _split3(x)])
    zero_bf = jnp.zeros((n, n), BF16)
    spread = lambda xs: [jnp.where(head, _tile_heads(x.astype(BF16)), zero_bf) for x in xs]
    k_c, q_c, v_c = ([ref[rw, :] for rw in rows] for ref in (k_ref, q_ref, v_ref))
    khb_c, qhb_c = spread(k_c), spread(q_c)
    kk = per_item(each(_dot_nt, khb_c, khb_c))
    qk = per_item(each(_dot_nt, qhb_c, khb_c))
    k_t, q_t, v_t = per_item(k_c), per_item(q_c), per_item(v_c)
    gb = per_item([gb_ref[rw, :] for rw in rows])
    gb4 = [_tile_heads(x) for x in gb]
    gc = [pick(sel_g[f], x) for f, x in zip(dirs, gb4)]
    beta = [pick(sel_b[f], x) for f, x in zip(dirs, gb4)]
    gc_t = [widen(x, widen_g[f]) for f, x in zip(dirs, gb)]
    beta_t = [widen(x, widen_b[f]) for f, x in zip(dirs, gb)]
    ends = [cs - 1 if f else 0 for f in dirs]
    gl_t = [jnp.broadcast_to(x[e:e + 1, :], x.shape) for x, e in zip(gc_t, ends)]
    gc_b = [jnp.broadcast_to(x, (n, n)) for x in gc]
    decay = [jnp.exp(jnp.minimum(x - x.T, 0.0)) for x in gc_b]
    lmat = [jnp.where(tri_strict[f], b_ * kk_ * d_, 0.0) for f, b_, kk_, d_ in zip(dirs, beta, kk, decay)]
    attn = bf([jnp.where(tri[f], qk_ * d_, 0.0) for f, qk_, d_ in zip(dirs, qk, decay)])
    nl = bf([jnp.where(b8, -x, 0.0) for x in lmat])
    n2 = bf(each(_dot, nl, nl))
    n4 = each(_dot, n2, n2)
    p1 = bf(each(lambda a, b_: _dot((eye + a).astype(BF16), (eye + b_).astype(BF16)), nl, n2))
    tinv = each(lambda p, x: _dot(p, (eye + x).astype(BF16)), p1, n4)

    def moving_rows(fwd, x, sz):
        return jnp.concatenate([x[i:i + sz] for i in range(sz if fwd else 0, n, 2 * sz)], axis=0)

    def with_moving_rows(fwd, x, new, sz):
        pieces = []
        for j, i in enumerate(range(0, n, 2 * sz)):
            kept = x[i:i + sz] if fwd else x[i + sz:i + 2 * sz]
            moved = new[j * sz:(j + 1) * sz]
            pieces += [kept, moved] if fwd else [moved, kept]
        return jnp.concatenate(pieces, axis=0)

    for sz, inner, outer in ((8, b8, b16), (16, b16, b32), (32, b32, head)):
        off = bf([jnp.where(outer & ~inner, x, 0.0) for x in lmat])
        tb = bf(tinv)
        t_mv = [moving_rows(f, x, sz) for f, x in zip(dirs, tinv)]
        to = bf(each(_dot, bf(t_mv), off))
        tinv = [with_moving_rows(f, t_, tm_ - _dot(to_, tb_), sz)
                for f, t_, tm_, to_, tb_ in zip(dirs, tinv, t_mv, to, tb)]
    tb = bf(tinv)
    eg_t = [jnp.exp(x) for x in gc_t]
    u = bf(each(_dot, tb, spread(each(lambda b_, v_: b_ * v_, beta_t, v_t))))
    w = bf(each(_dot, tb, spread(each(lambda b_, e_, k_: (b_ * e_) * k_, beta_t, eg_t, k_t))))
    o0 = each(_dot, attn, u)
    aw = each(_dot, attn, w)
    kg = spread(each(lambda k_, gl_, gc_: k_ * jnp.exp(gl_ - gc_), k_t, gl_t, gc_t))
    a_mat = each(_dot_tn, kg, w)
    b_mat = each(_dot_tn, kg, u)
    for (fwd, ch), o0_, aw_, a_, b_, q_, e_, gl_ in zip(items, o0, aw, a_mat, b_mat, q_t, eg_t, gl_t):
        dr, rw = 0 if fwd else 1, rows[ch]
        o0_ref[dr, rw, :] = _collapse_heads(o0_)
        qe_ref[dr, rw, :] = q_ * e_ - _collapse_heads(aw_)
        a_ref[dr, rw, :] = _collapse_heads(a_)
        bm_ref[dr, rw, :] = _collapse_heads(b_)
        gam_ref[dr, rw, :] = jnp.exp(gl_)


def gdn_chunks(q, k, v, gb):
    b, l, _ = q.shape
    tm = ROW_TILE
    row = lambda i, t: (i, t, 0)
    out = pl.BlockSpec((2, None, tm, GDN_W), lambda i, t: (0, i, t, 0))
    return pl.pallas_call(
        _gdn_chunk_kernel,
        grid=(b, l // tm),
        in_specs=[pl.BlockSpec((None, tm, GDN_W), row)] * 3 + [pl.BlockSpec((None, tm, LANES), row)],
        out_specs=[out] * 5,
        out_shape=[jax.ShapeDtypeStruct((2, b, l, GDN_W), F32)] * 5,
        compiler_params=pltpu.CompilerParams(
            dimension_semantics=("parallel", "parallel"), vmem_limit_bytes=VMEM_LIMIT),
        name="gdn_chunk",
    )(q, k, v, gb)


def _gdn_scan_kernel(o0_ref, qe_ref, a_ref, bm_ref, gam_ref, gate_ref, g_ref, o_ref, s_ref, part_ref):
    dr = pl.program_id(0)
    s = pl.program_id(1)
    ns = pl.num_programs(1)
    nb, tm = o0_ref.shape[0], o0_ref.shape[1]
    cs = GDN_CHUNK
    n = GDN_W
    nch = tm // cs
    bs = range(nb)
    tile = jnp.where(s == 0, 0, jnp.where(dr == 0, s, ns - s))
    r = lax.broadcasted_iota(jnp.int32, (n, n), 0)
    c = lax.broadcasted_iota(jnp.int32, (n, n), 1)
    head = r // cs == c // cs

    @pl.when(s == 0)
    def _():
        s_ref[...] = jnp.zeros_like(s_ref)

    def run(order):
        state = [s_ref[i] for i in bs]
        outs = [{} for _ in bs]
        for ch in order:
            rows = slice(ch * cs, (ch + 1) * cs)
            sb = [x.astype(BF16) for x in state]
            for i in bs:
                outs[i][ch] = o0_ref[i, rows, :] + _dot(qe_ref[i, rows, :].astype(BF16), sb[i])
            a_full = [jnp.where(head, _tile_heads(a_ref[i, rows, :]), 0.0).astype(BF16) for i in bs]
            state = [_tile_heads(gam_ref[i, rows, :]) * state[i] - _dot(a_full[i], sb[i])
                     + jnp.where(head, _tile_heads(bm_ref[i, rows, :]), 0.0) for i in bs]
        for i in bs:
            s_ref[i] = state[i]
        return [jnp.concatenate([outs[i][ch] for ch in range(nch)], axis=0) for i in bs]

    rows_out = pl.ds(pl.multiple_of(tile * tm, tm), tm)

    @pl.when(dr == 0)
    def _():
        for i, o in enumerate(run(range(nch))):
            part_ref[i, rows_out, :] = o

    @pl.when(dr == 1)
    def _():
        ones_bd = jnp.where(head, 1.0, 0.0).astype(BF16)
        for i, o_bwd in enumerate(run(range(nch - 1, -1, -1))):
            o = part_ref[i, rows_out, :] + o_bwd
            ms = _head_mean(o * o, ones_bd)
            o_ref[i] = (o * lax.rsqrt(ms + EPS) * g_ref[...] * _silu(gate_ref[i])).astype(o_ref.dtype)


def gdn_scan(o0, qe, a, bm, gam, pb, norm_g):
    _, b, l, _ = o0.shape
    tm = ROW_TILE
    ns = l // tm

    def tile_of(dr, s):
        return jnp.where(s == 0, 0, jnp.where(dr == 0, s, ns - s))

    per_dir = pl.BlockSpec((None, b, tm, GDN_W), lambda dr, s: (dr, 0, tile_of(dr, s), 0))
    gate_col = 3 * GDN_W // GDN_W
    return pl.pallas_call(
        _gdn_scan_kernel,
        grid=(2, ns),
        in_specs=[per_dir] * 5 + [
            pl.BlockSpec((b, tm, GDN_W), lambda dr, s: (0, tile_of(dr, s), gate_col)),
            pl.BlockSpec((1, GDN_W), lambda dr, s: (0, 0))],
        out_specs=pl.BlockSpec((b, tm, GDN_W), lambda dr, s: (0, jnp.where(dr == 0, 0, tile_of(dr, s)), 0)),
        out_shape=jax.ShapeDtypeStruct((b, l, GDN_W), BF16),
        scratch_shapes=[pltpu.VMEM((b, GDN_W, GDN_W), F32), pltpu.VMEM((b, l, GDN_W), F32)],
        compiler_params=pltpu.CompilerParams(
            dimension_semantics=("arbitrary", "arbitrary"), vmem_limit_bytes=VMEM_LIMIT),
        name="gdn_scan",
    )(o0, qe, a, bm, gam, pb, jnp.tile(norm_g, GDN_HEADS)[None, :])


def gdn_mixer(pb, conv_w, a_log, dt_bias, norm_g):
    q, k, v, gb = gdn_prep(pb, conv_w, a_log, dt_bias)
    o0, qe, a, bm, gam = gdn_chunks(q, k, v, gb)
    return gdn_scan(o0, qe, a, bm, gam, pb, norm_g)


def kernel(x, c, ctx, c_ctx, w_mod, b_mod, norm1, norm2, w_in, w_out, swa_sink, gdn_conv, gdn_a_log, gdn_dt_bias, gdn_norm, mla_q_norm, mla_kv_norm, mla_w_q_up, mla_w_kv_up, ret_log_decay, ret_norm, ffn_w_gate, ffn_w_up, ffn_w_down, moe_router, moe_w_gate, moe_w_up, moe_w_down, final_norm):
    b, n, d = x.shape
    depth = w_in.shape[0]
    cos_t, sin_t = rope_tables(n)
    cos_c, sin_c = cos_t[:, A_ROT_W:A_ROT_W + C_ROT_W], sin_t[:, A_ROT_W:A_ROT_W + C_ROT_W]
    h = jnp.concatenate([ctx, x], axis=1)
    cond = jnp.concatenate([jax.nn.silu(c_ctx)[None, :], jax.nn.silu(c)], axis=0)
    mods = jnp.einsum("bd,ldk->lbk", cond, w_mod, precision=lax.Precision.HIGHEST) + b_mod[:, None, :]
    mods = mods.reshape(depth, 1 + b, 6, d)
    mods = jnp.stack([jnp.broadcast_to(mods[:, :1], (depth, b, 6, d)), mods[:, 1:]], axis=2)
    mods = jnp.pad(mods, ((0, 0), (0, 0), (0, 0), (0, SUBLANES - 6), (0, 0)))
    w_in_all = build_in_weight(w_in)
    w_out_all = build_out_weight(w_out)
    ffn_wg, ffn_wu, ffn_wd = ffn_w_gate.astype(BF16), ffn_w_up.astype(BF16), ffn_w_down.astype(BF16)
    moe_wg, moe_wu, moe_wd = moe_w_gate.astype(BF16), moe_w_up.astype(BF16), moe_w_down.astype(BF16)
    for layer in range(depth):
        mod = mods[layer]
        pa, pb, pc, pd = norm_proj(h, mod, norm1[layer][None, :], w_in_all, layer, cos_t, sin_t)
        mix_a = swa_mixer(pa, swa_sink[layer])
        mix_b = gdn_mixer(pb, gdn_conv[layer], gdn_a_log[layer], gdn_dt_bias[layer], gdn_norm[layer])
        mq, mk, mv = mla_prep(pc, mla_q_norm[layer], mla_kv_norm[layer], mla_w_q_up[layer], mla_w_kv_up[layer],
                              cos_c, sin_c)
        mix_c = mla_attention(mq, mk, mv)
        mix_d = retention_mixer(pd, ret_log_decay[layer], ret_norm[layer])
        mixes = (mix_a, mix_b, mix_c, mix_d)
        i = layer // 2
        if layer % 2 == 0:
            h, v = out_proj(mixes, h, mod, norm2[layer][None, :], w_out_all, layer)
            h = dense_ffn(v, h, mod, i, ffn_wg, ffn_wu, ffn_wd)
        else:
            w_r = jnp.pad(moe_router[i], ((0, 0), (0, LANES - N_EXPERTS)))
            h, v, logits = out_proj(mixes, h, mod, norm2[layer][None, :], w_out_all, layer, w_r)
            if layer == depth - 1:
                return moe_ffn(v, logits, h, mod, i, moe_wg, moe_wu, moe_wd, final_norm[None, :])
            h = moe_ffn(v, logits, h, mod, i, moe_wg, moe_wu, moe_wd)
    return final_rms_norm(h, final_norm[None, :], CTX_LEN // ROW_TILE)
```

```python
import functools

import numpy as np
import jax
import jax.numpy as jnp
from jax import lax
from jax.experimental import pallas as pl
from jax.experimental.pallas import tpu as pltpu

D_MODEL = 1024
GRID_W = 64
CTX_LEN = 256
HEAD_DIM = 64
ROPE_THETA = 10000.0
EPS = 1e-6
NEG_INF = -1e30

SWA_WINDOW = 128
GDN_HEADS = 4
GDN_DK = 64
GDN_DV = 64
GDN_CHUNK = 64
MLA_HEADS = 4
MLA_NOPE = 64
MLA_ROPE = 32
MLA_V = 64
RET_HEADS = 4
RET_DK = 64
D_FF = 3584
N_EXPERTS = 8
TOP_K = 2

LANES = 128
SUBLANES = 8
VMEM_LIMIT = 56 * 1024 * 1024

ROW_TILE = 256
FF_CHUNK = 512
MOE_TILE = 1024
MOE_PART = 256
MOE_FF_CHUNK = 512

A_W, B_W, C_W, D_W = 768, 1152, 512, 1024
A_ROT_W, C_ROT_W, D_ROT_W = 640, 128, 512
OFF_A = 0
OFF_B = OFF_A + A_W
OFF_C = OFF_B + B_W
OFF_D = OFF_C + C_W
OFF_AR = OFF_D + D_W
OFF_CR = OFF_AR + A_ROT_W
OFF_DR = OFF_CR + C_ROT_W
W_ALL = OFF_DR + D_ROT_W
ROPE_W = A_ROT_W + C_ROT_W + D_ROT_W

LOG2_E = float(np.log2(np.e))
F32 = jnp.float32
BF16 = jnp.bfloat16
NT_DIMS = (((1,), (1,)), ((), ()))
TN_DIMS = (((0,), (0,)), ((), ()))


def _rms(x):
    return x * lax.rsqrt(jnp.mean(x * x, axis=-1, keepdims=True) + EPS)


def _silu(x):
    return x * (1.0 / (1.0 + jnp.exp(-x)))


def _dot(a, b):
    return jnp.dot(a, b, preferred_element_type=F32)


def _dot_nt(a, b):
    return lax.dot_general(a, b, NT_DIMS, preferred_element_type=F32)


def _dot_tn(a, b):
    return lax.dot_general(a, b, TN_DIMS, preferred_element_type=F32)


WIDE = 2


def _sub_tile_specs(block, tiles_per_seq, index_of):
    def spec(k):
        return pl.BlockSpec(block, lambda j: index_of((WIDE * j + k) // tiles_per_seq, (WIDE * j + k) % tiles_per_seq))
    return [spec(k) for k in range(WIDE)]


def _sub_mod_specs(d, tiles_per_seq):
    return _sub_tile_specs((None, None, SUBLANES, d), tiles_per_seq, lambda b, t: (b, jnp.minimum(t, 1), 0, 0))


def _sub_rows(k):
    return slice(k * ROW_TILE, (k + 1) * ROW_TILE)


def _norm_proj_kernel(h_ref, *refs):
    mods, (g_ref, w_ref), tabs = refs[:WIDE], refs[WIDE:WIDE + 2], refs[WIDE + 2:3 * WIDE + 2]
    a_ref, b_ref, c_ref, d_ref = refs[3 * WIDE + 2:]
    y = _rms(h_ref[...]) * g_ref[...]
    u = jnp.concatenate([y[_sub_rows(k)] * (1.0 + m[1:2, :]) + m[0:1, :] for k, m in enumerate(mods)],
                        axis=0).astype(BF16)

    def mm(lo, width):
        return _dot(u, w_ref[:, lo:lo + width])

    a_main = mm(OFF_A, A_W)
    a_rot = mm(OFF_AR, A_ROT_W)
    b_ref[...] = mm(OFF_B, B_W)
    c_main = mm(OFF_C, C_W)
    c_rot = mm(OFF_CR, C_ROT_W)
    d_main = mm(OFF_D, D_W)
    d_rot = mm(OFF_DR, D_ROT_W)
    lo, hi = A_ROT_W, A_ROT_W + C_ROT_W
    a_ref[:, A_ROT_W:] = a_main[:, A_ROT_W:].astype(BF16)
    c_ref[:, :C_W - C_ROT_W] = c_main[:, :C_W - C_ROT_W]
    d_ref[:, D_ROT_W:] = d_main[:, D_ROT_W:]
    for k in range(WIDE):
        cos_ref, sin_ref = tabs[2 * k], tabs[2 * k + 1]
        r = _sub_rows(k)
        a_ref[r, :A_ROT_W] = (a_main[r, :A_ROT_W] * cos_ref[:, :A_ROT_W] + a_rot[r] * sin_ref[:, :A_ROT_W]).astype(BF16)
        c_ref[r, C_W - C_ROT_W:] = c_main[r, C_W - C_ROT_W:] * cos_ref[:, lo:hi] + c_rot[r] * sin_ref[:, lo:hi]
        d_ref[r, :D_ROT_W] = d_main[r, :D_ROT_W] * cos_ref[:, hi:] + d_rot[r] * sin_ref[:, hi:]


def _layer_weight_spec(w, layer):
    return pl.BlockSpec((None,) + w.shape[1:], lambda j: (layer, 0, 0), pipeline_mode=pl.Buffered(1))


def norm_proj(h, mod, gain, w, layer, cos_t, sin_t):
    b, l, d = h.shape
    tm = WIDE * ROW_TILE
    tps = l // ROW_TILE
    row = lambda j: (j, 0)
    const = lambda j: (0, 0)
    tabs = _sub_tile_specs((ROW_TILE, ROPE_W), tps, lambda bi, t: (t, 0))
    tab_specs = [s for pair in zip(tabs, _sub_tile_specs((ROW_TILE, ROPE_W), tps, lambda bi, t: (t, 0))) for s in pair]
    outs = pl.pallas_call(
        _norm_proj_kernel,
        grid=(b * l // tm,),
        in_specs=[pl.BlockSpec((tm, d), row)] + _sub_mod_specs(d, tps) + [
            pl.BlockSpec((1, d), const),
            _layer_weight_spec(w, layer)] + tab_specs,
        out_specs=[pl.BlockSpec((tm, A_W), row), pl.BlockSpec((tm, B_W), row),
                   pl.BlockSpec((tm, C_W), row), pl.BlockSpec((tm, D_W), row)],
        out_shape=[jax.ShapeDtypeStruct((b * l, A_W), BF16), jax.ShapeDtypeStruct((b * l, B_W), F32),
                   jax.ShapeDtypeStruct((b * l, C_W), F32), jax.ShapeDtypeStruct((b * l, D_W), F32)],
        compiler_params=pltpu.CompilerParams(dimension_semantics=("parallel",), vmem_limit_bytes=VMEM_LIMIT),
        name="norm_proj",
    )(h.reshape(b * l, d), *([mod] * WIDE), gain, w, *([cos_t, sin_t] * WIDE))
    return [o.reshape(b, l, -1) for o in outs]


def _rot_cols(w, hd):
    x = w.reshape(w.shape[:-1] + (w.shape[-1] // hd, 4, hd // 4))
    x1, x2, x3, x4 = x[..., 0, :], x[..., 1, :], x[..., 2, :], x[..., 3, :]
    return jnp.stack([-x2, x1, -x4, x3], axis=-2).reshape(w.shape)


def _place_swa_q(q):
    z = jnp.zeros(q.shape[:-1] + (HEAD_DIM,), q.dtype)
    blocks = []
    for h in range(4):
        qh = q[..., HEAD_DIM * h:HEAD_DIM * (h + 1)]
        blocks += [qh, z] if h // 2 == 0 else [z, qh]
    return jnp.concatenate(blocks, axis=-1)


def build_in_weight(w):
    o = [int(v) for v in np.cumsum((256, 128, 128, 768, 256, 16, 256, 128, 32, 256, 256, 256, 256))]
    aq, ak, av = w[..., :o[0]] * (HEAD_DIM ** -0.5 * LOG2_E), w[..., o[0]:o[1]], w[..., o[1]:o[2]]
    b_main, b_ab = w[..., o[2]:o[4]], w[..., o[4]:o[5]]
    c_q, c_kv, c_kr = w[..., o[5]:o[6]], w[..., o[6]:o[7]], w[..., o[7]:o[8]]
    dq, dk, dvg = w[..., o[8]:o[9]], w[..., o[9]:o[10]] * RET_DK ** -0.5, w[..., o[10]:]
    z = lambda n: jnp.zeros(w.shape[:-1] + (n,), w.dtype)
    parts = [
        _place_swa_q(aq), ak, av,
        b_main, b_ab, z(LANES - b_ab.shape[-1]),
        c_q, c_kv, z(64), c_kr, z(32),
        dq, dk, dvg,
        _place_swa_q(_rot_cols(aq, HEAD_DIM)), _rot_cols(ak, HEAD_DIM),
        z(64), _rot_cols(c_kr, MLA_ROPE), z(32),
        _rot_cols(dq, HEAD_DIM), _rot_cols(dk, HEAD_DIM),
    ]
    out = jnp.concatenate(parts, axis=-1)
    assert out.shape[-1] == W_ALL
    return out.astype(BF16)


def rope_tables(n):
    lat = jnp.arange(CTX_LEN + n, dtype=jnp.int32) - CTX_LEN
    grid_row = jnp.where(lat >= 0, lat // GRID_W, 0).astype(F32)[:, None]
    grid_col = jnp.where(lat >= 0, lat % GRID_W, 0).astype(F32)[:, None]
    narrow, col_of = [], {}
    for rot_dim in (HEAD_DIM, MLA_ROPE):
        n_freq = rot_dim // 4
        inv_freq = ROPE_THETA ** (-jnp.arange(n_freq, dtype=F32) / n_freq)
        for axis, pos in enumerate((grid_row, grid_col)):
            col_of[rot_dim, axis] = sum(a.shape[1] for a in narrow)
            narrow.append(pos * inv_freq)
    ang = jnp.concatenate(narrow, axis=1)
    identity_col = ang.shape[1]
    sel = np.zeros((identity_col + 1, ROPE_W), np.float32)

    def plan(lane0, width, group, rot_lo, rot_dim):
        for c in range(width):
            j = c % group - rot_lo
            if 0 <= j < rot_dim:
                quarter, f = divmod(j, rot_dim // 4)
                sel[col_of[rot_dim, quarter // 2] + f, lane0 + c] = 1.0
            else:
                sel[identity_col, lane0 + c] = 1.0

    plan(0, A_ROT_W, HEAD_DIM, 0, HEAD_DIM)
    plan(A_ROT_W, C_ROT_W, C_ROT_W, 64, MLA_ROPE)
    plan(A_ROT_W + C_ROT_W, D_ROT_W, HEAD_DIM, 0, HEAD_DIM)
    spread = lambda t: jnp.dot(t, jnp.asarray(sel), precision=lax.Precision.HIGHEST)
    ones, zeros = jnp.ones_like(grid_row), jnp.zeros_like(grid_row)
    return (spread(jnp.concatenate([jnp.cos(ang), ones], axis=1)),
            spread(jnp.concatenate([jnp.sin(ang), zeros], axis=1)))


def _swa_kernel(sink_ref, q_ref, kp_ref, ko_ref, kn_ref, kc_ref, vp_ref, vo_ref, vn_ref, vc_ref, o_ref):
    t = pl.program_id(1)
    last = pl.num_programs(1) - 1
    tq = q_ref.shape[0]
    half = tq // 2

    def attend(k, v, mask):
        v_lane = lax.broadcasted_iota(jnp.int32, v.shape, 1)
        v_ones = [jnp.where(v_lane >= HEAD_DIM, jnp.ones_like(v), v), jnp.where(v_lane < HEAD_DIM, jnp.ones_like(v), v)]
        heads = range(4)
        s = [_dot_nt(q_ref[:, LANES * h:LANES * (h + 1)], k) for h in heads]
        if mask is not None:
            s = [jnp.where(mask, x, NEG_INF) for x in s]
        sink = [sink_ref[h] * LOG2_E for h in heads]
        m = [jnp.maximum(s[h].max(axis=-1, keepdims=True), sink[h]) for h in heads]
        p = [jnp.exp2(s[h] - m[h]).astype(BF16) for h in heads]
        o = [_dot(p[h], v_ones[h // 2]) for h in heads]
        den_lane = [HEAD_DIM if h // 2 == 0 else 0 for h in heads]
        outs = [o[h] / (o[h][:, den_lane[h]:den_lane[h] + 1] + jnp.exp2(sink[h] - m[h])) for h in heads]
        lane = lax.broadcasted_iota(jnp.int32, (tq, LANES), 1)
        for r in range(2):
            o_ref[:, LANES * r:LANES * (r + 1)] = jnp.where(lane < HEAD_DIM, outs[r], outs[2 + r]).astype(o_ref.dtype)

    @pl.when(t == 0)
    def _():
        attend(kc_ref[...], vc_ref[...], None)

    @pl.when(t > 0)
    def _():
        band = 2 * half + tq
        qi = lax.broadcasted_iota(jnp.int32, (tq, band + CTX_LEN), 0)
        col = lax.broadcasted_iota(jnp.int32, (tq, band + CTX_LEN), 1)
        in_window = jnp.abs(col - half - qi) <= SWA_WINDOW
        exists = ((col >= half) | (t > 1)) & ((col < half + tq) | (t < last))
        mask = (col >= band) | (in_window & exists)
        attend(jnp.concatenate([kp_ref[...], ko_ref[...], kn_ref[...], kc_ref[...]], axis=0),
               jnp.concatenate([vp_ref[...], vo_ref[...], vn_ref[...], vc_ref[...]], axis=0), mask)


def swa_mixer(pa, sink):
    b, l, _ = pa.shape
    tq = ROW_TILE
    nblk = l // SWA_WINDOW
    kcol, vcol = 4, 5
    prev = lambda c: (lambda i, t: (i, jnp.maximum(2 * t - 1, 2), c))
    nxt = lambda c: (lambda i, t: (i, jnp.minimum(2 * t + 2, nblk - 1), c))
    own = lambda c: (lambda i, t: (i, t, c))
    ctx = lambda c: (lambda i, t: (i, 0, c))
    kv_specs = lambda c: [pl.BlockSpec((None, SWA_WINDOW, LANES), prev(c)), pl.BlockSpec((None, tq, LANES), own(c)),
                          pl.BlockSpec((None, SWA_WINDOW, LANES), nxt(c)), pl.BlockSpec((None, tq, LANES), ctx(c))]
    return pl.pallas_call(
        _swa_kernel,
        grid=(b, l // tq),
        in_specs=[pl.BlockSpec(memory_space=pltpu.SMEM),
                  pl.BlockSpec((None, tq, 4 * LANES), lambda i, t: (i, t, 0))] + kv_specs(kcol) + kv_specs(vcol),
        out_specs=pl.BlockSpec((None, tq, 2 * LANES), lambda i, t: (i, t, 0)),
        out_shape=jax.ShapeDtypeStruct((b, l, 2 * LANES), BF16),
        compiler_params=pltpu.CompilerParams(
            dimension_semantics=("parallel", "parallel"), vmem_limit_bytes=VMEM_LIMIT),
        name="swa",
    )(sink, pa, pa, pa, pa, pa, pa, pa, pa, pa)


def _mla_prep_kernel(c_ref, qn_ref, kvn_ref, wq_ref, wqr_ref, wk_ref, wv_ref, cos_ref, sin_ref,
                     q_ref, k_ref, v_ref):
    cq = c_ref[:, 0:256]
    ckv = c_ref[:, 256:384]
    kr = c_ref[:, 384:512]
    nq = (_rms(cq) * qn_ref[...]).astype(BF16)
    nkv = (_rms(ckv) * kvn_ref[...]).astype(BF16)
    cos = jnp.concatenate([cos_ref[...]] * MLA_HEADS, axis=1)
    sin = jnp.concatenate([sin_ref[...]] * MLA_HEADS, axis=1)
    q_ref[...] = (_dot(nq, wq_ref[...]) * cos + _dot(nq, wqr_ref[...]) * sin).astype(BF16)
    k_ref[...] = (_dot(nkv, wk_ref[...]) + jnp.concatenate([kr] * MLA_HEADS, axis=1)).astype(BF16)
    v_ref[...] = _dot(nkv, wv_ref[...]).astype(BF16)


def mla_prep(pc, q_norm, kv_norm, w_q_up, w_kv_up, cos_c, sin_c):
    b, l, _ = pc.shape
    tm = ROW_TILE
    scale = (MLA_NOPE + MLA_ROPE) ** -0.5 * LOG2_E
    wq = (w_q_up * scale).reshape(-1, MLA_HEADS, MLA_NOPE + MLA_ROPE)
    zq = jnp.zeros(wq.shape[:2] + (LANES - MLA_NOPE - MLA_ROPE,), F32)
    wq_main = jnp.concatenate([wq, zq], axis=-1).reshape(-1, MLA_HEADS * LANES)
    wq_rot = jnp.concatenate([jnp.zeros_like(wq[..., :MLA_NOPE]), _rot_cols(wq[..., MLA_NOPE:], MLA_ROPE), zq],
                             axis=-1).reshape(-1, MLA_HEADS * LANES)
    wkv = w_kv_up.reshape(-1, MLA_HEADS, MLA_NOPE + MLA_V)
    wk = jnp.concatenate([wkv[..., :MLA_NOPE], jnp.zeros_like(wkv[..., :LANES - MLA_NOPE])],
                         axis=-1).reshape(-1, MLA_HEADS * LANES)
    wv = wkv[..., MLA_NOPE:].reshape(-1, MLA_HEADS * MLA_V)
    row = lambda i, t: (i, t, 0)
    const = lambda i, t: (0, 0)
    full = lambda a: pl.BlockSpec(a.shape, const)
    args = [q_norm[None, :], kv_norm[None, :], wq_main.astype(BF16), wq_rot.astype(BF16), wk.astype(BF16),
            wv.astype(BF16)]
    return pl.pallas_call(
        _mla_prep_kernel,
        grid=(b, l // tm),
        in_specs=[pl.BlockSpec((None, tm, C_W), row)] + [full(a) for a in args]
        + [pl.BlockSpec((tm, LANES), lambda i, t: (t, 0)), pl.BlockSpec((tm, LANES), lambda i, t: (t, 0))],
        out_specs=[pl.BlockSpec((None, tm, 4 * LANES), row), pl.BlockSpec((None, tm, 4 * LANES), row),
                   pl.BlockSpec((None, tm, 2 * LANES), row)],
        out_shape=[jax.ShapeDtypeStruct((b, l, 4 * LANES), BF16), jax.ShapeDtypeStruct((b, l, 4 * LANES), BF16),
                   jax.ShapeDtypeStruct((b, l, 2 * LANES), BF16)],
        compiler_params=pltpu.CompilerParams(dimension_semantics=("parallel", "parallel")),
        name="mla_prep",
    )(pc, *args, cos_c, sin_c)


def _mla_attn_kernel(q_ref, k_ref, v_ref, o_ref):
    t = pl.program_id(2)
    tq = q_ref.shape[0]

    def attend(nk):
        v = v_ref[0:nk, :]
        v_lane = lax.broadcasted_iota(jnp.int32, v.shape, 1)
        v_ones = [jnp.where(v_lane >= MLA_V, jnp.ones_like(v), v), jnp.where(v_lane < MLA_V, jnp.ones_like(v), v)]
        pair = range(2)
        s = [_dot_nt(q_ref[:, LANES * j:LANES * (j + 1)], k_ref[0:nk, LANES * j:LANES * (j + 1)]) for j in pair]
        p = [jnp.exp2(s[j] - s[j].max(axis=-1, keepdims=True)).astype(BF16) for j in pair]
        o = [_dot(p[j], v_ones[j]) for j in pair]
        outs = [o[j] / o[j][:, (MLA_V, 0)[j]:(MLA_V, 0)[j] + 1] for j in pair]
        lane = lax.broadcasted_iota(jnp.int32, (tq, LANES), 1)
        o_ref[...] = jnp.where(lane < MLA_V, outs[0], outs[1]).astype(o_ref.dtype)

    @pl.when(t == 0)
    def _():
        attend(CTX_LEN)

    @pl.when(t > 0)
    def _():
        attend(k_ref.shape[0])


def mla_attention(q, k, v):
    b, l, _ = q.shape
    tq = ROW_TILE
    return pl.pallas_call(
        _mla_attn_kernel,
        grid=(b, 2, l // tq),
        in_specs=[pl.BlockSpec((None, tq, 2 * LANES), lambda i, p, t: (i, t, p)),
                  pl.BlockSpec((None, l, 2 * LANES), lambda i, p, t: (i, 0, p)),
                  pl.BlockSpec((None, l, LANES), lambda i, p, t: (i, 0, p))],
        out_specs=pl.BlockSpec((None, tq, LANES), lambda i, p, t: (i, t, p)),
        out_shape=jax.ShapeDtypeStruct((b, l, 2 * LANES), BF16),
        compiler_params=pltpu.CompilerParams(
            dimension_semantics=("parallel", "parallel", "parallel"), vmem_limit_bytes=VMEM_LIMIT),
        name="mla_attn",
    )(q, k, v)


def _head_mean(x, ones_bd):
    hi = x.astype(BF16)
    lo = (x - hi.astype(F32)).astype(BF16)
    return (_dot(hi, ones_bd) + _dot(lo, ones_bd)) * (1.0 / HEAD_DIM)


def _ret_kernel(x_ref, lg_ref, g_ref, o_ref, s_ref, dec_ref, part_ref):
    dr = pl.program_id(0)
    s = pl.program_id(1)
    ns = pl.num_programs(1)
    nb, c = x_ref.shape[0], x_ref.shape[1]
    w = RET_HEADS * HEAD_DIM
    chunk = jnp.where(s == 0, 0, jnp.where(dr == 0, s, ns - s))
    lg = lg_ref[...]
    fwd = dr == 0
    row_h = lax.broadcasted_iota(jnp.int32, (w, w), 0) // HEAD_DIM
    col_h = lax.broadcasted_iota(jnp.int32, (w, w), 1) // HEAD_DIM
    same_head = row_h == col_h

    @pl.when(s == 0)
    def _():
        s_ref[...] = jnp.zeros_like(s_ref)
        i = lax.broadcasted_iota(jnp.int32, (c, c), 0)
        j = lax.broadcasted_iota(jnp.int32, (c, c), 1)
        rel = jnp.where(fwd, i - j, j - i)
        relf = jnp.maximum(rel, 0).astype(F32)
        for h in range(RET_HEADS):
            lg_h = lg_ref[0:1, HEAD_DIM * h:HEAD_DIM * h + 1]
            dec_ref[h] = jnp.where(rel >= 0, jnp.exp(lg_h * relf), 0.0)

    pos = lax.broadcasted_iota(jnp.int32, (c, 1), 0).astype(F32)
    q_dec = jnp.exp(lg * jnp.where(fwd, pos + 1.0, c - pos))
    k_dec = jnp.exp(lg * jnp.where(fwd, c - 1.0 - pos, pos))
    lane_h = lax.broadcasted_iota(jnp.int32, (c, w), 1) // HEAD_DIM
    bs = range(nb)
    q = [x_ref[i, :, 0:w] for i in bs]
    kf = [x_ref[i, :, w:2 * w] for i in bs]
    v = [x_ref[i, :, 2 * w:3 * w].astype(BF16) for i in bs]
    kb = [x.astype(BF16) for x in kf]
    acc = [_dot((q[i] * q_dec).astype(BF16), s_ref[i].astype(BF16)) for i in bs]
    for h in range(RET_HEADS):
        qh = [jnp.where(lane_h == h, q[i], 0.0).astype(BF16) for i in bs]
        a = [(_dot_nt(qh[i], kb[i]) * dec_ref[h]).astype(BF16) for i in bs]
        acc = [acc[i] + jnp.where(lane_h == h, _dot(a[i], v[i]), 0.0) for i in bs]
    kv = [_dot_tn((kf[i] * k_dec).astype(BF16), v[i]) for i in bs]
    chunk_dec = jnp.exp(lg * float(c))
    for i in bs:
        s_ref[i] = s_ref[i] * chunk_dec + jnp.where(same_head, kv[i], 0.0)

    rows = pl.ds(pl.multiple_of(chunk * c, c), c)

    @pl.when(dr == 0)
    def _():
        for i in bs:
            part_ref[i, rows, :] = acc[i]

    @pl.when(dr == 1)
    def _():
        ones_bd = jnp.where(same_head, 1.0, 0.0).astype(BF16)
        for i in bs:
            o = part_ref[i, rows, :] + acc[i]
            mu = _head_mean(o, ones_bd)
            var = _head_mean(jnp.square(o - mu), ones_bd)
            y = (o - mu) * lax.rsqrt(var + EPS) * g_ref[...]
            o_ref[i] = (y * _silu(x_ref[i, :, 3 * w:4 * w])).astype(o_ref.dtype)


def retention_mixer(pd, log_decay, norm_g):
    b, l, _ = pd.shape
    c = ROW_TILE
    ns = l // c
    w = RET_HEADS * HEAD_DIM
    lg = jnp.repeat(-jnp.exp(log_decay.astype(F32)), HEAD_DIM, axis=-1)[:, None, :]

    def chunk_of(dr, s):
        return jnp.where(s == 0, 0, jnp.where(dr == 0, s, ns - s))

    return pl.pallas_call(
        _ret_kernel,
        grid=(2, ns),
        in_specs=[pl.BlockSpec((b, c, D_W), lambda dr, s: (0, chunk_of(dr, s), 0)),
                  pl.BlockSpec((None, 1, w), lambda dr, s: (dr, 0, 0)),
                  pl.BlockSpec((1, w), lambda dr, s: (0, 0))],
        out_specs=pl.BlockSpec((b, c, w), lambda dr, s: (0, jnp.where(dr == 0, 0, chunk_of(dr, s)), 0)),
        out_shape=jax.ShapeDtypeStruct((b, l, w), BF16),
        scratch_shapes=[pltpu.VMEM((b, w, w), F32), pltpu.VMEM((RET_HEADS, c, c), F32), pltpu.VMEM((b, l, w), F32)],
        compiler_params=pltpu.CompilerParams(
            dimension_semantics=("arbitrary", "arbitrary"), vmem_limit_bytes=VMEM_LIMIT),
        name="retention",
    )(pd, lg, norm_g[None, :])


def _out_proj_kernel(ma_ref, mb_ref, mc_ref, md_ref, h_ref, *refs, with_router):
    mods, (g_ref, w_ref), rest = refs[:WIDE], refs[WIDE:WIDE + 2], refs[WIDE + 2:]
    if with_router:
        wr_ref, hn_ref, v_ref, lg_ref = rest
    else:
        hn_ref, v_ref = rest
    gw = 2 * LANES
    mix = functools.reduce(jnp.add, [
        _dot(m_ref[...].astype(BF16), w_ref[gw * i:gw * (i + 1), :])
        for i, m_ref in enumerate((ma_ref, mb_ref, mc_ref, md_ref))])
    for k, mod_ref in enumerate(mods):
        r = _sub_rows(k)
        hn = h_ref[r, :] + mod_ref[2:3, :] * mix[r]
        hn_ref[r, :] = hn
        v = _rms(hn) * g_ref[...] * (1.0 + mod_ref[4:5, :]) + mod_ref[3:4, :]
        v_ref[r, :] = v.astype(v_ref.dtype)
        if with_router:
            lg_ref[r, :] = jnp.dot(v, wr_ref[...], preferred_element_type=F32, precision=lax.Precision.HIGHEST)


def out_proj(mixes, h, mod, gain, w, layer, w_router=None):
    b, l, d = h.shape
    tm = WIDE * ROW_TILE
    n = b * l
    with_router = w_router is not None
    row = lambda j: (j, 0)
    const = lambda j: (0, 0)
    in_specs = [pl.BlockSpec((tm, 2 * LANES), row) for _ in mixes] + [pl.BlockSpec((tm, d), row)] + _sub_mod_specs(
        d, l // ROW_TILE) + [pl.BlockSpec((1, d), const), _layer_weight_spec(w, layer)]
    out_specs = [pl.BlockSpec((tm, d), row), pl.BlockSpec((tm, d), row)]
    out_shape = [jax.ShapeDtypeStruct((n, d), F32), jax.ShapeDtypeStruct((n, d), F32 if with_router else BF16)]
    args = [m.reshape(n, 2 * LANES) for m in mixes] + [h.reshape(n, d)] + [mod] * WIDE + [gain, w]
    if with_router:
        in_specs.append(pl.BlockSpec(w_router.shape, const))
        out_specs.append(pl.BlockSpec((tm, LANES), row))
        out_shape.append(jax.ShapeDtypeStruct((n, LANES), F32))
        args.append(w_router)
    outs = pl.pallas_call(
        functools.partial(_out_proj_kernel, with_router=with_router),
        grid=(n // tm,),
        in_specs=in_specs,
        out_specs=out_specs,
        out_shape=out_shape,
        compiler_params=pltpu.CompilerParams(dimension_semantics=("parallel",), vmem_limit_bytes=VMEM_LIMIT),
        name="out_proj",
    )(*args)
    return [o.reshape(b, l, -1) for o in outs]


def build_out_weight(w):
    hd = HEAD_DIM
    rows = lambda lo, hi: w[..., lo:hi, :]
    return jnp.concatenate([rows(0, hd), rows(2 * hd, 3 * hd), rows(hd, 2 * hd), rows(3 * hd, None)],
                           axis=-2).astype(BF16)


def _ffn_kernel(v_ref, h_ref, *refs):
    mods, (wg_ref, wu_ref, wd_ref, o_ref) = refs[:WIDE], refs[WIDE:]
    v = v_ref[...]
    acc = jnp.zeros(o_ref.shape, F32)
    for j in range(D_FF // FF_CHUNK):
        cols = slice(j * FF_CHUNK, (j + 1) * FF_CHUNK)
        a = _dot(v, wg_ref[:, cols])
        u = _dot(v, wu_ref[:, cols])
        mid = (_silu(a) * u).astype(BF16)
        acc = acc + _dot(mid, wd_ref[cols, :])
    for k, mod_ref in enumerate(mods):
        r = _sub_rows(k)
        o_ref[r, :] = h_ref[r, :] + mod_ref[5:6, :] * acc[r]


def dense_ffn(v, h, mod, layer, wg, wu, wd):
    b, l, d = h.shape
    tm = WIDE * ROW_TILE
    n = b * l
    row = lambda j: (j, 0)
    const = lambda j: (0, 0)
    return pl.pallas_call(
        _ffn_kernel,
        grid=(n // tm,),
        in_specs=[pl.BlockSpec((tm, d), row), pl.BlockSpec((tm, d), row)] + _sub_mod_specs(d, l // ROW_TILE) + [
            _layer_weight_spec(wg, layer), _layer_weight_spec(wu, layer), _layer_weight_spec(wd, layer)],
        out_specs=pl.BlockSpec((tm, d), row),
        out_shape=jax.ShapeDtypeStruct((n, d), F32),
        compiler_params=pltpu.CompilerParams(dimension_semantics=("parallel",), vmem_limit_bytes=VMEM_LIMIT),
        name="dense_ffn",
    )(v.reshape(n, d), h.reshape(n, d), *([mod] * WIDE), wg, wu, wd).reshape(b, l, d)


def _moe_kernel(wt_ref, we_ref, lo_ref, hi_ref, first_ref, x_ref, wg_ref, wu_ref, wd_ref, o_ref, xm_ref, acc_ref):
    w = pl.program_id(0)
    j = pl.program_id(1)
    nj = pl.num_programs(1)
    tm = x_ref.shape[0]

    @pl.when(j == 0)
    def _():
        row = wt_ref[w] * tm + lax.broadcasted_iota(jnp.int32, (tm, 1), 0)
        keep = (row >= lo_ref[w]) & (row < hi_ref[w])
        xm_ref[...] = jnp.where(keep, x_ref[...], 0.0).astype(BF16)

    @pl.when((j == 0) & (first_ref[w] > 0))
    def _():
        acc_ref[...] = jnp.zeros_like(acc_ref)

    wg, wu, wd = wg_ref[...].astype(BF16), wu_ref[...].astype(BF16), wd_ref[...].astype(BF16)

    def swiglu_rows(rows):
        x = xm_ref[rows, :]
        mid = (_silu(_dot(x, wg)) * _dot(x, wu)).astype(BF16)
        acc_ref[rows, :] += _dot(mid, wd)

    tile_lo = wt_ref[w] * tm
    whole = (lo_ref[w] <= tile_lo) & (hi_ref[w] >= tile_lo + tm)

    @pl.when(whole)
    def _():
        swiglu_rows(slice(0, tm))

    for part in range(tm // MOE_PART):
        part_lo = tile_lo + part * MOE_PART

        @pl.when(jnp.logical_not(whole) & (hi_ref[w] > jnp.maximum(part_lo, lo_ref[w]))
                 & (lo_ref[w] < part_lo + MOE_PART))
        def _():
            swiglu_rows(slice(part * MOE_PART, (part + 1) * MOE_PART))

    @pl.when(j == nj - 1)
    def _():
        o_ref[...] = acc_ref[...]


def moe_grouped_ffn(xs, items, layer_idx, wg, wu, wd):
    s, d = xs.shape
    tm = MOE_TILE
    fc = MOE_FF_CHUNK
    nw = items[0].shape[0]
    nj = D_FF // fc
    grid_spec = pltpu.PrefetchScalarGridSpec(
        num_scalar_prefetch=5,
        grid=(nw, nj),
        in_specs=[
            pl.BlockSpec((tm, d), lambda w, j, wt, we, lo, hi, fi: (wt[w], 0)),
            pl.BlockSpec((None, None, d, fc), lambda w, j, wt, we, lo, hi, fi: (layer_idx, we[w], 0, j)),
            pl.BlockSpec((None, None, d, fc), lambda w, j, wt, we, lo, hi, fi: (layer_idx, we[w], 0, j)),
            pl.BlockSpec((None, None, fc, d), lambda w, j, wt, we, lo, hi, fi: (layer_idx, we[w], j, 0)),
        ],
        out_specs=pl.BlockSpec((tm, d), lambda w, j, wt, we, lo, hi, fi: (wt[w], 0)),
        scratch_shapes=[pltpu.VMEM((tm, d), BF16), pltpu.VMEM((tm, d), F32)],
    )
    return pl.pallas_call(
        _moe_kernel,
        grid_spec=grid_spec,
        out_shape=jax.ShapeDtypeStruct((s, d), F32),
        compiler_params=pltpu.CompilerParams(
            dimension_semantics=("arbitrary", "arbitrary"), vmem_limit_bytes=VMEM_LIMIT),
        name="moe_ffn",
    )(*items, xs, wg, wu, wd)


def _residual_kernel(h_ref, f0_ref, f1_ref, gate_ref, mod_ref, *rest):
    f = gate_ref[:, 0:1] * f0_ref[...] + gate_ref[:, 1:2] * f1_ref[...]
    hn = h_ref[...] + mod_ref[5:6, :] * f
    if len(rest) == 2:
        gain_ref, o_ref = rest
        o_ref[...] = _rms(hn) * gain_ref[...]
    else:
        rest[0][...] = hn


def gated_residual(h, f0, f1, gates, mod, final_gain=None):
    b, l, d = h.shape
    tm = ROW_TILE
    skip = 0 if final_gain is None else CTX_LEN // tm
    row = lambda i, t: (i, t + skip, 0)
    in_specs = [pl.BlockSpec((None, tm, d), row), pl.BlockSpec((None, tm, d), row), pl.BlockSpec((None, tm, d), row),
                pl.BlockSpec((None, tm, LANES), row),
                pl.BlockSpec((None, None, SUBLANES, d), lambda i, t: (i, jnp.minimum(t + skip, 1), 0, 0))]
    args = [h, f0, f1, gates, mod]
    if final_gain is not None:
        in_specs.append(pl.BlockSpec((1, d), lambda i, t: (0, 0)))
        args.append(final_gain)
    return pl.pallas_call(
        _residual_kernel,
        grid=(b, l // tm - skip),
        in_specs=in_specs,
        out_specs=pl.BlockSpec((None, tm, d), lambda i, t: (i, t, 0)),
        out_shape=jax.ShapeDtypeStruct((b, l - skip * tm, d), F32),
        compiler_params=pltpu.CompilerParams(dimension_semantics=("parallel", "parallel")),
        name="gated_residual",
    )(*args)


def moe_ffn(v, logits, h, mod, layer_idx, wg, wu, wd, final_gain=None):
    b, l, d = h.shape
    t = b * l
    s = TOP_K * t
    tm = MOE_TILE
    nt = s // tm
    nw = nt + N_EXPERTS - 1
    i32 = jnp.int32
    lg = logits.reshape(t, LANES)[:, :N_EXPERTS]
    top_val, top_idx = lax.top_k(lg, TOP_K)
    gates = jax.nn.softmax(top_val, axis=-1)
    slot = jnp.arange(s, dtype=i32)
    skey = jnp.sort(top_idx.reshape(-1).astype(i32) * s + slot)
    order = skey % s
    _, inv = lax.sort_key_val(order, slot)
    bounds = (jnp.arange(N_EXPERTS, dtype=i32) + 1) * s
    cum = jnp.sum((skey[None, :] < bounds[:, None]).astype(i32), axis=1)
    cum_prev = jnp.concatenate([jnp.zeros((1,), i32), cum[:-1]])
    tile_lo = jnp.arange(nt, dtype=i32) * tm
    count_le = lambda edges, x: jnp.sum((edges[None, :] <= x[:, None]).astype(i32), axis=1)
    e_first = count_le(cum, tile_lo)
    e_last = count_le(cum, tile_lo + tm - 1)
    n_items = e_last - e_first + 1
    item_end = jnp.cumsum(n_items)
    item_start = item_end - n_items
    w = jnp.arange(nw, dtype=i32)
    wt = jnp.minimum(count_le(item_end, w), nt - 1)
    valid = w < item_end[-1]
    we = jnp.clip(e_first[wt] + w - item_start[wt], 0, N_EXPERTS - 1).astype(i32)
    lo = jnp.where(valid, cum_prev[we], 0).astype(i32)
    hi = jnp.where(valid, cum[we], 0).astype(i32)
    first = (valid & (w == item_start[wt])).astype(i32)
    rows_of = lambda a, idx: a.at[idx].get(mode="promise_in_bounds")
    xs = rows_of(v.reshape(t, d), order // TOP_K)
    ys = moe_grouped_ffn(xs, (wt, we, lo, hi, first), layer_idx, wg, wu, wd)
    dest = inv.reshape(t, TOP_K)
    f0 = rows_of(ys, dest[:, 0]).reshape(b, l, d)
    f1 = rows_of(ys, dest[:, 1]).reshape(b, l, d)
    gates_p = jnp.pad(gates, ((0, 0), (0, LANES - TOP_K))).reshape(b, l, LANES)
    return gated_residual(h, f0, f1, gates_p, mod, final_gain)


def _final_norm_kernel(h_ref, g_ref, o_ref):
    o_ref[...] = _rms(h_ref[...]) * g_ref[...]


def final_rms_norm(h, gain, n_ctx_tiles):
    b, l, d = h.shape
    tm = ROW_TILE
    n = l - n_ctx_tiles * tm
    return pl.pallas_call(
        _final_norm_kernel,
        grid=(b, n // tm),
        in_specs=[
            pl.BlockSpec((None, tm, d), lambda i, t: (i, t + n_ctx_tiles, 0)),
            pl.BlockSpec((1, d), lambda i, t: (0, 0)),
        ],
        out_specs=pl.BlockSpec((None, tm, d), lambda i, t: (i, t, 0)),
        out_shape=jax.ShapeDtypeStruct((b, n, d), F32),
        compiler_params=pltpu.CompilerParams(dimension_semantics=("parallel", "parallel")),
        name="final_norm",
    )(h, gain)


GDN_W = GDN_HEADS * GDN_DK
GDN_CONV_K = 5
GDN_HALO = SUBLANES


def _split3(x):
    p0 = x.astype(BF16)
    r1 = x - p0.astype(F32)
    p1 = r1.astype(BF16)
    p2 = (r1 - p1.astype(F32)).astype(BF16)
    return p0, p1, p2


def _gdn_prep_kernel(x_ref, prev_ref, next_ref, cw_ref, par_ref, q_ref, k_ref, v_ref, gb_ref):
    t = pl.program_id(1)
    last = pl.num_programs(1) - 1
    tm = x_ref.shape[0]
    w3 = 3 * GDN_W
    has_prev = t > 1
    has_next = (t > 0) & (t < last)
    prev = jnp.where(has_prev, prev_ref[...], 0.0)
    nxt = jnp.where(has_next, next_ref[...], 0.0)
    xe = jnp.concatenate([prev, x_ref[:, :w3], nxt], axis=0)
    y = jnp.zeros((tm, w3), F32)
    for j in range(GDN_CONV_K):
        lo = GDN_HALO - GDN_CONV_K // 2 + j
        y = y + cw_ref[j:j + 1, :] * xe[lo:lo + tm, :]
    y = _silu(y)
    r = lax.broadcasted_iota(jnp.int32, (GDN_W, GDN_W), 0)
    c = lax.broadcasted_iota(jnp.int32, (GDN_W, GDN_W), 1)
    ones_bd = jnp.where(r // GDN_DK == c // GDN_DK, 1.0, 0.0).astype(BF16)

    def l2n(x):
        sq = x * x
        hi = sq.astype(BF16)
        lo = (sq - hi.astype(F32)).astype(BF16)
        return x * lax.rsqrt(_dot(hi, ones_bd) + _dot(lo, ones_bd) + EPS)

    q_ref[...] = l2n(y[:, :GDN_W]) * GDN_DK ** -0.5
    k_ref[...] = l2n(y[:, GDN_W:2 * GDN_W])
    v_ref[...] = y[:, 2 * GDN_W:]
    ab = x_ref[:, w3 + GDN_W:]
    lane = lax.broadcasted_iota(jnp.int32, ab.shape, 1)
    is_g = (lane % 8) < 4
    z = ab + par_ref[1:2, :]
    softplus = jnp.maximum(z, 0.0) + jnp.log1p(jnp.exp(-jnp.abs(z)))
    g = jnp.where(is_g, par_ref[0:1, :] * softplus, 0.0)
    beta = 1.0 / (1.0 + jnp.exp(-ab))
    i = lax.broadcasted_iota(jnp.int32, (tm, tm), 0)
    j = lax.broadcasted_iota(jnp.int32, (tm, tm), 1)
    same_chunk = i // GDN_CHUNK == j // GDN_CHUNK
    tri_f = jnp.where(same_chunk & (j <= i), 1.0, 0.0).astype(BF16)
    tri_b = jnp.where(same_chunk & (j >= i), 1.0, 0.0).astype(BF16)
    pieces = _split3(g)
    gc_f = functools.reduce(jnp.add, [_dot(tri_f, p) for p in pieces])
    gc_b = functools.reduce(jnp.add, [_dot(tri_b, p) for p in pieces])
    gb_ref[...] = jnp.where(is_g, jnp.where(lane < 8, gc_f, gc_b), beta)


def gdn_prep(pb, conv_w, a_log, dt_bias):
    b, l, _ = pb.shape
    tm = ROW_TILE
    w3 = 3 * GDN_W
    halo_blocks = tm // GDN_HALO
    n_halo = l // GDN_HALO
    cw = jnp.pad(conv_w, ((0, SUBLANES - GDN_CONV_K), (0, 0)))
    neg_a = jnp.pad(-jnp.exp(a_log.astype(F32)), ((0, 0), (0, 4))).reshape(-1)
    dtb = jnp.pad(dt_bias.astype(F32), ((0, 0), (0, 4))).reshape(-1)
    par = jnp.pad(jnp.stack([neg_a, dtb]), ((0, SUBLANES - 2), (0, LANES - 16)))
    row = lambda i, t: (i, t, 0)
    out = lambda w: pl.BlockSpec((None, tm, w), row)
    return pl.pallas_call(
        _gdn_prep_kernel,
        grid=(b, l // tm),
        in_specs=[pl.BlockSpec((None, tm, B_W), row),
                  pl.BlockSpec((None, GDN_HALO, w3), lambda i, t: (i, jnp.maximum(t * halo_blocks - 1, 0), 0)),
                  pl.BlockSpec((None, GDN_HALO, w3),
                               lambda i, t: (i, jnp.minimum((t + 1) * halo_blocks, n_halo - 1), 0)),
                  pl.BlockSpec(cw.shape, lambda i, t: (0, 0)),
                  pl.BlockSpec(par.shape, lambda i, t: (0, 0))],
        out_specs=[out(GDN_W), out(GDN_W), out(GDN_W), out(LANES)],
        out_shape=[jax.ShapeDtypeStruct((b, l, GDN_W), F32)] * 3 + [jax.ShapeDtypeStruct((b, l, LANES), F32)],
        compiler_params=pltpu.CompilerParams(
            dimension_semantics=("parallel", "parallel"), vmem_limit_bytes=VMEM_LIMIT),
        name="gdn_prep",
    )(pb, pb, pb, cw, par)


def _tile_heads(x):
    return jnp.concatenate([x] * GDN_HEADS, axis=0)


def _collapse_heads(x):
    c = GDN_CHUNK
    return x[0:c] + x[c:2 * c] + x[2 * c:3 * c] + x[3 * c:4 * c]


def _gdn_chunk_kernel(q_ref, k_ref, v_ref, gb_ref, o0_ref, qe_ref, a_ref, bm_ref, gam_ref):
    n = GDN_W
    cs = GDN_CHUNK
    r = lax.broadcasted_iota(jnp.int32, (n, n), 0)
    c = lax.broadcasted_iota(jnp.int32, (n, n), 1)
    ri, ci = r % cs, c % cs
    head = r // cs == c // cs
    eye = jnp.where(r == c, 1.0, 0.0)
    blk = lambda s: r // s == c // s
    b8, b16, b32 = blk(8), blk(16), blk(32)
    lane = lax.broadcasted_iota(jnp.int32, (n, LANES), 1)
    row_head = lax.broadcasted_iota(jnp.int32, (n, LANES), 0) // cs
    pick = lambda sel, x: jnp.sum(jnp.where(sel, x, 0.0), axis=1, keepdims=True)
    src_lane = lax.broadcasted_iota(jnp.int32, (LANES, n), 0)
    dst_head = lax.broadcasted_iota(jnp.int32, (LANES, n), 1) // cs
    tri, tri_strict, sel_g, sel_b, widen_g, widen_b = {}, {}, {}, {}, {}, {}
    for fwd in (True, False):
        ahead = ri - ci if fwd else ci - ri
        tri[fwd] = head & (ahead >= 0)
        tri_strict[fwd] = head & (ahead > 0)
        lane0 = 0 if fwd else 8
        sel_g[fwd] = lane == lane0 + row_head
        sel_b[fwd] = lane == lane0 + 4 + row_head
        widen_g[fwd] = jnp.where(src_lane == lane0 + dst_head, 1.0, 0.0).astype(BF16)
        widen_b[fwd] = jnp.where(src_lane == lane0 + 4 + dst_head, 1.0, 0.0).astype(BF16)

    n_chunks = q_ref.shape[0] // cs
    rows = [slice(ch * cs, (ch + 1) * cs) for ch in range(n_chunks)]
    items = [(fwd, ch) for fwd in (True, False) for ch in range(n_chunks)]
    dirs = [fwd for fwd, _ in items]
    per_item = lambda xs: [xs[ch] for _, ch in items]
    each = lambda f, *xs: [f(*a) for a in zip(*xs)]
    bf = lambda xs: [x.astype(BF16) for x in xs]
    widen = lambda x, e: functools.reduce(jnp.add, [_dot(p, e) for p in _split3(x)])
    zero_bf = jnp.zeros((n, n), BF16)
    spread = lambda xs: [jnp.where(head, _tile_heads(x.astype(BF16)), zero_bf) for x in xs]
    k_c, q_c, v_c = ([ref[rw, :] for rw in rows] for ref in (k_ref, q_ref, v_ref))
    khb_c, qhb_c = spread(k_c), spread(q_c)
    kk = per_item(each(_dot_nt, khb_c, khb_c))
    qk = per_item(each(_dot_nt, qhb_c, khb_c))
    k_t, q_t, v_t = per_item(k_c), per_item(q_c), per_item(v_c)
    gb = per_item([gb_ref[rw, :] for rw in rows])
    gb4 = [_tile_heads(x) for x in gb]
    gc = [pick(sel_g[f], x) for f, x in zip(dirs, gb4)]
    beta = [pick(sel_b[f], x) for f, x in zip(dirs, gb4)]
    gc_t = [widen(x, widen_g[f]) for f, x in zip(dirs, gb)]
    beta_t = [widen(x, widen_b[f]) for f, x in zip(dirs, gb)]
    ends = [cs - 1 if f else 0 for f in dirs]
    gl_t = [jnp.broadcast_to(x[e:e + 1, :], x.shape) for x, e in zip(gc_t, ends)]
    gc_b = [jnp.broadcast_to(x, (n, n)) for x in gc]
    decay = [jnp.exp(jnp.minimum(x - x.T, 0.0)) for x in gc_b]
    lmat = [jnp.where(tri_strict[f], b_ * kk_ * d_, 0.0) for f, b_, kk_, d_ in zip(dirs, beta, kk, decay)]
    attn = bf([jnp.where(tri[f], qk_ * d_, 0.0) for f, qk_, d_ in zip(dirs, qk, decay)])
    nl = bf([jnp.where(b8, -x, 0.0) for x in lmat])
    n2 = bf(each(_dot, nl, nl))
    n4 = each(_dot, n2, n2)
    p1 = bf(each(lambda a, b_: _dot((eye + a).astype(BF16), (eye + b_).astype(BF16)), nl, n2))
    tinv = each(lambda p, x: _dot(p, (eye + x).astype(BF16)), p1, n4)

    def moving_rows(fwd, x, sz):
        return jnp.concatenate([x[i:i + sz] for i in range(sz if fwd else 0, n, 2 * sz)], axis=0)

    def with_moving_rows(fwd, x, new, sz):
        pieces = []
        for j, i in enumerate(range(0, n, 2 * sz)):
            kept = x[i:i + sz] if fwd else x[i + sz:i + 2 * sz]
            moved = new[j * sz:(j + 1) * sz]
            pieces += [kept, moved] if fwd else [moved, kept]
        return jnp.concatenate(pieces, axis=0)

    for sz, inner, outer in ((8, b8, b16), (16, b16, b32), (32, b32, head)):
        off = bf([jnp.where(outer & ~inner, x, 0.0) for x in lmat])
        tb = bf(tinv)
        t_mv = [moving_rows(f, x, sz) for f, x in zip(dirs, tinv)]
        to = bf(each(_dot, bf(t_mv), off))
        tinv = [with_moving_rows(f, t_, tm_ - _dot(to_, tb_), sz)
                for f, t_, tm_, to_, tb_ in zip(dirs, tinv, t_mv, to, tb)]
    tb = bf(tinv)
    eg_t = [jnp.exp(x) for x in gc_t]
    u = bf(each(_dot, tb, spread(each(lambda b_, v_: b_ * v_, beta_t, v_t))))
    w = bf(each(_dot, tb, spread(each(lambda b_, e_, k_: (b_ * e_) * k_, beta_t, eg_t, k_t))))
    o0 = each(_dot, attn, u)
    aw = each(_dot, attn, w)
    kg = spread(each(lambda k_, gl_, gc_: k_ * jnp.exp(gl_ - gc_), k_t, gl_t, gc_t))
    a_mat = each(_dot_tn, kg, w)
    b_mat = each(_dot_tn, kg, u)
    for (fwd, ch), o0_, aw_, a_, b_, q_, e_, gl_ in zip(items, o0, aw, a_mat, b_mat, q_t, eg_t, gl_t):
        dr, rw = 0 if fwd else 1, rows[ch]
        o0_ref[dr, rw, :] = _collapse_heads(o0_)
        qe_ref[dr, rw, :] = q_ * e_ - _collapse_heads(aw_)
        a_ref[dr, rw, :] = _collapse_heads(a_)
        bm_ref[dr, rw, :] = _collapse_heads(b_)
        gam_ref[dr, rw, :] = jnp.exp(gl_)


def gdn_chunks(q, k, v, gb):
    b, l, _ = q.shape
    tm = ROW_TILE
    row = lambda i, t: (i, t, 0)
    out = pl.BlockSpec((2, None, tm, GDN_W), lambda i, t: (0, i, t, 0))
    return pl.pallas_call(
        _gdn_chunk_kernel,
        grid=(b, l // tm),
        in_specs=[pl.BlockSpec((None, tm, GDN_W), row)] * 3 + [pl.BlockSpec((None, tm, LANES), row)],
        out_specs=[out] * 5,
        out_shape=[jax.ShapeDtypeStruct((2, b, l, GDN_W), F32)] * 5,
        compiler_params=pltpu.CompilerParams(
            dimension_semantics=("parallel", "parallel"), vmem_limit_bytes=VMEM_LIMIT),
        name="gdn_chunk",
    )(q, k, v, gb)


def _gdn_scan_kernel(o0_ref, qe_ref, a_ref, bm_ref, gam_ref, gate_ref, g_ref, o_ref, s_ref, part_ref):
    dr = pl.program_id(0)
    s = pl.program_id(1)
    ns = pl.num_programs(1)
    nb, tm = o0_ref.shape[0], o0_ref.shape[1]
    cs = GDN_CHUNK
    n = GDN_W
    nch = tm // cs
    bs = range(nb)
    tile = jnp.where(s == 0, 0, jnp.where(dr == 0, s, ns - s))
    r = lax.broadcasted_iota(jnp.int32, (n, n), 0)
    c = lax.broadcasted_iota(jnp.int32, (n, n), 1)
    head = r // cs == c // cs

    @pl.when(s == 0)
    def _():
        s_ref[...] = jnp.zeros_like(s_ref)

    def run(order):
        state = [s_ref[i] for i in bs]
        outs = [{} for _ in bs]
        for ch in order:
            rows = slice(ch * cs, (ch + 1) * cs)
            sb = [x.astype(BF16) for x in state]
            for i in bs:
                outs[i][ch] = o0_ref[i, rows, :] + _dot(qe_ref[i, rows, :].astype(BF16), sb[i])
            a_full = [jnp.where(head, _tile_heads(a_ref[i, rows, :]), 0.0).astype(BF16) for i in bs]
            state = [_tile_heads(gam_ref[i, rows, :]) * state[i] - _dot(a_full[i], sb[i])
                     + jnp.where(head, _tile_heads(bm_ref[i, rows, :]), 0.0) for i in bs]
        for i in bs:
            s_ref[i] = state[i]
        return [jnp.concatenate([outs[i][ch] for ch in range(nch)], axis=0) for i in bs]

    rows_out = pl.ds(pl.multiple_of(tile * tm, tm), tm)

    @pl.when(dr == 0)
    def _():
        for i, o in enumerate(run(range(nch))):
            part_ref[i, rows_out, :] = o

    @pl.when(dr == 1)
    def _():
        ones_bd = jnp.where(head, 1.0, 0.0).astype(BF16)
        for i, o_bwd in enumerate(run(range(nch - 1, -1, -1))):
            o = part_ref[i, rows_out, :] + o_bwd
            ms = _head_mean(o * o, ones_bd)
            o_ref[i] = (o * lax.rsqrt(ms + EPS) * g_ref[...] * _silu(gate_ref[i])).astype(o_ref.dtype)


def gdn_scan(o0, qe, a, bm, gam, pb, norm_g):
    _, b, l, _ = o0.shape
    tm = ROW_TILE
    ns = l // tm

    def tile_of(dr, s):
        return jnp.where(s == 0, 0, jnp.where(dr == 0, s, ns - s))

    per_dir = pl.BlockSpec((None, b, tm, GDN_W), lambda dr, s: (dr, 0, tile_of(dr, s), 0))
    gate_col = 3 * GDN_W // GDN_W
    return pl.pallas_call(
        _gdn_scan_kernel,
        grid=(2, ns),
        in_specs=[per_dir] * 5 + [
            pl.BlockSpec((b, tm, GDN_W), lambda dr, s: (0, tile_of(dr, s), gate_col)),
            pl.BlockSpec((1, GDN_W), lambda dr, s: (0, 0))],
        out_specs=pl.BlockSpec((b, tm, GDN_W), lambda dr, s: (0, jnp.where(dr == 0, 0, tile_of(dr, s)), 0)),
        out_shape=jax.ShapeDtypeStruct((b, l, GDN_W), BF16),
        scratch_shapes=[pltpu.VMEM((b, GDN_W, GDN_W), F32), pltpu.VMEM((b, l, GDN_W), F32)],
        compiler_params=pltpu.CompilerParams(
            dimension_semantics=("arbitrary", "arbitrary"), vmem_limit_bytes=VMEM_LIMIT),
        name="gdn_scan",
    )(o0, qe, a, bm, gam, pb, jnp.tile(norm_g, GDN_HEADS)[None, :])


def gdn_mixer(pb, conv_w, a_log, dt_bias, norm_g):
    q, k, v, gb = gdn_prep(pb, conv_w, a_log, dt_bias)
    o0, qe, a, bm, gam = gdn_chunks(q, k, v, gb)
    return gdn_scan(o0, qe, a, bm, gam, pb, norm_g)


def kernel(x, c, ctx, c_ctx, w_mod, b_mod, norm1, norm2, w_in, w_out, swa_sink, gdn_conv, gdn_a_log, gdn_dt_bias, gdn_norm, mla_q_norm, mla_kv_norm, mla_w_q_up, mla_w_kv_up, ret_log_decay, ret_norm, ffn_w_gate, ffn_w_up, ffn_w_down, moe_router, moe_w_gate, moe_w_up, moe_w_down, final_norm):
    b, n, d = x.shape
    depth = w_in.shape[0]
    cos_t, sin_t = rope_tables(n)
    cos_c, sin_c = cos_t[:, A_ROT_W:A_ROT_W + C_ROT_W], sin_t[:, A_ROT_W:A_ROT_W + C_ROT_W]
    h = jnp.concatenate([ctx, x], axis=1)
    cond = jnp.concatenate([jax.nn.silu(c_ctx)[None, :], jax.nn.silu(c)], axis=0)
    mods = jnp.einsum("bd,ldk->lbk", cond, w_mod, precision=lax.Precision.HIGHEST) + b_mod[:, None, :]
    mods = mods.reshape(depth, 1 + b, 6, d)
    mods = jnp.stack([jnp.broadcast_to(mods[:, :1], (depth, b, 6, d)), mods[:, 1:]], axis=2)
    mods = jnp.pad(mods, ((0, 0), (0, 0), (0, 0), (0, SUBLANES - 6), (0, 0)))
    w_in_all = build_in_weight(w_in)
    w_out_all = build_out_weight(w_out)
    ffn_wg, ffn_wu, ffn_wd = ffn_w_gate.astype(BF16), ffn_w_up.astype(BF16), ffn_w_down.astype(BF16)
    for layer in range(depth):
        mod = mods[layer]
        pa, pb, pc, pd = norm_proj(h, mod, norm1[layer][None, :], w_in_all, layer, cos_t, sin_t)
        mix_a = swa_mixer(pa, swa_sink[layer])
        mix_b = gdn_mixer(pb, gdn_conv[layer], gdn_a_log[layer], gdn_dt_bias[layer], gdn_norm[layer])
        mq, mk, mv = mla_prep(pc, mla_q_norm[layer], mla_kv_norm[layer], mla_w_q_up[layer], mla_w_kv_up[layer],
                              cos_c, sin_c)
        mix_c = mla_attention(mq, mk, mv)
        mix_d = retention_mixer(pd, ret_log_decay[layer], ret_norm[layer])
        mixes = (mix_a, mix_b, mix_c, mix_d)
        i = layer // 2
        if layer % 2 == 0:
            h, v = out_proj(mixes, h, mod, norm2[layer][None, :], w_out_all, layer)
            h = dense_ffn(v, h, mod, i, ffn_wg, ffn_wu, ffn_wd)
        else:
            w_r = jnp.pad(moe_router[i], ((0, 0), (0, LANES - N_EXPERTS)))
            h, v, logits = out_proj(mixes, h, mod, norm2[layer][None, :], w_out_all, layer, w_r)
            if layer == depth - 1:
                return moe_ffn(v, logits, h, mod, i, moe_w_gate, moe_w_up, moe_w_down, final_norm[None, :])
            h = moe_ffn(v, logits, h, mod, i, moe_w_gate, moe_w_up, moe_w_down)
    return final_rms_norm(h, final_norm[None, :], CTX_LEN // ROW_TILE)
```

```python
import functools

import numpy as np
import jax
import jax.numpy as jnp
from jax import lax
from jax.experimental import pallas as pl
from jax.experimental.pallas import tpu as pltpu

D_MODEL = 1024
GRID_W = 64
CTX_LEN = 256
HEAD_DIM = 64
ROPE_THETA = 10000.0
EPS = 1e-6
NEG_INF = -1e30

SWA_WINDOW = 128
GDN_HEADS = 4
GDN_DK = 64
GDN_DV = 64
GDN_CHUNK = 64
MLA_HEADS = 4
MLA_NOPE = 64
MLA_ROPE = 32
MLA_V = 64
RET_HEADS = 4
RET_DK = 64
D_FF = 3584
N_EXPERTS = 8
TOP_K = 2

LANES = 128
SUBLANES = 8
VMEM_LIMIT = 56 * 1024 * 1024

ROW_TILE = 256
FF_CHUNK = 512
MOE_TILE = 1024
MOE_PART = 256
MOE_FF_CHUNK = 896

A_W, B_W, C_W, D_W = 768, 1152, 512, 1024
A_ROT_W, C_ROT_W, D_ROT_W = 640, 128, 512
OFF_A = 0
OFF_B = OFF_A + A_W
OFF_C = OFF_B + B_W
OFF_D = OFF_C + C_W
OFF_AR = OFF_D + D_W
OFF_CR = OFF_AR + A_ROT_W
OFF_DR = OFF_CR + C_ROT_W
W_ALL = OFF_DR + D_ROT_W
ROPE_W = A_ROT_W + C_ROT_W + D_ROT_W

LOG2_E = float(np.log2(np.e))
F32 = jnp.float32
BF16 = jnp.bfloat16
NT_DIMS = (((1,), (1,)), ((), ()))
TN_DIMS = (((0,), (0,)), ((), ()))


def _rms(x):
    return x * lax.rsqrt(jnp.mean(x * x, axis=-1, keepdims=True) + EPS)


def _silu(x):
    return x * (1.0 / (1.0 + jnp.exp(-x)))


def _dot(a, b):
    return jnp.dot(a, b, preferred_element_type=F32)


def _dot_nt(a, b):
    return lax.dot_general(a, b, NT_DIMS, preferred_element_type=F32)


def _dot_tn(a, b):
    return lax.dot_general(a, b, TN_DIMS, preferred_element_type=F32)


WIDE = 2


def _sub_tile_specs(block, tiles_per_seq, index_of):
    def spec(k):
        return pl.BlockSpec(block, lambda j: index_of((WIDE * j + k) // tiles_per_seq, (WIDE * j + k) % tiles_per_seq))
    return [spec(k) for k in range(WIDE)]


def _sub_mod_specs(d, tiles_per_seq):
    return _sub_tile_specs((None, None, SUBLANES, d), tiles_per_seq, lambda b, t: (b, jnp.minimum(t, 1), 0, 0))


def _sub_rows(k):
    return slice(k * ROW_TILE, (k + 1) * ROW_TILE)


def _norm_proj_kernel(h_ref, *refs):
    mods, (g_ref, w_ref), tabs = refs[:WIDE], refs[WIDE:WIDE + 2], refs[WIDE + 2:3 * WIDE + 2]
    a_ref, b_ref, c_ref, d_ref = refs[3 * WIDE + 2:]
    y = _rms(h_ref[...]) * g_ref[...]
    u = jnp.concatenate([y[_sub_rows(k)] * (1.0 + m[1:2, :]) + m[0:1, :] for k, m in enumerate(mods)],
                        axis=0).astype(BF16)

    def mm(lo, width):
        return _dot(u, w_ref[:, lo:lo + width])

    a_main = mm(OFF_A, A_W)
    a_rot = mm(OFF_AR, A_ROT_W)
    b_ref[...] = mm(OFF_B, B_W)
    c_main = mm(OFF_C, C_W)
    c_rot = mm(OFF_CR, C_ROT_W)
    d_main = mm(OFF_D, D_W)
    d_rot = mm(OFF_DR, D_ROT_W)
    lo, hi = A_ROT_W, A_ROT_W + C_ROT_W
    a_ref[:, A_ROT_W:] = a_main[:, A_ROT_W:].astype(BF16)
    c_ref[:, :C_W - C_ROT_W] = c_main[:, :C_W - C_ROT_W]
    d_ref[:, D_ROT_W:] = d_main[:, D_ROT_W:]
    for k in range(WIDE):
        cos_ref, sin_ref = tabs[2 * k], tabs[2 * k + 1]
        r = _sub_rows(k)
        a_ref[r, :A_ROT_W] = (a_main[r, :A_ROT_W] * cos_ref[:, :A_ROT_W] + a_rot[r] * sin_ref[:, :A_ROT_W]).astype(BF16)
        c_ref[r, C_W - C_ROT_W:] = c_main[r, C_W - C_ROT_W:] * cos_ref[:, lo:hi] + c_rot[r] * sin_ref[:, lo:hi]
        d_ref[r, :D_ROT_W] = d_main[r, :D_ROT_W] * cos_ref[:, hi:] + d_rot[r] * sin_ref[:, hi:]


def _layer_weight_spec(w, layer):
    return pl.BlockSpec((None,) + w.shape[1:], lambda j: (layer, 0, 0), pipeline_mode=pl.Buffered(1))


def norm_proj(h, mod, gain, w, layer, cos_t, sin_t):
    b, l, d = h.shape
    tm = WIDE * ROW_TILE
    tps = l // ROW_TILE
    row = lambda j: (j, 0)
    const = lambda j: (0, 0)
    tabs = _sub_tile_specs((ROW_TILE, ROPE_W), tps, lambda bi, t: (t, 0))
    tab_specs = [s for pair in zip(tabs, _sub_tile_specs((ROW_TILE, ROPE_W), tps, lambda bi, t: (t, 0))) for s in pair]
    outs = pl.pallas_call(
        _norm_proj_kernel,
        grid=(b * l // tm,),
        in_specs=[pl.BlockSpec((tm, d), row)] + _sub_mod_specs(d, tps) + [
            pl.BlockSpec((1, d), const),
            _layer_weight_spec(w, layer)] + tab_specs,
        out_specs=[pl.BlockSpec((tm, A_W), row), pl.BlockSpec((tm, B_W), row),
                   pl.BlockSpec((tm, C_W), row), pl.BlockSpec((tm, D_W), row)],
        out_shape=[jax.ShapeDtypeStruct((b * l, A_W), BF16), jax.ShapeDtypeStruct((b * l, B_W), F32),
                   jax.ShapeDtypeStruct((b * l, C_W), F32), jax.ShapeDtypeStruct((b * l, D_W), F32)],
        compiler_params=pltpu.CompilerParams(dimension_semantics=("parallel",), vmem_limit_bytes=VMEM_LIMIT),
        name="norm_proj",
    )(h.reshape(b * l, d), *([mod] * WIDE), gain, w, *([cos_t, sin_t] * WIDE))
    return [o.reshape(b, l, -1) for o in outs]


def _rot_cols(w, hd):
    x = w.reshape(w.shape[:-1] + (w.shape[-1] // hd, 4, hd // 4))
    x1, x2, x3, x4 = x[..., 0, :], x[..., 1, :], x[..., 2, :], x[..., 3, :]
    return jnp.stack([-x2, x1, -x4, x3], axis=-2).reshape(w.shape)


def _place_swa_q(q):
    z = jnp.zeros(q.shape[:-1] + (HEAD_DIM,), q.dtype)
    blocks = []
    for h in range(4):
        qh = q[..., HEAD_DIM * h:HEAD_DIM * (h + 1)]
        blocks += [qh, z] if h // 2 == 0 else [z, qh]
    return jnp.concatenate(blocks, axis=-1)


def build_in_weight(w):
    o = [int(v) for v in np.cumsum((256, 128, 128, 768, 256, 16, 256, 128, 32, 256, 256, 256, 256))]
    aq, ak, av = w[..., :o[0]] * (HEAD_DIM ** -0.5 * LOG2_E), w[..., o[0]:o[1]], w[..., o[1]:o[2]]
    b_main, b_ab = w[..., o[2]:o[4]], w[..., o[4]:o[5]]
    c_q, c_kv, c_kr = w[..., o[5]:o[6]], w[..., o[6]:o[7]], w[..., o[7]:o[8]]
    dq, dk, dvg = w[..., o[8]:o[9]], w[..., o[9]:o[10]] * RET_DK ** -0.5, w[..., o[10]:]
    z = lambda n: jnp.zeros(w.shape[:-1] + (n,), w.dtype)
    parts = [
        _place_swa_q(aq), ak, av,
        b_main, b_ab, z(LANES - b_ab.shape[-1]),
        c_q, c_kv, z(64), c_kr, z(32),
        dq, dk, dvg,
        _place_swa_q(_rot_cols(aq, HEAD_DIM)), _rot_cols(ak, HEAD_DIM),
        z(64), _rot_cols(c_kr, MLA_ROPE), z(32),
        _rot_cols(dq, HEAD_DIM), _rot_cols(dk, HEAD_DIM),
    ]
    out = jnp.concatenate(parts, axis=-1)
    assert out.shape[-1] == W_ALL
    return out.astype(BF16)


def rope_tables(n):
    lat = jnp.arange(CTX_LEN + n, dtype=jnp.int32) - CTX_LEN
    grid_row = jnp.where(lat >= 0, lat // GRID_W, 0).astype(F32)[:, None]
    grid_col = jnp.where(lat >= 0, lat % GRID_W, 0).astype(F32)[:, None]
    narrow, col_of = [], {}
    for rot_dim in (HEAD_DIM, MLA_ROPE):
        n_freq = rot_dim // 4
        inv_freq = ROPE_THETA ** (-jnp.arange(n_freq, dtype=F32) / n_freq)
        for axis, pos in enumerate((grid_row, grid_col)):
            col_of[rot_dim, axis] = sum(a.shape[1] for a in narrow)
            narrow.append(pos * inv_freq)
    ang = jnp.concatenate(narrow, axis=1)
    identity_col = ang.shape[1]
    sel = np.zeros((identity_col + 1, ROPE_W), np.float32)

    def plan(lane0, width, group, rot_lo, rot_dim):
        for c in range(width):
            j = c % group - rot_lo
            if 0 <= j < rot_dim:
                quarter, f = divmod(j, rot_dim // 4)
                sel[col_of[rot_dim, quarter // 2] + f, lane0 + c] = 1.0
            else:
                sel[identity_col, lane0 + c] = 1.0

    plan(0, A_ROT_W, HEAD_DIM, 0, HEAD_DIM)
    plan(A_ROT_W, C_ROT_W, C_ROT_W, 64, MLA_ROPE)
    plan(A_ROT_W + C_ROT_W, D_ROT_W, HEAD_DIM, 0, HEAD_DIM)
    spread = lambda t: jnp.dot(t, jnp.asarray(sel), precision=lax.Precision.HIGHEST)
    ones, zeros = jnp.ones_like(grid_row), jnp.zeros_like(grid_row)
    return (spread(jnp.concatenate([jnp.cos(ang), ones], axis=1)),
            spread(jnp.concatenate([jnp.sin(ang), zeros], axis=1)))


def _swa_kernel(sink_ref, q_ref, kp_ref, ko_ref, kn_ref, kc_ref, vp_ref, vo_ref, vn_ref, vc_ref, o_ref):
    t = pl.program_id(1)
    last = pl.num_programs(1) - 1
    tq = q_ref.shape[0]
    half = tq // 2

    def attend(k, v, mask):
        v_lane = lax.broadcasted_iota(jnp.int32, v.shape, 1)
        v_ones = [jnp.where(v_lane >= HEAD_DIM, jnp.ones_like(v), v), jnp.where(v_lane < HEAD_DIM, jnp.ones_like(v), v)]
        heads = range(4)
        s = [_dot_nt(q_ref[:, LANES * h:LANES * (h + 1)], k) for h in heads]
        if mask is not None:
            s = [jnp.where(mask, x, NEG_INF) for x in s]
        sink = [sink_ref[h] * LOG2_E for h in heads]
        m = [jnp.maximum(s[h].max(axis=-1, keepdims=True), sink[h]) for h in heads]
        p = [jnp.exp2(s[h] - m[h]).astype(BF16) for h in heads]
        o = [_dot(p[h], v_ones[h // 2]) for h in heads]
        den_lane = [HEAD_DIM if h // 2 == 0 else 0 for h in heads]
        outs = [o[h] / (o[h][:, den_lane[h]:den_lane[h] + 1] + jnp.exp2(sink[h] - m[h])) for h in heads]
        lane = lax.broadcasted_iota(jnp.int32, (tq, LANES), 1)
        for r in range(2):
            o_ref[:, LANES * r:LANES * (r + 1)] = jnp.where(lane < HEAD_DIM, outs[r], outs[2 + r]).astype(o_ref.dtype)

    @pl.when(t == 0)
    def _():
        attend(kc_ref[...], vc_ref[...], None)

    @pl.when(t > 0)
    def _():
        band = 2 * half + tq
        qi = lax.broadcasted_iota(jnp.int32, (tq, band + CTX_LEN), 0)
        col = lax.broadcasted_iota(jnp.int32, (tq, band + CTX_LEN), 1)
        in_window = jnp.abs(col - half - qi) <= SWA_WINDOW
        exists = ((col >= half) | (t > 1)) & ((col < half + tq) | (t < last))
        mask = (col >= band) | (in_window & exists)
        attend(jnp.concatenate([kp_ref[...], ko_ref[...], kn_ref[...], kc_ref[...]], axis=0),
               jnp.concatenate([vp_ref[...], vo_ref[...], vn_ref[...], vc_ref[...]], axis=0), mask)


def swa_mixer(pa, sink):
    b, l, _ = pa.shape
    tq = ROW_TILE
    nblk = l // SWA_WINDOW
    kcol, vcol = 4, 5
    prev = lambda c: (lambda i, t: (i, jnp.maximum(2 * t - 1, 2), c))
    nxt = lambda c: (lambda i, t: (i, jnp.minimum(2 * t + 2, nblk - 1), c))
    own = lambda c: (lambda i, t: (i, t, c))
    ctx = lambda c: (lambda i, t: (i, 0, c))
    kv_specs = lambda c: [pl.BlockSpec((None, SWA_WINDOW, LANES), prev(c)), pl.BlockSpec((None, tq, LANES), own(c)),
                          pl.BlockSpec((None, SWA_WINDOW, LANES), nxt(c)), pl.BlockSpec((None, tq, LANES), ctx(c))]
    return pl.pallas_call(
        _swa_kernel,
        grid=(b, l // tq),
        in_specs=[pl.BlockSpec(memory_space=pltpu.SMEM),
                  pl.BlockSpec((None, tq, 4 * LANES), lambda i, t: (i, t, 0))] + kv_specs(kcol) + kv_specs(vcol),
        out_specs=pl.BlockSpec((None, tq, 2 * LANES), lambda i, t: (i, t, 0)),
        out_shape=jax.ShapeDtypeStruct((b, l, 2 * LANES), BF16),
        compiler_params=pltpu.CompilerParams(
            dimension_semantics=("parallel", "parallel"), vmem_limit_bytes=VMEM_LIMIT),
        name="swa",
    )(sink, pa, pa, pa, pa, pa, pa, pa, pa, pa)


def _mla_prep_kernel(c_ref, qn_ref, kvn_ref, wq_ref, wqr_ref, wk_ref, wv_ref, cos_ref, sin_ref,
                     q_ref, k_ref, v_ref):
    cq = c_ref[:, 0:256]
    ckv = c_ref[:, 256:384]
    kr = c_ref[:, 384:512]
    nq = (_rms(cq) * qn_ref[...]).astype(BF16)
    nkv = (_rms(ckv) * kvn_ref[...]).astype(BF16)
    cos = jnp.concatenate([cos_ref[...]] * MLA_HEADS, axis=1)
    sin = jnp.concatenate([sin_ref[...]] * MLA_HEADS, axis=1)
    q_ref[...] = (_dot(nq, wq_ref[...]) * cos + _dot(nq, wqr_ref[...]) * sin).astype(BF16)
    k_ref[...] = (_dot(nkv, wk_ref[...]) + jnp.concatenate([kr] * MLA_HEADS, axis=1)).astype(BF16)
    v_ref[...] = _dot(nkv, wv_ref[...]).astype(BF16)


def mla_prep(pc, q_norm, kv_norm, w_q_up, w_kv_up, cos_c, sin_c):
    b, l, _ = pc.shape
    tm = ROW_TILE
    scale = (MLA_NOPE + MLA_ROPE) ** -0.5 * LOG2_E
    wq = (w_q_up * scale).reshape(-1, MLA_HEADS, MLA_NOPE + MLA_ROPE)
    zq = jnp.zeros(wq.shape[:2] + (LANES - MLA_NOPE - MLA_ROPE,), F32)
    wq_main = jnp.concatenate([wq, zq], axis=-1).reshape(-1, MLA_HEADS * LANES)
    wq_rot = jnp.concatenate([jnp.zeros_like(wq[..., :MLA_NOPE]), _rot_cols(wq[..., MLA_NOPE:], MLA_ROPE), zq],
                             axis=-1).reshape(-1, MLA_HEADS * LANES)
    wkv = w_kv_up.reshape(-1, MLA_HEADS, MLA_NOPE + MLA_V)
    wk = jnp.concatenate([wkv[..., :MLA_NOPE], jnp.zeros_like(wkv[..., :LANES - MLA_NOPE])],
                         axis=-1).reshape(-1, MLA_HEADS * LANES)
    wv = wkv[..., MLA_NOPE:].reshape(-1, MLA_HEADS * MLA_V)
    row = lambda i, t: (i, t, 0)
    const = lambda i, t: (0, 0)
    full = lambda a: pl.BlockSpec(a.shape, const)
    args = [q_norm[None, :], kv_norm[None, :], wq_main.astype(BF16), wq_rot.astype(BF16), wk.astype(BF16),
            wv.astype(BF16)]
    return pl.pallas_call(
        _mla_prep_kernel,
        grid=(b, l // tm),
        in_specs=[pl.BlockSpec((None, tm, C_W), row)] + [full(a) for a in args]
        + [pl.BlockSpec((tm, LANES), lambda i, t: (t, 0)), pl.BlockSpec((tm, LANES), lambda i, t: (t, 0))],
        out_specs=[pl.BlockSpec((None, tm, 4 * LANES), row), pl.BlockSpec((None, tm, 4 * LANES), row),
                   pl.BlockSpec((None, tm, 2 * LANES), row)],
        out_shape=[jax.ShapeDtypeStruct((b, l, 4 * LANES), BF16), jax.ShapeDtypeStruct((b, l, 4 * LANES), BF16),
                   jax.ShapeDtypeStruct((b, l, 2 * LANES), BF16)],
        compiler_params=pltpu.CompilerParams(dimension_semantics=("parallel", "parallel")),
        name="mla_prep",
    )(pc, *args, cos_c, sin_c)


def _mla_attn_kernel(q_ref, k_ref, v_ref, o_ref):
    t = pl.program_id(2)
    tq = q_ref.shape[0]

    def attend(nk):
        v = v_ref[0:nk, :]
        v_lane = lax.broadcasted_iota(jnp.int32, v.shape, 1)
        v_ones = [jnp.where(v_lane >= MLA_V, jnp.ones_like(v), v), jnp.where(v_lane < MLA_V, jnp.ones_like(v), v)]
        pair = range(2)
        s = [_dot_nt(q_ref[:, LANES * j:LANES * (j + 1)], k_ref[0:nk, LANES * j:LANES * (j + 1)]) for j in pair]
        p = [jnp.exp2(s[j] - s[j].max(axis=-1, keepdims=True)).astype(BF16) for j in pair]
        o = [_dot(p[j], v_ones[j]) for j in pair]
        outs = [o[j] / o[j][:, (MLA_V, 0)[j]:(MLA_V, 0)[j] + 1] for j in pair]
        lane = lax.broadcasted_iota(jnp.int32, (tq, LANES), 1)
        o_ref[...] = jnp.where(lane < MLA_V, outs[0], outs[1]).astype(o_ref.dtype)

    @pl.when(t == 0)
    def _():
        attend(CTX_LEN)

    @pl.when(t > 0)
    def _():
        attend(k_ref.shape[0])


def mla_attention(q, k, v):
    b, l, _ = q.shape
    tq = ROW_TILE
    return pl.pallas_call(
        _mla_attn_kernel,
        grid=(b, 2, l // tq),
        in_specs=[pl.BlockSpec((None, tq, 2 * LANES), lambda i, p, t: (i, t, p)),
                  pl.BlockSpec((None, l, 2 * LANES), lambda i, p, t: (i, 0, p)),
                  pl.BlockSpec((None, l, LANES), lambda i, p, t: (i, 0, p))],
        out_specs=pl.BlockSpec((None, tq, LANES), lambda i, p, t: (i, t, p)),
        out_shape=jax.ShapeDtypeStruct((b, l, 2 * LANES), BF16),
        compiler_params=pltpu.CompilerParams(
            dimension_semantics=("parallel", "parallel", "parallel"), vmem_limit_bytes=VMEM_LIMIT),
        name="mla_attn",
    )(q, k, v)


def _head_mean(x, ones_bd):
    hi = x.astype(BF16)
    lo = (x - hi.astype(F32)).astype(BF16)
    return (_dot(hi, ones_bd) + _dot(lo, ones_bd)) * (1.0 / HEAD_DIM)


def _ret_kernel(x_ref, lg_ref, g_ref, o_ref, s_ref, dec_ref, part_ref):
    dr = pl.program_id(0)
    s = pl.program_id(1)
    ns = pl.num_programs(1)
    nb, c = x_ref.shape[0], x_ref.shape[1]
    w = RET_HEADS * HEAD_DIM
    chunk = jnp.where(s == 0, 0, jnp.where(dr == 0, s, ns - s))
    lg = lg_ref[...]
    fwd = dr == 0
    row_h = lax.broadcasted_iota(jnp.int32, (w, w), 0) // HEAD_DIM
    col_h = lax.broadcasted_iota(jnp.int32, (w, w), 1) // HEAD_DIM
    same_head = row_h == col_h

    @pl.when(s == 0)
    def _():
        s_ref[...] = jnp.zeros_like(s_ref)
        i = lax.broadcasted_iota(jnp.int32, (c, c), 0)
        j = lax.broadcasted_iota(jnp.int32, (c, c), 1)
        rel = jnp.where(fwd, i - j, j - i)
        relf = jnp.maximum(rel, 0).astype(F32)
        for h in range(RET_HEADS):
            lg_h = lg_ref[0:1, HEAD_DIM * h:HEAD_DIM * h + 1]
            dec_ref[h] = jnp.where(rel >= 0, jnp.exp(lg_h * relf), 0.0)

    pos = lax.broadcasted_iota(jnp.int32, (c, 1), 0).astype(F32)
    q_dec = jnp.exp(lg * jnp.where(fwd, pos + 1.0, c - pos))
    k_dec = jnp.exp(lg * jnp.where(fwd, c - 1.0 - pos, pos))
    lane_h = lax.broadcasted_iota(jnp.int32, (c, w), 1) // HEAD_DIM
    bs = range(nb)
    q = [x_ref[i, :, 0:w] for i in bs]
    kf = [x_ref[i, :, w:2 * w] for i in bs]
    v = [x_ref[i, :, 2 * w:3 * w].astype(BF16) for i in bs]
    kb = [x.astype(BF16) for x in kf]
    acc = [_dot((q[i] * q_dec).astype(BF16), s_ref[i].astype(BF16)) for i in bs]
    for h in range(RET_HEADS):
        qh = [jnp.where(lane_h == h, q[i], 0.0).astype(BF16) for i in bs]
        a = [(_dot_nt(qh[i], kb[i]) * dec_ref[h]).astype(BF16) for i in bs]
        acc = [acc[i] + jnp.where(lane_h == h, _dot(a[i], v[i]), 0.0) for i in bs]
    kv = [_dot_tn((kf[i] * k_dec).astype(BF16), v[i]) for i in bs]
    chunk_dec = jnp.exp(lg * float(c))
    for i in bs:
        s_ref[i] = s_ref[i] * chunk_dec + jnp.where(same_head, kv[i], 0.0)

    rows = pl.ds(pl.multiple_of(chunk * c, c), c)

    @pl.when(dr == 0)
    def _():
        for i in bs:
            part_ref[i, rows, :] = acc[i]

    @pl.when(dr == 1)
    def _():
        ones_bd = jnp.where(same_head, 1.0, 0.0).astype(BF16)
        for i in bs:
            o = part_ref[i, rows, :] + acc[i]
            mu = _head_mean(o, ones_bd)
            var = _head_mean(jnp.square(o - mu), ones_bd)
            y = (o - mu) * lax.rsqrt(var + EPS) * g_ref[...]
            o_ref[i] = (y * _silu(x_ref[i, :, 3 * w:4 * w])).astype(o_ref.dtype)


def retention_mixer(pd, log_decay, norm_g):
    b, l, _ = pd.shape
    c = ROW_TILE
    ns = l // c
    w = RET_HEADS * HEAD_DIM
    lg = jnp.repeat(-jnp.exp(log_decay.astype(F32)), HEAD_DIM, axis=-1)[:, None, :]

    def chunk_of(dr, s):
        return jnp.where(s == 0, 0, jnp.where(dr == 0, s, ns - s))

    return pl.pallas_call(
        _ret_kernel,
        grid=(2, ns),
        in_specs=[pl.BlockSpec((b, c, D_W), lambda dr, s: (0, chunk_of(dr, s), 0)),
                  pl.BlockSpec((None, 1, w), lambda dr, s: (dr, 0, 0)),
                  pl.BlockSpec((1, w), lambda dr, s: (0, 0))],
        out_specs=pl.BlockSpec((b, c, w), lambda dr, s: (0, jnp.where(dr == 0, 0, chunk_of(dr, s)), 0)),
        out_shape=jax.ShapeDtypeStruct((b, l, w), BF16),
        scratch_shapes=[pltpu.VMEM((b, w, w), F32), pltpu.VMEM((RET_HEADS, c, c), F32), pltpu.VMEM((b, l, w), F32)],
        compiler_params=pltpu.CompilerParams(
            dimension_semantics=("arbitrary", "arbitrary"), vmem_limit_bytes=VMEM_LIMIT),
        name="retention",
    )(pd, lg, norm_g[None, :])


def _out_proj_kernel(ma_ref, mb_ref, mc_ref, md_ref, h_ref, *refs, with_router):
    mods, (g_ref, w_ref), rest = refs[:WIDE], refs[WIDE:WIDE + 2], refs[WIDE + 2:]
    if with_router:
        wr_ref, hn_ref, v_ref, lg_ref = rest
    else:
        hn_ref, v_ref = rest
    gw = 2 * LANES
    mix = functools.reduce(jnp.add, [
        _dot(m_ref[...].astype(BF16), w_ref[gw * i:gw * (i + 1), :])
        for i, m_ref in enumerate((ma_ref, mb_ref, mc_ref, md_ref))])
    for k, mod_ref in enumerate(mods):
        r = _sub_rows(k)
        hn = h_ref[r, :] + mod_ref[2:3, :] * mix[r]
        hn_ref[r, :] = hn
        v = _rms(hn) * g_ref[...] * (1.0 + mod_ref[4:5, :]) + mod_ref[3:4, :]
        v_ref[r, :] = v.astype(v_ref.dtype)
        if with_router:
            lg_ref[r, :] = jnp.dot(v, wr_ref[...], preferred_element_type=F32, precision=lax.Precision.HIGHEST)


def out_proj(mixes, h, mod, gain, w, layer, w_router=None):
    b, l, d = h.shape
    tm = WIDE * ROW_TILE
    n = b * l
    with_router = w_router is not None
    row = lambda j: (j, 0)
    const = lambda j: (0, 0)
    in_specs = [pl.BlockSpec((tm, 2 * LANES), row) for _ in mixes] + [pl.BlockSpec((tm, d), row)] + _sub_mod_specs(
        d, l // ROW_TILE) + [pl.BlockSpec((1, d), const), _layer_weight_spec(w, layer)]
    out_specs = [pl.BlockSpec((tm, d), row), pl.BlockSpec((tm, d), row)]
    out_shape = [jax.ShapeDtypeStruct((n, d), F32), jax.ShapeDtypeStruct((n, d), F32 if with_router else BF16)]
    args = [m.reshape(n, 2 * LANES) for m in mixes] + [h.reshape(n, d)] + [mod] * WIDE + [gain, w]
    if with_router:
        in_specs.append(pl.BlockSpec(w_router.shape, const))
        out_specs.append(pl.BlockSpec((tm, LANES), row))
        out_shape.append(jax.ShapeDtypeStruct((n, LANES), F32))
        args.append(w_router)
    outs = pl.pallas_call(
        functools.partial(_out_proj_kernel, with_router=with_router),
        grid=(n // tm,),
        in_specs=in_specs,
        out_specs=out_specs,
        out_shape=out_shape,
        compiler_params=pltpu.CompilerParams(dimension_semantics=("parallel",), vmem_limit_bytes=VMEM_LIMIT),
        name="out_proj",
    )(*args)
    return [o.reshape(b, l, -1) for o in outs]


def build_out_weight(w):
    hd = HEAD_DIM
    rows = lambda lo, hi: w[..., lo:hi, :]
    return jnp.concatenate([rows(0, hd), rows(2 * hd, 3 * hd), rows(hd, 2 * hd), rows(3 * hd, None)],
                           axis=-2).astype(BF16)


def _ffn_kernel(v_ref, h_ref, *refs):
    mods, (wg_ref, wu_ref, wd_ref, o_ref) = refs[:WIDE], refs[WIDE:]
    v = v_ref[...]
    acc = jnp.zeros(o_ref.shape, F32)
    for j in range(D_FF // FF_CHUNK):
        cols = slice(j * FF_CHUNK, (j + 1) * FF_CHUNK)
        a = _dot(v, wg_ref[:, cols])
        u = _dot(v, wu_ref[:, cols])
        mid = (_silu(a) * u).astype(BF16)
        acc = acc + _dot(mid, wd_ref[cols, :])
    for k, mod_ref in enumerate(mods):
        r = _sub_rows(k)
        o_ref[r, :] = h_ref[r, :] + mod_ref[5:6, :] * acc[r]


def dense_ffn(v, h, mod, layer, wg, wu, wd):
    b, l, d = h.shape
    tm = WIDE * ROW_TILE
    n = b * l
    row = lambda j: (j, 0)
    const = lambda j: (0, 0)
    return pl.pallas_call(
        _ffn_kernel,
        grid=(n // tm,),
        in_specs=[pl.BlockSpec((tm, d), row), pl.BlockSpec((tm, d), row)] + _sub_mod_specs(d, l // ROW_TILE) + [
            _layer_weight_spec(wg, layer), _layer_weight_spec(wu, layer), _layer_weight_spec(wd, layer)],
        out_specs=pl.BlockSpec((tm, d), row),
        out_shape=jax.ShapeDtypeStruct((n, d), F32),
        compiler_params=pltpu.CompilerParams(dimension_semantics=("parallel",), vmem_limit_bytes=VMEM_LIMIT),
        name="dense_ffn",
    )(v.reshape(n, d), h.reshape(n, d), *([mod] * WIDE), wg, wu, wd).reshape(b, l, d)


def _moe_kernel(wt_ref, we_ref, lo_ref, hi_ref, first_ref, x_ref, wg_ref, wu_ref, wd_ref, o_ref, xm_ref, acc_ref):
    w = pl.program_id(0)
    j = pl.program_id(1)
    nj = pl.num_programs(1)
    tm = x_ref.shape[0]

    @pl.when(j == 0)
    def _():
        row = wt_ref[w] * tm + lax.broadcasted_iota(jnp.int32, (tm, 1), 0)
        keep = (row >= lo_ref[w]) & (row < hi_ref[w])
        xm_ref[...] = jnp.where(keep, x_ref[...], 0.0).astype(BF16)

    @pl.when((j == 0) & (first_ref[w] > 0))
    def _():
        acc_ref[...] = jnp.zeros_like(acc_ref)

    def swiglu_rows(rows):
        x = xm_ref[rows, :]
        mid = (_silu(_dot(x, wg_ref[...])) * _dot(x, wu_ref[...])).astype(BF16)
        acc_ref[rows, :] += _dot(mid, wd_ref[...])

    tile_lo = wt_ref[w] * tm
    whole = (lo_ref[w] <= tile_lo) & (hi_ref[w] >= tile_lo + tm)

    @pl.when(whole)
    def _():
        swiglu_rows(slice(0, tm))

    for part in range(tm // MOE_PART):
        part_lo = tile_lo + part * MOE_PART

        @pl.when(jnp.logical_not(whole) & (hi_ref[w] > jnp.maximum(part_lo, lo_ref[w]))
                 & (lo_ref[w] < part_lo + MOE_PART))
        def _():
            swiglu_rows(slice(part * MOE_PART, (part + 1) * MOE_PART))

    @pl.when(j == nj - 1)
    def _():
        o_ref[...] = acc_ref[...]


def moe_grouped_ffn(xs, items, layer_idx, wg, wu, wd):
    s, d = xs.shape
    tm = MOE_TILE
    fc = MOE_FF_CHUNK
    nw = items[0].shape[0]
    nj = D_FF // fc
    grid_spec = pltpu.PrefetchScalarGridSpec(
        num_scalar_prefetch=5,
        grid=(nw, nj),
        in_specs=[
            pl.BlockSpec((tm, d), lambda w, j, wt, we, lo, hi, fi: (wt[w], 0)),
            pl.BlockSpec((None, None, d, fc), lambda w, j, wt, we, lo, hi, fi: (layer_idx, we[w], 0, j)),
            pl.BlockSpec((None, None, d, fc), lambda w, j, wt, we, lo, hi, fi: (layer_idx, we[w], 0, j)),
            pl.BlockSpec((None, None, fc, d), lambda w, j, wt, we, lo, hi, fi: (layer_idx, we[w], j, 0)),
        ],
        out_specs=pl.BlockSpec((tm, d), lambda w, j, wt, we, lo, hi, fi: (wt[w], 0)),
        scratch_shapes=[pltpu.VMEM((tm, d), BF16), pltpu.VMEM((tm, d), F32)],
    )
    return pl.pallas_call(
        _moe_kernel,
        grid_spec=grid_spec,
        out_shape=jax.ShapeDtypeStruct((s, d), F32),
        compiler_params=pltpu.CompilerParams(
            dimension_semantics=("arbitrary", "arbitrary"), vmem_limit_bytes=VMEM_LIMIT),
        name="moe_ffn",
    )(*items, xs, wg, wu, wd)


def _residual_kernel(h_ref, f0_ref, f1_ref, gate_ref, mod_ref, *rest):
    f = gate_ref[:, 0:1] * f0_ref[...] + gate_ref[:, 1:2] * f1_ref[...]
    hn = h_ref[...] + mod_ref[5:6, :] * f
    if len(rest) == 2:
        gain_ref, o_ref = rest
        o_ref[...] = _rms(hn) * gain_ref[...]
    else:
        rest[0][...] = hn


def gated_residual(h, f0, f1, gates, mod, final_gain=None):
    b, l, d = h.shape
    tm = ROW_TILE
    skip = 0 if final_gain is None else CTX_LEN // tm
    row = lambda i, t: (i, t + skip, 0)
    in_specs = [pl.BlockSpec((None, tm, d), row), pl.BlockSpec((None, tm, d), row), pl.BlockSpec((None, tm, d), row),
                pl.BlockSpec((None, tm, LANES), row),
                pl.BlockSpec((None, None, SUBLANES, d), lambda i, t: (i, jnp.minimum(t + skip, 1), 0, 0))]
    args = [h, f0, f1, gates, mod]
    if final_gain is not None:
        in_specs.append(pl.BlockSpec((1, d), lambda i, t: (0, 0)))
        args.append(final_gain)
    return pl.pallas_call(
        _residual_kernel,
        grid=(b, l // tm - skip),
        in_specs=in_specs,
        out_specs=pl.BlockSpec((None, tm, d), lambda i, t: (i, t, 0)),
        out_shape=jax.ShapeDtypeStruct((b, l - skip * tm, d), F32),
        compiler_params=pltpu.CompilerParams(dimension_semantics=("parallel", "parallel")),
        name="gated_residual",
    )(*args)


def moe_ffn(v, logits, h, mod, layer_idx, wg, wu, wd, final_gain=None):
    b, l, d = h.shape
    t = b * l
    s = TOP_K * t
    tm = MOE_TILE
    nt = s // tm
    nw = nt + N_EXPERTS - 1
    i32 = jnp.int32
    lg = logits.reshape(t, LANES)[:, :N_EXPERTS]
    top_val, top_idx = lax.top_k(lg, TOP_K)
    gates = jax.nn.softmax(top_val, axis=-1)
    slot = jnp.arange(s, dtype=i32)
    skey = jnp.sort(top_idx.reshape(-1).astype(i32) * s + slot)
    order = skey % s
    _, inv = lax.sort_key_val(order, slot)
    bounds = (jnp.arange(N_EXPERTS, dtype=i32) + 1) * s
    cum = jnp.sum((skey[None, :] < bounds[:, None]).astype(i32), axis=1)
    cum_prev = jnp.concatenate([jnp.zeros((1,), i32), cum[:-1]])
    tile_lo = jnp.arange(nt, dtype=i32) * tm
    count_le = lambda edges, x: jnp.sum((edges[None, :] <= x[:, None]).astype(i32), axis=1)
    e_first = count_le(cum, tile_lo)
    e_last = count_le(cum, tile_lo + tm - 1)
    n_items = e_last - e_first + 1
    item_end = jnp.cumsum(n_items)
    item_start = item_end - n_items
    w = jnp.arange(nw, dtype=i32)
    wt = jnp.minimum(count_le(item_end, w), nt - 1)
    valid = w < item_end[-1]
    we = jnp.clip(e_first[wt] + w - item_start[wt], 0, N_EXPERTS - 1).astype(i32)
    lo = jnp.where(valid, cum_prev[we], 0).astype(i32)
    hi = jnp.where(valid, cum[we], 0).astype(i32)
    first = (valid & (w == item_start[wt])).astype(i32)
    rows_of = lambda a, idx: a.at[idx].get(mode="promise_in_bounds")
    xs = rows_of(v.reshape(t, d), order // TOP_K)
    ys = moe_grouped_ffn(xs, (wt, we, lo, hi, first), layer_idx, wg, wu, wd)
    dest = inv.reshape(t, TOP_K)
    f0 = rows_of(ys, dest[:, 0]).reshape(b, l, d)
    f1 = rows_of(ys, dest[:, 1]).reshape(b, l, d)
    gates_p = jnp.pad(gates, ((0, 0), (0, LANES - TOP_K))).reshape(b, l, LANES)
    return gated_residual(h, f0, f1, gates_p, mod, final_gain)


def _final_norm_kernel(h_ref, g_ref, o_ref):
    o_ref[...] = _rms(h_ref[...]) * g_ref[...]


def final_rms_norm(h, gain, n_ctx_tiles):
    b, l, d = h.shape
    tm = ROW_TILE
    n = l - n_ctx_tiles * tm
    return pl.pallas_call(
        _final_norm_kernel,
        grid=(b, n // tm),
        in_specs=[
            pl.BlockSpec((None, tm, d), lambda i, t: (i, t + n_ctx_tiles, 0)),
            pl.BlockSpec((1, d), lambda i, t: (0, 0)),
        ],
        out_specs=pl.BlockSpec((None, tm, d), lambda i, t: (i, t, 0)),
        out_shape=jax.ShapeDtypeStruct((b, n, d), F32),
        compiler_params=pltpu.CompilerParams(dimension_semantics=("parallel", "parallel")),
        name="final_norm",
    )(h, gain)


GDN_W = GDN_HEADS * GDN_DK
GDN_CONV_K = 5
GDN_HALO = SUBLANES


def _split3(x):
    p0 = x.astype(BF16)
    r1 = x - p0.astype(F32)
    p1 = r1.astype(BF16)
    p2 = (r1 - p1.astype(F32)).astype(BF16)
    return p0, p1, p2


def _gdn_prep_kernel(x_ref, prev_ref, next_ref, cw_ref, par_ref, q_ref, k_ref, v_ref, gb_ref):
    t = pl.program_id(1)
    last = pl.num_programs(1) - 1
    tm = x_ref.shape[0]
    w3 = 3 * GDN_W
    has_prev = t > 1
    has_next = (t > 0) & (t < last)
    prev = jnp.where(has_prev, prev_ref[...], 0.0)
    nxt = jnp.where(has_next, next_ref[...], 0.0)
    xe = jnp.concatenate([prev, x_ref[:, :w3], nxt], axis=0)
    y = jnp.zeros((tm, w3), F32)
    for j in range(GDN_CONV_K):
        lo = GDN_HALO - GDN_CONV_K // 2 + j
        y = y + cw_ref[j:j + 1, :] * xe[lo:lo + tm, :]
    y = _silu(y)
    r = lax.broadcasted_iota(jnp.int32, (GDN_W, GDN_W), 0)
    c = lax.broadcasted_iota(jnp.int32, (GDN_W, GDN_W), 1)
    ones_bd = jnp.where(r // GDN_DK == c // GDN_DK, 1.0, 0.0).astype(BF16)

    def l2n(x):
        sq = x * x
        hi = sq.astype(BF16)
        lo = (sq - hi.astype(F32)).astype(BF16)
        return x * lax.rsqrt(_dot(hi, ones_bd) + _dot(lo, ones_bd) + EPS)

    q_ref[...] = l2n(y[:, :GDN_W]) * GDN_DK ** -0.5
    k_ref[...] = l2n(y[:, GDN_W:2 * GDN_W])
    v_ref[...] = y[:, 2 * GDN_W:]
    ab = x_ref[:, w3 + GDN_W:]
    lane = lax.broadcasted_iota(jnp.int32, ab.shape, 1)
    is_g = (lane % 8) < 4
    z = ab + par_ref[1:2, :]
    softplus = jnp.maximum(z, 0.0) + jnp.log1p(jnp.exp(-jnp.abs(z)))
    g = jnp.where(is_g, par_ref[0:1, :] * softplus, 0.0)
    beta = 1.0 / (1.0 + jnp.exp(-ab))
    i = lax.broadcasted_iota(jnp.int32, (tm, tm), 0)
    j = lax.broadcasted_iota(jnp.int32, (tm, tm), 1)
    same_chunk = i // GDN_CHUNK == j // GDN_CHUNK
    tri_f = jnp.where(same_chunk & (j <= i), 1.0, 0.0).astype(BF16)
    tri_b = jnp.where(same_chunk & (j >= i), 1.0, 0.0).astype(BF16)
    pieces = _split3(g)
    gc_f = functools.reduce(jnp.add, [_dot(tri_f, p) for p in pieces])
    gc_b = functools.reduce(jnp.add, [_dot(tri_b, p) for p in pieces])
    gb_ref[...] = jnp.where(is_g, jnp.where(lane < 8, gc_f, gc_b), beta)


def gdn_prep(pb, conv_w, a_log, dt_bias):
    b, l, _ = pb.shape
    tm = ROW_TILE
    w3 = 3 * GDN_W
    halo_blocks = tm // GDN_HALO
    n_halo = l // GDN_HALO
    cw = jnp.pad(conv_w, ((0, SUBLANES - GDN_CONV_K), (0, 0)))
    neg_a = jnp.pad(-jnp.exp(a_log.astype(F32)), ((0, 0), (0, 4))).reshape(-1)
    dtb = jnp.pad(dt_bias.astype(F32), ((0, 0), (0, 4))).reshape(-1)
    par = jnp.pad(jnp.stack([neg_a, dtb]), ((0, SUBLANES - 2), (0, LANES - 16)))
    row = lambda i, t: (i, t, 0)
    out = lambda w: pl.BlockSpec((None, tm, w), row)
    return pl.pallas_call(
        _gdn_prep_kernel,
        grid=(b, l // tm),
        in_specs=[pl.BlockSpec((None, tm, B_W), row),
                  pl.BlockSpec((None, GDN_HALO, w3), lambda i, t: (i, jnp.maximum(t * halo_blocks - 1, 0), 0)),
                  pl.BlockSpec((None, GDN_HALO, w3),
                               lambda i, t: (i, jnp.minimum((t + 1) * halo_blocks, n_halo - 1), 0)),
                  pl.BlockSpec(cw.shape, lambda i, t: (0, 0)),
                  pl.BlockSpec(par.shape, lambda i, t: (0, 0))],
        out_specs=[out(GDN_W), out(GDN_W), out(GDN_W), out(LANES)],
        out_shape=[jax.ShapeDtypeStruct((b, l, GDN_W), F32)] * 3 + [jax.ShapeDtypeStruct((b, l, LANES), F32)],
        compiler_params=pltpu.CompilerParams(
            dimension_semantics=("parallel", "parallel"), vmem_limit_bytes=VMEM_LIMIT),
        name="gdn_prep",
    )(pb, pb, pb, cw, par)


def _tile_heads(x):
    return jnp.concatenate([x] * GDN_HEADS, axis=0)


def _collapse_heads(x):
    c = GDN_CHUNK
    return x[0:c] + x[c:2 * c] + x[2 * c:3 * c] + x[3 * c:4 * c]


def _gdn_chunk_kernel(q_ref, k_ref, v_ref, gb_ref, o0_ref, qe_ref, a_ref, bm_ref, gam_ref):
    n = GDN_W
    cs = GDN_CHUNK
    r = lax.broadcasted_iota(jnp.int32, (n, n), 0)
    c = lax.broadcasted_iota(jnp.int32, (n, n), 1)
    ri, ci = r % cs, c % cs
    head = r // cs == c // cs
    eye = jnp.where(r == c, 1.0, 0.0)
    blk = lambda s: r // s == c // s
    b8, b16, b32 = blk(8), blk(16), blk(32)
    lane = lax.broadcasted_iota(jnp.int32, (n, LANES), 1)
    row_head = lax.broadcasted_iota(jnp.int32, (n, LANES), 0) // cs
    pick = lambda sel, x: jnp.sum(jnp.where(sel, x, 0.0), axis=1, keepdims=True)
    src_lane = lax.broadcasted_iota(jnp.int32, (LANES, n), 0)
    dst_head = lax.broadcasted_iota(jnp.int32, (LANES, n), 1) // cs
    tri, tri_strict, sel_g, sel_b, widen_g, widen_b = {}, {}, {}, {}, {}, {}
    for fwd in (True, False):
        ahead = ri - ci if fwd else ci - ri
        tri[fwd] = head & (ahead >= 0)
        tri_strict[fwd] = head & (ahead > 0)
        lane0 = 0 if fwd else 8
        sel_g[fwd] = lane == lane0 + row_head
        sel_b[fwd] = lane == lane0 + 4 + row_head
        widen_g[fwd] = jnp.where(src_lane == lane0 + dst_head, 1.0, 0.0).astype(BF16)
        widen_b[fwd] = jnp.where(src_lane == lane0 + 4 + dst_head, 1.0, 0.0).astype(BF16)

    n_chunks = q_ref.shape[0] // cs
    rows = [slice(ch * cs, (ch + 1) * cs) for ch in range(n_chunks)]
    items = [(fwd, ch) for fwd in (True, False) for ch in range(n_chunks)]
    dirs = [fwd for fwd, _ in items]
    per_item = lambda xs: [xs[ch] for _, ch in items]
    each = lambda f, *xs: [f(*a) for a in zip(*xs)]
    bf = lambda xs: [x.astype(BF16) for x in xs]
    widen = lambda x, e: functools.reduce(jnp.add, [_dot(p, e) for p in _split3(x)])
    zero_bf = jnp.zeros((n, n), BF16)
    spread = lambda xs: [jnp.where(head, _tile_heads(x.astype(BF16)), zero_bf) for x in xs]
    k_c, q_c, v_c = ([ref[rw, :] for rw in rows] for ref in (k_ref, q_ref, v_ref))
    khb_c, qhb_c = spread(k_c), spread(q_c)
    kk = per_item(each(_dot_nt, khb_c, khb_c))
    qk = per_item(each(_dot_nt, qhb_c, khb_c))
    k_t, q_t, v_t = per_item(k_c), per_item(q_c), per_item(v_c)
    gb = per_item([gb_ref[rw, :] for rw in rows])
    gb4 = [_tile_heads(x) for x in gb]
    gc = [pick(sel_g[f], x) for f, x in zip(dirs, gb4)]
    beta = [pick(sel_b[f], x) for f, x in zip(dirs, gb4)]
    gc_t = [widen(x, widen_g[f]) for f, x in zip(dirs, gb)]
    beta_t = [widen(x, widen_b[f]) for f, x in zip(dirs, gb)]
    ends = [cs - 1 if f else 0 for f in dirs]
    gl_t = [jnp.broadcast_to(x[e:e + 1, :], x.shape) for x, e in zip(gc_t, ends)]
    gc_b = [jnp.broadcast_to(x, (n, n)) for x in gc]
    decay = [jnp.exp(jnp.minimum(x - x.T, 0.0)) for x in gc_b]
    lmat = [jnp.where(tri_strict[f], b_ * kk_ * d_, 0.0) for f, b_, kk_, d_ in zip(dirs, beta, kk, decay)]
    attn = bf([jnp.where(tri[f], qk_ * d_, 0.0) for f, qk_, d_ in zip(dirs, qk, decay)])
    nl = bf([jnp.where(b8, -x, 0.0) for x in lmat])
    n2 = bf(each(_dot, nl, nl))
    n4 = each(_dot, n2, n2)
    p1 = bf(each(lambda a, b_: _dot((eye + a).astype(BF16), (eye + b_).astype(BF16)), nl, n2))
    tinv = each(lambda p, x: _dot(p, (eye + x).astype(BF16)), p1, n4)

    def moving_rows(fwd, x, sz):
        return jnp.concatenate([x[i:i + sz] for i in range(sz if fwd else 0, n, 2 * sz)], axis=0)

    def with_moving_rows(fwd, x, new, sz):
        pieces = []
        for j, i in enumerate(range(0, n, 2 * sz)):
            kept = x[i:i + sz] if fwd else x[i + sz:i + 2 * sz]
            moved = new[j * sz:(j + 1) * sz]
            pieces += [kept, moved] if fwd else [moved, kept]
        return jnp.concatenate(pieces, axis=0)

    for sz, inner, outer in ((8, b8, b16), (16, b16, b32), (32, b32, head)):
        off = bf([jnp.where(outer & ~inner, x, 0.0) for x in lmat])
        tb = bf(tinv)
        t_mv = [moving_rows(f, x, sz) for f, x in zip(dirs, tinv)]
        to = bf(each(_dot, bf(t_mv), off))
        tinv = [with_moving_rows(f, t_, tm_ - _dot(to_, tb_), sz)
                for f, t_, tm_, to_, tb_ in zip(dirs, tinv, t_mv, to, tb)]
    tb = bf(tinv)
    eg_t = [jnp.exp(x) for x in gc_t]
    u = bf(each(_dot, tb, spread(each(lambda b_, v_: b_ * v_, beta_t, v_t))))
    w = bf(each(_dot, tb, spread(each(lambda b_, e_, k_: (b_ * e_) * k_, beta_t, eg_t, k_t))))
    o0 = each(_dot, attn, u)
    aw = each(_dot, attn, w)
    kg = spread(each(lambda k_, gl_, gc_: k_ * jnp.exp(gl_ - gc_), k_t, gl_t, gc_t))
    a_mat = each(_dot_tn, kg, w)
    b_mat = each(_dot_tn, kg, u)
    for (fwd, ch), o0_, aw_, a_, b_, q_, e_, gl_ in zip(items, o0, aw, a_mat, b_mat, q_t, eg_t, gl_t):
        dr, rw = 0 if fwd else 1, rows[ch]
        o0_ref[dr, rw, :] = _collapse_heads(o0_)
        qe_ref[dr, rw, :] = q_ * e_ - _collapse_heads(aw_)
        a_ref[dr, rw, :] = _collapse_heads(a_)
        bm_ref[dr, rw, :] = _collapse_heads(b_)
        gam_ref[dr, rw, :] = jnp.exp(gl_)


def gdn_chunks(q, k, v, gb):
    b, l, _ = q.shape
    tm = ROW_TILE
    row = lambda i, t: (i, t, 0)
    out = pl.BlockSpec((2, None, tm, GDN_W), lambda i, t: (0, i, t, 0))
    return pl.pallas_call(
        _gdn_chunk_kernel,
        grid=(b, l // tm),
        in_specs=[pl.BlockSpec((None, tm, GDN_W), row)] * 3 + [pl.BlockSpec((None, tm, LANES), row)],
        out_specs=[out] * 5,
        out_shape=[jax.ShapeDtypeStruct((2, b, l, GDN_W), F32)] * 5,
        compiler_params=pltpu.CompilerParams(
            dimension_semantics=("parallel", "parallel"), vmem_limit_bytes=VMEM_LIMIT),
        name="gdn_chunk",
    )(q, k, v, gb)


def _gdn_scan_kernel(o0_ref, qe_ref, a_ref, bm_ref, gam_ref, gate_ref, g_ref, o_ref, s_ref, part_ref):
    dr = pl.program_id(0)
    s = pl.program_id(1)
    ns = pl.num_programs(1)
    nb, tm = o0_ref.shape[0], o0_ref.shape[1]
    cs = GDN_CHUNK
    n = GDN_W
    nch = tm // cs
    bs = range(nb)
    tile = jnp.where(s == 0, 0, jnp.where(dr == 0, s, ns - s))
    r = lax.broadcasted_iota(jnp.int32, (n, n), 0)
    c = lax.broadcasted_iota(jnp.int32, (n, n), 1)
    head = r // cs == c // cs

    @pl.when(s == 0)
    def _():
        s_ref[...] = jnp.zeros_like(s_ref)

    def run(order):
        state = [s_ref[i] for i in bs]
        outs = [{} for _ in bs]
        for ch in order:
            rows = slice(ch * cs, (ch + 1) * cs)
            sb = [x.astype(BF16) for x in state]
            for i in bs:
                outs[i][ch] = o0_ref[i, rows, :] + _dot(qe_ref[i, rows, :].astype(BF16), sb[i])
            a_full = [jnp.where(head, _tile_heads(a_ref[i, rows, :]), 0.0).astype(BF16) for i in bs]
            state = [_tile_heads(gam_ref[i, rows, :]) * state[i] - _dot(a_full[i], sb[i])
                     + jnp.where(head, _tile_heads(bm_ref[i, rows, :]), 0.0) for i in bs]
        for i in bs:
            s_ref[i] = state[i]
        return [jnp.concatenate([outs[i][ch] for ch in range(nch)], axis=0) for i in bs]

    rows_out = pl.ds(pl.multiple_of(tile * tm, tm), tm)

    @pl.when(dr == 0)
    def _():
        for i, o in enumerate(run(range(nch))):
            part_ref[i, rows_out, :] = o

    @pl.when(dr == 1)
    def _():
        ones_bd = jnp.where(head, 1.0, 0.0).astype(BF16)
        for i, o_bwd in enumerate(run(range(nch - 1, -1, -1))):
            o = part_ref[i, rows_out, :] + o_bwd
            ms = _head_mean(o * o, ones_bd)
            o_ref[i] = (o * lax.rsqrt(ms + EPS) * g_ref[...] * _silu(gate_ref[i])).astype(o_ref.dtype)


def gdn_scan(o0, qe, a, bm, gam, pb, norm_g):
    _, b, l, _ = o0.shape
    tm = ROW_TILE
    ns = l // tm

    def tile_of(dr, s):
        return jnp.where(s == 0, 0, jnp.where(dr == 0, s, ns - s))

    per_dir = pl.BlockSpec((None, b, tm, GDN_W), lambda dr, s: (dr, 0, tile_of(dr, s), 0))
    gate_col = 3 * GDN_W // GDN_W
    return pl.pallas_call(
        _gdn_scan_kernel,
        grid=(2, ns),
        in_specs=[per_dir] * 5 + [
            pl.BlockSpec((b, tm, GDN_W), lambda dr, s: (0, tile_of(dr, s), gate_col)),
            pl.BlockSpec((1, GDN_W), lambda dr, s: (0, 0))],
        out_specs=pl.BlockSpec((b, tm, GDN_W), lambda dr, s: (0, jnp.where(dr == 0, 0, tile_of(dr, s)), 0)),
        out_shape=jax.ShapeDtypeStruct((b, l, GDN_W), BF16),
        scratch_shapes=[pltpu.VMEM((b, GDN_W, GDN_W), F32), pltpu.VMEM((b, l, GDN_W), F32)],
        compiler_params=pltpu.CompilerParams(
            dimension_semantics=("arbitrary", "arbitrary"), vmem_limit_bytes=VMEM_LIMIT),
        name="gdn_scan",
    )(o0, qe, a, bm, gam, pb, jnp.tile(norm_g, GDN_HEADS)[None, :])


def gdn_mixer(pb, conv_w, a_log, dt_bias, norm_g):
    q, k, v, gb = gdn_prep(pb, conv_w, a_log, dt_bias)
    o0, qe, a, bm, gam = gdn_chunks(q, k, v, gb)
    return gdn_scan(o0, qe, a, bm, gam, pb, norm_g)


def kernel(x, c, ctx, c_ctx, w_mod, b_mod, norm1, norm2, w_in, w_out, swa_sink, gdn_conv, gdn_a_log, gdn_dt_bias, gdn_norm, mla_q_norm, mla_kv_norm, mla_w_q_up, mla_w_kv_up, ret_log_decay, ret_norm, ffn_w_gate, ffn_w_up, ffn_w_down, moe_router, moe_w_gate, moe_w_up, moe_w_down, final_norm):
    b, n, d = x.shape
    depth = w_in.shape[0]
    cos_t, sin_t = rope_tables(n)
    cos_c, sin_c = cos_t[:, A_ROT_W:A_ROT_W + C_ROT_W], sin_t[:, A_ROT_W:A_ROT_W + C_ROT_W]
    h = jnp.concatenate([ctx, x], axis=1)
    cond = jnp.concatenate([jax.nn.silu(c_ctx)[None, :], jax.nn.silu(c)], axis=0)
    mods = jnp.einsum("bd,ldk->lbk", cond, w_mod, precision=lax.Precision.HIGHEST) + b_mod[:, None, :]
    mods = mods.reshape(depth, 1 + b, 6, d)
    mods = jnp.stack([jnp.broadcast_to(mods[:, :1], (depth, b, 6, d)), mods[:, 1:]], axis=2)
    mods = jnp.pad(mods, ((0, 0), (0, 0), (0, 0), (0, SUBLANES - 6), (0, 0)))
    w_in_all = build_in_weight(w_in)
    w_out_all = build_out_weight(w_out)
    ffn_wg, ffn_wu, ffn_wd = ffn_w_gate.astype(BF16), ffn_w_up.astype(BF16), ffn_w_down.astype(BF16)
    moe_wg, moe_wu, moe_wd = moe_w_gate.astype(BF16), moe_w_up.astype(BF16), moe_w_down.astype(BF16)
    for layer in range(depth):
        mod = mods[layer]
        pa, pb, pc, pd = norm_proj(h, mod, norm1[layer][None, :], w_in_all, layer, cos_t, sin_t)
        mix_a = swa_mixer(pa, swa_sink[layer])
        mix_b = gdn_mixer(pb, gdn_conv[layer], gdn_a_log[layer], gdn_dt_bias[layer], gdn_norm[layer])
        mq, mk, mv = mla_prep(pc, mla_q_norm[layer], mla_kv_norm[layer], mla_w_q_up[layer], mla_w_kv_up[layer],
                              cos_c, sin_c)
        mix_c = mla_attention(mq, mk, mv)
        mix_d = retention_mixer(pd, ret_log_decay[layer], ret_norm[layer])
        mixes = (mix_a, mix_b, mix_c, mix_d)
        i = layer // 2
        if layer % 2 == 0:
            h, v = out_proj(mixes, h, mod, norm2[layer][None, :], w_out_all, layer)
            h = dense_ffn(v, h, mod, i, ffn_wg, ffn_wu, ffn_wd)
        else:
            w_r = jnp.pad(moe_router[i], ((0, 0), (0, LANES - N_EXPERTS)))
            h, v, logits = out_proj(mixes, h, mod, norm2[layer][None, :], w_out_all, layer, w_r)
            if layer == depth - 1:
                return moe_ffn(v, logits, h, mod, i, moe_wg, moe_wu, moe_wd, final_norm[None, :])
            h = moe_ffn(v, logits, h, mod, i, moe_wg, moe_wu, moe_wd)
    return final_rms_norm(h, final_norm[None, :], CTX_LEN // ROW_TILE)
```

```python
import functools

import numpy as np
import jax
import jax.numpy as jnp
from jax import lax
from jax.experimental import pallas as pl
from jax.experimental.pallas import tpu as pltpu

D_MODEL = 1024
GRID_W = 64
CTX_LEN = 256
HEAD_DIM = 64
ROPE_THETA = 10000.0
EPS = 1e-6
NEG_INF = -1e30

SWA_WINDOW = 128
GDN_HEADS = 4
GDN_DK = 64
GDN_DV = 64
GDN_CHUNK = 64
MLA_HEADS = 4
MLA_Q_RANK = 256
MLA_NOPE = 64
MLA_ROPE = 32
MLA_V = 64
RET_HEADS = 4
RET_DK = 64
D_FF = 3584
N_EXPERTS = 8
TOP_K = 2

LANES = 128
SUBLANES = 8
VMEM_LIMIT = 56 * 1024 * 1024

ROW_TILE = 256
FF_CHUNK = 512
MOE_TILE = 512
MOE_PART = 256
MOE_FF_CHUNK = 1792

A_W, B_W, C_W, D_W = 768, 1152, 512, 1024
A_ROT_W, C_ROT_W, D_ROT_W = 640, 128, 512
OFF_A = 0
OFF_B = OFF_A + A_W
OFF_C = OFF_B + B_W
OFF_D = OFF_C + C_W
OFF_AR = OFF_D + D_W
OFF_CR = OFF_AR + A_ROT_W
OFF_DR = OFF_CR + C_ROT_W
W_ALL = OFF_DR + D_ROT_W
ROPE_W = A_ROT_W + C_ROT_W + D_ROT_W

LOG2_E = float(np.log2(np.e))
F32 = jnp.float32
BF16 = jnp.bfloat16
NT_DIMS = (((1,), (1,)), ((), ()))
TN_DIMS = (((0,), (0,)), ((), ()))


def _rms(x):
    return x * lax.rsqrt(jnp.mean(x * x, axis=-1, keepdims=True) + EPS)


def _silu(x):
    return x * (1.0 / (1.0 + jnp.exp(-x)))


def _dot(a, b):
    return jnp.dot(a, b, preferred_element_type=F32)


def _dot_nt(a, b):
    return lax.dot_general(a, b, NT_DIMS, preferred_element_type=F32)


def _dot_tn(a, b):
    return lax.dot_general(a, b, TN_DIMS, preferred_element_type=F32)


WIDE = 2


def _sub_tile_specs(block, tiles_per_seq, index_of):
    def spec(k):
        return pl.BlockSpec(block, lambda j: index_of((WIDE * j + k) // tiles_per_seq, (WIDE * j + k) % tiles_per_seq))
    return [spec(k) for k in range(WIDE)]


def _sub_mod_specs(d, tiles_per_seq):
    return _sub_tile_specs((None, None, SUBLANES, d), tiles_per_seq, lambda b, t: (b, jnp.minimum(t, 1), 0, 0))


def _sub_rows(k):
    return slice(k * ROW_TILE, (k + 1) * ROW_TILE)


def _norm_proj_kernel(h_ref, *refs):
    mods, (g_ref, w_ref), tabs = refs[:WIDE], refs[WIDE:WIDE + 2], refs[WIDE + 2:3 * WIDE + 2]
    qn_ref, kvn_ref, wq_ref, wqr_ref, wk_ref, wv_ref = refs[3 * WIDE + 2:3 * WIDE + 8]
    a_ref, b_ref, mq_ref, mk_ref, mv_ref, d_ref = refs[3 * WIDE + 8:]
    y = _rms(h_ref[...]) * g_ref[...]
    u = jnp.concatenate([y[_sub_rows(k)] * (1.0 + m[1:2, :]) + m[0:1, :] for k, m in enumerate(mods)],
                        axis=0).astype(BF16)

    def mm(lo, width):
        return _dot(u, w_ref[:, lo:lo + width])

    a_main = mm(OFF_A, A_W)
    a_rot = mm(OFF_AR, A_ROT_W)
    b_ref[...] = mm(OFF_B, B_W)
    c_main = mm(OFF_C, C_W)
    c_rot = mm(OFF_CR, C_ROT_W)
    d_main = mm(OFF_D, D_W)
    d_rot = mm(OFF_DR, D_ROT_W)
    lo, hi = A_ROT_W, A_ROT_W + C_ROT_W
    a_ref[:, A_ROT_W:] = a_main[:, A_ROT_W:].astype(BF16)
    d_ref[:, D_ROT_W:] = d_main[:, D_ROT_W:]
    nq = (_rms(c_main[:, :MLA_Q_RANK]) * qn_ref[...]).astype(BF16)
    nkv = (_rms(c_main[:, MLA_Q_RANK:C_W - C_ROT_W]) * kvn_ref[...]).astype(BF16)
    q_main, q_rot = _dot(nq, wq_ref[...]), _dot(nq, wqr_ref[...])
    k_nope = _dot(nkv, wk_ref[...])
    mv_ref[...] = _dot(nkv, wv_ref[...]).astype(BF16)
    per_head = lambda x: jnp.concatenate([x] * MLA_HEADS, axis=1)
    for k in range(WIDE):
        cos_ref, sin_ref = tabs[2 * k], tabs[2 * k + 1]
        r = _sub_rows(k)
        a_ref[r, :A_ROT_W] = (a_main[r, :A_ROT_W] * cos_ref[:, :A_ROT_W] + a_rot[r] * sin_ref[:, :A_ROT_W]).astype(BF16)
        cos_c, sin_c = cos_ref[:, lo:hi], sin_ref[:, lo:hi]
        k_rope = c_main[r, C_W - C_ROT_W:] * cos_c + c_rot[r] * sin_c
        mq_ref[r, :] = (q_main[r] * per_head(cos_c) + q_rot[r] * per_head(sin_c)).astype(BF16)
        mk_ref[r, :] = (k_nope[r] + per_head(k_rope)).astype(BF16)
        d_ref[r, :D_ROT_W] = d_main[r, :D_ROT_W] * cos_ref[:, hi:] + d_rot[r] * sin_ref[:, hi:]


def _layer_weight_spec(w, layer):
    return pl.BlockSpec((None,) + w.shape[1:], lambda j: (layer, 0, 0), pipeline_mode=pl.Buffered(1))


def norm_proj(h, mod, gain, w, layer, cos_t, sin_t, mla_weights):
    b, l, d = h.shape
    tm = WIDE * ROW_TILE
    tps = l // ROW_TILE
    row = lambda j: (j, 0)
    const = lambda j: (0, 0)
    tabs = _sub_tile_specs((ROW_TILE, ROPE_W), tps, lambda bi, t: (t, 0))
    tab_specs = [s for pair in zip(tabs, _sub_tile_specs((ROW_TILE, ROPE_W), tps, lambda bi, t: (t, 0))) for s in pair]
    outs = pl.pallas_call(
        _norm_proj_kernel,
        grid=(b * l // tm,),
        in_specs=[pl.BlockSpec((tm, d), row)] + _sub_mod_specs(d, tps) + [
            pl.BlockSpec((1, d), const),
            _layer_weight_spec(w, layer)] + tab_specs + [pl.BlockSpec(a.shape, const) for a in mla_weights],
        out_specs=[pl.BlockSpec((tm, A_W), row), pl.BlockSpec((tm, B_W), row), pl.BlockSpec((tm, 4 * LANES), row),
                   pl.BlockSpec((tm, 4 * LANES), row), pl.BlockSpec((tm, 2 * LANES), row), pl.BlockSpec((tm, D_W), row)],
        out_shape=[jax.ShapeDtypeStruct((b * l, A_W), BF16), jax.ShapeDtypeStruct((b * l, B_W), F32),
                   jax.ShapeDtypeStruct((b * l, 4 * LANES), BF16), jax.ShapeDtypeStruct((b * l, 4 * LANES), BF16),
                   jax.ShapeDtypeStruct((b * l, 2 * LANES), BF16), jax.ShapeDtypeStruct((b * l, D_W), F32)],
        compiler_params=pltpu.CompilerParams(dimension_semantics=("parallel",), vmem_limit_bytes=VMEM_LIMIT),
        name="norm_proj",
    )(h.reshape(b * l, d), *([mod] * WIDE), gain, w, *([cos_t, sin_t] * WIDE), *mla_weights)
    pa, pb, mq, mk, mv, pd = [o.reshape(b, l, -1) for o in outs]
    return pa, pb, (mq, mk, mv), pd


def _rot_cols(w, hd):
    x = w.reshape(w.shape[:-1] + (w.shape[-1] // hd, 4, hd // 4))
    x1, x2, x3, x4 = x[..., 0, :], x[..., 1, :], x[..., 2, :], x[..., 3, :]
    return jnp.stack([-x2, x1, -x4, x3], axis=-2).reshape(w.shape)


def _place_swa_q(q):
    z = jnp.zeros(q.shape[:-1] + (HEAD_DIM,), q.dtype)
    blocks = []
    for h in range(4):
        qh = q[..., HEAD_DIM * h:HEAD_DIM * (h + 1)]
        blocks += [qh, z] if h // 2 == 0 else [z, qh]
    return jnp.concatenate(blocks, axis=-1)


def build_in_weight(w):
    o = [int(v) for v in np.cumsum((256, 128, 128, 768, 256, 16, 256, 128, 32, 256, 256, 256, 256))]
    aq, ak, av = w[..., :o[0]] * (HEAD_DIM ** -0.5 * LOG2_E), w[..., o[0]:o[1]], w[..., o[1]:o[2]]
    b_main, b_ab = w[..., o[2]:o[4]], w[..., o[4]:o[5]]
    c_q, c_kv, c_kr = w[..., o[5]:o[6]], w[..., o[6]:o[7]], w[..., o[7]:o[8]]
    dq, dk, dvg = w[..., o[8]:o[9]], w[..., o[9]:o[10]] * RET_DK ** -0.5, w[..., o[10]:]
    z = lambda n: jnp.zeros(w.shape[:-1] + (n,), w.dtype)
    parts = [
        _place_swa_q(aq), ak, av,
        b_main, b_ab, z(LANES - b_ab.shape[-1]),
        c_q, c_kv, z(64), c_kr, z(32),
        dq, dk, dvg,
        _place_swa_q(_rot_cols(aq, HEAD_DIM)), _rot_cols(ak, HEAD_DIM),
        z(64), _rot_cols(c_kr, MLA_ROPE), z(32),
        _rot_cols(dq, HEAD_DIM), _rot_cols(dk, HEAD_DIM),
    ]
    out = jnp.concatenate(parts, axis=-1)
    assert out.shape[-1] == W_ALL
    return out.astype(BF16)


def rope_tables(n):
    lat = jnp.arange(CTX_LEN + n, dtype=jnp.int32) - CTX_LEN
    grid_row = jnp.where(lat >= 0, lat // GRID_W, 0).astype(F32)[:, None]
    grid_col = jnp.where(lat >= 0, lat % GRID_W, 0).astype(F32)[:, None]
    narrow, col_of = [], {}
    for rot_dim in (HEAD_DIM, MLA_ROPE):
        n_freq = rot_dim // 4
        inv_freq = ROPE_THETA ** (-jnp.arange(n_freq, dtype=F32) / n_freq)
        for axis, pos in enumerate((grid_row, grid_col)):
            col_of[rot_dim, axis] = sum(a.shape[1] for a in narrow)
            narrow.append(pos * inv_freq)
    ang = jnp.concatenate(narrow, axis=1)
    identity_col = ang.shape[1]
    sel = np.zeros((identity_col + 1, ROPE_W), np.float32)

    def plan(lane0, width, group, rot_lo, rot_dim):
        for c in range(width):
            j = c % group - rot_lo
            if 0 <= j < rot_dim:
                quarter, f = divmod(j, rot_dim // 4)
                sel[col_of[rot_dim, quarter // 2] + f, lane0 + c] = 1.0
            else:
                sel[identity_col, lane0 + c] = 1.0

    plan(0, A_ROT_W, HEAD_DIM, 0, HEAD_DIM)
    plan(A_ROT_W, C_ROT_W, C_ROT_W, 64, MLA_ROPE)
    plan(A_ROT_W + C_ROT_W, D_ROT_W, HEAD_DIM, 0, HEAD_DIM)
    spread = lambda t: jnp.dot(t, jnp.asarray(sel), precision=lax.Precision.HIGHEST)
    ones, zeros = jnp.ones_like(grid_row), jnp.zeros_like(grid_row)
    return (spread(jnp.concatenate([jnp.cos(ang), ones], axis=1)),
            spread(jnp.concatenate([jnp.sin(ang), zeros], axis=1)))


def _swa_kernel(sink_ref, q_ref, kp_ref, ko_ref, kn_ref, kc_ref, vp_ref, vo_ref, vn_ref, vc_ref, o_ref):
    t = pl.program_id(1)
    last = pl.num_programs(1) - 1
    tq = q_ref.shape[0]
    half = tq // 2

    def attend(k, v, mask):
        v_lane = lax.broadcasted_iota(jnp.int32, v.shape, 1)
        v_ones = [jnp.where(v_lane >= HEAD_DIM, jnp.ones_like(v), v), jnp.where(v_lane < HEAD_DIM, jnp.ones_like(v), v)]
        heads = range(4)
        s = [_dot_nt(q_ref[:, LANES * h:LANES * (h + 1)], k) for h in heads]
        if mask is not None:
            s = [jnp.where(mask, x, NEG_INF) for x in s]
        sink = [sink_ref[h] * LOG2_E for h in heads]
        m = [jnp.maximum(s[h].max(axis=-1, keepdims=True), sink[h]) for h in heads]
        p = [jnp.exp2(s[h] - m[h]).astype(BF16) for h in heads]
        o = [_dot(p[h], v_ones[h // 2]) for h in heads]
        den_lane = [HEAD_DIM if h // 2 == 0 else 0 for h in heads]
        outs = [o[h] / (o[h][:, den_lane[h]:den_lane[h] + 1] + jnp.exp2(sink[h] - m[h])) for h in heads]
        lane = lax.broadcasted_iota(jnp.int32, (tq, LANES), 1)
        for r in range(2):
            o_ref[:, LANES * r:LANES * (r + 1)] = jnp.where(lane < HEAD_DIM, outs[r], outs[2 + r]).astype(o_ref.dtype)

    @pl.when(t == 0)
    def _():
        attend(kc_ref[...], vc_ref[...], None)

    @pl.when(t > 0)
    def _():
        band = 2 * half + tq
        qi = lax.broadcasted_iota(jnp.int32, (tq, band + CTX_LEN), 0)
        col = lax.broadcasted_iota(jnp.int32, (tq, band + CTX_LEN), 1)
        in_window = jnp.abs(col - half - qi) <= SWA_WINDOW
        exists = ((col >= half) | (t > 1)) & ((col < half + tq) | (t < last))
        mask = (col >= band) | (in_window & exists)
        attend(jnp.concatenate([kp_ref[...], ko_ref[...], kn_ref[...], kc_ref[...]], axis=0),
               jnp.concatenate([vp_ref[...], vo_ref[...], vn_ref[...], vc_ref[...]], axis=0), mask)


def swa_mixer(pa, sink):
    b, l, _ = pa.shape
    tq = ROW_TILE
    nblk = l // SWA_WINDOW
    kcol, vcol = 4, 5
    prev = lambda c: (lambda i, t: (i, jnp.maximum(2 * t - 1, 2), c))
    nxt = lambda c: (lambda i, t: (i, jnp.minimum(2 * t + 2, nblk - 1), c))
    own = lambda c: (lambda i, t: (i, t, c))
    ctx = lambda c: (lambda i, t: (i, 0, c))
    kv_specs = lambda c: [pl.BlockSpec((None, SWA_WINDOW, LANES), prev(c)), pl.BlockSpec((None, tq, LANES), own(c)),
                          pl.BlockSpec((None, SWA_WINDOW, LANES), nxt(c)), pl.BlockSpec((None, tq, LANES), ctx(c))]
    return pl.pallas_call(
        _swa_kernel,
        grid=(b, l // tq),
        in_specs=[pl.BlockSpec(memory_space=pltpu.SMEM),
                  pl.BlockSpec((None, tq, 4 * LANES), lambda i, t: (i, t, 0))] + kv_specs(kcol) + kv_specs(vcol),
        out_specs=pl.BlockSpec((None, tq, 2 * LANES), lambda i, t: (i, t, 0)),
        out_shape=jax.ShapeDtypeStruct((b, l, 2 * LANES), BF16),
        compiler_params=pltpu.CompilerParams(
            dimension_semantics=("parallel", "parallel"), vmem_limit_bytes=VMEM_LIMIT),
        name="swa",
    )(sink, pa, pa, pa, pa, pa, pa, pa, pa, pa)


def build_mla_weights(q_norm, kv_norm, w_q_up, w_kv_up):
    scale = (MLA_NOPE + MLA_ROPE) ** -0.5 * LOG2_E
    wq = (w_q_up * scale).reshape(-1, MLA_HEADS, MLA_NOPE + MLA_ROPE)
    zq = jnp.zeros(wq.shape[:2] + (LANES - MLA_NOPE - MLA_ROPE,), F32)
    wq_main = jnp.concatenate([wq, zq], axis=-1).reshape(-1, MLA_HEADS * LANES)
    wq_rot = jnp.concatenate([jnp.zeros_like(wq[..., :MLA_NOPE]), _rot_cols(wq[..., MLA_NOPE:], MLA_ROPE), zq],
                             axis=-1).reshape(-1, MLA_HEADS * LANES)
    wkv = w_kv_up.reshape(-1, MLA_HEADS, MLA_NOPE + MLA_V)
    wk = jnp.concatenate([wkv[..., :MLA_NOPE], jnp.zeros_like(wkv[..., :LANES - MLA_NOPE])],
                         axis=-1).reshape(-1, MLA_HEADS * LANES)
    wv = wkv[..., MLA_NOPE:].reshape(-1, MLA_HEADS * MLA_V)
    return [q_norm[None, :], kv_norm[None, :], wq_main.astype(BF16), wq_rot.astype(BF16), wk.astype(BF16),
            wv.astype(BF16)]


def _mla_attn_kernel(q_ref, k_ref, v_ref, o_ref):
    t = pl.program_id(2)
    tq = q_ref.shape[0]

    def attend(nk):
        v = v_ref[0:nk, :]
        v_lane = lax.broadcasted_iota(jnp.int32, v.shape, 1)
        v_ones = [jnp.where(v_lane >= MLA_V, jnp.ones_like(v), v), jnp.where(v_lane < MLA_V, jnp.ones_like(v), v)]
        pair = range(2)
        s = [_dot_nt(q_ref[:, LANES * j:LANES * (j + 1)], k_ref[0:nk, LANES * j:LANES * (j + 1)]) for j in pair]
        p = [jnp.exp2(s[j] - s[j].max(axis=-1, keepdims=True)).astype(BF16) for j in pair]
        o = [_dot(p[j], v_ones[j]) for j in pair]
        outs = [o[j] / o[j][:, (MLA_V, 0)[j]:(MLA_V, 0)[j] + 1] for j in pair]
        lane = lax.broadcasted_iota(jnp.int32, (tq, LANES), 1)
        o_ref[...] = jnp.where(lane < MLA_V, outs[0], outs[1]).astype(o_ref.dtype)

    @pl.when(t == 0)
    def _():
        attend(CTX_LEN)

    @pl.when(t > 0)
    def _():
        attend(k_ref.shape[0])


def mla_attention(q, k, v):
    b, l, _ = q.shape
    tq = ROW_TILE
    return pl.pallas_call(
        _mla_attn_kernel,
        grid=(b, 2, l // tq),
        in_specs=[pl.BlockSpec((None, tq, 2 * LANES), lambda i, p, t: (i, t, p)),
                  pl.BlockSpec((None, l, 2 * LANES), lambda i, p, t: (i, 0, p)),
                  pl.BlockSpec((None, l, LANES), lambda i, p, t: (i, 0, p))],
        out_specs=pl.BlockSpec((None, tq, LANES), lambda i, p, t: (i, t, p)),
        out_shape=jax.ShapeDtypeStruct((b, l, 2 * LANES), BF16),
        compiler_params=pltpu.CompilerParams(
            dimension_semantics=("parallel", "parallel", "parallel"), vmem_limit_bytes=VMEM_LIMIT),
        name="mla_attn",
    )(q, k, v)


def _head_mean(x, ones_bd):
    hi = x.astype(BF16)
    lo = (x - hi.astype(F32)).astype(BF16)
    return (_dot(hi, ones_bd) + _dot(lo, ones_bd)) * (1.0 / HEAD_DIM)


def _ret_kernel(x_ref, lg_ref, g_ref, o_ref, s_ref, dec_ref, part_ref):
    dr = pl.program_id(0)
    s = pl.program_id(1)
    ns = pl.num_programs(1)
    nb, c = x_ref.shape[0], x_ref.shape[1]
    w = RET_HEADS * HEAD_DIM
    chunk = jnp.where(s == 0, 0, jnp.where(dr == 0, s, ns - s))
    lg = lg_ref[...]
    fwd = dr == 0
    row_h = lax.broadcasted_iota(jnp.int32, (w, w), 0) // HEAD_DIM
    col_h = lax.broadcasted_iota(jnp.int32, (w, w), 1) // HEAD_DIM
    same_head = row_h == col_h

    @pl.when(s == 0)
    def _():
        s_ref[...] = jnp.zeros_like(s_ref)
        i = lax.broadcasted_iota(jnp.int32, (c, c), 0)
        j = lax.broadcasted_iota(jnp.int32, (c, c), 1)
        rel = jnp.where(fwd, i - j, j - i)
        relf = jnp.maximum(rel, 0).astype(F32)
        for h in range(RET_HEADS):
            lg_h = lg_ref[0:1, HEAD_DIM * h:HEAD_DIM * h + 1]
            dec_ref[h] = jnp.where(rel >= 0, jnp.exp(lg_h * relf), 0.0)

    pos = lax.broadcasted_iota(jnp.int32, (c, 1), 0).astype(F32)
    q_dec = jnp.exp(lg * jnp.where(fwd, pos + 1.0, c - pos))
    k_dec = jnp.exp(lg * jnp.where(fwd, c - 1.0 - pos, pos))
    lane_h = lax.broadcasted_iota(jnp.int32, (c, w), 1) // HEAD_DIM
    bs = range(nb)
    q = [x_ref[i, :, 0:w] for i in bs]
    kf = [x_ref[i, :, w:2 * w] for i in bs]
    v = [x_ref[i, :, 2 * w:3 * w].astype(BF16) for i in bs]
    kb = [x.astype(BF16) for x in kf]
    acc = [_dot((q[i] * q_dec).astype(BF16), s_ref[i].astype(BF16)) for i in bs]
    for h in range(RET_HEADS):
        qh = [jnp.where(lane_h == h, q[i], 0.0).astype(BF16) for i in bs]
        a = [(_dot_nt(qh[i], kb[i]) * dec_ref[h]).astype(BF16) for i in bs]
        acc = [acc[i] + jnp.where(lane_h == h, _dot(a[i], v[i]), 0.0) for i in bs]
    kv = [_dot_tn((kf[i] * k_dec).astype(BF16), v[i]) for i in bs]
    chunk_dec = jnp.exp(lg * float(c))
    for i in bs:
        s_ref[i] = s_ref[i] * chunk_dec + jnp.where(same_head, kv[i], 0.0)

    rows = pl.ds(pl.multiple_of(chunk * c, c), c)

    @pl.when(dr == 0)
    def _():
        for i in bs:
            part_ref[i, rows, :] = acc[i]

    @pl.when(dr == 1)
    def _():
        ones_bd = jnp.where(same_head, 1.0, 0.0).astype(BF16)
        for i in bs:
            o = part_ref[i, rows, :] + acc[i]
            mu = _head_mean(o, ones_bd)
            var = _head_mean(jnp.square(o - mu), ones_bd)
            y = (o - mu) * lax.rsqrt(var + EPS) * g_ref[...]
            o_ref[i] = (y * _silu(x_ref[i, :, 3 * w:4 * w])).astype(o_ref.dtype)


def retention_mixer(pd, log_decay, norm_g):
    b, l, _ = pd.shape
    c = ROW_TILE
    ns = l // c
    w = RET_HEADS * HEAD_DIM
    lg = jnp.repeat(-jnp.exp(log_decay.astype(F32)), HEAD_DIM, axis=-1)[:, None, :]

    def chunk_of(dr, s):
        return jnp.where(s == 0, 0, jnp.where(dr == 0, s, ns - s))

    return pl.pallas_call(
        _ret_kernel,
        grid=(2, ns),
        in_specs=[pl.BlockSpec((b, c, D_W), lambda dr, s: (0, chunk_of(dr, s), 0)),
                  pl.BlockSpec((None, 1, w), lambda dr, s: (dr, 0, 0)),
                  pl.BlockSpec((1, w), lambda dr, s: (0, 0))],
        out_specs=pl.BlockSpec((b, c, w), lambda dr, s: (0, jnp.where(dr == 0, 0, chunk_of(dr, s)), 0)),
        out_shape=jax.ShapeDtypeStruct((b, l, w), BF16),
        scratch_shapes=[pltpu.VMEM((b, w, w), F32), pltpu.VMEM((RET_HEADS, c, c), F32), pltpu.VMEM((b, l, w), F32)],
        compiler_params=pltpu.CompilerParams(
            dimension_semantics=("arbitrary", "arbitrary"), vmem_limit_bytes=VMEM_LIMIT),
        name="retention",
    )(pd, lg, norm_g[None, :])


def _out_proj_kernel(ma_ref, mb_ref, mc_ref, md_ref, h_ref, *refs, with_router):
    mods, (g_ref, w_ref), rest = refs[:WIDE], refs[WIDE:WIDE + 2], refs[WIDE + 2:]
    if with_router:
        wr_ref, hn_ref, v_ref, lg_ref = rest
    else:
        hn_ref, v_ref = rest
    gw = 2 * LANES
    mix = functools.reduce(jnp.add, [
        _dot(m_ref[...].astype(BF16), w_ref[gw * i:gw * (i + 1), :])
        for i, m_ref in enumerate((ma_ref, mb_ref, mc_ref, md_ref))])
    for k, mod_ref in enumerate(mods):
        r = _sub_rows(k)
        hn = h_ref[r, :] + mod_ref[2:3, :] * mix[r]
        hn_ref[r, :] = hn
        v = _rms(hn) * g_ref[...] * (1.0 + mod_ref[4:5, :]) + mod_ref[3:4, :]
        v_ref[r, :] = v.astype(v_ref.dtype)
        if with_router:
            lg_ref[r, :] = jnp.dot(v, wr_ref[...], preferred_element_type=F32, precision=lax.Precision.HIGHEST)


def out_proj(mixes, h, mod, gain, w, layer, w_router=None):
    b, l, d = h.shape
    tm = WIDE * ROW_TILE
    n = b * l
    with_router = w_router is not None
    row = lambda j: (j, 0)
    const = lambda j: (0, 0)
    in_specs = [pl.BlockSpec((tm, 2 * LANES), row) for _ in mixes] + [pl.BlockSpec((tm, d), row)] + _sub_mod_specs(
        d, l // ROW_TILE) + [pl.BlockSpec((1, d), const), _layer_weight_spec(w, layer)]
    out_specs = [pl.BlockSpec((tm, d), row), pl.BlockSpec((tm, d), row)]
    out_shape = [jax.ShapeDtypeStruct((n, d), F32), jax.ShapeDtypeStruct((n, d), F32 if with_router else BF16)]
    args = [m.reshape(n, 2 * LANES) for m in mixes] + [h.reshape(n, d)] + [mod] * WIDE + [gain, w]
    if with_router:
        in_specs.append(pl.BlockSpec(w_router.shape, const))
        out_specs.append(pl.BlockSpec((tm, LANES), row))
        out_shape.append(jax.ShapeDtypeStruct((n, LANES), F32))
        args.append(w_router)
    outs = pl.pallas_call(
        functools.partial(_out_proj_kernel, with_router=with_router),
        grid=(n // tm,),
        in_specs=in_specs,
        out_specs=out_specs,
        out_shape=out_shape,
        compiler_params=pltpu.CompilerParams(dimension_semantics=("parallel",), vmem_limit_bytes=VMEM_LIMIT),
        name="out_proj",
    )(*args)
    return [o.reshape(b, l, -1) for o in outs]


def build_out_weight(w):
    hd = HEAD_DIM
    rows = lambda lo, hi: w[..., lo:hi, :]
    return jnp.concatenate([rows(0, hd), rows(2 * hd, 3 * hd), rows(hd, 2 * hd), rows(3 * hd, None)],
                           axis=-2).astype(BF16)


def _ffn_kernel(v_ref, h_ref, *refs):
    mods, (wg_ref, wu_ref, wd_ref, o_ref) = refs[:WIDE], refs[WIDE:]
    v = v_ref[...]
    acc = jnp.zeros(o_ref.shape, F32)
    for j in range(D_FF // FF_CHUNK):
        cols = slice(j * FF_CHUNK, (j + 1) * FF_CHUNK)
        a = _dot(v, wg_ref[:, cols])
        u = _dot(v, wu_ref[:, cols])
        mid = (_silu(a) * u).astype(BF16)
        acc = acc + _dot(mid, wd_ref[cols, :])
    for k, mod_ref in enumerate(mods):
        r = _sub_rows(k)
        o_ref[r, :] = h_ref[r, :] + mod_ref[5:6, :] * acc[r]


def dense_ffn(v, h, mod, layer, wg, wu, wd):
    b, l, d = h.shape
    tm = WIDE * ROW_TILE
    n = b * l
    row = lambda j: (j, 0)
    const = lambda j: (0, 0)
    return pl.pallas_call(
        _ffn_kernel,
        grid=(n // tm,),
        in_specs=[pl.BlockSpec((tm, d), row), pl.BlockSpec((tm, d), row)] + _sub_mod_specs(d, l // ROW_TILE) + [
            _layer_weight_spec(wg, layer), _layer_weight_spec(wu, layer), _layer_weight_spec(wd, layer)],
        out_specs=pl.BlockSpec((tm, d), row),
        out_shape=jax.ShapeDtypeStruct((n, d), F32),
        compiler_params=pltpu.CompilerParams(dimension_semantics=("parallel",), vmem_limit_bytes=VMEM_LIMIT),
        name="dense_ffn",
    )(v.reshape(n, d), h.reshape(n, d), *([mod] * WIDE), wg, wu, wd).reshape(b, l, d)


def _moe_kernel(wt_ref, we_ref, lo_ref, hi_ref, first_ref, x_ref, wg_ref, wu_ref, wd_ref, o_ref, xm_ref, acc_ref):
    w = pl.program_id(0)
    j = pl.program_id(1)
    nj = pl.num_programs(1)
    tm = x_ref.shape[0]

    @pl.when(j == 0)
    def _():
        row = wt_ref[w] * tm + lax.broadcasted_iota(jnp.int32, (tm, 1), 0)
        keep = (row >= lo_ref[w]) & (row < hi_ref[w])
        xm_ref[...] = jnp.where(keep, x_ref[...], 0.0).astype(BF16)

    @pl.when((j == 0) & (first_ref[w] > 0))
    def _():
        acc_ref[...] = jnp.zeros_like(acc_ref)

    def swiglu_rows(rows):
        x = xm_ref[rows, :]
        mid = (_silu(_dot(x, wg_ref[...])) * _dot(x, wu_ref[...])).astype(BF16)
        acc_ref[rows, :] += _dot(mid, wd_ref[...])

    tile_lo = wt_ref[w] * tm
    whole = (lo_ref[w] <= tile_lo) & (hi_ref[w] >= tile_lo + tm)

    @pl.when(whole)
    def _():
        swiglu_rows(slice(0, tm))

    for part in range(tm // MOE_PART):
        part_lo = tile_lo + part * MOE_PART

        @pl.when(jnp.logical_not(whole) & (hi_ref[w] > jnp.maximum(part_lo, lo_ref[w]))
                 & (lo_ref[w] < part_lo + MOE_PART))
        def _():
            swiglu_rows(slice(part * MOE_PART, (part + 1) * MOE_PART))

    @pl.when(j == nj - 1)
    def _():
        o_ref[...] = acc_ref[...]


def moe_grouped_ffn(xs, items, layer_idx, wg, wu, wd):
    s, d = xs.shape
    tm = MOE_TILE
    fc = MOE_FF_CHUNK
    nw = items[0].shape[0]
    nj = D_FF // fc
    grid_spec = pltpu.PrefetchScalarGridSpec(
        num_scalar_prefetch=5,
        grid=(nw, nj),
        in_specs=[
            pl.BlockSpec((tm, d), lambda w, j, wt, we, lo, hi, fi: (wt[w], 0)),
            pl.BlockSpec((None, None, d, fc), lambda w, j, wt, we, lo, hi, fi: (layer_idx, we[w], 0, j)),
            pl.BlockSpec((None, None, d, fc), lambda w, j, wt, we, lo, hi, fi: (layer_idx, we[w], 0, j)),
            pl.BlockSpec((None, None, fc, d), lambda w, j, wt, we, lo, hi, fi: (layer_idx, we[w], j, 0)),
        ],
        out_specs=pl.BlockSpec((tm, d), lambda w, j, wt, we, lo, hi, fi: (wt[w], 0)),
        scratch_shapes=[pltpu.VMEM((tm, d), BF16), pltpu.VMEM((tm, d), F32)],
    )
    return pl.pallas_call(
        _moe_kernel,
        grid_spec=grid_spec,
        out_shape=jax.ShapeDtypeStruct((s, d), F32),
        compiler_params=pltpu.CompilerParams(
            dimension_semantics=("arbitrary", "arbitrary"), vmem_limit_bytes=VMEM_LIMIT),
        name="moe_ffn",
    )(*items, xs, wg, wu, wd)


def _residual_kernel(h_ref, f0_ref, f1_ref, gate_ref, mod_ref, *rest):
    f = gate_ref[:, 0:1] * f0_ref[...] + gate_ref[:, 1:2] * f1_ref[...]
    hn = h_ref[...] + mod_ref[5:6, :] * f
    if len(rest) == 2:
        gain_ref, o_ref = rest
        o_ref[...] = _rms(hn) * gain_ref[...]
    else:
        rest[0][...] = hn


def gated_residual(h, f0, f1, gates, mod, final_gain=None):
    b, l, d = h.shape
    tm = ROW_TILE
    skip = 0 if final_gain is None else CTX_LEN // tm
    row = lambda i, t: (i, t + skip, 0)
    in_specs = [pl.BlockSpec((None, tm, d), row), pl.BlockSpec((None, tm, d), row), pl.BlockSpec((None, tm, d), row),
                pl.BlockSpec((None, tm, LANES), row),
                pl.BlockSpec((None, None, SUBLANES, d), lambda i, t: (i, jnp.minimum(t + skip, 1), 0, 0))]
    args = [h, f0, f1, gates, mod]
    if final_gain is not None:
        in_specs.append(pl.BlockSpec((1, d), lambda i, t: (0, 0)))
        args.append(final_gain)
    return pl.pallas_call(
        _residual_kernel,
        grid=(b, l // tm - skip),
        in_specs=in_specs,
        out_specs=pl.BlockSpec((None, tm, d), lambda i, t: (i, t, 0)),
        out_shape=jax.ShapeDtypeStruct((b, l - skip * tm, d), F32),
        compiler_params=pltpu.CompilerParams(dimension_semantics=("parallel", "parallel")),
        name="gated_residual",
    )(*args)


def moe_ffn(v, logits, h, mod, layer_idx, wg, wu, wd, final_gain=None):
    b, l, d = h.shape
    t = b * l
    s = TOP_K * t
    tm = MOE_TILE
    nt = s // tm
    nw = nt + N_EXPERTS - 1
    i32 = jnp.int32
    lg = logits.reshape(t, LANES)[:, :N_EXPERTS]
    top_val, top_idx = lax.top_k(lg, TOP_K)
    gates = jax.nn.softmax(top_val, axis=-1)
    slot = jnp.arange(s, dtype=i32)
    skey = jnp.sort(top_idx.reshape(-1).astype(i32) * s + slot)
    order = skey % s
    _, inv = lax.sort_key_val(order, slot)
    bounds = (jnp.arange(N_EXPERTS, dtype=i32) + 1) * s
    cum = jnp.sum((skey[None, :] < bounds[:, None]).astype(i32), axis=1)
    cum_prev = jnp.concatenate([jnp.zeros((1,), i32), cum[:-1]])
    tile_lo = jnp.arange(nt, dtype=i32) * tm
    count_le = lambda edges, x: jnp.sum((edges[None, :] <= x[:, None]).astype(i32), axis=1)
    e_first = count_le(cum, tile_lo)
    e_last = count_le(cum, tile_lo + tm - 1)
    n_items = e_last - e_first + 1
    item_end = jnp.cumsum(n_items)
    item_start = item_end - n_items
    w = jnp.arange(nw, dtype=i32)
    wt = jnp.minimum(count_le(item_end, w), nt - 1)
    valid = w < item_end[-1]
    we = jnp.clip(e_first[wt] + w - item_start[wt], 0, N_EXPERTS - 1).astype(i32)
    lo = jnp.where(valid, cum_prev[we], 0).astype(i32)
    hi = jnp.where(valid, cum[we], 0).astype(i32)
    first = (valid & (w == item_start[wt])).astype(i32)
    rows_of = lambda a, idx: a.at[idx].get(mode="promise_in_bounds")
    xs = rows_of(v.reshape(t, d), order // TOP_K)
    ys = moe_grouped_ffn(xs, (wt, we, lo, hi, first), layer_idx, wg, wu, wd)
    dest = inv.reshape(t, TOP_K)
    f0 = rows_of(ys, dest[:, 0]).reshape(b, l, d)
    f1 = rows_of(ys, dest[:, 1]).reshape(b, l, d)
    gates_p = jnp.pad(gates, ((0, 0), (0, LANES - TOP_K))).reshape(b, l, LANES)
    return gated_residual(h, f0, f1, gates_p, mod, final_gain)


def _final_norm_kernel(h_ref, g_ref, o_ref):
    o_ref[...] = _rms(h_ref[...]) * g_ref[...]


def final_rms_norm(h, gain, n_ctx_tiles):
    b, l, d = h.shape
    tm = ROW_TILE
    n = l - n_ctx_tiles * tm
    return pl.pallas_call(
        _final_norm_kernel,
        grid=(b, n // tm),
        in_specs=[
            pl.BlockSpec((None, tm, d), lambda i, t: (i, t + n_ctx_tiles, 0)),
            pl.BlockSpec((1, d), lambda i, t: (0, 0)),
        ],
        out_specs=pl.BlockSpec((None, tm, d), lambda i, t: (i, t, 0)),
        out_shape=jax.ShapeDtypeStruct((b, n, d), F32),
        compiler_params=pltpu.CompilerParams(dimension_semantics=("parallel", "parallel")),
        name="final_norm",
    )(h, gain)


GDN_W = GDN_HEADS * GDN_DK
GDN_CONV_K = 5
GDN_HALO = SUBLANES


def _split3(x):
    p0 = x.astype(BF16)
    r1 = x - p0.astype(F32)
    p1 = r1.astype(BF16)
    p2 = (r1 - p1.astype(F32)).astype(BF16)
    return p0, p1, p2


def _gdn_prep_kernel(x_ref, prev_ref, next_ref, cw_ref, par_ref, q_ref, k_ref, v_ref, gb_ref):
    t = pl.program_id(1)
    last = pl.num_programs(1) - 1
    tm = x_ref.shape[0]
    w3 = 3 * GDN_W
    has_prev = t > 1
    has_next = (t > 0) & (t < last)
    prev = jnp.where(has_prev, prev_ref[...], 0.0)
    nxt = jnp.where(has_next, next_ref[...], 0.0)
    xe = jnp.concatenate([prev, x_ref[:, :w3], nxt], axis=0)
    y = jnp.zeros((tm, w3), F32)
    for j in range(GDN_CONV_K):
        lo = GDN_HALO - GDN_CONV_K // 2 + j
        y = y + cw_ref[j:j + 1, :] * xe[lo:lo + tm, :]
    y = _silu(y)
    r = lax.broadcasted_iota(jnp.int32, (GDN_W, GDN_W), 0)
    c = lax.broadcasted_iota(jnp.int32, (GDN_W, GDN_W), 1)
    ones_bd = jnp.where(r // GDN_DK == c // GDN_DK, 1.0, 0.0).astype(BF16)

    def l2n(x):
        sq = x * x
        hi = sq.astype(BF16)
        lo = (sq - hi.astype(F32)).astype(BF16)
        return x * lax.rsqrt(_dot(hi, ones_bd) + _dot(lo, ones_bd) + EPS)

    q_ref[...] = l2n(y[:, :GDN_W]) * GDN_DK ** -0.5
    k_ref[...] = l2n(y[:, GDN_W:2 * GDN_W])
    v_ref[...] = y[:, 2 * GDN_W:]
    ab = x_ref[:, w3 + GDN_W:]
    lane = lax.broadcasted_iota(jnp.int32, ab.shape, 1)
    is_g = (lane % 8) < 4
    z = ab + par_ref[1:2, :]
    softplus = jnp.maximum(z, 0.0) + jnp.log1p(jnp.exp(-jnp.abs(z)))
    g = jnp.where(is_g, par_ref[0:1, :] * softplus, 0.0)
    beta = 1.0 / (1.0 + jnp.exp(-ab))
    i = lax.broadcasted_iota(jnp.int32, (tm, tm), 0)
    j = lax.broadcasted_iota(jnp.int32, (tm, tm), 1)
    same_chunk = i // GDN_CHUNK == j // GDN_CHUNK
    tri_f = jnp.where(same_chunk & (j <= i), 1.0, 0.0).astype(BF16)
    tri_b = jnp.where(same_chunk & (j >= i), 1.0, 0.0).astype(BF16)
    pieces = _split3(g)
    gc_f = functools.reduce(jnp.add, [_dot(tri_f, p) for p in pieces])
    gc_b = functools.reduce(jnp.add, [_dot(tri_b, p) for p in pieces])
    gb_ref[...] = jnp.where(is_g, jnp.where(lane < 8, gc_f, gc_b), beta)


def gdn_prep(pb, conv_w, a_log, dt_bias):
    b, l, _ = pb.shape
    tm = ROW_TILE
    w3 = 3 * GDN_W
    halo_blocks = tm // GDN_HALO
    n_halo = l // GDN_HALO
    cw = jnp.pad(conv_w, ((0, SUBLANES - GDN_CONV_K), (0, 0)))
    neg_a = jnp.pad(-jnp.exp(a_log.astype(F32)), ((0, 0), (0, 4))).reshape(-1)
    dtb = jnp.pad(dt_bias.astype(F32), ((0, 0), (0, 4))).reshape(-1)
    par = jnp.pad(jnp.stack([neg_a, dtb]), ((0, SUBLANES - 2), (0, LANES - 16)))
    row = lambda i, t: (i, t, 0)
    out = lambda w: pl.BlockSpec((None, tm, w), row)
    return pl.pallas_call(
        _gdn_prep_kernel,
        grid=(b, l // tm),
        in_specs=[pl.BlockSpec((None, tm, B_W), row),
                  pl.BlockSpec((None, GDN_HALO, w3), lambda i, t: (i, jnp.maximum(t * halo_blocks - 1, 0), 0)),
                  pl.BlockSpec((None, GDN_HALO, w3),
                               lambda i, t: (i, jnp.minimum((t + 1) * halo_blocks, n_halo - 1), 0)),
                  pl.BlockSpec(cw.shape, lambda i, t: (0, 0)),
                  pl.BlockSpec(par.shape, lambda i, t: (0, 0))],
        out_specs=[out(GDN_W), out(GDN_W), out(GDN_W), out(LANES)],
        out_shape=[jax.ShapeDtypeStruct((b, l, GDN_W), F32)] * 3 + [jax.ShapeDtypeStruct((b, l, LANES), F32)],
        compiler_params=pltpu.CompilerParams(
            dimension_semantics=("parallel", "parallel"), vmem_limit_bytes=VMEM_LIMIT),
        name="gdn_prep",
    )(pb, pb, pb, cw, par)


def _tile_heads(x):
    return jnp.concatenate([x] * GDN_HEADS, axis=0)


def _collapse_heads(x):
    c = GDN_CHUNK
    return x[0:c] + x[c:2 * c] + x[2 * c:3 * c] + x[3 * c:4 * c]


def _gdn_chunk_kernel(q_ref, k_ref, v_ref, gb_ref, o0_ref, qe_ref, a_ref, bm_ref, gam_ref):
    n = GDN_W
    cs = GDN_CHUNK
    r = lax.broadcasted_iota(jnp.int32, (n, n), 0)
    c = lax.broadcasted_iota(jnp.int32, (n, n), 1)
    ri, ci = r % cs, c % cs
    head = r // cs == c // cs
    eye = jnp.where(r == c, 1.0, 0.0)
    blk = lambda s: r // s == c // s
    b8, b16, b32 = blk(8), blk(16), blk(32)
    lane = lax.broadcasted_iota(jnp.int32, (n, LANES), 1)
    row_head = lax.broadcasted_iota(jnp.int32, (n, LANES), 0) // cs
    pick = lambda sel, x: jnp.sum(jnp.where(sel, x, 0.0), axis=1, keepdims=True)
    src_lane = lax.broadcasted_iota(jnp.int32, (LANES, n), 0)
    dst_head = lax.broadcasted_iota(jnp.int32, (LANES, n), 1) // cs
    tri, tri_strict, sel_g, sel_b, widen_g, widen_b = {}, {}, {}, {}, {}, {}
    for fwd in (True, False):
        ahead = ri - ci if fwd else ci - ri
        tri[fwd] = head & (ahead >= 0)
        tri_strict[fwd] = head & (ahead > 0)
        lane0 = 0 if fwd else 8
        sel_g[fwd] = lane == lane0 + row_head
        sel_b[fwd] = lane == lane0 + 4 + row_head
        widen_g[fwd] = jnp.where(src_lane == lane0 + dst_head, 1.0, 0.0).astype(BF16)
        widen_b[fwd] = jnp.where(src_lane == lane0 + 4 + dst_head, 1.0, 0.0).astype(BF16)

    n_chunks = q_ref.shape[0] // cs
    rows = [slice(ch * cs, (ch + 1) * cs) for ch in range(n_chunks)]
    items = [(fwd, ch) for fwd in (True, False) for ch in range(n_chunks)]
    dirs = [fwd for fwd, _ in items]
    per_item = lambda xs: [xs[ch] for _, ch in items]
    each = lambda f, *xs: [f(*a) for a in zip(*xs)]
    bf = lambda xs: [x.astype(BF16) for x in xs]
    widen = lambda x, e: functools.reduce(jnp.add, [_dot(p, e) for p in _split3(x)])
    zero_bf = jnp.zeros((n, n), BF16)
    spread = lambda xs: [jnp.where(head, _tile_heads(x.astype(BF16)), zero_bf) for x in xs]
    k_c, q_c, v_c = ([ref[rw, :] for rw in rows] for ref in (k_ref, q_ref, v_ref))
    khb_c, qhb_c = spread(k_c), spread(q_c)
    kk = per_item(each(_dot_nt, khb_c, khb_c))
    qk = per_item(each(_dot_nt, qhb_c, khb_c))
    k_t, q_t, v_t = per_item(k_c), per_item(q_c), per_item(v_c)
    gb = per_item([gb_ref[rw, :] for rw in rows])
    gb4 = [_tile_heads(x) for x in gb]
    gc = [pick(sel_g[f], x) for f, x in zip(dirs, gb4)]
    beta = [pick(sel_b[f], x) for f, x in zip(dirs, gb4)]
    gc_t = [widen(x, widen_g[f]) for f, x in zip(dirs, gb)]
    beta_t = [widen(x, widen_b[f]) for f, x in zip(dirs, gb)]
    ends = [cs - 1 if f else 0 for f in dirs]
    gl_t = [jnp.broadcast_to(x[e:e + 1, :], x.shape) for x, e in zip(gc_t, ends)]
    gc_b = [jnp.broadcast_to(x, (n, n)) for x in gc]
    decay = [jnp.exp(jnp.minimum(x - x.T, 0.0)) for x in gc_b]
    lmat = [jnp.where(tri_strict[f], b_ * kk_ * d_, 0.0) for f, b_, kk_, d_ in zip(dirs, beta, kk, decay)]
    attn = bf([jnp.where(tri[f], qk_ * d_, 0.0) for f, qk_, d_ in zip(dirs, qk, decay)])
    nl = bf([jnp.where(b8, -x, 0.0) for x in lmat])
    n2 = bf(each(_dot, nl, nl))
    n4 = each(_dot, n2, n2)
    p1 = bf(each(lambda a, b_: _dot((eye + a).astype(BF16), (eye + b_).astype(BF16)), nl, n2))
    tinv = each(lambda p, x: _dot(p, (eye + x).astype(BF16)), p1, n4)

    def moving_rows(fwd, x, sz):
        return jnp.concatenate([x[i:i + sz] for i in range(sz if fwd else 0, n, 2 * sz)], axis=0)

    def with_moving_rows(fwd, x, new, sz):
        pieces = []
        for j, i in enumerate(range(0, n, 2 * sz)):
            kept = x[i:i + sz] if fwd else x[i + sz:i + 2 * sz]
            moved = new[j * sz:(j + 1) * sz]
            pieces += [kept, moved] if fwd else [moved, kept]
        return jnp.concatenate(pieces, axis=0)

    for sz, inner, outer in ((8, b8, b16), (16, b16, b32), (32, b32, head)):
        off = bf([jnp.where(outer & ~inner, x, 0.0) for x in lmat])
        tb = bf(tinv)
        t_mv = [moving_rows(f, x, sz) for f, x in zip(dirs, tinv)]
        to = bf(each(_dot, bf(t_mv), off))
        tinv = [with_moving_rows(f, t_, tm_ - _dot(to_, tb_), sz)
                for f, t_, tm_, to_, tb_ in zip(dirs, tinv, t_mv, to, tb)]
    tb = bf(tinv)
    eg_t = [jnp.exp(x) for x in gc_t]
    u = bf(each(_dot, tb, spread(each(lambda b_, v_: b_ * v_, beta_t, v_t))))
    w = bf(each(_dot, tb, spread(each(lambda b_, e_, k_: (b_ * e_) * k_, beta_t, eg_t, k_t))))
    o0 = each(_dot, attn, u)
    aw = each(_dot, attn, w)
    kg = spread(each(lambda k_, gl_, gc_: k_ * jnp.exp(gl_ - gc_), k_t, gl_t, gc_t))
    a_mat = each(_dot_tn, kg, w)
    b_mat = each(_dot_tn, kg, u)
    for (fwd, ch), o0_, aw_, a_, b_, q_, e_, gl_ in zip(items, o0, aw, a_mat, b_mat, q_t, eg_t, gl_t):
        dr, rw = 0 if fwd else 1, rows[ch]
        o0_ref[dr, rw, :] = _collapse_heads(o0_)
        qe_ref[dr, rw, :] = q_ * e_ - _collapse_heads(aw_)
        a_ref[dr, rw, :] = _collapse_heads(a_)
        bm_ref[dr, rw, :] = _collapse_heads(b_)
        gam_ref[dr, rw, :] = jnp.exp(gl_)


def gdn_chunks(q, k, v, gb):
    b, l, _ = q.shape
    tm = ROW_TILE
    row = lambda i, t: (i, t, 0)
    out = pl.BlockSpec((2, None, tm, GDN_W), lambda i, t: (0, i, t, 0))
    return pl.pallas_call(
        _gdn_chunk_kernel,
        grid=(b, l // tm),
        in_specs=[pl.BlockSpec((None, tm, GDN_W), row)] * 3 + [pl.BlockSpec((None, tm, LANES), row)],
        out_specs=[out] * 5,
        out_shape=[jax.ShapeDtypeStruct((2, b, l, GDN_W), F32)] * 5,
        compiler_params=pltpu.CompilerParams(
            dimension_semantics=("parallel", "parallel"), vmem_limit_bytes=VMEM_LIMIT),
        name="gdn_chunk",
    )(q, k, v, gb)


def _gdn_scan_kernel(o0_ref, qe_ref, a_ref, bm_ref, gam_ref, gate_ref, g_ref, o_ref, s_ref, part_ref):
    dr = pl.program_id(0)
    s = pl.program_id(1)
    ns = pl.num_programs(1)
    nb, tm = o0_ref.shape[0], o0_ref.shape[1]
    cs = GDN_CHUNK
    n = GDN_W
    nch = tm // cs
    bs = range(nb)
    tile = jnp.where(s == 0, 0, jnp.where(dr == 0, s, ns - s))
    r = lax.broadcasted_iota(jnp.int32, (n, n), 0)
    c = lax.broadcasted_iota(jnp.int32, (n, n), 1)
    head = r // cs == c // cs

    @pl.when(s == 0)
    def _():
        s_ref[...] = jnp.zeros_like(s_ref)

    def run(order):
        state = [s_ref[i] for i in bs]
        outs = [{} for _ in bs]
        for ch in order:
            rows = slice(ch * cs, (ch + 1) * cs)
            sb = [x.astype(BF16) for x in state]
            for i in bs:
                outs[i][ch] = o0_ref[i, rows, :] + _dot(qe_ref[i, rows, :].astype(BF16), sb[i])
            a_full = [jnp.where(head, _tile_heads(a_ref[i, rows, :]), 0.0).astype(BF16) for i in bs]
            state = [_tile_heads(gam_ref[i, rows, :]) * state[i] - _dot(a_full[i], sb[i])
                     + jnp.where(head, _tile_heads(bm_ref[i, rows, :]), 0.0) for i in bs]
        for i in bs:
            s_ref[i] = state[i]
        return [jnp.concatenate([outs[i][ch] for ch in range(nch)], axis=0) for i in bs]

    rows_out = pl.ds(pl.multiple_of(tile * tm, tm), tm)

    @pl.when(dr == 0)
    def _():
        for i, o in enumerate(run(range(nch))):
            part_ref[i, rows_out, :] = o

    @pl.when(dr == 1)
    def _():
        ones_bd = jnp.where(head, 1.0, 0.0).astype(BF16)
        for i, o_bwd in enumerate(run(range(nch - 1, -1, -1))):
            o = part_ref[i, rows_out, :] + o_bwd
            ms = _head_mean(o * o, ones_bd)
            o_ref[i] = (o * lax.rsqrt(ms + EPS) * g_ref[...] * _silu(gate_ref[i])).astype(o_ref.dtype)


def gdn_scan(o0, qe, a, bm, gam, pb, norm_g):
    _, b, l, _ = o0.shape
    tm = ROW_TILE
    ns = l // tm

    def tile_of(dr, s):
        return jnp.where(s == 0, 0, jnp.where(dr == 0, s, ns - s))

    per_dir = pl.BlockSpec((None, b, tm, GDN_W), lambda dr, s: (dr, 0, tile_of(dr, s), 0))
    gate_col = 3 * GDN_W // GDN_W
    return pl.pallas_call(
        _gdn_scan_kernel,
        grid=(2, ns),
        in_specs=[per_dir] * 5 + [
            pl.BlockSpec((b, tm, GDN_W), lambda dr, s: (0, tile_of(dr, s), gate_col)),
            pl.BlockSpec((1, GDN_W), lambda dr, s: (0, 0))],
        out_specs=pl.BlockSpec((b, tm, GDN_W), lambda dr, s: (0, jnp.where(dr == 0, 0, tile_of(dr, s)), 0)),
        out_shape=jax.ShapeDtypeStruct((b, l, GDN_W), BF16),
        scratch_shapes=[pltpu.VMEM((b, GDN_W, GDN_W), F32), pltpu.VMEM((b, l, GDN_W), F32)],
        compiler_params=pltpu.CompilerParams(
            dimension_semantics=("arbitrary", "arbitrary"), vmem_limit_bytes=VMEM_LIMIT),
        name="gdn_scan",
    )(o0, qe, a, bm, gam, pb, jnp.tile(norm_g, GDN_HEADS)[None, :])


def gdn_mixer(pb, conv_w, a_log, dt_bias, norm_g):
    q, k, v, gb = gdn_prep(pb, conv_w, a_log, dt_bias)
    o0, qe, a, bm, gam = gdn_chunks(q, k, v, gb)
    return gdn_scan(o0, qe, a, bm, gam, pb, norm_g)


def kernel(x, c, ctx, c_ctx, w_mod, b_mod, norm1, norm2, w_in, w_out, swa_sink, gdn_conv, gdn_a_log, gdn_dt_bias, gdn_norm, mla_q_norm, mla_kv_norm, mla_w_q_up, mla_w_kv_up, ret_log_decay, ret_norm, ffn_w_gate, ffn_w_up, ffn_w_down, moe_router, moe_w_gate, moe_w_up, moe_w_down, final_norm):
    b, n, d = x.shape
    depth = w_in.shape[0]
    cos_t, sin_t = rope_tables(n)
    h = jnp.concatenate([ctx, x], axis=1)
    cond = jnp.concatenate([jax.nn.silu(c_ctx)[None, :], jax.nn.silu(c)], axis=0)
    mods = jnp.einsum("bd,ldk->lbk", cond, w_mod, precision=lax.Precision.HIGHEST) + b_mod[:, None, :]
    mods = mods.reshape(depth, 1 + b, 6, d)
    mods = jnp.stack([jnp.broadcast_to(mods[:, :1], (depth, b, 6, d)), mods[:, 1:]], axis=2)
    mods = jnp.pad(mods, ((0, 0), (0, 0), (0, 0), (0, SUBLANES - 6), (0, 0)))
    w_in_all = build_in_weight(w_in)
    w_out_all = build_out_weight(w_out)
    ffn_wg, ffn_wu, ffn_wd = ffn_w_gate.astype(BF16), ffn_w_up.astype(BF16), ffn_w_down.astype(BF16)
    moe_wg, moe_wu, moe_wd = moe_w_gate.astype(BF16), moe_w_up.astype(BF16), moe_w_down.astype(BF16)
    for layer in range(depth):
        mod = mods[layer]
        mla_w = build_mla_weights(mla_q_norm[layer], mla_kv_norm[layer], mla_w_q_up[layer], mla_w_kv_up[layer])
        pa, pb, (mq, mk, mv), pd = norm_proj(h, mod, norm1[layer][None, :], w_in_all, layer, cos_t, sin_t, mla_w)
        mix_a = swa_mixer(pa, swa_sink[layer])
        mix_b = gdn_mixer(pb, gdn_conv[layer], gdn_a_log[layer], gdn_dt_bias[layer], gdn_norm[layer])
        mix_c = mla_attention(mq, mk, mv)
        mix_d = retention_mixer(pd, ret_log_decay[layer], ret_norm[layer])
        mixes = (mix_a, mix_b, mix_c, mix_d)
        i = layer // 2
        if layer % 2 == 0:
            h, v = out_proj(mixes, h, mod, norm2[layer][None, :], w_out_all, layer)
            h = dense_ffn(v, h, mod, i, ffn_wg, ffn_wu, ffn_wd)
        else:
            w_r = jnp.pad(moe_router[i], ((0, 0), (0, LANES - N_EXPERTS)))
            h, v, logits = out_proj(mixes, h, mod, norm2[layer][None, :], w_out_all, layer, w_r)
            if layer == depth - 1:
                return moe_ffn(v, logits, h, mod, i, moe_wg, moe_wu, moe_wd, final_norm[None, :])
            h = moe_ffn(v, logits, h, mod, i, moe_wg, moe_wu, moe_wd)
    return final_rms_norm(h, final_norm[None, :], CTX_LEN // ROW_TILE)
```

```python
import functools

import numpy as np
import jax
import jax.numpy as jnp
from jax import lax
from jax.experimental import pallas as pl
from jax.experimental.pallas import tpu as pltpu

D_MODEL = 1024
GRID_W = 64
CTX_LEN = 256
HEAD_DIM = 64
ROPE_THETA = 10000.0
EPS = 1e-6
NEG_INF = -1e30

SWA_WINDOW = 128
GDN_HEADS = 4
GDN_DK = 64
GDN_DV = 64
GDN_CHUNK = 64
MLA_HEADS = 4
MLA_Q_RANK = 256
MLA_NOPE = 64
MLA_ROPE = 32
MLA_V = 64
RET_HEADS = 4
RET_DK = 64
D_FF = 3584
N_EXPERTS = 8
TOP_K = 2

LANES = 128
SUBLANES = 8
VMEM_LIMIT = 56 * 1024 * 1024

ROW_TILE = 256
FF_CHUNK = 512
MOE_TILE = 512
MOE_PART = 256
MOE_FF_CHUNK = 1792
A_W, B_W, C_W, D_W = 768, 1152, 512, 1024
A_ROT_W, C_ROT_W, D_ROT_W = 640, 128, 512
OFF_A = 0
OFF_B = OFF_A + A_W
OFF_C = OFF_B + B_W
OFF_D = OFF_C + C_W
OFF_AR = OFF_D + D_W
OFF_CR = OFF_AR + A_ROT_W
OFF_DR = OFF_CR + C_ROT_W
W_ALL = OFF_DR + D_ROT_W
ROPE_W = A_ROT_W + C_ROT_W + D_ROT_W

LOG2_E = float(np.log2(np.e))
F32 = jnp.float32
BF16 = jnp.bfloat16
NT_DIMS = (((1,), (1,)), ((), ()))
TN_DIMS = (((0,), (0,)), ((), ()))


def _rms(x):
    return x * lax.rsqrt(jnp.mean(x * x, axis=-1, keepdims=True) + EPS)


def _silu(x):
    return x * (1.0 / (1.0 + jnp.exp(-x)))


def _dot(a, b):
    return jnp.dot(a, b, preferred_element_type=F32)


def _dot_nt(a, b):
    return lax.dot_general(a, b, NT_DIMS, preferred_element_type=F32)


def _dot_tn(a, b):
    return lax.dot_general(a, b, TN_DIMS, preferred_element_type=F32)


WIDE = 2


def _sub_tile_specs(block, tiles_per_seq, index_of):
    def spec(k):
        return pl.BlockSpec(block, lambda j: index_of((WIDE * j + k) // tiles_per_seq, (WIDE * j + k) % tiles_per_seq))
    return [spec(k) for k in range(WIDE)]


def _sub_mod_specs(d, tiles_per_seq):
    return _sub_tile_specs((None, None, SUBLANES, d), tiles_per_seq, lambda b, t: (b, jnp.minimum(t, 1), 0, 0))


def _sub_rows(k):
    return slice(k * ROW_TILE, (k + 1) * ROW_TILE)


def _norm_proj_kernel(h_ref, *refs):
    mods, (g_ref, w_ref), tabs = refs[:WIDE], refs[WIDE:WIDE + 2], refs[WIDE + 2:3 * WIDE + 2]
    qn_ref, kvn_ref, wq_ref, wqr_ref, wk_ref, wv_ref = refs[3 * WIDE + 2:3 * WIDE + 8]
    a_ref, b_ref, mq_ref, mk_ref, mv_ref, d_ref = refs[3 * WIDE + 8:]
    y = _rms(h_ref[...]) * g_ref[...]
    u = jnp.concatenate([y[_sub_rows(k)] * (1.0 + m[1:2, :]) + m[0:1, :] for k, m in enumerate(mods)],
                        axis=0).astype(BF16)

    def mm(lo, width):
        return _dot(u, w_ref[:, lo:lo + width])

    a_main = mm(OFF_A, A_W)
    a_rot = mm(OFF_AR, A_ROT_W)
    b_ref[...] = mm(OFF_B, B_W)
    c_main = mm(OFF_C, C_W)
    c_rot = mm(OFF_CR, C_ROT_W)
    d_main = mm(OFF_D, D_W)
    d_rot = mm(OFF_DR, D_ROT_W)
    lo, hi = A_ROT_W, A_ROT_W + C_ROT_W
    a_ref[:, A_ROT_W:] = a_main[:, A_ROT_W:].astype(BF16)
    d_ref[:, D_ROT_W:] = d_main[:, D_ROT_W:]
    nq = (_rms(c_main[:, :MLA_Q_RANK]) * qn_ref[...]).astype(BF16)
    nkv = (_rms(c_main[:, MLA_Q_RANK:C_W - C_ROT_W]) * kvn_ref[...]).astype(BF16)
    q_main, q_rot = _dot(nq, wq_ref[...]), _dot(nq, wqr_ref[...])
    k_nope = _dot(nkv, wk_ref[...])
    mv_ref[...] = _dot(nkv, wv_ref[...]).astype(BF16)
    per_head = lambda x: jnp.concatenate([x] * MLA_HEADS, axis=1)
    for k in range(WIDE):
        cos_ref, sin_ref = tabs[2 * k], tabs[2 * k + 1]
        r = _sub_rows(k)
        a_ref[r, :A_ROT_W] = (a_main[r, :A_ROT_W] * cos_ref[:, :A_ROT_W] + a_rot[r] * sin_ref[:, :A_ROT_W]).astype(BF16)
        cos_c, sin_c = cos_ref[:, lo:hi], sin_ref[:, lo:hi]
        k_rope = c_main[r, C_W - C_ROT_W:] * cos_c + c_rot[r] * sin_c
        mq_ref[r, :] = (q_main[r] * per_head(cos_c) + q_rot[r] * per_head(sin_c)).astype(BF16)
        mk_ref[r, :] = (k_nope[r] + per_head(k_rope)).astype(BF16)
        d_ref[r, :D_ROT_W] = d_main[r, :D_ROT_W] * cos_ref[:, hi:] + d_rot[r] * sin_ref[:, hi:]


def _layer_weight_spec(w, layer):
    return pl.BlockSpec((None,) + w.shape[1:], lambda j: (layer, 0, 0), pipeline_mode=pl.Buffered(1))


def norm_proj(h, mod, gain, w, layer, cos_t, sin_t, mla_weights):
    b, l, d = h.shape
    tm = WIDE * ROW_TILE
    tps = l // ROW_TILE
    row = lambda j: (j, 0)
    const = lambda j: (0, 0)
    tabs = _sub_tile_specs((ROW_TILE, ROPE_W), tps, lambda bi, t: (t, 0))
    tab_specs = [s for pair in zip(tabs, _sub_tile_specs((ROW_TILE, ROPE_W), tps, lambda bi, t: (t, 0))) for s in pair]
    outs = pl.pallas_call(
        _norm_proj_kernel,
        grid=(b * l // tm,),
        in_specs=[pl.BlockSpec((tm, d), row)] + _sub_mod_specs(d, tps) + [
            pl.BlockSpec((1, d), const),
            _layer_weight_spec(w, layer)] + tab_specs + [pl.BlockSpec(a.shape, const) for a in mla_weights],
        out_specs=[pl.BlockSpec((tm, A_W), row), pl.BlockSpec((tm, B_W), row), pl.BlockSpec((tm, 4 * LANES), row),
                   pl.BlockSpec((tm, 4 * LANES), row), pl.BlockSpec((tm, 2 * LANES), row), pl.BlockSpec((tm, D_W), row)],
        out_shape=[jax.ShapeDtypeStruct((b * l, A_W), BF16), jax.ShapeDtypeStruct((b * l, B_W), F32),
                   jax.ShapeDtypeStruct((b * l, 4 * LANES), BF16), jax.ShapeDtypeStruct((b * l, 4 * LANES), BF16),
                   jax.ShapeDtypeStruct((b * l, 2 * LANES), BF16), jax.ShapeDtypeStruct((b * l, D_W), F32)],
        compiler_params=pltpu.CompilerParams(dimension_semantics=("parallel",), vmem_limit_bytes=VMEM_LIMIT),
        name="norm_proj",
    )(h.reshape(b * l, d), *([mod] * WIDE), gain, w, *([cos_t, sin_t] * WIDE), *mla_weights)
    pa, pb, mq, mk, mv, pd = [o.reshape(b, l, -1) for o in outs]
    return pa, pb, (mq, mk, mv), pd


def _rot_cols(w, hd):
    x = w.reshape(w.shape[:-1] + (w.shape[-1] // hd, 4, hd // 4))
    x1, x2, x3, x4 = x[..., 0, :], x[..., 1, :], x[..., 2, :], x[..., 3, :]
    return jnp.stack([-x2, x1, -x4, x3], axis=-2).reshape(w.shape)


def _place_swa_q(q):
    z = jnp.zeros(q.shape[:-1] + (HEAD_DIM,), q.dtype)
    blocks = []
    for h in range(4):
        qh = q[..., HEAD_DIM * h:HEAD_DIM * (h + 1)]
        blocks += [qh, z] if h // 2 == 0 else [z, qh]
    return jnp.concatenate(blocks, axis=-1)


def build_in_weight(w):
    o = [int(v) for v in np.cumsum((256, 128, 128, 768, 256, 16, 256, 128, 32, 256, 256, 256, 256))]
    aq, ak, av = w[..., :o[0]] * (HEAD_DIM ** -0.5 * LOG2_E), w[..., o[0]:o[1]], w[..., o[1]:o[2]]
    b_main, b_ab = w[..., o[2]:o[4]], w[..., o[4]:o[5]]
    c_q, c_kv, c_kr = w[..., o[5]:o[6]], w[..., o[6]:o[7]], w[..., o[7]:o[8]]
    dq, dk, dvg = w[..., o[8]:o[9]], w[..., o[9]:o[10]] * RET_DK ** -0.5, w[..., o[10]:]
    z = lambda n: jnp.zeros(w.shape[:-1] + (n,), w.dtype)
    parts = [
        _place_swa_q(aq), ak, av,
        b_main, b_ab, z(LANES - b_ab.shape[-1]),
        c_q, c_kv, z(64), c_kr, z(32),
        dq, dk, dvg,
        _place_swa_q(_rot_cols(aq, HEAD_DIM)), _rot_cols(ak, HEAD_DIM),
        z(64), _rot_cols(c_kr, MLA_ROPE), z(32),
        _rot_cols(dq, HEAD_DIM), _rot_cols(dk, HEAD_DIM),
    ]
    out = jnp.concatenate(parts, axis=-1)
    assert out.shape[-1] == W_ALL
    return out.astype(BF16)


def rope_tables(n):
    lat = jnp.arange(CTX_LEN + n, dtype=jnp.int32) - CTX_LEN
    grid_row = jnp.where(lat >= 0, lat // GRID_W, 0).astype(F32)[:, None]
    grid_col = jnp.where(lat >= 0, lat % GRID_W, 0).astype(F32)[:, None]
    narrow, col_of = [], {}
    for rot_dim in (HEAD_DIM, MLA_ROPE):
        n_freq = rot_dim // 4
        inv_freq = ROPE_THETA ** (-jnp.arange(n_freq, dtype=F32) / n_freq)
        for axis, pos in enumerate((grid_row, grid_col)):
            col_of[rot_dim, axis] = sum(a.shape[1] for a in narrow)
            narrow.append(pos * inv_freq)
    ang = jnp.concatenate(narrow, axis=1)
    identity_col = ang.shape[1]
    sel = np.zeros((identity_col + 1, ROPE_W), np.float32)

    def plan(lane0, width, group, rot_lo, rot_dim):
        for c in range(width):
            j = c % group - rot_lo
            if 0 <= j < rot_dim:
                quarter, f = divmod(j, rot_dim // 4)
                sel[col_of[rot_dim, quarter // 2] + f, lane0 + c] = 1.0
            else:
                sel[identity_col, lane0 + c] = 1.0

    plan(0, A_ROT_W, HEAD_DIM, 0, HEAD_DIM)
    plan(A_ROT_W, C_ROT_W, C_ROT_W, 64, MLA_ROPE)
    plan(A_ROT_W + C_ROT_W, D_ROT_W, HEAD_DIM, 0, HEAD_DIM)
    spread = lambda t: jnp.dot(t, jnp.asarray(sel), precision=lax.Precision.HIGHEST)
    ones, zeros = jnp.ones_like(grid_row), jnp.zeros_like(grid_row)
    return (spread(jnp.concatenate([jnp.cos(ang), ones], axis=1)),
            spread(jnp.concatenate([jnp.sin(ang), zeros], axis=1)))


def _swa_kernel(sink_ref, q_ref, kp_ref, ko_ref, kn_ref, kc_ref, vp_ref, vo_ref, vn_ref, vc_ref, o_ref):
    t = pl.program_id(1)
    last = pl.num_programs(1) - 1
    tq = q_ref.shape[0]
    half = tq // 2

    def attend(k, v, mask):
        v_lane = lax.broadcasted_iota(jnp.int32, v.shape, 1)
        v_ones = [jnp.where(v_lane >= HEAD_DIM, jnp.ones_like(v), v), jnp.where(v_lane < HEAD_DIM, jnp.ones_like(v), v)]
        heads = range(4)
        s = [_dot_nt(q_ref[:, LANES * h:LANES * (h + 1)], k) for h in heads]
        if mask is not None:
            s = [jnp.where(mask, x, NEG_INF) for x in s]
        sink = [sink_ref[h] * LOG2_E for h in heads]
        m = [jnp.maximum(s[h].max(axis=-1, keepdims=True), sink[h]) for h in heads]
        p = [jnp.exp2(s[h] - m[h]).astype(BF16) for h in heads]
        o = [_dot(p[h], v_ones[h // 2]) for h in heads]
        den_lane = [HEAD_DIM if h // 2 == 0 else 0 for h in heads]
        outs = [o[h] / (o[h][:, den_lane[h]:den_lane[h] + 1] + jnp.exp2(sink[h] - m[h])) for h in heads]
        lane = lax.broadcasted_iota(jnp.int32, (tq, LANES), 1)
        for r in range(2):
            o_ref[:, LANES * r:LANES * (r + 1)] = jnp.where(lane < HEAD_DIM, outs[r], outs[2 + r]).astype(o_ref.dtype)

    @pl.when(t == 0)
    def _():
        attend(kc_ref[...], vc_ref[...], None)

    @pl.when(t > 0)
    def _():
        band = 2 * half + tq
        qi = lax.broadcasted_iota(jnp.int32, (tq, band + CTX_LEN), 0)
        col = lax.broadcasted_iota(jnp.int32, (tq, band + CTX_LEN), 1)
        in_window = jnp.abs(col - half - qi) <= SWA_WINDOW
        exists = ((col >= half) | (t > 1)) & ((col < half + tq) | (t < last))
        mask = (col >= band) | (in_window & exists)
        attend(jnp.concatenate([kp_ref[...], ko_ref[...], kn_ref[...], kc_ref[...]], axis=0),
               jnp.concatenate([vp_ref[...], vo_ref[...], vn_ref[...], vc_ref[...]], axis=0), mask)


def swa_mixer(pa, sink):
    b, l, _ = pa.shape
    tq = ROW_TILE
    nblk = l // SWA_WINDOW
    kcol, vcol = 4, 5
    prev = lambda c: (lambda i, t: (i, jnp.maximum(2 * t - 1, 2), c))
    nxt = lambda c: (lambda i, t: (i, jnp.minimum(2 * t + 2, nblk - 1), c))
    own = lambda c: (lambda i, t: (i, t, c))
    ctx = lambda c: (lambda i, t: (i, 0, c))
    kv_specs = lambda c: [pl.BlockSpec((None, SWA_WINDOW, LANES), prev(c)), pl.BlockSpec((None, tq, LANES), own(c)),
                          pl.BlockSpec((None, SWA_WINDOW, LANES), nxt(c)), pl.BlockSpec((None, tq, LANES), ctx(c))]
    return pl.pallas_call(
        _swa_kernel,
        grid=(b, l // tq),
        in_specs=[pl.BlockSpec(memory_space=pltpu.SMEM),
                  pl.BlockSpec((None, tq, 4 * LANES), lambda i, t: (i, t, 0))] + kv_specs(kcol) + kv_specs(vcol),
        out_specs=pl.BlockSpec((None, tq, 2 * LANES), lambda i, t: (i, t, 0)),
        out_shape=jax.ShapeDtypeStruct((b, l, 2 * LANES), BF16),
        compiler_params=pltpu.CompilerParams(
            dimension_semantics=("parallel", "parallel"), vmem_limit_bytes=VMEM_LIMIT),
        name="swa",
    )(sink, pa, pa, pa, pa, pa, pa, pa, pa, pa)


def build_mla_weights(q_norm, kv_norm, w_q_up, w_kv_up):
    scale = (MLA_NOPE + MLA_ROPE) ** -0.5 * LOG2_E
    wq = (w_q_up * scale).reshape(-1, MLA_HEADS, MLA_NOPE + MLA_ROPE)
    zq = jnp.zeros(wq.shape[:2] + (LANES - MLA_NOPE - MLA_ROPE,), F32)
    wq_main = jnp.concatenate([wq, zq], axis=-1).reshape(-1, MLA_HEADS * LANES)
    wq_rot = jnp.concatenate([jnp.zeros_like(wq[..., :MLA_NOPE]), _rot_cols(wq[..., MLA_NOPE:], MLA_ROPE), zq],
                             axis=-1).reshape(-1, MLA_HEADS * LANES)
    wkv = w_kv_up.reshape(-1, MLA_HEADS, MLA_NOPE + MLA_V)
    wk = jnp.concatenate([wkv[..., :MLA_NOPE], jnp.zeros_like(wkv[..., :LANES - MLA_NOPE])],
                         axis=-1).reshape(-1, MLA_HEADS * LANES)
    wv = wkv[..., MLA_NOPE:].reshape(-1, MLA_HEADS * MLA_V)
    return [q_norm[None, :], kv_norm[None, :], wq_main.astype(BF16), wq_rot.astype(BF16), wk.astype(BF16),
            wv.astype(BF16)]


def _mla_attn_kernel(q_ref, k_ref, v_ref, o_ref):
    t = pl.program_id(2)
    tq = q_ref.shape[0]

    def attend(nk):
        v = v_ref[0:nk, :]
        v_lane = lax.broadcasted_iota(jnp.int32, v.shape, 1)
        v_ones = [jnp.where(v_lane >= MLA_V, jnp.ones_like(v), v), jnp.where(v_lane < MLA_V, jnp.ones_like(v), v)]
        pair = range(2)
        s = [_dot_nt(q_ref[:, LANES * j:LANES * (j + 1)], k_ref[0:nk, LANES * j:LANES * (j + 1)]) for j in pair]
        p = [jnp.exp2(s[j] - s[j].max(axis=-1, keepdims=True)).astype(BF16) for j in pair]
        o = [_dot(p[j], v_ones[j]) for j in pair]
        outs = [o[j] / o[j][:, (MLA_V, 0)[j]:(MLA_V, 0)[j] + 1] for j in pair]
        lane = lax.broadcasted_iota(jnp.int32, (tq, LANES), 1)
        o_ref[...] = jnp.where(lane < MLA_V, outs[0], outs[1]).astype(o_ref.dtype)

    @pl.when(t == 0)
    def _():
        attend(CTX_LEN)

    @pl.when(t > 0)
    def _():
        attend(k_ref.shape[0])


def mla_attention(q, k, v):
    b, l, _ = q.shape
    tq = ROW_TILE
    return pl.pallas_call(
        _mla_attn_kernel,
        grid=(b, 2, l // tq),
        in_specs=[pl.BlockSpec((None, tq, 2 * LANES), lambda i, p, t: (i, t, p)),
                  pl.BlockSpec((None, l, 2 * LANES), lambda i, p, t: (i, 0, p)),
                  pl.BlockSpec((None, l, LANES), lambda i, p, t: (i, 0, p))],
        out_specs=pl.BlockSpec((None, tq, LANES), lambda i, p, t: (i, t, p)),
        out_shape=jax.ShapeDtypeStruct((b, l, 2 * LANES), BF16),
        compiler_params=pltpu.CompilerParams(
            dimension_semantics=("parallel", "parallel", "parallel"), vmem_limit_bytes=VMEM_LIMIT),
        name="mla_attn",
    )(q, k, v)


def _head_mean(x, ones_bd):
    hi = x.astype(BF16)
    lo = (x - hi.astype(F32)).astype(BF16)
    return (_dot(hi, ones_bd) + _dot(lo, ones_bd)) * (1.0 / HEAD_DIM)


def _ret_kernel(x_ref, lg_ref, g_ref, o_ref, s_ref, dec_ref, part_ref):
    dr = pl.program_id(0)
    s = pl.program_id(1)
    ns = pl.num_programs(1)
    nb, c = x_ref.shape[0], x_ref.shape[1]
    w = RET_HEADS * HEAD_DIM
    chunk = jnp.where(s == 0, 0, jnp.where(dr == 0, s, ns - s))
    lg = lg_ref[...]
    fwd = dr == 0
    row_h = lax.broadcasted_iota(jnp.int32, (w, w), 0) // HEAD_DIM
    col_h = lax.broadcasted_iota(jnp.int32, (w, w), 1) // HEAD_DIM
    same_head = row_h == col_h

    @pl.when(s == 0)
    def _():
        s_ref[...] = jnp.zeros_like(s_ref)
        i = lax.broadcasted_iota(jnp.int32, (c, c), 0)
        j = lax.broadcasted_iota(jnp.int32, (c, c), 1)
        rel = jnp.where(fwd, i - j, j - i)
        relf = jnp.maximum(rel, 0).astype(F32)
        for h in range(RET_HEADS):
            lg_h = lg_ref[0:1, HEAD_DIM * h:HEAD_DIM * h + 1]
            dec_ref[h] = jnp.where(rel >= 0, jnp.exp(lg_h * relf), 0.0)

    pos = lax.broadcasted_iota(jnp.int32, (c, 1), 0).astype(F32)
    q_dec = jnp.exp(lg * jnp.where(fwd, pos + 1.0, c - pos))
    k_dec = jnp.exp(lg * jnp.where(fwd, c - 1.0 - pos, pos))
    lane_h = lax.broadcasted_iota(jnp.int32, (c, w), 1) // HEAD_DIM
    bs = range(nb)
    q = [x_ref[i, :, 0:w] for i in bs]
    kf = [x_ref[i, :, w:2 * w] for i in bs]
    v = [x_ref[i, :, 2 * w:3 * w].astype(BF16) for i in bs]
    kb = [x.astype(BF16) for x in kf]
    qb = [x.astype(BF16) for x in q]
    zero_bf = jnp.zeros((c, w), BF16)
    hs = range(RET_HEADS)
    a = [[(_dot_nt(jnp.where(lane_h == h, qb[i], zero_bf), kb[i]) * dec_ref[h]).astype(BF16) for h in hs] for i in bs]
    lhs = [jnp.concatenate([(q[i] * q_dec).astype(BF16)] + a[i], axis=1) for i in bs]
    rhs = [jnp.concatenate([s_ref[i].astype(BF16)] + [jnp.where(lane_h == h, v[i], zero_bf) for h in hs], axis=0)
           for i in bs]
    acc = [_dot(lhs[i], rhs[i]) for i in bs]
    kv = [_dot_tn((kf[i] * k_dec).astype(BF16), v[i]) for i in bs]
    chunk_dec = jnp.exp(lg * float(c))
    for i in bs:
        s_ref[i] = s_ref[i] * chunk_dec + jnp.where(same_head, kv[i], 0.0)

    rows = pl.ds(pl.multiple_of(chunk * c, c), c)

    @pl.when(dr == 0)
    def _():
        for i in bs:
            part_ref[i, rows, :] = acc[i]

    @pl.when(dr == 1)
    def _():
        ones_bd = jnp.where(same_head, 1.0, 0.0).astype(BF16)
        for i in bs:
            o = part_ref[i, rows, :] + acc[i]
            mu = _head_mean(o, ones_bd)
            var = _head_mean(jnp.square(o - mu), ones_bd)
            y = (o - mu) * lax.rsqrt(var + EPS) * g_ref[...]
            o_ref[i] = (y * _silu(x_ref[i, :, 3 * w:4 * w])).astype(o_ref.dtype)


def retention_mixer(pd, log_decay, norm_g):
    b, l, _ = pd.shape
    c = ROW_TILE
    ns = l // c
    w = RET_HEADS * HEAD_DIM
    lg = jnp.repeat(-jnp.exp(log_decay.astype(F32)), HEAD_DIM, axis=-1)[:, None, :]

    def chunk_of(dr, s):
        return jnp.where(s == 0, 0, jnp.where(dr == 0, s, ns - s))

    return pl.pallas_call(
        _ret_kernel,
        grid=(2, ns),
        in_specs=[pl.BlockSpec((b, c, D_W), lambda dr, s: (0, chunk_of(dr, s), 0)),
                  pl.BlockSpec((None, 1, w), lambda dr, s: (dr, 0, 0)),
                  pl.BlockSpec((1, w), lambda dr, s: (0, 0))],
        out_specs=pl.BlockSpec((b, c, w), lambda dr, s: (0, jnp.where(dr == 0, 0, chunk_of(dr, s)), 0)),
        out_shape=jax.ShapeDtypeStruct((b, l, w), BF16),
        scratch_shapes=[pltpu.VMEM((b, w, w), F32), pltpu.VMEM((RET_HEADS, c, c), F32), pltpu.VMEM((b, l, w), F32)],
        compiler_params=pltpu.CompilerParams(
            dimension_semantics=("arbitrary", "arbitrary"), vmem_limit_bytes=VMEM_LIMIT),
        name="retention",
    )(pd, lg, norm_g[None, :])


def _out_proj_kernel(ma_ref, mb_ref, mc_ref, md_ref, h_ref, *refs, with_router):
    mods, (g_ref, w_ref), rest = refs[:WIDE], refs[WIDE:WIDE + 2], refs[WIDE + 2:]
    if with_router:
        wr_ref, hn_ref, v_ref, lg_ref = rest
    else:
        hn_ref, v_ref = rest
    gw = 2 * LANES
    mix = functools.reduce(jnp.add, [
        _dot(m_ref[...].astype(BF16), w_ref[gw * i:gw * (i + 1), :])
        for i, m_ref in enumerate((ma_ref, mb_ref, mc_ref, md_ref))])
    for k, mod_ref in enumerate(mods):
        r = _sub_rows(k)
        hn = h_ref[r, :] + mod_ref[2:3, :] * mix[r]
        hn_ref[r, :] = hn
        v = _rms(hn) * g_ref[...] * (1.0 + mod_ref[4:5, :]) + mod_ref[3:4, :]
        v_ref[r, :] = v.astype(v_ref.dtype)
        if with_router:
            lg_ref[r, :] = jnp.dot(v, wr_ref[...], preferred_element_type=F32, precision=lax.Precision.HIGHEST)


def out_proj(mixes, h, mod, gain, w, layer, w_router=None):
    b, l, d = h.shape
    tm = WIDE * ROW_TILE
    n = b * l
    with_router = w_router is not None
    row = lambda j: (j, 0)
    const = lambda j: (0, 0)
    in_specs = [pl.BlockSpec((tm, 2 * LANES), row) for _ in mixes] + [pl.BlockSpec((tm, d), row)] + _sub_mod_specs(
        d, l // ROW_TILE) + [pl.BlockSpec((1, d), const), _layer_weight_spec(w, layer)]
    out_specs = [pl.BlockSpec((tm, d), row), pl.BlockSpec((tm, d), row)]
    out_shape = [jax.ShapeDtypeStruct((n, d), F32), jax.ShapeDtypeStruct((n, d), F32 if with_router else BF16)]
    args = [m.reshape(n, 2 * LANES) for m in mixes] + [h.reshape(n, d)] + [mod] * WIDE + [gain, w]
    if with_router:
        in_specs.append(pl.BlockSpec(w_router.shape, const))
        out_specs.append(pl.BlockSpec((tm, LANES), row))
        out_shape.append(jax.ShapeDtypeStruct((n, LANES), F32))
        args.append(w_router)
    outs = pl.pallas_call(
        functools.partial(_out_proj_kernel, with_router=with_router),
        grid=(n // tm,),
        in_specs=in_specs,
        out_specs=out_specs,
        out_shape=out_shape,
        compiler_params=pltpu.CompilerParams(dimension_semantics=("parallel",), vmem_limit_bytes=VMEM_LIMIT),
        name="out_proj",
    )(*args)
    return [o.reshape(b, l, -1) for o in outs]


def build_out_weight(w):
    hd = HEAD_DIM
    rows = lambda lo, hi: w[..., lo:hi, :]
    return jnp.concatenate([rows(0, hd), rows(2 * hd, 3 * hd), rows(hd, 2 * hd), rows(3 * hd, None)],
                           axis=-2).astype(BF16)


def _ffn_kernel(v_ref, h_ref, *refs):
    mods, (wg_ref, wu_ref, wd_ref, o_ref) = refs[:WIDE], refs[WIDE:]
    v = v_ref[...]
    acc = jnp.zeros(o_ref.shape, F32)
    for j in range(D_FF // FF_CHUNK):
        cols = slice(j * FF_CHUNK, (j + 1) * FF_CHUNK)
        a = _dot(v, wg_ref[:, cols])
        u = _dot(v, wu_ref[:, cols])
        mid = (_silu(a) * u).astype(BF16)
        acc = acc + _dot(mid, wd_ref[cols, :])
    for k, mod_ref in enumerate(mods):
        r = _sub_rows(k)
        o_ref[r, :] = h_ref[r, :] + mod_ref[5:6, :] * acc[r]


def dense_ffn(v, h, mod, layer, wg, wu, wd):
    b, l, d = h.shape
    tm = WIDE * ROW_TILE
    n = b * l
    row = lambda j: (j, 0)
    const = lambda j: (0, 0)
    return pl.pallas_call(
        _ffn_kernel,
        grid=(n // tm,),
        in_specs=[pl.BlockSpec((tm, d), row), pl.BlockSpec((tm, d), row)] + _sub_mod_specs(d, l // ROW_TILE) + [
            _layer_weight_spec(wg, layer), _layer_weight_spec(wu, layer), _layer_weight_spec(wd, layer)],
        out_specs=pl.BlockSpec((tm, d), row),
        out_shape=jax.ShapeDtypeStruct((n, d), F32),
        compiler_params=pltpu.CompilerParams(dimension_semantics=("parallel",), vmem_limit_bytes=VMEM_LIMIT),
        name="dense_ffn",
    )(v.reshape(n, d), h.reshape(n, d), *([mod] * WIDE), wg, wu, wd).reshape(b, l, d)


def _moe_kernel(wt_ref, we_ref, lo_ref, hi_ref, first_ref, x_ref, wg_ref, wu_ref, wd_ref, o_ref, xm_ref, acc_ref):
    w = pl.program_id(0)
    j = pl.program_id(1)
    nj = pl.num_programs(1)
    tm = x_ref.shape[0]

    @pl.when(j == 0)
    def _():
        row = wt_ref[w] * tm + lax.broadcasted_iota(jnp.int32, (tm, 1), 0)
        keep = (row >= lo_ref[w]) & (row < hi_ref[w])
        xm_ref[...] = jnp.where(keep, x_ref[...], 0.0).astype(BF16)

    @pl.when((j == 0) & (first_ref[w] > 0))
    def _():
        acc_ref[...] = jnp.zeros_like(acc_ref)

    def swiglu_rows(rows):
        x = xm_ref[rows, :]
        mid = (_silu(_dot(x, wg_ref[...])) * _dot(x, wu_ref[...])).astype(BF16)
        acc_ref[rows, :] += _dot(mid, wd_ref[...])

    tile_lo = wt_ref[w] * tm
    whole = (lo_ref[w] <= tile_lo) & (hi_ref[w] >= tile_lo + tm)

    @pl.when(whole)
    def _():
        swiglu_rows(slice(0, tm))

    for part in range(tm // MOE_PART):
        part_lo = tile_lo + part * MOE_PART

        @pl.when(jnp.logical_not(whole) & (hi_ref[w] > jnp.maximum(part_lo, lo_ref[w]))
                 & (lo_ref[w] < part_lo + MOE_PART))
        def _():
            swiglu_rows(slice(part * MOE_PART, (part + 1) * MOE_PART))

    @pl.when(j == nj - 1)
    def _():
        o_ref[...] = acc_ref[...]


def moe_grouped_ffn(xs, items, layer_idx, wg, wu, wd):
    s, d = xs.shape
    tm = MOE_TILE
    fc = MOE_FF_CHUNK
    nw = items[0].shape[0]
    nj = D_FF // fc
    grid_spec = pltpu.PrefetchScalarGridSpec(
        num_scalar_prefetch=5,
        grid=(nw, nj),
        in_specs=[
            pl.BlockSpec((tm, d), lambda w, j, wt, we, lo, hi, fi: (wt[w], 0)),
            pl.BlockSpec((None, None, d, fc), lambda w, j, wt, we, lo, hi, fi: (layer_idx, we[w], 0, j)),
            pl.BlockSpec((None, None, d, fc), lambda w, j, wt, we, lo, hi, fi: (layer_idx, we[w], 0, j)),
            pl.BlockSpec((None, None, fc, d), lambda w, j, wt, we, lo, hi, fi: (layer_idx, we[w], j, 0)),
        ],
        out_specs=pl.BlockSpec((tm, d), lambda w, j, wt, we, lo, hi, fi: (wt[w], 0)),
        scratch_shapes=[pltpu.VMEM((tm, d), BF16), pltpu.VMEM((tm, d), F32)],
    )
    return pl.pallas_call(
        _moe_kernel,
        grid_spec=grid_spec,
        out_shape=jax.ShapeDtypeStruct((s, d), F32),
        compiler_params=pltpu.CompilerParams(
            dimension_semantics=("arbitrary", "arbitrary"), vmem_limit_bytes=VMEM_LIMIT),
        name="moe_ffn",
    )(*items, xs, wg, wu, wd)


def _residual_kernel(h_ref, f0_ref, f1_ref, gate_ref, mod_ref, *rest):
    f = gate_ref[:, 0:1] * f0_ref[...] + gate_ref[:, 1:2] * f1_ref[...]
    hn = h_ref[...] + mod_ref[5:6, :] * f
    if len(rest) == 2:
        gain_ref, o_ref = rest
        o_ref[...] = _rms(hn) * gain_ref[...]
    else:
        rest[0][...] = hn


def gated_residual(h, f0, f1, gates, mod, final_gain=None):
    b, l, d = h.shape
    tm = ROW_TILE
    skip = 0 if final_gain is None else CTX_LEN // tm
    row = lambda i, t: (i, t + skip, 0)
    in_specs = [pl.BlockSpec((None, tm, d), row), pl.BlockSpec((None, tm, d), row), pl.BlockSpec((None, tm, d), row),
                pl.BlockSpec((None, tm, LANES), row),
                pl.BlockSpec((None, None, SUBLANES, d), lambda i, t: (i, jnp.minimum(t + skip, 1), 0, 0))]
    args = [h, f0, f1, gates, mod]
    if final_gain is not None:
        in_specs.append(pl.BlockSpec((1, d), lambda i, t: (0, 0)))
        args.append(final_gain)
    return pl.pallas_call(
        _residual_kernel,
        grid=(b, l // tm - skip),
        in_specs=in_specs,
        out_specs=pl.BlockSpec((None, tm, d), lambda i, t: (i, t, 0)),
        out_shape=jax.ShapeDtypeStruct((b, l - skip * tm, d), F32),
        compiler_params=pltpu.CompilerParams(dimension_semantics=("parallel", "parallel")),
        name="gated_residual",
    )(*args)


def moe_ffn(v, logits, h, mod, layer_idx, wg, wu, wd, final_gain=None):
    b, l, d = h.shape
    t = b * l
    s = TOP_K * t
    tm = MOE_TILE
    nt = s // tm
    nw = nt + N_EXPERTS - 1
    i32 = jnp.int32
    lg = logits.reshape(t, LANES)[:, :N_EXPERTS]
    top_val, top_idx = lax.top_k(lg, TOP_K)
    gates = jax.nn.softmax(top_val, axis=-1)
    slot = jnp.arange(s, dtype=i32)
    skey = jnp.sort(top_idx.reshape(-1).astype(i32) * s + slot)
    order = skey % s
    _, inv = lax.sort_key_val(order, slot)
    bounds = (jnp.arange(N_EXPERTS, dtype=i32) + 1) * s
    cum = jnp.sum((skey[None, :] < bounds[:, None]).astype(i32), axis=1)
    cum_prev = jnp.concatenate([jnp.zeros((1,), i32), cum[:-1]])
    tile_lo = jnp.arange(nt, dtype=i32) * tm
    count_le = lambda edges, x: jnp.sum((edges[None, :] <= x[:, None]).astype(i32), axis=1)
    e_first = count_le(cum, tile_lo)
    e_last = count_le(cum, tile_lo + tm - 1)
    n_items = e_last - e_first + 1
    item_end = jnp.cumsum(n_items)
    item_start = item_end - n_items
    w = jnp.arange(nw, dtype=i32)
    wt = jnp.minimum(count_le(item_end, w), nt - 1)
    valid = w < item_end[-1]
    we = jnp.clip(e_first[wt] + w - item_start[wt], 0, N_EXPERTS - 1).astype(i32)
    lo = jnp.where(valid, cum_prev[we], 0).astype(i32)
    hi = jnp.where(valid, cum[we], 0).astype(i32)
    first = (valid & (w == item_start[wt])).astype(i32)
    rows_of = lambda a, idx: a.at[idx].get(mode="promise_in_bounds")
    xs = rows_of(v.reshape(t, d), order // TOP_K)
    ys = moe_grouped_ffn(xs, (wt, we, lo, hi, first), layer_idx, wg, wu, wd)
    dest = inv.reshape(t, TOP_K)
    f0 = rows_of(ys, dest[:, 0]).reshape(b, l, d)
    f1 = rows_of(ys, dest[:, 1]).reshape(b, l, d)
    gates_p = jnp.pad(gates, ((0, 0), (0, LANES - TOP_K))).reshape(b, l, LANES)
    return gated_residual(h, f0, f1, gates_p, mod, final_gain)


def _final_norm_kernel(h_ref, g_ref, o_ref):
    o_ref[...] = _rms(h_ref[...]) * g_ref[...]


def final_rms_norm(h, gain, n_ctx_tiles):
    b, l, d = h.shape
    tm = ROW_TILE
    n = l - n_ctx_tiles * tm
    return pl.pallas_call(
        _final_norm_kernel,
        grid=(b, n // tm),
        in_specs=[
            pl.BlockSpec((None, tm, d), lambda i, t: (i, t + n_ctx_tiles, 0)),
            pl.BlockSpec((1, d), lambda i, t: (0, 0)),
        ],
        out_specs=pl.BlockSpec((None, tm, d), lambda i, t: (i, t, 0)),
        out_shape=jax.ShapeDtypeStruct((b, n, d), F32),
        compiler_params=pltpu.CompilerParams(dimension_semantics=("parallel", "parallel")),
        name="final_norm",
    )(h, gain)


GDN_W = GDN_HEADS * GDN_DK
GDN_CONV_K = 5
GDN_HALO = SUBLANES


def _split3(x):
    p0 = x.astype(BF16)
    r1 = x - p0.astype(F32)
    p1 = r1.astype(BF16)
    p2 = (r1 - p1.astype(F32)).astype(BF16)
    return p0, p1, p2


def _gdn_prep_kernel(x_ref, prev_ref, next_ref, cw_ref, par_ref, q_ref, k_ref, v_ref, gb_ref):
    t = pl.program_id(1)
    last = pl.num_programs(1) - 1
    tm = x_ref.shape[0]
    w3 = 3 * GDN_W
    has_prev = t > 1
    has_next = (t > 0) & (t < last)
    prev = jnp.where(has_prev, prev_ref[...], 0.0)
    nxt = jnp.where(has_next, next_ref[...], 0.0)
    xe = jnp.concatenate([prev, x_ref[:, :w3], nxt], axis=0)
    y = jnp.zeros((tm, w3), F32)
    for j in range(GDN_CONV_K):
        lo = GDN_HALO - GDN_CONV_K // 2 + j
        y = y + cw_ref[j:j + 1, :] * xe[lo:lo + tm, :]
    y = _silu(y)
    r = lax.broadcasted_iota(jnp.int32, (GDN_W, GDN_W), 0)
    c = lax.broadcasted_iota(jnp.int32, (GDN_W, GDN_W), 1)
    ones_bd = jnp.where(r // GDN_DK == c // GDN_DK, 1.0, 0.0).astype(BF16)

    def l2n(x):
        sq = x * x
        hi = sq.astype(BF16)
        lo = (sq - hi.astype(F32)).astype(BF16)
        return x * lax.rsqrt(_dot(hi, ones_bd) + _dot(lo, ones_bd) + EPS)

    q_ref[...] = l2n(y[:, :GDN_W]) * GDN_DK ** -0.5
    k_ref[...] = l2n(y[:, GDN_W:2 * GDN_W])
    v_ref[...] = y[:, 2 * GDN_W:]
    ab = x_ref[:, w3 + GDN_W:]
    lane = lax.broadcasted_iota(jnp.int32, ab.shape, 1)
    is_g = (lane % 8) < 4
    z = ab + par_ref[1:2, :]
    softplus = jnp.maximum(z, 0.0) + jnp.log1p(jnp.exp(-jnp.abs(z)))
    g = jnp.where(is_g, par_ref[0:1, :] * softplus, 0.0)
    beta = 1.0 / (1.0 + jnp.exp(-ab))
    i = lax.broadcasted_iota(jnp.int32, (tm, tm), 0)
    j = lax.broadcasted_iota(jnp.int32, (tm, tm), 1)
    same_chunk = i // GDN_CHUNK == j // GDN_CHUNK
    tri_f = jnp.where(same_chunk & (j <= i), 1.0, 0.0).astype(BF16)
    tri_b = jnp.where(same_chunk & (j >= i), 1.0, 0.0).astype(BF16)
    pieces = _split3(g)
    gc_f = functools.reduce(jnp.add, [_dot(tri_f, p) for p in pieces])
    gc_b = functools.reduce(jnp.add, [_dot(tri_b, p) for p in pieces])
    gb_ref[...] = jnp.where(is_g, jnp.where(lane < 8, gc_f, gc_b), beta)


def gdn_prep(pb, conv_w, a_log, dt_bias):
    b, l, _ = pb.shape
    tm = ROW_TILE
    w3 = 3 * GDN_W
    halo_blocks = tm // GDN_HALO
    n_halo = l // GDN_HALO
    cw = jnp.pad(conv_w, ((0, SUBLANES - GDN_CONV_K), (0, 0)))
    neg_a = jnp.pad(-jnp.exp(a_log.astype(F32)), ((0, 0), (0, 4))).reshape(-1)
    dtb = jnp.pad(dt_bias.astype(F32), ((0, 0), (0, 4))).reshape(-1)
    par = jnp.pad(jnp.stack([neg_a, dtb]), ((0, SUBLANES - 2), (0, LANES - 16)))
    row = lambda i, t: (i, t, 0)
    out = lambda w: pl.BlockSpec((None, tm, w), row)
    return pl.pallas_call(
        _gdn_prep_kernel,
        grid=(b, l // tm),
        in_specs=[pl.BlockSpec((None, tm, B_W), row),
                  pl.BlockSpec((None, GDN_HALO, w3), lambda i, t: (i, jnp.maximum(t * halo_blocks - 1, 0), 0)),
                  pl.BlockSpec((None, GDN_HALO, w3),
                               lambda i, t: (i, jnp.minimum((t + 1) * halo_blocks, n_halo - 1), 0)),
                  pl.BlockSpec(cw.shape, lambda i, t: (0, 0)),
                  pl.BlockSpec(par.shape, lambda i, t: (0, 0))],
        out_specs=[out(GDN_W), out(GDN_W), out(GDN_W), out(LANES)],
        out_shape=[jax.ShapeDtypeStruct((b, l, GDN_W), F32)] * 3 + [jax.ShapeDtypeStruct((b, l, LANES), F32)],
        compiler_params=pltpu.CompilerParams(
            dimension_semantics=("parallel", "parallel"), vmem_limit_bytes=VMEM_LIMIT),
        name="gdn_prep",
    )(pb, pb, pb, cw, par)


def _tile_heads(x):
    return jnp.concatenate([x] * GDN_HEADS, axis=0)


def _collapse_heads(x):
    c = GDN_CHUNK
    return x[0:c] + x[c:2 * c] + x[2 * c:3 * c] + x[3 * c:4 * c]


def _gdn_chunk_kernel(q_ref, k_ref, v_ref, gb_ref, o0_ref, qe_ref, a_ref, bm_ref, gam_ref):
    n = GDN_W
    cs = GDN_CHUNK
    r = lax.broadcasted_iota(jnp.int32, (n, n), 0)
    c = lax.broadcasted_iota(jnp.int32, (n, n), 1)
    ri, ci = r % cs, c % cs
    head = r // cs == c // cs
    eye = jnp.where(r == c, 1.0, 0.0)
    blk = lambda s: r // s == c // s
    b8, b16, b32 = blk(8), blk(16), blk(32)
    lane = lax.broadcasted_iota(jnp.int32, (n, LANES), 1)
    row_head = lax.broadcasted_iota(jnp.int32, (n, LANES), 0) // cs
    pick = lambda sel, x: jnp.sum(jnp.where(sel, x, 0.0), axis=1, keepdims=True)
    src_lane = lax.broadcasted_iota(jnp.int32, (LANES, n), 0)
    dst_head = lax.broadcasted_iota(jnp.int32, (LANES, n), 1) // cs
    tri, tri_strict, sel_g, sel_b, widen_g, widen_b = {}, {}, {}, {}, {}, {}
    for fwd in (True, False):
        ahead = ri - ci if fwd else ci - ri
        tri[fwd] = head & (ahead >= 0)
        tri_strict[fwd] = head & (ahead > 0)
        lane0 = 0 if fwd else 8
        sel_g[fwd] = lane == lane0 + row_head
        sel_b[fwd] = lane == lane0 + 4 + row_head
        widen_g[fwd] = jnp.where(src_lane == lane0 + dst_head, 1.0, 0.0).astype(BF16)
        widen_b[fwd] = jnp.where(src_lane == lane0 + 4 + dst_head, 1.0, 0.0).astype(BF16)

    n_chunks = q_ref.shape[0] // cs
    rows = [slice(ch * cs, (ch + 1) * cs) for ch in range(n_chunks)]
    items = [(fwd, ch) for fwd in (True, False) for ch in range(n_chunks)]
    dirs = [fwd for fwd, _ in items]
    per_item = lambda xs: [xs[ch] for _, ch in items]
    each = lambda f, *xs: [f(*a) for a in zip(*xs)]
    bf = lambda xs: [x.astype(BF16) for x in xs]
    widen = lambda x, e: functools.reduce(jnp.add, [_dot(p, e) for p in _split3(x)])
    zero_bf = jnp.zeros((n, n), BF16)
    spread = lambda xs: [jnp.where(head, _tile_heads(x.astype(BF16)), zero_bf) for x in xs]
    k_c, q_c, v_c = ([ref[rw, :] for rw in rows] for ref in (k_ref, q_ref, v_ref))
    khb_c, qhb_c = spread(k_c), spread(q_c)
    kk = per_item(each(_dot_nt, khb_c, khb_c))
    qk = per_item(each(_dot_nt, qhb_c, khb_c))
    k_t, q_t, v_t = per_item(k_c), per_item(q_c), per_item(v_c)
    gb = per_item([gb_ref[rw, :] for rw in rows])
    gb4 = [_tile_heads(x) for x in gb]
    gc = [pick(sel_g[f], x) for f, x in zip(dirs, gb4)]
    beta = [pick(sel_b[f], x) for f, x in zip(dirs, gb4)]
    gc_t = [widen(x, widen_g[f]) for f, x in zip(dirs, gb)]
    beta_t = [widen(x, widen_b[f]) for f, x in zip(dirs, gb)]
    ends = [cs - 1 if f else 0 for f in dirs]
    gl_t = [jnp.broadcast_to(x[e:e + 1, :], x.shape) for x, e in zip(gc_t, ends)]
    gc_b = [jnp.broadcast_to(x, (n, n)) for x in gc]
    decay = [jnp.exp(jnp.minimum(x - x.T, 0.0)) for x in gc_b]
    lmat = [jnp.where(tri_strict[f], b_ * kk_ * d_, 0.0) for f, b_, kk_, d_ in zip(dirs, beta, kk, decay)]
    attn = bf([jnp.where(tri[f], qk_ * d_, 0.0) for f, qk_, d_ in zip(dirs, qk, decay)])
    nl = bf([jnp.where(b8, -x, 0.0) for x in lmat])
    n2 = bf(each(_dot, nl, nl))
    n4 = each(_dot, n2, n2)
    p1 = bf(each(lambda a, b_: _dot((eye + a).astype(BF16), (eye + b_).astype(BF16)), nl, n2))
    tinv = each(lambda p, x: _dot(p, (eye + x).astype(BF16)), p1, n4)

    def moving_rows(fwd, x, sz):
        return jnp.concatenate([x[i:i + sz] for i in range(sz if fwd else 0, n, 2 * sz)], axis=0)

    def with_moving_rows(fwd, x, new, sz):
        pieces = []
        for j, i in enumerate(range(0, n, 2 * sz)):
            kept = x[i:i + sz] if fwd else x[i + sz:i + 2 * sz]
            moved = new[j * sz:(j + 1) * sz]
            pieces += [kept, moved] if fwd else [moved, kept]
        return jnp.concatenate(pieces, axis=0)

    for sz, inner, outer in ((8, b8, b16), (16, b16, b32), (32, b32, head)):
        off = bf([jnp.where(outer & ~inner, x, 0.0) for x in lmat])
        tb = bf(tinv)
        t_mv = [moving_rows(f, x, sz) for f, x in zip(dirs, tinv)]
        to = bf(each(_dot, bf(t_mv), off))
        tinv = [with_moving_rows(f, t_, tm_ - _dot(to_, tb_), sz)
                for f, t_, tm_, to_, tb_ in zip(dirs, tinv, t_mv, to, tb)]
    tb = bf(tinv)
    eg_t = [jnp.exp(x) for x in gc_t]
    u = bf(each(_dot, tb, spread(each(lambda b_, v_: b_ * v_, beta_t, v_t))))
    w = bf(each(_dot, tb, spread(each(lambda b_, e_, k_: (b_ * e_) * k_, beta_t, eg_t, k_t))))
    o0 = each(_dot, attn, u)
    aw = each(_dot, attn, w)
    kg = spread(each(lambda k_, gl_, gc_: k_ * jnp.exp(gl_ - gc_), k_t, gl_t, gc_t))
    a_mat = each(_dot_tn, kg, w)
    b_mat = each(_dot_tn, kg, u)
    for (fwd, ch), o0_, aw_, a_, b_, q_, e_, gl_ in zip(items, o0, aw, a_mat, b_mat, q_t, eg_t, gl_t):
        dr, rw = 0 if fwd else 1, rows[ch]
        o0_ref[dr, rw, :] = _collapse_heads(o0_)
        qe_ref[dr, rw, :] = q_ * e_ - _collapse_heads(aw_)
        a_ref[dr, rw, :] = _collapse_heads(a_)
        bm_ref[dr, rw, :] = _collapse_heads(b_)
        gam_ref[dr, rw, :] = jnp.exp(gl_)


def gdn_chunks(q, k, v, gb):
    b, l, _ = q.shape
    tm = ROW_TILE
    row = lambda i, t: (i, t, 0)
    out = pl.BlockSpec((2, None, tm, GDN_W), lambda i, t: (0, i, t, 0))
    return pl.pallas_call(
        _gdn_chunk_kernel,
        grid=(b, l // tm),
        in_specs=[pl.BlockSpec((None, tm, GDN_W), row)] * 3 + [pl.BlockSpec((None, tm, LANES), row)],
        out_specs=[out] * 5,
        out_shape=[jax.ShapeDtypeStruct((2, b, l, GDN_W), F32)] * 5,
        compiler_params=pltpu.CompilerParams(
            dimension_semantics=("parallel", "parallel"), vmem_limit_bytes=VMEM_LIMIT),
        name="gdn_chunk",
    )(q, k, v, gb)


def _gdn_scan_kernel(o0_ref, qe_ref, a_ref, bm_ref, gam_ref, gate_ref, g_ref, o_ref, s_ref, part_ref):
    dr = pl.program_id(0)
    s = pl.program_id(1)
    ns = pl.num_programs(1)
    nb, tm = o0_ref.shape[0], o0_ref.shape[1]
    cs = GDN_CHUNK
    n = GDN_W
    nch = tm // cs
    bs = range(nb)
    tile = jnp.where(s == 0, 0, jnp.where(dr == 0, s, ns - s))
    r = lax.broadcasted_iota(jnp.int32, (n, n), 0)
    c = lax.broadcasted_iota(jnp.int32, (n, n), 1)
    head = r // cs == c // cs

    @pl.when(s == 0)
    def _():
        s_ref[...] = jnp.zeros_like(s_ref)

    def run(order):
        state = [s_ref[i] for i in bs]
        outs = [{} for _ in bs]
        for ch in order:
            rows = slice(ch * cs, (ch + 1) * cs)
            sb = [x.astype(BF16) for x in state]
            for i in bs:
                outs[i][ch] = o0_ref[i, rows, :] + _dot(qe_ref[i, rows, :].astype(BF16), sb[i])
            a_full = [jnp.where(head, _tile_heads(a_ref[i, rows, :].astype(BF16)), jnp.zeros((n, n), BF16))
                      for i in bs]
            state = [_tile_heads(gam_ref[i, rows, :]) * state[i] - _dot(a_full[i], sb[i])
                     + jnp.where(head, _tile_heads(bm_ref[i, rows, :]), 0.0) for i in bs]
        for i in bs:
            s_ref[i] = state[i]
        return [jnp.concatenate([outs[i][ch] for ch in range(nch)], axis=0) for i in bs]

    rows_out = pl.ds(pl.multiple_of(tile * tm, tm), tm)

    @pl.when(dr == 0)
    def _():
        for i, o in enumerate(run(range(nch))):
            part_ref[i, rows_out, :] = o

    @pl.when(dr == 1)
    def _():
        ones_bd = jnp.where(head, 1.0, 0.0).astype(BF16)
        for i, o_bwd in enumerate(run(range(nch - 1, -1, -1))):
            o = part_ref[i, rows_out, :] + o_bwd
            ms = _head_mean(o * o, ones_bd)
            o_ref[i] = (o * lax.rsqrt(ms + EPS) * g_ref[...] * _silu(gate_ref[i])).astype(o_ref.dtype)


def gdn_scan(o0, qe, a, bm, gam, pb, norm_g):
    _, b, l, _ = o0.shape
    tm = ROW_TILE
    ns = l // tm

    def tile_of(dr, s):
        return jnp.where(s == 0, 0, jnp.where(dr == 0, s, ns - s))

    per_dir = pl.BlockSpec((None, b, tm, GDN_W), lambda dr, s: (dr, 0, tile_of(dr, s), 0))
    gate_col = 3 * GDN_W // GDN_W
    return pl.pallas_call(
        _gdn_scan_kernel,
        grid=(2, ns),
        in_specs=[per_dir] * 5 + [
            pl.BlockSpec((b, tm, GDN_W), lambda dr, s: (0, tile_of(dr, s), gate_col)),
            pl.BlockSpec((1, GDN_W), lambda dr, s: (0, 0))],
        out_specs=pl.BlockSpec((b, tm, GDN_W), lambda dr, s: (0, jnp.where(dr == 0, 0, tile_of(dr, s)), 0)),
        out_shape=jax.ShapeDtypeStruct((b, l, GDN_W), BF16),
        scratch_shapes=[pltpu.VMEM((b, GDN_W, GDN_W), F32), pltpu.VMEM((b, l, GDN_W), F32)],
        compiler_params=pltpu.CompilerParams(
            dimension_semantics=("arbitrary", "arbitrary"), vmem_limit_bytes=VMEM_LIMIT),
        name="gdn_scan",
    )(o0, qe, a, bm, gam, pb, jnp.tile(norm_g, GDN_HEADS)[None, :])


def gdn_mixer(pb, conv_w, a_log, dt_bias, norm_g):
    q, k, v, gb = gdn_prep(pb, conv_w, a_log, dt_bias)
    o0, qe, a, bm, gam = gdn_chunks(q, k, v, gb)
    return gdn_scan(o0, qe, a, bm, gam, pb, norm_g)


def kernel(x, c, ctx, c_ctx, w_mod, b_mod, norm1, norm2, w_in, w_out, swa_sink, gdn_conv, gdn_a_log, gdn_dt_bias, gdn_norm, mla_q_norm, mla_kv_norm, mla_w_q_up, mla_w_kv_up, ret_log_decay, ret_norm, ffn_w_gate, ffn_w_up, ffn_w_down, moe_router, moe_w_gate, moe_w_up, moe_w_down, final_norm):
    b, n, d = x.shape
    depth = w_in.shape[0]
    cos_t, sin_t = rope_tables(n)
    h = jnp.concatenate([ctx, x], axis=1)
    cond = jnp.concatenate([jax.nn.silu(c_ctx)[None, :], jax.nn.silu(c)], axis=0)
    mods = jnp.einsum("bd,ldk->lbk", cond, w_mod, precision=lax.Precision.HIGHEST) + b_mod[:, None, :]
    mods = mods.reshape(depth, 1 + b, 6, d)
    mods = jnp.stack([jnp.broadcast_to(mods[:, :1], (depth, b, 6, d)), mods[:, 1:]], axis=2)
    mods = jnp.pad(mods, ((0, 0), (0, 0), (0, 0), (0, SUBLANES - 6), (0, 0)))
    w_in_all = build_in_weight(w_in)
    w_out_all = build_out_weight(w_out)
    ffn_wg, ffn_wu, ffn_wd = ffn_w_gate.astype(BF16), ffn_w_up.astype(BF16), ffn_w_down.astype(BF16)
    moe_wg, moe_wu, moe_wd = moe_w_gate.astype(BF16), moe_w_up.astype(BF16), moe_w_down.astype(BF16)
    for layer in range(depth):
        mod = mods[layer]
        mla_w = build_mla_weights(mla_q_norm[layer], mla_kv_norm[layer], mla_w_q_up[layer], mla_w_kv_up[layer])
        pa, pb, (mq, mk, mv), pd = norm_proj(h, mod, norm1[layer][None, :], w_in_all, layer, cos_t, sin_t, mla_w)
        mix_a = swa_mixer(pa, swa_sink[layer])
        mix_b = gdn_mixer(pb, gdn_conv[layer], gdn_a_log[layer], gdn_dt_bias[layer], gdn_norm[layer])
        mix_c = mla_attention(mq, mk, mv)
        mix_d = retention_mixer(pd, ret_log_decay[layer], ret_norm[layer])
        mixes = (mix_a, mix_b, mix_c, mix_d)
        i = layer // 2
        if layer % 2 == 0:
            h, v = out_proj(mixes, h, mod, norm2[layer][None, :], w_out_all, layer)
            h = dense_ffn(v, h, mod, i, ffn_wg, ffn_wu, ffn_wd)
        else:
            w_r = jnp.pad(moe_router[i], ((0, 0), (0, LANES - N_EXPERTS)))
            h, v, logits = out_proj(mixes, h, mod, norm2[layer][None, :], w_out_all, layer, w_r)
            if layer == depth - 1:
                return moe_ffn(v, logits, h, mod, i, moe_wg, moe_wu, moe_wd, final_norm[None, :])
            h = moe_ffn(v, logits, h, mod, i, moe_wg, moe_wu, moe_wd)
    return final_rms_norm(h, final_norm[None, :], CTX_LEN // ROW_TILE)
```

```python
import functools

import numpy as np
import jax
import jax.numpy as jnp
from jax import lax
from jax.experimental import pallas as pl
from jax.experimental.pallas import tpu as pltpu

GRID_W = 64
CTX_LEN = 256
HEAD_DIM = 64
ROPE_THETA = 10000.0
EPS = 1e-6
NEG_INF = -1e30

SWA_WINDOW = 128
GDN_HEADS = 4
GDN_DK = 64
GDN_CHUNK = 64
MLA_HEADS = 4
MLA_Q_RANK = 256
MLA_NOPE = 64
MLA_ROPE = 32
MLA_V = 64
RET_HEADS = 4
RET_DK = 64
D_FF = 3584
N_EXPERTS = 8
TOP_K = 2

LANES = 128
SUBLANES = 8
VMEM_LIMIT = 56 * 1024 * 1024

ROW_TILE = 256
FF_CHUNK = 512
MOE_TILE = 512
MOE_PART = 256
MOE_FF_CHUNK = 1792
A_W, B_W, C_W, D_W = 768, 1152, 512, 1024
A_ROT_W, C_ROT_W, D_ROT_W = 640, 128, 512
OFF_A = 0
OFF_B = OFF_A + A_W
OFF_C = OFF_B + B_W
OFF_D = OFF_C + C_W
OFF_AR = OFF_D + D_W
OFF_CR = OFF_AR + A_ROT_W
OFF_DR = OFF_CR + C_ROT_W
W_ALL = OFF_DR + D_ROT_W
ROPE_W = A_ROT_W + C_ROT_W + D_ROT_W

LOG2_E = float(np.log2(np.e))
F32 = jnp.float32
BF16 = jnp.bfloat16
NT_DIMS = (((1,), (1,)), ((), ()))
TN_DIMS = (((0,), (0,)), ((), ()))


def _rms(x):
    return x * lax.rsqrt(jnp.mean(x * x, axis=-1, keepdims=True) + EPS)


def _silu(x):
    return x * (1.0 / (1.0 + jnp.exp(-x)))


def _dot(a, b):
    return jnp.dot(a, b, preferred_element_type=F32)


def _dot_nt(a, b):
    return lax.dot_general(a, b, NT_DIMS, preferred_element_type=F32)


def _dot_tn(a, b):
    return lax.dot_general(a, b, TN_DIMS, preferred_element_type=F32)


WIDE = 2


def _sub_tile_specs(block, tiles_per_seq, index_of):
    def spec(k):
        return pl.BlockSpec(block, lambda j: index_of((WIDE * j + k) // tiles_per_seq, (WIDE * j + k) % tiles_per_seq))
    return [spec(k) for k in range(WIDE)]


def _sub_mod_specs(d, tiles_per_seq):
    return _sub_tile_specs((None, None, SUBLANES, d), tiles_per_seq, lambda b, t: (b, jnp.minimum(t, 1), 0, 0))


def _sub_rows(k):
    return slice(k * ROW_TILE, (k + 1) * ROW_TILE)


def _norm_proj_kernel(h_ref, *refs):
    mods, (g_ref, w_ref), tabs = refs[:WIDE], refs[WIDE:WIDE + 2], refs[WIDE + 2:3 * WIDE + 2]
    qn_ref, kvn_ref, wq_ref, wqr_ref, wk_ref, wv_ref = refs[3 * WIDE + 2:3 * WIDE + 8]
    a_ref, b_ref, mq_ref, mk_ref, mv_ref, d_ref = refs[3 * WIDE + 8:]
    y = _rms(h_ref[...]) * g_ref[...]
    u = jnp.concatenate([y[_sub_rows(k)] * (1.0 + m[1:2, :]) + m[0:1, :] for k, m in enumerate(mods)],
                        axis=0).astype(BF16)

    def mm(lo, width):
        return _dot(u, w_ref[:, lo:lo + width])

    a_main = mm(OFF_A, A_W)
    a_rot = mm(OFF_AR, A_ROT_W)
    b_ref[...] = mm(OFF_B, B_W)
    c_main = mm(OFF_C, C_W)
    c_rot = mm(OFF_CR, C_ROT_W)
    d_main = mm(OFF_D, D_W)
    d_rot = mm(OFF_DR, D_ROT_W)
    lo, hi = A_ROT_W, A_ROT_W + C_ROT_W
    a_ref[:, A_ROT_W:] = a_main[:, A_ROT_W:].astype(BF16)
    d_ref[:, D_ROT_W:] = d_main[:, D_ROT_W:]
    nq = (_rms(c_main[:, :MLA_Q_RANK]) * qn_ref[...]).astype(BF16)
    nkv = (_rms(c_main[:, MLA_Q_RANK:C_W - C_ROT_W]) * kvn_ref[...]).astype(BF16)
    q_main, q_rot = _dot(nq, wq_ref[...]), _dot(nq, wqr_ref[...])
    k_nope = _dot(nkv, wk_ref[...])
    mv_ref[...] = _dot(nkv, wv_ref[...]).astype(BF16)
    per_head = lambda x: jnp.concatenate([x] * MLA_HEADS, axis=1)
    for k in range(WIDE):
        cos_ref, sin_ref = tabs[2 * k], tabs[2 * k + 1]
        r = _sub_rows(k)
        a_ref[r, :A_ROT_W] = (a_main[r, :A_ROT_W] * cos_ref[:, :A_ROT_W] + a_rot[r] * sin_ref[:, :A_ROT_W]).astype(BF16)
        cos_c, sin_c = cos_ref[:, lo:hi], sin_ref[:, lo:hi]
        k_rope = c_main[r, C_W - C_ROT_W:] * cos_c + c_rot[r] * sin_c
        mq_ref[r, :] = (q_main[r] * per_head(cos_c) + q_rot[r] * per_head(sin_c)).astype(BF16)
        mk_ref[r, :] = (k_nope[r] + per_head(k_rope)).astype(BF16)
        d_ref[r, :D_ROT_W] = d_main[r, :D_ROT_W] * cos_ref[:, hi:] + d_rot[r] * sin_ref[:, hi:]


def _layer_weight_spec(w, layer):
    return pl.BlockSpec((None,) + w.shape[1:], lambda j: (layer, 0, 0), pipeline_mode=pl.Buffered(1))


def norm_proj(h, mod, gain, w, layer, cos_t, sin_t, mla_weights):
    b, l, d = h.shape
    tm = WIDE * ROW_TILE
    tps = l // ROW_TILE
    row = lambda j: (j, 0)
    const = lambda j: (0, 0)
    tab_specs = [s for s in _sub_tile_specs((ROW_TILE, ROPE_W), tps, lambda bi, t: (t, 0)) for _ in range(2)]
    outs = pl.pallas_call(
        _norm_proj_kernel,
        grid=(b * l // tm,),
        in_specs=[pl.BlockSpec((tm, d), row)] + _sub_mod_specs(d, tps) + [
            pl.BlockSpec((1, d), const),
            _layer_weight_spec(w, layer)] + tab_specs + [pl.BlockSpec(a.shape, const) for a in mla_weights],
        out_specs=[pl.BlockSpec((tm, A_W), row), pl.BlockSpec((tm, B_W), row), pl.BlockSpec((tm, 4 * LANES), row),
                   pl.BlockSpec((tm, 4 * LANES), row), pl.BlockSpec((tm, 2 * LANES), row), pl.BlockSpec((tm, D_W), row)],
        out_shape=[jax.ShapeDtypeStruct((b * l, A_W), BF16), jax.ShapeDtypeStruct((b * l, B_W), F32),
                   jax.ShapeDtypeStruct((b * l, 4 * LANES), BF16), jax.ShapeDtypeStruct((b * l, 4 * LANES), BF16),
                   jax.ShapeDtypeStruct((b * l, 2 * LANES), BF16), jax.ShapeDtypeStruct((b * l, D_W), F32)],
        compiler_params=pltpu.CompilerParams(dimension_semantics=("parallel",), vmem_limit_bytes=VMEM_LIMIT),
        name="norm_proj",
    )(h.reshape(b * l, d), *([mod] * WIDE), gain, w, *([cos_t, sin_t] * WIDE), *mla_weights)
    pa, pb, mq, mk, mv, pd = [o.reshape(b, l, -1) for o in outs]
    return pa, pb, (mq, mk, mv), pd


def _rot_cols(w, hd):
    x = w.reshape(w.shape[:-1] + (w.shape[-1] // hd, 4, hd // 4))
    x1, x2, x3, x4 = x[..., 0, :], x[..., 1, :], x[..., 2, :], x[..., 3, :]
    return jnp.stack([-x2, x1, -x4, x3], axis=-2).reshape(w.shape)


def _place_swa_q(q):
    z = jnp.zeros(q.shape[:-1] + (HEAD_DIM,), q.dtype)
    blocks = []
    for h in range(4):
        qh = q[..., HEAD_DIM * h:HEAD_DIM * (h + 1)]
        blocks += [qh, z] if h // 2 == 0 else [z, qh]
    return jnp.concatenate(blocks, axis=-1)


def build_in_weight(w):
    o = [int(v) for v in np.cumsum((256, 128, 128, 768, 256, 16, 256, 128, 32, 256, 256, 256, 256))]
    aq, ak, av = w[..., :o[0]] * (HEAD_DIM ** -0.5 * LOG2_E), w[..., o[0]:o[1]], w[..., o[1]:o[2]]
    b_main, b_ab = w[..., o[2]:o[4]], w[..., o[4]:o[5]]
    c_q, c_kv, c_kr = w[..., o[5]:o[6]], w[..., o[6]:o[7]], w[..., o[7]:o[8]]
    dq, dk, dvg = w[..., o[8]:o[9]], w[..., o[9]:o[10]] * RET_DK ** -0.5, w[..., o[10]:]
    z = lambda n: jnp.zeros(w.shape[:-1] + (n,), w.dtype)
    parts = [
        _place_swa_q(aq), ak, av,
        b_main, b_ab, z(LANES - b_ab.shape[-1]),
        c_q, c_kv, z(64), c_kr, z(32),
        dq, dk, dvg,
        _place_swa_q(_rot_cols(aq, HEAD_DIM)), _rot_cols(ak, HEAD_DIM),
        z(64), _rot_cols(c_kr, MLA_ROPE), z(32),
        _rot_cols(dq, HEAD_DIM), _rot_cols(dk, HEAD_DIM),
    ]
    out = jnp.concatenate(parts, axis=-1)
    assert out.shape[-1] == W_ALL
    return out.astype(BF16)


def rope_tables(n):
    lat = jnp.arange(CTX_LEN + n, dtype=jnp.int32) - CTX_LEN
    grid_row = jnp.where(lat >= 0, lat // GRID_W, 0).astype(F32)[:, None]
    grid_col = jnp.where(lat >= 0, lat % GRID_W, 0).astype(F32)[:, None]
    narrow, col_of = [], {}
    for rot_dim in (HEAD_DIM, MLA_ROPE):
        n_freq = rot_dim // 4
        inv_freq = ROPE_THETA ** (-jnp.arange(n_freq, dtype=F32) / n_freq)
        for axis, pos in enumerate((grid_row, grid_col)):
            col_of[rot_dim, axis] = sum(a.shape[1] for a in narrow)
            narrow.append(pos * inv_freq)
    ang = jnp.concatenate(narrow, axis=1)
    identity_col = ang.shape[1]
    sel = np.zeros((identity_col + 1, ROPE_W), np.float32)

    def plan(lane0, width, group, rot_lo, rot_dim):
        for c in range(width):
            j = c % group - rot_lo
            if 0 <= j < rot_dim:
                quarter, f = divmod(j, rot_dim // 4)
                sel[col_of[rot_dim, quarter // 2] + f, lane0 + c] = 1.0
            else:
                sel[identity_col, lane0 + c] = 1.0

    plan(0, A_ROT_W, HEAD_DIM, 0, HEAD_DIM)
    plan(A_ROT_W, C_ROT_W, C_ROT_W, 64, MLA_ROPE)
    plan(A_ROT_W + C_ROT_W, D_ROT_W, HEAD_DIM, 0, HEAD_DIM)
    spread = lambda t: jnp.dot(t, jnp.asarray(sel), precision=lax.Precision.HIGHEST)
    ones, zeros = jnp.ones_like(grid_row), jnp.zeros_like(grid_row)
    return (spread(jnp.concatenate([jnp.cos(ang), ones], axis=1)),
            spread(jnp.concatenate([jnp.sin(ang), zeros], axis=1)))


def _swa_kernel(sink_ref, q_ref, kp_ref, ko_ref, kn_ref, kc_ref, vp_ref, vo_ref, vn_ref, vc_ref, o_ref):
    t = pl.program_id(1)
    last = pl.num_programs(1) - 1
    tq = q_ref.shape[0]
    half = tq // 2

    def attend(k, v, mask):
        v_lane = lax.broadcasted_iota(jnp.int32, v.shape, 1)
        v_ones = [jnp.where(v_lane >= HEAD_DIM, jnp.ones_like(v), v), jnp.where(v_lane < HEAD_DIM, jnp.ones_like(v), v)]
        heads = range(4)
        s = [_dot_nt(q_ref[:, LANES * h:LANES * (h + 1)], k) for h in heads]
        if mask is not None:
            s = [jnp.where(mask, x, NEG_INF) for x in s]
        sink = [sink_ref[h] * LOG2_E for h in heads]
        m = [jnp.maximum(s[h].max(axis=-1, keepdims=True), sink[h]) for h in heads]
        p = [jnp.exp2(s[h] - m[h]).astype(BF16) for h in heads]
        o = [_dot(p[h], v_ones[h // 2]) for h in heads]
        den_lane = [HEAD_DIM if h // 2 == 0 else 0 for h in heads]
        outs = [o[h] / (o[h][:, den_lane[h]:den_lane[h] + 1] + jnp.exp2(sink[h] - m[h])) for h in heads]
        lane = lax.broadcasted_iota(jnp.int32, (tq, LANES), 1)
        for r in range(2):
            o_ref[:, LANES * r:LANES * (r + 1)] = jnp.where(lane < HEAD_DIM, outs[r], outs[2 + r]).astype(o_ref.dtype)

    @pl.when(t == 0)
    def _():
        attend(kc_ref[...], vc_ref[...], None)

    @pl.when(t > 0)
    def _():
        band = 2 * half + tq
        qi = lax.broadcasted_iota(jnp.int32, (tq, band + CTX_LEN), 0)
        col = lax.broadcasted_iota(jnp.int32, (tq, band + CTX_LEN), 1)
        in_window = jnp.abs(col - half - qi) <= SWA_WINDOW
        exists = ((col >= half) | (t > 1)) & ((col < half + tq) | (t < last))
        mask = (col >= band) | (in_window & exists)
        attend(jnp.concatenate([kp_ref[...], ko_ref[...], kn_ref[...], kc_ref[...]], axis=0),
               jnp.concatenate([vp_ref[...], vo_ref[...], vn_ref[...], vc_ref[...]], axis=0), mask)


def swa_mixer(pa, sink):
    b, l, _ = pa.shape
    tq = ROW_TILE
    nblk = l // SWA_WINDOW
    kcol, vcol = 4, 5
    prev = lambda c: (lambda i, t: (i, jnp.maximum(2 * t - 1, 2), c))
    nxt = lambda c: (lambda i, t: (i, jnp.minimum(2 * t + 2, nblk - 1), c))
    own = lambda c: (lambda i, t: (i, t, c))
    ctx = lambda c: (lambda i, t: (i, 0, c))
    kv_specs = lambda c: [pl.BlockSpec((None, SWA_WINDOW, LANES), prev(c)), pl.BlockSpec((None, tq, LANES), own(c)),
                          pl.BlockSpec((None, SWA_WINDOW, LANES), nxt(c)), pl.BlockSpec((None, tq, LANES), ctx(c))]
    return pl.pallas_call(
        _swa_kernel,
        grid=(b, l // tq),
        in_specs=[pl.BlockSpec(memory_space=pltpu.SMEM),
                  pl.BlockSpec((None, tq, 4 * LANES), lambda i, t: (i, t, 0))] + kv_specs(kcol) + kv_specs(vcol),
        out_specs=pl.BlockSpec((None, tq, 2 * LANES), lambda i, t: (i, t, 0)),
        out_shape=jax.ShapeDtypeStruct((b, l, 2 * LANES), BF16),
        compiler_params=pltpu.CompilerParams(
            dimension_semantics=("parallel", "parallel"), vmem_limit_bytes=VMEM_LIMIT),
        name="swa",
    )(sink, pa, pa, pa, pa, pa, pa, pa, pa, pa)


def build_mla_weights(q_norm, kv_norm, w_q_up, w_kv_up):
    scale = (MLA_NOPE + MLA_ROPE) ** -0.5 * LOG2_E
    wq = (w_q_up * scale).reshape(-1, MLA_HEADS, MLA_NOPE + MLA_ROPE)
    zq = jnp.zeros(wq.shape[:2] + (LANES - MLA_NOPE - MLA_ROPE,), F32)
    wq_main = jnp.concatenate([wq, zq], axis=-1).reshape(-1, MLA_HEADS * LANES)
    wq_rot = jnp.concatenate([jnp.zeros_like(wq[..., :MLA_NOPE]), _rot_cols(wq[..., MLA_NOPE:], MLA_ROPE), zq],
                             axis=-1).reshape(-1, MLA_HEADS * LANES)
    wkv = w_kv_up.reshape(-1, MLA_HEADS, MLA_NOPE + MLA_V)
    wk = jnp.concatenate([wkv[..., :MLA_NOPE], jnp.zeros_like(wkv[..., :LANES - MLA_NOPE])],
                         axis=-1).reshape(-1, MLA_HEADS * LANES)
    wv = wkv[..., MLA_NOPE:].reshape(-1, MLA_HEADS * MLA_V)
    return [q_norm[None, :], kv_norm[None, :], wq_main.astype(BF16), wq_rot.astype(BF16), wk.astype(BF16),
            wv.astype(BF16)]


def _mla_attn_kernel(q_ref, k_ref, v_ref, o_ref):
    t = pl.program_id(2)
    tq = q_ref.shape[0]

    def attend(nk):
        v = v_ref[0:nk, :]
        v_lane = lax.broadcasted_iota(jnp.int32, v.shape, 1)
        v_ones = [jnp.where(v_lane >= MLA_V, jnp.ones_like(v), v), jnp.where(v_lane < MLA_V, jnp.ones_like(v), v)]
        pair = range(2)
        s = [_dot_nt(q_ref[:, LANES * j:LANES * (j + 1)], k_ref[0:nk, LANES * j:LANES * (j + 1)]) for j in pair]
        p = [jnp.exp2(s[j] - s[j].max(axis=-1, keepdims=True)).astype(BF16) for j in pair]
        o = [_dot(p[j], v_ones[j]) for j in pair]
        outs = [o[j] / o[j][:, (MLA_V, 0)[j]:(MLA_V, 0)[j] + 1] for j in pair]
        lane = lax.broadcasted_iota(jnp.int32, (tq, LANES), 1)
        o_ref[...] = jnp.where(lane < MLA_V, outs[0], outs[1]).astype(o_ref.dtype)

    @pl.when(t == 0)
    def _():
        attend(CTX_LEN)

    @pl.when(t > 0)
    def _():
        attend(k_ref.shape[0])


def mla_attention(q, k, v):
    b, l, _ = q.shape
    tq = ROW_TILE
    return pl.pallas_call(
        _mla_attn_kernel,
        grid=(b, 2, l // tq),
        in_specs=[pl.BlockSpec((None, tq, 2 * LANES), lambda i, p, t: (i, t, p)),
                  pl.BlockSpec((None, l, 2 * LANES), lambda i, p, t: (i, 0, p)),
                  pl.BlockSpec((None, l, LANES), lambda i, p, t: (i, 0, p))],
        out_specs=pl.BlockSpec((None, tq, LANES), lambda i, p, t: (i, t, p)),
        out_shape=jax.ShapeDtypeStruct((b, l, 2 * LANES), BF16),
        compiler_params=pltpu.CompilerParams(
            dimension_semantics=("parallel", "parallel", "parallel"), vmem_limit_bytes=VMEM_LIMIT),
        name="mla_attn",
    )(q, k, v)


def _head_mean(x, ones_bd):
    hi = x.astype(BF16)
    lo = (x - hi.astype(F32)).astype(BF16)
    return (_dot(hi, ones_bd) + _dot(lo, ones_bd)) * (1.0 / HEAD_DIM)


def _ret_kernel(x_ref, lg_ref, g_ref, o_ref, s_ref, dec_ref, part_ref):
    dr = pl.program_id(0)
    s = pl.program_id(1)
    ns = pl.num_programs(1)
    nb, c = x_ref.shape[0], x_ref.shape[1]
    w = RET_HEADS * HEAD_DIM
    chunk = jnp.where(s == 0, 0, jnp.where(dr == 0, s, ns - s))
    lg = lg_ref[...]
    fwd = dr == 0
    row_h = lax.broadcasted_iota(jnp.int32, (w, w), 0) // HEAD_DIM
    col_h = lax.broadcasted_iota(jnp.int32, (w, w), 1) // HEAD_DIM
    same_head = row_h == col_h

    @pl.when(s == 0)
    def _():
        s_ref[...] = jnp.zeros_like(s_ref)
        i = lax.broadcasted_iota(jnp.int32, (c, c), 0)
        j = lax.broadcasted_iota(jnp.int32, (c, c), 1)
        rel = jnp.where(fwd, i - j, j - i)
        relf = jnp.maximum(rel, 0).astype(F32)
        for h in range(RET_HEADS):
            lg_h = lg_ref[0:1, HEAD_DIM * h:HEAD_DIM * h + 1]
            dec_ref[h] = jnp.where(rel >= 0, jnp.exp(lg_h * relf), 0.0)

    pos = lax.broadcasted_iota(jnp.int32, (c, 1), 0).astype(F32)
    q_dec = jnp.exp(lg * jnp.where(fwd, pos + 1.0, c - pos))
    k_dec = jnp.exp(lg * jnp.where(fwd, c - 1.0 - pos, pos))
    lane_h = lax.broadcasted_iota(jnp.int32, (c, w), 1) // HEAD_DIM
    bs = range(nb)
    q = [x_ref[i, :, 0:w] for i in bs]
    kf = [x_ref[i, :, w:2 * w] for i in bs]
    v = [x_ref[i, :, 2 * w:3 * w].astype(BF16) for i in bs]
    kb = [x.astype(BF16) for x in kf]
    acc = [_dot((q[i] * q_dec).astype(BF16), s_ref[i].astype(BF16)) for i in bs]
    for h in range(RET_HEADS):
        qh = [jnp.where(lane_h == h, q[i], 0.0).astype(BF16) for i in bs]
        a = [(_dot_nt(qh[i], kb[i]) * dec_ref[h]).astype(BF16) for i in bs]
        acc = [acc[i] + jnp.where(lane_h == h, _dot(a[i], v[i]), 0.0) for i in bs]
    kv = [_dot_tn((kf[i] * k_dec).astype(BF16), v[i]) for i in bs]
    chunk_dec = jnp.exp(lg * float(c))
    for i in bs:
        s_ref[i] = s_ref[i] * chunk_dec + jnp.where(same_head, kv[i], 0.0)

    rows = pl.ds(pl.multiple_of(chunk * c, c), c)

    @pl.when(dr == 0)
    def _():
        for i in bs:
            part_ref[i, rows, :] = acc[i]

    @pl.when(dr == 1)
    def _():
        ones_bd = jnp.where(same_head, 1.0, 0.0).astype(BF16)
        for i in bs:
            o = part_ref[i, rows, :] + acc[i]
            mu = _head_mean(o, ones_bd)
            var = _head_mean(jnp.square(o - mu), ones_bd)
            y = (o - mu) * lax.rsqrt(var + EPS) * g_ref[...]
            o_ref[i] = (y * _silu(x_ref[i, :, 3 * w:4 * w])).astype(o_ref.dtype)


def retention_mixer(pd, log_decay, norm_g):
    b, l, _ = pd.shape
    c = ROW_TILE
    ns = l // c
    w = RET_HEADS * HEAD_DIM
    lg = jnp.repeat(-jnp.exp(log_decay.astype(F32)), HEAD_DIM, axis=-1)[:, None, :]

    def chunk_of(dr, s):
        return jnp.where(s == 0, 0, jnp.where(dr == 0, s, ns - s))

    return pl.pallas_call(
        _ret_kernel,
        grid=(2, ns),
        in_specs=[pl.BlockSpec((b, c, D_W), lambda dr, s: (0, chunk_of(dr, s), 0)),
                  pl.BlockSpec((None, 1, w), lambda dr, s: (dr, 0, 0)),
                  pl.BlockSpec((1, w), lambda dr, s: (0, 0))],
        out_specs=pl.BlockSpec((b, c, w), lambda dr, s: (0, jnp.where(dr == 0, 0, chunk_of(dr, s)), 0)),
        out_shape=jax.ShapeDtypeStruct((b, l, w), BF16),
        scratch_shapes=[pltpu.VMEM((b, w, w), F32), pltpu.VMEM((RET_HEADS, c, c), F32), pltpu.VMEM((b, l, w), F32)],
        compiler_params=pltpu.CompilerParams(
            dimension_semantics=("arbitrary", "arbitrary"), vmem_limit_bytes=VMEM_LIMIT),
        name="retention",
    )(pd, lg, norm_g[None, :])


def _out_proj_kernel(ma_ref, mb_ref, mc_ref, md_ref, h_ref, *refs, with_router):
    mods, (g_ref, w_ref), rest = refs[:WIDE], refs[WIDE:WIDE + 2], refs[WIDE + 2:]
    if with_router:
        wr_ref, hn_ref, v_ref, lg_ref = rest
    else:
        hn_ref, v_ref = rest
    gw = 2 * LANES
    mix = functools.reduce(jnp.add, [
        _dot(m_ref[...].astype(BF16), w_ref[gw * i:gw * (i + 1), :])
        for i, m_ref in enumerate((ma_ref, mb_ref, mc_ref, md_ref))])
    for k, mod_ref in enumerate(mods):
        r = _sub_rows(k)
        hn = h_ref[r, :] + mod_ref[2:3, :] * mix[r]
        hn_ref[r, :] = hn
        v = _rms(hn) * g_ref[...] * (1.0 + mod_ref[4:5, :]) + mod_ref[3:4, :]
        v_ref[r, :] = v.astype(v_ref.dtype)
        if with_router:
            lg_ref[r, :] = jnp.dot(v, wr_ref[...], preferred_element_type=F32, precision=lax.Precision.HIGHEST)


def out_proj(mixes, h, mod, gain, w, layer, w_router=None):
    b, l, d = h.shape
    tm = WIDE * ROW_TILE
    n = b * l
    with_router = w_router is not None
    row = lambda j: (j, 0)
    const = lambda j: (0, 0)
    in_specs = [pl.BlockSpec((tm, 2 * LANES), row) for _ in mixes] + [pl.BlockSpec((tm, d), row)] + _sub_mod_specs(
        d, l // ROW_TILE) + [pl.BlockSpec((1, d), const), _layer_weight_spec(w, layer)]
    out_specs = [pl.BlockSpec((tm, d), row), pl.BlockSpec((tm, d), row)]
    out_shape = [jax.ShapeDtypeStruct((n, d), F32), jax.ShapeDtypeStruct((n, d), F32 if with_router else BF16)]
    args = [m.reshape(n, 2 * LANES) for m in mixes] + [h.reshape(n, d)] + [mod] * WIDE + [gain, w]
    if with_router:
        in_specs.append(pl.BlockSpec(w_router.shape, const))
        out_specs.append(pl.BlockSpec((tm, LANES), row))
        out_shape.append(jax.ShapeDtypeStruct((n, LANES), F32))
        args.append(w_router)
    outs = pl.pallas_call(
        functools.partial(_out_proj_kernel, with_router=with_router),
        grid=(n // tm,),
        in_specs=in_specs,
        out_specs=out_specs,
        out_shape=out_shape,
        compiler_params=pltpu.CompilerParams(dimension_semantics=("parallel",), vmem_limit_bytes=VMEM_LIMIT),
        name="out_proj",
    )(*args)
    return [o.reshape(b, l, -1) for o in outs]


def build_out_weight(w):
    hd = HEAD_DIM
    rows = lambda lo, hi: w[..., lo:hi, :]
    return jnp.concatenate([rows(0, hd), rows(2 * hd, 3 * hd), rows(hd, 2 * hd), rows(3 * hd, None)],
                           axis=-2).astype(BF16)


def _ffn_kernel(v_ref, h_ref, *refs):
    mods, (wg_ref, wu_ref, wd_ref, o_ref) = refs[:WIDE], refs[WIDE:]
    v = v_ref[...]
    acc = jnp.zeros(o_ref.shape, F32)
    for j in range(D_FF // FF_CHUNK):
        cols = slice(j * FF_CHUNK, (j + 1) * FF_CHUNK)
        a = _dot(v, wg_ref[:, cols])
        u = _dot(v, wu_ref[:, cols])
        mid = (_silu(a) * u).astype(BF16)
        acc = acc + _dot(mid, wd_ref[cols, :])
    for k, mod_ref in enumerate(mods):
        r = _sub_rows(k)
        o_ref[r, :] = h_ref[r, :] + mod_ref[5:6, :] * acc[r]


def dense_ffn(v, h, mod, layer, wg, wu, wd):
    b, l, d = h.shape
    tm = WIDE * ROW_TILE
    n = b * l
    row = lambda j: (j, 0)
    const = lambda j: (0, 0)
    return pl.pallas_call(
        _ffn_kernel,
        grid=(n // tm,),
        in_specs=[pl.BlockSpec((tm, d), row), pl.BlockSpec((tm, d), row)] + _sub_mod_specs(d, l // ROW_TILE) + [
            _layer_weight_spec(wg, layer), _layer_weight_spec(wu, layer), _layer_weight_spec(wd, layer)],
        out_specs=pl.BlockSpec((tm, d), row),
        out_shape=jax.ShapeDtypeStruct((n, d), F32),
        compiler_params=pltpu.CompilerParams(dimension_semantics=("parallel",), vmem_limit_bytes=VMEM_LIMIT),
        name="dense_ffn",
    )(v.reshape(n, d), h.reshape(n, d), *([mod] * WIDE), wg, wu, wd).reshape(b, l, d)


def _moe_kernel(wt_ref, we_ref, lo_ref, hi_ref, first_ref, x_ref, wg_ref, wu_ref, wd_ref, o_ref, xm_ref, acc_ref):
    w = pl.program_id(0)
    j = pl.program_id(1)
    nj = pl.num_programs(1)
    tm = x_ref.shape[0]

    @pl.when(j == 0)
    def _():
        row = wt_ref[w] * tm + lax.broadcasted_iota(jnp.int32, (tm, 1), 0)
        keep = (row >= lo_ref[w]) & (row < hi_ref[w])
        xm_ref[...] = jnp.where(keep, x_ref[...], 0.0).astype(BF16)

    @pl.when((j == 0) & (first_ref[w] > 0))
    def _():
        acc_ref[...] = jnp.zeros_like(acc_ref)

    def swiglu_rows(rows):
        x = xm_ref[rows, :]
        mid = (_silu(_dot(x, wg_ref[...])) * _dot(x, wu_ref[...])).astype(BF16)
        acc_ref[rows, :] += _dot(mid, wd_ref[...])

    tile_lo = wt_ref[w] * tm
    whole = (lo_ref[w] <= tile_lo) & (hi_ref[w] >= tile_lo + tm)

    @pl.when(whole)
    def _():
        swiglu_rows(slice(0, tm))

    for part in range(tm // MOE_PART):
        part_lo = tile_lo + part * MOE_PART

        @pl.when(jnp.logical_not(whole) & (hi_ref[w] > jnp.maximum(part_lo, lo_ref[w]))
                 & (lo_ref[w] < part_lo + MOE_PART))
        def _():
            swiglu_rows(slice(part * MOE_PART, (part + 1) * MOE_PART))

    @pl.when(j == nj - 1)
    def _():
        o_ref[...] = acc_ref[...]


def moe_grouped_ffn(xs, items, layer_idx, wg, wu, wd):
    s, d = xs.shape
    tm = MOE_TILE
    fc = MOE_FF_CHUNK
    nw = items[0].shape[0]
    nj = D_FF // fc
    grid_spec = pltpu.PrefetchScalarGridSpec(
        num_scalar_prefetch=5,
        grid=(nw, nj),
        in_specs=[
            pl.BlockSpec((tm, d), lambda w, j, wt, we, lo, hi, fi: (wt[w], 0)),
            pl.BlockSpec((None, None, d, fc), lambda w, j, wt, we, lo, hi, fi: (layer_idx, we[w], 0, j)),
            pl.BlockSpec((None, None, d, fc), lambda w, j, wt, we, lo, hi, fi: (layer_idx, we[w], 0, j)),
            pl.BlockSpec((None, None, fc, d), lambda w, j, wt, we, lo, hi, fi: (layer_idx, we[w], j, 0)),
        ],
        out_specs=pl.BlockSpec((tm, d), lambda w, j, wt, we, lo, hi, fi: (wt[w], 0)),
        scratch_shapes=[pltpu.VMEM((tm, d), BF16), pltpu.VMEM((tm, d), F32)],
    )
    return pl.pallas_call(
        _moe_kernel,
        grid_spec=grid_spec,
        out_shape=jax.ShapeDtypeStruct((s, d), F32),
        compiler_params=pltpu.CompilerParams(
            dimension_semantics=("arbitrary", "arbitrary"), vmem_limit_bytes=VMEM_LIMIT),
        name="moe_ffn",
    )(*items, xs, wg, wu, wd)


def _residual_kernel(h_ref, f0_ref, f1_ref, gate_ref, mod_ref, *rest):
    f = gate_ref[:, 0:1] * f0_ref[...] + gate_ref[:, 1:2] * f1_ref[...]
    hn = h_ref[...] + mod_ref[5:6, :] * f
    if len(rest) == 2:
        gain_ref, o_ref = rest
        o_ref[...] = _rms(hn) * gain_ref[...]
    else:
        rest[0][...] = hn


def gated_residual(h, f0, f1, gates, mod, final_gain=None):
    b, l, d = h.shape
    tm = ROW_TILE
    skip = 0 if final_gain is None else CTX_LEN // tm
    row = lambda i, t: (i, t + skip, 0)
    in_specs = [pl.BlockSpec((None, tm, d), row), pl.BlockSpec((None, tm, d), row), pl.BlockSpec((None, tm, d), row),
                pl.BlockSpec((None, tm, LANES), row),
                pl.BlockSpec((None, None, SUBLANES, d), lambda i, t: (i, jnp.minimum(t + skip, 1), 0, 0))]
    args = [h, f0, f1, gates, mod]
    if final_gain is not None:
        in_specs.append(pl.BlockSpec((1, d), lambda i, t: (0, 0)))
        args.append(final_gain)
    return pl.pallas_call(
        _residual_kernel,
        grid=(b, l // tm - skip),
        in_specs=in_specs,
        out_specs=pl.BlockSpec((None, tm, d), lambda i, t: (i, t, 0)),
        out_shape=jax.ShapeDtypeStruct((b, l - skip * tm, d), F32),
        compiler_params=pltpu.CompilerParams(dimension_semantics=("parallel", "parallel")),
        name="gated_residual",
    )(*args)


def moe_ffn(v, logits, h, mod, layer_idx, wg, wu, wd, final_gain=None):
    b, l, d = h.shape
    t = b * l
    s = TOP_K * t
    tm = MOE_TILE
    nt = s // tm
    nw = nt + N_EXPERTS - 1
    i32 = jnp.int32
    lg = logits.reshape(t, LANES)[:, :N_EXPERTS]
    top_val, top_idx = lax.top_k(lg, TOP_K)
    gates = jax.nn.softmax(top_val, axis=-1)
    slot = jnp.arange(s, dtype=i32)
    skey = jnp.sort(top_idx.reshape(-1).astype(i32) * s + slot)
    order = skey % s
    _, inv = lax.sort_key_val(order, slot)
    bounds = (jnp.arange(N_EXPERTS, dtype=i32) + 1) * s
    cum = jnp.sum((skey[None, :] < bounds[:, None]).astype(i32), axis=1)
    cum_prev = jnp.concatenate([jnp.zeros((1,), i32), cum[:-1]])
    tile_lo = jnp.arange(nt, dtype=i32) * tm
    count_le = lambda edges, x: jnp.sum((edges[None, :] <= x[:, None]).astype(i32), axis=1)
    e_first = count_le(cum, tile_lo)
    e_last = count_le(cum, tile_lo + tm - 1)
    n_items = e_last - e_first + 1
    item_end = jnp.cumsum(n_items)
    item_start = item_end - n_items
    w = jnp.arange(nw, dtype=i32)
    wt = jnp.minimum(count_le(item_end, w), nt - 1)
    valid = w < item_end[-1]
    we = jnp.clip(e_first[wt] + w - item_start[wt], 0, N_EXPERTS - 1).astype(i32)
    lo = jnp.where(valid, cum_prev[we], 0).astype(i32)
    hi = jnp.where(valid, cum[we], 0).astype(i32)
    first = (valid & (w == item_start[wt])).astype(i32)
    rows_of = lambda a, idx: a.at[idx].get(mode="promise_in_bounds")
    xs = rows_of(v.reshape(t, d), order // TOP_K)
    ys = moe_grouped_ffn(xs, (wt, we, lo, hi, first), layer_idx, wg, wu, wd)
    dest = inv.reshape(t, TOP_K)
    f0 = rows_of(ys, dest[:, 0]).reshape(b, l, d)
    f1 = rows_of(ys, dest[:, 1]).reshape(b, l, d)
    gates_p = jnp.pad(gates, ((0, 0), (0, LANES - TOP_K))).reshape(b, l, LANES)
    return gated_residual(h, f0, f1, gates_p, mod, final_gain)


def _final_norm_kernel(h_ref, g_ref, o_ref):
    o_ref[...] = _rms(h_ref[...]) * g_ref[...]


def final_rms_norm(h, gain, n_ctx_tiles):
    b, l, d = h.shape
    tm = ROW_TILE
    n = l - n_ctx_tiles * tm
    return pl.pallas_call(
        _final_norm_kernel,
        grid=(b, n // tm),
        in_specs=[
            pl.BlockSpec((None, tm, d), lambda i, t: (i, t + n_ctx_tiles, 0)),
            pl.BlockSpec((1, d), lambda i, t: (0, 0)),
        ],
        out_specs=pl.BlockSpec((None, tm, d), lambda i, t: (i, t, 0)),
        out_shape=jax.ShapeDtypeStruct((b, n, d), F32),
        compiler_params=pltpu.CompilerParams(dimension_semantics=("parallel", "parallel")),
        name="final_norm",
    )(h, gain)


GDN_W = GDN_HEADS * GDN_DK
GDN_CONV_K = 5
GDN_HALO = SUBLANES


def _split3(x):
    p0 = x.astype(BF16)
    r1 = x - p0.astype(F32)
    p1 = r1.astype(BF16)
    p2 = (r1 - p1.astype(F32)).astype(BF16)
    return p0, p1, p2


def _gdn_prep_kernel(x_ref, prev_ref, next_ref, cw_ref, par_ref, q_ref, k_ref, v_ref, gb_ref):
    t = pl.program_id(1)
    last = pl.num_programs(1) - 1
    tm = x_ref.shape[0]
    w3 = 3 * GDN_W
    has_prev = t > 1
    has_next = (t > 0) & (t < last)
    prev = jnp.where(has_prev, prev_ref[...], 0.0)
    nxt = jnp.where(has_next, next_ref[...], 0.0)
    xe = jnp.concatenate([prev, x_ref[:, :w3], nxt], axis=0)
    y = jnp.zeros((tm, w3), F32)
    for j in range(GDN_CONV_K):
        lo = GDN_HALO - GDN_CONV_K // 2 + j
        y = y + cw_ref[j:j + 1, :] * xe[lo:lo + tm, :]
    y = _silu(y)
    r = lax.broadcasted_iota(jnp.int32, (GDN_W, GDN_W), 0)
    c = lax.broadcasted_iota(jnp.int32, (GDN_W, GDN_W), 1)
    ones_bd = jnp.where(r // GDN_DK == c // GDN_DK, 1.0, 0.0).astype(BF16)

    def l2n(x):
        sq = x * x
        hi = sq.astype(BF16)
        lo = (sq - hi.astype(F32)).astype(BF16)
        return x * lax.rsqrt(_dot(hi, ones_bd) + _dot(lo, ones_bd) + EPS)

    q_ref[...] = l2n(y[:, :GDN_W]) * GDN_DK ** -0.5
    k_ref[...] = l2n(y[:, GDN_W:2 * GDN_W])
    v_ref[...] = y[:, 2 * GDN_W:]
    ab = x_ref[:, w3 + GDN_W:]
    lane = lax.broadcasted_iota(jnp.int32, ab.shape, 1)
    is_g = (lane % 8) < 4
    z = ab + par_ref[1:2, :]
    softplus = jnp.maximum(z, 0.0) + jnp.log1p(jnp.exp(-jnp.abs(z)))
    g = jnp.where(is_g, par_ref[0:1, :] * softplus, 0.0)
    beta = 1.0 / (1.0 + jnp.exp(-ab))
    i = lax.broadcasted_iota(jnp.int32, (tm, tm), 0)
    j = lax.broadcasted_iota(jnp.int32, (tm, tm), 1)
    same_chunk = i // GDN_CHUNK == j // GDN_CHUNK
    tri_f = jnp.where(same_chunk & (j <= i), 1.0, 0.0).astype(BF16)
    tri_b = jnp.where(same_chunk & (j >= i), 1.0, 0.0).astype(BF16)
    pieces = _split3(g)
    gc_f = functools.reduce(jnp.add, [_dot(tri_f, p) for p in pieces])
    gc_b = functools.reduce(jnp.add, [_dot(tri_b, p) for p in pieces])
    gb_ref[...] = jnp.where(is_g, jnp.where(lane < 8, gc_f, gc_b), beta)


def gdn_prep(pb, conv_w, a_log, dt_bias):
    b, l, _ = pb.shape
    tm = ROW_TILE
    w3 = 3 * GDN_W
    halo_blocks = tm // GDN_HALO
    n_halo = l // GDN_HALO
    cw = jnp.pad(conv_w, ((0, SUBLANES - GDN_CONV_K), (0, 0)))
    neg_a = jnp.pad(-jnp.exp(a_log.astype(F32)), ((0, 0), (0, 4))).reshape(-1)
    dtb = jnp.pad(dt_bias.astype(F32), ((0, 0), (0, 4))).reshape(-1)
    par = jnp.pad(jnp.stack([neg_a, dtb]), ((0, SUBLANES - 2), (0, LANES - 16)))
    row = lambda i, t: (i, t, 0)
    out = lambda w: pl.BlockSpec((None, tm, w), row)
    return pl.pallas_call(
        _gdn_prep_kernel,
        grid=(b, l // tm),
        in_specs=[pl.BlockSpec((None, tm, B_W), row),
                  pl.BlockSpec((None, GDN_HALO, w3), lambda i, t: (i, jnp.maximum(t * halo_blocks - 1, 0), 0)),
                  pl.BlockSpec((None, GDN_HALO, w3),
                               lambda i, t: (i, jnp.minimum((t + 1) * halo_blocks, n_halo - 1), 0)),
                  pl.BlockSpec(cw.shape, lambda i, t: (0, 0)),
                  pl.BlockSpec(par.shape, lambda i, t: (0, 0))],
        out_specs=[out(GDN_W), out(GDN_W), out(GDN_W), out(LANES)],
        out_shape=[jax.ShapeDtypeStruct((b, l, GDN_W), F32)] * 3 + [jax.ShapeDtypeStruct((b, l, LANES), F32)],
        compiler_params=pltpu.CompilerParams(
            dimension_semantics=("parallel", "parallel"), vmem_limit_bytes=VMEM_LIMIT),
        name="gdn_prep",
    )(pb, pb, pb, cw, par)


def _tile_heads(x):
    return jnp.concatenate([x] * GDN_HEADS, axis=0)


def _collapse_heads(x):
    c = GDN_CHUNK
    return x[0:c] + x[c:2 * c] + x[2 * c:3 * c] + x[3 * c:4 * c]


def _gdn_chunk_kernel(q_ref, k_ref, v_ref, gb_ref, o0_ref, qe_ref, a_ref, bm_ref, gam_ref):
    n = GDN_W
    cs = GDN_CHUNK
    r = lax.broadcasted_iota(jnp.int32, (n, n), 0)
    c = lax.broadcasted_iota(jnp.int32, (n, n), 1)
    ri, ci = r % cs, c % cs
    head = r // cs == c // cs
    eye = jnp.where(r == c, 1.0, 0.0)
    blk = lambda s: r // s == c // s
    b8, b16, b32 = blk(8), blk(16), blk(32)
    lane = lax.broadcasted_iota(jnp.int32, (n, LANES), 1)
    row_head = lax.broadcasted_iota(jnp.int32, (n, LANES), 0) // cs
    pick = lambda sel, x: jnp.sum(jnp.where(sel, x, 0.0), axis=1, keepdims=True)
    src_lane = lax.broadcasted_iota(jnp.int32, (LANES, n), 0)
    dst_head = lax.broadcasted_iota(jnp.int32, (LANES, n), 1) // cs
    tri, tri_strict, sel_g, sel_b, widen_g, widen_b = {}, {}, {}, {}, {}, {}
    for fwd in (True, False):
        ahead = ri - ci if fwd else ci - ri
        tri[fwd] = head & (ahead >= 0)
        tri_strict[fwd] = head & (ahead > 0)
        lane0 = 0 if fwd else 8
        sel_g[fwd] = lane == lane0 + row_head
        sel_b[fwd] = lane == lane0 + 4 + row_head
        widen_g[fwd] = jnp.where(src_lane == lane0 + dst_head, 1.0, 0.0).astype(BF16)
        widen_b[fwd] = jnp.where(src_lane == lane0 + 4 + dst_head, 1.0, 0.0).astype(BF16)

    n_chunks = q_ref.shape[0] // cs
    rows = [slice(ch * cs, (ch + 1) * cs) for ch in range(n_chunks)]
    items = [(fwd, ch) for fwd in (True, False) for ch in range(n_chunks)]
    dirs = [fwd for fwd, _ in items]
    per_item = lambda xs: [xs[ch] for _, ch in items]
    each = lambda f, *xs: [f(*a) for a in zip(*xs)]
    bf = lambda xs: [x.astype(BF16) for x in xs]
    widen = lambda x, e: functools.reduce(jnp.add, [_dot(p, e) for p in _split3(x)])
    zero_bf = jnp.zeros((n, n), BF16)
    spread = lambda xs: [jnp.where(head, _tile_heads(x.astype(BF16)), zero_bf) for x in xs]
    k_c, q_c, v_c = ([ref[rw, :] for rw in rows] for ref in (k_ref, q_ref, v_ref))
    khb_c, qhb_c = spread(k_c), spread(q_c)
    kk = per_item(each(_dot_nt, khb_c, khb_c))
    qk = per_item(each(_dot_nt, qhb_c, khb_c))
    k_t, q_t, v_t = per_item(k_c), per_item(q_c), per_item(v_c)
    gb = per_item([gb_ref[rw, :] for rw in rows])
    gb4 = [_tile_heads(x) for x in gb]
    gc = [pick(sel_g[f], x) for f, x in zip(dirs, gb4)]
    beta = [pick(sel_b[f], x) for f, x in zip(dirs, gb4)]
    gc_t = [widen(x, widen_g[f]) for f, x in zip(dirs, gb)]
    beta_t = [widen(x, widen_b[f]) for f, x in zip(dirs, gb)]
    ends = [cs - 1 if f else 0 for f in dirs]
    gl_t = [jnp.broadcast_to(x[e:e + 1, :], x.shape) for x, e in zip(gc_t, ends)]
    gc_b = [jnp.broadcast_to(x, (n, n)) for x in gc]
    decay = [jnp.exp(jnp.minimum(x - x.T, 0.0)) for x in gc_b]
    lmat = [jnp.where(tri_strict[f], b_ * kk_ * d_, 0.0) for f, b_, kk_, d_ in zip(dirs, beta, kk, decay)]
    attn = bf([jnp.where(tri[f], qk_ * d_, 0.0) for f, qk_, d_ in zip(dirs, qk, decay)])
    nl = bf([jnp.where(b8, -x, 0.0) for x in lmat])
    n2 = bf(each(_dot, nl, nl))
    n4 = each(_dot, n2, n2)
    p1 = bf(each(lambda a, b_: _dot((eye + a).astype(BF16), (eye + b_).astype(BF16)), nl, n2))
    tinv = each(lambda p, x: _dot(p, (eye + x).astype(BF16)), p1, n4)

    def moving_rows(fwd, x, sz):
        return jnp.concatenate([x[i:i + sz] for i in range(sz if fwd else 0, n, 2 * sz)], axis=0)

    def with_moving_rows(fwd, x, new, sz):
        pieces = []
        for j, i in enumerate(range(0, n, 2 * sz)):
            kept = x[i:i + sz] if fwd else x[i + sz:i + 2 * sz]
            moved = new[j * sz:(j + 1) * sz]
            pieces += [kept, moved] if fwd else [moved, kept]
        return jnp.concatenate(pieces, axis=0)

    for sz, inner, outer in ((8, b8, b16), (16, b16, b32), (32, b32, head)):
        off = bf([jnp.where(outer & ~inner, x, 0.0) for x in lmat])
        tb = bf(tinv)
        t_mv = [moving_rows(f, x, sz) for f, x in zip(dirs, tinv)]
        to = bf(each(_dot, bf(t_mv), off))
        tinv = [with_moving_rows(f, t_, tm_ - _dot(to_, tb_), sz)
                for f, t_, tm_, to_, tb_ in zip(dirs, tinv, t_mv, to, tb)]
    tb = bf(tinv)
    eg_t = [jnp.exp(x) for x in gc_t]
    u = bf(each(_dot, tb, spread(each(lambda b_, v_: b_ * v_, beta_t, v_t))))
    w = bf(each(_dot, tb, spread(each(lambda b_, e_, k_: (b_ * e_) * k_, beta_t, eg_t, k_t))))
    o0 = each(_dot, attn, u)
    aw = each(_dot, attn, w)
    kg = spread(each(lambda k_, gl_, gc_: k_ * jnp.exp(gl_ - gc_), k_t, gl_t, gc_t))
    a_mat = each(_dot_tn, kg, w)
    b_mat = each(_dot_tn, kg, u)
    for (fwd, ch), o0_, aw_, a_, b_, q_, e_, gl_ in zip(items, o0, aw, a_mat, b_mat, q_t, eg_t, gl_t):
        dr, rw = 0 if fwd else 1, rows[ch]
        o0_ref[dr, rw, :] = _collapse_heads(o0_)
        qe_ref[dr, rw, :] = q_ * e_ - _collapse_heads(aw_)
        a_ref[dr, rw, :] = _collapse_heads(a_)
        bm_ref[dr, rw, :] = _collapse_heads(b_)
        gam_ref[dr, rw, :] = jnp.exp(gl_)


def gdn_chunks(q, k, v, gb):
    b, l, _ = q.shape
    tm = ROW_TILE
    row = lambda i, t: (i, t, 0)
    out = pl.BlockSpec((2, None, tm, GDN_W), lambda i, t: (0, i, t, 0))
    return pl.pallas_call(
        _gdn_chunk_kernel,
        grid=(b, l // tm),
        in_specs=[pl.BlockSpec((None, tm, GDN_W), row)] * 3 + [pl.BlockSpec((None, tm, LANES), row)],
        out_specs=[out] * 5,
        out_shape=[jax.ShapeDtypeStruct((2, b, l, GDN_W), F32)] * 5,
        compiler_params=pltpu.CompilerParams(
            dimension_semantics=("parallel", "parallel"), vmem_limit_bytes=VMEM_LIMIT),
        name="gdn_chunk",
    )(q, k, v, gb)


def _gdn_scan_kernel(o0_ref, qe_ref, a_ref, bm_ref, gam_ref, gate_ref, g_ref, o_ref, s_ref, part_ref):
    dr = pl.program_id(0)
    s = pl.program_id(1)
    ns = pl.num_programs(1)
    nb, tm = o0_ref.shape[0], o0_ref.shape[1]
    cs = GDN_CHUNK
    n = GDN_W
    nch = tm // cs
    bs = range(nb)
    tile = jnp.where(s == 0, 0, jnp.where(dr == 0, s, ns - s))
    r = lax.broadcasted_iota(jnp.int32, (n, n), 0)
    c = lax.broadcasted_iota(jnp.int32, (n, n), 1)
    head = r // cs == c // cs

    @pl.when(s == 0)
    def _():
        s_ref[...] = jnp.zeros_like(s_ref)

    def run(order):
        state = [s_ref[i] for i in bs]
        outs = [{} for _ in bs]
        for ch in order:
            rows = slice(ch * cs, (ch + 1) * cs)
            sb = [x.astype(BF16) for x in state]
            for i in bs:
                outs[i][ch] = o0_ref[i, rows, :] + _dot(qe_ref[i, rows, :].astype(BF16), sb[i])
            a_full = [jnp.where(head, _tile_heads(a_ref[i, rows, :]), 0.0).astype(BF16) for i in bs]
            state = [_tile_heads(gam_ref[i, rows, :]) * state[i] - _dot(a_full[i], sb[i])
                     + jnp.where(head, _tile_heads(bm_ref[i, rows, :]), 0.0) for i in bs]
        for i in bs:
            s_ref[i] = state[i]
        return [jnp.concatenate([outs[i][ch] for ch in range(nch)], axis=0) for i in bs]

    rows_out = pl.ds(pl.multiple_of(tile * tm, tm), tm)

    @pl.when(dr == 0)
    def _():
        for i, o in enumerate(run(range(nch))):
            part_ref[i, rows_out, :] = o

    @pl.when(dr == 1)
    def _():
        ones_bd = jnp.where(head, 1.0, 0.0).astype(BF16)
        for i, o_bwd in enumerate(run(range(nch - 1, -1, -1))):
            o = part_ref[i, rows_out, :] + o_bwd
            ms = _head_mean(o * o, ones_bd)
            o_ref[i] = (o * lax.rsqrt(ms + EPS) * g_ref[...] * _silu(gate_ref[i])).astype(o_ref.dtype)


def gdn_scan(o0, qe, a, bm, gam, pb, norm_g):
    _, b, l, _ = o0.shape
    tm = ROW_TILE
    ns = l // tm

    def tile_of(dr, s):
        return jnp.where(s == 0, 0, jnp.where(dr == 0, s, ns - s))

    per_dir = pl.BlockSpec((None, b, tm, GDN_W), lambda dr, s: (dr, 0, tile_of(dr, s), 0))
    gate_col = 3 * GDN_W // GDN_W
    return pl.pallas_call(
        _gdn_scan_kernel,
        grid=(2, ns),
        in_specs=[per_dir] * 5 + [
            pl.BlockSpec((b, tm, GDN_W), lambda dr, s: (0, tile_of(dr, s), gate_col)),
            pl.BlockSpec((1, GDN_W), lambda dr, s: (0, 0))],
        out_specs=pl.BlockSpec((b, tm, GDN_W), lambda dr, s: (0, jnp.where(dr == 0, 0, tile_of(dr, s)), 0)),
        out_shape=jax.ShapeDtypeStruct((b, l, GDN_W), BF16),
        scratch_shapes=[pltpu.VMEM((b, GDN_W, GDN_W), F32), pltpu.VMEM((b, l, GDN_W), F32)],
        compiler_params=pltpu.CompilerParams(
            dimension_semantics=("arbitrary", "arbitrary"), vmem_limit_bytes=VMEM_LIMIT),
        name="gdn_scan",
    )(o0, qe, a, bm, gam, pb, jnp.tile(norm_g, GDN_HEADS)[None, :])


def gdn_mixer(pb, conv_w, a_log, dt_bias, norm_g):
    q, k, v, gb = gdn_prep(pb, conv_w, a_log, dt_bias)
    o0, qe, a, bm, gam = gdn_chunks(q, k, v, gb)
    return gdn_scan(o0, qe, a, bm, gam, pb, norm_g)


def kernel(x, c, ctx, c_ctx, w_mod, b_mod, norm1, norm2, w_in, w_out, swa_sink, gdn_conv, gdn_a_log, gdn_dt_bias, gdn_norm, mla_q_norm, mla_kv_norm, mla_w_q_up, mla_w_kv_up, ret_log_decay, ret_norm, ffn_w_gate, ffn_w_up, ffn_w_down, moe_router, moe_w_gate, moe_w_up, moe_w_down, final_norm):
    b, n, d = x.shape
    depth = w_in.shape[0]
    cos_t, sin_t = rope_tables(n)
    h = jnp.concatenate([ctx, x], axis=1)
    cond = jnp.concatenate([jax.nn.silu(c_ctx)[None, :], jax.nn.silu(c)], axis=0)
    mods = jnp.einsum("bd,ldk->lbk", cond, w_mod, precision=lax.Precision.HIGHEST) + b_mod[:, None, :]
    mods = mods.reshape(depth, 1 + b, 6, d)
    mods = jnp.stack([jnp.broadcast_to(mods[:, :1], (depth, b, 6, d)), mods[:, 1:]], axis=2)
    mods = jnp.pad(mods, ((0, 0), (0, 0), (0, 0), (0, SUBLANES - 6), (0, 0)))
    w_in_all = build_in_weight(w_in)
    w_out_all = build_out_weight(w_out)
    ffn_wg, ffn_wu, ffn_wd = ffn_w_gate.astype(BF16), ffn_w_up.astype(BF16), ffn_w_down.astype(BF16)
    moe_wg, moe_wu, moe_wd = moe_w_gate.astype(BF16), moe_w_up.astype(BF16), moe_w_down.astype(BF16)
    for layer in range(depth):
        mod = mods[layer]
        mla_w = build_mla_weights(mla_q_norm[layer], mla_kv_norm[layer], mla_w_q_up[layer], mla_w_kv_up[layer])
        pa, pb, (mq, mk, mv), pd = norm_proj(h, mod, norm1[layer][None, :], w_in_all, layer, cos_t, sin_t, mla_w)
        mix_a = swa_mixer(pa, swa_sink[layer])
        mix_b = gdn_mixer(pb, gdn_conv[layer], gdn_a_log[layer], gdn_dt_bias[layer], gdn_norm[layer])
        mix_c = mla_attention(mq, mk, mv)
        mix_d = retention_mixer(pd, ret_log_decay[layer], ret_norm[layer])
        mixes = (mix_a, mix_b, mix_c, mix_d)
        i = layer // 2
        if layer % 2 == 0:
            h, v = out_proj(mixes, h, mod, norm2[layer][None, :], w_out_all, layer)
            h = dense_ffn(v, h, mod, i, ffn_wg, ffn_wu, ffn_wd)
        else:
            w_r = jnp.pad(moe_router[i], ((0, 0), (0, LANES - N_EXPERTS)))
            h, v, logits = out_proj(mixes, h, mod, norm2[layer][None, :], w_out_all, layer, w_r)
            if layer == depth - 1:
                return moe_ffn(v, logits, h, mod, i, moe_wg, moe_wu, moe_wd, final_norm[None, :])
            h = moe_ffn(v, logits, h, mod, i, moe_wg, moe_wu, moe_wd)
    return final_rms_norm(h, final_norm[None, :], CTX_LEN // ROW_TILE)
```

```python
import functools

import numpy as np
import jax
import jax.numpy as jnp
from jax import lax
from jax.experimental import pallas as pl
from jax.experimental.pallas import tpu as pltpu

GRID_W = 64
CTX_LEN = 256
HEAD_DIM = 64
ROPE_THETA = 10000.0
EPS = 1e-6
NEG_INF = -1e30

SWA_WINDOW = 128
GDN_HEADS = 4
GDN_DK = 64
GDN_CHUNK = 64
MLA_HEADS = 4
MLA_Q_RANK = 256
MLA_NOPE = 64
MLA_ROPE = 32
MLA_V = 64
RET_HEADS = 4
RET_DK = 64
D_FF = 3584
N_EXPERTS = 8
TOP_K = 2

LANES = 128
SUBLANES = 8
VMEM_LIMIT = 56 * 1024 * 1024

ROW_TILE = 256
FF_CHUNK = 512
MOE_TILE = 512
MOE_PART = 256
MOE_FF_CHUNK = 1792
A_W, B_W, C_W, D_W = 768, 1152, 512, 1024
A_ROT_W, C_ROT_W, D_ROT_W = 640, 128, 512
OFF_A = 0
OFF_B = OFF_A + A_W
OFF_C = OFF_B + B_W
OFF_D = OFF_C + C_W
OFF_AR = OFF_D + D_W
OFF_CR = OFF_AR + A_ROT_W
OFF_DR = OFF_CR + C_ROT_W
W_ALL = OFF_DR + D_ROT_W
ROPE_W = A_ROT_W + C_ROT_W + D_ROT_W

LOG2_E = float(np.log2(np.e))
F32 = jnp.float32
BF16 = jnp.bfloat16
NT_DIMS = (((1,), (1,)), ((), ()))
TN_DIMS = (((0,), (0,)), ((), ()))


def _rms(x):
    return x * lax.rsqrt(jnp.mean(x * x, axis=-1, keepdims=True) + EPS)


def _silu(x):
    return x * (1.0 / (1.0 + jnp.exp(-x)))


def _dot(a, b):
    return jnp.dot(a, b, preferred_element_type=F32)


def _dot_nt(a, b):
    return lax.dot_general(a, b, NT_DIMS, preferred_element_type=F32)


def _dot_tn(a, b):
    return lax.dot_general(a, b, TN_DIMS, preferred_element_type=F32)


WIDE = 2


def _sub_tile_specs(block, tiles_per_seq, index_of):
    def spec(k):
        return pl.BlockSpec(block, lambda j: index_of((WIDE * j + k) // tiles_per_seq, (WIDE * j + k) % tiles_per_seq))
    return [spec(k) for k in range(WIDE)]


def _sub_mod_specs(d, tiles_per_seq):
    return _sub_tile_specs((None, None, SUBLANES, d), tiles_per_seq, lambda b, t: (b, jnp.minimum(t, 1), 0, 0))


def _sub_rows(k):
    return slice(k * ROW_TILE, (k + 1) * ROW_TILE)


def _norm_proj_kernel(h_ref, *refs):
    mods, (g_ref, w_ref), tabs = refs[:WIDE], refs[WIDE:WIDE + 2], refs[WIDE + 2:3 * WIDE + 2]
    qn_ref, kvn_ref, wq_ref, wqr_ref, wk_ref, wv_ref = refs[3 * WIDE + 2:3 * WIDE + 8]
    a_ref, b_ref, mq_ref, mk_ref, mv_ref, d_ref = refs[3 * WIDE + 8:]
    y = _rms(h_ref[...]) * g_ref[...]
    u = jnp.concatenate([y[_sub_rows(k)] * (1.0 + m[1:2, :]) + m[0:1, :] for k, m in enumerate(mods)],
                        axis=0).astype(BF16)

    def mm(lo, width):
        return _dot(u, w_ref[:, lo:lo + width])

    a_main = mm(OFF_A, A_W)
    a_rot = mm(OFF_AR, A_ROT_W)
    b_ref[...] = mm(OFF_B, B_W)
    c_main = mm(OFF_C, C_W)
    c_rot = mm(OFF_CR, C_ROT_W)
    d_main = mm(OFF_D, D_W)
    d_rot = mm(OFF_DR, D_ROT_W)
    lo, hi = A_ROT_W, A_ROT_W + C_ROT_W
    a_ref[:, A_ROT_W:] = a_main[:, A_ROT_W:].astype(BF16)
    d_ref[:, D_ROT_W:] = d_main[:, D_ROT_W:]
    nq = (_rms(c_main[:, :MLA_Q_RANK]) * qn_ref[...]).astype(BF16)
    nkv = (_rms(c_main[:, MLA_Q_RANK:C_W - C_ROT_W]) * kvn_ref[...]).astype(BF16)
    q_main, q_rot = _dot(nq, wq_ref[...]), _dot(nq, wqr_ref[...])
    k_nope = _dot(nkv, wk_ref[...])
    mv_ref[...] = _dot(nkv, wv_ref[...]).astype(BF16)
    per_head = lambda x: jnp.concatenate([x] * MLA_HEADS, axis=1)
    for k in range(WIDE):
        cos_ref, sin_ref = tabs[2 * k], tabs[2 * k + 1]
        r = _sub_rows(k)
        a_ref[r, :A_ROT_W] = (a_main[r, :A_ROT_W] * cos_ref[:, :A_ROT_W] + a_rot[r] * sin_ref[:, :A_ROT_W]).astype(BF16)
        cos_c, sin_c = cos_ref[:, lo:hi], sin_ref[:, lo:hi]
        k_rope = c_main[r, C_W - C_ROT_W:] * cos_c + c_rot[r] * sin_c
        mq_ref[r, :] = (q_main[r] * per_head(cos_c) + q_rot[r] * per_head(sin_c)).astype(BF16)
        mk_ref[r, :] = (k_nope[r] + per_head(k_rope)).astype(BF16)
        d_ref[r, :D_ROT_W] = d_main[r, :D_ROT_W] * cos_ref[:, hi:] + d_rot[r] * sin_ref[:, hi:]


def _layer_weight_spec(w, layer):
    return pl.BlockSpec((None,) + w.shape[1:], lambda j: (layer, 0, 0), pipeline_mode=pl.Buffered(1))


def norm_proj(h, mod, gain, w, layer, cos_t, sin_t, mla_weights):
    b, l, d = h.shape
    tm = WIDE * ROW_TILE
    tps = l // ROW_TILE
    row = lambda j: (j, 0)
    const = lambda j: (0, 0)
    tab_specs = [s for s in _sub_tile_specs((ROW_TILE, ROPE_W), tps, lambda bi, t: (t, 0)) for _ in range(2)]
    outs = pl.pallas_call(
        _norm_proj_kernel,
        grid=(b * l // tm,),
        in_specs=[pl.BlockSpec((tm, d), row)] + _sub_mod_specs(d, tps) + [
            pl.BlockSpec((1, d), const),
            _layer_weight_spec(w, layer)] + tab_specs + [pl.BlockSpec(a.shape, const) for a in mla_weights],
        out_specs=[pl.BlockSpec((tm, A_W), row), pl.BlockSpec((tm, B_W), row), pl.BlockSpec((tm, 4 * LANES), row),
                   pl.BlockSpec((tm, 4 * LANES), row), pl.BlockSpec((tm, 2 * LANES), row), pl.BlockSpec((tm, D_W), row)],
        out_shape=[jax.ShapeDtypeStruct((b * l, A_W), BF16), jax.ShapeDtypeStruct((b * l, B_W), F32),
                   jax.ShapeDtypeStruct((b * l, 4 * LANES), BF16), jax.ShapeDtypeStruct((b * l, 4 * LANES), BF16),
                   jax.ShapeDtypeStruct((b * l, 2 * LANES), BF16), jax.ShapeDtypeStruct((b * l, D_W), F32)],
        compiler_params=pltpu.CompilerParams(dimension_semantics=("parallel",), vmem_limit_bytes=VMEM_LIMIT),
        name="norm_proj",
    )(h.reshape(b * l, d), *([mod] * WIDE), gain, w, *([cos_t, sin_t] * WIDE), *mla_weights)
    pa, pb, mq, mk, mv, pd = [o.reshape(b, l, -1) for o in outs]
    return pa, pb, (mq, mk, mv), pd


def _rot_cols(w, hd):
    x = w.reshape(w.shape[:-1] + (w.shape[-1] // hd, 4, hd // 4))
    x1, x2, x3, x4 = x[..., 0, :], x[..., 1, :], x[..., 2, :], x[..., 3, :]
    return jnp.stack([-x2, x1, -x4, x3], axis=-2).reshape(w.shape)


def _place_swa_q(q):
    z = jnp.zeros(q.shape[:-1] + (HEAD_DIM,), q.dtype)
    blocks = []
    for h in range(4):
        qh = q[..., HEAD_DIM * h:HEAD_DIM * (h + 1)]
        blocks += [qh, z] if h // 2 == 0 else [z, qh]
    return jnp.concatenate(blocks, axis=-1)


def build_in_weight(w):
    o = [int(v) for v in np.cumsum((256, 128, 128, 768, 256, 16, 256, 128, 32, 256, 256, 256, 256))]
    aq, ak, av = w[..., :o[0]] * (HEAD_DIM ** -0.5 * LOG2_E), w[..., o[0]:o[1]], w[..., o[1]:o[2]]
    b_main, b_ab = w[..., o[2]:o[4]], w[..., o[4]:o[5]]
    c_q, c_kv, c_kr = w[..., o[5]:o[6]], w[..., o[6]:o[7]], w[..., o[7]:o[8]]
    dq, dk, dvg = w[..., o[8]:o[9]], w[..., o[9]:o[10]] * RET_DK ** -0.5, w[..., o[10]:]
    z = lambda n: jnp.zeros(w.shape[:-1] + (n,), w.dtype)
    parts = [
        _place_swa_q(aq), ak, av,
        b_main, b_ab, z(LANES - b_ab.shape[-1]),
        c_q, c_kv, z(64), c_kr, z(32),
        dq, dk, dvg,
        _place_swa_q(_rot_cols(aq, HEAD_DIM)), _rot_cols(ak, HEAD_DIM),
        z(64), _rot_cols(c_kr, MLA_ROPE), z(32),
        _rot_cols(dq, HEAD_DIM), _rot_cols(dk, HEAD_DIM),
    ]
    out = jnp.concatenate(parts, axis=-1)
    assert out.shape[-1] == W_ALL
    return out.astype(BF16)


def rope_tables(n):
    lat = jnp.arange(CTX_LEN + n, dtype=jnp.int32) - CTX_LEN
    grid_row = jnp.where(lat >= 0, lat // GRID_W, 0).astype(F32)[:, None]
    grid_col = jnp.where(lat >= 0, lat % GRID_W, 0).astype(F32)[:, None]
    narrow, col_of = [], {}
    for rot_dim in (HEAD_DIM, MLA_ROPE):
        n_freq = rot_dim // 4
        inv_freq = ROPE_THETA ** (-jnp.arange(n_freq, dtype=F32) / n_freq)
        for axis, pos in enumerate((grid_row, grid_col)):
            col_of[rot_dim, axis] = sum(a.shape[1] for a in narrow)
            narrow.append(pos * inv_freq)
    ang = jnp.concatenate(narrow, axis=1)
    identity_col = ang.shape[1]
    sel = np.zeros((identity_col + 1, ROPE_W), np.float32)

    def plan(lane0, width, group, rot_lo, rot_dim):
        for c in range(width):
            j = c % group - rot_lo
            if 0 <= j < rot_dim:
                quarter, f = divmod(j, rot_dim // 4)
                sel[col_of[rot_dim, quarter // 2] + f, lane0 + c] = 1.0
            else:
                sel[identity_col, lane0 + c] = 1.0

    plan(0, A_ROT_W, HEAD_DIM, 0, HEAD_DIM)
    plan(A_ROT_W, C_ROT_W, C_ROT_W, 64, MLA_ROPE)
    plan(A_ROT_W + C_ROT_W, D_ROT_W, HEAD_DIM, 0, HEAD_DIM)
    spread = lambda t: jnp.dot(t, jnp.asarray(sel), precision=lax.Precision.HIGHEST)
    ones, zeros = jnp.ones_like(grid_row), jnp.zeros_like(grid_row)
    return (spread(jnp.concatenate([jnp.cos(ang), ones], axis=1)),
            spread(jnp.concatenate([jnp.sin(ang), zeros], axis=1)))


def _swa_kernel(sink_ref, q_ref, kp_ref, ko_ref, kn_ref, kc_ref, vp_ref, vo_ref, vn_ref, vc_ref, o_ref):
    t = pl.program_id(1)
    last = pl.num_programs(1) - 1
    tq = q_ref.shape[0]
    half = tq // 2

    def attend(k, v, mask):
        v_lane = lax.broadcasted_iota(jnp.int32, v.shape, 1)
        v_ones = [jnp.where(v_lane >= HEAD_DIM, jnp.ones_like(v), v), jnp.where(v_lane < HEAD_DIM, jnp.ones_like(v), v)]
        heads = range(4)
        s = [_dot_nt(q_ref[:, LANES * h:LANES * (h + 1)], k) for h in heads]
        if mask is not None:
            s = [jnp.where(mask, x, NEG_INF) for x in s]
        sink = [sink_ref[h] * LOG2_E for h in heads]
        m = [jnp.maximum(s[h].max(axis=-1, keepdims=True), sink[h]) for h in heads]
        p = [jnp.exp2(s[h] - m[h]).astype(BF16) for h in heads]
        o = [_dot(p[h], v_ones[h // 2]) for h in heads]
        den_lane = [HEAD_DIM if h // 2 == 0 else 0 for h in heads]
        outs = [o[h] / (o[h][:, den_lane[h]:den_lane[h] + 1] + jnp.exp2(sink[h] - m[h])) for h in heads]
        lane = lax.broadcasted_iota(jnp.int32, (tq, LANES), 1)
        for r in range(2):
            o_ref[:, LANES * r:LANES * (r + 1)] = jnp.where(lane < HEAD_DIM, outs[r], outs[2 + r]).astype(o_ref.dtype)

    @pl.when(t == 0)
    def _():
        attend(kc_ref[...], vc_ref[...], None)

    @pl.when(t > 0)
    def _():
        band = 2 * half + tq
        qi = lax.broadcasted_iota(jnp.int32, (tq, band + CTX_LEN), 0)
        col = lax.broadcasted_iota(jnp.int32, (tq, band + CTX_LEN), 1)
        in_window = jnp.abs(col - half - qi) <= SWA_WINDOW
        exists = ((col >= half) | (t > 1)) & ((col < half + tq) | (t < last))
        mask = (col >= band) | (in_window & exists)
        attend(jnp.concatenate([kp_ref[...], ko_ref[...], kn_ref[...], kc_ref[...]], axis=0),
               jnp.concatenate([vp_ref[...], vo_ref[...], vn_ref[...], vc_ref[...]], axis=0), mask)


def swa_mixer(pa, sink):
    b, l, _ = pa.shape
    tq = ROW_TILE
    nblk = l // SWA_WINDOW
    kcol, vcol = 4, 5
    prev = lambda c: (lambda i, t: (i, jnp.maximum(2 * t - 1, 2), c))
    nxt = lambda c: (lambda i, t: (i, jnp.minimum(2 * t + 2, nblk - 1), c))
    own = lambda c: (lambda i, t: (i, t, c))
    ctx = lambda c: (lambda i, t: (i, 0, c))
    kv_specs = lambda c: [pl.BlockSpec((None, SWA_WINDOW, LANES), prev(c)), pl.BlockSpec((None, tq, LANES), own(c)),
                          pl.BlockSpec((None, SWA_WINDOW, LANES), nxt(c)), pl.BlockSpec((None, tq, LANES), ctx(c))]
    return pl.pallas_call(
        _swa_kernel,
        grid=(b, l // tq),
        in_specs=[pl.BlockSpec(memory_space=pltpu.SMEM),
                  pl.BlockSpec((None, tq, 4 * LANES), lambda i, t: (i, t, 0))] + kv_specs(kcol) + kv_specs(vcol),
        out_specs=pl.BlockSpec((None, tq, 2 * LANES), lambda i, t: (i, t, 0)),
        out_shape=jax.ShapeDtypeStruct((b, l, 2 * LANES), BF16),
        compiler_params=pltpu.CompilerParams(
            dimension_semantics=("parallel", "parallel"), vmem_limit_bytes=VMEM_LIMIT),
        name="swa",
    )(sink, pa, pa, pa, pa, pa, pa, pa, pa, pa)


def build_mla_weights(q_norm, kv_norm, w_q_up, w_kv_up):
    scale = (MLA_NOPE + MLA_ROPE) ** -0.5 * LOG2_E
    wq = (w_q_up * scale).reshape(-1, MLA_HEADS, MLA_NOPE + MLA_ROPE)
    zq = jnp.zeros(wq.shape[:2] + (LANES - MLA_NOPE - MLA_ROPE,), F32)
    wq_main = jnp.concatenate([wq, zq], axis=-1).reshape(-1, MLA_HEADS * LANES)
    wq_rot = jnp.concatenate([jnp.zeros_like(wq[..., :MLA_NOPE]), _rot_cols(wq[..., MLA_NOPE:], MLA_ROPE), zq],
                             axis=-1).reshape(-1, MLA_HEADS * LANES)
    wkv = w_kv_up.reshape(-1, MLA_HEADS, MLA_NOPE + MLA_V)
    wk = jnp.concatenate([wkv[..., :MLA_NOPE], jnp.zeros_like(wkv[..., :LANES - MLA_NOPE])],
                         axis=-1).reshape(-1, MLA_HEADS * LANES)
    wv = wkv[..., MLA_NOPE:].reshape(-1, MLA_HEADS * MLA_V)
    return [q_norm[None, :], kv_norm[None, :], wq_main.astype(BF16), wq_rot.astype(BF16), wk.astype(BF16),
            wv.astype(BF16)]


def _mla_attn_kernel(q_ref, k_ref, v_ref, o_ref):
    t = pl.program_id(2)
    tq = q_ref.shape[0]

    def attend(nk):
        v = v_ref[0:nk, :]
        v_lane = lax.broadcasted_iota(jnp.int32, v.shape, 1)
        v_ones = [jnp.where(v_lane >= MLA_V, jnp.ones_like(v), v), jnp.where(v_lane < MLA_V, jnp.ones_like(v), v)]
        pair = range(2)
        s = [_dot_nt(q_ref[:, LANES * j:LANES * (j + 1)], k_ref[0:nk, LANES * j:LANES * (j + 1)]) for j in pair]
        p = [jnp.exp2(s[j] - s[j].max(axis=-1, keepdims=True)).astype(BF16) for j in pair]
        o = [_dot(p[j], v_ones[j]) for j in pair]
        outs = [o[j] / o[j][:, (MLA_V, 0)[j]:(MLA_V, 0)[j] + 1] for j in pair]
        lane = lax.broadcasted_iota(jnp.int32, (tq, LANES), 1)
        o_ref[...] = jnp.where(lane < MLA_V, outs[0], outs[1]).astype(o_ref.dtype)

    @pl.when(t == 0)
    def _():
        attend(CTX_LEN)

    @pl.when(t > 0)
    def _():
        attend(k_ref.shape[0])


def mla_attention(q, k, v):
    b, l, _ = q.shape
    tq = ROW_TILE
    return pl.pallas_call(
        _mla_attn_kernel,
        grid=(b, 2, l // tq),
        in_specs=[pl.BlockSpec((None, tq, 2 * LANES), lambda i, p, t: (i, t, p)),
                  pl.BlockSpec((None, l, 2 * LANES), lambda i, p, t: (i, 0, p)),
                  pl.BlockSpec((None, l, LANES), lambda i, p, t: (i, 0, p))],
        out_specs=pl.BlockSpec((None, tq, LANES), lambda i, p, t: (i, t, p)),
        out_shape=jax.ShapeDtypeStruct((b, l, 2 * LANES), BF16),
        compiler_params=pltpu.CompilerParams(
            dimension_semantics=("parallel", "parallel", "parallel"), vmem_limit_bytes=VMEM_LIMIT),
        name="mla_attn",
    )(q, k, v)


def _head_mean(x, ones_bd):
    hi = x.astype(BF16)
    lo = (x - hi.astype(F32)).astype(BF16)
    return (_dot(hi, ones_bd) + _dot(lo, ones_bd)) * (1.0 / HEAD_DIM)


def _ret_kernel(x_ref, lg_ref, g_ref, o_ref, s_ref, dec_ref, part_ref):
    dr = pl.program_id(0)
    s = pl.program_id(1)
    ns = pl.num_programs(1)
    nb, c = x_ref.shape[0], x_ref.shape[1]
    w = RET_HEADS * HEAD_DIM
    chunk = jnp.where(s == 0, 0, jnp.where(dr == 0, s, ns - s))
    lg = lg_ref[...]
    fwd = dr == 0
    row_h = lax.broadcasted_iota(jnp.int32, (w, w), 0) // HEAD_DIM
    col_h = lax.broadcasted_iota(jnp.int32, (w, w), 1) // HEAD_DIM
    same_head = row_h == col_h

    @pl.when(s == 0)
    def _():
        s_ref[...] = jnp.zeros_like(s_ref)
        i = lax.broadcasted_iota(jnp.int32, (c, c), 0)
        j = lax.broadcasted_iota(jnp.int32, (c, c), 1)
        rel = jnp.where(fwd, i - j, j - i)
        relf = jnp.maximum(rel, 0).astype(F32)
        for h in range(RET_HEADS):
            lg_h = lg_ref[0:1, HEAD_DIM * h:HEAD_DIM * h + 1]
            dec_ref[h] = jnp.where(rel >= 0, jnp.exp(lg_h * relf), 0.0)

    pos = lax.broadcasted_iota(jnp.int32, (c, 1), 0).astype(F32)
    q_dec = jnp.exp(lg * jnp.where(fwd, pos + 1.0, c - pos))
    k_dec = jnp.exp(lg * jnp.where(fwd, c - 1.0 - pos, pos))
    lane_h = lax.broadcasted_iota(jnp.int32, (c, w), 1) // HEAD_DIM
    bs = range(nb)
    q = [x_ref[i, :, 0:w] for i in bs]
    kf = [x_ref[i, :, w:2 * w] for i in bs]
    v = [x_ref[i, :, 2 * w:3 * w].astype(BF16) for i in bs]
    kb = [x.astype(BF16) for x in kf]
    acc = [_dot((q[i] * q_dec).astype(BF16), s_ref[i].astype(BF16)) for i in bs]
    for h in range(RET_HEADS):
        qh = [jnp.where(lane_h == h, q[i], 0.0).astype(BF16) for i in bs]
        a = [(_dot_nt(qh[i], kb[i]) * dec_ref[h]).astype(BF16) for i in bs]
        acc = [acc[i] + jnp.where(lane_h == h, _dot(a[i], v[i]), 0.0) for i in bs]
    kv = [_dot_tn((kf[i] * k_dec).astype(BF16), v[i]) for i in bs]
    chunk_dec = jnp.exp(lg * float(c))
    for i in bs:
        s_ref[i] = s_ref[i] * chunk_dec + jnp.where(same_head, kv[i], 0.0)

    rows = pl.ds(pl.multiple_of(chunk * c, c), c)

    @pl.when(dr == 0)
    def _():
        for i in bs:
            part_ref[i, rows, :] = acc[i]

    @pl.when(dr == 1)
    def _():
        ones_bd = jnp.where(same_head, 1.0, 0.0).astype(BF16)
        for i in bs:
            o = part_ref[i, rows, :] + acc[i]
            mu = _head_mean(o, ones_bd)
            var = _head_mean(jnp.square(o - mu), ones_bd)
            y = (o - mu) * lax.rsqrt(var + EPS) * g_ref[...]
            o_ref[i] = (y * _silu(x_ref[i, :, 3 * w:4 * w])).astype(o_ref.dtype)


def retention_mixer(pd, log_decay, norm_g):
    b, l, _ = pd.shape
    c = ROW_TILE
    ns = l // c
    w = RET_HEADS * HEAD_DIM
    lg = jnp.repeat(-jnp.exp(log_decay.astype(F32)), HEAD_DIM, axis=-1)[:, None, :]

    def chunk_of(dr, s):
        return jnp.where(s == 0, 0, jnp.where(dr == 0, s, ns - s))

    return pl.pallas_call(
        _ret_kernel,
        grid=(2, ns),
        in_specs=[pl.BlockSpec((b, c, D_W), lambda dr, s: (0, chunk_of(dr, s), 0)),
                  pl.BlockSpec((None, 1, w), lambda dr, s: (dr, 0, 0)),
                  pl.BlockSpec((1, w), lambda dr, s: (0, 0))],
        out_specs=pl.BlockSpec((b, c, w), lambda dr, s: (0, jnp.where(dr == 0, 0, chunk_of(dr, s)), 0)),
        out_shape=jax.ShapeDtypeStruct((b, l, w), BF16),
        scratch_shapes=[pltpu.VMEM((b, w, w), F32), pltpu.VMEM((RET_HEADS, c, c), F32), pltpu.VMEM((b, l, w), F32)],
        compiler_params=pltpu.CompilerParams(
            dimension_semantics=("arbitrary", "arbitrary"), vmem_limit_bytes=VMEM_LIMIT),
        name="retention",
    )(pd, lg, norm_g[None, :])


def _out_proj_kernel(ma_ref, mb_ref, mc_ref, md_ref, h_ref, *refs, with_router):
    mods, (g_ref, w_ref), rest = refs[:WIDE], refs[WIDE:WIDE + 2], refs[WIDE + 2:]
    if with_router:
        wr_ref, hn_ref, v_ref, lg_ref = rest
    else:
        hn_ref, v_ref = rest
    gw = 2 * LANES
    mix = functools.reduce(jnp.add, [
        _dot(m_ref[...].astype(BF16), w_ref[gw * i:gw * (i + 1), :])
        for i, m_ref in enumerate((ma_ref, mb_ref, mc_ref, md_ref))])
    for k, mod_ref in enumerate(mods):
        r = _sub_rows(k)
        hn = h_ref[r, :] + mod_ref[2:3, :] * mix[r]
        hn_ref[r, :] = hn
        v = _rms(hn) * g_ref[...] * (1.0 + mod_ref[4:5, :]) + mod_ref[3:4, :]
        v_ref[r, :] = v.astype(v_ref.dtype)
        if with_router:
            lg_ref[r, :] = jnp.dot(v, wr_ref[...], preferred_element_type=F32, precision=lax.Precision.HIGHEST)


def out_proj(mixes, h, mod, gain, w, layer, w_router=None):
    b, l, d = h.shape
    tm = WIDE * ROW_TILE
    n = b * l
    with_router = w_router is not None
    row = lambda j: (j, 0)
    const = lambda j: (0, 0)
    in_specs = [pl.BlockSpec((tm, 2 * LANES), row) for _ in mixes] + [pl.BlockSpec((tm, d), row)] + _sub_mod_specs(
        d, l // ROW_TILE) + [pl.BlockSpec((1, d), const), _layer_weight_spec(w, layer)]
    out_specs = [pl.BlockSpec((tm, d), row), pl.BlockSpec((tm, d), row)]
    out_shape = [jax.ShapeDtypeStruct((n, d), F32), jax.ShapeDtypeStruct((n, d), F32 if with_router else BF16)]
    args = [m.reshape(n, 2 * LANES) for m in mixes] + [h.reshape(n, d)] + [mod] * WIDE + [gain, w]
    if with_router:
        in_specs.append(pl.BlockSpec(w_router.shape, const))
        out_specs.append(pl.BlockSpec((tm, LANES), row))
        out_shape.append(jax.ShapeDtypeStruct((n, LANES), F32))
        args.append(w_router)
    outs = pl.pallas_call(
        functools.partial(_out_proj_kernel, with_router=with_router),
        grid=(n // tm,),
        in_specs=in_specs,
        out_specs=out_specs,
        out_shape=out_shape,
        compiler_params=pltpu.CompilerParams(dimension_semantics=("parallel",), vmem_limit_bytes=VMEM_LIMIT),
        name="out_proj",
    )(*args)
    return [o.reshape(b, l, -1) for o in outs]


def build_out_weight(w):
    hd = HEAD_DIM
    rows = lambda lo, hi: w[..., lo:hi, :]
    return jnp.concatenate([rows(0, hd), rows(2 * hd, 3 * hd), rows(hd, 2 * hd), rows(3 * hd, None)],
                           axis=-2).astype(BF16)


def _ffn_kernel(v_ref, h_ref, *refs):
    mods, (wg_ref, wu_ref, wd_ref, o_ref) = refs[:WIDE], refs[WIDE:]
    v = v_ref[...]
    acc = jnp.zeros(o_ref.shape, F32)
    for j in range(D_FF // FF_CHUNK):
        cols = slice(j * FF_CHUNK, (j + 1) * FF_CHUNK)
        a = _dot(v, wg_ref[:, cols])
        u = _dot(v, wu_ref[:, cols])
        mid = (_silu(a) * u).astype(BF16)
        acc = acc + _dot(mid, wd_ref[cols, :])
    for k, mod_ref in enumerate(mods):
        r = _sub_rows(k)
        o_ref[r, :] = h_ref[r, :] + mod_ref[5:6, :] * acc[r]


def dense_ffn(v, h, mod, layer, wg, wu, wd):
    b, l, d = h.shape
    tm = WIDE * ROW_TILE
    n = b * l
    row = lambda j: (j, 0)
    const = lambda j: (0, 0)
    return pl.pallas_call(
        _ffn_kernel,
        grid=(n // tm,),
        in_specs=[pl.BlockSpec((tm, d), row), pl.BlockSpec((tm, d), row)] + _sub_mod_specs(d, l // ROW_TILE) + [
            _layer_weight_spec(wg, layer), _layer_weight_spec(wu, layer), _layer_weight_spec(wd, layer)],
        out_specs=pl.BlockSpec((tm, d), row),
        out_shape=jax.ShapeDtypeStruct((n, d), F32),
        compiler_params=pltpu.CompilerParams(dimension_semantics=("parallel",), vmem_limit_bytes=VMEM_LIMIT),
        name="dense_ffn",
    )(v.reshape(n, d), h.reshape(n, d), *([mod] * WIDE), wg, wu, wd).reshape(b, l, d)


def _moe_kernel(wt_ref, we_ref, lo_ref, hi_ref, first_ref, x_ref, wg_ref, wu_ref, wd_ref, o_ref, xm_ref, acc_ref):
    w = pl.program_id(0)
    j = pl.program_id(1)
    nj = pl.num_programs(1)
    tm = x_ref.shape[0]

    @pl.when(j == 0)
    def _():
        row = wt_ref[w] * tm + lax.broadcasted_iota(jnp.int32, (tm, 1), 0)
        keep = (row >= lo_ref[w]) & (row < hi_ref[w])
        xm_ref[...] = jnp.where(keep, x_ref[...], 0.0).astype(BF16)

    @pl.when((j == 0) & (first_ref[w] > 0))
    def _():
        acc_ref[...] = jnp.zeros_like(acc_ref)

    def swiglu_rows(rows):
        x = xm_ref[rows, :]
        mid = (_silu(_dot(x, wg_ref[...])) * _dot(x, wu_ref[...])).astype(BF16)
        acc_ref[rows, :] += _dot(mid, wd_ref[...])

    tile_lo = wt_ref[w] * tm
    whole = (lo_ref[w] <= tile_lo) & (hi_ref[w] >= tile_lo + tm)

    @pl.when(whole)
    def _():
        swiglu_rows(slice(0, tm))

    for part in range(tm // MOE_PART):
        part_lo = tile_lo + part * MOE_PART

        @pl.when(jnp.logical_not(whole) & (hi_ref[w] > jnp.maximum(part_lo, lo_ref[w]))
                 & (lo_ref[w] < part_lo + MOE_PART))
        def _():
            swiglu_rows(slice(part * MOE_PART, (part + 1) * MOE_PART))

    @pl.when(j == nj - 1)
    def _():
        o_ref[...] = acc_ref[...]


def moe_grouped_ffn(xs, items, layer_idx, wg, wu, wd):
    s, d = xs.shape
    tm = MOE_TILE
    fc = MOE_FF_CHUNK
    nw = items[0].shape[0]
    nj = D_FF // fc
    grid_spec = pltpu.PrefetchScalarGridSpec(
        num_scalar_prefetch=5,
        grid=(nw, nj),
        in_specs=[
            pl.BlockSpec((tm, d), lambda w, j, wt, we, lo, hi, fi: (wt[w], 0)),
            pl.BlockSpec((None, None, d, fc), lambda w, j, wt, we, lo, hi, fi: (layer_idx, we[w], 0, j)),
            pl.BlockSpec((None, None, d, fc), lambda w, j, wt, we, lo, hi, fi: (layer_idx, we[w], 0, j)),
            pl.BlockSpec((None, None, fc, d), lambda w, j, wt, we, lo, hi, fi: (layer_idx, we[w], j, 0)),
        ],
        out_specs=pl.BlockSpec((tm, d), lambda w, j, wt, we, lo, hi, fi: (wt[w], 0)),
        scratch_shapes=[pltpu.VMEM((tm, d), BF16), pltpu.VMEM((tm, d), F32)],
    )
    return pl.pallas_call(
        _moe_kernel,
        grid_spec=grid_spec,
        out_shape=jax.ShapeDtypeStruct((s, d), F32),
        compiler_params=pltpu.CompilerParams(
            dimension_semantics=("arbitrary", "arbitrary"), vmem_limit_bytes=VMEM_LIMIT),
        name="moe_ffn",
    )(*items, xs, wg, wu, wd)


def _residual_kernel(h_ref, f0_ref, f1_ref, gate_ref, mod_ref, *rest):
    f = gate_ref[:, 0:1] * f0_ref[...] + gate_ref[:, 1:2] * f1_ref[...]
    hn = h_ref[...] + mod_ref[5:6, :] * f
    if len(rest) == 2:
        gain_ref, o_ref = rest
        o_ref[...] = _rms(hn) * gain_ref[...]
    else:
        rest[0][...] = hn


def gated_residual(h, f0, f1, gates, mod, final_gain=None):
    b, l, d = h.shape
    tm = ROW_TILE
    skip = 0 if final_gain is None else CTX_LEN // tm
    row = lambda i, t: (i, t + skip, 0)
    in_specs = [pl.BlockSpec((None, tm, d), row), pl.BlockSpec((None, tm, d), row), pl.BlockSpec((None, tm, d), row),
                pl.BlockSpec((None, tm, LANES), row),
                pl.BlockSpec((None, None, SUBLANES, d), lambda i, t: (i, jnp.minimum(t + skip, 1), 0, 0))]
    args = [h, f0, f1, gates, mod]
    if final_gain is not None:
        in_specs.append(pl.BlockSpec((1, d), lambda i, t: (0, 0)))
        args.append(final_gain)
    return pl.pallas_call(
        _residual_kernel,
        grid=(b, l // tm - skip),
        in_specs=in_specs,
        out_specs=pl.BlockSpec((None, tm, d), lambda i, t: (i, t, 0)),
        out_shape=jax.ShapeDtypeStruct((b, l - skip * tm, d), F32),
        compiler_params=pltpu.CompilerParams(dimension_semantics=("parallel", "parallel")),
        name="gated_residual",
    )(*args)


def moe_ffn(v, logits, h, mod, layer_idx, wg, wu, wd, final_gain=None):
    b, l, d = h.shape
    t = b * l
    s = TOP_K * t
    tm = MOE_TILE
    nt = s // tm
    nw = nt + N_EXPERTS - 1
    i32 = jnp.int32
    lg = logits.reshape(t, LANES)[:, :N_EXPERTS]
    top_val, top_idx = lax.top_k(lg, TOP_K)
    gates = jax.nn.softmax(top_val, axis=-1)
    slot = jnp.arange(s, dtype=i32)
    skey = jnp.sort(top_idx.reshape(-1).astype(i32) * s + slot)
    order = skey % s
    _, inv = lax.sort_key_val(order, slot)
    bounds = (jnp.arange(N_EXPERTS, dtype=i32) + 1) * s
    cum = jnp.sum((skey[None, :] < bounds[:, None]).astype(i32), axis=1)
    cum_prev = jnp.concatenate([jnp.zeros((1,), i32), cum[:-1]])
    tile_lo = jnp.arange(nt, dtype=i32) * tm
    count_le = lambda edges, x: jnp.sum((edges[None, :] <= x[:, None]).astype(i32), axis=1)
    e_first = count_le(cum, tile_lo)
    e_last = count_le(cum, tile_lo + tm - 1)
    n_items = e_last - e_first + 1
    item_end = jnp.cumsum(n_items)
    item_start = item_end - n_items
    w = jnp.arange(nw, dtype=i32)
    wt = jnp.minimum(count_le(item_end, w), nt - 1)
    valid = w < item_end[-1]
    we = jnp.clip(e_first[wt] + w - item_start[wt], 0, N_EXPERTS - 1).astype(i32)
    lo = jnp.where(valid, cum_prev[we], 0).astype(i32)
    hi = jnp.where(valid, cum[we], 0).astype(i32)
    first = (valid & (w == item_start[wt])).astype(i32)
    rows_of = lambda a, idx: a.at[idx].get(mode="promise_in_bounds")
    xs = rows_of(v.reshape(t, d), order // TOP_K)
    ys = moe_grouped_ffn(xs, (wt, we, lo, hi, first), layer_idx, wg, wu, wd)
    dest = inv.reshape(t, TOP_K)
    f0 = rows_of(ys, dest[:, 0]).reshape(b, l, d)
    f1 = rows_of(ys, dest[:, 1]).reshape(b, l, d)
    gates_p = jnp.pad(gates, ((0, 0), (0, LANES - TOP_K))).reshape(b, l, LANES)
    return gated_residual(h, f0, f1, gates_p, mod, final_gain)


def _final_norm_kernel(h_ref, g_ref, o_ref):
    o_ref[...] = _rms(h_ref[...]) * g_ref[...]


def final_rms_norm(h, gain, n_ctx_tiles):
    b, l, d = h.shape
    tm = ROW_TILE
    n = l - n_ctx_tiles * tm
    return pl.pallas_call(
        _final_norm_kernel,
        grid=(b, n // tm),
        in_specs=[
            pl.BlockSpec((None, tm, d), lambda i, t: (i, t + n_ctx_tiles, 0)),
            pl.BlockSpec((1, d), lambda i, t: (0, 0)),
        ],
        out_specs=pl.BlockSpec((None, tm, d), lambda i, t: (i, t, 0)),
        out_shape=jax.ShapeDtypeStruct((b, n, d), F32),
        compiler_params=pltpu.CompilerParams(dimension_semantics=("parallel", "parallel")),
        name="final_norm",
    )(h, gain)


GDN_W = GDN_HEADS * GDN_DK
GDN_CONV_K = 5
GDN_HALO = SUBLANES


def _split3(x):
    p0 = x.astype(BF16)
    r1 = x - p0.astype(F32)
    p1 = r1.astype(BF16)
    p2 = (r1 - p1.astype(F32)).astype(BF16)
    return p0, p1, p2


def _gdn_prep_kernel(x_ref, prev_ref, next_ref, cw_ref, par_ref, q_ref, k_ref, v_ref, gb_ref):
    t = pl.program_id(1)
    last = pl.num_programs(1) - 1
    tm = x_ref.shape[0]
    w3 = 3 * GDN_W
    has_prev = t > 1
    has_next = (t > 0) & (t < last)
    prev = jnp.where(has_prev, prev_ref[...], 0.0)
    nxt = jnp.where(has_next, next_ref[...], 0.0)
    xe = jnp.concatenate([prev, x_ref[:, :w3], nxt], axis=0)
    y = jnp.zeros((tm, w3), F32)
    for j in range(GDN_CONV_K):
        lo = GDN_HALO - GDN_CONV_K // 2 + j
        y = y + cw_ref[j:j + 1, :] * xe[lo:lo + tm, :]
    y = _silu(y)
    r = lax.broadcasted_iota(jnp.int32, (GDN_W, GDN_W), 0)
    c = lax.broadcasted_iota(jnp.int32, (GDN_W, GDN_W), 1)
    ones_bd = jnp.where(r // GDN_DK == c // GDN_DK, 1.0, 0.0).astype(BF16)

    def l2n(x):
        sq = x * x
        hi = sq.astype(BF16)
        lo = (sq - hi.astype(F32)).astype(BF16)
        return x * lax.rsqrt(_dot(hi, ones_bd) + _dot(lo, ones_bd) + EPS)

    q_ref[...] = l2n(y[:, :GDN_W]) * GDN_DK ** -0.5
    k_ref[...] = l2n(y[:, GDN_W:2 * GDN_W])
    v_ref[...] = y[:, 2 * GDN_W:]
    ab = x_ref[:, w3 + GDN_W:]
    lane = lax.broadcasted_iota(jnp.int32, ab.shape, 1)
    is_g = (lane % 8) < 4
    z = ab + par_ref[1:2, :]
    softplus = jnp.maximum(z, 0.0) + jnp.log1p(jnp.exp(-jnp.abs(z)))
    g = jnp.where(is_g, par_ref[0:1, :] * softplus, 0.0)
    beta = 1.0 / (1.0 + jnp.exp(-ab))
    i = lax.broadcasted_iota(jnp.int32, (tm, tm), 0)
    j = lax.broadcasted_iota(jnp.int32, (tm, tm), 1)
    same_chunk = i // GDN_CHUNK == j // GDN_CHUNK
    tri_f = jnp.where(same_chunk & (j <= i), 1.0, 0.0).astype(BF16)
    tri_b = jnp.where(same_chunk & (j >= i), 1.0, 0.0).astype(BF16)
    pieces = _split3(g)
    gc_f = functools.reduce(jnp.add, [_dot(tri_f, p) for p in pieces])
    gc_b = functools.reduce(jnp.add, [_dot(tri_b, p) for p in pieces])
    gb_ref[...] = jnp.where(is_g, jnp.where(lane < 8, gc_f, gc_b), beta)


def gdn_prep(pb, conv_w, a_log, dt_bias):
    b, l, _ = pb.shape
    tm = ROW_TILE
    w3 = 3 * GDN_W
    halo_blocks = tm // GDN_HALO
    n_halo = l // GDN_HALO
    cw = jnp.pad(conv_w, ((0, SUBLANES - GDN_CONV_K), (0, 0)))
    neg_a = jnp.pad(-jnp.exp(a_log.astype(F32)), ((0, 0), (0, 4))).reshape(-1)
    dtb = jnp.pad(dt_bias.astype(F32), ((0, 0), (0, 4))).reshape(-1)
    par = jnp.pad(jnp.stack([neg_a, dtb]), ((0, SUBLANES - 2), (0, LANES - 16)))
    row = lambda i, t: (i, t, 0)
    out = lambda w: pl.BlockSpec((None, tm, w), row)
    return pl.pallas_call(
        _gdn_prep_kernel,
        grid=(b, l // tm),
        in_specs=[pl.BlockSpec((None, tm, B_W), row),
                  pl.BlockSpec((None, GDN_HALO, w3), lambda i, t: (i, jnp.maximum(t * halo_blocks - 1, 0), 0)),
                  pl.BlockSpec((None, GDN_HALO, w3),
                               lambda i, t: (i, jnp.minimum((t + 1) * halo_blocks, n_halo - 1), 0)),
                  pl.BlockSpec(cw.shape, lambda i, t: (0, 0)),
                  pl.BlockSpec(par.shape, lambda i, t: (0, 0))],
        out_specs=[out(GDN_W), out(GDN_W), out(GDN_W), out(LANES)],
        out_shape=[jax.ShapeDtypeStruct((b, l, GDN_W), F32)] * 3 + [jax.ShapeDtypeStruct((b, l, LANES), F32)],
        compiler_params=pltpu.CompilerParams(
            dimension_semantics=("parallel", "parallel"), vmem_limit_bytes=VMEM_LIMIT),
        name="gdn_prep",
    )(pb, pb, pb, cw, par)


def _tile_heads(x):
    return jnp.concatenate([x] * GDN_HEADS, axis=0)


def _collapse_heads(x):
    c = GDN_CHUNK
    return x[0:c] + x[c:2 * c] + x[2 * c:3 * c] + x[3 * c:4 * c]


def _gdn_chunk_kernel(q_ref, k_ref, v_ref, gb_ref, o0_ref, qe_ref, a_ref, bm_ref, gam_ref):
    n = GDN_W
    cs = GDN_CHUNK
    r = lax.broadcasted_iota(jnp.int32, (n, n), 0)
    c = lax.broadcasted_iota(jnp.int32, (n, n), 1)
    ri, ci = r % cs, c % cs
    head = r // cs == c // cs
    eye = jnp.where(r == c, 1.0, 0.0)
    blk = lambda s: r // s == c // s
    b8, b16, b32 = blk(8), blk(16), blk(32)
    lane = lax.broadcasted_iota(jnp.int32, (n, LANES), 1)
    row_head = lax.broadcasted_iota(jnp.int32, (n, LANES), 0) // cs
    pick = lambda sel, x: jnp.sum(jnp.where(sel, x, 0.0), axis=1, keepdims=True)
    src_lane = lax.broadcasted_iota(jnp.int32, (LANES, n), 0)
    dst_head = lax.broadcasted_iota(jnp.int32, (LANES, n), 1) // cs
    tri, tri_strict, sel_g, sel_b, widen_g, widen_b = {}, {}, {}, {}, {}, {}
    for fwd in (True, False):
        ahead = ri - ci if fwd else ci - ri
        tri[fwd] = head & (ahead >= 0)
        tri_strict[fwd] = head & (ahead > 0)
        lane0 = 0 if fwd else 8
        sel_g[fwd] = lane == lane0 + row_head
        sel_b[fwd] = lane == lane0 + 4 + row_head
        widen_g[fwd] = jnp.where(src_lane == lane0 + dst_head, 1.0, 0.0).astype(BF16)
        widen_b[fwd] = jnp.where(src_lane == lane0 + 4 + dst_head, 1.0, 0.0).astype(BF16)

    n_chunks = q_ref.shape[0] // cs
    rows = [slice(ch * cs, (ch + 1) * cs) for ch in range(n_chunks)]
    items = [(fwd, ch) for fwd in (True, False) for ch in range(n_chunks)]
    dirs = [fwd for fwd, _ in items]
    per_item = lambda xs: [xs[ch] for _, ch in items]
    each = lambda f, *xs: [f(*a) for a in zip(*xs)]
    bf = lambda xs: [x.astype(BF16) for x in xs]
    widen = lambda x, e: functools.reduce(jnp.add, [_dot(p, e) for p in _split3(x)])
    zero_bf = jnp.zeros((n, n), BF16)
    spread = lambda xs: [jnp.where(head, _tile_heads(x.astype(BF16)), zero_bf) for x in xs]
    k_c, q_c, v_c = ([ref[rw, :] for rw in rows] for ref in (k_ref, q_ref, v_ref))
    khb_c, qhb_c = spread(k_c), spread(q_c)
    kk = per_item(each(_dot_nt, khb_c, khb_c))
    qk = per_item(each(_dot_nt, qhb_c, khb_c))
    k_t, q_t, v_t = per_item(k_c), per_item(q_c), per_item(v_c)
    gb = per_item([gb_ref[rw, :] for rw in rows])
    gb4 = [_tile_heads(x) for x in gb]
    gc = [pick(sel_g[f], x) for f, x in zip(dirs, gb4)]
    beta = [pick(sel_b[f], x) for f, x in zip(dirs, gb4)]
    gc_t = [widen(x, widen_g[f]) for f, x in zip(dirs, gb)]
    beta_t = [widen(x, widen_b[f]) for f, x in zip(dirs, gb)]
    ends = [cs - 1 if f else 0 for f in dirs]
    gl_t = [jnp.broadcast_to(x[e:e + 1, :], x.shape) for x, e in zip(gc_t, ends)]
    gc_b = [jnp.broadcast_to(x, (n, n)) for x in gc]
    decay = [jnp.exp(jnp.minimum(x - x.T, 0.0)) for x in gc_b]
    lmat = [jnp.where(tri_strict[f], b_ * kk_ * d_, 0.0) for f, b_, kk_, d_ in zip(dirs, beta, kk, decay)]
    attn = bf([jnp.where(tri[f], qk_ * d_, 0.0) for f, qk_, d_ in zip(dirs, qk, decay)])
    nl = bf([jnp.where(b8, -x, 0.0) for x in lmat])
    n2 = bf(each(_dot, nl, nl))
    n4 = each(_dot, n2, n2)
    p1 = bf(each(lambda a, b_: _dot((eye + a).astype(BF16), (eye + b_).astype(BF16)), nl, n2))
    tinv = each(lambda p, x: _dot(p, (eye + x).astype(BF16)), p1, n4)

    def moving_rows(fwd, x, sz):
        return jnp.concatenate([x[i:i + sz] for i in range(sz if fwd else 0, n, 2 * sz)], axis=0)

    def with_moving_rows(fwd, x, new, sz):
        pieces = []
        for j, i in enumerate(range(0, n, 2 * sz)):
            kept = x[i:i + sz] if fwd else x[i + sz:i + 2 * sz]
            moved = new[j * sz:(j + 1) * sz]
            pieces += [kept, moved] if fwd else [moved, kept]
        return jnp.concatenate(pieces, axis=0)

    for sz, inner, outer in ((8, b8, b16), (16, b16, b32), (32, b32, head)):
        off = bf([jnp.where(outer & ~inner, x, 0.0) for x in lmat])
        tb = bf(tinv)
        t_mv = [moving_rows(f, x, sz) for f, x in zip(dirs, tinv)]
        to = bf(each(_dot, bf(t_mv), off))
        tinv = [with_moving_rows(f, t_, tm_ - _dot(to_, tb_), sz)
                for f, t_, tm_, to_, tb_ in zip(dirs, tinv, t_mv, to, tb)]
    tb = bf(tinv)
    eg_t = [jnp.exp(x) for x in gc_t]
    u = bf(each(_dot, tb, spread(each(lambda b_, v_: b_ * v_, beta_t, v_t))))
    w = bf(each(_dot, tb, spread(each(lambda b_, e_, k_: (b_ * e_) * k_, beta_t, eg_t, k_t))))
    o0 = each(_dot, attn, u)
    aw = each(_dot, attn, w)
    kg = spread(each(lambda k_, gl_, gc_: k_ * jnp.exp(gl_ - gc_), k_t, gl_t, gc_t))
    a_mat = each(_dot_tn, kg, w)
    b_mat = each(_dot_tn, kg, u)
    for (fwd, ch), o0_, aw_, a_, b_, q_, e_, gl_ in zip(items, o0, aw, a_mat, b_mat, q_t, eg_t, gl_t):
        dr, rw = 0 if fwd else 1, rows[ch]
        o0_ref[dr, rw, :] = _collapse_heads(o0_)
        qe_ref[dr, rw, :] = (q_ * e_ - _collapse_heads(aw_)).astype(qe_ref.dtype)
        a_ref[dr, rw, :] = _collapse_heads(a_).astype(a_ref.dtype)
        bm_ref[dr, rw, :] = _collapse_heads(b_)
        gam_ref[dr, ch * SUBLANES:(ch + 1) * SUBLANES, :] = jnp.exp(gl_[0:SUBLANES, :])


def gdn_chunks(q, k, v, gb):
    b, l, _ = q.shape
    tm = ROW_TILE
    row = lambda i, t: (i, t, 0)
    out = pl.BlockSpec((2, None, tm, GDN_W), lambda i, t: (0, i, t, 0))
    gam_rows = SUBLANES * tm // GDN_CHUNK
    out_gam = pl.BlockSpec((2, None, gam_rows, GDN_W), lambda i, t: (0, i, t, 0))
    full = lambda dt: jax.ShapeDtypeStruct((2, b, l, GDN_W), dt)
    return pl.pallas_call(
        _gdn_chunk_kernel,
        grid=(b, l // tm),
        in_specs=[pl.BlockSpec((None, tm, GDN_W), row)] * 3 + [pl.BlockSpec((None, tm, LANES), row)],
        out_specs=[out] * 4 + [out_gam],
        out_shape=[full(F32), full(BF16), full(BF16), full(F32),
                   jax.ShapeDtypeStruct((2, b, SUBLANES * l // GDN_CHUNK, GDN_W), F32)],
        compiler_params=pltpu.CompilerParams(
            dimension_semantics=("parallel", "parallel"), vmem_limit_bytes=VMEM_LIMIT),
        name="gdn_chunk",
    )(q, k, v, gb)


def _gdn_scan_kernel(o0_ref, qe_ref, a_ref, bm_ref, gam_ref, gate_ref, g_ref, o_ref, s_ref, part_ref):
    dr = pl.program_id(0)
    s = pl.program_id(1)
    ns = pl.num_programs(1)
    nb, tm = o0_ref.shape[0], o0_ref.shape[1]
    cs = GDN_CHUNK
    n = GDN_W
    nch = tm // cs
    bs = range(nb)
    tile = jnp.where(s == 0, 0, jnp.where(dr == 0, s, ns - s))
    r = lax.broadcasted_iota(jnp.int32, (n, n), 0)
    c = lax.broadcasted_iota(jnp.int32, (n, n), 1)
    head = r // cs == c // cs

    @pl.when(s == 0)
    def _():
        s_ref[...] = jnp.zeros_like(s_ref)

    def run(order):
        state = [s_ref[i] for i in bs]
        outs = [{} for _ in bs]
        for ch in order:
            rows = slice(ch * cs, (ch + 1) * cs)
            sb = [x.astype(BF16) for x in state]
            for i in bs:
                outs[i][ch] = o0_ref[i, rows, :] + _dot(qe_ref[i, rows, :].astype(BF16), sb[i])
            a_full = [jnp.where(head, _tile_heads(a_ref[i, rows, :]), jnp.zeros((n, n), BF16)) for i in bs]
            gam = [gam_ref[i, ch * SUBLANES:ch * SUBLANES + 1, :] for i in bs]
            state = [gam[i] * state[i] - _dot(a_full[i], sb[i])
                     + jnp.where(head, _tile_heads(bm_ref[i, rows, :]), 0.0) for i in bs]
        for i in bs:
            s_ref[i] = state[i]
        return [jnp.concatenate([outs[i][ch] for ch in range(nch)], axis=0) for i in bs]

    rows_out = pl.ds(pl.multiple_of(tile * tm, tm), tm)

    @pl.when(dr == 0)
    def _():
        for i, o in enumerate(run(range(nch))):
            part_ref[i, rows_out, :] = o

    @pl.when(dr == 1)
    def _():
        ones_bd = jnp.where(head, 1.0, 0.0).astype(BF16)
        for i, o_bwd in enumerate(run(range(nch - 1, -1, -1))):
            o = part_ref[i, rows_out, :] + o_bwd
            ms = _head_mean(o * o, ones_bd)
            o_ref[i] = (o * lax.rsqrt(ms + EPS) * g_ref[...] * _silu(gate_ref[i])).astype(o_ref.dtype)


def gdn_scan(o0, qe, a, bm, gam, pb, norm_g):
    _, b, l, _ = o0.shape
    tm = ROW_TILE
    ns = l // tm

    def tile_of(dr, s):
        return jnp.where(s == 0, 0, jnp.where(dr == 0, s, ns - s))

    per_dir = pl.BlockSpec((None, b, tm, GDN_W), lambda dr, s: (dr, 0, tile_of(dr, s), 0))
    gate_col = 3 * GDN_W // GDN_W
    return pl.pallas_call(
        _gdn_scan_kernel,
        grid=(2, ns),
        in_specs=[per_dir] * 4 + [
            pl.BlockSpec((None, b, SUBLANES * tm // GDN_CHUNK, GDN_W), lambda dr, s: (dr, 0, tile_of(dr, s), 0)),
            pl.BlockSpec((b, tm, GDN_W), lambda dr, s: (0, tile_of(dr, s), gate_col)),
            pl.BlockSpec((1, GDN_W), lambda dr, s: (0, 0))],
        out_specs=pl.BlockSpec((b, tm, GDN_W), lambda dr, s: (0, jnp.where(dr == 0, 0, tile_of(dr, s)), 0)),
        out_shape=jax.ShapeDtypeStruct((b, l, GDN_W), BF16),
        scratch_shapes=[pltpu.VMEM((b, GDN_W, GDN_W), F32), pltpu.VMEM((b, l, GDN_W), F32)],
        compiler_params=pltpu.CompilerParams(
            dimension_semantics=("arbitrary", "arbitrary"), vmem_limit_bytes=VMEM_LIMIT),
        name="gdn_scan",
    )(o0, qe, a, bm, gam, pb, jnp.tile(norm_g, GDN_HEADS)[None, :])


def gdn_mixer(pb, conv_w, a_log, dt_bias, norm_g):
    q, k, v, gb = gdn_prep(pb, conv_w, a_log, dt_bias)
    o0, qe, a, bm, gam = gdn_chunks(q, k, v, gb)
    return gdn_scan(o0, qe, a, bm, gam, pb, norm_g)


def kernel(x, c, ctx, c_ctx, w_mod, b_mod, norm1, norm2, w_in, w_out, swa_sink, gdn_conv, gdn_a_log, gdn_dt_bias, gdn_norm, mla_q_norm, mla_kv_norm, mla_w_q_up, mla_w_kv_up, ret_log_decay, ret_norm, ffn_w_gate, ffn_w_up, ffn_w_down, moe_router, moe_w_gate, moe_w_up, moe_w_down, final_norm):
    b, n, d = x.shape
    depth = w_in.shape[0]
    cos_t, sin_t = rope_tables(n)
    h = jnp.concatenate([ctx, x], axis=1)
    cond = jnp.concatenate([jax.nn.silu(c_ctx)[None, :], jax.nn.silu(c)], axis=0)
    mods = jnp.einsum("bd,ldk->lbk", cond, w_mod, precision=lax.Precision.HIGHEST) + b_mod[:, None, :]
    mods = mods.reshape(depth, 1 + b, 6, d)
    mods = jnp.stack([jnp.broadcast_to(mods[:, :1], (depth, b, 6, d)), mods[:, 1:]], axis=2)
    mods = jnp.pad(mods, ((0, 0), (0, 0), (0, 0), (0, SUBLANES - 6), (0, 0)))
    w_in_all = build_in_weight(w_in)
    w_out_all = build_out_weight(w_out)
    ffn_wg, ffn_wu, ffn_wd = ffn_w_gate.astype(BF16), ffn_w_up.astype(BF16), ffn_w_down.astype(BF16)
    moe_wg, moe_wu, moe_wd = moe_w_gate.astype(BF16), moe_w_up.astype(BF16), moe_w_down.astype(BF16)
    for layer in range(depth):
        mod = mods[layer]
        mla_w = build_mla_weights(mla_q_norm[layer], mla_kv_norm[layer], mla_w_q_up[layer], mla_w_kv_up[layer])
        pa, pb, (mq, mk, mv), pd = norm_proj(h, mod, norm1[layer][None, :], w_in_all, layer, cos_t, sin_t, mla_w)
        mix_a = swa_mixer(pa, swa_sink[layer])
        mix_b = gdn_mixer(pb, gdn_conv[layer], gdn_a_log[layer], gdn_dt_bias[layer], gdn_norm[layer])
        mix_c = mla_attention(mq, mk, mv)
        mix_d = retention_mixer(pd, ret_log_decay[layer], ret_norm[layer])
        mixes = (mix_a, mix_b, mix_c, mix_d)
        i = layer // 2
        if layer % 2 == 0:
            h, v = out_proj(mixes, h, mod, norm2[layer][None, :], w_out_all, layer)
            h = dense_ffn(v, h, mod, i, ffn_wg, ffn_wu, ffn_wd)
        else:
            w_r = jnp.pad(moe_router[i], ((0, 0), (0, LANES - N_EXPERTS)))
            h, v, logits = out_proj(mixes, h, mod, norm2[layer][None, :], w_out_all, layer, w_r)
            if layer == depth - 1:
                return moe_ffn(v, logits, h, mod, i, moe_wg, moe_wu, moe_wd, final_norm[None, :])
            h = moe_ffn(v, logits, h, mod, i, moe_wg, moe_wu, moe_wd)
    return final_rms_norm(h, final_norm[None, :], CTX_LEN // ROW_TILE)
```
